```python
import math
import jax, jax.numpy as jnp
from jax import lax
import numpy as np

D_MODEL = 2048
BATCH = 4
SEQ = 2048
DEPTH = 2

N_MIXERS = 2
NORM_EPS = 1e-6
NEG_INF = -1e30
ROPE_THETA = 10000.0
NSA_HEADS = 16
NSA_KV_HEADS = 4
NSA_HEAD_DIM = D_MODEL // NSA_HEADS
NSA_Q_PER_KV = NSA_HEADS // NSA_KV_HEADS
CMP_BLOCK = 32
CMP_STRIDE = 16
CMP_HIDDEN = 256
SLC_BLOCK = 64
SLC_TOPK = 16
SLC_LOCAL = 2
SLC_FORCE = 1e4
WINDOW = 512
WIN_Q_BLOCK = 128
SLC_Q_CHUNK = 32
NSA_QD = NSA_HEADS * NSA_HEAD_DIM
NSA_KVD = NSA_KV_HEADS * NSA_HEAD_DIM
NSA_PROJ = NSA_QD + 6 * NSA_KVD + 3 * NSA_HEADS
SSD_D_INNER = 2 * D_MODEL
SSD_HEAD_DIM = 64
SSD_HEADS = SSD_D_INNER // SSD_HEAD_DIM
SSD_GROUPS = 8
SSD_HEADS_PER_GROUP = SSD_HEADS // SSD_GROUPS
SSD_D_STATE = 128
SSD_CONV = 4
SSD_CHUNK = 128
SSD_CONV_CH = SSD_D_INNER + 2 * SSD_GROUPS * SSD_D_STATE
SSD_PROJ = SSD_D_INNER + SSD_CONV_CH + SSD_HEADS
MOE_GROUPS = 4
MOE_EXPERTS_PER_GROUP = 8
MOE_EXPERTS = MOE_GROUPS * MOE_EXPERTS_PER_GROUP
MOE_TOPK = 2
MOE_D_FF = 512

kernel_name = 'hybrid_nsa_ssd_hier_moe'


def rms_norm(x, g):
    xf = x.astype(jnp.float32)
    y = xf * lax.rsqrt(jnp.mean(xf * xf, axis=-1, keepdims=True) + NORM_EPS)
    return (y * g.astype(jnp.float32)).astype(x.dtype)


def rope_tables(seq_len):
    pos = jnp.arange(seq_len, dtype=jnp.float32)
    inv = 1.0 / (ROPE_THETA ** (jnp.arange(0, NSA_HEAD_DIM, 2, dtype=jnp.float32) / NSA_HEAD_DIM))
    ang = pos[:, None] * inv[None, :]
    return jnp.cos(ang), jnp.sin(ang)


def apply_rope(x, cos, sin):
    half = x.shape[-1] // 2
    x1, x2 = x[..., :half], x[..., half:]
    c = cos[None, :, None, :].astype(x.dtype)
    s = sin[None, :, None, :].astype(x.dtype)
    return jnp.concatenate([x1 * c - x2 * s, x2 * c + x1 * s], axis=-1)


def compress_blocks(kv, pe, w1, w2):
    B, S, G, Dh = kv.shape
    n_cmp = (S - CMP_BLOCK) // CMP_STRIDE + 1
    idx = jnp.arange(n_cmp)[:, None] * CMP_STRIDE + jnp.arange(CMP_BLOCK)[None, :]
    blk = kv[:, idx] + pe[None, None, :, None, :]
    blk = jnp.swapaxes(blk, 2, 3).reshape(B, n_cmp, G, CMP_BLOCK * Dh)
    return jax.nn.gelu(blk @ w1) @ w2


def nsa_mixer(h, w_in, cmp_pe, cmp_w1, cmp_w2, w_out, cos, sin):
    B, S, _ = h.shape
    G, R, Dh = NSA_KV_HEADS, NSA_Q_PER_KV, NSA_HEAD_DIM
    scale = Dh ** -0.5
    proj = h @ w_in
    cuts = [NSA_QD + i * NSA_KVD for i in range(7)]
    q, k_c, v_c, k_s, v_s, k_w, v_w, g_lin = jnp.split(proj, cuts, axis=-1)
    q = apply_rope(q.reshape(B, S, NSA_HEADS, Dh), cos, sin).reshape(B, S, G, R, Dh)
    kvs = (B, S, G, Dh)
    k_c = apply_rope(k_c.reshape(kvs), cos, sin)
    k_s = apply_rope(k_s.reshape(kvs), cos, sin)
    k_w = apply_rope(k_w.reshape(kvs), cos, sin)
    v_c, v_s, v_w = v_c.reshape(kvs), v_s.reshape(kvs), v_w.reshape(kvs)
    t = jnp.arange(S)

    n_cmp = (S - CMP_BLOCK) // CMP_STRIDE + 1
    kc = compress_blocks(k_c, cmp_pe[0], cmp_w1[0], cmp_w2[0])
    vc = compress_blocks(v_c, cmp_pe[1], cmp_w1[1], cmp_w2[1])
    s_cmp = jnp.einsum('bsgrd,bcgd->bgrsc', q, kc).astype(jnp.float32) * scale
    cmp_ok = (jnp.arange(n_cmp) * CMP_STRIDE + CMP_BLOCK - 1)[None, :] <= t[:, None]
    p_cmp = jax.nn.softmax(jnp.where(cmp_ok, s_cmp, NEG_INF), axis=-1) * cmp_ok
    o_cmp = jnp.einsum('bgrsc,bcgd->bsgrd', p_cmp.astype(vc.dtype), vc)

    n_slc = S // SLC_BLOCK
    cmp_start = jnp.arange(n_cmp) * CMP_STRIDE
    slc_start = jnp.arange(n_slc) * SLC_BLOCK
    overlap = jnp.clip(jnp.minimum(cmp_start[:, None] + CMP_BLOCK, slc_start[None, :] + SLC_BLOCK)
                       - jnp.maximum(cmp_start[:, None], slc_start[None, :]), 0, None).astype(jnp.float32) / CMP_BLOCK
    imp = jnp.einsum('bgrsc,cj->bgsj', p_cmp, overlap)
    dist = (t // SLC_BLOCK)[:, None] - jnp.arange(n_slc)[None, :]
    forced = (jnp.arange(n_slc)[None, :] == 0) | ((dist >= 0) & (dist < SLC_LOCAL))
    imp = jnp.where(forced, SLC_FORCE, imp)
    imp = jnp.where(dist >= 0, imp, -jnp.inf)
    _, sel = lax.top_k(imp, min(SLC_TOPK, n_slc))
    n_sel = sel.shape[-1]
    ks_flat = k_s.reshape(B, n_slc, SLC_BLOCK, G, Dh).transpose(0, 3, 1, 2, 4).reshape(B * G * n_slc, SLC_BLOCK, Dh)
    vs_flat = v_s.reshape(B, n_slc, SLC_BLOCK, G, Dh).transpose(0, 3, 1, 2, 4).reshape(B * G * n_slc, SLC_BLOCK, Dh)
    base = (jnp.arange(B)[:, None, None, None] * G + jnp.arange(G)[None, :, None, None]) * n_slc
    n_chunk = S // SLC_Q_CHUNK
    q_ch = jnp.moveaxis(q.reshape(B, n_chunk, SLC_Q_CHUNK, G, R, Dh), 1, 0)
    sel_ch = jnp.moveaxis(sel.reshape(B, G, n_chunk, SLC_Q_CHUNK, n_sel), 2, 0)
    t_ch = t.reshape(n_chunk, SLC_Q_CHUNK)
    tok = jnp.arange(SLC_BLOCK)

    def select_chunk(args):
        qc, sc, tc = args
        kg = ks_flat[base + sc]
        vg = vs_flat[base + sc]
        s = jnp.einsum('bqgrd,bgqkld->bgrqkl', qc, kg).astype(jnp.float32) * scale
        ok = (sc[..., None] * SLC_BLOCK + tok) <= tc[None, None, :, None, None]
        s = jnp.where(ok[:, :, None], s, NEG_INF)
        p = jax.nn.softmax(s.reshape(s.shape[:4] + (-1,)), axis=-1).reshape(s.shape)
        return jnp.einsum('bgrqkl,bgqkld->bqgrd', p.astype(vg.dtype), vg)

    o_slc = jnp.moveaxis(lax.map(select_chunk, (q_ch, sel_ch, t_ch)), 0, 1).reshape(B, S, G, R, Dh)

    nqb = S // WIN_Q_BLOCK
    nwb = WINDOW // WIN_Q_BLOCK
    kw_len = (nwb + 1) * WIN_Q_BLOCK

    def band(kv):
        kp = jnp.pad(kv, ((0, 0), (WINDOW, 0), (0, 0), (0, 0))).reshape(B, nqb + nwb, WIN_Q_BLOCK, G, Dh)
        return jnp.concatenate([kp[:, i:i + nqb] for i in range(nwb + 1)], axis=2)

    kwin, vwin = band(k_w), band(v_w)
    qb = q.reshape(B, nqb, WIN_Q_BLOCK, G, R, Dh)
    s_w = jnp.einsum('bnqgrd,bnkgd->bngrqk', qb, kwin).astype(jnp.float32) * scale
    qpos = jnp.arange(nqb)[:, None] * WIN_Q_BLOCK + jnp.arange(WIN_Q_BLOCK)[None, :]
    kpos = jnp.arange(nqb)[:, None] * WIN_Q_BLOCK - WINDOW + jnp.arange(kw_len)[None, :]
    kq = kpos[:, None, :]
    qq = qpos[:, :, None]
    win_ok = (kq <= qq) & (kq > qq - WINDOW) & (kq >= 0)
    p_w = jax.nn.softmax(jnp.where(win_ok[None, :, None, None], s_w, NEG_INF), axis=-1)
    o_win = jnp.einsum('bngrqk,bnkgd->bnqgrd', p_w.astype(vwin.dtype), vwin).reshape(B, S, G, R, Dh)

    gate = jax.nn.sigmoid(g_lin.astype(jnp.float32)).astype(h.dtype).reshape(B, S, G, R, 3)
    o = gate[..., 0:1] * o_cmp + gate[..., 1:2] * o_slc + gate[..., 2:3] * o_win
    return o.reshape(B, S, NSA_QD) @ w_out


def ssd_chunked_scan(xdt, a_dt, b_mat, c_mat):
    B, S, H, P = xdt.shape
    Gs, R, N, L = SSD_GROUPS, SSD_HEADS_PER_GROUP, SSD_D_STATE, SSD_CHUNK
    nc = S // L
    x = xdt.reshape(B, nc, L, Gs, R, P)
    a = a_dt.reshape(B, nc, L, Gs, R).transpose(0, 3, 4, 1, 2)
    bc = b_mat.reshape(B, nc, L, Gs, N)
    cc = c_mat.reshape(B, nc, L, Gs, N)
    a_cum = jnp.cumsum(a, axis=-1)
    seg = a_cum[..., :, None] - a_cum[..., None, :]
    causal = jnp.tril(jnp.ones((L, L), dtype=bool))
    decay = jnp.where(causal, jnp.exp(jnp.where(causal, seg, 0.0)), 0.0)
    cb = jnp.einsum('bclgn,bcsgn->bgcls', cc, bc)
    y_diag = jnp.einsum('bgcls,bgrcls,bcsgrp->bclgrp', cb, decay, x)
    state_decay = jnp.exp(a_cum[..., -1:] - a_cum)
    states = jnp.einsum('bclgn,bgrcl,bclgrp->bcgrpn', bc, state_decay, x)
    chunk_decay = jnp.exp(a_cum[..., -1])

    def step(carry, inp):
        st, dec = inp
        return carry * dec[..., None, None] + st, carry

    init = jnp.zeros((B, Gs, R, P, N), dtype=jnp.float32)
    _, prev = lax.scan(step, init, (jnp.moveaxis(states, 1, 0), jnp.moveaxis(chunk_decay, -1, 0)))
    prev = jnp.moveaxis(prev, 0, 1)
    y_off = jnp.einsum('bclgn,bcgrpn,bgrcl->bclgrp', cc, prev, jnp.exp(a_cum))
    return (y_diag + y_off).reshape(B, S, H, P)


def ssd_mixer(h, w_in, conv_w, conv_b, dt_bias, a_log, d_skip, norm_g, w_out):
    B, S, _ = h.shape
    f32 = jnp.float32
    proj = h @ w_in
    z, xbc, dt = jnp.split(proj, [SSD_D_INNER, SSD_D_INNER + SSD_CONV_CH], axis=-1)
    xpad = jnp.pad(xbc, ((0, 0), (SSD_CONV - 1, 0), (0, 0)))
    conv = conv_b
    for k in range(SSD_CONV):
        conv = conv + xpad[:, k:k + S] * conv_w[k]
    xbc = jax.nn.silu(conv)
    xs, b_mat, c_mat = jnp.split(xbc, [SSD_D_INNER, SSD_D_INNER + SSD_GROUPS * SSD_D_STATE], axis=-1)
    xs = xs.reshape(B, S, SSD_HEADS, SSD_HEAD_DIM).astype(f32)
    b_mat = b_mat.reshape(B, S, SSD_GROUPS, SSD_D_STATE).astype(f32)
    c_mat = c_mat.reshape(B, S, SSD_GROUPS, SSD_D_STATE).astype(f32)
    dt = jax.nn.softplus(dt.astype(f32) + dt_bias.astype(f32))
    a = -jnp.exp(a_log.astype(f32))
    y = ssd_chunked_scan(xs * dt[..., None], dt * a, b_mat, c_mat)
    y = y + xs * d_skip.astype(f32)[:, None]
    y = y.reshape(B, S, SSD_D_INNER) * jax.nn.silu(z.astype(f32))
    yg = y.reshape(B, S, SSD_GROUPS, SSD_D_INNER // SSD_GROUPS)
    yg = yg * lax.rsqrt(jnp.mean(yg * yg, axis=-1, keepdims=True) + NORM_EPS)
    y = (yg.reshape(B, S, SSD_D_INNER) * norm_g.astype(f32)).astype(h.dtype)
    return y @ w_out


def hier_moe(h, w_group, b_group, w_expert, b_expert, w_gate, w_up, w_down):
    B, S, D = h.shape
    T = B * S
    f32 = jnp.float32
    hf = h.reshape(T, D)
    g_logits = (hf @ w_group + b_group).astype(f32)
    g_prob = jax.nn.softmax(g_logits, axis=-1)
    g_sel = jnp.argmax(g_logits, axis=-1)
    g_w = jnp.take_along_axis(g_prob, g_sel[:, None], axis=1)
    e_logits = (hf @ w_expert + b_expert).astype(f32).reshape(T, MOE_GROUPS, MOE_EXPERTS_PER_GROUP)
    e_in = jnp.take_along_axis(e_logits, g_sel[:, None, None], axis=1)[:, 0]
    top_v, top_i = lax.top_k(e_in, MOE_TOPK)
    w = jax.nn.softmax(top_v, axis=-1) * g_w
    eid = g_sel[:, None] * MOE_EXPERTS_PER_GROUP + top_i
    combine = jnp.einsum('tk,tke->te', w, jax.nn.one_hot(eid, MOE_EXPERTS, dtype=f32)).astype(h.dtype)
    y = jnp.zeros((T, D), dtype=h.dtype)
    for grp in range(MOE_GROUPS):
        sl = slice(grp * MOE_EXPERTS_PER_GROUP, (grp + 1) * MOE_EXPERTS_PER_GROUP)
        act = jax.nn.silu(jnp.einsum('td,edf->tef', hf, w_gate[sl])) * jnp.einsum('td,edf->tef', hf, w_up[sl])
        y = y + jnp.einsum('tef,te,efd->td', act, combine[:, sl], w_down[sl])
    return y.reshape(B, S, D)


def setup_inputs(seed: int = 0) -> dict:
    key = jax.random.key(seed)
    ks = jax.random.split(key, 24)
    f32 = jnp.float32
    n_nsa = (DEPTH + N_MIXERS - 1) // N_MIXERS
    n_ssd = DEPTH // N_MIXERS
    Dh = NSA_HEAD_DIM

    def nrm(k, shape, s):
        return jax.random.normal(k, shape, f32) * s

    dt0 = jnp.exp(jax.random.uniform(ks[12], (n_ssd, SSD_HEADS), f32, math.log(1e-3), math.log(1e-1)))
    return {
        'x': nrm(ks[0], (BATCH, SEQ, D_MODEL), 1.0),
        'ln_mix': 1.0 + nrm(ks[1], (DEPTH, D_MODEL), 0.02),
        'ln_ffn': 1.0 + nrm(ks[2], (DEPTH, D_MODEL), 0.02),
        'ln_final': 1.0 + nrm(ks[3], (D_MODEL,), 0.02),
        'nsa_w_in': nrm(ks[4], (n_nsa, D_MODEL, NSA_PROJ), D_MODEL ** -0.5),
        'nsa_cmp_pe': nrm(ks[5], (n_nsa, 2, CMP_BLOCK, Dh), 0.1),
        'nsa_cmp_w1': nrm(ks[6], (n_nsa, 2, CMP_BLOCK * Dh, CMP_HIDDEN), (CMP_BLOCK * Dh) ** -0.5),
        'nsa_cmp_w2': nrm(ks[7], (n_nsa, 2, CMP_HIDDEN, Dh), CMP_HIDDEN ** -0.5),
        'nsa_w_out': nrm(ks[8], (n_nsa, NSA_QD, D_MODEL), NSA_QD ** -0.5),
        'ssd_w_in': nrm(ks[9], (n_ssd, D_MODEL, SSD_PROJ), D_MODEL ** -0.5),
        'ssd_conv_w': nrm(ks[10], (n_ssd, SSD_CONV, SSD_CONV_CH), SSD_CONV ** -0.5),
        'ssd_conv_b': nrm(ks[11], (n_ssd, SSD_CONV_CH), 0.01),
        'ssd_dt_bias': dt0 + jnp.log(-jnp.expm1(-dt0)),
        'ssd_a_log': jnp.log(jax.random.uniform(ks[13], (n_ssd, SSD_HEADS), f32, 1.0, 16.0)),
        'ssd_d': 1.0 + nrm(ks[14], (n_ssd, SSD_HEADS), 0.1),
        'ssd_norm': 1.0 + nrm(ks[15], (n_ssd, SSD_D_INNER), 0.02),
        'ssd_w_out': nrm(ks[16], (n_ssd, SSD_D_INNER, D_MODEL), SSD_D_INNER ** -0.5),
        'moe_w_group': nrm(ks[17], (DEPTH, D_MODEL, MOE_GROUPS), D_MODEL ** -0.5),
        'moe_b_group': nrm(ks[18], (DEPTH, MOE_GROUPS), 0.01),
        'moe_w_expert': nrm(ks[19], (DEPTH, D_MODEL, MOE_EXPERTS), D_MODEL ** -0.5),
        'moe_b_expert': nrm(ks[20], (DEPTH, MOE_EXPERTS), 0.01),
        'moe_w_gate': nrm(ks[21], (DEPTH, MOE_EXPERTS, D_MODEL, MOE_D_FF), D_MODEL ** -0.5),
        'moe_w_up': nrm(ks[22], (DEPTH, MOE_EXPERTS, D_MODEL, MOE_D_FF), D_MODEL ** -0.5),
        'moe_w_down': nrm(ks[23], (DEPTH, MOE_EXPERTS, MOE_D_FF, D_MODEL), MOE_D_FF ** -0.5),
    }


def reference(x, ln_mix, ln_ffn, ln_final, nsa_w_in, nsa_cmp_pe, nsa_cmp_w1, nsa_cmp_w2, nsa_w_out,
              ssd_w_in, ssd_conv_w, ssd_conv_b, ssd_dt_bias, ssd_a_log, ssd_d, ssd_norm, ssd_w_out,
              moe_w_group, moe_b_group, moe_w_expert, moe_b_expert, moe_w_gate, moe_w_up, moe_w_down):
    cos, sin = rope_tables(x.shape[1])
    for i in range(DEPTH):
        j = i // N_MIXERS
        h = rms_norm(x, ln_mix[i])
        if i % N_MIXERS == 0:
            mix = nsa_mixer(h, nsa_w_in[j], nsa_cmp_pe[j], nsa_cmp_w1[j], nsa_cmp_w2[j], nsa_w_out[j], cos, sin)
        else:
            mix = ssd_mixer(h, ssd_w_in[j], ssd_conv_w[j], ssd_conv_b[j], ssd_dt_bias[j], ssd_a_log[j],
                            ssd_d[j], ssd_norm[j], ssd_w_out[j])
        x = x + mix
        x = x + hier_moe(rms_norm(x, ln_ffn[i]), moe_w_group[i], moe_b_group[i], moe_w_expert[i],
                         moe_b_expert[i], moe_w_gate[i], moe_w_up[i], moe_w_down[i])
    return rms_norm(x, ln_final)
```

```python
import functools

import jax
import jax.numpy as jnp
from jax import lax
from jax.experimental import pallas as pl
from jax.experimental.pallas import tpu as pltpu

F32 = jnp.float32
BF16 = jnp.bfloat16
I32 = jnp.int32

D_MODEL = 2048
BATCH = 4
SEQ = 2048
TOKENS = BATCH * SEQ
DEPTH = 2
N_MIXERS = 2
NORM_EPS = 1e-6
NEG_INF = -1e30
ROPE_THETA = 10000.0

NSA_HEADS = 16
NSA_KV_HEADS = 4
NSA_HEAD_DIM = D_MODEL // NSA_HEADS
NSA_Q_PER_KV = NSA_HEADS // NSA_KV_HEADS
CMP_BLOCK = 32
CMP_STRIDE = 16
CMP_HIDDEN = 256
N_CMP = (SEQ - CMP_BLOCK) // CMP_STRIDE + 1
SLC_BLOCK = 64
SLC_TOPK = 16
SLC_LOCAL = 2
SLC_FORCE = 1e4
N_SLC = SEQ // SLC_BLOCK
WINDOW = 512
NSA_QD = NSA_HEADS * NSA_HEAD_DIM
NSA_KVD = NSA_KV_HEADS * NSA_HEAD_DIM
NSA_MAIN = NSA_QD + 6 * NSA_KVD
NSA_GATES = 3 * NSA_HEADS

SSD_D_INNER = 2 * D_MODEL
SSD_HEAD_DIM = 64
SSD_HEADS = SSD_D_INNER // SSD_HEAD_DIM
SSD_GROUPS = 8
SSD_HEADS_PER_GROUP = SSD_HEADS // SSD_GROUPS
SSD_D_STATE = 128
SSD_CONV = 4
SSD_CHUNK = 128
SSD_GROUP_W = SSD_D_INNER // SSD_GROUPS
SSD_BC = SSD_GROUPS * SSD_D_STATE
SSD_CONV_CH = SSD_D_INNER + 2 * SSD_BC
SSD_MAIN = SSD_D_INNER + SSD_CONV_CH

MOE_GROUPS = 4
MOE_EPG = 8
MOE_EXPERTS = MOE_GROUPS * MOE_EPG
MOE_TOPK = 2
MOE_D_FF = 512

LANES = 128
VMEM_LIMIT = 56 * 1024 * 1024

NORM_TM = 256
MM_TM = 512
ATT_TQ = 128
MOE_TM = 256
MOE_TILES = (TOKENS * MOE_TOPK) // MOE_TM + MOE_EXPERTS
MOE_ROWS = MOE_TILES * MOE_TM
CMB_TM = 128
GATHER_UNROLL = 8


def _cparams(sem):
    return pltpu.CompilerParams(dimension_semantics=sem, vmem_limit_bytes=VMEM_LIMIT)


def _split3(x):
    hi = x.astype(BF16)
    r1 = x - hi.astype(F32)
    mid = r1.astype(BF16)
    lo = (r1 - mid.astype(F32)).astype(BF16)
    return hi, mid, lo


def _dot(a, b):
    return jnp.dot(a, b, preferred_element_type=F32)


def _dot_nt(a, b):
    return lax.dot_general(a, b, (((1,), (1,)), ((), ())), preferred_element_type=F32)


def _dot_tn(a, b):
    return lax.dot_general(a, b, (((0,), (0,)), ((), ())), preferred_element_type=F32)


def _dot_split_lhs(x, m_bf16):
    hi, mid, lo = _split3(x)
    return _dot(hi, m_bf16) + _dot(mid, m_bf16) + _dot(lo, m_bf16)


def _dot_split_rhs(m_bf16, x):
    hi, mid, lo = _split3(x)
    return _dot(m_bf16, hi) + _dot(m_bf16, mid) + _dot(m_bf16, lo)


def _dot_x3(a, w):
    a_hi = a.astype(BF16)
    a_lo = (a - a_hi.astype(F32)).astype(BF16)
    w_hi = w.astype(BF16)
    w_lo = (w - w_hi.astype(F32)).astype(BF16)
    return _dot(a_hi, w_hi) + _dot(a_hi, w_lo) + _dot(a_lo, w_hi)


def _dot_x3_nt(a, w):
    a_hi = a.astype(BF16)
    a_lo = (a - a_hi.astype(F32)).astype(BF16)
    w_hi = w.astype(BF16)
    w_lo = (w - w_hi.astype(F32)).astype(BF16)
    return _dot_nt(a_hi, w_hi) + _dot_nt(a_hi, w_lo) + _dot_nt(a_lo, w_hi)


def _rms(x, g):
    y = x * lax.rsqrt(jnp.mean(x * x, axis=-1, keepdims=True) + NORM_EPS)
    return y * g


def _norm_small_kernel(x_ref, g_ref, ws_ref, hn_ref, small_ref):
    y = _rms(x_ref[...], g_ref[...])
    hn_ref[...] = y.astype(BF16)
    small_ref[...] = _dot_x3(y, ws_ref[...])


def _norm_small_t_kernel(x_ref, g_ref, ws_ref, wst_ref, hn_ref, small_ref, small_t_ref):
    y = _rms(x_ref[...], g_ref[...])
    hn_ref[...] = y.astype(BF16)
    small_ref[...] = _dot_x3(y, ws_ref[...])
    small_t_ref[...] = _dot_x3_nt(wst_ref[...], y)


def _norm_small(x, g, w_small, transposed=False):
    n = w_small.shape[1]
    ws = jnp.zeros((D_MODEL, LANES), F32).at[:, :n].set(w_small)
    grid = (TOKENS // NORM_TM,)
    x_spec = pl.BlockSpec((NORM_TM, D_MODEL), lambda i: (i, 0))
    g_spec = pl.BlockSpec((1, D_MODEL), lambda i: (0, 0))
    w_spec = pl.BlockSpec((D_MODEL, LANES), lambda i: (0, 0))
    hn_spec = pl.BlockSpec((NORM_TM, D_MODEL), lambda i: (i, 0))
    sm_spec = pl.BlockSpec((NORM_TM, LANES), lambda i: (i, 0))
    hn_shape = jax.ShapeDtypeStruct((TOKENS, D_MODEL), BF16)
    sm_shape = jax.ShapeDtypeStruct((TOKENS, LANES), F32)
    if not transposed:
        return pl.pallas_call(
            _norm_small_kernel, grid=grid,
            in_specs=[x_spec, g_spec, w_spec],
            out_specs=[hn_spec, sm_spec],
            out_shape=[hn_shape, sm_shape],
            compiler_params=_cparams(("parallel",)),
        )(x, g.reshape(1, D_MODEL), ws)
    wt_spec = pl.BlockSpec((LANES, D_MODEL), lambda i: (0, 0))
    smt_spec = pl.BlockSpec((LANES, NORM_TM), lambda i: (0, i))
    smt_shape = jax.ShapeDtypeStruct((LANES, TOKENS), F32)
    return pl.pallas_call(
        _norm_small_t_kernel, grid=grid,
        in_specs=[x_spec, g_spec, w_spec, wt_spec],
        out_specs=[hn_spec, sm_spec, smt_spec],
        out_shape=[hn_shape, sm_shape, smt_shape],
        compiler_params=_cparams(("parallel",)),
    )(x, g.reshape(1, D_MODEL), ws, ws.T)


def _router_kernel(x_ref, g_ref, ws_ref, b_ref, eid_ref, cw_ref):
    y = _rms(x_ref[...], g_ref[...])
    logits = _dot_x3(y, ws_ref[...]) + b_ref[...]
    lane = lax.broadcasted_iota(I32, logits.shape, 1)
    big = jnp.int32(LANES)
    neg = -jnp.inf
    gl = jnp.where(lane < MOE_GROUPS, logits, neg)
    gmax = jnp.max(gl, axis=-1, keepdims=True)
    gsum = jnp.sum(jnp.exp(gl - gmax), axis=-1, keepdims=True)
    g_w = 1.0 / gsum
    g_sel = jnp.min(jnp.where(gl == gmax, lane, big), axis=-1, keepdims=True)
    lo = MOE_GROUPS + g_sel * MOE_EPG
    el = jnp.where((lane >= lo) & (lane < lo + MOE_EPG), logits, neg)
    v1 = jnp.max(el, axis=-1, keepdims=True)
    i1 = jnp.min(jnp.where(el == v1, lane, big), axis=-1, keepdims=True)
    el2 = jnp.where(lane == i1, neg, el)
    v2 = jnp.max(el2, axis=-1, keepdims=True)
    i2 = jnp.min(jnp.where(el2 == v2, lane, big), axis=-1, keepdims=True)
    e2 = jnp.exp(v2 - v1)
    den = 1.0 + e2
    w1 = (1.0 / den) * g_w
    w2 = (e2 / den) * g_w
    eid_ref[...] = jnp.where(lane == 0, i1 - MOE_GROUPS, jnp.where(lane == 1, i2 - MOE_GROUPS, 0))
    cw_ref[...] = jnp.where(lane == 0, w1, jnp.where(lane == 1, w2, 0.0))


def _router(x, g, w_group, b_group, w_expert, b_expert):
    n = MOE_GROUPS + MOE_EXPERTS
    ws = jnp.zeros((D_MODEL, LANES), F32).at[:, :n].set(jnp.concatenate([w_group, w_expert], axis=1))
    bs = jnp.zeros((1, LANES), F32).at[0, :n].set(jnp.concatenate([b_group, b_expert]))
    return pl.pallas_call(
        _router_kernel, grid=(TOKENS // NORM_TM,),
        in_specs=[pl.BlockSpec((NORM_TM, D_MODEL), lambda i: (i, 0)),
                  pl.BlockSpec((1, D_MODEL), lambda i: (0, 0)),
                  pl.BlockSpec((D_MODEL, LANES), lambda i: (0, 0)),
                  pl.BlockSpec((1, LANES), lambda i: (0, 0))],
        out_specs=[pl.BlockSpec((NORM_TM, LANES), lambda i: (i, 0)),
                   pl.BlockSpec((NORM_TM, LANES), lambda i: (i, 0))],
        out_shape=[jax.ShapeDtypeStruct((TOKENS, LANES), I32),
                   jax.ShapeDtypeStruct((TOKENS, LANES), F32)],
        compiler_params=_cparams(("parallel",)),
    )(x, g.reshape(1, D_MODEL), ws, bs)


def _inproj_heads_kernel(a_ref, w_ref, cos_ref, sin_ref, o_ref, wbf_ref):
    j = pl.program_id(0)

    @pl.when(pl.program_id(1) == 0)
    def _():
        wbf_ref[...] = w_ref[...].astype(BF16)

    acc = _dot(a_ref[...], wbf_ref[...])
    heads = acc.shape[1] // NSA_HEAD_DIM
    is_rope = (j < NSA_QD // acc.shape[1]) | (j % 2 == 0)

    @pl.when(is_rope)
    def _():
        c = cos_ref[...]
        s = sin_ref[...]
        for h in range(heads):
            xh = acc[:, h * NSA_HEAD_DIM:(h + 1) * NSA_HEAD_DIM]
            o_ref[h] = (xh * c + pltpu.roll(xh, NSA_HEAD_DIM // 2, 1) * s).astype(BF16)

    @pl.when(jnp.logical_not(is_rope))
    def _():
        for h in range(heads):
            o_ref[h] = acc[:, h * NSA_HEAD_DIM:(h + 1) * NSA_HEAD_DIM].astype(BF16)


def _nsa_inproj(hn, w_in, cos_full, sin_signed):
    tn = NSA_KVD
    n_heads_tile = tn // NSA_HEAD_DIM
    s_tiles = SEQ // MM_TM
    return pl.pallas_call(
        _inproj_heads_kernel, grid=(NSA_MAIN // tn, TOKENS // MM_TM),
        in_specs=[pl.BlockSpec((MM_TM, D_MODEL), lambda j, i: (i, 0)),
                  pl.BlockSpec((D_MODEL, tn), lambda j, i: (0, j)),
                  pl.BlockSpec((MM_TM, NSA_HEAD_DIM), lambda j, i: (i % s_tiles, 0)),
                  pl.BlockSpec((MM_TM, NSA_HEAD_DIM), lambda j, i: (i % s_tiles, 0))],
        out_specs=pl.BlockSpec((n_heads_tile, MM_TM, NSA_HEAD_DIM), lambda j, i: (j, i, 0)),
        out_shape=jax.ShapeDtypeStruct((NSA_MAIN // NSA_HEAD_DIM, TOKENS, NSA_HEAD_DIM), BF16),
        scratch_shapes=[pltpu.VMEM((D_MODEL, tn), BF16)],
        compiler_params=_cparams(("arbitrary", "arbitrary")),
    )(hn, w_in, cos_full, sin_signed)


def _matmul_kernel(a_ref, w_ref, o_ref, wbf_ref):
    @pl.when(pl.program_id(1) == 0)
    def _():
        wbf_ref[...] = w_ref[...].astype(BF16)

    o_ref[...] = _dot(a_ref[...], wbf_ref[...]).astype(o_ref.dtype)


def _matmul_resid_kernel(a_ref, w_ref, r_ref, o_ref, wbf_ref):
    @pl.when(pl.program_id(1) == 0)
    def _():
        wbf_ref[...] = w_ref[...].astype(BF16)

    o_ref[...] = r_ref[...] + _dot(a_ref[...], wbf_ref[...])


def _matmul(a, w, n_cols, tn, out_dtype):
    k = a.shape[1]
    return pl.pallas_call(
        _matmul_kernel, grid=(n_cols // tn, TOKENS // MM_TM),
        in_specs=[pl.BlockSpec((MM_TM, k), lambda j, i: (i, 0)),
                  pl.BlockSpec((k, tn), lambda j, i: (0, j))],
        out_specs=pl.BlockSpec((MM_TM, tn), lambda j, i: (i, j)),
        out_shape=jax.ShapeDtypeStruct((TOKENS, n_cols), out_dtype),
        scratch_shapes=[pltpu.VMEM((k, tn), BF16)],
        compiler_params=_cparams(("arbitrary", "arbitrary")),
    )(a, w)


def _matmul_resid(a, w, resid, tn):
    k = a.shape[1]
    n = w.shape[1]
    return pl.pallas_call(
        _matmul_resid_kernel, grid=(n // tn, TOKENS // MM_TM),
        in_specs=[pl.BlockSpec((MM_TM, k), lambda j, i: (i, 0)),
                  pl.BlockSpec((k, tn), lambda j, i: (0, j)),
                  pl.BlockSpec((MM_TM, tn), lambda j, i: (i, j))],
        out_specs=pl.BlockSpec((MM_TM, tn), lambda j, i: (i, j)),
        out_shape=jax.ShapeDtypeStruct((TOKENS, n), F32),
        scratch_shapes=[pltpu.VMEM((k, tn), BF16)],
        compiler_params=_cparams(("arbitrary", "arbitrary")),
    )(a, w, resid)


def _compress_kernel(x_ref, pe_ref, w1_ref, w2_ref, o_ref):
    half = CMP_STRIDE * NSA_HEAD_DIM
    x = x_ref[0, 0, 0]
    w1 = w1_ref[0].astype(BF16)
    top = _dot(x, w1[:half])
    bot = _dot(x, w1[half:])
    pe = jnp.broadcast_to(pe_ref[0], (8, 2 * half)).astype(BF16)
    pe_bias = _dot(pe, w1)[0:1]
    hid = top + pltpu.roll(bot, bot.shape[0] - 1, 0) + pe_bias
    act = jax.nn.gelu(hid)
    o_ref[0, 0, 0] = _dot(act.astype(BF16), w2_ref[0].astype(BF16))


def _compress(kv_chunks, pe, w1, w2):
    n_chunk = SEQ // CMP_STRIDE
    feat = CMP_STRIDE * NSA_HEAD_DIM
    return pl.pallas_call(
        _compress_kernel, grid=(2, NSA_KV_HEADS, BATCH),
        in_specs=[pl.BlockSpec((1, 1, 1, n_chunk, feat), lambda a, g, b: (a, g, b, 0, 0)),
                  pl.BlockSpec((1, 1, 2 * feat), lambda a, g, b: (a, 0, 0)),
                  pl.BlockSpec((1, 2 * feat, CMP_HIDDEN), lambda a, g, b: (a, 0, 0)),
                  pl.BlockSpec((1, CMP_HIDDEN, NSA_HEAD_DIM), lambda a, g, b: (a, 0, 0))],
        out_specs=pl.BlockSpec((1, 1, 1, n_chunk, NSA_HEAD_DIM), lambda a, g, b: (a, g, b, 0, 0)),
        out_shape=jax.ShapeDtypeStruct((2, NSA_KV_HEADS, BATCH, n_chunk, NSA_HEAD_DIM), F32),
        compiler_params=_cparams(("parallel", "parallel", "parallel")),
    )(kv_chunks, pe.reshape(2, 1, 2 * feat), w1, w2)


def _nsa_attn_kernel(q_ref, ks_ref, vs_ref, kw_ref, vw_ref, kc_ref, vc_ref, gate_ref, o_ref,
                     m_scr, l_scr, acc_scr):
    qi = pl.program_id(2)
    tq = ATT_TQ
    r_heads = NSA_Q_PER_KV
    scale = NSA_HEAD_DIM ** -0.5
    q = q_ref[...].reshape(r_heads * tq, NSA_HEAD_DIM)
    t_glob = qi * tq + lax.broadcasted_iota(I32, (tq, LANES), 0)
    lane = lax.broadcasted_iota(I32, (tq, LANES), 1)

    kc = kc_ref[0, 0, 0].astype(BF16)
    vc = vc_ref[0, 0, 0].astype(BF16)
    s = _dot_nt(q, kc) * scale
    ok_c = ((lane * CMP_STRIDE + CMP_BLOCK - 1) <= t_glob) & (lane < N_CMP)
    s3 = jnp.where(ok_c[None], s.reshape(r_heads, tq, LANES), NEG_INF)
    e = jnp.exp(s3 - jnp.max(s3, axis=-1, keepdims=True))
    p = e / jnp.sum(e, axis=-1, keepdims=True)
    p = jnp.where(ok_c[None], p, 0.0)
    o_cmp = _dot(p.reshape(r_heads * tq, LANES).astype(BF16), vc)
    p_sum = p[0]
    for r in range(1, r_heads):
        p_sum = p_sum + p[r]

    c_row = lax.broadcasted_iota(I32, (LANES, LANES), 0)
    j_col = lax.broadcasted_iota(I32, (LANES, LANES), 1) % N_SLC
    c_start = c_row * CMP_STRIDE
    s_start = j_col * SLC_BLOCK
    ov = jnp.maximum(jnp.minimum(c_start + CMP_BLOCK, s_start + SLC_BLOCK)
                     - jnp.maximum(c_start, s_start), 0).astype(F32) / CMP_BLOCK
    imp = _dot_split_lhs(p_sum, ov.astype(BF16))
    jblk = lane % N_SLC
    dist = t_glob // SLC_BLOCK - jblk
    forced = (jblk == 0) | ((dist >= 0) & (dist < SLC_LOCAL))
    imp = jnp.where(forced, SLC_FORCE, imp)
    imp = jnp.where(dist >= 0, imp, -jnp.inf)
    cnt = jnp.zeros((tq, LANES), I32)
    for d in range(1, N_SLC):
        other = pltpu.roll(imp, d, 1)
        ahead = (other > imp) | ((other == imp) & (jblk >= d))
        cnt = cnt + ahead.astype(I32)
    sel = jnp.where((cnt < min(SLC_TOPK, N_SLC)) & (lane < N_SLC), 1.0, 0.0).astype(BF16)

    def attend(k_ref, v_ref, lo, hi, mask_fn):
        m_scr[...] = jnp.full(m_scr.shape, NEG_INF, F32)
        l_scr[...] = jnp.zeros(l_scr.shape, F32)
        acc_scr[...] = jnp.zeros(acc_scr.shape, F32)

        def body(kt, carry):
            start = pl.multiple_of(kt * tq, tq)
            k = k_ref[0, pl.ds(start, tq), :]
            v = v_ref[0, pl.ds(start, tq), :]
            sc = _dot_nt(q, k) * scale
            ok = mask_fn(kt)
            sc3 = jnp.where(ok[None], sc.reshape(r_heads, tq, tq), NEG_INF)
            m_old = m_scr[...]
            m_new = jnp.maximum(m_old, jnp.max(sc3, axis=-1, keepdims=True))
            alpha = jnp.exp(m_old - m_new)
            pr = jnp.where(ok[None], jnp.exp(sc3 - m_new), 0.0)
            l_scr[...] = alpha * l_scr[...] + jnp.sum(pr, axis=-1, keepdims=True)
            m_scr[...] = m_new
            pv = _dot(pr.reshape(r_heads * tq, tq).astype(BF16), v)
            acc_scr[...] = alpha.reshape(r_heads * tq, 1) * acc_scr[...] + pv
            return carry

        lax.fori_loop(lo, hi, body, 0)
        return acc_scr[...] / l_scr[...].reshape(r_heads * tq, 1)

    def slc_mask(kt):
        key = kt * tq + lane
        blk_row = lax.broadcasted_iota(I32, (LANES, tq), 0)
        blk_of_key = (kt * tq + lax.broadcasted_iota(I32, (LANES, tq), 1)) // SLC_BLOCK
        expand = jnp.where(blk_row == blk_of_key, 1.0, 0.0).astype(BF16)
        picked = _dot(sel, expand)
        return (picked > 0.5) & (key <= t_glob)

    def win_mask(kt):
        key = kt * tq + lane
        return (key <= t_glob) & (key > t_glob - WINDOW)

    o_slc = attend(ks_ref, vs_ref, 0, qi + 1, slc_mask)
    o_win = attend(kw_ref, vw_ref, jnp.maximum(qi - WINDOW // tq, 0), qi + 1, win_mask)

    gate = jax.nn.sigmoid(gate_ref[0])
    for r in range(r_heads):
        rows = slice(r * tq, (r + 1) * tq)
        o = (gate[:, 3 * r:3 * r + 1] * o_cmp[rows] + gate[:, 3 * r + 1:3 * r + 2] * o_slc[rows]
             + gate[:, 3 * r + 2:3 * r + 3] * o_win[rows])
        o_ref[:, r * NSA_HEAD_DIM:(r + 1) * NSA_HEAD_DIM] = o.astype(BF16)


def _nsa_attention(heads, kc_vc, gates):
    tq = ATT_TQ
    nq = SEQ // tq
    r = NSA_Q_PER_KV
    g_heads = NSA_KV_HEADS
    q_spec = pl.BlockSpec((r, tq, NSA_HEAD_DIM), lambda b, g, i: (g, b * nq + i, 0))

    def kv_spec(first_head):
        return pl.BlockSpec((1, SEQ, NSA_HEAD_DIM), lambda b, g, i: (first_head + g, b, 0))

    first = NSA_HEADS
    specs = [q_spec,
             kv_spec(first + 2 * g_heads), kv_spec(first + 3 * g_heads),
             kv_spec(first + 4 * g_heads), kv_spec(first + 5 * g_heads),
             pl.BlockSpec((1, 1, 1, SEQ // CMP_STRIDE, NSA_HEAD_DIM), lambda b, g, i: (0, g, b, 0, 0)),
             pl.BlockSpec((1, 1, 1, SEQ // CMP_STRIDE, NSA_HEAD_DIM), lambda b, g, i: (1, g, b, 0, 0)),
             pl.BlockSpec((1, tq, 3 * r), lambda b, g, i: (g, b * nq + i, 0))]
    return pl.pallas_call(
        _nsa_attn_kernel, grid=(BATCH, g_heads, nq),
        in_specs=specs,
        out_specs=pl.BlockSpec((tq, r * NSA_HEAD_DIM), lambda b, g, i: (b * nq + i, g)),
        out_shape=jax.ShapeDtypeStruct((TOKENS, NSA_QD), BF16),
        scratch_shapes=[pltpu.VMEM((r, tq, 1), F32), pltpu.VMEM((r, tq, 1), F32),
                        pltpu.VMEM((r * tq, NSA_HEAD_DIM), F32)],
        compiler_params=_cparams(("parallel", "parallel", "arbitrary")),
    )(heads, heads, heads, heads, heads, kc_vc, kc_vc, gates)


def _ssd_kernel(z_ref, xs_ref, b_ref, c_ref, dtc_ref, dtr_ref,
                cwx_ref, cwb_ref, cwc_ref, cbx_ref, cbb_ref, cbc_ref,
                dtb_c_ref, alog_c_ref, dtb_r_ref, alog_r_ref, dskip_ref, ng_ref,
                o_ref, ext_x, ext_b, ext_c, acum_r_scr, st_scr):
    g = pl.program_id(1)
    chunk = pl.program_id(2)
    L = SSD_CHUNK
    W = SSD_GROUP_W
    hpg = SSD_HEADS_PER_GROUP

    @pl.when(chunk == 0)
    def _():
        ext_x[0:8, :] = jnp.zeros((8, W), F32)
        ext_b[0:8, :] = jnp.zeros((8, SSD_D_STATE), F32)
        ext_c[0:8, :] = jnp.zeros((8, SSD_D_STATE), F32)
        st_scr[...] = jnp.zeros(st_scr.shape, F32)

    def conv_silu(ext, cur_ref, w_ref, bias_ref):
        ext[8:8 + L, :] = cur_ref[...].astype(F32)
        acc = jnp.broadcast_to(bias_ref[...], (L, ext.shape[1]))
        for k in range(SSD_CONV):
            acc = acc + ext[pl.ds(8 - (SSD_CONV - 1) + k, L), :] * w_ref[k:k + 1, :]
        ext[0:8, :] = ext[L:L + 8, :]
        return jax.nn.silu(acc)

    xs = conv_silu(ext_x, xs_ref, cwx_ref, cbx_ref)
    bm = conv_silu(ext_b, b_ref, cwb_ref, cbb_ref)
    cm = conv_silu(ext_c, c_ref, cwc_ref, cbc_ref)

    dt_c = jax.nn.softplus(dtc_ref[...] + dtb_c_ref[...])
    adt_c = dt_c * (-jnp.exp(alog_c_ref[...]))
    dt_r = jax.nn.softplus(dtr_ref[...] + dtb_r_ref[...])
    adt_r = dt_r * (-jnp.exp(alog_r_ref[...]))
    row = lax.broadcasted_iota(I32, (L, L), 0)
    col = lax.broadcasted_iota(I32, (L, L), 1)
    causal = row >= col
    tri = jnp.where(causal, 1.0, 0.0).astype(BF16)
    tri_t = jnp.where(col >= row, 1.0, 0.0).astype(BF16)
    acum_c = _dot_split_rhs(tri, adt_c)
    acum_r_scr[...] = _dot_split_lhs(adt_r, tri_t)
    a_last = acum_c[L - 1:L, :]

    head_row = lax.broadcasted_iota(I32, (LANES, W), 0)
    head_of_lane = g * hpg + lax.broadcasted_iota(I32, (LANES, W), 1) // SSD_HEAD_DIM
    e_chan = jnp.where(head_row == head_of_lane, 1.0, 0.0).astype(BF16)
    stack = jnp.concatenate([dt_c, jnp.exp(acum_c), jnp.exp(a_last - acum_c),
                             jnp.broadcast_to(jnp.exp(a_last), (8, LANES))], axis=0)
    ex = _dot_split_lhs(stack, e_chan)
    dt_x, ea_x, sd_x, cd_x = ex[0:L], ex[L:2 * L], ex[2 * L:3 * L], ex[3 * L:3 * L + 1]
    head_row2 = lax.broadcasted_iota(I32, (LANES, hpg * L), 0)
    head_of_lane2 = g * hpg + lax.broadcasted_iota(I32, (LANES, hpg * L), 1) // L
    e_head = jnp.where(head_row2 == head_of_lane2, 1.0, 0.0).astype(BF16)
    acum_b = _dot_split_lhs(acum_c, e_head)

    xdt = xs * dt_x
    cb = _dot_nt(cm.astype(BF16), bm.astype(BF16))
    lane_w = lax.broadcasted_iota(I32, (L, LANES), 1)
    first_half = lane_w < SSD_HEAD_DIM
    y_parts = []
    for pair in range(hpg // 2):
        xd = xdt[:, pair * LANES:(pair + 1) * LANES]
        y_pair = None
        for sub in range(2):
            h = 2 * pair + sub
            a_row = acum_r_scr[pl.ds(g * hpg + h, 1), :]
            seg = acum_b[:, h * L:(h + 1) * L] - a_row
            decay = jnp.where(causal, jnp.exp(jnp.where(causal, seg, 0.0)), 0.0)
            m_h = (cb * decay).astype(BF16)
            x_h = jnp.where(first_half if sub == 0 else jnp.logical_not(first_half), xd, 0.0)
            term = _dot(m_h, x_h.astype(BF16))
            y_pair = term if y_pair is None else y_pair + term
        y_parts.append(y_pair)
    y_diag = jnp.concatenate(y_parts, axis=1)

    st = st_scr[...]
    y_off = _dot(cm.astype(BF16), st.astype(BF16)) * ea_x
    st_scr[...] = st * cd_x + _dot_tn(bm.astype(BF16), (xdt * sd_x).astype(BF16))

    y = y_diag + y_off + xs * dskip_ref[...]
    y = y * jax.nn.silu(z_ref[...].astype(F32))
    y = y * lax.rsqrt(jnp.mean(y * y, axis=-1, keepdims=True) + NORM_EPS)
    o_ref[...] = (y * ng_ref[...]).astype(BF16)


def _ssd_scan(zx, dt_small, dt_small_t, conv_w, conv_b, dt_bias, a_log, d_skip, norm_g):
    L = SSD_CHUNK
    W = SSD_GROUP_W
    N = SSD_D_STATE
    nc = SEQ // L
    x_off = SSD_D_INNER // W
    b_off = (2 * SSD_D_INNER) // N
    c_off = b_off + SSD_GROUPS
    cw_b_off = SSD_D_INNER // N
    cw_c_off = cw_b_off + SSD_GROUPS

    def pad_heads(v):
        return jnp.zeros((LANES,), F32).at[:SSD_HEADS].set(v)

    dtb = pad_heads(dt_bias)
    alog = pad_heads(a_log)
    conv_b2 = conv_b.reshape(1, SSD_CONV_CH)
    d_chan = jnp.repeat(d_skip, SSD_HEAD_DIM).reshape(1, SSD_D_INNER)
    row = lambda b, g, c: b * nc + c
    in_specs = [
        pl.BlockSpec((L, W), lambda b, g, c: (row(b, g, c), g)),
        pl.BlockSpec((L, W), lambda b, g, c: (row(b, g, c), x_off + g)),
        pl.BlockSpec((L, N), lambda b, g, c: (row(b, g, c), b_off + g)),
        pl.BlockSpec((L, N), lambda b, g, c: (row(b, g, c), c_off + g)),
        pl.BlockSpec((L, LANES), lambda b, g, c: (row(b, g, c), 0)),
        pl.BlockSpec((LANES, L), lambda b, g, c: (0, row(b, g, c))),
        pl.BlockSpec((SSD_CONV, W), lambda b, g, c: (0, g)),
        pl.BlockSpec((SSD_CONV, N), lambda b, g, c: (0, cw_b_off + g)),
        pl.BlockSpec((SSD_CONV, N), lambda b, g, c: (0, cw_c_off + g)),
        pl.BlockSpec((1, W), lambda b, g, c: (0, g)),
        pl.BlockSpec((1, N), lambda b, g, c: (0, cw_b_off + g)),
        pl.BlockSpec((1, N), lambda b, g, c: (0, cw_c_off + g)),
        pl.BlockSpec((1, LANES), lambda b, g, c: (0, 0)),
        pl.BlockSpec((1, LANES), lambda b, g, c: (0, 0)),
        pl.BlockSpec((LANES, 1), lambda b, g, c: (0, 0)),
        pl.BlockSpec((LANES, 1), lambda b, g, c: (0, 0)),
        pl.BlockSpec((1, W), lambda b, g, c: (0, g)),
        pl.BlockSpec((1, W), lambda b, g, c: (0, g)),
    ]
    return pl.pallas_call(
        _ssd_kernel, grid=(BATCH, SSD_GROUPS, nc),
        in_specs=in_specs,
        out_specs=pl.BlockSpec((L, W), lambda b, g, c: (row(b, g, c), g)),
        out_shape=jax.ShapeDtypeStruct((TOKENS, SSD_D_INNER), BF16),
        scratch_shapes=[pltpu.VMEM((L + 8, W), F32), pltpu.VMEM((L + 8, N), F32),
                        pltpu.VMEM((L + 8, N), F32), pltpu.VMEM((LANES, L), F32),
                        pltpu.VMEM((N, W), F32)],
        compiler_params=_cparams(("parallel", "parallel", "arbitrary")),
    )(zx, zx, zx, zx, dt_small, dt_small_t, conv_w, conv_w, conv_w, conv_b2, conv_b2, conv_b2,
      dtb.reshape(1, LANES), alog.reshape(1, LANES), dtb.reshape(LANES, 1), alog.reshape(LANES, 1),
      d_chan, norm_g.reshape(1, SSD_D_INNER))


def _gather_rows(src_hbm, idx_ref, base, dst, sem, n_rows):
    def body(r, carry):
        tok = idx_ref[base + r]
        pltpu.make_async_copy(src_hbm.at[pl.ds(tok, 1), :], dst.at[pl.ds(r, 1), :], sem).start()
        return carry

    lax.fori_loop(0, n_rows, body, 0, unroll=GATHER_UNROLL)


def _moe_ffn_kernel(te_ref, tok_ref, nact_ref, x_hbm, g_ref, wg_ref, wu_ref, wd_ref, y_ref,
                    buf, sem, wg_bf, wu_bf, wd_bf):
    i = pl.program_id(0)
    n_act = nact_ref[0]
    tm = MOE_TM
    slot = i % 2

    def issue(tile, s):
        _gather_rows(x_hbm, tok_ref, tile * tm, buf.at[s], sem.at[s], tm)

    @pl.when((i == 0) & (n_act > 0))
    def _():
        issue(0, 0)

    @pl.when(i + 1 < n_act)
    def _():
        issue(i + 1, 1 - slot)

    e_cur = te_ref[i]
    e_prev = te_ref[jnp.maximum(i - 1, 0)]

    @pl.when((i == 0) | (e_cur != e_prev))
    def _():
        wg_bf[...] = wg_ref[...].astype(BF16)
        wu_bf[...] = wu_ref[...].astype(BF16)
        wd_bf[...] = wd_ref[...].astype(BF16)

    @pl.when(i < n_act)
    def _():
        pltpu.make_async_copy(x_hbm.at[pl.ds(0, tm), :], buf.at[slot], sem.at[slot]).wait()
        h = _rms(buf[slot], g_ref[...]).astype(BF16)
        act = jax.nn.silu(_dot(h, wg_bf[...])) * _dot(h, wu_bf[...])
        y_ref[...] = _dot(act.astype(BF16), wd_bf[...])

    @pl.when(i >= n_act)
    def _():
        y_ref[...] = jnp.zeros(y_ref.shape, F32)


def _moe_ffn(x, g, w_gate, w_up, w_down, layer, tile_expert, slot_token, n_active):
    grid_spec = pltpu.PrefetchScalarGridSpec(
        num_scalar_prefetch=3, grid=(MOE_TILES,),
        in_specs=[pl.BlockSpec(memory_space=pl.ANY),
                  pl.BlockSpec((1, D_MODEL), lambda i, te, tok, na: (0, 0)),
                  pl.BlockSpec((None, None, D_MODEL, MOE_D_FF), lambda i, te, tok, na: (layer, te[i], 0, 0)),
                  pl.BlockSpec((None, None, D_MODEL, MOE_D_FF), lambda i, te, tok, na: (layer, te[i], 0, 0)),
                  pl.BlockSpec((None, None, MOE_D_FF, D_MODEL), lambda i, te, tok, na: (layer, te[i], 0, 0))],
        out_specs=pl.BlockSpec((MOE_TM, D_MODEL), lambda i, te, tok, na: (i, 0)),
        scratch_shapes=[pltpu.VMEM((2, MOE_TM, D_MODEL), F32), pltpu.SemaphoreType.DMA((2,)),
                        pltpu.VMEM((D_MODEL, MOE_D_FF), BF16), pltpu.VMEM((D_MODEL, MOE_D_FF), BF16),
                        pltpu.VMEM((MOE_D_FF, D_MODEL), BF16)])
    return pl.pallas_call(
        _moe_ffn_kernel, grid_spec=grid_spec,
        out_shape=jax.ShapeDtypeStruct((MOE_ROWS, D_MODEL), F32),
        compiler_params=_cparams(("arbitrary",)),
    )(tile_expert, slot_token, n_active, x, g.reshape(1, D_MODEL), w_gate, w_up, w_down)


def _moe_combine_kernel(pos_ref, x_ref, cw_ref, g_ref, y_hbm, o_ref, buf, sem, *, final_norm):
    i = pl.program_id(0)
    n = pl.num_programs(0)
    tm = CMB_TM
    slot = i % 2

    def issue(tile, s):
        for k in range(MOE_TOPK):
            _gather_rows(y_hbm, pos_ref, (k * (TOKENS // tm) + tile) * tm, buf.at[s, k], sem.at[s], tm)

    @pl.when(i == 0)
    def _():
        issue(0, 0)

    @pl.when(i + 1 < n)
    def _():
        issue(i + 1, 1 - slot)

    for k in range(MOE_TOPK):
        pltpu.make_async_copy(y_hbm.at[pl.ds(0, tm), :], buf.at[slot, k], sem.at[slot]).wait()
    cw = cw_ref[...]
    out = x_ref[...] + cw[:, 0:1] * buf[slot, 0] + cw[:, 1:2] * buf[slot, 1]
    if final_norm:
        out = _rms(out, g_ref[...])
    o_ref[...] = out


def _moe_combine(x, cw, y_sorted, pos_kmajor, g_final, final_norm):
    grid_spec = pltpu.PrefetchScalarGridSpec(
        num_scalar_prefetch=1, grid=(TOKENS // CMB_TM,),
        in_specs=[pl.BlockSpec((CMB_TM, D_MODEL), lambda i, pos: (i, 0)),
                  pl.BlockSpec((CMB_TM, LANES), lambda i, pos: (i, 0)),
                  pl.BlockSpec((1, D_MODEL), lambda i, pos: (0, 0)),
                  pl.BlockSpec(memory_space=pl.ANY)],
        out_specs=pl.BlockSpec((CMB_TM, D_MODEL), lambda i, pos: (i, 0)),
        scratch_shapes=[pltpu.VMEM((2, MOE_TOPK, CMB_TM, D_MODEL), F32), pltpu.SemaphoreType.DMA((2,))])
    return pl.pallas_call(
        functools.partial(_moe_combine_kernel, final_norm=final_norm), grid_spec=grid_spec,
        out_shape=jax.ShapeDtypeStruct((TOKENS, D_MODEL), F32),
        compiler_params=_cparams(("arbitrary",)),
    )(pos_kmajor, x, cw, g_final.reshape(1, D_MODEL), y_sorted)


def _moe_plan(eid):
    e = eid[:, :MOE_TOPK].reshape(-1)
    n_pairs = e.shape[0]
    onehot = (e[:, None] == jnp.arange(MOE_EXPERTS, dtype=I32)[None, :]).astype(I32)
    csum = jnp.cumsum(onehot, axis=0)
    counts = csum[-1]
    rank = jnp.take_along_axis(csum, e[:, None], axis=1)[:, 0] - 1
    padded = ((counts + MOE_TM - 1) // MOE_TM) * MOE_TM
    g_end = jnp.cumsum(padded)
    g_start = g_end - padded
    pos = g_start[e] + rank
    n_active = (g_end[-1] // MOE_TM).astype(I32)
    tile_start = jnp.arange(MOE_TILES, dtype=I32) * MOE_TM
    te = jnp.searchsorted(g_end, tile_start, side='right').astype(I32)
    last = jnp.max(jnp.where(counts > 0, jnp.arange(MOE_EXPERTS, dtype=I32), 0))
    tile_expert = jnp.minimum(te, last)
    slot_token = jnp.zeros((MOE_ROWS,), I32).at[pos].set(jnp.arange(n_pairs, dtype=I32) // MOE_TOPK)
    pos_kmajor = pos.reshape(TOKENS, MOE_TOPK).T.reshape(-1)
    return tile_expert, slot_token, n_active.reshape(1), pos_kmajor


def _hier_moe_add(x, ln_g, w_group, b_group, w_expert, b_expert, w_gate, w_up, w_down, layer,
                  g_final, final_norm):
    eid, cw = _router(x, ln_g, w_group, b_group, w_expert, b_expert)
    tile_expert, slot_token, n_active, pos_kmajor = _moe_plan(eid)
    y_sorted = _moe_ffn(x, ln_g, w_gate, w_up, w_down, layer, tile_expert, slot_token, n_active)
    return _moe_combine(x, cw, y_sorted, pos_kmajor, g_final, final_norm)


def _rope_tables():
    pos = jnp.arange(SEQ, dtype=F32)
    inv = 1.0 / (ROPE_THETA ** (jnp.arange(0, NSA_HEAD_DIM, 2, dtype=F32) / NSA_HEAD_DIM))
    ang = pos[:, None] * inv[None, :]
    cos, sin = jnp.cos(ang), jnp.sin(ang)
    return jnp.concatenate([cos, cos], axis=1), jnp.concatenate([-sin, sin], axis=1)


def _nsa_mixer_add(x, ln_g, w_in, cmp_pe, cmp_w1, cmp_w2, w_out):
    hn, g_lin = _norm_small(x, ln_g, w_in[:, NSA_MAIN:])
    cos_full, sin_signed = _rope_tables()
    heads = _nsa_inproj(hn, w_in, cos_full, sin_signed)
    first_c = NSA_HEADS
    kv_c = heads[first_c:first_c + 2 * NSA_KV_HEADS]
    kv_chunks = kv_c.reshape(2, NSA_KV_HEADS, BATCH, SEQ // CMP_STRIDE, CMP_STRIDE * NSA_HEAD_DIM)
    kc_vc = _compress(kv_chunks, cmp_pe, cmp_w1, cmp_w2)
    gates = g_lin[:, :NSA_GATES].reshape(TOKENS, NSA_KV_HEADS, 3 * NSA_Q_PER_KV).transpose(1, 0, 2)
    o = _nsa_attention(heads, kc_vc, gates)
    return _matmul_resid(o, w_out, x, 512)


def _ssd_mixer_add(x, ln_g, w_in, conv_w, conv_b, dt_bias, a_log, d_skip, norm_g, w_out):
    hn, dt_small, dt_small_t = _norm_small(x, ln_g, w_in[:, SSD_MAIN:], transposed=True)
    zx = _matmul(hn, w_in, SSD_MAIN, 1024, BF16)
    y = _ssd_scan(zx, dt_small, dt_small_t, conv_w, conv_b, dt_bias, a_log, d_skip, norm_g)
    return _matmul_resid(y, w_out, x, 512)


def kernel(x, ln_mix, ln_ffn, ln_final, nsa_w_in, nsa_cmp_pe, nsa_cmp_w1, nsa_cmp_w2, nsa_w_out,
           ssd_w_in, ssd_conv_w, ssd_conv_b, ssd_dt_bias, ssd_a_log, ssd_d, ssd_norm, ssd_w_out,
           moe_w_group, moe_b_group, moe_w_expert, moe_b_expert, moe_w_gate, moe_w_up, moe_w_down):
    h = x.reshape(TOKENS, D_MODEL)
    for i in range(DEPTH):
        j = i // N_MIXERS
        if i % N_MIXERS == 0:
            h = _nsa_mixer_add(h, ln_mix[i], nsa_w_in[j], nsa_cmp_pe[j], nsa_cmp_w1[j], nsa_cmp_w2[j],
                               nsa_w_out[j])
        else:
            h = _ssd_mixer_add(h, ln_mix[i], ssd_w_in[j], ssd_conv_w[j], ssd_conv_b[j], ssd_dt_bias[j],
                               ssd_a_log[j], ssd_d[j], ssd_norm[j], ssd_w_out[j])
        h = _hier_moe_add(h, ln_ffn[i], moe_w_group[i], moe_b_group[i], moe_w_expert[i], moe_b_expert[i],
                          moe_w_gate, moe_w_up, moe_w_down, i, ln_final, i == DEPTH - 1)
    return h.reshape(BATCH, SEQ, D_MODEL)
```

```python
import functools

import jax
import jax.numpy as jnp
from jax import lax
from jax.experimental import pallas as pl
from jax.experimental.pallas import tpu as pltpu

F32 = jnp.float32
BF16 = jnp.bfloat16
I32 = jnp.int32

D_MODEL = 2048
BATCH = 4
SEQ = 2048
TOKENS = BATCH * SEQ
DEPTH = 2
N_MIXERS = 2
NORM_EPS = 1e-6
NEG_INF = -1e30
ROPE_THETA = 10000.0

NSA_HEADS = 16
NSA_KV_HEADS = 4
NSA_HEAD_DIM = D_MODEL // NSA_HEADS
NSA_Q_PER_KV = NSA_HEADS // NSA_KV_HEADS
CMP_BLOCK = 32
CMP_STRIDE = 16
CMP_HIDDEN = 256
N_CMP = (SEQ - CMP_BLOCK) // CMP_STRIDE + 1
SLC_BLOCK = 64
SLC_TOPK = 16
SLC_LOCAL = 2
SLC_FORCE = 1e4
N_SLC = SEQ // SLC_BLOCK
WINDOW = 512
NSA_QD = NSA_HEADS * NSA_HEAD_DIM
NSA_KVD = NSA_KV_HEADS * NSA_HEAD_DIM
NSA_MAIN = NSA_QD + 6 * NSA_KVD
NSA_GATES = 3 * NSA_HEADS

SSD_D_INNER = 2 * D_MODEL
SSD_HEAD_DIM = 64
SSD_HEADS = SSD_D_INNER // SSD_HEAD_DIM
SSD_GROUPS = 8
SSD_HEADS_PER_GROUP = SSD_HEADS // SSD_GROUPS
SSD_D_STATE = 128
SSD_CONV = 4
SSD_CHUNK = 128
SSD_GROUP_W = SSD_D_INNER // SSD_GROUPS
SSD_BC = SSD_GROUPS * SSD_D_STATE
SSD_CONV_CH = SSD_D_INNER + 2 * SSD_BC
SSD_MAIN = SSD_D_INNER + SSD_CONV_CH

MOE_GROUPS = 4
MOE_EPG = 8
MOE_EXPERTS = MOE_GROUPS * MOE_EPG
MOE_TOPK = 2
MOE_D_FF = 512

LANES = 128
VMEM_LIMIT = 56 * 1024 * 1024

NORM_TM = 256
MM_TM = 512
ATT_TQ = 128
ATT_CHUNK_TILES = 4
ATT_ONES = 16
MOE_TM = 256
MOE_TILES = (TOKENS * MOE_TOPK) // MOE_TM + MOE_EXPERTS
MOE_ROWS = MOE_TILES * MOE_TM
CMB_TM = 128
GATHER_UNROLL = 8


def _cparams(sem):
    return pltpu.CompilerParams(dimension_semantics=sem, vmem_limit_bytes=VMEM_LIMIT)


def _split3(x):
    hi = x.astype(BF16)
    r1 = x - hi.astype(F32)
    mid = r1.astype(BF16)
    lo = (r1 - mid.astype(F32)).astype(BF16)
    return hi, mid, lo


def _dot(a, b):
    return jnp.dot(a, b, preferred_element_type=F32)


def _dot_nt(a, b):
    return lax.dot_general(a, b, (((1,), (1,)), ((), ())), preferred_element_type=F32)


def _dot_tn(a, b):
    return lax.dot_general(a, b, (((0,), (0,)), ((), ())), preferred_element_type=F32)


def _dot_split_lhs(x, m_bf16):
    hi, mid, lo = _split3(x)
    return _dot(hi, m_bf16) + _dot(mid, m_bf16) + _dot(lo, m_bf16)


def _dot_split_rhs(m_bf16, x):
    hi, mid, lo = _split3(x)
    return _dot(m_bf16, hi) + _dot(m_bf16, mid) + _dot(m_bf16, lo)


def _dot_x3(a, w):
    a_hi = a.astype(BF16)
    a_lo = (a - a_hi.astype(F32)).astype(BF16)
    w_hi = w.astype(BF16)
    w_lo = (w - w_hi.astype(F32)).astype(BF16)
    return _dot(a_hi, w_hi) + _dot(a_hi, w_lo) + _dot(a_lo, w_hi)


def _dot_x3_nt(a, w):
    a_hi = a.astype(BF16)
    a_lo = (a - a_hi.astype(F32)).astype(BF16)
    w_hi = w.astype(BF16)
    w_lo = (w - w_hi.astype(F32)).astype(BF16)
    return _dot_nt(a_hi, w_hi) + _dot_nt(a_hi, w_lo) + _dot_nt(a_lo, w_hi)


def _rms(x, g):
    y = x * lax.rsqrt(jnp.mean(x * x, axis=-1, keepdims=True) + NORM_EPS)
    return y * g


def _norm_small_kernel(x_ref, g_ref, ws_ref, hn_ref, small_ref):
    y = _rms(x_ref[...], g_ref[...])
    hn_ref[...] = y.astype(BF16)
    small_ref[...] = _dot_x3(y, ws_ref[...])


def _norm_small_t_kernel(x_ref, g_ref, ws_ref, wst_ref, hn_ref, small_ref, small_t_ref):
    y = _rms(x_ref[...], g_ref[...])
    hn_ref[...] = y.astype(BF16)
    small_ref[...] = _dot_x3(y, ws_ref[...])
    small_t_ref[...] = _dot_x3_nt(wst_ref[...], y)


def _norm_small(x, g, w_small, transposed=False):
    n = w_small.shape[1]
    ws = jnp.zeros((D_MODEL, LANES), F32).at[:, :n].set(w_small)
    grid = (TOKENS // NORM_TM,)
    x_spec = pl.BlockSpec((NORM_TM, D_MODEL), lambda i: (i, 0))
    g_spec = pl.BlockSpec((1, D_MODEL), lambda i: (0, 0))
    w_spec = pl.BlockSpec((D_MODEL, LANES), lambda i: (0, 0))
    hn_spec = pl.BlockSpec((NORM_TM, D_MODEL), lambda i: (i, 0))
    sm_spec = pl.BlockSpec((NORM_TM, LANES), lambda i: (i, 0))
    hn_shape = jax.ShapeDtypeStruct((TOKENS, D_MODEL), BF16)
    sm_shape = jax.ShapeDtypeStruct((TOKENS, LANES), F32)
    if not transposed:
        return pl.pallas_call(
            _norm_small_kernel, name="norm_small", grid=grid,
            in_specs=[x_spec, g_spec, w_spec],
            out_specs=[hn_spec, sm_spec],
            out_shape=[hn_shape, sm_shape],
            compiler_params=_cparams(("parallel",)),
        )(x, g.reshape(1, D_MODEL), ws)
    wt_spec = pl.BlockSpec((LANES, D_MODEL), lambda i: (0, 0))
    smt_spec = pl.BlockSpec((LANES, NORM_TM), lambda i: (0, i))
    smt_shape = jax.ShapeDtypeStruct((LANES, TOKENS), F32)
    return pl.pallas_call(
        _norm_small_t_kernel, name="norm_small_t", grid=grid,
        in_specs=[x_spec, g_spec, w_spec, wt_spec],
        out_specs=[hn_spec, sm_spec, smt_spec],
        out_shape=[hn_shape, sm_shape, smt_shape],
        compiler_params=_cparams(("parallel",)),
    )(x, g.reshape(1, D_MODEL), ws, ws.T)


def _router_kernel(x_ref, g_ref, ws_ref, b_ref, eid_ref, cw_ref):
    y = _rms(x_ref[...], g_ref[...])
    logits = _dot_x3(y, ws_ref[...]) + b_ref[...]
    lane = lax.broadcasted_iota(I32, logits.shape, 1)
    big = jnp.int32(LANES)
    neg = -jnp.inf
    gl = jnp.where(lane < MOE_GROUPS, logits, neg)
    gmax = jnp.max(gl, axis=-1, keepdims=True)
    gsum = jnp.sum(jnp.exp(gl - gmax), axis=-1, keepdims=True)
    g_w = 1.0 / gsum
    g_sel = jnp.min(jnp.where(gl == gmax, lane, big), axis=-1, keepdims=True)
    lo = MOE_GROUPS + g_sel * MOE_EPG
    el = jnp.where((lane >= lo) & (lane < lo + MOE_EPG), logits, neg)
    v1 = jnp.max(el, axis=-1, keepdims=True)
    i1 = jnp.min(jnp.where(el == v1, lane, big), axis=-1, keepdims=True)
    el2 = jnp.where(lane == i1, neg, el)
    v2 = jnp.max(el2, axis=-1, keepdims=True)
    i2 = jnp.min(jnp.where(el2 == v2, lane, big), axis=-1, keepdims=True)
    e2 = jnp.exp(v2 - v1)
    den = 1.0 + e2
    w1 = (1.0 / den) * g_w
    w2 = (e2 / den) * g_w
    eid_ref[...] = jnp.where(lane == 0, i1 - MOE_GROUPS, jnp.where(lane == 1, i2 - MOE_GROUPS, 0))
    cw_ref[...] = jnp.where(lane == 0, w1, jnp.where(lane == 1, w2, 0.0))


def _router(x, g, w_group, b_group, w_expert, b_expert):
    n = MOE_GROUPS + MOE_EXPERTS
    ws = jnp.zeros((D_MODEL, LANES), F32).at[:, :n].set(jnp.concatenate([w_group, w_expert], axis=1))
    bs = jnp.zeros((1, LANES), F32).at[0, :n].set(jnp.concatenate([b_group, b_expert]))
    return pl.pallas_call(
        _router_kernel, name="router", grid=(TOKENS // NORM_TM,),
        in_specs=[pl.BlockSpec((NORM_TM, D_MODEL), lambda i: (i, 0)),
                  pl.BlockSpec((1, D_MODEL), lambda i: (0, 0)),
                  pl.BlockSpec((D_MODEL, LANES), lambda i: (0, 0)),
                  pl.BlockSpec((1, LANES), lambda i: (0, 0))],
        out_specs=[pl.BlockSpec((NORM_TM, LANES), lambda i: (i, 0)),
                   pl.BlockSpec((NORM_TM, LANES), lambda i: (i, 0))],
        out_shape=[jax.ShapeDtypeStruct((TOKENS, LANES), I32),
                   jax.ShapeDtypeStruct((TOKENS, LANES), F32)],
        compiler_params=_cparams(("parallel",)),
    )(x, g.reshape(1, D_MODEL), ws, bs)


def _inproj_heads_kernel(a_ref, w_ref, cos_ref, sin_ref, o_ref, wbf_ref):
    j = pl.program_id(0)

    @pl.when(pl.program_id(1) == 0)
    def _():
        wbf_ref[...] = w_ref[...].astype(BF16)

    acc = _dot(a_ref[...], wbf_ref[...])
    heads = acc.shape[1] // NSA_HEAD_DIM
    is_rope = (j < NSA_QD // acc.shape[1]) | (j % 2 == 0)

    @pl.when(is_rope)
    def _():
        c = cos_ref[...]
        s = sin_ref[...]
        for h in range(heads):
            xh = acc[:, h * NSA_HEAD_DIM:(h + 1) * NSA_HEAD_DIM]
            o_ref[h] = (xh * c + pltpu.roll(xh, NSA_HEAD_DIM // 2, 1) * s).astype(BF16)

    @pl.when(jnp.logical_not(is_rope))
    def _():
        for h in range(heads):
            o_ref[h] = acc[:, h * NSA_HEAD_DIM:(h + 1) * NSA_HEAD_DIM].astype(BF16)


def _nsa_inproj(hn, w_in, cos_full, sin_signed):
    tn = NSA_KVD
    n_heads_tile = tn // NSA_HEAD_DIM
    s_tiles = SEQ // MM_TM
    return pl.pallas_call(
        _inproj_heads_kernel, name="nsa_inproj", grid=(NSA_MAIN // tn, TOKENS // MM_TM),
        in_specs=[pl.BlockSpec((MM_TM, D_MODEL), lambda j, i: (i, 0)),
                  pl.BlockSpec((D_MODEL, tn), lambda j, i: (0, j)),
                  pl.BlockSpec((MM_TM, NSA_HEAD_DIM), lambda j, i: (i % s_tiles, 0)),
                  pl.BlockSpec((MM_TM, NSA_HEAD_DIM), lambda j, i: (i % s_tiles, 0))],
        out_specs=pl.BlockSpec((n_heads_tile, MM_TM, NSA_HEAD_DIM), lambda j, i: (j, i, 0)),
        out_shape=jax.ShapeDtypeStruct((NSA_MAIN // NSA_HEAD_DIM, TOKENS, NSA_HEAD_DIM), BF16),
        scratch_shapes=[pltpu.VMEM((D_MODEL, tn), BF16)],
        compiler_params=_cparams(("arbitrary", "arbitrary")),
    )(hn, w_in, cos_full, sin_signed)


def _matmul_kernel(a_ref, w_ref, o_ref, wbf_ref):
    @pl.when(pl.program_id(1) == 0)
    def _():
        wbf_ref[...] = w_ref[...].astype(BF16)

    o_ref[...] = _dot(a_ref[...], wbf_ref[...]).astype(o_ref.dtype)


def _matmul_resid_kernel(a_ref, w_ref, r_ref, o_ref, wbf_ref):
    @pl.when(pl.program_id(1) == 0)
    def _():
        wbf_ref[...] = w_ref[...].astype(BF16)

    o_ref[...] = r_ref[...] + _dot(a_ref[...], wbf_ref[...])


def _matmul(a, w, n_cols, tn, out_dtype):
    k = a.shape[1]
    return pl.pallas_call(
        _matmul_kernel, name="matmul", grid=(n_cols // tn, TOKENS // MM_TM),
        in_specs=[pl.BlockSpec((MM_TM, k), lambda j, i: (i, 0)),
                  pl.BlockSpec((k, tn), lambda j, i: (0, j))],
        out_specs=pl.BlockSpec((MM_TM, tn), lambda j, i: (i, j)),
        out_shape=jax.ShapeDtypeStruct((TOKENS, n_cols), out_dtype),
        scratch_shapes=[pltpu.VMEM((k, tn), BF16)],
        compiler_params=_cparams(("arbitrary", "arbitrary")),
    )(a, w)


def _matmul_resid(a, w, resid, tn):
    k = a.shape[1]
    n = w.shape[1]
    return pl.pallas_call(
        _matmul_resid_kernel, name="matmul_resid", grid=(n // tn, TOKENS // MM_TM),
        in_specs=[pl.BlockSpec((MM_TM, k), lambda j, i: (i, 0)),
                  pl.BlockSpec((k, tn), lambda j, i: (0, j)),
                  pl.BlockSpec((MM_TM, tn), lambda j, i: (i, j))],
        out_specs=pl.BlockSpec((MM_TM, tn), lambda j, i: (i, j)),
        out_shape=jax.ShapeDtypeStruct((TOKENS, n), F32),
        scratch_shapes=[pltpu.VMEM((k, tn), BF16)],
        compiler_params=_cparams(("arbitrary", "arbitrary")),
    )(a, w, resid)


def _compress_kernel(x_ref, pe_ref, w1_ref, w2_ref, o_ref):
    half = CMP_STRIDE * NSA_HEAD_DIM
    x = x_ref[0, 0, 0]
    w1 = w1_ref[0].astype(BF16)
    top = _dot(x, w1[:half])
    bot = _dot(x, w1[half:])
    pe = jnp.broadcast_to(pe_ref[0], (8, 2 * half)).astype(BF16)
    pe_bias = _dot(pe, w1)[0:1]
    hid = top + pltpu.roll(bot, bot.shape[0] - 1, 0) + pe_bias
    act = jax.nn.gelu(hid)
    o_ref[0, 0, 0] = _dot(act.astype(BF16), w2_ref[0].astype(BF16))


def _compress(kv_chunks, pe, w1, w2):
    n_chunk = SEQ // CMP_STRIDE
    feat = CMP_STRIDE * NSA_HEAD_DIM
    return pl.pallas_call(
        _compress_kernel, name="nsa_compress", grid=(2, NSA_KV_HEADS, BATCH),
        in_specs=[pl.BlockSpec((1, 1, 1, n_chunk, feat), lambda a, g, b: (a, g, b, 0, 0)),
                  pl.BlockSpec((1, 1, 2 * feat), lambda a, g, b: (a, 0, 0)),
                  pl.BlockSpec((1, 2 * feat, CMP_HIDDEN), lambda a, g, b: (a, 0, 0)),
                  pl.BlockSpec((1, CMP_HIDDEN, NSA_HEAD_DIM), lambda a, g, b: (a, 0, 0))],
        out_specs=pl.BlockSpec((1, 1, 1, n_chunk, NSA_HEAD_DIM), lambda a, g, b: (a, g, b, 0, 0)),
        out_shape=jax.ShapeDtypeStruct((2, NSA_KV_HEADS, BATCH, n_chunk, NSA_HEAD_DIM), F32),
        compiler_params=_cparams(("parallel", "parallel", "parallel")),
    )(kv_chunks, pe.reshape(2, 1, 2 * feat), w1, w2)


def _nsa_attn_kernel(q_ref, ks_ref, vs_ref, kw_ref, vw_ref, kc_ref, vc_ref, gate_ref, o_ref,
                     vst_scr, vwt_scr, sel_scr, acc_scr):
    qi = pl.program_id(2)
    tq = ATT_TQ
    tk = ATT_TQ
    dh = NSA_HEAD_DIM
    r_heads = NSA_Q_PER_KV
    n_kt = SEQ // tk
    scale = dh ** -0.5

    @pl.when(qi == 0)
    def _():
        ones = jnp.ones((ATT_ONES, tk), BF16)
        for kt in range(n_kt):
            rows = slice(kt * tk, (kt + 1) * tk)
            vst_scr[kt, 0:dh, :] = vs_ref[0, rows, :].astype(F32).T.astype(BF16)
            vwt_scr[kt, 0:dh, :] = vw_ref[0, rows, :].astype(F32).T.astype(BF16)
            vst_scr[kt, dh:dh + ATT_ONES, :] = ones
            vwt_scr[kt, dh:dh + ATT_ONES, :] = ones

    q_t = jnp.concatenate([q_ref[r].astype(F32).T for r in range(r_heads)], axis=1).astype(BF16)
    sub = lax.broadcasted_iota(I32, (tk, tq), 0)
    t_pos = qi * tq + lax.broadcasted_iota(I32, (tk, tq), 1)

    def tile4(a):
        return jnp.concatenate([a] * r_heads, axis=1)

    kc = kc_ref[0, 0, 0].astype(BF16)
    vc = vc_ref[0, 0, 0].astype(BF16)
    ok_c = jnp.where(sub * CMP_STRIDE + CMP_BLOCK - 1 <= t_pos, jnp.where(sub < N_CMP, 1.0, 0.0), 0.0)
    ok_c4 = tile4(ok_c)
    s_c = _dot(kc, q_t) * scale + (ok_c4 - 1.0) * (-NEG_INF)
    e_c = jnp.exp(s_c - jnp.max(s_c, axis=0, keepdims=True))
    p_c = (e_c / jnp.sum(e_c, axis=0, keepdims=True)) * ok_c4
    o_cmp = _dot_tn(vc, p_c.astype(BF16))
    p_sum = p_c[:, 0:tq]
    for r in range(1, r_heads):
        p_sum = p_sum + p_c[:, r * tq:(r + 1) * tq]

    blk_row = lax.broadcasted_iota(I32, (LANES, LANES), 0)
    cmp_col = lax.broadcasted_iota(I32, (LANES, LANES), 1)
    s_start = blk_row * SLC_BLOCK
    c_start = cmp_col * CMP_STRIDE
    ov_t = jnp.maximum(jnp.minimum(c_start + CMP_BLOCK, s_start + SLC_BLOCK)
                       - jnp.maximum(c_start, s_start), 0).astype(F32) / CMP_BLOCK
    ov_t = jnp.where(blk_row < N_SLC, ov_t, 0.0).astype(BF16)
    imp = _dot_split_rhs(ov_t, p_sum)[0:N_SLC]
    j_blk = lax.broadcasted_iota(I32, (N_SLC, tq), 0)
    dist = (qi * tq + lax.broadcasted_iota(I32, (N_SLC, tq), 1)) // SLC_BLOCK - j_blk
    imp = jnp.where(j_blk == 0, SLC_FORCE, jnp.where(dist < 0, imp, jnp.where(dist < SLC_LOCAL, SLC_FORCE, imp)))
    imp = jnp.where(dist >= 0, imp, -jnp.inf)
    cnt = jnp.zeros((N_SLC, tq), I32)
    for k in range(N_SLC):
        row_k = imp[k:k + 1, :]
        tie = jnp.where(j_blk > k, 1, 0)
        cnt = cnt + jnp.where(row_k > imp, 1, jnp.where(row_k == imp, tie, 0))
    sel = jnp.where(cnt < min(SLC_TOPK, N_SLC), 1.0, 0.0)
    for j in range(N_SLC):
        sel_scr[8 * j:8 * j + 8, :] = jnp.broadcast_to(sel[j:j + 1, :], (8, tq))

    def scores(k, okf):
        return _dot(k, q_t) * scale + (okf - 1.0) * (-NEG_INF)

    def weighted_values(vt_scr, kt0, pr, n_tiles):
        out = None
        for u in range(n_tiles):
            term = _dot(vt_scr[kt0 + u], pr[u * tk:(u + 1) * tk].astype(BF16))
            out = term if out is None else out + term
        return out

    n_ct = ATT_CHUNK_TILES
    ck = n_ct * tk
    sub_c = lax.broadcasted_iota(I32, (ck, tq), 0)
    t_pos_c = qi * tq + lax.broadcasted_iota(I32, (ck, tq), 1)
    blocks_per_chunk = ck // SLC_BLOCK
    acc_scr[...] = jnp.zeros(acc_scr.shape, F32)

    def slc_body(c, m_old):
        start = pl.multiple_of(c * ck, ck)
        k = ks_ref[0, pl.ds(start, ck), :]
        rows8 = sel_scr[pl.ds(pl.multiple_of(c * (8 * blocks_per_chunk), 8 * blocks_per_chunk),
                              8 * blocks_per_chunk), :]
        picked = jnp.concatenate(
            [rows8[8 * u:8 * u + 8] for u in range(blocks_per_chunk) for _ in range(SLC_BLOCK // 8)], axis=0)
        okf = tile4(jnp.where(start + sub_c <= t_pos_c, picked, 0.0))
        sc = scores(k, okf)
        m_new = jnp.maximum(m_old, jnp.max(sc, axis=0, keepdims=True))
        alpha = jnp.exp(m_old - m_new)
        pr = jnp.exp(sc - m_new) * okf
        acc_scr[...] = alpha * acc_scr[...] + weighted_values(vst_scr, c * n_ct, pr, n_ct)
        return m_new

    lax.fori_loop(0, (qi + n_ct) // n_ct, slc_body, jnp.full((1, r_heads * tq), NEG_INF, F32))
    acc = acc_scr[...]
    o_slc = acc[0:dh] / acc[dh:dh + 1]

    n_wt = WINDOW // tk + 1
    kt0 = jnp.maximum(qi - WINDOW // tk, 0)
    w_start = pl.multiple_of(kt0 * tk, tk)
    key_w = w_start + lax.broadcasted_iota(I32, (n_wt * tk, tq), 0)
    t_pos_w = qi * tq + lax.broadcasted_iota(I32, (n_wt * tk, tq), 1)
    okf_w = tile4(jnp.where(key_w <= t_pos_w, jnp.where(key_w > t_pos_w - WINDOW, 1.0, 0.0), 0.0))
    sc_w = scores(kw_ref[0, pl.ds(w_start, n_wt * tk), :], okf_w)
    pr_w = jnp.exp(sc_w - jnp.max(sc_w, axis=0, keepdims=True)) * okf_w
    acc_w = weighted_values(vwt_scr, kt0, pr_w, n_wt)
    o_win = acc_w[0:dh] / acc_w[dh:dh + 1]

    gate = jax.nn.sigmoid(gate_ref[0])
    for r in range(r_heads):
        cols = slice(r * tq, (r + 1) * tq)
        o = (gate[3 * r:3 * r + 1] * o_cmp[:, cols] + gate[3 * r + 1:3 * r + 2] * o_slc[:, cols]
             + gate[3 * r + 2:3 * r + 3] * o_win[:, cols])
        o_ref[:, r * dh:(r + 1) * dh] = o.T.astype(BF16)


def _nsa_attention(heads, kc_vc, gates):
    tq = ATT_TQ
    nq = SEQ // tq
    r = NSA_Q_PER_KV
    g_heads = NSA_KV_HEADS
    q_spec = pl.BlockSpec((r, tq, NSA_HEAD_DIM), lambda b, g, i: (g, b * nq + i, 0))

    def kv_spec(first_head):
        return pl.BlockSpec((1, SEQ, NSA_HEAD_DIM), lambda b, g, i: (first_head + g, b, 0))

    first = NSA_HEADS
    specs = [q_spec,
             kv_spec(first + 2 * g_heads), kv_spec(first + 3 * g_heads),
             kv_spec(first + 4 * g_heads), kv_spec(first + 5 * g_heads),
             pl.BlockSpec((1, 1, 1, SEQ // CMP_STRIDE, NSA_HEAD_DIM), lambda b, g, i: (0, g, b, 0, 0)),
             pl.BlockSpec((1, 1, 1, SEQ // CMP_STRIDE, NSA_HEAD_DIM), lambda b, g, i: (1, g, b, 0, 0)),
             pl.BlockSpec((1, 3 * r, tq), lambda b, g, i: (g, 0, b * nq + i))]
    vt_shape = (SEQ // tq, NSA_HEAD_DIM + ATT_ONES, tq)
    return pl.pallas_call(
        _nsa_attn_kernel, name="nsa_attn", grid=(BATCH, g_heads, nq),
        in_specs=specs,
        out_specs=pl.BlockSpec((tq, r * NSA_HEAD_DIM), lambda b, g, i: (b * nq + i, g)),
        out_shape=jax.ShapeDtypeStruct((TOKENS, NSA_QD), BF16),
        scratch_shapes=[pltpu.VMEM(vt_shape, BF16), pltpu.VMEM(vt_shape, BF16),
                        pltpu.VMEM((8 * N_SLC, tq), F32),
                        pltpu.VMEM((NSA_HEAD_DIM + ATT_ONES, r * tq), F32)],
        compiler_params=_cparams(("arbitrary", "arbitrary", "arbitrary")),
    )(heads, heads, heads, heads, heads, kc_vc, kc_vc, gates)


def _ssd_kernel(z_ref, xs_ref, b_ref, c_ref, dtc_ref, dtr_ref,
                cwx_ref, cwb_ref, cwc_ref, cbx_ref, cbb_ref, cbc_ref,
                dtb_c_ref, alog_c_ref, dtb_r_ref, alog_r_ref, dskip_ref, ng_ref,
                o_ref, ext_x, ext_b, ext_c, acum_r_scr, st_scr):
    g = pl.program_id(1)
    chunk = pl.program_id(2)
    L = SSD_CHUNK
    W = SSD_GROUP_W
    hpg = SSD_HEADS_PER_GROUP

    @pl.when(chunk == 0)
    def _():
        ext_x[0:8, :] = jnp.zeros((8, W), F32)
        ext_b[0:8, :] = jnp.zeros((8, SSD_D_STATE), F32)
        ext_c[0:8, :] = jnp.zeros((8, SSD_D_STATE), F32)
        st_scr[...] = jnp.zeros(st_scr.shape, F32)

    def conv_silu(ext, cur_ref, w_ref, bias_ref):
        ext[8:8 + L, :] = cur_ref[...].astype(F32)
        acc = jnp.broadcast_to(bias_ref[...], (L, ext.shape[1]))
        for k in range(SSD_CONV):
            acc = acc + ext[pl.ds(8 - (SSD_CONV - 1) + k, L), :] * w_ref[k:k + 1, :]
        ext[0:8, :] = ext[L:L + 8, :]
        return jax.nn.silu(acc)

    xs = conv_silu(ext_x, xs_ref, cwx_ref, cbx_ref)
    bm = conv_silu(ext_b, b_ref, cwb_ref, cbb_ref)
    cm = conv_silu(ext_c, c_ref, cwc_ref, cbc_ref)

    dt_c = jax.nn.softplus(dtc_ref[...] + dtb_c_ref[...])
    adt_c = dt_c * (-jnp.exp(alog_c_ref[...]))
    dt_r = jax.nn.softplus(dtr_ref[...] + dtb_r_ref[...])
    adt_r = dt_r * (-jnp.exp(alog_r_ref[...]))
    row = lax.broadcasted_iota(I32, (L, L), 0)
    col = lax.broadcasted_iota(I32, (L, L), 1)
    causal = row >= col
    tri = jnp.where(causal, 1.0, 0.0).astype(BF16)
    tri_t = jnp.where(col >= row, 1.0, 0.0).astype(BF16)
    acum_c = _dot_split_rhs(tri, adt_c)
    acum_r_scr[...] = _dot_split_lhs(adt_r, tri_t)
    a_last = acum_c[L - 1:L, :]

    head_row = lax.broadcasted_iota(I32, (LANES, W), 0)
    head_of_lane = g * hpg + lax.broadcasted_iota(I32, (LANES, W), 1) // SSD_HEAD_DIM
    e_chan = jnp.where(head_row == head_of_lane, 1.0, 0.0).astype(BF16)
    stack = jnp.concatenate([dt_c, jnp.exp(acum_c), jnp.exp(a_last - acum_c),
                             jnp.broadcast_to(jnp.exp(a_last), (8, LANES))], axis=0)
    ex = _dot_split_lhs(stack, e_chan)
    dt_x, ea_x, sd_x, cd_x = ex[0:L], ex[L:2 * L], ex[2 * L:3 * L], ex[3 * L:3 * L + 1]
    head_row2 = lax.broadcasted_iota(I32, (LANES, hpg * L), 0)
    head_of_lane2 = g * hpg + lax.broadcasted_iota(I32, (LANES, hpg * L), 1) // L
    e_head = jnp.where(head_row2 == head_of_lane2, 1.0, 0.0).astype(BF16)
    acum_b = _dot_split_lhs(acum_c, e_head)

    xdt = xs * dt_x
    cb = _dot_nt(cm.astype(BF16), bm.astype(BF16))
    lane_w = lax.broadcasted_iota(I32, (L, LANES), 1)
    first_half = lane_w < SSD_HEAD_DIM
    y_parts = []
    for pair in range(hpg // 2):
        xd = xdt[:, pair * LANES:(pair + 1) * LANES]
        y_pair = None
        for sub in range(2):
            h = 2 * pair + sub
            a_row = acum_r_scr[pl.ds(g * hpg + h, 1), :]
            seg = acum_b[:, h * L:(h + 1) * L] - a_row
            decay = jnp.where(causal, jnp.exp(jnp.where(causal, seg, 0.0)), 0.0)
            m_h = (cb * decay).astype(BF16)
            x_h = jnp.where(first_half if sub == 0 else jnp.logical_not(first_half), xd, 0.0)
            term = _dot(m_h, x_h.astype(BF16))
            y_pair = term if y_pair is None else y_pair + term
        y_parts.append(y_pair)
    y_diag = jnp.concatenate(y_parts, axis=1)

    st = st_scr[...]
    y_off = _dot(cm.astype(BF16), st.astype(BF16)) * ea_x
    st_scr[...] = st * cd_x + _dot_tn(bm.astype(BF16), (xdt * sd_x).astype(BF16))

    y = y_diag + y_off + xs * dskip_ref[...]
    y = y * jax.nn.silu(z_ref[...].astype(F32))
    y = y * lax.rsqrt(jnp.mean(y * y, axis=-1, keepdims=True) + NORM_EPS)
    o_ref[...] = (y * ng_ref[...]).astype(BF16)


def _ssd_scan(zx, dt_small, dt_small_t, conv_w, conv_b, dt_bias, a_log, d_skip, norm_g):
    L = SSD_CHUNK
    W = SSD_GROUP_W
    N = SSD_D_STATE
    nc = SEQ // L
    x_off = SSD_D_INNER // W
    b_off = (2 * SSD_D_INNER) // N
    c_off = b_off + SSD_GROUPS
    cw_b_off = SSD_D_INNER // N
    cw_c_off = cw_b_off + SSD_GROUPS

    def pad_heads(v):
        return jnp.zeros((LANES,), F32).at[:SSD_HEADS].set(v)

    dtb = pad_heads(dt_bias)
    alog = pad_heads(a_log)
    conv_b2 = conv_b.reshape(1, SSD_CONV_CH)
    d_chan = jnp.repeat(d_skip, SSD_HEAD_DIM).reshape(1, SSD_D_INNER)
    row = lambda b, g, c: b * nc + c
    in_specs = [
        pl.BlockSpec((L, W), lambda b, g, c: (row(b, g, c), g)),
        pl.BlockSpec((L, W), lambda b, g, c: (row(b, g, c), x_off + g)),
        pl.BlockSpec((L, N), lambda b, g, c: (row(b, g, c), b_off + g)),
        pl.BlockSpec((L, N), lambda b, g, c: (row(b, g, c), c_off + g)),
        pl.BlockSpec((L, LANES), lambda b, g, c: (row(b, g, c), 0)),
        pl.BlockSpec((LANES, L), lambda b, g, c: (0, row(b, g, c))),
        pl.BlockSpec((SSD_CONV, W), lambda b, g, c: (0, g)),
        pl.BlockSpec((SSD_CONV, N), lambda b, g, c: (0, cw_b_off + g)),
        pl.BlockSpec((SSD_CONV, N), lambda b, g, c: (0, cw_c_off + g)),
        pl.BlockSpec((1, W), lambda b, g, c: (0, g)),
        pl.BlockSpec((1, N), lambda b, g, c: (0, cw_b_off + g)),
        pl.BlockSpec((1, N), lambda b, g, c: (0, cw_c_off + g)),
        pl.BlockSpec((1, LANES), lambda b, g, c: (0, 0)),
        pl.BlockSpec((1, LANES), lambda b, g, c: (0, 0)),
        pl.BlockSpec((LANES, 1), lambda b, g, c: (0, 0)),
        pl.BlockSpec((LANES, 1), lambda b, g, c: (0, 0)),
        pl.BlockSpec((1, W), lambda b, g, c: (0, g)),
        pl.BlockSpec((1, W), lambda b, g, c: (0, g)),
    ]
    return pl.pallas_call(
        _ssd_kernel, name="ssd_scan", grid=(BATCH, SSD_GROUPS, nc),
        in_specs=in_specs,
        out_specs=pl.BlockSpec((L, W), lambda b, g, c: (row(b, g, c), g)),
        out_shape=jax.ShapeDtypeStruct((TOKENS, SSD_D_INNER), BF16),
        scratch_shapes=[pltpu.VMEM((L + 8, W), F32), pltpu.VMEM((L + 8, N), F32),
                        pltpu.VMEM((L + 8, N), F32), pltpu.VMEM((LANES, L), F32),
                        pltpu.VMEM((N, W), F32)],
        compiler_params=_cparams(("parallel", "parallel", "arbitrary")),
    )(zx, zx, zx, zx, dt_small, dt_small_t, conv_w, conv_w, conv_w, conv_b2, conv_b2, conv_b2,
      dtb.reshape(1, LANES), alog.reshape(1, LANES), dtb.reshape(LANES, 1), alog.reshape(LANES, 1),
      d_chan, norm_g.reshape(1, SSD_D_INNER))


def _gather_rows(src_hbm, idx_ref, base, dst, sem, n_rows):
    def body(r, carry):
        tok = idx_ref[base + r]
        pltpu.make_async_copy(src_hbm.at[pl.ds(tok, 1), :], dst.at[pl.ds(r, 1), :], sem).start()
        return carry

    lax.fori_loop(0, n_rows, body, 0, unroll=GATHER_UNROLL)


def _moe_ffn_kernel(te_ref, tok_ref, nact_ref, x_hbm, g_ref, wg_ref, wu_ref, wd_ref, y_ref,
                    buf, sem, wg_bf, wu_bf, wd_bf):
    i = pl.program_id(0)
    n_act = nact_ref[0]
    tm = MOE_TM
    slot = i % 2

    def issue(tile, s):
        _gather_rows(x_hbm, tok_ref, tile * tm, buf.at[s], sem.at[s], tm)

    @pl.when((i == 0) & (n_act > 0))
    def _():
        issue(0, 0)

    @pl.when(i + 1 < n_act)
    def _():
        issue(i + 1, 1 - slot)

    e_cur = te_ref[i]
    e_prev = te_ref[jnp.maximum(i - 1, 0)]

    @pl.when((i == 0) | (e_cur != e_prev))
    def _():
        wg_bf[...] = wg_ref[...].astype(BF16)
        wu_bf[...] = wu_ref[...].astype(BF16)
        wd_bf[...] = wd_ref[...].astype(BF16)

    @pl.when(i < n_act)
    def _():
        pltpu.make_async_copy(x_hbm.at[pl.ds(0, tm), :], buf.at[slot], sem.at[slot]).wait()
        h = _rms(buf[slot], g_ref[...]).astype(BF16)
        act = jax.nn.silu(_dot(h, wg_bf[...])) * _dot(h, wu_bf[...])
        y_ref[...] = _dot(act.astype(BF16), wd_bf[...])

    @pl.when(i >= n_act)
    def _():
        y_ref[...] = jnp.zeros(y_ref.shape, F32)


def _moe_ffn(x, g, w_gate, w_up, w_down, layer, tile_expert, slot_token, n_active):
    grid_spec = pltpu.PrefetchScalarGridSpec(
        num_scalar_prefetch=3, grid=(MOE_TILES,),
        in_specs=[pl.BlockSpec(memory_space=pl.ANY),
                  pl.BlockSpec((1, D_MODEL), lambda i, te, tok, na: (0, 0)),
                  pl.BlockSpec((None, None, D_MODEL, MOE_D_FF), lambda i, te, tok, na: (layer, te[i], 0, 0)),
                  pl.BlockSpec((None, None, D_MODEL, MOE_D_FF), lambda i, te, tok, na: (layer, te[i], 0, 0)),
                  pl.BlockSpec((None, None, MOE_D_FF, D_MODEL), lambda i, te, tok, na: (layer, te[i], 0, 0))],
        out_specs=pl.BlockSpec((MOE_TM, D_MODEL), lambda i, te, tok, na: (i, 0)),
        scratch_shapes=[pltpu.VMEM((2, MOE_TM, D_MODEL), F32), pltpu.SemaphoreType.DMA((2,)),
                        pltpu.VMEM((D_MODEL, MOE_D_FF), BF16), pltpu.VMEM((D_MODEL, MOE_D_FF), BF16),
                        pltpu.VMEM((MOE_D_FF, D_MODEL), BF16)])
    return pl.pallas_call(
        _moe_ffn_kernel, name="moe_ffn", grid_spec=grid_spec,
        out_shape=jax.ShapeDtypeStruct((MOE_ROWS, D_MODEL), F32),
        compiler_params=_cparams(("arbitrary",)),
    )(tile_expert, slot_token, n_active, x, g.reshape(1, D_MODEL), w_gate, w_up, w_down)


def _moe_combine_kernel(pos_ref, x_ref, cw_ref, g_ref, y_hbm, o_ref, buf, sem, *, final_norm):
    i = pl.program_id(0)
    n = pl.num_programs(0)
    tm = CMB_TM
    slot = i % 2

    def issue(tile, s):
        for k in range(MOE_TOPK):
            _gather_rows(y_hbm, pos_ref, (k * (TOKENS // tm) + tile) * tm, buf.at[s, k], sem.at[s], tm)

    @pl.when(i == 0)
    def _():
        issue(0, 0)

    @pl.when(i + 1 < n)
    def _():
        issue(i + 1, 1 - slot)

    for k in range(MOE_TOPK):
        pltpu.make_async_copy(y_hbm.at[pl.ds(0, tm), :], buf.at[slot, k], sem.at[slot]).wait()
    cw = cw_ref[...]
    out = x_ref[...] + cw[:, 0:1] * buf[slot, 0] + cw[:, 1:2] * buf[slot, 1]
    if final_norm:
        out = _rms(out, g_ref[...])
    o_ref[...] = out


def _moe_combine(x, cw, y_sorted, pos_kmajor, g_final, final_norm):
    grid_spec = pltpu.PrefetchScalarGridSpec(
        num_scalar_prefetch=1, grid=(TOKENS // CMB_TM,),
        in_specs=[pl.BlockSpec((CMB_TM, D_MODEL), lambda i, pos: (i, 0)),
                  pl.BlockSpec((CMB_TM, LANES), lambda i, pos: (i, 0)),
                  pl.BlockSpec((1, D_MODEL), lambda i, pos: (0, 0)),
                  pl.BlockSpec(memory_space=pl.ANY)],
        out_specs=pl.BlockSpec((CMB_TM, D_MODEL), lambda i, pos: (i, 0)),
        scratch_shapes=[pltpu.VMEM((2, MOE_TOPK, CMB_TM, D_MODEL), F32), pltpu.SemaphoreType.DMA((2,))])
    return pl.pallas_call(
        functools.partial(_moe_combine_kernel, final_norm=final_norm), name="moe_combine", grid_spec=grid_spec,
        out_shape=jax.ShapeDtypeStruct((TOKENS, D_MODEL), F32),
        compiler_params=_cparams(("arbitrary",)),
    )(pos_kmajor, x, cw, g_final.reshape(1, D_MODEL), y_sorted)


def _moe_plan(eid):
    e = eid[:, :MOE_TOPK].reshape(-1)
    n_pairs = e.shape[0]
    onehot = (e[:, None] == jnp.arange(MOE_EXPERTS, dtype=I32)[None, :]).astype(I32)
    csum = jnp.cumsum(onehot, axis=0)
    counts = csum[-1]
    rank = jnp.take_along_axis(csum, e[:, None], axis=1)[:, 0] - 1
    padded = ((counts + MOE_TM - 1) // MOE_TM) * MOE_TM
    g_end = jnp.cumsum(padded)
    g_start = g_end - padded
    pos = g_start[e] + rank
    n_active = (g_end[-1] // MOE_TM).astype(I32)
    tile_start = jnp.arange(MOE_TILES, dtype=I32) * MOE_TM
    te = jnp.sum((g_end[None, :] <= tile_start[:, None]).astype(I32), axis=1)
    last = jnp.max(jnp.where(counts > 0, jnp.arange(MOE_EXPERTS, dtype=I32), 0))
    tile_expert = jnp.minimum(te, last)
    slot_token = jnp.zeros((MOE_ROWS,), I32).at[pos].set(jnp.arange(n_pairs, dtype=I32) // MOE_TOPK)
    pos_kmajor = pos.reshape(TOKENS, MOE_TOPK).T.reshape(-1)
    return tile_expert, slot_token, n_active.reshape(1), pos_kmajor


def _hier_moe_add(x, ln_g, w_group, b_group, w_expert, b_expert, w_gate, w_up, w_down, layer,
                  g_final, final_norm):
    eid, cw = _router(x, ln_g, w_group, b_group, w_expert, b_expert)
    tile_expert, slot_token, n_active, pos_kmajor = _moe_plan(eid)
    y_sorted = _moe_ffn(x, ln_g, w_gate, w_up, w_down, layer, tile_expert, slot_token, n_active)
    return _moe_combine(x, cw, y_sorted, pos_kmajor, g_final, final_norm)


def _rope_tables():
    pos = jnp.arange(SEQ, dtype=F32)
    inv = 1.0 / (ROPE_THETA ** (jnp.arange(0, NSA_HEAD_DIM, 2, dtype=F32) / NSA_HEAD_DIM))
    ang = pos[:, None] * inv[None, :]
    cos, sin = jnp.cos(ang), jnp.sin(ang)
    return jnp.concatenate([cos, cos], axis=1), jnp.concatenate([-sin, sin], axis=1)


def _nsa_mixer_add(x, ln_g, w_in, cmp_pe, cmp_w1, cmp_w2, w_out):
    hn, _, g_lin_t = _norm_small(x, ln_g, w_in[:, NSA_MAIN:], transposed=True)
    cos_full, sin_signed = _rope_tables()
    heads = _nsa_inproj(hn, w_in, cos_full, sin_signed)
    first_c = NSA_HEADS
    kv_c = heads[first_c:first_c + 2 * NSA_KV_HEADS]
    kv_chunks = kv_c.reshape(2, NSA_KV_HEADS, BATCH, SEQ // CMP_STRIDE, CMP_STRIDE * NSA_HEAD_DIM)
    kc_vc = _compress(kv_chunks, cmp_pe, cmp_w1, cmp_w2)
    gates_t = g_lin_t[:NSA_GATES].reshape(NSA_KV_HEADS, 3 * NSA_Q_PER_KV, TOKENS)
    o = _nsa_attention(heads, kc_vc, gates_t)
    return _matmul_resid(o, w_out, x, 512)


def _ssd_mixer_add(x, ln_g, w_in, conv_w, conv_b, dt_bias, a_log, d_skip, norm_g, w_out):
    hn, dt_small, dt_small_t = _norm_small(x, ln_g, w_in[:, SSD_MAIN:], transposed=True)
    zx = _matmul(hn, w_in, SSD_MAIN, 1024, BF16)
    y = _ssd_scan(zx, dt_small, dt_small_t, conv_w, conv_b, dt_bias, a_log, d_skip, norm_g)
    return _matmul_resid(y, w_out, x, 512)


def kernel(x, ln_mix, ln_ffn, ln_final, nsa_w_in, nsa_cmp_pe, nsa_cmp_w1, nsa_cmp_w2, nsa_w_out,
           ssd_w_in, ssd_conv_w, ssd_conv_b, ssd_dt_bias, ssd_a_log, ssd_d, ssd_norm, ssd_w_out,
           moe_w_group, moe_b_group, moe_w_expert, moe_b_expert, moe_w_gate, moe_w_up, moe_w_down):
    h = x.reshape(TOKENS, D_MODEL)
    for i in range(DEPTH):
        j = i // N_MIXERS
        if i % N_MIXERS == 0:
            h = _nsa_mixer_add(h, ln_mix[i], nsa_w_in[j], nsa_cmp_pe[j], nsa_cmp_w1[j], nsa_cmp_w2[j],
                               nsa_w_out[j])
        else:
            h = _ssd_mixer_add(h, ln_mix[i], ssd_w_in[j], ssd_conv_w[j], ssd_conv_b[j], ssd_dt_bias[j],
                               ssd_a_log[j], ssd_d[j], ssd_norm[j], ssd_w_out[j])
        h = _hier_moe_add(h, ln_ffn[i], moe_w_group[i], moe_b_group[i], moe_w_expert[i], moe_b_expert[i],
                          moe_w_gate, moe_w_up, moe_w_down, i, ln_final, i == DEPTH - 1)
    return h.reshape(BATCH, SEQ, D_MODEL)
```

```python
import functools

import jax
import jax.numpy as jnp
from jax import lax
from jax.experimental import pallas as pl
from jax.experimental.pallas import tpu as pltpu

F32 = jnp.float32
BF16 = jnp.bfloat16
I32 = jnp.int32

D_MODEL = 2048
BATCH = 4
SEQ = 2048
TOKENS = BATCH * SEQ
DEPTH = 2
N_MIXERS = 2
NORM_EPS = 1e-6
NEG_INF = -1e30
ROPE_THETA = 10000.0

NSA_HEADS = 16
NSA_KV_HEADS = 4
NSA_HEAD_DIM = D_MODEL // NSA_HEADS
NSA_Q_PER_KV = NSA_HEADS // NSA_KV_HEADS
CMP_BLOCK = 32
CMP_STRIDE = 16
CMP_HIDDEN = 256
N_CMP = (SEQ - CMP_BLOCK) // CMP_STRIDE + 1
SLC_BLOCK = 64
SLC_TOPK = 16
SLC_LOCAL = 2
SLC_FORCE = 1e4
N_SLC = SEQ // SLC_BLOCK
WINDOW = 512
NSA_QD = NSA_HEADS * NSA_HEAD_DIM
NSA_KVD = NSA_KV_HEADS * NSA_HEAD_DIM
NSA_MAIN = NSA_QD + 6 * NSA_KVD
NSA_GATES = 3 * NSA_HEADS

SSD_D_INNER = 2 * D_MODEL
SSD_HEAD_DIM = 64
SSD_HEADS = SSD_D_INNER // SSD_HEAD_DIM
SSD_GROUPS = 8
SSD_HEADS_PER_GROUP = SSD_HEADS // SSD_GROUPS
SSD_D_STATE = 128
SSD_CONV = 4
SSD_CHUNK = 128
SSD_GROUP_W = SSD_D_INNER // SSD_GROUPS
SSD_BC = SSD_GROUPS * SSD_D_STATE
SSD_CONV_CH = SSD_D_INNER + 2 * SSD_BC
SSD_MAIN = SSD_D_INNER + SSD_CONV_CH
SSD_FAC_PAD = 16

MOE_GROUPS = 4
MOE_EPG = 8
MOE_EXPERTS = MOE_GROUPS * MOE_EPG
MOE_TOPK = 2
MOE_D_FF = 512

LANES = 128
ROW_SLAB = D_MODEL // LANES
VMEM_LIMIT = 56 * 1024 * 1024

NORM_TM = 256
MM_TM = 512
ATT_TQ = 128
ATT_CHUNK_TILES = 4
ATT_ONES = 16
MOE_TM = 256
MOE_TILES = (TOKENS * MOE_TOPK) // MOE_TM + MOE_EXPERTS
MOE_ROWS = MOE_TILES * MOE_TM
CMB_TM = 128
GATHER_UNROLL = 8


def _cparams(sem):
    return pltpu.CompilerParams(dimension_semantics=sem, vmem_limit_bytes=VMEM_LIMIT)


def _split3(x):
    hi = x.astype(BF16)
    r1 = x - hi.astype(F32)
    mid = r1.astype(BF16)
    lo = (r1 - mid.astype(F32)).astype(BF16)
    return hi, mid, lo


def _dot(a, b):
    return jnp.dot(a, b, preferred_element_type=F32)


def _dot_nt(a, b):
    return lax.dot_general(a, b, (((1,), (1,)), ((), ())), preferred_element_type=F32)


def _dot_tn(a, b):
    return lax.dot_general(a, b, (((0,), (0,)), ((), ())), preferred_element_type=F32)


def _dot_split_lhs(x, m_bf16):
    hi, mid, lo = _split3(x)
    return _dot(hi, m_bf16) + _dot(mid, m_bf16) + _dot(lo, m_bf16)


def _dot_split_rhs(m_bf16, x):
    hi, mid, lo = _split3(x)
    return _dot(m_bf16, hi) + _dot(m_bf16, mid) + _dot(m_bf16, lo)


def _dot_x3(a, w):
    a_hi = a.astype(BF16)
    a_lo = (a - a_hi.astype(F32)).astype(BF16)
    w_hi = w.astype(BF16)
    w_lo = (w - w_hi.astype(F32)).astype(BF16)
    return _dot(a_hi, w_hi) + _dot(a_hi, w_lo) + _dot(a_lo, w_hi)


def _dot_x3_nt(a, w):
    a_hi = a.astype(BF16)
    a_lo = (a - a_hi.astype(F32)).astype(BF16)
    w_hi = w.astype(BF16)
    w_lo = (w - w_hi.astype(F32)).astype(BF16)
    return _dot_nt(a_hi, w_hi) + _dot_nt(a_hi, w_lo) + _dot_nt(a_lo, w_hi)


def _rms(x, g):
    y = x * lax.rsqrt(jnp.mean(x * x, axis=-1, keepdims=True) + NORM_EPS)
    return y * g


def _norm_small_kernel(x_ref, g_ref, ws_ref, hn_ref, small_ref):
    y = _rms(x_ref[...], g_ref[...])
    hn_ref[...] = y.astype(BF16)
    small_ref[...] = _dot_x3(y, ws_ref[...])


def _norm_small_t_kernel(x_ref, g_ref, ws_ref, wst_ref, hn_ref, small_ref, small_t_ref):
    y = _rms(x_ref[...], g_ref[...])
    hn_ref[...] = y.astype(BF16)
    small_ref[...] = _dot_x3(y, ws_ref[...])
    small_t_ref[...] = _dot_x3_nt(wst_ref[...], y)


def _norm_small(x, g, w_small, transposed=False):
    n = w_small.shape[1]
    ws = jnp.zeros((D_MODEL, LANES), F32).at[:, :n].set(w_small)
    grid = (TOKENS // NORM_TM,)
    x_spec = pl.BlockSpec((NORM_TM, D_MODEL), lambda i: (i, 0))
    g_spec = pl.BlockSpec((1, D_MODEL), lambda i: (0, 0))
    w_spec = pl.BlockSpec((D_MODEL, LANES), lambda i: (0, 0))
    hn_spec = pl.BlockSpec((NORM_TM, D_MODEL), lambda i: (i, 0))
    sm_spec = pl.BlockSpec((NORM_TM, LANES), lambda i: (i, 0))
    hn_shape = jax.ShapeDtypeStruct((TOKENS, D_MODEL), BF16)
    sm_shape = jax.ShapeDtypeStruct((TOKENS, LANES), F32)
    if not transposed:
        return pl.pallas_call(
            _norm_small_kernel, name="norm_small", grid=grid,
            in_specs=[x_spec, g_spec, w_spec],
            out_specs=[hn_spec, sm_spec],
            out_shape=[hn_shape, sm_shape],
            compiler_params=_cparams(("parallel",)),
        )(x, g.reshape(1, D_MODEL), ws)
    wt_spec = pl.BlockSpec((LANES, D_MODEL), lambda i: (0, 0))
    smt_spec = pl.BlockSpec((LANES, NORM_TM), lambda i: (0, i))
    smt_shape = jax.ShapeDtypeStruct((LANES, TOKENS), F32)
    return pl.pallas_call(
        _norm_small_t_kernel, name="norm_small_t", grid=grid,
        in_specs=[x_spec, g_spec, w_spec, wt_spec],
        out_specs=[hn_spec, sm_spec, smt_spec],
        out_shape=[hn_shape, sm_shape, smt_shape],
        compiler_params=_cparams(("parallel",)),
    )(x, g.reshape(1, D_MODEL), ws, ws.T)


def _router_kernel(x_ref, g_ref, ws_ref, b_ref, eid_ref, cw_ref, hn_ref):
    y = _rms(x_ref[...], g_ref[...])
    _to_slabs(hn_ref, y)
    logits = _dot_x3(y, ws_ref[...]) + b_ref[...]
    lane = lax.broadcasted_iota(I32, logits.shape, 1)
    big = jnp.int32(LANES)
    neg = -jnp.inf
    gl = jnp.where(lane < MOE_GROUPS, logits, neg)
    gmax = jnp.max(gl, axis=-1, keepdims=True)
    gsum = jnp.sum(jnp.exp(gl - gmax), axis=-1, keepdims=True)
    g_w = 1.0 / gsum
    g_sel = jnp.min(jnp.where(gl == gmax, lane, big), axis=-1, keepdims=True)
    lo = MOE_GROUPS + g_sel * MOE_EPG
    el = jnp.where((lane >= lo) & (lane < lo + MOE_EPG), logits, neg)
    v1 = jnp.max(el, axis=-1, keepdims=True)
    i1 = jnp.min(jnp.where(el == v1, lane, big), axis=-1, keepdims=True)
    el2 = jnp.where(lane == i1, neg, el)
    v2 = jnp.max(el2, axis=-1, keepdims=True)
    i2 = jnp.min(jnp.where(el2 == v2, lane, big), axis=-1, keepdims=True)
    e2 = jnp.exp(v2 - v1)
    den = 1.0 + e2
    w1 = (1.0 / den) * g_w
    w2 = (e2 / den) * g_w
    eid_ref[...] = jnp.where(lane == 0, i1 - MOE_GROUPS, jnp.where(lane == 1, i2 - MOE_GROUPS, 0))
    cw_ref[...] = jnp.where(lane == 0, w1, jnp.where(lane == 1, w2, 0.0))


def _router(x, g, w_group, b_group, w_expert, b_expert):
    n = MOE_GROUPS + MOE_EXPERTS
    ws = jnp.zeros((D_MODEL, LANES), F32).at[:, :n].set(jnp.concatenate([w_group, w_expert], axis=1))
    bs = jnp.zeros((1, LANES), F32).at[0, :n].set(jnp.concatenate([b_group, b_expert]))
    return pl.pallas_call(
        _router_kernel, name="router", grid=(TOKENS // NORM_TM,),
        in_specs=[pl.BlockSpec((NORM_TM, D_MODEL), lambda i: (i, 0)),
                  pl.BlockSpec((1, D_MODEL), lambda i: (0, 0)),
                  pl.BlockSpec((D_MODEL, LANES), lambda i: (0, 0)),
                  pl.BlockSpec((1, LANES), lambda i: (0, 0))],
        out_specs=[pl.BlockSpec((NORM_TM, LANES), lambda i: (i, 0)),
                   pl.BlockSpec((NORM_TM, LANES), lambda i: (i, 0)),
                   pl.BlockSpec((NORM_TM, ROW_SLAB, LANES), lambda i: (i, 0, 0))],
        out_shape=[jax.ShapeDtypeStruct((TOKENS, LANES), I32),
                   jax.ShapeDtypeStruct((TOKENS, LANES), F32),
                   jax.ShapeDtypeStruct((TOKENS, ROW_SLAB, LANES), F32)],
        compiler_params=_cparams(("parallel",)),
    )(x, g.reshape(1, D_MODEL), ws, bs)


def _inproj_heads_kernel(a_ref, w_ref, cos_ref, sin_ref, o_ref, wbf_ref):
    j = pl.program_id(0)

    @pl.when(pl.program_id(1) == 0)
    def _():
        wbf_ref[...] = w_ref[...].astype(BF16)

    acc = _dot(a_ref[...], wbf_ref[...])
    heads = acc.shape[1] // NSA_HEAD_DIM
    is_rope = (j < NSA_QD // acc.shape[1]) | (j % 2 == 0)

    @pl.when(is_rope)
    def _():
        c = cos_ref[...]
        s = sin_ref[...]
        for h in range(heads):
            xh = acc[:, h * NSA_HEAD_DIM:(h + 1) * NSA_HEAD_DIM]
            o_ref[h] = (xh * c + pltpu.roll(xh, NSA_HEAD_DIM // 2, 1) * s).astype(BF16)

    @pl.when(jnp.logical_not(is_rope))
    def _():
        for h in range(heads):
            o_ref[h] = acc[:, h * NSA_HEAD_DIM:(h + 1) * NSA_HEAD_DIM].astype(BF16)


def _nsa_inproj(hn, w_in, cos_full, sin_signed):
    tn = NSA_KVD
    n_heads_tile = tn // NSA_HEAD_DIM
    s_tiles = SEQ // MM_TM
    return pl.pallas_call(
        _inproj_heads_kernel, name="nsa_inproj", grid=(NSA_MAIN // tn, TOKENS // MM_TM),
        in_specs=[pl.BlockSpec((MM_TM, D_MODEL), lambda j, i: (i, 0)),
                  pl.BlockSpec((D_MODEL, tn), lambda j, i: (0, j)),
                  pl.BlockSpec((MM_TM, NSA_HEAD_DIM), lambda j, i: (i % s_tiles, 0)),
                  pl.BlockSpec((MM_TM, NSA_HEAD_DIM), lambda j, i: (i % s_tiles, 0))],
        out_specs=pl.BlockSpec((n_heads_tile, MM_TM, NSA_HEAD_DIM), lambda j, i: (j, i, 0)),
        out_shape=jax.ShapeDtypeStruct((NSA_MAIN // NSA_HEAD_DIM, TOKENS, NSA_HEAD_DIM), BF16),
        scratch_shapes=[pltpu.VMEM((D_MODEL, tn), BF16)],
        compiler_params=_cparams(("arbitrary", "arbitrary")),
    )(hn, w_in, cos_full, sin_signed)


def _matmul_resid_kernel(a_ref, w_ref, r_ref, o_ref, wbf_ref):
    @pl.when(pl.program_id(1) == 0)
    def _():
        wbf_ref[...] = w_ref[...].astype(BF16)

    o_ref[...] = r_ref[...] + _dot(a_ref[...], wbf_ref[...])


def _matmul_tiles_kernel(a_ref, w_ref, o_ref, wbf_ref):
    @pl.when(pl.program_id(1) == 0)
    def _():
        wbf_ref[...] = w_ref[...].astype(BF16)

    acc = _dot(a_ref[...], wbf_ref[...])
    for t in range(acc.shape[1] // LANES):
        o_ref[t] = acc[:, t * LANES:(t + 1) * LANES].astype(o_ref.dtype)


def _matmul_tiles(a, w, n_cols, tn):
    k = a.shape[1]
    return pl.pallas_call(
        _matmul_tiles_kernel, name="matmul_tiles", grid=(n_cols // tn, TOKENS // MM_TM),
        in_specs=[pl.BlockSpec((MM_TM, k), lambda j, i: (i, 0)),
                  pl.BlockSpec((k, tn), lambda j, i: (0, j))],
        out_specs=pl.BlockSpec((tn // LANES, MM_TM, LANES), lambda j, i: (j, i, 0)),
        out_shape=jax.ShapeDtypeStruct((n_cols // LANES, TOKENS, LANES), BF16),
        scratch_shapes=[pltpu.VMEM((k, tn), BF16)],
        compiler_params=_cparams(("arbitrary", "arbitrary")),
    )(a, w)


def _matmul_resid_tiles_kernel(a_ref, w_ref, r_ref, o_ref, wbf_ref):
    @pl.when(pl.program_id(1) == 0)
    def _():
        wbf_ref[...] = w_ref[...].astype(BF16)

    a = jnp.concatenate([a_ref[t] for t in range(a_ref.shape[0])], axis=1)
    o_ref[...] = r_ref[...] + _dot(a, wbf_ref[...])


def _matmul_resid_tiles(a_tiles, w, resid, tn):
    n_kt = a_tiles.shape[0]
    k = n_kt * LANES
    n = w.shape[1]
    return pl.pallas_call(
        _matmul_resid_tiles_kernel, name="matmul_resid_tiles", grid=(n // tn, TOKENS // MM_TM),
        in_specs=[pl.BlockSpec((n_kt, MM_TM, LANES), lambda j, i: (0, i, 0)),
                  pl.BlockSpec((k, tn), lambda j, i: (0, j)),
                  pl.BlockSpec((MM_TM, tn), lambda j, i: (i, j))],
        out_specs=pl.BlockSpec((MM_TM, tn), lambda j, i: (i, j)),
        out_shape=jax.ShapeDtypeStruct((TOKENS, n), F32),
        scratch_shapes=[pltpu.VMEM((k, tn), BF16)],
        compiler_params=_cparams(("arbitrary", "arbitrary")),
    )(a_tiles, w, resid)


def _matmul_resid(a, w, resid, tn):
    k = a.shape[1]
    n = w.shape[1]
    return pl.pallas_call(
        _matmul_resid_kernel, name="matmul_resid", grid=(n // tn, TOKENS // MM_TM),
        in_specs=[pl.BlockSpec((MM_TM, k), lambda j, i: (i, 0)),
                  pl.BlockSpec((k, tn), lambda j, i: (0, j)),
                  pl.BlockSpec((MM_TM, tn), lambda j, i: (i, j))],
        out_specs=pl.BlockSpec((MM_TM, tn), lambda j, i: (i, j)),
        out_shape=jax.ShapeDtypeStruct((TOKENS, n), F32),
        scratch_shapes=[pltpu.VMEM((k, tn), BF16)],
        compiler_params=_cparams(("arbitrary", "arbitrary")),
    )(a, w, resid)


def _compress_kernel(x_ref, pe_ref, w1_ref, w2_ref, o_ref):
    half = CMP_STRIDE * NSA_HEAD_DIM
    x = x_ref[0, 0, 0]
    w1 = w1_ref[0].astype(BF16)
    top = _dot(x, w1[:half])
    bot = _dot(x, w1[half:])
    pe = jnp.broadcast_to(pe_ref[0], (8, 2 * half)).astype(BF16)
    pe_bias = _dot(pe, w1)[0:1]
    hid = top + pltpu.roll(bot, bot.shape[0] - 1, 0) + pe_bias
    act = jax.nn.gelu(hid)
    o_ref[0, 0, 0] = _dot(act.astype(BF16), w2_ref[0].astype(BF16))


def _compress(kv_chunks, pe, w1, w2):
    n_chunk = SEQ // CMP_STRIDE
    feat = CMP_STRIDE * NSA_HEAD_DIM
    return pl.pallas_call(
        _compress_kernel, name="nsa_compress", grid=(2, NSA_KV_HEADS, BATCH),
        in_specs=[pl.BlockSpec((1, 1, 1, n_chunk, feat), lambda a, g, b: (a, g, b, 0, 0)),
                  pl.BlockSpec((1, 1, 2 * feat), lambda a, g, b: (a, 0, 0)),
                  pl.BlockSpec((1, 2 * feat, CMP_HIDDEN), lambda a, g, b: (a, 0, 0)),
                  pl.BlockSpec((1, CMP_HIDDEN, NSA_HEAD_DIM), lambda a, g, b: (a, 0, 0))],
        out_specs=pl.BlockSpec((1, 1, 1, n_chunk, NSA_HEAD_DIM), lambda a, g, b: (a, g, b, 0, 0)),
        out_shape=jax.ShapeDtypeStruct((2, NSA_KV_HEADS, BATCH, n_chunk, NSA_HEAD_DIM), F32),
        compiler_params=_cparams(("parallel", "parallel", "parallel")),
    )(kv_chunks, pe.reshape(2, 1, 2 * feat), w1, w2)


def _nsa_attn_kernel(q_ref, ks_ref, vs_ref, kw_ref, vw_ref, kc_ref, vc_ref, gate_ref, o_ref,
                     vst_scr, vwt_scr, sel_scr, acc_scr):
    qi = pl.program_id(2)
    tq = ATT_TQ
    tk = ATT_TQ
    dh = NSA_HEAD_DIM
    r_heads = NSA_Q_PER_KV
    n_kt = SEQ // tk
    scale = dh ** -0.5

    @pl.when(qi == 0)
    def _():
        ones = jnp.ones((ATT_ONES, tk), BF16)
        for kt in range(n_kt):
            rows = slice(kt * tk, (kt + 1) * tk)
            vst_scr[kt, 0:dh, :] = vs_ref[0, rows, :].astype(F32).T.astype(BF16)
            vwt_scr[kt, 0:dh, :] = vw_ref[0, rows, :].astype(F32).T.astype(BF16)
            vst_scr[kt, dh:dh + ATT_ONES, :] = ones
            vwt_scr[kt, dh:dh + ATT_ONES, :] = ones

    q_t = jnp.concatenate([q_ref[r].astype(F32).T for r in range(r_heads)], axis=1).astype(BF16)
    sub = lax.broadcasted_iota(I32, (tk, tq), 0)
    t_pos = qi * tq + lax.broadcasted_iota(I32, (tk, tq), 1)

    def tile4(a):
        return jnp.concatenate([a] * r_heads, axis=1)

    kc = kc_ref[0, 0, 0].astype(BF16)
    vc = vc_ref[0, 0, 0].astype(BF16)
    ok_c = jnp.where(sub * CMP_STRIDE + CMP_BLOCK - 1 <= t_pos, jnp.where(sub < N_CMP, 1.0, 0.0), 0.0)
    ok_c4 = tile4(ok_c)
    s_c = _dot(kc, q_t) * scale + (ok_c4 - 1.0) * (-NEG_INF)
    e_c = jnp.exp(s_c - jnp.max(s_c, axis=0, keepdims=True))
    p_c = (e_c / jnp.sum(e_c, axis=0, keepdims=True)) * ok_c4
    o_cmp = _dot_tn(vc, p_c.astype(BF16))
    p_sum = p_c[:, 0:tq]
    for r in range(1, r_heads):
        p_sum = p_sum + p_c[:, r * tq:(r + 1) * tq]

    blk_row = lax.broadcasted_iota(I32, (LANES, LANES), 0)
    cmp_col = lax.broadcasted_iota(I32, (LANES, LANES), 1)
    s_start = blk_row * SLC_BLOCK
    c_start = cmp_col * CMP_STRIDE
    ov_t = jnp.maximum(jnp.minimum(c_start + CMP_BLOCK, s_start + SLC_BLOCK)
                       - jnp.maximum(c_start, s_start), 0).astype(F32) / CMP_BLOCK
    ov_t = jnp.where(blk_row < N_SLC, ov_t, 0.0).astype(BF16)
    imp = _dot_split_rhs(ov_t, p_sum)[0:N_SLC]
    j_blk = lax.broadcasted_iota(I32, (N_SLC, tq), 0)
    dist = (qi * tq + lax.broadcasted_iota(I32, (N_SLC, tq), 1)) // SLC_BLOCK - j_blk
    imp = jnp.where(j_blk == 0, SLC_FORCE, jnp.where(dist < 0, imp, jnp.where(dist < SLC_LOCAL, SLC_FORCE, imp)))
    imp = jnp.where(dist >= 0, imp, -jnp.inf)
    cnt = jnp.zeros((N_SLC, tq), I32)
    for k in range(N_SLC):
        row_k = imp[k:k + 1, :]
        tie = jnp.where(j_blk > k, 1, 0)
        cnt = cnt + jnp.where(row_k > imp, 1, jnp.where(row_k == imp, tie, 0))
    sel = jnp.where(cnt < min(SLC_TOPK, N_SLC), 1.0, 0.0)
    for j in range(N_SLC):
        sel_scr[8 * j:8 * j + 8, :] = jnp.broadcast_to(sel[j:j + 1, :], (8, tq))

    def scores(k, okf):
        return _dot(k, q_t) * scale + (okf - 1.0) * (-NEG_INF)

    def weighted_values(vt_scr, kt0, pr, n_tiles):
        out = None
        for u in range(n_tiles):
            term = _dot(vt_scr[kt0 + u], pr[u * tk:(u + 1) * tk].astype(BF16))
            out = term if out is None else out + term
        return out

    n_ct = ATT_CHUNK_TILES
    ck = n_ct * tk
    sub_c = lax.broadcasted_iota(I32, (ck, tq), 0)
    t_pos_c = qi * tq + lax.broadcasted_iota(I32, (ck, tq), 1)
    blocks_per_chunk = ck // SLC_BLOCK
    acc_scr[...] = jnp.zeros(acc_scr.shape, F32)

    def slc_body(c, m_old):
        start = pl.multiple_of(c * ck, ck)
        k = ks_ref[0, pl.ds(start, ck), :]
        rows8 = sel_scr[pl.ds(pl.multiple_of(c * (8 * blocks_per_chunk), 8 * blocks_per_chunk),
                              8 * blocks_per_chunk), :]
        picked = jnp.concatenate(
            [rows8[8 * u:8 * u + 8] for u in range(blocks_per_chunk) for _ in range(SLC_BLOCK // 8)], axis=0)
        okf = tile4(jnp.where(start + sub_c <= t_pos_c, picked, 0.0))
        sc = scores(k, okf)
        m_new = jnp.maximum(m_old, jnp.max(sc, axis=0, keepdims=True))
        alpha = jnp.exp(m_old - m_new)
        pr = jnp.exp(sc - m_new) * okf
        acc_scr[...] = alpha * acc_scr[...] + weighted_values(vst_scr, c * n_ct, pr, n_ct)
        return m_new

    lax.fori_loop(0, (qi + n_ct) // n_ct, slc_body, jnp.full((1, r_heads * tq), NEG_INF, F32))
    acc = acc_scr[...]
    o_slc = acc[0:dh] / acc[dh:dh + 1]

    n_wt = WINDOW // tk + 1
    kt0 = jnp.maximum(qi - WINDOW // tk, 0)
    w_start = pl.multiple_of(kt0 * tk, tk)
    key_w = w_start + lax.broadcasted_iota(I32, (n_wt * tk, tq), 0)
    t_pos_w = qi * tq + lax.broadcasted_iota(I32, (n_wt * tk, tq), 1)
    okf_w = tile4(jnp.where(key_w <= t_pos_w, jnp.where(key_w > t_pos_w - WINDOW, 1.0, 0.0), 0.0))
    sc_w = scores(kw_ref[0, pl.ds(w_start, n_wt * tk), :], okf_w)
    pr_w = jnp.exp(sc_w - jnp.max(sc_w, axis=0, keepdims=True)) * okf_w
    acc_w = weighted_values(vwt_scr, kt0, pr_w, n_wt)
    o_win = acc_w[0:dh] / acc_w[dh:dh + 1]

    gate = jax.nn.sigmoid(gate_ref[0])
    for r in range(r_heads):
        cols = slice(r * tq, (r + 1) * tq)
        o = (gate[3 * r:3 * r + 1] * o_cmp[:, cols] + gate[3 * r + 1:3 * r + 2] * o_slc[:, cols]
             + gate[3 * r + 2:3 * r + 3] * o_win[:, cols])
        o_ref[:, r * dh:(r + 1) * dh] = o.T.astype(BF16)


def _nsa_attention(heads, kc_vc, gates):
    tq = ATT_TQ
    nq = SEQ // tq
    r = NSA_Q_PER_KV
    g_heads = NSA_KV_HEADS
    q_spec = pl.BlockSpec((r, tq, NSA_HEAD_DIM), lambda b, g, i: (g, b * nq + i, 0))

    def kv_spec(first_head):
        return pl.BlockSpec((1, SEQ, NSA_HEAD_DIM), lambda b, g, i: (first_head + g, b, 0))

    first = NSA_HEADS
    specs = [q_spec,
             kv_spec(first + 2 * g_heads), kv_spec(first + 3 * g_heads),
             kv_spec(first + 4 * g_heads), kv_spec(first + 5 * g_heads),
             pl.BlockSpec((1, 1, 1, SEQ // CMP_STRIDE, NSA_HEAD_DIM), lambda b, g, i: (0, g, b, 0, 0)),
             pl.BlockSpec((1, 1, 1, SEQ // CMP_STRIDE, NSA_HEAD_DIM), lambda b, g, i: (1, g, b, 0, 0)),
             pl.BlockSpec((1, 3 * r, tq), lambda b, g, i: (g, 0, b * nq + i))]
    vt_shape = (SEQ // tq, NSA_HEAD_DIM + ATT_ONES, tq)
    return pl.pallas_call(
        _nsa_attn_kernel, name="nsa_attn", grid=(BATCH, g_heads, nq),
        in_specs=specs,
        out_specs=pl.BlockSpec((tq, r * NSA_HEAD_DIM), lambda b, g, i: (b * nq + i, g)),
        out_shape=jax.ShapeDtypeStruct((TOKENS, NSA_QD), BF16),
        scratch_shapes=[pltpu.VMEM(vt_shape, BF16), pltpu.VMEM(vt_shape, BF16),
                        pltpu.VMEM((8 * N_SLC, tq), F32),
                        pltpu.VMEM((NSA_HEAD_DIM + ATT_ONES, r * tq), F32)],
        compiler_params=_cparams(("arbitrary", "arbitrary", "arbitrary")),
    )(heads, heads, heads, heads, heads, kc_vc, kc_vc, gates)


def _ssd_chunk_kernel(zx_ref, dtc_ref, dtr_ref, cw_ref, cb_ref, dtb_c_ref, alog_c_ref, dtb_r_ref, alog_r_ref,
                      dskip_ref, ng_ref, shift_ref, echan_ref, ehead_ref, o_ref,
                      prev_scr, acum_r_scr, st_scr, fac_scr, acp_scr):
    chunk = pl.program_id(1)
    L = SSD_CHUNK
    W = SSD_GROUP_W
    hpg = SSD_HEADS_PER_GROUP
    n_xt = W // LANES
    x0 = SSD_D_INNER // LANES
    b0 = 2 * SSD_D_INNER // LANES
    c0 = b0 + SSD_GROUPS
    cb0 = SSD_D_INNER // LANES
    cc0 = cb0 + SSD_GROUPS

    @pl.when(chunk == 0)
    def _():
        prev_scr[...] = jnp.zeros(prev_scr.shape, BF16)
        st_scr[...] = jnp.zeros(st_scr.shape, F32)

    dt_c = jax.nn.softplus(dtc_ref[...] + dtb_c_ref[...])
    adt_c = dt_c * (-jnp.exp(alog_c_ref[...]))
    dt_r = jax.nn.softplus(dtr_ref[...] + dtb_r_ref[...])
    adt_r = dt_r * (-jnp.exp(alog_r_ref[...]))
    row = lax.broadcasted_iota(I32, (L, L), 0)
    col = lax.broadcasted_iota(I32, (L, L), 1)
    causal = row >= col
    tri = jnp.where(causal, 1.0, 0.0).astype(BF16)
    tri_t = jnp.where(col >= row, 1.0, 0.0).astype(BF16)
    acum_c = _dot_split_rhs(tri, adt_c)
    acum_r_scr[...] = _dot_split_lhs(adt_r, tri_t)
    a_last = acum_c[L - 1:L, :]
    fac = jnp.concatenate([dt_c, jnp.exp(acum_c), jnp.exp(a_last - acum_c),
                           jnp.broadcast_to(jnp.exp(a_last), (SSD_FAC_PAD, LANES))], axis=0)
    fac_hi = fac.astype(BF16)
    fac_scr[0] = fac_hi
    fac_scr[1] = (fac - fac_hi.astype(F32)).astype(BF16)
    for i, part in enumerate(_split3(acum_c)):
        acp_scr[i] = part
    lane_w = lax.broadcasted_iota(I32, (L, LANES), 1)
    first_half = lane_w < SSD_HEAD_DIM

    def tiles(ref, first, n):
        return jnp.concatenate([ref[first + q] for q in range(n)], axis=1)

    def group_body(g, carry):
        e_chan = echan_ref[g]
        ex = _dot(fac_scr[0], e_chan) + _dot(fac_scr[1], e_chan)
        dt_x, ea_x, sd_x, cd_x = ex[0:L], ex[L:2 * L], ex[2 * L:3 * L], ex[3 * L:3 * L + 1]
        e_head = ehead_ref[g]
        acum_b = _dot(acp_scr[0], e_head) + _dot(acp_scr[1], e_head) + _dot(acp_scr[2], e_head)

        def conv_silu(zx_first, conv_first, n):
            cur = tiles(zx_ref, zx_first, n)
            ext = jnp.concatenate([tiles(prev_scr, conv_first, n), cur], axis=0)
            w = tiles(cw_ref, conv_first, n)
            acc = jnp.broadcast_to(tiles(cb_ref, conv_first, n), (L, n * LANES))
            for k in range(SSD_CONV):
                back = SSD_CONV - 1 - k
                xk = cur.astype(F32) if back == 0 else _dot(shift_ref[back - 1], ext)
                acc = acc + xk * w[k:k + 1, :]
            for q in range(n):
                prev_scr[conv_first + q] = zx_ref[zx_first + q]
            return jax.nn.silu(acc)

        xs = conv_silu(x0 + n_xt * g, n_xt * g, n_xt)
        bm = conv_silu(b0 + g, cb0 + g, 1)
        cm = conv_silu(c0 + g, cc0 + g, 1)

        xdt = xs * dt_x
        cb = jnp.where(causal, _dot_nt(cm.astype(BF16), bm.astype(BF16)), 0.0)
        y_parts = []
        for pair in range(hpg // 2):
            xd = xdt[:, pair * LANES:(pair + 1) * LANES]
            y_pair = None
            for sub in range(2):
                h = 2 * pair + sub
                a_row = acum_r_scr[pl.ds(g * hpg + h, 1), :]
                seg = jnp.minimum(acum_b[:, h * L:(h + 1) * L] - a_row, 0.0)
                m_h = (cb * jnp.exp(seg)).astype(BF16)
                x_h = jnp.where(first_half if sub == 0 else jnp.logical_not(first_half), xd, 0.0)
                term = _dot(m_h, x_h.astype(BF16))
                y_pair = term if y_pair is None else y_pair + term
            y_parts.append(y_pair)
        y_diag = jnp.concatenate(y_parts, axis=1)

        st = st_scr[g]
        y_off = _dot(cm.astype(BF16), st.astype(BF16)) * ea_x
        st_scr[g] = st * cd_x + _dot_tn(bm.astype(BF16), (xdt * sd_x).astype(BF16))

        y = y_diag + y_off + xs * tiles(dskip_ref, n_xt * g, n_xt)
        y = y * jax.nn.silu(tiles(zx_ref, n_xt * g, n_xt).astype(F32))
        y = y * lax.rsqrt(jnp.mean(y * y, axis=-1, keepdims=True) + NORM_EPS)
        y = y * tiles(ng_ref, n_xt * g, n_xt)
        for q in range(n_xt):
            o_ref[n_xt * g + q] = y[:, q * LANES:(q + 1) * LANES].astype(BF16)
        return carry

    lax.fori_loop(0, SSD_GROUPS, group_body, 0)


def _ssd_chunks(zx_tiles, dt_small, dt_small_t, conv_w, conv_b, dt_bias, a_log, d_skip, norm_g):
    L = SSD_CHUNK
    nc = SEQ // L
    n_zx = SSD_MAIN // LANES
    n_conv = SSD_CONV_CH // LANES
    n_inner = SSD_D_INNER // LANES
    hpg = SSD_HEADS_PER_GROUP

    def pad_heads(v):
        return jnp.zeros((LANES,), F32).at[:SSD_HEADS].set(v)

    dtb = pad_heads(dt_bias)
    alog = pad_heads(a_log)
    cw = conv_w.reshape(SSD_CONV, n_conv, LANES).transpose(1, 0, 2)
    cb = conv_b.reshape(n_conv, 1, LANES)
    d_chan = jnp.repeat(d_skip, SSD_HEAD_DIM).reshape(n_inner, 1, LANES)
    ng = norm_g.reshape(n_inner, 1, LANES)
    t_idx = jnp.arange(L, dtype=I32)[None, :, None]
    r_idx = jnp.arange(2 * L, dtype=I32)[None, None, :]
    back = jnp.arange(1, SSD_CONV, dtype=I32)[:, None, None]
    shift = (r_idx == L + t_idx - back).astype(BF16)
    head = jnp.arange(LANES, dtype=I32)[None, :, None]
    grp = jnp.arange(SSD_GROUPS, dtype=I32)[:, None, None]
    e_chan = (head == grp * hpg + jnp.arange(SSD_GROUP_W, dtype=I32)[None, None, :] // SSD_HEAD_DIM).astype(BF16)
    e_head = (head == grp * hpg + jnp.arange(hpg * L, dtype=I32)[None, None, :] // L).astype(BF16)
    row = lambda b, c: b * nc + c
    const3 = lambda b, c: (0, 0, 0)
    const2 = lambda b, c: (0, 0)
    in_specs = [
        pl.BlockSpec((n_zx, L, LANES), lambda b, c: (0, row(b, c), 0)),
        pl.BlockSpec((L, LANES), lambda b, c: (row(b, c), 0)),
        pl.BlockSpec((LANES, L), lambda b, c: (0, row(b, c))),
        pl.BlockSpec((n_conv, SSD_CONV, LANES), const3),
        pl.BlockSpec((n_conv, 1, LANES), const3),
        pl.BlockSpec((1, LANES), const2), pl.BlockSpec((1, LANES), const2),
        pl.BlockSpec((LANES, 1), const2), pl.BlockSpec((LANES, 1), const2),
        pl.BlockSpec((n_inner, 1, LANES), const3),
        pl.BlockSpec((n_inner, 1, LANES), const3),
        pl.BlockSpec((SSD_CONV - 1, L, 2 * L), const3),
        pl.BlockSpec((SSD_GROUPS, LANES, SSD_GROUP_W), const3),
        pl.BlockSpec((SSD_GROUPS, LANES, hpg * L), const3),
    ]
    return pl.pallas_call(
        _ssd_chunk_kernel, name="ssd_chunks", grid=(BATCH, nc),
        in_specs=in_specs,
        out_specs=pl.BlockSpec((n_inner, L, LANES), lambda b, c: (0, row(b, c), 0)),
        out_shape=jax.ShapeDtypeStruct((n_inner, TOKENS, LANES), BF16),
        scratch_shapes=[pltpu.VMEM((n_conv, L, LANES), BF16),
                        pltpu.VMEM((LANES, L), F32),
                        pltpu.VMEM((SSD_GROUPS, SSD_D_STATE, SSD_GROUP_W), F32),
                        pltpu.VMEM((2, 3 * L + SSD_FAC_PAD, LANES), BF16),
                        pltpu.VMEM((3, L, LANES), BF16)],
        compiler_params=_cparams(("arbitrary", "arbitrary")),
    )(zx_tiles, dt_small, dt_small_t, cw, cb, dtb.reshape(1, LANES), alog.reshape(1, LANES),
      dtb.reshape(LANES, 1), alog.reshape(LANES, 1), d_chan, ng, shift, e_chan, e_head)


def _gather_rows(src_hbm, idx_ref, base, dst, sem, n_rows):
    def body(r, carry):
        pltpu.make_async_copy(src_hbm.at[idx_ref[base + r]], dst.at[r], sem).start()
        return carry

    lax.fori_loop(0, n_rows, body, 0, unroll=GATHER_UNROLL)


def _to_slabs(ref, val):
    for s in range(ROW_SLAB):
        ref[:, s, :] = val[:, s * LANES:(s + 1) * LANES]


def _from_slabs(slab):
    return jnp.concatenate([slab[:, s, :] for s in range(ROW_SLAB)], axis=1)


def _moe_ffn_kernel(te_ref, tok_ref, nact_ref, x_hbm, wg_ref, wu_ref, wd_ref, y_ref,
                    buf, sem, wg_bf, wu_bf, wd_bf):
    i = pl.program_id(0)
    n_act = nact_ref[0]
    tm = MOE_TM
    slot = i % 2

    def issue(tile, s):
        _gather_rows(x_hbm, tok_ref, tile * tm, buf.at[s], sem.at[s], tm)

    @pl.when(i == 0)
    def _():
        issue(0, 0)

    @pl.when(i + 1 < n_act)
    def _():
        issue(i + 1, 1 - slot)

    e_cur = te_ref[i]
    e_prev = te_ref[jnp.maximum(i - 1, 0)]

    @pl.when((i == 0) | (e_cur != e_prev))
    def _():
        wg_bf[...] = wg_ref[...].astype(BF16)
        wu_bf[...] = wu_ref[...].astype(BF16)
        wd_bf[...] = wd_ref[...].astype(BF16)

    @pl.when(i >= n_act)
    def _():
        y_ref[...] = jnp.zeros(y_ref.shape, F32)

    @pl.when(i < n_act)
    def _():
        pltpu.make_async_copy(x_hbm.at[pl.ds(0, tm)], buf.at[slot], sem.at[slot]).wait()
        h = _from_slabs(buf.at[slot]).astype(BF16)
        act = jax.nn.silu(_dot(h, wg_bf[...])) * _dot(h, wu_bf[...])
        _to_slabs(y_ref, _dot(act.astype(BF16), wd_bf[...]))


def _moe_ffn(hn_slabs, w_gate, w_up, w_down, layer, tile_expert, slot_token, n_active):
    grid_spec = pltpu.PrefetchScalarGridSpec(
        num_scalar_prefetch=3, grid=(MOE_TILES,),
        in_specs=[pl.BlockSpec(memory_space=pl.ANY),
                  pl.BlockSpec((None, None, D_MODEL, MOE_D_FF), lambda i, te, tok, na: (layer, te[i], 0, 0)),
                  pl.BlockSpec((None, None, D_MODEL, MOE_D_FF), lambda i, te, tok, na: (layer, te[i], 0, 0)),
                  pl.BlockSpec((None, None, MOE_D_FF, D_MODEL), lambda i, te, tok, na: (layer, te[i], 0, 0))],
        out_specs=pl.BlockSpec((MOE_TM, ROW_SLAB, LANES), lambda i, te, tok, na: (i, 0, 0)),
        scratch_shapes=[pltpu.VMEM((2, MOE_TM, ROW_SLAB, LANES), F32), pltpu.SemaphoreType.DMA((2,)),
                        pltpu.VMEM((D_MODEL, MOE_D_FF), BF16), pltpu.VMEM((D_MODEL, MOE_D_FF), BF16),
                        pltpu.VMEM((MOE_D_FF, D_MODEL), BF16)])
    return pl.pallas_call(
        _moe_ffn_kernel, name="moe_ffn", grid_spec=grid_spec,
        out_shape=jax.ShapeDtypeStruct((MOE_ROWS, ROW_SLAB, LANES), F32),
        compiler_params=_cparams(("arbitrary",)),
    )(tile_expert, slot_token, n_active, hn_slabs, w_gate, w_up, w_down)


def _moe_combine_kernel(pos_ref, x_ref, cw_ref, g_ref, y_hbm, o_ref, buf, sem, *, final_norm):
    i = pl.program_id(0)
    n = pl.num_programs(0)
    tm = CMB_TM
    slot = i % 2

    def issue(tile, s):
        for k in range(MOE_TOPK):
            _gather_rows(y_hbm, pos_ref, (k * (TOKENS // tm) + tile) * tm, buf.at[s, k], sem.at[s], tm)

    @pl.when(i == 0)
    def _():
        issue(0, 0)

    @pl.when(i + 1 < n)
    def _():
        issue(i + 1, 1 - slot)

    for k in range(MOE_TOPK):
        pltpu.make_async_copy(y_hbm.at[pl.ds(0, tm)], buf.at[slot, k], sem.at[slot]).wait()
    cw = cw_ref[...]
    out = x_ref[...] + cw[:, 0:1] * _from_slabs(buf.at[slot, 0]) + cw[:, 1:2] * _from_slabs(buf.at[slot, 1])
    if final_norm:
        out = _rms(out, g_ref[...])
    o_ref[...] = out


def _moe_combine(x, cw, y_sorted, pos_kmajor, g_final, final_norm):
    grid_spec = pltpu.PrefetchScalarGridSpec(
        num_scalar_prefetch=1, grid=(TOKENS // CMB_TM,),
        in_specs=[pl.BlockSpec((CMB_TM, D_MODEL), lambda i, pos: (i, 0)),
                  pl.BlockSpec((CMB_TM, LANES), lambda i, pos: (i, 0)),
                  pl.BlockSpec((1, D_MODEL), lambda i, pos: (0, 0)),
                  pl.BlockSpec(memory_space=pl.ANY)],
        out_specs=pl.BlockSpec((CMB_TM, D_MODEL), lambda i, pos: (i, 0)),
        scratch_shapes=[pltpu.VMEM((2, MOE_TOPK, CMB_TM, ROW_SLAB, LANES), F32), pltpu.SemaphoreType.DMA((2,))])
    return pl.pallas_call(
        functools.partial(_moe_combine_kernel, final_norm=final_norm), name="moe_combine", grid_spec=grid_spec,
        out_shape=jax.ShapeDtypeStruct((TOKENS, D_MODEL), F32),
        compiler_params=_cparams(("arbitrary",)),
    )(pos_kmajor, x, cw, g_final.reshape(1, D_MODEL), y_sorted)


def _moe_plan(eid):
    e = eid[:, :MOE_TOPK].reshape(-1)
    n_pairs = e.shape[0]
    onehot = (e[:, None] == jnp.arange(MOE_EXPERTS, dtype=I32)[None, :]).astype(I32)
    csum = jnp.cumsum(onehot, axis=0)
    counts = csum[-1]
    rank = jnp.take_along_axis(csum, e[:, None], axis=1)[:, 0] - 1
    padded = ((counts + MOE_TM - 1) // MOE_TM) * MOE_TM
    g_end = jnp.cumsum(padded)
    g_start = g_end - padded
    pos = g_start[e] + rank
    n_active = (g_end[-1] // MOE_TM).astype(I32)
    tile_start = jnp.arange(MOE_TILES, dtype=I32) * MOE_TM
    te = jnp.sum((g_end[None, :] <= tile_start[:, None]).astype(I32), axis=1)
    last = jnp.max(jnp.where(counts > 0, jnp.arange(MOE_EXPERTS, dtype=I32), 0))
    tile_expert = jnp.minimum(te, last)
    slot_token = jnp.zeros((MOE_ROWS,), I32).at[pos].set(jnp.arange(n_pairs, dtype=I32) // MOE_TOPK)
    pos_kmajor = pos.reshape(TOKENS, MOE_TOPK).T.reshape(-1)
    return tile_expert, slot_token, n_active.reshape(1), pos_kmajor


def _hier_moe_add(x, ln_g, w_group, b_group, w_expert, b_expert, w_gate, w_up, w_down, layer,
                  g_final, final_norm):
    eid, cw, hn_slabs = _router(x, ln_g, w_group, b_group, w_expert, b_expert)
    tile_expert, slot_token, n_active, pos_kmajor = _moe_plan(eid)
    y_sorted = _moe_ffn(hn_slabs, w_gate, w_up, w_down, layer, tile_expert, slot_token, n_active)
    return _moe_combine(x, cw, y_sorted, pos_kmajor, g_final, final_norm)


def _rope_tables():
    pos = jnp.arange(SEQ, dtype=F32)
    inv = 1.0 / (ROPE_THETA ** (jnp.arange(0, NSA_HEAD_DIM, 2, dtype=F32) / NSA_HEAD_DIM))
    ang = pos[:, None] * inv[None, :]
    cos, sin = jnp.cos(ang), jnp.sin(ang)
    return jnp.concatenate([cos, cos], axis=1), jnp.concatenate([-sin, sin], axis=1)


def _nsa_mixer_add(x, ln_g, w_in, cmp_pe, cmp_w1, cmp_w2, w_out):
    hn, _, g_lin_t = _norm_small(x, ln_g, w_in[:, NSA_MAIN:], transposed=True)
    cos_full, sin_signed = _rope_tables()
    heads = _nsa_inproj(hn, w_in, cos_full, sin_signed)
    first_c = NSA_HEADS
    kv_c = heads[first_c:first_c + 2 * NSA_KV_HEADS]
    kv_chunks = kv_c.reshape(2, NSA_KV_HEADS, BATCH, SEQ // CMP_STRIDE, CMP_STRIDE * NSA_HEAD_DIM)
    kc_vc = _compress(kv_chunks, cmp_pe, cmp_w1, cmp_w2)
    gates_t = g_lin_t[:NSA_GATES].reshape(NSA_KV_HEADS, 3 * NSA_Q_PER_KV, TOKENS)
    o = _nsa_attention(heads, kc_vc, gates_t)
    return _matmul_resid(o, w_out, x, 512)


def _ssd_mixer_add(x, ln_g, w_in, conv_w, conv_b, dt_bias, a_log, d_skip, norm_g, w_out):
    hn, dt_small, dt_small_t = _norm_small(x, ln_g, w_in[:, SSD_MAIN:], transposed=True)
    zx_tiles = _matmul_tiles(hn, w_in, SSD_MAIN, 1024)
    y_tiles = _ssd_chunks(zx_tiles, dt_small, dt_small_t, conv_w, conv_b, dt_bias, a_log, d_skip, norm_g)
    return _matmul_resid_tiles(y_tiles, w_out, x, 512)


def kernel(x, ln_mix, ln_ffn, ln_final, nsa_w_in, nsa_cmp_pe, nsa_cmp_w1, nsa_cmp_w2, nsa_w_out,
           ssd_w_in, ssd_conv_w, ssd_conv_b, ssd_dt_bias, ssd_a_log, ssd_d, ssd_norm, ssd_w_out,
           moe_w_group, moe_b_group, moe_w_expert, moe_b_expert, moe_w_gate, moe_w_up, moe_w_down):
    h = x.reshape(TOKENS, D_MODEL)
    for i in range(DEPTH):
        j = i // N_MIXERS
        if i % N_MIXERS == 0:
            h = _nsa_mixer_add(h, ln_mix[i], nsa_w_in[j], nsa_cmp_pe[j], nsa_cmp_w1[j], nsa_cmp_w2[j],
                               nsa_w_out[j])
        else:
            h = _ssd_mixer_add(h, ln_mix[i], ssd_w_in[j], ssd_conv_w[j], ssd_conv_b[j], ssd_dt_bias[j],
                               ssd_a_log[j], ssd_d[j], ssd_norm[j], ssd_w_out[j])
        h = _hier_moe_add(h, ln_ffn[i], moe_w_group[i], moe_b_group[i], moe_w_expert[i], moe_b_expert[i],
                          moe_w_gate, moe_w_up, moe_w_down, i, ln_final, i == DEPTH - 1)
    return h.reshape(BATCH, SEQ, D_MODEL)
```

```python
import functools

import jax
import jax.numpy as jnp
from jax import lax
from jax.experimental import pallas as pl
from jax.experimental.pallas import tpu as pltpu

F32 = jnp.float32
BF16 = jnp.bfloat16
I32 = jnp.int32

D_MODEL = 2048
BATCH = 4
SEQ = 2048
TOKENS = BATCH * SEQ
DEPTH = 2
N_MIXERS = 2
NORM_EPS = 1e-6
NEG_INF = -1e30
ROPE_THETA = 10000.0

NSA_HEADS = 16
NSA_KV_HEADS = 4
NSA_HEAD_DIM = D_MODEL // NSA_HEADS
NSA_Q_PER_KV = NSA_HEADS // NSA_KV_HEADS
CMP_BLOCK = 32
CMP_STRIDE = 16
CMP_HIDDEN = 256
N_CMP = (SEQ - CMP_BLOCK) // CMP_STRIDE + 1
SLC_BLOCK = 64
SLC_TOPK = 16
SLC_LOCAL = 2
SLC_FORCE = 1e4
N_SLC = SEQ // SLC_BLOCK
WINDOW = 512
NSA_QD = NSA_HEADS * NSA_HEAD_DIM
NSA_KVD = NSA_KV_HEADS * NSA_HEAD_DIM
NSA_MAIN = NSA_QD + 6 * NSA_KVD
NSA_GATES = 3 * NSA_HEADS

SSD_D_INNER = 2 * D_MODEL
SSD_HEAD_DIM = 64
SSD_HEADS = SSD_D_INNER // SSD_HEAD_DIM
SSD_GROUPS = 8
SSD_HEADS_PER_GROUP = SSD_HEADS // SSD_GROUPS
SSD_D_STATE = 128
SSD_CONV = 4
SSD_CHUNK = 128
SSD_GROUP_W = SSD_D_INNER // SSD_GROUPS
SSD_BC = SSD_GROUPS * SSD_D_STATE
SSD_CONV_CH = SSD_D_INNER + 2 * SSD_BC
SSD_MAIN = SSD_D_INNER + SSD_CONV_CH
SSD_FAC_PAD = 16

MOE_GROUPS = 4
MOE_EPG = 8
MOE_EXPERTS = MOE_GROUPS * MOE_EPG
MOE_TOPK = 2
MOE_D_FF = 512

LANES = 128
VMEM_LIMIT = 56 * 1024 * 1024

NORM_TM = 256
MM_TM = 512
ATT_TQ = 128
ATT_CHUNK_TILES = 4
ATT_ONES = 16
MOE_TM = 256
MOE_TILES = (TOKENS * MOE_TOPK) // MOE_TM + MOE_EXPERTS
MOE_ROWS = MOE_TILES * MOE_TM
CMB_TM = 128
GATHER_UNROLL = 8
MOE_ISSUE_PARTS = 3


def _cparams(sem):
    return pltpu.CompilerParams(dimension_semantics=sem, vmem_limit_bytes=VMEM_LIMIT)


def _split3(x):
    hi = x.astype(BF16)
    r1 = x - hi.astype(F32)
    mid = r1.astype(BF16)
    lo = (r1 - mid.astype(F32)).astype(BF16)
    return hi, mid, lo


def _dot(a, b):
    return jnp.dot(a, b, preferred_element_type=F32)


def _dot_nt(a, b):
    return lax.dot_general(a, b, (((1,), (1,)), ((), ())), preferred_element_type=F32)


def _dot_tn(a, b):
    return lax.dot_general(a, b, (((0,), (0,)), ((), ())), preferred_element_type=F32)


def _dot_split_lhs(x, m_bf16):
    hi, mid, lo = _split3(x)
    return _dot(hi, m_bf16) + _dot(mid, m_bf16) + _dot(lo, m_bf16)


def _dot_split_rhs(m_bf16, x):
    hi, mid, lo = _split3(x)
    return _dot(m_bf16, hi) + _dot(m_bf16, mid) + _dot(m_bf16, lo)


def _dot_x3(a, w):
    a_hi = a.astype(BF16)
    a_lo = (a - a_hi.astype(F32)).astype(BF16)
    w_hi = w.astype(BF16)
    w_lo = (w - w_hi.astype(F32)).astype(BF16)
    return _dot(a_hi, w_hi) + _dot(a_hi, w_lo) + _dot(a_lo, w_hi)


def _dot_x3_nt(a, w):
    a_hi = a.astype(BF16)
    a_lo = (a - a_hi.astype(F32)).astype(BF16)
    w_hi = w.astype(BF16)
    w_lo = (w - w_hi.astype(F32)).astype(BF16)
    return _dot_nt(a_hi, w_hi) + _dot_nt(a_hi, w_lo) + _dot_nt(a_lo, w_hi)


def _rms(x, g):
    y = x * lax.rsqrt(jnp.mean(x * x, axis=-1, keepdims=True) + NORM_EPS)
    return y * g


def _norm_small_kernel(x_ref, g_ref, ws_ref, hn_ref, small_ref):
    y = _rms(x_ref[...], g_ref[...])
    hn_ref[...] = y.astype(BF16)
    small_ref[...] = _dot_x3(y, ws_ref[...])


def _norm_small_t_kernel(x_ref, g_ref, ws_ref, wst_ref, hn_ref, small_ref, small_t_ref):
    y = _rms(x_ref[...], g_ref[...])
    hn_ref[...] = y.astype(BF16)
    small_ref[...] = _dot_x3(y, ws_ref[...])
    small_t_ref[...] = _dot_x3_nt(wst_ref[...], y)


def _norm_small(x, g, w_small, transposed=False):
    n = w_small.shape[1]
    ws = jnp.zeros((D_MODEL, LANES), F32).at[:, :n].set(w_small)
    grid = (TOKENS // NORM_TM,)
    x_spec = pl.BlockSpec((NORM_TM, D_MODEL), lambda i: (i, 0))
    g_spec = pl.BlockSpec((1, D_MODEL), lambda i: (0, 0))
    w_spec = pl.BlockSpec((D_MODEL, LANES), lambda i: (0, 0))
    hn_spec = pl.BlockSpec((NORM_TM, D_MODEL), lambda i: (i, 0))
    sm_spec = pl.BlockSpec((NORM_TM, LANES), lambda i: (i, 0))
    hn_shape = jax.ShapeDtypeStruct((TOKENS, D_MODEL), BF16)
    sm_shape = jax.ShapeDtypeStruct((TOKENS, LANES), F32)
    if not transposed:
        return pl.pallas_call(
            _norm_small_kernel, name="norm_small", grid=grid,
            in_specs=[x_spec, g_spec, w_spec],
            out_specs=[hn_spec, sm_spec],
            out_shape=[hn_shape, sm_shape],
            compiler_params=_cparams(("parallel",)),
        )(x, g.reshape(1, D_MODEL), ws)
    wt_spec = pl.BlockSpec((LANES, D_MODEL), lambda i: (0, 0))
    smt_spec = pl.BlockSpec((LANES, NORM_TM), lambda i: (0, i))
    smt_shape = jax.ShapeDtypeStruct((LANES, TOKENS), F32)
    return pl.pallas_call(
        _norm_small_t_kernel, name="norm_small_t", grid=grid,
        in_specs=[x_spec, g_spec, w_spec, wt_spec],
        out_specs=[hn_spec, sm_spec, smt_spec],
        out_shape=[hn_shape, sm_shape, smt_shape],
        compiler_params=_cparams(("parallel",)),
    )(x, g.reshape(1, D_MODEL), ws, ws.T)


def _router_kernel(x_ref, g_ref, ws_ref, b_ref, eid_ref, cw_ref):
    y = _rms(x_ref[...], g_ref[...])
    logits = _dot_x3(y, ws_ref[...]) + b_ref[...]
    lane = lax.broadcasted_iota(I32, logits.shape, 1)
    big = jnp.int32(LANES)
    neg = -jnp.inf
    gl = jnp.where(lane < MOE_GROUPS, logits, neg)
    gmax = jnp.max(gl, axis=-1, keepdims=True)
    gsum = jnp.sum(jnp.exp(gl - gmax), axis=-1, keepdims=True)
    g_w = 1.0 / gsum
    g_sel = jnp.min(jnp.where(gl == gmax, lane, big), axis=-1, keepdims=True)
    lo = MOE_GROUPS + g_sel * MOE_EPG
    el = jnp.where((lane >= lo) & (lane < lo + MOE_EPG), logits, neg)
    v1 = jnp.max(el, axis=-1, keepdims=True)
    i1 = jnp.min(jnp.where(el == v1, lane, big), axis=-1, keepdims=True)
    el2 = jnp.where(lane == i1, neg, el)
    v2 = jnp.max(el2, axis=-1, keepdims=True)
    i2 = jnp.min(jnp.where(el2 == v2, lane, big), axis=-1, keepdims=True)
    e2 = jnp.exp(v2 - v1)
    den = 1.0 + e2
    w1 = (1.0 / den) * g_w
    w2 = (e2 / den) * g_w
    eid_ref[...] = jnp.where(lane == 0, i1 - MOE_GROUPS, jnp.where(lane == 1, i2 - MOE_GROUPS, 0))
    cw_ref[...] = jnp.where(lane == 0, w1, jnp.where(lane == 1, w2, 0.0))


def _router(x, g, w_group, b_group, w_expert, b_expert):
    n = MOE_GROUPS + MOE_EXPERTS
    ws = jnp.zeros((D_MODEL, LANES), F32).at[:, :n].set(jnp.concatenate([w_group, w_expert], axis=1))
    bs = jnp.zeros((1, LANES), F32).at[0, :n].set(jnp.concatenate([b_group, b_expert]))
    return pl.pallas_call(
        _router_kernel, name="router", grid=(TOKENS // NORM_TM,),
        in_specs=[pl.BlockSpec((NORM_TM, D_MODEL), lambda i: (i, 0)),
                  pl.BlockSpec((1, D_MODEL), lambda i: (0, 0)),
                  pl.BlockSpec((D_MODEL, LANES), lambda i: (0, 0)),
                  pl.BlockSpec((1, LANES), lambda i: (0, 0))],
        out_specs=[pl.BlockSpec((NORM_TM, LANES), lambda i: (i, 0)),
                   pl.BlockSpec((NORM_TM, LANES), lambda i: (i, 0))],
        out_shape=[jax.ShapeDtypeStruct((TOKENS, LANES), I32),
                   jax.ShapeDtypeStruct((TOKENS, LANES), F32)],
        compiler_params=_cparams(("parallel",)),
    )(x, g.reshape(1, D_MODEL), ws, bs)


def _inproj_heads_kernel(a_ref, w_ref, cos_ref, sin_ref, o_ref, wbf_ref):
    j = pl.program_id(0)

    @pl.when(pl.program_id(1) == 0)
    def _():
        wbf_ref[...] = w_ref[...].astype(BF16)

    acc = _dot(a_ref[...], wbf_ref[...])
    heads = acc.shape[1] // NSA_HEAD_DIM
    is_rope = (j < NSA_QD // acc.shape[1]) | (j % 2 == 0)

    @pl.when(is_rope)
    def _():
        c = cos_ref[...]
        s = sin_ref[...]
        for h in range(heads):
            xh = acc[:, h * NSA_HEAD_DIM:(h + 1) * NSA_HEAD_DIM]
            o_ref[h] = (xh * c + pltpu.roll(xh, NSA_HEAD_DIM // 2, 1) * s).astype(BF16)

    @pl.when(jnp.logical_not(is_rope))
    def _():
        for h in range(heads):
            o_ref[h] = acc[:, h * NSA_HEAD_DIM:(h + 1) * NSA_HEAD_DIM].astype(BF16)


def _nsa_inproj(hn, w_in, cos_full, sin_signed):
    tn = NSA_KVD
    n_heads_tile = tn // NSA_HEAD_DIM
    s_tiles = SEQ // MM_TM
    return pl.pallas_call(
        _inproj_heads_kernel, name="nsa_inproj", grid=(NSA_MAIN // tn, TOKENS // MM_TM),
        in_specs=[pl.BlockSpec((MM_TM, D_MODEL), lambda j, i: (i, 0)),
                  pl.BlockSpec((D_MODEL, tn), lambda j, i: (0, j)),
                  pl.BlockSpec((MM_TM, NSA_HEAD_DIM), lambda j, i: (i % s_tiles, 0)),
                  pl.BlockSpec((MM_TM, NSA_HEAD_DIM), lambda j, i: (i % s_tiles, 0))],
        out_specs=pl.BlockSpec((n_heads_tile, MM_TM, NSA_HEAD_DIM), lambda j, i: (j, i, 0)),
        out_shape=jax.ShapeDtypeStruct((NSA_MAIN // NSA_HEAD_DIM, TOKENS, NSA_HEAD_DIM), BF16),
        scratch_shapes=[pltpu.VMEM((D_MODEL, tn), BF16)],
        compiler_params=_cparams(("arbitrary", "arbitrary")),
    )(hn, w_in, cos_full, sin_signed)


def _matmul_resid_kernel(a_ref, w_ref, r_ref, o_ref, wbf_ref):
    @pl.when(pl.program_id(1) == 0)
    def _():
        wbf_ref[...] = w_ref[...].astype(BF16)

    o_ref[...] = r_ref[...] + _dot(a_ref[...], wbf_ref[...])


def _matmul_tiles_kernel(a_ref, w_ref, o_ref, wbf_ref):
    @pl.when(pl.program_id(1) == 0)
    def _():
        wbf_ref[...] = w_ref[...].astype(BF16)

    acc = _dot(a_ref[...], wbf_ref[...])
    for t in range(acc.shape[1] // LANES):
        o_ref[t] = acc[:, t * LANES:(t + 1) * LANES].astype(o_ref.dtype)


def _matmul_tiles(a, w, n_cols, tn):
    k = a.shape[1]
    return pl.pallas_call(
        _matmul_tiles_kernel, name="matmul_tiles", grid=(n_cols // tn, TOKENS // MM_TM),
        in_specs=[pl.BlockSpec((MM_TM, k), lambda j, i: (i, 0)),
                  pl.BlockSpec((k, tn), lambda j, i: (0, j))],
        out_specs=pl.BlockSpec((tn // LANES, MM_TM, LANES), lambda j, i: (j, i, 0)),
        out_shape=jax.ShapeDtypeStruct((n_cols // LANES, TOKENS, LANES), BF16),
        scratch_shapes=[pltpu.VMEM((k, tn), BF16)],
        compiler_params=_cparams(("arbitrary", "arbitrary")),
    )(a, w)


def _matmul_resid_tiles_kernel(a_ref, w_ref, r_ref, o_ref, wbf_ref):
    @pl.when(pl.program_id(1) == 0)
    def _():
        wbf_ref[...] = w_ref[...].astype(BF16)

    a = jnp.concatenate([a_ref[t] for t in range(a_ref.shape[0])], axis=1)
    o_ref[...] = r_ref[...] + _dot(a, wbf_ref[...])


def _matmul_resid_tiles(a_tiles, w, resid, tn):
    n_kt = a_tiles.shape[0]
    k = n_kt * LANES
    n = w.shape[1]
    return pl.pallas_call(
        _matmul_resid_tiles_kernel, name="matmul_resid_tiles", grid=(n // tn, TOKENS // MM_TM),
        in_specs=[pl.BlockSpec((n_kt, MM_TM, LANES), lambda j, i: (0, i, 0)),
                  pl.BlockSpec((k, tn), lambda j, i: (0, j)),
                  pl.BlockSpec((MM_TM, tn), lambda j, i: (i, j))],
        out_specs=pl.BlockSpec((MM_TM, tn), lambda j, i: (i, j)),
        out_shape=jax.ShapeDtypeStruct((TOKENS, n), F32),
        scratch_shapes=[pltpu.VMEM((k, tn), BF16)],
        compiler_params=_cparams(("arbitrary", "arbitrary")),
    )(a_tiles, w, resid)


def _matmul_resid(a, w, resid, tn):
    k = a.shape[1]
    n = w.shape[1]
    return pl.pallas_call(
        _matmul_resid_kernel, name="matmul_resid", grid=(n // tn, TOKENS // MM_TM),
        in_specs=[pl.BlockSpec((MM_TM, k), lambda j, i: (i, 0)),
                  pl.BlockSpec((k, tn), lambda j, i: (0, j)),
                  pl.BlockSpec((MM_TM, tn), lambda j, i: (i, j))],
        out_specs=pl.BlockSpec((MM_TM, tn), lambda j, i: (i, j)),
        out_shape=jax.ShapeDtypeStruct((TOKENS, n), F32),
        scratch_shapes=[pltpu.VMEM((k, tn), BF16)],
        compiler_params=_cparams(("arbitrary", "arbitrary")),
    )(a, w, resid)


def _compress_kernel(x_ref, pe_ref, w1_ref, w2_ref, o_ref):
    half = CMP_STRIDE * NSA_HEAD_DIM
    x = x_ref[0, 0, 0]
    w1 = w1_ref[0].astype(BF16)
    top = _dot(x, w1[:half])
    bot = _dot(x, w1[half:])
    pe = jnp.broadcast_to(pe_ref[0], (8, 2 * half)).astype(BF16)
    pe_bias = _dot(pe, w1)[0:1]
    hid = top + pltpu.roll(bot, bot.shape[0] - 1, 0) + pe_bias
    act = jax.nn.gelu(hid)
    o_ref[0, 0, 0] = _dot(act.astype(BF16), w2_ref[0].astype(BF16))


def _compress(kv_chunks, pe, w1, w2):
    n_chunk = SEQ // CMP_STRIDE
    feat = CMP_STRIDE * NSA_HEAD_DIM
    return pl.pallas_call(
        _compress_kernel, name="nsa_compress", grid=(2, NSA_KV_HEADS, BATCH),
        in_specs=[pl.BlockSpec((1, 1, 1, n_chunk, feat), lambda a, g, b: (a, g, b, 0, 0)),
                  pl.BlockSpec((1, 1, 2 * feat), lambda a, g, b: (a, 0, 0)),
                  pl.BlockSpec((1, 2 * feat, CMP_HIDDEN), lambda a, g, b: (a, 0, 0)),
                  pl.BlockSpec((1, CMP_HIDDEN, NSA_HEAD_DIM), lambda a, g, b: (a, 0, 0))],
        out_specs=pl.BlockSpec((1, 1, 1, n_chunk, NSA_HEAD_DIM), lambda a, g, b: (a, g, b, 0, 0)),
        out_shape=jax.ShapeDtypeStruct((2, NSA_KV_HEADS, BATCH, n_chunk, NSA_HEAD_DIM), F32),
        compiler_params=_cparams(("parallel", "parallel", "parallel")),
    )(kv_chunks, pe.reshape(2, 1, 2 * feat), w1, w2)


def _nsa_attn_kernel(q_ref, ks_ref, vs_ref, kw_ref, vw_ref, kc_ref, vc_ref, gate_ref, o_ref,
                     vst_scr, vwt_scr, sel_scr, acc_scr):
    qi = pl.program_id(2)
    tq = ATT_TQ
    tk = ATT_TQ
    dh = NSA_HEAD_DIM
    r_heads = NSA_Q_PER_KV
    n_kt = SEQ // tk
    scale = dh ** -0.5

    @pl.when(qi == 0)
    def _():
        ones = jnp.ones((ATT_ONES, tk), BF16)
        for kt in range(n_kt):
            rows = slice(kt * tk, (kt + 1) * tk)
            vst_scr[kt, 0:dh, :] = vs_ref[0, rows, :].astype(F32).T.astype(BF16)
            vwt_scr[kt, 0:dh, :] = vw_ref[0, rows, :].astype(F32).T.astype(BF16)
            vst_scr[kt, dh:dh + ATT_ONES, :] = ones
            vwt_scr[kt, dh:dh + ATT_ONES, :] = ones

    q_t = jnp.concatenate([q_ref[r].astype(F32).T for r in range(r_heads)], axis=1).astype(BF16)
    sub = lax.broadcasted_iota(I32, (tk, tq), 0)
    t_pos = qi * tq + lax.broadcasted_iota(I32, (tk, tq), 1)

    def tile4(a):
        return jnp.concatenate([a] * r_heads, axis=1)

    kc = kc_ref[0, 0, 0].astype(BF16)
    vc = vc_ref[0, 0, 0].astype(BF16)
    ok_c = jnp.where(sub * CMP_STRIDE + CMP_BLOCK - 1 <= t_pos, jnp.where(sub < N_CMP, 1.0, 0.0), 0.0)
    ok_c4 = tile4(ok_c)
    s_c = _dot(kc, q_t) * scale + (ok_c4 - 1.0) * (-NEG_INF)
    e_c = jnp.exp(s_c - jnp.max(s_c, axis=0, keepdims=True))
    p_c = (e_c / jnp.sum(e_c, axis=0, keepdims=True)) * ok_c4
    o_cmp = _dot_tn(vc, p_c.astype(BF16))
    p_sum = p_c[:, 0:tq]
    for r in range(1, r_heads):
        p_sum = p_sum + p_c[:, r * tq:(r + 1) * tq]

    blk_row = lax.broadcasted_iota(I32, (LANES, LANES), 0)
    cmp_col = lax.broadcasted_iota(I32, (LANES, LANES), 1)
    s_start = blk_row * SLC_BLOCK
    c_start = cmp_col * CMP_STRIDE
    ov_t = jnp.maximum(jnp.minimum(c_start + CMP_BLOCK, s_start + SLC_BLOCK)
                       - jnp.maximum(c_start, s_start), 0).astype(F32) / CMP_BLOCK
    ov_t = jnp.where(blk_row < N_SLC, ov_t, 0.0).astype(BF16)
    imp = _dot_split_rhs(ov_t, p_sum)[0:N_SLC]
    j_blk = lax.broadcasted_iota(I32, (N_SLC, tq), 0)
    dist = (qi * tq + lax.broadcasted_iota(I32, (N_SLC, tq), 1)) // SLC_BLOCK - j_blk
    imp = jnp.where(j_blk == 0, SLC_FORCE, jnp.where(dist < 0, imp, jnp.where(dist < SLC_LOCAL, SLC_FORCE, imp)))
    imp = jnp.where(dist >= 0, imp, -jnp.inf)
    cnt = jnp.zeros((N_SLC, tq), I32)
    for k in range(N_SLC):
        row_k = imp[k:k + 1, :]
        tie = jnp.where(j_blk > k, 1, 0)
        cnt = cnt + jnp.where(row_k > imp, 1, jnp.where(row_k == imp, tie, 0))
    sel = jnp.where(cnt < min(SLC_TOPK, N_SLC), 1.0, 0.0)
    for j in range(N_SLC):
        sel_scr[8 * j:8 * j + 8, :] = jnp.broadcast_to(sel[j:j + 1, :], (8, tq))

    def scores(k, okf):
        return _dot(k, q_t) * scale + (okf - 1.0) * (-NEG_INF)

    def weighted_values(vt_scr, kt0, pr, n_tiles):
        out = None
        for u in range(n_tiles):
            term = _dot(vt_scr[kt0 + u], pr[u * tk:(u + 1) * tk].astype(BF16))
            out = term if out is None else out + term
        return out

    n_ct = ATT_CHUNK_TILES
    ck = n_ct * tk
    sub_c = lax.broadcasted_iota(I32, (ck, tq), 0)
    t_pos_c = qi * tq + lax.broadcasted_iota(I32, (ck, tq), 1)
    blocks_per_chunk = ck // SLC_BLOCK
    acc_scr[...] = jnp.zeros(acc_scr.shape, F32)

    def slc_body(c, m_old):
        start = pl.multiple_of(c * ck, ck)
        k = ks_ref[0, pl.ds(start, ck), :]
        rows8 = sel_scr[pl.ds(pl.multiple_of(c * (8 * blocks_per_chunk), 8 * blocks_per_chunk),
                              8 * blocks_per_chunk), :]
        picked = jnp.concatenate(
            [rows8[8 * u:8 * u + 8] for u in range(blocks_per_chunk) for _ in range(SLC_BLOCK // 8)], axis=0)
        okf = tile4(jnp.where(start + sub_c <= t_pos_c, picked, 0.0))
        sc = scores(k, okf)
        m_new = jnp.maximum(m_old, jnp.max(sc, axis=0, keepdims=True))
        alpha = jnp.exp(m_old - m_new)
        pr = jnp.exp(sc - m_new) * okf
        acc_scr[...] = alpha * acc_scr[...] + weighted_values(vst_scr, c * n_ct, pr, n_ct)
        return m_new

    lax.fori_loop(0, (qi + n_ct) // n_ct, slc_body, jnp.full((1, r_heads * tq), NEG_INF, F32))
    acc = acc_scr[...]
    o_slc = acc[0:dh] / acc[dh:dh + 1]

    n_wt = WINDOW // tk + 1
    kt0 = jnp.maximum(qi - WINDOW // tk, 0)
    w_start = pl.multiple_of(kt0 * tk, tk)
    key_w = w_start + lax.broadcasted_iota(I32, (n_wt * tk, tq), 0)
    t_pos_w = qi * tq + lax.broadcasted_iota(I32, (n_wt * tk, tq), 1)
    okf_w = tile4(jnp.where(key_w <= t_pos_w, jnp.where(key_w > t_pos_w - WINDOW, 1.0, 0.0), 0.0))
    sc_w = scores(kw_ref[0, pl.ds(w_start, n_wt * tk), :], okf_w)
    pr_w = jnp.exp(sc_w - jnp.max(sc_w, axis=0, keepdims=True)) * okf_w
    acc_w = weighted_values(vwt_scr, kt0, pr_w, n_wt)
    o_win = acc_w[0:dh] / acc_w[dh:dh + 1]

    gate = jax.nn.sigmoid(gate_ref[0])
    for r in range(r_heads):
        cols = slice(r * tq, (r + 1) * tq)
        o = (gate[3 * r:3 * r + 1] * o_cmp[:, cols] + gate[3 * r + 1:3 * r + 2] * o_slc[:, cols]
             + gate[3 * r + 2:3 * r + 3] * o_win[:, cols])
        o_ref[:, r * dh:(r + 1) * dh] = o.T.astype(BF16)


def _nsa_attention(heads, kc_vc, gates):
    tq = ATT_TQ
    nq = SEQ // tq
    r = NSA_Q_PER_KV
    g_heads = NSA_KV_HEADS
    q_spec = pl.BlockSpec((r, tq, NSA_HEAD_DIM), lambda b, g, i: (g, b * nq + i, 0))

    def kv_spec(first_head):
        return pl.BlockSpec((1, SEQ, NSA_HEAD_DIM), lambda b, g, i: (first_head + g, b, 0))

    first = NSA_HEADS
    specs = [q_spec,
             kv_spec(first + 2 * g_heads), kv_spec(first + 3 * g_heads),
             kv_spec(first + 4 * g_heads), kv_spec(first + 5 * g_heads),
             pl.BlockSpec((1, 1, 1, SEQ // CMP_STRIDE, NSA_HEAD_DIM), lambda b, g, i: (0, g, b, 0, 0)),
             pl.BlockSpec((1, 1, 1, SEQ // CMP_STRIDE, NSA_HEAD_DIM), lambda b, g, i: (1, g, b, 0, 0)),
             pl.BlockSpec((1, 3 * r, tq), lambda b, g, i: (g, 0, b * nq + i))]
    vt_shape = (SEQ // tq, NSA_HEAD_DIM + ATT_ONES, tq)
    return pl.pallas_call(
        _nsa_attn_kernel, name="nsa_attn", grid=(BATCH, g_heads, nq),
        in_specs=specs,
        out_specs=pl.BlockSpec((tq, r * NSA_HEAD_DIM), lambda b, g, i: (b * nq + i, g)),
        out_shape=jax.ShapeDtypeStruct((TOKENS, NSA_QD), BF16),
        scratch_shapes=[pltpu.VMEM(vt_shape, BF16), pltpu.VMEM(vt_shape, BF16),
                        pltpu.VMEM((8 * N_SLC, tq), F32),
                        pltpu.VMEM((NSA_HEAD_DIM + ATT_ONES, r * tq), F32)],
        compiler_params=_cparams(("arbitrary", "arbitrary", "arbitrary")),
    )(heads, heads, heads, heads, heads, kc_vc, kc_vc, gates)


def _ssd_chunk_kernel(zx_ref, dtc_ref, dtr_ref, cw_ref, cb_ref, dtb_c_ref, alog_c_ref, dtb_r_ref, alog_r_ref,
                      dskip_ref, ng_ref, shift_ref, echan_ref, ehead_ref, o_ref,
                      prev_scr, acum_r_scr, st_scr, fac_scr, acp_scr):
    chunk = pl.program_id(1)
    L = SSD_CHUNK
    W = SSD_GROUP_W
    hpg = SSD_HEADS_PER_GROUP
    n_xt = W // LANES
    x0 = SSD_D_INNER // LANES
    b0 = 2 * SSD_D_INNER // LANES
    c0 = b0 + SSD_GROUPS
    cb0 = SSD_D_INNER // LANES
    cc0 = cb0 + SSD_GROUPS

    @pl.when(chunk == 0)
    def _():
        prev_scr[...] = jnp.zeros(prev_scr.shape, BF16)
        st_scr[...] = jnp.zeros(st_scr.shape, F32)

    dt_c = jax.nn.softplus(dtc_ref[...] + dtb_c_ref[...])
    adt_c = dt_c * (-jnp.exp(alog_c_ref[...]))
    dt_r = jax.nn.softplus(dtr_ref[...] + dtb_r_ref[...])
    adt_r = dt_r * (-jnp.exp(alog_r_ref[...]))
    row = lax.broadcasted_iota(I32, (L, L), 0)
    col = lax.broadcasted_iota(I32, (L, L), 1)
    causal = row >= col
    tri = jnp.where(causal, 1.0, 0.0).astype(BF16)
    tri_t = jnp.where(col >= row, 1.0, 0.0).astype(BF16)
    acum_c = _dot_split_rhs(tri, adt_c)
    acum_r_scr[...] = _dot_split_lhs(adt_r, tri_t)
    a_last = acum_c[L - 1:L, :]
    fac = jnp.concatenate([dt_c, jnp.exp(acum_c), jnp.exp(a_last - acum_c),
                           jnp.broadcast_to(jnp.exp(a_last), (SSD_FAC_PAD, LANES))], axis=0)
    fac_hi = fac.astype(BF16)
    fac_scr[0] = fac_hi
    fac_scr[1] = (fac - fac_hi.astype(F32)).astype(BF16)
    for i, part in enumerate(_split3(acum_c)):
        acp_scr[i] = part
    lane_w = lax.broadcasted_iota(I32, (L, LANES), 1)
    first_half = lane_w < SSD_HEAD_DIM

    def tiles(ref, first, n):
        return jnp.concatenate([ref[first + q] for q in range(n)], axis=1)

    def group_body(g, carry):
        e_chan = echan_ref[g]
        ex = _dot(fac_scr[0], e_chan) + _dot(fac_scr[1], e_chan)
        dt_x, ea_x, sd_x, cd_x = ex[0:L], ex[L:2 * L], ex[2 * L:3 * L], ex[3 * L:3 * L + 1]
        e_head = ehead_ref[g]
        acum_b = _dot(acp_scr[0], e_head) + _dot(acp_scr[1], e_head) + _dot(acp_scr[2], e_head)

        def conv_silu(zx_first, conv_first, n):
            cur = tiles(zx_ref, zx_first, n)
            ext = jnp.concatenate([tiles(prev_scr, conv_first, n), cur], axis=0)
            w = tiles(cw_ref, conv_first, n)
            acc = jnp.broadcast_to(tiles(cb_ref, conv_first, n), (L, n * LANES))
            for k in range(SSD_CONV):
                back = SSD_CONV - 1 - k
                xk = cur.astype(F32) if back == 0 else _dot(shift_ref[back - 1], ext)
                acc = acc + xk * w[k:k + 1, :]
            for q in range(n):
                prev_scr[conv_first + q] = zx_ref[zx_first + q]
            return jax.nn.silu(acc)

        xs = conv_silu(x0 + n_xt * g, n_xt * g, n_xt)
        bm = conv_silu(b0 + g, cb0 + g, 1)
        cm = conv_silu(c0 + g, cc0 + g, 1)

        xdt = xs * dt_x
        cb = jnp.where(causal, _dot_nt(cm.astype(BF16), bm.astype(BF16)), 0.0)
        y_parts = []
        for pair in range(hpg // 2):
            xd = xdt[:, pair * LANES:(pair + 1) * LANES]
            y_pair = None
            for sub in range(2):
                h = 2 * pair + sub
                a_row = acum_r_scr[pl.ds(g * hpg + h, 1), :]
                seg = jnp.minimum(acum_b[:, h * L:(h + 1) * L] - a_row, 0.0)
                m_h = (cb * jnp.exp(seg)).astype(BF16)
                x_h = jnp.where(first_half if sub == 0 else jnp.logical_not(first_half), xd, 0.0)
                term = _dot(m_h, x_h.astype(BF16))
                y_pair = term if y_pair is None else y_pair + term
            y_parts.append(y_pair)
        y_diag = jnp.concatenate(y_parts, axis=1)

        st = st_scr[g]
        y_off = _dot(cm.astype(BF16), st.astype(BF16)) * ea_x
        st_scr[g] = st * cd_x + _dot_tn(bm.astype(BF16), (xdt * sd_x).astype(BF16))

        y = y_diag + y_off + xs * tiles(dskip_ref, n_xt * g, n_xt)
        y = y * jax.nn.silu(tiles(zx_ref, n_xt * g, n_xt).astype(F32))
        y = y * lax.rsqrt(jnp.mean(y * y, axis=-1, keepdims=True) + NORM_EPS)
        y = y * tiles(ng_ref, n_xt * g, n_xt)
        for q in range(n_xt):
            o_ref[n_xt * g + q] = y[:, q * LANES:(q + 1) * LANES].astype(BF16)
        return carry

    lax.fori_loop(0, SSD_GROUPS, group_body, 0)


def _ssd_chunks(zx_tiles, dt_small, dt_small_t, conv_w, conv_b, dt_bias, a_log, d_skip, norm_g):
    L = SSD_CHUNK
    nc = SEQ // L
    n_zx = SSD_MAIN // LANES
    n_conv = SSD_CONV_CH // LANES
    n_inner = SSD_D_INNER // LANES
    hpg = SSD_HEADS_PER_GROUP

    def pad_heads(v):
        return jnp.zeros((LANES,), F32).at[:SSD_HEADS].set(v)

    dtb = pad_heads(dt_bias)
    alog = pad_heads(a_log)
    cw = conv_w.reshape(SSD_CONV, n_conv, LANES).transpose(1, 0, 2)
    cb = conv_b.reshape(n_conv, 1, LANES)
    d_chan = jnp.repeat(d_skip, SSD_HEAD_DIM).reshape(n_inner, 1, LANES)
    ng = norm_g.reshape(n_inner, 1, LANES)
    t_idx = jnp.arange(L, dtype=I32)[None, :, None]
    r_idx = jnp.arange(2 * L, dtype=I32)[None, None, :]
    back = jnp.arange(1, SSD_CONV, dtype=I32)[:, None, None]
    shift = (r_idx == L + t_idx - back).astype(BF16)
    head = jnp.arange(LANES, dtype=I32)[None, :, None]
    grp = jnp.arange(SSD_GROUPS, dtype=I32)[:, None, None]
    e_chan = (head == grp * hpg + jnp.arange(SSD_GROUP_W, dtype=I32)[None, None, :] // SSD_HEAD_DIM).astype(BF16)
    e_head = (head == grp * hpg + jnp.arange(hpg * L, dtype=I32)[None, None, :] // L).astype(BF16)
    row = lambda b, c: b * nc + c
    const3 = lambda b, c: (0, 0, 0)
    const2 = lambda b, c: (0, 0)
    in_specs = [
        pl.BlockSpec((n_zx, L, LANES), lambda b, c: (0, row(b, c), 0)),
        pl.BlockSpec((L, LANES), lambda b, c: (row(b, c), 0)),
        pl.BlockSpec((LANES, L), lambda b, c: (0, row(b, c))),
        pl.BlockSpec((n_conv, SSD_CONV, LANES), const3),
        pl.BlockSpec((n_conv, 1, LANES), const3),
        pl.BlockSpec((1, LANES), const2), pl.BlockSpec((1, LANES), const2),
        pl.BlockSpec((LANES, 1), const2), pl.BlockSpec((LANES, 1), const2),
        pl.BlockSpec((n_inner, 1, LANES), const3),
        pl.BlockSpec((n_inner, 1, LANES), const3),
        pl.BlockSpec((SSD_CONV - 1, L, 2 * L), const3),
        pl.BlockSpec((SSD_GROUPS, LANES, SSD_GROUP_W), const3),
        pl.BlockSpec((SSD_GROUPS, LANES, hpg * L), const3),
    ]
    return pl.pallas_call(
        _ssd_chunk_kernel, name="ssd_chunks", grid=(BATCH, nc),
        in_specs=in_specs,
        out_specs=pl.BlockSpec((n_inner, L, LANES), lambda b, c: (0, row(b, c), 0)),
        out_shape=jax.ShapeDtypeStruct((n_inner, TOKENS, LANES), BF16),
        scratch_shapes=[pltpu.VMEM((n_conv, L, LANES), BF16),
                        pltpu.VMEM((LANES, L), F32),
                        pltpu.VMEM((SSD_GROUPS, SSD_D_STATE, SSD_GROUP_W), F32),
                        pltpu.VMEM((2, 3 * L + SSD_FAC_PAD, LANES), BF16),
                        pltpu.VMEM((3, L, LANES), BF16)],
        compiler_params=_cparams(("arbitrary", "arbitrary")),
    )(zx_tiles, dt_small, dt_small_t, cw, cb, dtb.reshape(1, LANES), alog.reshape(1, LANES),
      dtb.reshape(LANES, 1), alog.reshape(LANES, 1), d_chan, ng, shift, e_chan, e_head)


def _gather_rows(src_hbm, idx_ref, base, dst, sem, n_rows):
    def body(r, carry):
        tok = idx_ref[base + r]
        pltpu.make_async_copy(src_hbm.at[pl.ds(tok, 1), :], dst.at[pl.ds(r, 1), :], sem).start()
        return carry

    lax.fori_loop(0, n_rows, body, 0, unroll=GATHER_UNROLL)


def _start_rows(src_hbm, idx_ref, base, dst, sem, lo, hi):
    for r in range(lo, hi):
        tok = idx_ref[base + r]
        pltpu.make_async_copy(src_hbm.at[pl.ds(tok, 1), :], dst.at[pl.ds(r, 1), :], sem).start()


def _moe_ffn_kernel(te_ref, tok_ref, nact_ref, x_hbm, g_ref, wg_ref, wu_ref, wd_ref, y_ref,
                    buf, sem, wg_bf, wu_bf, wd_bf):
    i = pl.program_id(0)
    n_act = nact_ref[0]
    tm = MOE_TM
    slot = i % 2

    @pl.when(i == 0)
    def _():
        _gather_rows(x_hbm, tok_ref, 0, buf.at[0], sem.at[0], tm)

    e_cur = te_ref[i]
    e_prev = te_ref[jnp.maximum(i - 1, 0)]

    @pl.when((i == 0) | (e_cur != e_prev))
    def _():
        wg_bf[...] = wg_ref[...].astype(BF16)
        wu_bf[...] = wu_ref[...].astype(BF16)
        wd_bf[...] = wd_ref[...].astype(BF16)

    @pl.when(i >= n_act)
    def _():
        y_ref[...] = jnp.zeros(y_ref.shape, F32)

    def tile_body(prefetch_next):
        def start_next(part):
            if prefetch_next:
                lo, hi = (part * tm) // MOE_ISSUE_PARTS, ((part + 1) * tm) // MOE_ISSUE_PARTS
                _start_rows(x_hbm, tok_ref, (i + 1) * tm, buf.at[1 - slot], sem.at[1 - slot], lo, hi)

        pltpu.make_async_copy(x_hbm.at[pl.ds(0, tm), :], buf.at[slot], sem.at[slot]).wait()
        h = _rms(buf[slot], g_ref[...]).astype(BF16)
        start_next(0)
        gate = _dot(h, wg_bf[...])
        start_next(1)
        act = jax.nn.silu(gate) * _dot(h, wu_bf[...])
        start_next(2)
        y_ref[...] = _dot(act.astype(BF16), wd_bf[...])

    @pl.when(i + 1 < n_act)
    def _():
        tile_body(True)

    @pl.when(i + 1 == n_act)
    def _():
        tile_body(False)


def _moe_ffn(x, g, w_gate, w_up, w_down, layer, tile_expert, slot_token, n_active):
    grid_spec = pltpu.PrefetchScalarGridSpec(
        num_scalar_prefetch=3, grid=(MOE_TILES,),
        in_specs=[pl.BlockSpec(memory_space=pl.ANY),
                  pl.BlockSpec((1, D_MODEL), lambda i, te, tok, na: (0, 0)),
                  pl.BlockSpec((None, None, D_MODEL, MOE_D_FF), lambda i, te, tok, na: (layer, te[i], 0, 0)),
                  pl.BlockSpec((None, None, D_MODEL, MOE_D_FF), lambda i, te, tok, na: (layer, te[i], 0, 0)),
                  pl.BlockSpec((None, None, MOE_D_FF, D_MODEL), lambda i, te, tok, na: (layer, te[i], 0, 0))],
        out_specs=pl.BlockSpec((MOE_TM, D_MODEL), lambda i, te, tok, na: (i, 0)),
        scratch_shapes=[pltpu.VMEM((2, MOE_TM, D_MODEL), F32), pltpu.SemaphoreType.DMA((2,)),
                        pltpu.VMEM((D_MODEL, MOE_D_FF), BF16), pltpu.VMEM((D_MODEL, MOE_D_FF), BF16),
                        pltpu.VMEM((MOE_D_FF, D_MODEL), BF16)])
    return pl.pallas_call(
        _moe_ffn_kernel, name="moe_ffn", grid_spec=grid_spec,
        out_shape=jax.ShapeDtypeStruct((MOE_ROWS, D_MODEL), F32),
        compiler_params=_cparams(("arbitrary",)),
    )(tile_expert, slot_token, n_active, x, g.reshape(1, D_MODEL), w_gate, w_up, w_down)


def _moe_combine_kernel(pos_ref, x_ref, cw_ref, g_ref, y_hbm, o_ref, buf, sem, *, final_norm):
    i = pl.program_id(0)
    n = pl.num_programs(0)
    tm = CMB_TM
    slot = i % 2

    def issue(tile, s):
        for k in range(MOE_TOPK):
            _gather_rows(y_hbm, pos_ref, (k * (TOKENS // tm) + tile) * tm, buf.at[s, k], sem.at[s], tm)

    @pl.when(i == 0)
    def _():
        issue(0, 0)

    @pl.when(i + 1 < n)
    def _():
        issue(i + 1, 1 - slot)

    for k in range(MOE_TOPK):
        pltpu.make_async_copy(y_hbm.at[pl.ds(0, tm), :], buf.at[slot, k], sem.at[slot]).wait()
    cw = cw_ref[...]
    out = x_ref[...] + cw[:, 0:1] * buf[slot, 0] + cw[:, 1:2] * buf[slot, 1]
    if final_norm:
        out = _rms(out, g_ref[...])
    o_ref[...] = out


def _moe_combine(x, cw, y_sorted, pos_kmajor, g_final, final_norm):
    grid_spec = pltpu.PrefetchScalarGridSpec(
        num_scalar_prefetch=1, grid=(TOKENS // CMB_TM,),
        in_specs=[pl.BlockSpec((CMB_TM, D_MODEL), lambda i, pos: (i, 0)),
                  pl.BlockSpec((CMB_TM, LANES), lambda i, pos: (i, 0)),
                  pl.BlockSpec((1, D_MODEL), lambda i, pos: (0, 0)),
                  pl.BlockSpec(memory_space=pl.ANY)],
        out_specs=pl.BlockSpec((CMB_TM, D_MODEL), lambda i, pos: (i, 0)),
        scratch_shapes=[pltpu.VMEM((2, MOE_TOPK, CMB_TM, D_MODEL), F32), pltpu.SemaphoreType.DMA((2,))])
    return pl.pallas_call(
        functools.partial(_moe_combine_kernel, final_norm=final_norm), name="moe_combine", grid_spec=grid_spec,
        out_shape=jax.ShapeDtypeStruct((TOKENS, D_MODEL), F32),
        compiler_params=_cparams(("arbitrary",)),
    )(pos_kmajor, x, cw, g_final.reshape(1, D_MODEL), y_sorted)


def _moe_plan(eid):
    e = eid[:, :MOE_TOPK].reshape(-1)
    n_pairs = e.shape[0]
    onehot = (e[:, None] == jnp.arange(MOE_EXPERTS, dtype=I32)[None, :]).astype(I32)
    csum = jnp.cumsum(onehot, axis=0)
    counts = csum[-1]
    rank = jnp.take_along_axis(csum, e[:, None], axis=1)[:, 0] - 1
    padded = ((counts + MOE_TM - 1) // MOE_TM) * MOE_TM
    g_end = jnp.cumsum(padded)
    g_start = g_end - padded
    pos = g_start[e] + rank
    n_active = (g_end[-1] // MOE_TM).astype(I32)
    tile_start = jnp.arange(MOE_TILES, dtype=I32) * MOE_TM
    te = jnp.sum((g_end[None, :] <= tile_start[:, None]).astype(I32), axis=1)
    last = jnp.max(jnp.where(counts > 0, jnp.arange(MOE_EXPERTS, dtype=I32), 0))
    tile_expert = jnp.minimum(te, last)
    slot_token = jnp.zeros((MOE_ROWS,), I32).at[pos].set(jnp.arange(n_pairs, dtype=I32) // MOE_TOPK)
    pos_kmajor = pos.reshape(TOKENS, MOE_TOPK).T.reshape(-1)
    return tile_expert, slot_token, n_active.reshape(1), pos_kmajor


def _hier_moe_add(x, ln_g, w_group, b_group, w_expert, b_expert, w_gate, w_up, w_down, layer,
                  g_final, final_norm):
    eid, cw = _router(x, ln_g, w_group, b_group, w_expert, b_expert)
    tile_expert, slot_token, n_active, pos_kmajor = _moe_plan(eid)
    y_sorted = _moe_ffn(x, ln_g, w_gate, w_up, w_down, layer, tile_expert, slot_token, n_active)
    return _moe_combine(x, cw, y_sorted, pos_kmajor, g_final, final_norm)


def _rope_tables():
    pos = jnp.arange(SEQ, dtype=F32)
    inv = 1.0 / (ROPE_THETA ** (jnp.arange(0, NSA_HEAD_DIM, 2, dtype=F32) / NSA_HEAD_DIM))
    ang = pos[:, None] * inv[None, :]
    cos, sin = jnp.cos(ang), jnp.sin(ang)
    return jnp.concatenate([cos, cos], axis=1), jnp.concatenate([-sin, sin], axis=1)


def _nsa_mixer_add(x, ln_g, w_in, cmp_pe, cmp_w1, cmp_w2, w_out):
    hn, _, g_lin_t = _norm_small(x, ln_g, w_in[:, NSA_MAIN:], transposed=True)
    cos_full, sin_signed = _rope_tables()
    heads = _nsa_inproj(hn, w_in, cos_full, sin_signed)
    first_c = NSA_HEADS
    kv_c = heads[first_c:first_c + 2 * NSA_KV_HEADS]
    kv_chunks = kv_c.reshape(2, NSA_KV_HEADS, BATCH, SEQ // CMP_STRIDE, CMP_STRIDE * NSA_HEAD_DIM)
    kc_vc = _compress(kv_chunks, cmp_pe, cmp_w1, cmp_w2)
    gates_t = g_lin_t[:NSA_GATES].reshape(NSA_KV_HEADS, 3 * NSA_Q_PER_KV, TOKENS)
    o = _nsa_attention(heads, kc_vc, gates_t)
    return _matmul_resid(o, w_out, x, 512)


def _ssd_mixer_add(x, ln_g, w_in, conv_w, conv_b, dt_bias, a_log, d_skip, norm_g, w_out):
    hn, dt_small, dt_small_t = _norm_small(x, ln_g, w_in[:, SSD_MAIN:], transposed=True)
    zx_tiles = _matmul_tiles(hn, w_in, SSD_MAIN, 1024)
    y_tiles = _ssd_chunks(zx_tiles, dt_small, dt_small_t, conv_w, conv_b, dt_bias, a_log, d_skip, norm_g)
    return _matmul_resid_tiles(y_tiles, w_out, x, 512)


def kernel(x, ln_mix, ln_ffn, ln_final, nsa_w_in, nsa_cmp_pe, nsa_cmp_w1, nsa_cmp_w2, nsa_w_out,
           ssd_w_in, ssd_conv_w, ssd_conv_b, ssd_dt_bias, ssd_a_log, ssd_d, ssd_norm, ssd_w_out,
           moe_w_group, moe_b_group, moe_w_expert, moe_b_expert, moe_w_gate, moe_w_up, moe_w_down):
    h = x.reshape(TOKENS, D_MODEL)
    for i in range(DEPTH):
        j = i // N_MIXERS
        if i % N_MIXERS == 0:
            h = _nsa_mixer_add(h, ln_mix[i], nsa_w_in[j], nsa_cmp_pe[j], nsa_cmp_w1[j], nsa_cmp_w2[j],
                               nsa_w_out[j])
        else:
            h = _ssd_mixer_add(h, ln_mix[i], ssd_w_in[j], ssd_conv_w[j], ssd_conv_b[j], ssd_dt_bias[j],
                               ssd_a_log[j], ssd_d[j], ssd_norm[j], ssd_w_out[j])
        h = _hier_moe_add(h, ln_ffn[i], moe_w_group[i], moe_b_group[i], moe_w_expert[i], moe_b_expert[i],
                          moe_w_gate, moe_w_up, moe_w_down, i, ln_final, i == DEPTH - 1)
    return h.reshape(BATCH, SEQ, D_MODEL)
```

```python
import functools

import jax
import jax.numpy as jnp
from jax import lax
from jax.experimental import pallas as pl
from jax.experimental.pallas import tpu as pltpu

F32 = jnp.float32
BF16 = jnp.bfloat16
I32 = jnp.int32

D_MODEL = 2048
BATCH = 4
SEQ = 2048
TOKENS = BATCH * SEQ
DEPTH = 2
N_MIXERS = 2
NORM_EPS = 1e-6
NEG_INF = -1e30
ROPE_THETA = 10000.0

NSA_HEADS = 16
NSA_KV_HEADS = 4
NSA_HEAD_DIM = D_MODEL // NSA_HEADS
NSA_Q_PER_KV = NSA_HEADS // NSA_KV_HEADS
CMP_BLOCK = 32
CMP_STRIDE = 16
CMP_HIDDEN = 256
N_CMP = (SEQ - CMP_BLOCK) // CMP_STRIDE + 1
SLC_BLOCK = 64
SLC_TOPK = 16
SLC_LOCAL = 2
SLC_FORCE = 1e4
N_SLC = SEQ // SLC_BLOCK
WINDOW = 512
NSA_QD = NSA_HEADS * NSA_HEAD_DIM
NSA_KVD = NSA_KV_HEADS * NSA_HEAD_DIM
NSA_MAIN = NSA_QD + 6 * NSA_KVD
NSA_GATES = 3 * NSA_HEADS

SSD_D_INNER = 2 * D_MODEL
SSD_HEAD_DIM = 64
SSD_HEADS = SSD_D_INNER // SSD_HEAD_DIM
SSD_GROUPS = 8
SSD_HEADS_PER_GROUP = SSD_HEADS // SSD_GROUPS
SSD_D_STATE = 128
SSD_CONV = 4
SSD_CHUNK = 128
SSD_GROUP_W = SSD_D_INNER // SSD_GROUPS
SSD_BC = SSD_GROUPS * SSD_D_STATE
SSD_CONV_CH = SSD_D_INNER + 2 * SSD_BC
SSD_MAIN = SSD_D_INNER + SSD_CONV_CH
SSD_FAC_PAD = 16

MOE_GROUPS = 4
MOE_EPG = 8
MOE_EXPERTS = MOE_GROUPS * MOE_EPG
MOE_TOPK = 2
MOE_D_FF = 512

LANES = 128
VMEM_LIMIT = 56 * 1024 * 1024

NORM_TM = 256
MM_TM = 512
OUT_TM = 256
ATT_TQ = 128
ATT_CHUNK_TILES = 4
ATT_ONES = 16
MOE_TM = 256
MOE_TILES = (TOKENS * MOE_TOPK) // MOE_TM + MOE_EXPERTS
MOE_ROWS = MOE_TILES * MOE_TM
CMB_TM = 128
GATHER_UNROLL = 8
MOE_ISSUE_PARTS = 3


def _cparams(sem):
    return pltpu.CompilerParams(dimension_semantics=sem, vmem_limit_bytes=VMEM_LIMIT)


def _split3(x):
    hi = x.astype(BF16)
    r1 = x - hi.astype(F32)
    mid = r1.astype(BF16)
    lo = (r1 - mid.astype(F32)).astype(BF16)
    return hi, mid, lo


def _dot(a, b):
    return jnp.dot(a, b, preferred_element_type=F32)


def _dot_nt(a, b):
    return lax.dot_general(a, b, (((1,), (1,)), ((), ())), preferred_element_type=F32)


def _dot_tn(a, b):
    return lax.dot_general(a, b, (((0,), (0,)), ((), ())), preferred_element_type=F32)


def _dot_split_lhs(x, m_bf16):
    hi, mid, lo = _split3(x)
    return _dot(hi, m_bf16) + _dot(mid, m_bf16) + _dot(lo, m_bf16)


def _dot_split_rhs(m_bf16, x):
    hi, mid, lo = _split3(x)
    return _dot(m_bf16, hi) + _dot(m_bf16, mid) + _dot(m_bf16, lo)


def _dot_x3(a, w):
    a_hi = a.astype(BF16)
    a_lo = (a - a_hi.astype(F32)).astype(BF16)
    w_hi = w.astype(BF16)
    w_lo = (w - w_hi.astype(F32)).astype(BF16)
    return _dot(a_hi, w_hi) + _dot(a_hi, w_lo) + _dot(a_lo, w_hi)


def _dot_x3_nt(a, w):
    a_hi = a.astype(BF16)
    a_lo = (a - a_hi.astype(F32)).astype(BF16)
    w_hi = w.astype(BF16)
    w_lo = (w - w_hi.astype(F32)).astype(BF16)
    return _dot_nt(a_hi, w_hi) + _dot_nt(a_hi, w_lo) + _dot_nt(a_lo, w_hi)


def _rms(x, g):
    y = x * lax.rsqrt(jnp.mean(x * x, axis=-1, keepdims=True) + NORM_EPS)
    return y * g


def _norm_small_kernel(x_ref, g_ref, ws_ref, hn_ref, small_ref):
    y = _rms(x_ref[...], g_ref[...])
    hn_ref[...] = y.astype(BF16)
    small_ref[...] = _dot_x3(y, ws_ref[...])


def _norm_small_t_kernel(x_ref, g_ref, ws_ref, hn_ref, small_ref, small_t_ref):
    y = _rms(x_ref[...], g_ref[...])
    hn_ref[...] = y.astype(BF16)
    small = _dot_x3(y, ws_ref[...])
    small_ref[...] = small
    small_t_ref[...] = small.T


def _norm_small(x, g, w_small, transposed=False):
    n = w_small.shape[1]
    ws = jnp.zeros((D_MODEL, LANES), F32).at[:, :n].set(w_small)
    grid = (TOKENS // NORM_TM,)
    x_spec = pl.BlockSpec((NORM_TM, D_MODEL), lambda i: (i, 0))
    g_spec = pl.BlockSpec((1, D_MODEL), lambda i: (0, 0))
    w_spec = pl.BlockSpec((D_MODEL, LANES), lambda i: (0, 0))
    hn_spec = pl.BlockSpec((NORM_TM, D_MODEL), lambda i: (i, 0))
    sm_spec = pl.BlockSpec((NORM_TM, LANES), lambda i: (i, 0))
    hn_shape = jax.ShapeDtypeStruct((TOKENS, D_MODEL), BF16)
    sm_shape = jax.ShapeDtypeStruct((TOKENS, LANES), F32)
    if not transposed:
        return pl.pallas_call(
            _norm_small_kernel, name="norm_small", grid=grid,
            in_specs=[x_spec, g_spec, w_spec],
            out_specs=[hn_spec, sm_spec],
            out_shape=[hn_shape, sm_shape],
            compiler_params=_cparams(("parallel",)),
        )(x, g.reshape(1, D_MODEL), ws)
    smt_spec = pl.BlockSpec((LANES, NORM_TM), lambda i: (0, i))
    smt_shape = jax.ShapeDtypeStruct((LANES, TOKENS), F32)
    return pl.pallas_call(
        _norm_small_t_kernel, name="norm_small_t", grid=grid,
        in_specs=[x_spec, g_spec, w_spec],
        out_specs=[hn_spec, sm_spec, smt_spec],
        out_shape=[hn_shape, sm_shape, smt_shape],
        compiler_params=_cparams(("parallel",)),
    )(x, g.reshape(1, D_MODEL), ws)


def _router_kernel(x_ref, g_ref, ws_ref, b_ref, eid_ref, cw_ref):
    y = _rms(x_ref[...], g_ref[...])
    logits = _dot_x3(y, ws_ref[...]) + b_ref[...]
    lane = lax.broadcasted_iota(I32, logits.shape, 1)
    big = jnp.int32(LANES)
    neg = -jnp.inf
    gl = jnp.where(lane < MOE_GROUPS, logits, neg)
    gmax = jnp.max(gl, axis=-1, keepdims=True)
    gsum = jnp.sum(jnp.exp(gl - gmax), axis=-1, keepdims=True)
    g_w = 1.0 / gsum
    g_sel = jnp.min(jnp.where(gl == gmax, lane, big), axis=-1, keepdims=True)
    lo = MOE_GROUPS + g_sel * MOE_EPG
    el = jnp.where((lane >= lo) & (lane < lo + MOE_EPG), logits, neg)
    v1 = jnp.max(el, axis=-1, keepdims=True)
    i1 = jnp.min(jnp.where(el == v1, lane, big), axis=-1, keepdims=True)
    el2 = jnp.where(lane == i1, neg, el)
    v2 = jnp.max(el2, axis=-1, keepdims=True)
    i2 = jnp.min(jnp.where(el2 == v2, lane, big), axis=-1, keepdims=True)
    e2 = jnp.exp(v2 - v1)
    den = 1.0 + e2
    w1 = (1.0 / den) * g_w
    w2 = (e2 / den) * g_w
    eid_ref[...] = jnp.where(lane == 0, i1 - MOE_GROUPS, jnp.where(lane == 1, i2 - MOE_GROUPS, 0))
    cw_ref[...] = jnp.where(lane == 0, w1, jnp.where(lane == 1, w2, 0.0))


def _router(x, g, w_group, b_group, w_expert, b_expert):
    n = MOE_GROUPS + MOE_EXPERTS
    ws = jnp.zeros((D_MODEL, LANES), F32).at[:, :n].set(jnp.concatenate([w_group, w_expert], axis=1))
    bs = jnp.zeros((1, LANES), F32).at[0, :n].set(jnp.concatenate([b_group, b_expert]))
    return pl.pallas_call(
        _router_kernel, name="router", grid=(TOKENS // NORM_TM,),
        in_specs=[pl.BlockSpec((NORM_TM, D_MODEL), lambda i: (i, 0)),
                  pl.BlockSpec((1, D_MODEL), lambda i: (0, 0)),
                  pl.BlockSpec((D_MODEL, LANES), lambda i: (0, 0)),
                  pl.BlockSpec((1, LANES), lambda i: (0, 0))],
        out_specs=[pl.BlockSpec((NORM_TM, LANES), lambda i: (i, 0)),
                   pl.BlockSpec((NORM_TM, LANES), lambda i: (i, 0))],
        out_shape=[jax.ShapeDtypeStruct((TOKENS, LANES), I32),
                   jax.ShapeDtypeStruct((TOKENS, LANES), F32)],
        compiler_params=_cparams(("parallel",)),
    )(x, g.reshape(1, D_MODEL), ws, bs)


def _inproj_heads_kernel(a_ref, w_ref, cos_ref, sin_ref, o_ref, wbf_ref):
    j = pl.program_id(0)

    @pl.when(pl.program_id(1) == 0)
    def _():
        wbf_ref[...] = w_ref[...].astype(BF16)

    acc = _dot(a_ref[...], wbf_ref[...])
    q_tiles = NSA_QD // acc.shape[1]
    c = cos_ref[...]
    s = sin_ref[...]

    def head(h):
        return acc[:, h * NSA_HEAD_DIM:(h + 1) * NSA_HEAD_DIM]

    def rotary(xh):
        return (xh * c + pltpu.roll(xh, NSA_HEAD_DIM // 2, 1) * s).astype(BF16)

    for h in range(NSA_KV_HEADS):
        o_ref[h] = rotary(head(h))

    @pl.when(j < q_tiles)
    def _():
        for h in range(NSA_KV_HEADS, 2 * NSA_KV_HEADS):
            o_ref[h] = rotary(head(h))

    @pl.when(j >= q_tiles)
    def _():
        for h in range(NSA_KV_HEADS, 2 * NSA_KV_HEADS):
            o_ref[h] = head(h).astype(BF16)


def _nsa_inproj(hn, w_in, cos_full, sin_signed):
    tn = 2 * NSA_KVD
    n_heads_tile = tn // NSA_HEAD_DIM
    s_tiles = SEQ // MM_TM
    return pl.pallas_call(
        _inproj_heads_kernel, name="nsa_inproj", grid=(NSA_MAIN // tn, TOKENS // MM_TM),
        in_specs=[pl.BlockSpec((MM_TM, D_MODEL), lambda j, i: (i, 0)),
                  pl.BlockSpec((D_MODEL, tn), lambda j, i: (0, j)),
                  pl.BlockSpec((MM_TM, NSA_HEAD_DIM), lambda j, i: (i % s_tiles, 0)),
                  pl.BlockSpec((MM_TM, NSA_HEAD_DIM), lambda j, i: (i % s_tiles, 0))],
        out_specs=pl.BlockSpec((n_heads_tile, MM_TM, NSA_HEAD_DIM), lambda j, i: (j, i, 0)),
        out_shape=jax.ShapeDtypeStruct((NSA_MAIN // NSA_HEAD_DIM, TOKENS, NSA_HEAD_DIM), BF16),
        scratch_shapes=[pltpu.VMEM((D_MODEL, tn), BF16)],
        compiler_params=_cparams(("arbitrary", "arbitrary")),
    )(hn, w_in, cos_full, sin_signed)


def _matmul_tiles_kernel(a_ref, w_ref, o_ref, wbf_ref):
    @pl.when(pl.program_id(1) == 0)
    def _():
        wbf_ref[...] = w_ref[...].astype(BF16)

    acc = _dot(a_ref[...], wbf_ref[...])
    for t in range(acc.shape[1] // LANES):
        o_ref[t] = acc[:, t * LANES:(t + 1) * LANES].astype(o_ref.dtype)


def _matmul_tiles(a, w, n_cols, tn):
    k = a.shape[1]
    return pl.pallas_call(
        _matmul_tiles_kernel, name="matmul_tiles", grid=(n_cols // tn, TOKENS // MM_TM),
        in_specs=[pl.BlockSpec((MM_TM, k), lambda j, i: (i, 0)),
                  pl.BlockSpec((k, tn), lambda j, i: (0, j))],
        out_specs=pl.BlockSpec((tn // LANES, MM_TM, LANES), lambda j, i: (j, i, 0)),
        out_shape=jax.ShapeDtypeStruct((n_cols // LANES, TOKENS, LANES), BF16),
        scratch_shapes=[pltpu.VMEM((k, tn), BF16)],
        compiler_params=_cparams(("arbitrary", "arbitrary")),
    )(a, w)


def _outproj_kernel(a_ref, w_ref, r_ref, o_ref):
    if len(a_ref.shape) == 3:
        a = jnp.concatenate([a_ref[t] for t in range(a_ref.shape[0])], axis=1)
    else:
        a = a_ref[...]
    o_ref[...] = r_ref[...] + _dot(a, w_ref[...])


def _outproj_resid(a, w, resid):
    tiled = a.ndim == 3
    k, n = w.shape
    a_spec = (pl.BlockSpec((k // LANES, OUT_TM, LANES), lambda i: (0, i, 0)) if tiled
              else pl.BlockSpec((OUT_TM, k), lambda i: (i, 0)))
    return pl.pallas_call(
        _outproj_kernel, name="outproj_resid", grid=(TOKENS // OUT_TM,),
        in_specs=[a_spec,
                  pl.BlockSpec((k, n), lambda i: (0, 0)),
                  pl.BlockSpec((OUT_TM, n), lambda i: (i, 0))],
        out_specs=pl.BlockSpec((OUT_TM, n), lambda i: (i, 0)),
        out_shape=jax.ShapeDtypeStruct((TOKENS, n), F32),
        compiler_params=_cparams(("parallel",)),
    )(a, w.astype(BF16), resid)


def _compress_kernel(x_ref, pe_ref, w1_ref, w2_ref, o_ref):
    half = CMP_STRIDE * NSA_HEAD_DIM
    x = x_ref[0, 0, 0]
    w1 = w1_ref[0].astype(BF16)
    top = _dot(x, w1[:half])
    bot = _dot(x, w1[half:])
    pe = jnp.broadcast_to(pe_ref[0], (8, 2 * half)).astype(BF16)
    pe_bias = _dot(pe, w1)[0:1]
    hid = top + pltpu.roll(bot, bot.shape[0] - 1, 0) + pe_bias
    act = jax.nn.gelu(hid)
    o_ref[0, 0, 0] = _dot(act.astype(BF16), w2_ref[0].astype(BF16))


def _compress(kv_chunks, pe, w1, w2):
    n_chunk = SEQ // CMP_STRIDE
    feat = CMP_STRIDE * NSA_HEAD_DIM
    return pl.pallas_call(
        _compress_kernel, name="nsa_compress", grid=(2, NSA_KV_HEADS, BATCH),
        in_specs=[pl.BlockSpec((1, 1, 1, n_chunk, feat), lambda a, g, b: (a, g, b, 0, 0)),
                  pl.BlockSpec((1, 1, 2 * feat), lambda a, g, b: (a, 0, 0)),
                  pl.BlockSpec((1, 2 * feat, CMP_HIDDEN), lambda a, g, b: (a, 0, 0)),
                  pl.BlockSpec((1, CMP_HIDDEN, NSA_HEAD_DIM), lambda a, g, b: (a, 0, 0))],
        out_specs=pl.BlockSpec((1, 1, 1, n_chunk, NSA_HEAD_DIM), lambda a, g, b: (a, g, b, 0, 0)),
        out_shape=jax.ShapeDtypeStruct((2, NSA_KV_HEADS, BATCH, n_chunk, NSA_HEAD_DIM), F32),
        compiler_params=_cparams(("parallel", "parallel", "parallel")),
    )(kv_chunks, pe.reshape(2, 1, 2 * feat), w1, w2)


def _nsa_attn_kernel(q_ref, ks_ref, vs_ref, kw_ref, vw_ref, kc_ref, vc_ref, gate_ref, o_ref,
                     vst_scr, vwt_scr, sel_scr, acc_scr):
    qi = pl.program_id(2)
    tq = ATT_TQ
    tk = ATT_TQ
    dh = NSA_HEAD_DIM
    r_heads = NSA_Q_PER_KV
    n_kt = SEQ // tk
    scale = dh ** -0.5

    @pl.when(qi == 0)
    def _():
        ones = jnp.ones((ATT_ONES, tk), BF16)
        for kt in range(n_kt):
            rows = slice(kt * tk, (kt + 1) * tk)
            vst_scr[kt, 0:dh, :] = vs_ref[0, rows, :].astype(F32).T.astype(BF16)
            vwt_scr[kt, 0:dh, :] = vw_ref[0, rows, :].astype(F32).T.astype(BF16)
            vst_scr[kt, dh:dh + ATT_ONES, :] = ones
            vwt_scr[kt, dh:dh + ATT_ONES, :] = ones

    q_t = jnp.concatenate([q_ref[r].astype(F32).T for r in range(r_heads)], axis=1).astype(BF16)
    sub = lax.broadcasted_iota(I32, (tk, tq), 0)
    t_pos = qi * tq + lax.broadcasted_iota(I32, (tk, tq), 1)

    def tile4(a):
        return jnp.concatenate([a] * r_heads, axis=1)

    kc = kc_ref[0, 0, 0].astype(BF16)
    vc = vc_ref[0, 0, 0].astype(BF16)
    ok_c = jnp.where(sub * CMP_STRIDE + CMP_BLOCK - 1 <= t_pos, jnp.where(sub < N_CMP, 1.0, 0.0), 0.0)
    ok_c4 = tile4(ok_c)
    s_c = _dot(kc, q_t) * scale + (ok_c4 - 1.0) * (-NEG_INF)
    e_c = jnp.exp(s_c - jnp.max(s_c, axis=0, keepdims=True))
    p_c = (e_c / jnp.sum(e_c, axis=0, keepdims=True)) * ok_c4
    o_cmp = _dot_tn(vc, p_c.astype(BF16))
    p_sum = p_c[:, 0:tq]
    for r in range(1, r_heads):
        p_sum = p_sum + p_c[:, r * tq:(r + 1) * tq]

    blk_row = lax.broadcasted_iota(I32, (LANES, LANES), 0)
    cmp_col = lax.broadcasted_iota(I32, (LANES, LANES), 1)
    s_start = blk_row * SLC_BLOCK
    c_start = cmp_col * CMP_STRIDE
    ov_t = jnp.maximum(jnp.minimum(c_start + CMP_BLOCK, s_start + SLC_BLOCK)
                       - jnp.maximum(c_start, s_start), 0).astype(F32) / CMP_BLOCK
    ov_t = jnp.where(blk_row < N_SLC, ov_t, 0.0).astype(BF16)
    imp = _dot_split_rhs(ov_t, p_sum)[0:N_SLC]
    j_blk = lax.broadcasted_iota(I32, (N_SLC, tq), 0)
    dist = (qi * tq + lax.broadcasted_iota(I32, (N_SLC, tq), 1)) // SLC_BLOCK - j_blk
    imp = jnp.where(j_blk == 0, SLC_FORCE, jnp.where(dist < 0, imp, jnp.where(dist < SLC_LOCAL, SLC_FORCE, imp)))
    imp = jnp.where(dist >= 0, imp, -jnp.inf)
    cnt = jnp.zeros((N_SLC, tq), I32)
    for k in range(N_SLC):
        row_k = imp[k:k + 1, :]
        tie = jnp.where(j_blk > k, 1, 0)
        cnt = cnt + jnp.where(row_k > imp, 1, jnp.where(row_k == imp, tie, 0))
    sel = jnp.where(cnt < min(SLC_TOPK, N_SLC), 1.0, 0.0)
    for j in range(N_SLC):
        sel_scr[8 * j:8 * j + 8, :] = jnp.broadcast_to(sel[j:j + 1, :], (8, tq))

    def scores(k, ok):
        return _dot(k, q_t) * scale + tile4((ok - 1.0) * (-NEG_INF))

    def weighted_values(vt_scr, kt0, pr, n_tiles):
        out = None
        for u in range(n_tiles):
            term = _dot(vt_scr[kt0 + u], pr[u * tk:(u + 1) * tk].astype(BF16))
            out = term if out is None else out + term
        return out

    n_ct = ATT_CHUNK_TILES
    ck = n_ct * tk
    sub_c = lax.broadcasted_iota(I32, (ck, tq), 0)
    t_pos_c = qi * tq + lax.broadcasted_iota(I32, (ck, tq), 1)
    blocks_per_chunk = ck // SLC_BLOCK
    acc_scr[...] = jnp.zeros(acc_scr.shape, F32)

    def slc_body(c, m_old):
        start = pl.multiple_of(c * ck, ck)
        k = ks_ref[0, pl.ds(start, ck), :]
        rows8 = sel_scr[pl.ds(pl.multiple_of(c * (8 * blocks_per_chunk), 8 * blocks_per_chunk),
                              8 * blocks_per_chunk), :]
        picked = jnp.concatenate(
            [rows8[8 * u:8 * u + 8] for u in range(blocks_per_chunk) for _ in range(SLC_BLOCK // 8)], axis=0)
        sc = scores(k, jnp.where(start + sub_c <= t_pos_c, picked, 0.0))
        m_new = jnp.maximum(m_old, jnp.max(sc, axis=0, keepdims=True))
        alpha = jnp.exp(m_old - m_new)
        pr = jnp.exp(sc - m_new)
        acc_scr[...] = alpha * acc_scr[...] + weighted_values(vst_scr, c * n_ct, pr, n_ct)
        return m_new

    lax.fori_loop(0, (qi + n_ct) // n_ct, slc_body, jnp.full((1, r_heads * tq), NEG_INF, F32))
    acc = acc_scr[...]
    o_slc = acc[0:dh] / acc[dh:dh + 1]

    n_wt = WINDOW // tk + 1
    kt0 = jnp.maximum(qi - WINDOW // tk, 0)
    w_start = pl.multiple_of(kt0 * tk, tk)
    key_w = w_start + lax.broadcasted_iota(I32, (n_wt * tk, tq), 0)
    t_pos_w = qi * tq + lax.broadcasted_iota(I32, (n_wt * tk, tq), 1)
    ok_w = jnp.where(key_w <= t_pos_w, jnp.where(key_w > t_pos_w - WINDOW, 1.0, 0.0), 0.0)
    sc_w = scores(kw_ref[0, pl.ds(w_start, n_wt * tk), :], ok_w)
    pr_w = jnp.exp(sc_w - jnp.max(sc_w, axis=0, keepdims=True))
    acc_w = weighted_values(vwt_scr, kt0, pr_w, n_wt)
    o_win = acc_w[0:dh] / acc_w[dh:dh + 1]

    gate = jax.nn.sigmoid(gate_ref[0])
    for r in range(r_heads):
        cols = slice(r * tq, (r + 1) * tq)
        o = (gate[3 * r:3 * r + 1] * o_cmp[:, cols] + gate[3 * r + 1:3 * r + 2] * o_slc[:, cols]
             + gate[3 * r + 2:3 * r + 3] * o_win[:, cols])
        o_ref[:, r * dh:(r + 1) * dh] = o.T.astype(BF16)


def _nsa_attention(heads, kc_vc, gates):
    tq = ATT_TQ
    nq = SEQ // tq
    r = NSA_Q_PER_KV
    g_heads = NSA_KV_HEADS
    q_spec = pl.BlockSpec((r, tq, NSA_HEAD_DIM), lambda b, g, i: (g, b * nq + i, 0))

    def kv_spec(first_head):
        return pl.BlockSpec((1, SEQ, NSA_HEAD_DIM), lambda b, g, i: (first_head + g, b, 0))

    first = NSA_HEADS
    specs = [q_spec,
             kv_spec(first + 2 * g_heads), kv_spec(first + 3 * g_heads),
             kv_spec(first + 4 * g_heads), kv_spec(first + 5 * g_heads),
             pl.BlockSpec((1, 1, 1, SEQ // CMP_STRIDE, NSA_HEAD_DIM), lambda b, g, i: (0, g, b, 0, 0)),
             pl.BlockSpec((1, 1, 1, SEQ // CMP_STRIDE, NSA_HEAD_DIM), lambda b, g, i: (1, g, b, 0, 0)),
             pl.BlockSpec((1, 3 * r, tq), lambda b, g, i: (g, 0, b * nq + i))]
    vt_shape = (SEQ // tq, NSA_HEAD_DIM + ATT_ONES, tq)
    return pl.pallas_call(
        _nsa_attn_kernel, name="nsa_attn", grid=(BATCH, g_heads, nq),
        in_specs=specs,
        out_specs=pl.BlockSpec((tq, r * NSA_HEAD_DIM), lambda b, g, i: (b * nq + i, g)),
        out_shape=jax.ShapeDtypeStruct((TOKENS, NSA_QD), BF16),
        scratch_shapes=[pltpu.VMEM(vt_shape, BF16), pltpu.VMEM(vt_shape, BF16),
                        pltpu.VMEM((8 * N_SLC, tq), F32),
                        pltpu.VMEM((NSA_HEAD_DIM + ATT_ONES, r * tq), F32)],
        compiler_params=_cparams(("arbitrary", "arbitrary", "arbitrary")),
    )(heads, heads, heads, heads, heads, kc_vc, kc_vc, gates)


def _ssd_chunk_kernel(zx_ref, dtc_ref, dtr_ref, cw_ref, cb_ref, dtb_c_ref, alog_c_ref, dtb_r_ref, alog_r_ref,
                      dskip_ref, ng_ref, shift_ref, echan_ref, ehead_ref, o_ref,
                      prev_scr, acum_r_scr, st_scr, fac_scr, acp_scr):
    chunk = pl.program_id(1)
    L = SSD_CHUNK
    W = SSD_GROUP_W
    hpg = SSD_HEADS_PER_GROUP
    n_xt = W // LANES
    x0 = SSD_D_INNER // LANES
    b0 = 2 * SSD_D_INNER // LANES
    c0 = b0 + SSD_GROUPS
    cb0 = SSD_D_INNER // LANES
    cc0 = cb0 + SSD_GROUPS

    @pl.when(chunk == 0)
    def _():
        prev_scr[...] = jnp.zeros(prev_scr.shape, BF16)
        st_scr[...] = jnp.zeros(st_scr.shape, F32)

    dt_c = jax.nn.softplus(dtc_ref[...] + dtb_c_ref[...])
    adt_c = dt_c * (-jnp.exp(alog_c_ref[...]))
    dt_r = jax.nn.softplus(dtr_ref[...] + dtb_r_ref[...])
    adt_r = dt_r * (-jnp.exp(alog_r_ref[...]))
    row = lax.broadcasted_iota(I32, (L, L), 0)
    col = lax.broadcasted_iota(I32, (L, L), 1)
    causal = row >= col
    tri = jnp.where(causal, 1.0, 0.0).astype(BF16)
    tri_t = jnp.where(col >= row, 1.0, 0.0).astype(BF16)
    acum_c = _dot_split_rhs(tri, adt_c)
    acum_r_scr[...] = _dot_split_lhs(adt_r, tri_t)
    a_last = acum_c[L - 1:L, :]
    fac = jnp.concatenate([dt_c, jnp.exp(acum_c), jnp.exp(a_last - acum_c),
                           jnp.broadcast_to(jnp.exp(a_last), (SSD_FAC_PAD, LANES))], axis=0)
    fac_hi = fac.astype(BF16)
    fac_scr[0] = fac_hi
    fac_scr[1] = (fac - fac_hi.astype(F32)).astype(BF16)
    for i, part in enumerate(_split3(acum_c)):
        acp_scr[i] = part
    lane_w = lax.broadcasted_iota(I32, (L, LANES), 1)
    first_half = lane_w < SSD_HEAD_DIM

    def tiles(ref, first, n):
        return jnp.concatenate([ref[first + q] for q in range(n)], axis=1)

    def group_body(g, carry):
        e_chan = echan_ref[g]
        ex = _dot(fac_scr[0], e_chan) + _dot(fac_scr[1], e_chan)
        dt_x, ea_x, sd_x, cd_x = ex[0:L], ex[L:2 * L], ex[2 * L:3 * L], ex[3 * L:3 * L + 1]
        e_head = ehead_ref[g]
        acum_b = _dot(acp_scr[0], e_head) + _dot(acp_scr[1], e_head) + _dot(acp_scr[2], e_head)

        def conv_silu(zx_first, conv_first, n):
            cur = tiles(zx_ref, zx_first, n)
            ext = jnp.concatenate([tiles(prev_scr, conv_first, n), cur], axis=0)
            w = tiles(cw_ref, conv_first, n)
            acc = jnp.broadcast_to(tiles(cb_ref, conv_first, n), (L, n * LANES))
            for k in range(SSD_CONV):
                back = SSD_CONV - 1 - k
                xk = cur.astype(F32) if back == 0 else _dot(shift_ref[back - 1], ext)
                acc = acc + xk * w[k:k + 1, :]
            for q in range(n):
                prev_scr[conv_first + q] = zx_ref[zx_first + q]
            return jax.nn.silu(acc)

        xs = conv_silu(x0 + n_xt * g, n_xt * g, n_xt)
        bm = conv_silu(b0 + g, cb0 + g, 1)
        cm = conv_silu(c0 + g, cc0 + g, 1)

        xdt = xs * dt_x
        cb = jnp.where(causal, _dot_nt(cm.astype(BF16), bm.astype(BF16)), 0.0)
        y_parts = []
        for pair in range(hpg // 2):
            xd = xdt[:, pair * LANES:(pair + 1) * LANES]
            y_pair = None
            for sub in range(2):
                h = 2 * pair + sub
                a_row = acum_r_scr[pl.ds(g * hpg + h, 1), :]
                seg = jnp.minimum(acum_b[:, h * L:(h + 1) * L] - a_row, 0.0)
                m_h = (cb * jnp.exp(seg)).astype(BF16)
                x_h = jnp.where(first_half if sub == 0 else jnp.logical_not(first_half), xd, 0.0)
                term = _dot(m_h, x_h.astype(BF16))
                y_pair = term if y_pair is None else y_pair + term
            y_parts.append(y_pair)
        y_diag = jnp.concatenate(y_parts, axis=1)

        st = st_scr[g]
        y_off = _dot(cm.astype(BF16), st.astype(BF16)) * ea_x
        st_scr[g] = st * cd_x + _dot_tn(bm.astype(BF16), (xdt * sd_x).astype(BF16))

        y = y_diag + y_off + xs * tiles(dskip_ref, n_xt * g, n_xt)
        y = y * jax.nn.silu(tiles(zx_ref, n_xt * g, n_xt).astype(F32))
        y = y * lax.rsqrt(jnp.mean(y * y, axis=-1, keepdims=True) + NORM_EPS)
        y = y * tiles(ng_ref, n_xt * g, n_xt)
        for q in range(n_xt):
            o_ref[n_xt * g + q] = y[:, q * LANES:(q + 1) * LANES].astype(BF16)
        return carry

    lax.fori_loop(0, SSD_GROUPS, group_body, 0)


def _ssd_chunks(zx_tiles, dt_small, dt_small_t, conv_w, conv_b, dt_bias, a_log, d_skip, norm_g):
    L = SSD_CHUNK
    nc = SEQ // L
    n_zx = SSD_MAIN // LANES
    n_conv = SSD_CONV_CH // LANES
    n_inner = SSD_D_INNER // LANES
    hpg = SSD_HEADS_PER_GROUP

    def pad_heads(v):
        return jnp.zeros((LANES,), F32).at[:SSD_HEADS].set(v)

    dtb = pad_heads(dt_bias)
    alog = pad_heads(a_log)
    cw = conv_w.reshape(SSD_CONV, n_conv, LANES).transpose(1, 0, 2)
    cb = conv_b.reshape(n_conv, 1, LANES)
    d_chan = jnp.repeat(d_skip, SSD_HEAD_DIM).reshape(n_inner, 1, LANES)
    ng = norm_g.reshape(n_inner, 1, LANES)
    t_idx = jnp.arange(L, dtype=I32)[None, :, None]
    r_idx = jnp.arange(2 * L, dtype=I32)[None, None, :]
    back = jnp.arange(1, SSD_CONV, dtype=I32)[:, None, None]
    shift = (r_idx == L + t_idx - back).astype(BF16)
    head = jnp.arange(LANES, dtype=I32)[None, :, None]
    grp = jnp.arange(SSD_GROUPS, dtype=I32)[:, None, None]
    e_chan = (head == grp * hpg + jnp.arange(SSD_GROUP_W, dtype=I32)[None, None, :] // SSD_HEAD_DIM).astype(BF16)
    e_head = (head == grp * hpg + jnp.arange(hpg * L, dtype=I32)[None, None, :] // L).astype(BF16)
    row = lambda b, c: b * nc + c
    const3 = lambda b, c: (0, 0, 0)
    const2 = lambda b, c: (0, 0)
    in_specs = [
        pl.BlockSpec((n_zx, L, LANES), lambda b, c: (0, row(b, c), 0)),
        pl.BlockSpec((L, LANES), lambda b, c: (row(b, c), 0)),
        pl.BlockSpec((LANES, L), lambda b, c: (0, row(b, c))),
        pl.BlockSpec((n_conv, SSD_CONV, LANES), const3),
        pl.BlockSpec((n_conv, 1, LANES), const3),
        pl.BlockSpec((1, LANES), const2), pl.BlockSpec((1, LANES), const2),
        pl.BlockSpec((LANES, 1), const2), pl.BlockSpec((LANES, 1), const2),
        pl.BlockSpec((n_inner, 1, LANES), const3),
        pl.BlockSpec((n_inner, 1, LANES), const3),
        pl.BlockSpec((SSD_CONV - 1, L, 2 * L), const3),
        pl.BlockSpec((SSD_GROUPS, LANES, SSD_GROUP_W), const3),
        pl.BlockSpec((SSD_GROUPS, LANES, hpg * L), const3),
    ]
    return pl.pallas_call(
        _ssd_chunk_kernel, name="ssd_chunks", grid=(BATCH, nc),
        in_specs=in_specs,
        out_specs=pl.BlockSpec((n_inner, L, LANES), lambda b, c: (0, row(b, c), 0)),
        out_shape=jax.ShapeDtypeStruct((n_inner, TOKENS, LANES), BF16),
        scratch_shapes=[pltpu.VMEM((n_conv, L, LANES), BF16),
                        pltpu.VMEM((LANES, L), F32),
                        pltpu.VMEM((SSD_GROUPS, SSD_D_STATE, SSD_GROUP_W), F32),
                        pltpu.VMEM((2, 3 * L + SSD_FAC_PAD, LANES), BF16),
                        pltpu.VMEM((3, L, LANES), BF16)],
        compiler_params=_cparams(("arbitrary", "arbitrary")),
    )(zx_tiles, dt_small, dt_small_t, cw, cb, dtb.reshape(1, LANES), alog.reshape(1, LANES),
      dtb.reshape(LANES, 1), alog.reshape(LANES, 1), d_chan, ng, shift, e_chan, e_head)


def _gather_rows(src_hbm, idx_ref, base, dst, sem, n_rows):
    def body(r, carry):
        tok = idx_ref[base + r]
        pltpu.make_async_copy(src_hbm.at[pl.ds(tok, 1), :], dst.at[pl.ds(r, 1), :], sem).start()
        return carry

    lax.fori_loop(0, n_rows, body, 0, unroll=GATHER_UNROLL)


def _start_rows(src_hbm, idx_ref, base, dst, sem, lo, hi):
    for r in range(lo, hi):
        tok = idx_ref[base + r]
        pltpu.make_async_copy(src_hbm.at[pl.ds(tok, 1), :], dst.at[pl.ds(r, 1), :], sem).start()


def _moe_ffn_kernel(te_ref, pos_ref, nact_ref, x_hbm, g_ref, wg_ref, wu_ref, wd_ref, y_ref,
                    tok_ref, buf, sem, wg_bf, wu_bf, wd_bf):
    i = pl.program_id(0)
    n_act = nact_ref[0]
    tm = MOE_TM
    slot = i % 2

    @pl.when(i == 0)
    def _():
        def clear(q, carry):
            tok_ref[q] = 0
            return carry

        def place(p, carry):
            tok_ref[pos_ref[p]] = p % TOKENS
            return carry

        lax.fori_loop(0, MOE_ROWS, clear, 0, unroll=GATHER_UNROLL)
        lax.fori_loop(0, TOKENS * MOE_TOPK, place, 0, unroll=GATHER_UNROLL)
        _gather_rows(x_hbm, tok_ref, 0, buf.at[0], sem.at[0], tm)

    e_cur = te_ref[i]
    e_prev = te_ref[jnp.maximum(i - 1, 0)]

    @pl.when((i == 0) | (e_cur != e_prev))
    def _():
        wg_bf[...] = wg_ref[...].astype(BF16)
        wu_bf[...] = wu_ref[...].astype(BF16)
        wd_bf[...] = wd_ref[...].astype(BF16)

    @pl.when(i >= n_act)
    def _():
        y_ref[...] = jnp.zeros(y_ref.shape, F32)

    def tile_body(prefetch_next):
        def start_next(part):
            if prefetch_next:
                lo, hi = (part * tm) // MOE_ISSUE_PARTS, ((part + 1) * tm) // MOE_ISSUE_PARTS
                _start_rows(x_hbm, tok_ref, (i + 1) * tm, buf.at[1 - slot], sem.at[1 - slot], lo, hi)

        pltpu.make_async_copy(x_hbm.at[pl.ds(0, tm), :], buf.at[slot], sem.at[slot]).wait()
        h = _rms(buf[slot], g_ref[...]).astype(BF16)
        start_next(0)
        gate = _dot(h, wg_bf[...])
        start_next(1)
        act = jax.nn.silu(gate) * _dot(h, wu_bf[...])
        start_next(2)
        y_ref[...] = _dot(act.astype(BF16), wd_bf[...])

    @pl.when(i + 1 < n_act)
    def _():
        tile_body(True)

    @pl.when(i + 1 == n_act)
    def _():
        tile_body(False)


def _moe_ffn(x, g, w_gate, w_up, w_down, layer, tile_expert, pos_kmajor, n_active):
    grid_spec = pltpu.PrefetchScalarGridSpec(
        num_scalar_prefetch=3, grid=(MOE_TILES,),
        in_specs=[pl.BlockSpec(memory_space=pl.ANY),
                  pl.BlockSpec((1, D_MODEL), lambda i, te, tok, na: (0, 0)),
                  pl.BlockSpec((None, None, D_MODEL, MOE_D_FF), lambda i, te, tok, na: (layer, te[i], 0, 0)),
                  pl.BlockSpec((None, None, D_MODEL, MOE_D_FF), lambda i, te, tok, na: (layer, te[i], 0, 0)),
                  pl.BlockSpec((None, None, MOE_D_FF, D_MODEL), lambda i, te, tok, na: (layer, te[i], 0, 0))],
        out_specs=pl.BlockSpec((MOE_TM, D_MODEL), lambda i, te, tok, na: (i, 0)),
        scratch_shapes=[pltpu.SMEM((MOE_ROWS,), I32),
                        pltpu.VMEM((2, MOE_TM, D_MODEL), F32), pltpu.SemaphoreType.DMA((2,)),
                        pltpu.VMEM((D_MODEL, MOE_D_FF), BF16), pltpu.VMEM((D_MODEL, MOE_D_FF), BF16),
                        pltpu.VMEM((MOE_D_FF, D_MODEL), BF16)])
    return pl.pallas_call(
        _moe_ffn_kernel, name="moe_ffn", grid_spec=grid_spec,
        out_shape=jax.ShapeDtypeStruct((MOE_ROWS, D_MODEL), F32),
        compiler_params=_cparams(("arbitrary",)),
    )(tile_expert, pos_kmajor, n_active, x, g.reshape(1, D_MODEL), w_gate, w_up, w_down)


def _moe_combine_kernel(pos_ref, x_ref, cw_ref, g_ref, y_hbm, o_ref, buf, sem, *, final_norm):
    i = pl.program_id(0)
    n = pl.num_programs(0)
    tm = CMB_TM
    slot = i % 2

    def issue(tile, s):
        for k in range(MOE_TOPK):
            _gather_rows(y_hbm, pos_ref, (k * (TOKENS // tm) + tile) * tm, buf.at[s, k], sem.at[s], tm)

    @pl.when(i == 0)
    def _():
        issue(0, 0)

    @pl.when(i + 1 < n)
    def _():
        issue(i + 1, 1 - slot)

    for k in range(MOE_TOPK):
        pltpu.make_async_copy(y_hbm.at[pl.ds(0, tm), :], buf.at[slot, k], sem.at[slot]).wait()
    cw = cw_ref[...]
    out = x_ref[...] + cw[:, 0:1] * buf[slot, 0] + cw[:, 1:2] * buf[slot, 1]
    if final_norm:
        out = _rms(out, g_ref[...])
    o_ref[...] = out


def _moe_combine(x, cw, y_sorted, pos_kmajor, g_final, final_norm):
    grid_spec = pltpu.PrefetchScalarGridSpec(
        num_scalar_prefetch=1, grid=(TOKENS // CMB_TM,),
        in_specs=[pl.BlockSpec((CMB_TM, D_MODEL), lambda i, pos: (i, 0)),
                  pl.BlockSpec((CMB_TM, LANES), lambda i, pos: (i, 0)),
                  pl.BlockSpec((1, D_MODEL), lambda i, pos: (0, 0)),
                  pl.BlockSpec(memory_space=pl.ANY)],
        out_specs=pl.BlockSpec((CMB_TM, D_MODEL), lambda i, pos: (i, 0)),
        scratch_shapes=[pltpu.VMEM((2, MOE_TOPK, CMB_TM, D_MODEL), F32), pltpu.SemaphoreType.DMA((2,))])
    return pl.pallas_call(
        functools.partial(_moe_combine_kernel, final_norm=final_norm), name="moe_combine", grid_spec=grid_spec,
        out_shape=jax.ShapeDtypeStruct((TOKENS, D_MODEL), F32),
        compiler_params=_cparams(("arbitrary",)),
    )(pos_kmajor, x, cw, g_final.reshape(1, D_MODEL), y_sorted)


def _moe_plan(eid):
    e = eid[:, :MOE_TOPK].reshape(-1)
    onehot = (e[:, None] == jnp.arange(MOE_EXPERTS, dtype=I32)[None, :]).astype(I32)
    csum = jnp.cumsum(onehot, axis=0)
    counts = csum[-1]
    rank = jnp.take_along_axis(csum, e[:, None], axis=1)[:, 0] - 1
    padded = ((counts + MOE_TM - 1) // MOE_TM) * MOE_TM
    g_end = jnp.cumsum(padded)
    g_start = g_end - padded
    pos = g_start[e] + rank
    n_active = (g_end[-1] // MOE_TM).astype(I32)
    tile_start = jnp.arange(MOE_TILES, dtype=I32) * MOE_TM
    te = jnp.sum((g_end[None, :] <= tile_start[:, None]).astype(I32), axis=1)
    last = jnp.max(jnp.where(counts > 0, jnp.arange(MOE_EXPERTS, dtype=I32), 0))
    tile_expert = jnp.minimum(te, last)
    pos_kmajor = pos.reshape(TOKENS, MOE_TOPK).T.reshape(-1)
    return tile_expert, n_active.reshape(1), pos_kmajor


def _hier_moe_add(x, ln_g, w_group, b_group, w_expert, b_expert, w_gate, w_up, w_down, layer,
                  g_final, final_norm):
    eid, cw = _router(x, ln_g, w_group, b_group, w_expert, b_expert)
    tile_expert, n_active, pos_kmajor = _moe_plan(eid)
    y_sorted = _moe_ffn(x, ln_g, w_gate, w_up, w_down, layer, tile_expert, pos_kmajor, n_active)
    return _moe_combine(x, cw, y_sorted, pos_kmajor, g_final, final_norm)


def _rope_tables():
    pos = jnp.arange(SEQ, dtype=F32)
    inv = 1.0 / (ROPE_THETA ** (jnp.arange(0, NSA_HEAD_DIM, 2, dtype=F32) / NSA_HEAD_DIM))
    ang = pos[:, None] * inv[None, :]
    cos, sin = jnp.cos(ang), jnp.sin(ang)
    return jnp.concatenate([cos, cos], axis=1), jnp.concatenate([-sin, sin], axis=1)


def _nsa_mixer_add(x, ln_g, w_in, cmp_pe, cmp_w1, cmp_w2, w_out):
    hn, _, g_lin_t = _norm_small(x, ln_g, w_in[:, NSA_MAIN:], transposed=True)
    cos_full, sin_signed = _rope_tables()
    heads = _nsa_inproj(hn, w_in, cos_full, sin_signed)
    first_c = NSA_HEADS
    kv_c = heads[first_c:first_c + 2 * NSA_KV_HEADS]
    kv_chunks = kv_c.reshape(2, NSA_KV_HEADS, BATCH, SEQ // CMP_STRIDE, CMP_STRIDE * NSA_HEAD_DIM)
    kc_vc = _compress(kv_chunks, cmp_pe, cmp_w1, cmp_w2)
    gates_t = g_lin_t[:NSA_GATES].reshape(NSA_KV_HEADS, 3 * NSA_Q_PER_KV, TOKENS)
    o = _nsa_attention(heads, kc_vc, gates_t)
    return _outproj_resid(o, w_out, x)


def _ssd_mixer_add(x, ln_g, w_in, conv_w, conv_b, dt_bias, a_log, d_skip, norm_g, w_out):
    hn, dt_small, dt_small_t = _norm_small(x, ln_g, w_in[:, SSD_MAIN:], transposed=True)
    zx_tiles = _matmul_tiles(hn, w_in, SSD_MAIN, 1024)
    y_tiles = _ssd_chunks(zx_tiles, dt_small, dt_small_t, conv_w, conv_b, dt_bias, a_log, d_skip, norm_g)
    return _outproj_resid(y_tiles, w_out, x)


def kernel(x, ln_mix, ln_ffn, ln_final, nsa_w_in, nsa_cmp_pe, nsa_cmp_w1, nsa_cmp_w2, nsa_w_out,
           ssd_w_in, ssd_conv_w, ssd_conv_b, ssd_dt_bias, ssd_a_log, ssd_d, ssd_norm, ssd_w_out,
           moe_w_group, moe_b_group, moe_w_expert, moe_b_expert, moe_w_gate, moe_w_up, moe_w_down):
    h = x.reshape(TOKENS, D_MODEL)
    for i in range(DEPTH):
        j = i // N_MIXERS
        if i % N_MIXERS == 0:
            h = _nsa_mixer_add(h, ln_mix[i], nsa_w_in[j], nsa_cmp_pe[j], nsa_cmp_w1[j], nsa_cmp_w2[j],
                               nsa_w_out[j])
        else:
            h = _ssd_mixer_add(h, ln_mix[i], ssd_w_in[j], ssd_conv_w[j], ssd_conv_b[j], ssd_dt_bias[j],
                               ssd_a_log[j], ssd_d[j], ssd_norm[j], ssd_w_out[j])
        h = _hier_moe_add(h, ln_ffn[i], moe_w_group[i], moe_b_group[i], moe_w_expert[i], moe_b_expert[i],
                          moe_w_gate, moe_w_up, moe_w_down, i, ln_final, i == DEPTH - 1)
    return h.reshape(BATCH, SEQ, D_MODEL)
```

```python
import functools

import jax
import jax.numpy as jnp
from jax import lax
from jax.experimental import pallas as pl
from jax.experimental.pallas import tpu as pltpu

F32 = jnp.float32
BF16 = jnp.bfloat16
I32 = jnp.int32

D_MODEL = 2048
BATCH = 4
SEQ = 2048
TOKENS = BATCH * SEQ
DEPTH = 2
N_MIXERS = 2
NORM_EPS = 1e-6
NEG_INF = -1e30
ROPE_THETA = 10000.0

NSA_HEADS = 16
NSA_KV_HEADS = 4
NSA_HEAD_DIM = D_MODEL // NSA_HEADS
NSA_Q_PER_KV = NSA_HEADS // NSA_KV_HEADS
CMP_BLOCK = 32
CMP_STRIDE = 16
CMP_HIDDEN = 256
N_CMP = (SEQ - CMP_BLOCK) // CMP_STRIDE + 1
SLC_BLOCK = 64
SLC_TOPK = 16
SLC_LOCAL = 2
SLC_FORCE = 1e4
N_SLC = SEQ // SLC_BLOCK
WINDOW = 512
NSA_QD = NSA_HEADS * NSA_HEAD_DIM
NSA_KVD = NSA_KV_HEADS * NSA_HEAD_DIM
NSA_MAIN = NSA_QD + 6 * NSA_KVD
NSA_GATES = 3 * NSA_HEADS

SSD_D_INNER = 2 * D_MODEL
SSD_HEAD_DIM = 64
SSD_HEADS = SSD_D_INNER // SSD_HEAD_DIM
SSD_GROUPS = 8
SSD_HEADS_PER_GROUP = SSD_HEADS // SSD_GROUPS
SSD_D_STATE = 128
SSD_CONV = 4
SSD_CHUNK = 128
SSD_GROUP_W = SSD_D_INNER // SSD_GROUPS
SSD_BC = SSD_GROUPS * SSD_D_STATE
SSD_CONV_CH = SSD_D_INNER + 2 * SSD_BC
SSD_MAIN = SSD_D_INNER + SSD_CONV_CH
SSD_FAC_PAD = 16

MOE_GROUPS = 4
MOE_EPG = 8
MOE_EXPERTS = MOE_GROUPS * MOE_EPG
MOE_TOPK = 2
MOE_D_FF = 512

LANES = 128
VMEM_LIMIT = 56 * 1024 * 1024

NORM_TM = 256
MM_TM = 512
OUT_TM = 256
ATT_TQ = 128
ATT_CHUNK_TILES = 4
ATT_ONES = 16
MOE_TM = 256
MOE_TILES = (TOKENS * MOE_TOPK) // MOE_TM + MOE_EXPERTS
MOE_ROWS = MOE_TILES * MOE_TM
CMB_TM = 128
GATHER_UNROLL = 8
MOE_ISSUE_PARTS = 3


def _cparams(sem):
    return pltpu.CompilerParams(dimension_semantics=sem, vmem_limit_bytes=VMEM_LIMIT)


def _split3(x):
    hi = x.astype(BF16)
    r1 = x - hi.astype(F32)
    mid = r1.astype(BF16)
    lo = (r1 - mid.astype(F32)).astype(BF16)
    return hi, mid, lo


def _dot(a, b):
    return jnp.dot(a, b, preferred_element_type=F32)


def _dot_nt(a, b):
    return lax.dot_general(a, b, (((1,), (1,)), ((), ())), preferred_element_type=F32)


def _dot_tn(a, b):
    return lax.dot_general(a, b, (((0,), (0,)), ((), ())), preferred_element_type=F32)


def _dot_split_lhs(x, m_bf16):
    hi, mid, lo = _split3(x)
    return _dot(hi, m_bf16) + _dot(mid, m_bf16) + _dot(lo, m_bf16)


def _dot_split_rhs(m_bf16, x):
    hi, mid, lo = _split3(x)
    return _dot(m_bf16, hi) + _dot(m_bf16, mid) + _dot(m_bf16, lo)


def _dot_x3(a, w):
    a_hi = a.astype(BF16)
    a_lo = (a - a_hi.astype(F32)).astype(BF16)
    w_hi = w.astype(BF16)
    w_lo = (w - w_hi.astype(F32)).astype(BF16)
    return _dot(a_hi, w_hi) + _dot(a_hi, w_lo) + _dot(a_lo, w_hi)


def _dot_x3_nt(a, w):
    a_hi = a.astype(BF16)
    a_lo = (a - a_hi.astype(F32)).astype(BF16)
    w_hi = w.astype(BF16)
    w_lo = (w - w_hi.astype(F32)).astype(BF16)
    return _dot_nt(a_hi, w_hi) + _dot_nt(a_hi, w_lo) + _dot_nt(a_lo, w_hi)


def _rms(x, g):
    y = x * lax.rsqrt(jnp.mean(x * x, axis=-1, keepdims=True) + NORM_EPS)
    return y * g


def _norm_small_kernel(x_ref, g_ref, ws_ref, hn_ref, small_ref):
    y = _rms(x_ref[...], g_ref[...])
    hn_ref[...] = y.astype(BF16)
    small_ref[...] = _dot_x3(y, ws_ref[...])


def _norm_small_t_kernel(x_ref, g_ref, ws_ref, hn_ref, small_ref, small_t_ref):
    y = _rms(x_ref[...], g_ref[...])
    hn_ref[...] = y.astype(BF16)
    small = _dot_x3(y, ws_ref[...])
    small_ref[...] = small
    small_t_ref[...] = small.T


def _norm_small(x, g, w_small, transposed=False):
    n = w_small.shape[1]
    ws = jnp.zeros((D_MODEL, LANES), F32).at[:, :n].set(w_small)
    grid = (TOKENS // NORM_TM,)
    x_spec = pl.BlockSpec((NORM_TM, D_MODEL), lambda i: (i, 0))
    g_spec = pl.BlockSpec((1, D_MODEL), lambda i: (0, 0))
    w_spec = pl.BlockSpec((D_MODEL, LANES), lambda i: (0, 0))
    hn_spec = pl.BlockSpec((NORM_TM, D_MODEL), lambda i: (i, 0))
    sm_spec = pl.BlockSpec((NORM_TM, LANES), lambda i: (i, 0))
    hn_shape = jax.ShapeDtypeStruct((TOKENS, D_MODEL), BF16)
    sm_shape = jax.ShapeDtypeStruct((TOKENS, LANES), F32)
    if not transposed:
        return pl.pallas_call(
            _norm_small_kernel, name="norm_small", grid=grid,
            in_specs=[x_spec, g_spec, w_spec],
            out_specs=[hn_spec, sm_spec],
            out_shape=[hn_shape, sm_shape],
            compiler_params=_cparams(("parallel",)),
        )(x, g.reshape(1, D_MODEL), ws)
    smt_spec = pl.BlockSpec((LANES, NORM_TM), lambda i: (0, i))
    smt_shape = jax.ShapeDtypeStruct((LANES, TOKENS), F32)
    return pl.pallas_call(
        _norm_small_t_kernel, name="norm_small_t", grid=grid,
        in_specs=[x_spec, g_spec, w_spec],
        out_specs=[hn_spec, sm_spec, smt_spec],
        out_shape=[hn_shape, sm_shape, smt_shape],
        compiler_params=_cparams(("parallel",)),
    )(x, g.reshape(1, D_MODEL), ws)


def _router_kernel(x_ref, g_ref, ws_ref, b_ref, eid_ref, cw_ref):
    y = _rms(x_ref[...], g_ref[...])
    logits = _dot_x3(y, ws_ref[...]) + b_ref[...]
    lane = lax.broadcasted_iota(I32, logits.shape, 1)
    big = jnp.int32(LANES)
    neg = -jnp.inf
    gl = jnp.where(lane < MOE_GROUPS, logits, neg)
    gmax = jnp.max(gl, axis=-1, keepdims=True)
    gsum = jnp.sum(jnp.exp(gl - gmax), axis=-1, keepdims=True)
    g_w = 1.0 / gsum
    g_sel = jnp.min(jnp.where(gl == gmax, lane, big), axis=-1, keepdims=True)
    lo = MOE_GROUPS + g_sel * MOE_EPG
    el = jnp.where((lane >= lo) & (lane < lo + MOE_EPG), logits, neg)
    v1 = jnp.max(el, axis=-1, keepdims=True)
    i1 = jnp.min(jnp.where(el == v1, lane, big), axis=-1, keepdims=True)
    el2 = jnp.where(lane == i1, neg, el)
    v2 = jnp.max(el2, axis=-1, keepdims=True)
    i2 = jnp.min(jnp.where(el2 == v2, lane, big), axis=-1, keepdims=True)
    e2 = jnp.exp(v2 - v1)
    den = 1.0 + e2
    w1 = (1.0 / den) * g_w
    w2 = (e2 / den) * g_w
    eid_ref[...] = jnp.where(lane == 0, i1 - MOE_GROUPS, jnp.where(lane == 1, i2 - MOE_GROUPS, 0))
    cw_ref[...] = jnp.where(lane == 0, w1, jnp.where(lane == 1, w2, 0.0))


def _router(x, g, w_group, b_group, w_expert, b_expert):
    n = MOE_GROUPS + MOE_EXPERTS
    ws = jnp.zeros((D_MODEL, LANES), F32).at[:, :n].set(jnp.concatenate([w_group, w_expert], axis=1))
    bs = jnp.zeros((1, LANES), F32).at[0, :n].set(jnp.concatenate([b_group, b_expert]))
    return pl.pallas_call(
        _router_kernel, name="router", grid=(TOKENS // NORM_TM,),
        in_specs=[pl.BlockSpec((NORM_TM, D_MODEL), lambda i: (i, 0)),
                  pl.BlockSpec((1, D_MODEL), lambda i: (0, 0)),
                  pl.BlockSpec((D_MODEL, LANES), lambda i: (0, 0)),
                  pl.BlockSpec((1, LANES), lambda i: (0, 0))],
        out_specs=[pl.BlockSpec((NORM_TM, LANES), lambda i: (i, 0)),
                   pl.BlockSpec((NORM_TM, LANES), lambda i: (i, 0))],
        out_shape=[jax.ShapeDtypeStruct((TOKENS, LANES), I32),
                   jax.ShapeDtypeStruct((TOKENS, LANES), F32)],
        compiler_params=_cparams(("parallel",)),
    )(x, g.reshape(1, D_MODEL), ws, bs)


def _inproj_heads_kernel(a_ref, w_ref, cos_ref, sin_ref, o_ref, wbf_ref):
    j = pl.program_id(0)

    @pl.when(pl.program_id(1) == 0)
    def _():
        wbf_ref[...] = w_ref[...].astype(BF16)

    acc = _dot_nt(a_ref[...], wbf_ref[...])
    q_tiles = NSA_QD // acc.shape[1]
    c = cos_ref[...]
    s = sin_ref[...]

    def head(h):
        return acc[:, h * NSA_HEAD_DIM:(h + 1) * NSA_HEAD_DIM]

    def rotary(xh):
        return (xh * c + pltpu.roll(xh, NSA_HEAD_DIM // 2, 1) * s).astype(BF16)

    for h in range(NSA_KV_HEADS):
        o_ref[h] = rotary(head(h))

    @pl.when(j < q_tiles)
    def _():
        for h in range(NSA_KV_HEADS, 2 * NSA_KV_HEADS):
            o_ref[h] = rotary(head(h))

    @pl.when(j >= q_tiles)
    def _():
        for h in range(NSA_KV_HEADS, 2 * NSA_KV_HEADS):
            o_ref[h] = head(h).astype(BF16)


def _nsa_inproj(hn, w_in_t, cos_full, sin_signed):
    tn = 2 * NSA_KVD
    n_heads_tile = tn // NSA_HEAD_DIM
    s_tiles = SEQ // MM_TM
    return pl.pallas_call(
        _inproj_heads_kernel, name="nsa_inproj", grid=(NSA_MAIN // tn, TOKENS // MM_TM),
        in_specs=[pl.BlockSpec((MM_TM, D_MODEL), lambda j, i: (i, 0)),
                  pl.BlockSpec((tn, D_MODEL), lambda j, i: (j, 0)),
                  pl.BlockSpec((MM_TM, NSA_HEAD_DIM), lambda j, i: (i % s_tiles, 0)),
                  pl.BlockSpec((MM_TM, NSA_HEAD_DIM), lambda j, i: (i % s_tiles, 0))],
        out_specs=pl.BlockSpec((n_heads_tile, MM_TM, NSA_HEAD_DIM), lambda j, i: (j, i, 0)),
        out_shape=jax.ShapeDtypeStruct((NSA_MAIN // NSA_HEAD_DIM, TOKENS, NSA_HEAD_DIM), BF16),
        scratch_shapes=[pltpu.VMEM((tn, D_MODEL), BF16)],
        compiler_params=_cparams(("arbitrary", "arbitrary")),
    )(hn, w_in_t, cos_full, sin_signed)


def _matmul_tiles_kernel(a_ref, w_ref, o_ref, wbf_ref):
    @pl.when(pl.program_id(1) == 0)
    def _():
        wbf_ref[...] = w_ref[...].astype(BF16)

    acc = _dot_nt(a_ref[...], wbf_ref[...])
    for t in range(acc.shape[1] // LANES):
        o_ref[t] = acc[:, t * LANES:(t + 1) * LANES].astype(o_ref.dtype)


def _matmul_tiles(a, w_t, n_cols, tn):
    k = a.shape[1]
    return pl.pallas_call(
        _matmul_tiles_kernel, name="matmul_tiles", grid=(n_cols // tn, TOKENS // MM_TM),
        in_specs=[pl.BlockSpec((MM_TM, k), lambda j, i: (i, 0)),
                  pl.BlockSpec((tn, k), lambda j, i: (j, 0))],
        out_specs=pl.BlockSpec((tn // LANES, MM_TM, LANES), lambda j, i: (j, i, 0)),
        out_shape=jax.ShapeDtypeStruct((n_cols // LANES, TOKENS, LANES), BF16),
        scratch_shapes=[pltpu.VMEM((tn, k), BF16)],
        compiler_params=_cparams(("arbitrary", "arbitrary")),
    )(a, w_t)


def _outproj_kernel(a_ref, w_ref, r_ref, o_ref):
    if len(a_ref.shape) == 3:
        a = jnp.concatenate([a_ref[t] for t in range(a_ref.shape[0])], axis=1)
    else:
        a = a_ref[...]
    o_ref[...] = r_ref[...] + _dot(a, w_ref[...])


def _outproj_resid(a, w, resid):
    tiled = a.ndim == 3
    k, n = w.shape
    a_spec = (pl.BlockSpec((k // LANES, OUT_TM, LANES), lambda i: (0, i, 0)) if tiled
              else pl.BlockSpec((OUT_TM, k), lambda i: (i, 0)))
    return pl.pallas_call(
        _outproj_kernel, name="outproj_resid", grid=(TOKENS // OUT_TM,),
        in_specs=[a_spec,
                  pl.BlockSpec((k, n), lambda i: (0, 0)),
                  pl.BlockSpec((OUT_TM, n), lambda i: (i, 0))],
        out_specs=pl.BlockSpec((OUT_TM, n), lambda i: (i, 0)),
        out_shape=jax.ShapeDtypeStruct((TOKENS, n), F32),
        compiler_params=_cparams(("parallel",)),
    )(a, w.astype(BF16), resid)


def _compress_kernel(x_ref, pe_ref, w1_ref, w2_ref, o_ref):
    half = CMP_STRIDE * NSA_HEAD_DIM
    x = x_ref[0, 0, 0]
    w1 = w1_ref[0].astype(BF16)
    top = _dot(x, w1[:half])
    bot = _dot(x, w1[half:])
    pe = jnp.broadcast_to(pe_ref[0], (8, 2 * half)).astype(BF16)
    pe_bias = _dot(pe, w1)[0:1]
    hid = top + pltpu.roll(bot, bot.shape[0] - 1, 0) + pe_bias
    act = jax.nn.gelu(hid)
    o_ref[0, 0, 0] = _dot(act.astype(BF16), w2_ref[0].astype(BF16))


def _compress(kv_chunks, pe, w1, w2):
    n_chunk = SEQ // CMP_STRIDE
    feat = CMP_STRIDE * NSA_HEAD_DIM
    return pl.pallas_call(
        _compress_kernel, name="nsa_compress", grid=(2, NSA_KV_HEADS, BATCH),
        in_specs=[pl.BlockSpec((1, 1, 1, n_chunk, feat), lambda a, g, b: (a, g, b, 0, 0)),
                  pl.BlockSpec((1, 1, 2 * feat), lambda a, g, b: (a, 0, 0)),
                  pl.BlockSpec((1, 2 * feat, CMP_HIDDEN), lambda a, g, b: (a, 0, 0)),
                  pl.BlockSpec((1, CMP_HIDDEN, NSA_HEAD_DIM), lambda a, g, b: (a, 0, 0))],
        out_specs=pl.BlockSpec((1, 1, 1, n_chunk, NSA_HEAD_DIM), lambda a, g, b: (a, g, b, 0, 0)),
        out_shape=jax.ShapeDtypeStruct((2, NSA_KV_HEADS, BATCH, n_chunk, NSA_HEAD_DIM), F32),
        compiler_params=_cparams(("parallel", "parallel", "parallel")),
    )(kv_chunks, pe.reshape(2, 1, 2 * feat), w1, w2)


def _nsa_attn_kernel(q_ref, ks_ref, vs_ref, kw_ref, vw_ref, kc_ref, vc_ref, gate_ref, o_ref,
                     vst_scr, vwt_scr, sel_scr, acc_scr):
    qi = pl.program_id(2)
    tq = ATT_TQ
    tk = ATT_TQ
    dh = NSA_HEAD_DIM
    r_heads = NSA_Q_PER_KV
    n_kt = SEQ // tk
    scale = dh ** -0.5

    @pl.when(qi == 0)
    def _():
        ones = jnp.ones((ATT_ONES, tk), BF16)
        for kt in range(n_kt):
            rows = slice(kt * tk, (kt + 1) * tk)
            vst_scr[kt, 0:dh, :] = vs_ref[0, rows, :].astype(F32).T.astype(BF16)
            vwt_scr[kt, 0:dh, :] = vw_ref[0, rows, :].astype(F32).T.astype(BF16)
            vst_scr[kt, dh:dh + ATT_ONES, :] = ones
            vwt_scr[kt, dh:dh + ATT_ONES, :] = ones

    q_t = jnp.concatenate([q_ref[r].astype(F32).T for r in range(r_heads)], axis=1).astype(BF16)
    sub = lax.broadcasted_iota(I32, (tk, tq), 0)
    t_pos = qi * tq + lax.broadcasted_iota(I32, (tk, tq), 1)

    def tile4(a):
        return jnp.concatenate([a] * r_heads, axis=1)

    kc = kc_ref[0, 0, 0].astype(BF16)
    vc = vc_ref[0, 0, 0].astype(BF16)
    ok_c = jnp.where(sub * CMP_STRIDE + CMP_BLOCK - 1 <= t_pos, jnp.where(sub < N_CMP, 1.0, 0.0), 0.0)
    ok_c4 = tile4(ok_c)
    s_c = _dot(kc, q_t) * scale + (ok_c4 - 1.0) * (-NEG_INF)
    e_c = jnp.exp(s_c - jnp.max(s_c, axis=0, keepdims=True))
    p_c = (e_c / jnp.sum(e_c, axis=0, keepdims=True)) * ok_c4
    o_cmp = _dot_tn(vc, p_c.astype(BF16))
    p_sum = p_c[:, 0:tq]
    for r in range(1, r_heads):
        p_sum = p_sum + p_c[:, r * tq:(r + 1) * tq]

    blk_row = lax.broadcasted_iota(I32, (LANES, LANES), 0)
    cmp_col = lax.broadcasted_iota(I32, (LANES, LANES), 1)
    s_start = blk_row * SLC_BLOCK
    c_start = cmp_col * CMP_STRIDE
    ov_t = jnp.maximum(jnp.minimum(c_start + CMP_BLOCK, s_start + SLC_BLOCK)
                       - jnp.maximum(c_start, s_start), 0).astype(F32) / CMP_BLOCK
    ov_t = jnp.where(blk_row < N_SLC, ov_t, 0.0).astype(BF16)
    imp = _dot_split_rhs(ov_t, p_sum)[0:N_SLC]
    j_blk = lax.broadcasted_iota(I32, (N_SLC, tq), 0)
    dist = (qi * tq + lax.broadcasted_iota(I32, (N_SLC, tq), 1)) // SLC_BLOCK - j_blk
    imp = jnp.where(j_blk == 0, SLC_FORCE, jnp.where(dist < 0, imp, jnp.where(dist < SLC_LOCAL, SLC_FORCE, imp)))
    imp = jnp.where(dist >= 0, imp, -jnp.inf)
    cnt = jnp.zeros((N_SLC, tq), I32)
    for k in range(N_SLC):
        row_k = imp[k:k + 1, :]
        tie = jnp.where(j_blk > k, 1, 0)
        cnt = cnt + jnp.where(row_k > imp, 1, jnp.where(row_k == imp, tie, 0))
    sel = jnp.where(cnt < min(SLC_TOPK, N_SLC), 1.0, 0.0)
    for j in range(N_SLC):
        sel_scr[8 * j:8 * j + 8, :] = jnp.broadcast_to(sel[j:j + 1, :], (8, tq))

    def scores(k, ok):
        return _dot(k, q_t) * scale + tile4((ok - 1.0) * (-NEG_INF))

    def weighted_values(vt_scr, kt0, pr, n_tiles):
        out = None
        for u in range(n_tiles):
            term = _dot(vt_scr[kt0 + u], pr[u * tk:(u + 1) * tk].astype(BF16))
            out = term if out is None else out + term
        return out

    n_ct = ATT_CHUNK_TILES
    ck = n_ct * tk
    sub_c = lax.broadcasted_iota(I32, (ck, tq), 0)
    t_pos_c = qi * tq + lax.broadcasted_iota(I32, (ck, tq), 1)
    blocks_per_chunk = ck // SLC_BLOCK
    acc_scr[...] = jnp.zeros(acc_scr.shape, F32)

    def slc_body(c, m_old):
        start = pl.multiple_of(c * ck, ck)
        k = ks_ref[0, pl.ds(start, ck), :]
        rows8 = sel_scr[pl.ds(pl.multiple_of(c * (8 * blocks_per_chunk), 8 * blocks_per_chunk),
                              8 * blocks_per_chunk), :]
        picked = jnp.concatenate(
            [rows8[8 * u:8 * u + 8] for u in range(blocks_per_chunk) for _ in range(SLC_BLOCK // 8)], axis=0)
        sc = scores(k, jnp.where(start + sub_c <= t_pos_c, picked, 0.0))
        m_new = jnp.maximum(m_old, jnp.max(sc, axis=0, keepdims=True))
        alpha = jnp.exp(m_old - m_new)
        pr = jnp.exp(sc - m_new)
        acc_scr[...] = alpha * acc_scr[...] + weighted_values(vst_scr, c * n_ct, pr, n_ct)
        return m_new

    lax.fori_loop(0, (qi + n_ct) // n_ct, slc_body, jnp.full((1, r_heads * tq), NEG_INF, F32))
    acc = acc_scr[...]
    o_slc = acc[0:dh] / acc[dh:dh + 1]

    n_wt = WINDOW // tk + 1
    kt0 = jnp.maximum(qi - WINDOW // tk, 0)
    w_start = pl.multiple_of(kt0 * tk, tk)
    key_w = w_start + lax.broadcasted_iota(I32, (n_wt * tk, tq), 0)
    t_pos_w = qi * tq + lax.broadcasted_iota(I32, (n_wt * tk, tq), 1)
    ok_w = jnp.where(key_w <= t_pos_w, jnp.where(key_w > t_pos_w - WINDOW, 1.0, 0.0), 0.0)
    sc_w = scores(kw_ref[0, pl.ds(w_start, n_wt * tk), :], ok_w)
    pr_w = jnp.exp(sc_w - jnp.max(sc_w, axis=0, keepdims=True))
    acc_w = weighted_values(vwt_scr, kt0, pr_w, n_wt)
    o_win = acc_w[0:dh] / acc_w[dh:dh + 1]

    gate = jax.nn.sigmoid(gate_ref[0])
    for r in range(r_heads):
        cols = slice(r * tq, (r + 1) * tq)
        o = (gate[3 * r:3 * r + 1] * o_cmp[:, cols] + gate[3 * r + 1:3 * r + 2] * o_slc[:, cols]
             + gate[3 * r + 2:3 * r + 3] * o_win[:, cols])
        o_ref[:, r * dh:(r + 1) * dh] = o.T.astype(BF16)


def _nsa_attention(heads, kc_vc, gates):
    tq = ATT_TQ
    nq = SEQ // tq
    r = NSA_Q_PER_KV
    g_heads = NSA_KV_HEADS
    q_spec = pl.BlockSpec((r, tq, NSA_HEAD_DIM), lambda b, g, i: (g, b * nq + i, 0))

    def kv_spec(first_head):
        return pl.BlockSpec((1, SEQ, NSA_HEAD_DIM), lambda b, g, i: (first_head + g, b, 0))

    first = NSA_HEADS
    specs = [q_spec,
             kv_spec(first + 2 * g_heads), kv_spec(first + 3 * g_heads),
             kv_spec(first + 4 * g_heads), kv_spec(first + 5 * g_heads),
             pl.BlockSpec((1, 1, 1, SEQ // CMP_STRIDE, NSA_HEAD_DIM), lambda b, g, i: (0, g, b, 0, 0)),
             pl.BlockSpec((1, 1, 1, SEQ // CMP_STRIDE, NSA_HEAD_DIM), lambda b, g, i: (1, g, b, 0, 0)),
             pl.BlockSpec((1, 3 * r, tq), lambda b, g, i: (g, 0, b * nq + i))]
    vt_shape = (SEQ // tq, NSA_HEAD_DIM + ATT_ONES, tq)
    return pl.pallas_call(
        _nsa_attn_kernel, name="nsa_attn", grid=(BATCH, g_heads, nq),
        in_specs=specs,
        out_specs=pl.BlockSpec((tq, r * NSA_HEAD_DIM), lambda b, g, i: (b * nq + i, g)),
        out_shape=jax.ShapeDtypeStruct((TOKENS, NSA_QD), BF16),
        scratch_shapes=[pltpu.VMEM(vt_shape, BF16), pltpu.VMEM(vt_shape, BF16),
                        pltpu.VMEM((8 * N_SLC, tq), F32),
                        pltpu.VMEM((NSA_HEAD_DIM + ATT_ONES, r * tq), F32)],
        compiler_params=_cparams(("arbitrary", "arbitrary", "arbitrary")),
    )(heads, heads, heads, heads, heads, kc_vc, kc_vc, gates)


def _ssd_chunk_kernel(zx_ref, dtc_ref, dtr_ref, cw_ref, cb_ref, dtb_c_ref, alog_c_ref, dtb_r_ref, alog_r_ref,
                      dskip_ref, ng_ref, shift_ref, echan_ref, ehead_ref, o_ref,
                      prev_scr, acum_r_scr, st_scr, fac_scr, acp_scr):
    chunk = pl.program_id(1)
    L = SSD_CHUNK
    W = SSD_GROUP_W
    hpg = SSD_HEADS_PER_GROUP
    n_xt = W // LANES
    x0 = SSD_D_INNER // LANES
    b0 = 2 * SSD_D_INNER // LANES
    c0 = b0 + SSD_GROUPS
    cb0 = SSD_D_INNER // LANES
    cc0 = cb0 + SSD_GROUPS

    @pl.when(chunk == 0)
    def _():
        prev_scr[...] = jnp.zeros(prev_scr.shape, BF16)
        st_scr[...] = jnp.zeros(st_scr.shape, F32)

    dt_c = jax.nn.softplus(dtc_ref[...] + dtb_c_ref[...])
    adt_c = dt_c * (-jnp.exp(alog_c_ref[...]))
    dt_r = jax.nn.softplus(dtr_ref[...] + dtb_r_ref[...])
    adt_r = dt_r * (-jnp.exp(alog_r_ref[...]))
    row = lax.broadcasted_iota(I32, (L, L), 0)
    col = lax.broadcasted_iota(I32, (L, L), 1)
    causal = row >= col
    tri = jnp.where(causal, 1.0, 0.0).astype(BF16)
    tri_t = jnp.where(col >= row, 1.0, 0.0).astype(BF16)
    acum_c = _dot_split_rhs(tri, adt_c)
    acum_r_scr[...] = _dot_split_lhs(adt_r, tri_t)
    a_last = acum_c[L - 1:L, :]
    fac = jnp.concatenate([dt_c, jnp.exp(acum_c), jnp.exp(a_last - acum_c),
                           jnp.broadcast_to(jnp.exp(a_last), (SSD_FAC_PAD, LANES))], axis=0)
    fac_hi = fac.astype(BF16)
    fac_scr[0] = fac_hi
    fac_scr[1] = (fac - fac_hi.astype(F32)).astype(BF16)
    for i, part in enumerate(_split3(acum_c)):
        acp_scr[i] = part
    lane_w = lax.broadcasted_iota(I32, (L, LANES), 1)
    first_half = lane_w < SSD_HEAD_DIM

    def tiles(ref, first, n):
        return jnp.concatenate([ref[first + q] for q in range(n)], axis=1)

    def group_body(g, carry):
        e_chan = echan_ref[g]
        ex = _dot(fac_scr[0], e_chan) + _dot(fac_scr[1], e_chan)
        dt_x, ea_x, sd_x, cd_x = ex[0:L], ex[L:2 * L], ex[2 * L:3 * L], ex[3 * L:3 * L + 1]
        e_head = ehead_ref[g]
        acum_b = _dot(acp_scr[0], e_head) + _dot(acp_scr[1], e_head) + _dot(acp_scr[2], e_head)

        def conv_silu(zx_first, conv_first, n):
            cur = tiles(zx_ref, zx_first, n)
            ext = jnp.concatenate([tiles(prev_scr, conv_first, n), cur], axis=0)
            w = tiles(cw_ref, conv_first, n)
            acc = jnp.broadcast_to(tiles(cb_ref, conv_first, n), (L, n * LANES))
            for k in range(SSD_CONV):
                back = SSD_CONV - 1 - k
                xk = cur.astype(F32) if back == 0 else _dot(shift_ref[back - 1], ext)
                acc = acc + xk * w[k:k + 1, :]
            for q in range(n):
                prev_scr[conv_first + q] = zx_ref[zx_first + q]
            return jax.nn.silu(acc)

        xs = conv_silu(x0 + n_xt * g, n_xt * g, n_xt)
        bm = conv_silu(b0 + g, cb0 + g, 1)
        cm = conv_silu(c0 + g, cc0 + g, 1)

        xdt = xs * dt_x
        cb = jnp.where(causal, _dot_nt(cm.astype(BF16), bm.astype(BF16)), 0.0)
        y_parts = []
        for pair in range(hpg // 2):
            xd = xdt[:, pair * LANES:(pair + 1) * LANES]
            y_pair = None
            for sub in range(2):
                h = 2 * pair + sub
                a_row = acum_r_scr[pl.ds(g * hpg + h, 1), :]
                seg = jnp.minimum(acum_b[:, h * L:(h + 1) * L] - a_row, 0.0)
                m_h = (cb * jnp.exp(seg)).astype(BF16)
                x_h = jnp.where(first_half if sub == 0 else jnp.logical_not(first_half), xd, 0.0)
                term = _dot(m_h, x_h.astype(BF16))
                y_pair = term if y_pair is None else y_pair + term
            y_parts.append(y_pair)
        y_diag = jnp.concatenate(y_parts, axis=1)

        st = st_scr[g]
        y_off = _dot(cm.astype(BF16), st.astype(BF16)) * ea_x
        st_scr[g] = st * cd_x + _dot_tn(bm.astype(BF16), (xdt * sd_x).astype(BF16))

        y = y_diag + y_off + xs * tiles(dskip_ref, n_xt * g, n_xt)
        y = y * jax.nn.silu(tiles(zx_ref, n_xt * g, n_xt).astype(F32))
        y = y * lax.rsqrt(jnp.mean(y * y, axis=-1, keepdims=True) + NORM_EPS)
        y = y * tiles(ng_ref, n_xt * g, n_xt)
        for q in range(n_xt):
            o_ref[n_xt * g + q] = y[:, q * LANES:(q + 1) * LANES].astype(BF16)
        return carry

    lax.fori_loop(0, SSD_GROUPS, group_body, 0)


def _ssd_chunks(zx_tiles, dt_small, dt_small_t, conv_w, conv_b, dt_bias, a_log, d_skip, norm_g):
    L = SSD_CHUNK
    nc = SEQ // L
    n_zx = SSD_MAIN // LANES
    n_conv = SSD_CONV_CH // LANES
    n_inner = SSD_D_INNER // LANES
    hpg = SSD_HEADS_PER_GROUP

    def pad_heads(v):
        return jnp.zeros((LANES,), F32).at[:SSD_HEADS].set(v)

    dtb = pad_heads(dt_bias)
    alog = pad_heads(a_log)
    cw = conv_w.reshape(SSD_CONV, n_conv, LANES).transpose(1, 0, 2)
    cb = conv_b.reshape(n_conv, 1, LANES)
    d_chan = jnp.repeat(d_skip, SSD_HEAD_DIM).reshape(n_inner, 1, LANES)
    ng = norm_g.reshape(n_inner, 1, LANES)
    t_idx = jnp.arange(L, dtype=I32)[None, :, None]
    r_idx = jnp.arange(2 * L, dtype=I32)[None, None, :]
    back = jnp.arange(1, SSD_CONV, dtype=I32)[:, None, None]
    shift = (r_idx == L + t_idx - back).astype(BF16)
    head = jnp.arange(LANES, dtype=I32)[None, :, None]
    grp = jnp.arange(SSD_GROUPS, dtype=I32)[:, None, None]
    e_chan = (head == grp * hpg + jnp.arange(SSD_GROUP_W, dtype=I32)[None, None, :] // SSD_HEAD_DIM).astype(BF16)
    e_head = (head == grp * hpg + jnp.arange(hpg * L, dtype=I32)[None, None, :] // L).astype(BF16)
    row = lambda b, c: b * nc + c
    const3 = lambda b, c: (0, 0, 0)
    const2 = lambda b, c: (0, 0)
    in_specs = [
        pl.BlockSpec((n_zx, L, LANES), lambda b, c: (0, row(b, c), 0)),
        pl.BlockSpec((L, LANES), lambda b, c: (row(b, c), 0)),
        pl.BlockSpec((LANES, L), lambda b, c: (0, row(b, c))),
        pl.BlockSpec((n_conv, SSD_CONV, LANES), const3),
        pl.BlockSpec((n_conv, 1, LANES), const3),
        pl.BlockSpec((1, LANES), const2), pl.BlockSpec((1, LANES), const2),
        pl.BlockSpec((LANES, 1), const2), pl.BlockSpec((LANES, 1), const2),
        pl.BlockSpec((n_inner, 1, LANES), const3),
        pl.BlockSpec((n_inner, 1, LANES), const3),
        pl.BlockSpec((SSD_CONV - 1, L, 2 * L), const3),
        pl.BlockSpec((SSD_GROUPS, LANES, SSD_GROUP_W), const3),
        pl.BlockSpec((SSD_GROUPS, LANES, hpg * L), const3),
    ]
    return pl.pallas_call(
        _ssd_chunk_kernel, name="ssd_chunks", grid=(BATCH, nc),
        in_specs=in_specs,
        out_specs=pl.BlockSpec((n_inner, L, LANES), lambda b, c: (0, row(b, c), 0)),
        out_shape=jax.ShapeDtypeStruct((n_inner, TOKENS, LANES), BF16),
        scratch_shapes=[pltpu.VMEM((n_conv, L, LANES), BF16),
                        pltpu.VMEM((LANES, L), F32),
                        pltpu.VMEM((SSD_GROUPS, SSD_D_STATE, SSD_GROUP_W), F32),
                        pltpu.VMEM((2, 3 * L + SSD_FAC_PAD, LANES), BF16),
                        pltpu.VMEM((3, L, LANES), BF16)],
        compiler_params=_cparams(("arbitrary", "arbitrary")),
    )(zx_tiles, dt_small, dt_small_t, cw, cb, dtb.reshape(1, LANES), alog.reshape(1, LANES),
      dtb.reshape(LANES, 1), alog.reshape(LANES, 1), d_chan, ng, shift, e_chan, e_head)


def _gather_rows(src_hbm, idx_ref, base, dst, sem, n_rows):
    def body(r, carry):
        tok = idx_ref[base + r]
        pltpu.make_async_copy(src_hbm.at[pl.ds(tok, 1), :], dst.at[pl.ds(r, 1), :], sem).start()
        return carry

    lax.fori_loop(0, n_rows, body, 0, unroll=GATHER_UNROLL)


def _start_rows(src_hbm, idx_ref, base, dst, sem, lo, hi):
    for r in range(lo, hi):
        tok = idx_ref[base + r]
        pltpu.make_async_copy(src_hbm.at[pl.ds(tok, 1), :], dst.at[pl.ds(r, 1), :], sem).start()


def _moe_ffn_kernel(te_ref, pos_ref, nact_ref, pad_ref, x_hbm, g_ref, wg_ref, wu_ref, wd_ref, y_ref,
                    tok_ref, buf, sem, wg_bf, wu_bf, wd_bf):
    i = pl.program_id(0)
    n_act = nact_ref[0]
    tm = MOE_TM
    slot = i % 2

    @pl.when(i == 0)
    def _():
        def clear(q, carry):
            tok_ref[q] = 0
            return carry

        def clear_padding(e, carry):
            lax.fori_loop(pad_ref[e], pad_ref[MOE_EXPERTS + e], clear, 0)
            return carry

        def place(p, carry):
            tok_ref[pos_ref[p]] = p % TOKENS
            return carry

        lax.fori_loop(0, MOE_EXPERTS, clear_padding, 0)
        lax.fori_loop(0, TOKENS * MOE_TOPK, place, 0, unroll=GATHER_UNROLL)
        _gather_rows(x_hbm, tok_ref, 0, buf.at[0], sem.at[0], tm)

    e_cur = te_ref[i]
    e_prev = te_ref[jnp.maximum(i - 1, 0)]

    @pl.when((i == 0) | (e_cur != e_prev))
    def _():
        wg_bf[...] = wg_ref[...].astype(BF16)
        wu_bf[...] = wu_ref[...].astype(BF16)
        wd_bf[...] = wd_ref[...].astype(BF16)

    @pl.when(i >= n_act)
    def _():
        y_ref[...] = jnp.zeros(y_ref.shape, F32)

    def tile_body(prefetch_next):
        def start_next(part):
            if prefetch_next:
                lo, hi = (part * tm) // MOE_ISSUE_PARTS, ((part + 1) * tm) // MOE_ISSUE_PARTS
                _start_rows(x_hbm, tok_ref, (i + 1) * tm, buf.at[1 - slot], sem.at[1 - slot], lo, hi)

        pltpu.make_async_copy(x_hbm.at[pl.ds(0, tm), :], buf.at[slot], sem.at[slot]).wait()
        h = _rms(buf[slot], g_ref[...]).astype(BF16)
        start_next(0)
        gate = _dot(h, wg_bf[...])
        start_next(1)
        act = jax.nn.silu(gate) * _dot(h, wu_bf[...])
        start_next(2)
        y_ref[...] = _dot(act.astype(BF16), wd_bf[...])

    @pl.when(i + 1 < n_act)
    def _():
        tile_body(True)

    @pl.when(i + 1 == n_act)
    def _():
        tile_body(False)


def _moe_ffn(x, g, w_gate, w_up, w_down, layer, tile_expert, pos_kmajor, n_active, pad_rows):
    weight = lambda i, te, *_: (layer, te[i], 0, 0)
    grid_spec = pltpu.PrefetchScalarGridSpec(
        num_scalar_prefetch=4, grid=(MOE_TILES,),
        in_specs=[pl.BlockSpec(memory_space=pl.ANY),
                  pl.BlockSpec((1, D_MODEL), lambda i, *_: (0, 0)),
                  pl.BlockSpec((None, None, D_MODEL, MOE_D_FF), weight),
                  pl.BlockSpec((None, None, D_MODEL, MOE_D_FF), weight),
                  pl.BlockSpec((None, None, MOE_D_FF, D_MODEL), weight)],
        out_specs=pl.BlockSpec((MOE_TM, D_MODEL), lambda i, *_: (i, 0)),
        scratch_shapes=[pltpu.SMEM((MOE_ROWS,), I32),
                        pltpu.VMEM((2, MOE_TM, D_MODEL), F32), pltpu.SemaphoreType.DMA((2,)),
                        pltpu.VMEM((D_MODEL, MOE_D_FF), BF16), pltpu.VMEM((D_MODEL, MOE_D_FF), BF16),
                        pltpu.VMEM((MOE_D_FF, D_MODEL), BF16)])
    return pl.pallas_call(
        _moe_ffn_kernel, name="moe_ffn", grid_spec=grid_spec,
        out_shape=jax.ShapeDtypeStruct((MOE_ROWS, D_MODEL), F32),
        compiler_params=_cparams(("arbitrary",)),
    )(tile_expert, pos_kmajor, n_active, pad_rows, x, g.reshape(1, D_MODEL), w_gate, w_up, w_down)


def _moe_combine_kernel(pos_ref, x_ref, cw_ref, g_ref, y_hbm, o_ref, buf, sem, *, final_norm):
    i = pl.program_id(0)
    n = pl.num_programs(0)
    tm = CMB_TM
    slot = i % 2

    def issue(tile, s):
        for k in range(MOE_TOPK):
            _gather_rows(y_hbm, pos_ref, (k * (TOKENS // tm) + tile) * tm, buf.at[s, k], sem.at[s], tm)

    @pl.when(i == 0)
    def _():
        issue(0, 0)

    @pl.when(i + 1 < n)
    def _():
        issue(i + 1, 1 - slot)

    for k in range(MOE_TOPK):
        pltpu.make_async_copy(y_hbm.at[pl.ds(0, tm), :], buf.at[slot, k], sem.at[slot]).wait()
    cw = cw_ref[...]
    out = x_ref[...] + cw[:, 0:1] * buf[slot, 0] + cw[:, 1:2] * buf[slot, 1]
    if final_norm:
        out = _rms(out, g_ref[...])
    o_ref[...] = out


def _moe_combine(x, cw, y_sorted, pos_kmajor, g_final, final_norm):
    grid_spec = pltpu.PrefetchScalarGridSpec(
        num_scalar_prefetch=1, grid=(TOKENS // CMB_TM,),
        in_specs=[pl.BlockSpec((CMB_TM, D_MODEL), lambda i, pos: (i, 0)),
                  pl.BlockSpec((CMB_TM, LANES), lambda i, pos: (i, 0)),
                  pl.BlockSpec((1, D_MODEL), lambda i, pos: (0, 0)),
                  pl.BlockSpec(memory_space=pl.ANY)],
        out_specs=pl.BlockSpec((CMB_TM, D_MODEL), lambda i, pos: (i, 0)),
        scratch_shapes=[pltpu.VMEM((2, MOE_TOPK, CMB_TM, D_MODEL), F32), pltpu.SemaphoreType.DMA((2,))])
    return pl.pallas_call(
        functools.partial(_moe_combine_kernel, final_norm=final_norm), name="moe_combine", grid_spec=grid_spec,
        out_shape=jax.ShapeDtypeStruct((TOKENS, D_MODEL), F32),
        compiler_params=_cparams(("arbitrary",)),
    )(pos_kmajor, x, cw, g_final.reshape(1, D_MODEL), y_sorted)


def _moe_plan(eid):
    e = eid[:, :MOE_TOPK].reshape(-1)
    onehot = (e[:, None] == jnp.arange(MOE_EXPERTS, dtype=I32)[None, :]).astype(I32)
    csum = jnp.cumsum(onehot, axis=0)
    counts = csum[-1]
    rank = jnp.take_along_axis(csum, e[:, None], axis=1)[:, 0] - 1
    padded = ((counts + MOE_TM - 1) // MOE_TM) * MOE_TM
    g_end = jnp.cumsum(padded)
    g_start = g_end - padded
    pos = g_start[e] + rank
    n_active = (g_end[-1] // MOE_TM).astype(I32)
    tile_start = jnp.arange(MOE_TILES, dtype=I32) * MOE_TM
    te = jnp.sum((g_end[None, :] <= tile_start[:, None]).astype(I32), axis=1)
    last = jnp.max(jnp.where(counts > 0, jnp.arange(MOE_EXPERTS, dtype=I32), 0))
    tile_expert = jnp.minimum(te, last)
    pos_kmajor = pos.reshape(TOKENS, MOE_TOPK).T.reshape(-1)
    pad_rows = jnp.concatenate([g_start + counts, g_end])
    return tile_expert, n_active.reshape(1), pos_kmajor, pad_rows


def _hier_moe_add(x, ln_g, w_group, b_group, w_expert, b_expert, w_gate, w_up, w_down, layer,
                  g_final, final_norm):
    eid, cw = _router(x, ln_g, w_group, b_group, w_expert, b_expert)
    tile_expert, n_active, pos_kmajor, pad_rows = _moe_plan(eid)
    y_sorted = _moe_ffn(x, ln_g, w_gate, w_up, w_down, layer, tile_expert, pos_kmajor, n_active, pad_rows)
    return _moe_combine(x, cw, y_sorted, pos_kmajor, g_final, final_norm)


def _rope_tables():
    pos = jnp.arange(SEQ, dtype=F32)
    inv = 1.0 / (ROPE_THETA ** (jnp.arange(0, NSA_HEAD_DIM, 2, dtype=F32) / NSA_HEAD_DIM))
    ang = pos[:, None] * inv[None, :]
    cos, sin = jnp.cos(ang), jnp.sin(ang)
    return jnp.concatenate([cos, cos], axis=1), jnp.concatenate([-sin, sin], axis=1)


def _nsa_mixer_add(x, ln_g, w_in, cmp_pe, cmp_w1, cmp_w2, w_out):
    w_in_t = w_in.T
    hn, _, g_lin_t = _norm_small(x, ln_g, w_in_t[NSA_MAIN:].T, transposed=True)
    cos_full, sin_signed = _rope_tables()
    heads = _nsa_inproj(hn, w_in_t, cos_full, sin_signed)
    first_c = NSA_HEADS
    kv_c = heads[first_c:first_c + 2 * NSA_KV_HEADS]
    kv_chunks = kv_c.reshape(2, NSA_KV_HEADS, BATCH, SEQ // CMP_STRIDE, CMP_STRIDE * NSA_HEAD_DIM)
    kc_vc = _compress(kv_chunks, cmp_pe, cmp_w1, cmp_w2)
    gates_t = g_lin_t[:NSA_GATES].reshape(NSA_KV_HEADS, 3 * NSA_Q_PER_KV, TOKENS)
    o = _nsa_attention(heads, kc_vc, gates_t)
    return _outproj_resid(o, w_out, x)


def _ssd_mixer_add(x, ln_g, w_in, conv_w, conv_b, dt_bias, a_log, d_skip, norm_g, w_out):
    w_in_t = w_in.T
    hn, dt_small, dt_small_t = _norm_small(x, ln_g, w_in_t[SSD_MAIN:].T, transposed=True)
    zx_tiles = _matmul_tiles(hn, w_in_t, SSD_MAIN, 1024)
    y_tiles = _ssd_chunks(zx_tiles, dt_small, dt_small_t, conv_w, conv_b, dt_bias, a_log, d_skip, norm_g)
    return _outproj_resid(y_tiles, w_out, x)


def kernel(x, ln_mix, ln_ffn, ln_final, nsa_w_in, nsa_cmp_pe, nsa_cmp_w1, nsa_cmp_w2, nsa_w_out,
           ssd_w_in, ssd_conv_w, ssd_conv_b, ssd_dt_bias, ssd_a_log, ssd_d, ssd_norm, ssd_w_out,
           moe_w_group, moe_b_group, moe_w_expert, moe_b_expert, moe_w_gate, moe_w_up, moe_w_down):
    h = x.reshape(TOKENS, D_MODEL)
    for i in range(DEPTH):
        j = i // N_MIXERS
        if i % N_MIXERS == 0:
            h = _nsa_mixer_add(h, ln_mix[i], nsa_w_in[j], nsa_cmp_pe[j], nsa_cmp_w1[j], nsa_cmp_w2[j],
                               nsa_w_out[j])
        else:
            h = _ssd_mixer_add(h, ln_mix[i], ssd_w_in[j], ssd_conv_w[j], ssd_conv_b[j], ssd_dt_bias[j],
                               ssd_a_log[j], ssd_d[j], ssd_norm[j], ssd_w_out[j])
        h = _hier_moe_add(h, ln_ffn[i], moe_w_group[i], moe_b_group[i], moe_w_expert[i], moe_b_expert[i],
                          moe_w_gate, moe_w_up, moe_w_down, i, ln_final, i == DEPTH - 1)
    return h.reshape(BATCH, SEQ, D_MODEL)
```

```python
import functools

import jax
import jax.numpy as jnp
from jax import lax
from jax.experimental import pallas as pl
from jax.experimental.pallas import tpu as pltpu

F32 = jnp.float32
BF16 = jnp.bfloat16
I32 = jnp.int32

D_MODEL = 2048
BATCH = 4
SEQ = 2048
TOKENS = BATCH * SEQ
DEPTH = 2
N_MIXERS = 2
NORM_EPS = 1e-6
NEG_INF = -1e30
ROPE_THETA = 10000.0

NSA_HEADS = 16
NSA_KV_HEADS = 4
NSA_HEAD_DIM = D_MODEL // NSA_HEADS
NSA_Q_PER_KV = NSA_HEADS // NSA_KV_HEADS
CMP_BLOCK = 32
CMP_STRIDE = 16
CMP_HIDDEN = 256
N_CMP = (SEQ - CMP_BLOCK) // CMP_STRIDE + 1
SLC_BLOCK = 64
SLC_TOPK = 16
SLC_LOCAL = 2
SLC_FORCE = 1e4
N_SLC = SEQ // SLC_BLOCK
WINDOW = 512
NSA_QD = NSA_HEADS * NSA_HEAD_DIM
NSA_KVD = NSA_KV_HEADS * NSA_HEAD_DIM
NSA_MAIN = NSA_QD + 6 * NSA_KVD
NSA_GATES = 3 * NSA_HEADS

SSD_D_INNER = 2 * D_MODEL
SSD_HEAD_DIM = 64
SSD_HEADS = SSD_D_INNER // SSD_HEAD_DIM
SSD_GROUPS = 8
SSD_HEADS_PER_GROUP = SSD_HEADS // SSD_GROUPS
SSD_D_STATE = 128
SSD_CONV = 4
SSD_CHUNK = 128
SSD_GROUP_W = SSD_D_INNER // SSD_GROUPS
SSD_BC = SSD_GROUPS * SSD_D_STATE
SSD_CONV_CH = SSD_D_INNER + 2 * SSD_BC
SSD_MAIN = SSD_D_INNER + SSD_CONV_CH
SSD_FAC_PAD = 16

MOE_GROUPS = 4
MOE_EPG = 8
MOE_EXPERTS = MOE_GROUPS * MOE_EPG
MOE_TOPK = 2
MOE_D_FF = 512

LANES = 128
VMEM_LIMIT = 56 * 1024 * 1024

NORM_TM = 256
MM_TM = 512
OUT_TM = 256
ATT_TQ = 256
ATT_TK = 128
ATT_CHUNK_TILES = 4
ATT_ONES = 16
MOE_TM = 256
MOE_TILES = (TOKENS * MOE_TOPK) // MOE_TM + MOE_EXPERTS
MOE_ROWS = MOE_TILES * MOE_TM
CMB_TM = 128
GATHER_UNROLL = 8
DMA_QUEUES = 2
MOE_ISSUE_PARTS = 3


def _cparams(sem):
    return pltpu.CompilerParams(dimension_semantics=sem, vmem_limit_bytes=VMEM_LIMIT)


def _split3(x):
    hi = x.astype(BF16)
    r1 = x - hi.astype(F32)
    mid = r1.astype(BF16)
    lo = (r1 - mid.astype(F32)).astype(BF16)
    return hi, mid, lo


def _dot(a, b):
    return jnp.dot(a, b, preferred_element_type=F32)


def _dot_nt(a, b):
    return lax.dot_general(a, b, (((1,), (1,)), ((), ())), preferred_element_type=F32)


def _dot_tn(a, b):
    return lax.dot_general(a, b, (((0,), (0,)), ((), ())), preferred_element_type=F32)


def _dot_split_lhs(x, m_bf16):
    hi, mid, lo = _split3(x)
    return _dot(hi, m_bf16) + _dot(mid, m_bf16) + _dot(lo, m_bf16)


def _dot_split_rhs(m_bf16, x):
    hi, mid, lo = _split3(x)
    return _dot(m_bf16, hi) + _dot(m_bf16, mid) + _dot(m_bf16, lo)


def _dot_x3(a, w):
    a_hi = a.astype(BF16)
    a_lo = (a - a_hi.astype(F32)).astype(BF16)
    w_hi = w.astype(BF16)
    w_lo = (w - w_hi.astype(F32)).astype(BF16)
    return _dot(a_hi, w_hi) + _dot(a_hi, w_lo) + _dot(a_lo, w_hi)


def _dot_x3_nt(a, w):
    a_hi = a.astype(BF16)
    a_lo = (a - a_hi.astype(F32)).astype(BF16)
    w_hi = w.astype(BF16)
    w_lo = (w - w_hi.astype(F32)).astype(BF16)
    return _dot_nt(a_hi, w_hi) + _dot_nt(a_hi, w_lo) + _dot_nt(a_lo, w_hi)


def _rms(x, g):
    y = x * lax.rsqrt(jnp.mean(x * x, axis=-1, keepdims=True) + NORM_EPS)
    return y * g


def _norm_small_kernel(x_ref, g_ref, ws_ref, hn_ref, small_ref):
    y = _rms(x_ref[...], g_ref[...])
    hn_ref[...] = y.astype(BF16)
    small_ref[...] = _dot_x3(y, ws_ref[...])


def _norm_small_t_kernel(x_ref, g_ref, ws_ref, hn_ref, small_ref, small_t_ref):
    y = _rms(x_ref[...], g_ref[...])
    hn_ref[...] = y.astype(BF16)
    small = _dot_x3(y, ws_ref[...])
    small_ref[...] = small
    small_t_ref[...] = small.T


def _norm_small(x, g, w_small, transposed=False):
    n = w_small.shape[1]
    ws = jnp.zeros((D_MODEL, LANES), F32).at[:, :n].set(w_small)
    grid = (TOKENS // NORM_TM,)
    x_spec = pl.BlockSpec((NORM_TM, D_MODEL), lambda i: (i, 0))
    g_spec = pl.BlockSpec((1, D_MODEL), lambda i: (0, 0))
    w_spec = pl.BlockSpec((D_MODEL, LANES), lambda i: (0, 0))
    hn_spec = pl.BlockSpec((NORM_TM, D_MODEL), lambda i: (i, 0))
    sm_spec = pl.BlockSpec((NORM_TM, LANES), lambda i: (i, 0))
    hn_shape = jax.ShapeDtypeStruct((TOKENS, D_MODEL), BF16)
    sm_shape = jax.ShapeDtypeStruct((TOKENS, LANES), F32)
    if not transposed:
        return pl.pallas_call(
            _norm_small_kernel, name="norm_small", grid=grid,
            in_specs=[x_spec, g_spec, w_spec],
            out_specs=[hn_spec, sm_spec],
            out_shape=[hn_shape, sm_shape],
            compiler_params=_cparams(("parallel",)),
        )(x, g.reshape(1, D_MODEL), ws)
    smt_spec = pl.BlockSpec((LANES, NORM_TM), lambda i: (0, i))
    smt_shape = jax.ShapeDtypeStruct((LANES, TOKENS), F32)
    return pl.pallas_call(
        _norm_small_t_kernel, name="norm_small_t", grid=grid,
        in_specs=[x_spec, g_spec, w_spec],
        out_specs=[hn_spec, sm_spec, smt_spec],
        out_shape=[hn_shape, sm_shape, smt_shape],
        compiler_params=_cparams(("parallel",)),
    )(x, g.reshape(1, D_MODEL), ws)


def _router_kernel(x_ref, g_ref, ws_ref, b_ref, eid_ref, cw_ref):
    y = _rms(x_ref[...], g_ref[...])
    logits = _dot_x3(y, ws_ref[...]) + b_ref[...]
    lane = lax.broadcasted_iota(I32, logits.shape, 1)
    big = jnp.int32(LANES)
    neg = -jnp.inf
    gl = jnp.where(lane < MOE_GROUPS, logits, neg)
    gmax = jnp.max(gl, axis=-1, keepdims=True)
    gsum = jnp.sum(jnp.exp(gl - gmax), axis=-1, keepdims=True)
    g_w = 1.0 / gsum
    g_sel = jnp.min(jnp.where(gl == gmax, lane, big), axis=-1, keepdims=True)
    lo = MOE_GROUPS + g_sel * MOE_EPG
    el = jnp.where((lane >= lo) & (lane < lo + MOE_EPG), logits, neg)
    v1 = jnp.max(el, axis=-1, keepdims=True)
    i1 = jnp.min(jnp.where(el == v1, lane, big), axis=-1, keepdims=True)
    el2 = jnp.where(lane == i1, neg, el)
    v2 = jnp.max(el2, axis=-1, keepdims=True)
    i2 = jnp.min(jnp.where(el2 == v2, lane, big), axis=-1, keepdims=True)
    e2 = jnp.exp(v2 - v1)
    den = 1.0 + e2
    w1 = (1.0 / den) * g_w
    w2 = (e2 / den) * g_w
    eid_ref[...] = jnp.where(lane == 0, i1 - MOE_GROUPS, jnp.where(lane == 1, i2 - MOE_GROUPS, 0))
    cw_ref[...] = jnp.where(lane == 0, w1, jnp.where(lane == 1, w2, 0.0))


def _router(x, g, w_group, b_group, w_expert, b_expert):
    n = MOE_GROUPS + MOE_EXPERTS
    ws = jnp.zeros((D_MODEL, LANES), F32).at[:, :n].set(jnp.concatenate([w_group, w_expert], axis=1))
    bs = jnp.zeros((1, LANES), F32).at[0, :n].set(jnp.concatenate([b_group, b_expert]))
    return pl.pallas_call(
        _router_kernel, name="router", grid=(TOKENS // NORM_TM,),
        in_specs=[pl.BlockSpec((NORM_TM, D_MODEL), lambda i: (i, 0)),
                  pl.BlockSpec((1, D_MODEL), lambda i: (0, 0)),
                  pl.BlockSpec((D_MODEL, LANES), lambda i: (0, 0)),
                  pl.BlockSpec((1, LANES), lambda i: (0, 0))],
        out_specs=[pl.BlockSpec((NORM_TM, LANES), lambda i: (i, 0)),
                   pl.BlockSpec((NORM_TM, LANES), lambda i: (i, 0))],
        out_shape=[jax.ShapeDtypeStruct((TOKENS, LANES), I32),
                   jax.ShapeDtypeStruct((TOKENS, LANES), F32)],
        compiler_params=_cparams(("parallel",)),
    )(x, g.reshape(1, D_MODEL), ws, bs)


def _inproj_heads_kernel(a_ref, w_ref, cos_ref, sin_ref, o_ref, wbf_ref):
    j = pl.program_id(0)

    @pl.when(pl.program_id(1) == 0)
    def _():
        wbf_ref[...] = w_ref[...].astype(BF16)

    acc = _dot_nt(a_ref[...], wbf_ref[...])
    q_tiles = NSA_QD // acc.shape[1]
    c = cos_ref[...]
    s = sin_ref[...]

    def head(h):
        return acc[:, h * NSA_HEAD_DIM:(h + 1) * NSA_HEAD_DIM]

    def rotary(xh):
        return (xh * c + pltpu.roll(xh, NSA_HEAD_DIM // 2, 1) * s).astype(BF16)

    for h in range(NSA_KV_HEADS):
        o_ref[h] = rotary(head(h))

    @pl.when(j < q_tiles)
    def _():
        for h in range(NSA_KV_HEADS, 2 * NSA_KV_HEADS):
            o_ref[h] = rotary(head(h))

    @pl.when(j >= q_tiles)
    def _():
        for h in range(NSA_KV_HEADS, 2 * NSA_KV_HEADS):
            o_ref[h] = head(h).astype(BF16)


def _nsa_inproj(hn, w_in_t, cos_full, sin_signed):
    tn = 2 * NSA_KVD
    n_heads_tile = tn // NSA_HEAD_DIM
    s_tiles = SEQ // MM_TM
    return pl.pallas_call(
        _inproj_heads_kernel, name="nsa_inproj", grid=(NSA_MAIN // tn, TOKENS // MM_TM),
        in_specs=[pl.BlockSpec((MM_TM, D_MODEL), lambda j, i: (i, 0)),
                  pl.BlockSpec((tn, D_MODEL), lambda j, i: (j, 0)),
                  pl.BlockSpec((MM_TM, NSA_HEAD_DIM), lambda j, i: (i % s_tiles, 0)),
                  pl.BlockSpec((MM_TM, NSA_HEAD_DIM), lambda j, i: (i % s_tiles, 0))],
        out_specs=pl.BlockSpec((n_heads_tile, MM_TM, NSA_HEAD_DIM), lambda j, i: (j, i, 0)),
        out_shape=jax.ShapeDtypeStruct((NSA_MAIN // NSA_HEAD_DIM, TOKENS, NSA_HEAD_DIM), BF16),
        scratch_shapes=[pltpu.VMEM((tn, D_MODEL), BF16)],
        compiler_params=_cparams(("arbitrary", "arbitrary")),
    )(hn, w_in_t, cos_full, sin_signed)


def _matmul_tiles_kernel(a_ref, w_ref, o_ref, wbf_ref):
    @pl.when(pl.program_id(1) == 0)
    def _():
        wbf_ref[...] = w_ref[...].astype(BF16)

    acc = _dot_nt(a_ref[...], wbf_ref[...])
    for t in range(acc.shape[1] // LANES):
        o_ref[t] = acc[:, t * LANES:(t + 1) * LANES].astype(o_ref.dtype)


def _matmul_tiles(a, w_t, n_cols, tn):
    k = a.shape[1]
    return pl.pallas_call(
        _matmul_tiles_kernel, name="matmul_tiles", grid=(n_cols // tn, TOKENS // MM_TM),
        in_specs=[pl.BlockSpec((MM_TM, k), lambda j, i: (i, 0)),
                  pl.BlockSpec((tn, k), lambda j, i: (j, 0))],
        out_specs=pl.BlockSpec((tn // LANES, MM_TM, LANES), lambda j, i: (j, i, 0)),
        out_shape=jax.ShapeDtypeStruct((n_cols // LANES, TOKENS, LANES), BF16),
        scratch_shapes=[pltpu.VMEM((tn, k), BF16)],
        compiler_params=_cparams(("arbitrary", "arbitrary")),
    )(a, w_t)


def _outproj_kernel(a_ref, w_ref, r_ref, o_ref):
    if len(a_ref.shape) == 3:
        a = jnp.concatenate([a_ref[t] for t in range(a_ref.shape[0])], axis=1)
    else:
        a = a_ref[...]
    o_ref[...] = r_ref[...] + _dot(a, w_ref[...])


def _outproj_resid(a, w, resid):
    tiled = a.ndim == 3
    k, n = w.shape
    a_spec = (pl.BlockSpec((k // LANES, OUT_TM, LANES), lambda i: (0, i, 0)) if tiled
              else pl.BlockSpec((OUT_TM, k), lambda i: (i, 0)))
    return pl.pallas_call(
        _outproj_kernel, name="outproj_resid", grid=(TOKENS // OUT_TM,),
        in_specs=[a_spec,
                  pl.BlockSpec((k, n), lambda i: (0, 0)),
                  pl.BlockSpec((OUT_TM, n), lambda i: (i, 0))],
        out_specs=pl.BlockSpec((OUT_TM, n), lambda i: (i, 0)),
        out_shape=jax.ShapeDtypeStruct((TOKENS, n), F32),
        compiler_params=_cparams(("parallel",)),
    )(a, w.astype(BF16), resid)


def _compress_kernel(x_ref, pe_ref, w1_ref, w2_ref, o_ref):
    half = CMP_STRIDE * NSA_HEAD_DIM
    x = x_ref[0, 0, 0]
    w1 = w1_ref[0].astype(BF16)
    top = _dot(x, w1[:half])
    bot = _dot(x, w1[half:])
    pe = jnp.broadcast_to(pe_ref[0], (8, 2 * half)).astype(BF16)
    pe_bias = _dot(pe, w1)[0:1]
    hid = top + pltpu.roll(bot, bot.shape[0] - 1, 0) + pe_bias
    act = jax.nn.gelu(hid)
    o_ref[0, 0, 0] = _dot(act.astype(BF16), w2_ref[0].astype(BF16))


def _compress(kv_chunks, pe, w1, w2):
    n_chunk = SEQ // CMP_STRIDE
    feat = CMP_STRIDE * NSA_HEAD_DIM
    return pl.pallas_call(
        _compress_kernel, name="nsa_compress", grid=(2, NSA_KV_HEADS, BATCH),
        in_specs=[pl.BlockSpec((1, 1, 1, n_chunk, feat), lambda a, g, b: (a, g, b, 0, 0)),
                  pl.BlockSpec((1, 1, 2 * feat), lambda a, g, b: (a, 0, 0)),
                  pl.BlockSpec((1, 2 * feat, CMP_HIDDEN), lambda a, g, b: (a, 0, 0)),
                  pl.BlockSpec((1, CMP_HIDDEN, NSA_HEAD_DIM), lambda a, g, b: (a, 0, 0))],
        out_specs=pl.BlockSpec((1, 1, 1, n_chunk, NSA_HEAD_DIM), lambda a, g, b: (a, g, b, 0, 0)),
        out_shape=jax.ShapeDtypeStruct((2, NSA_KV_HEADS, BATCH, n_chunk, NSA_HEAD_DIM), F32),
        compiler_params=_cparams(("parallel", "parallel", "parallel")),
    )(kv_chunks, pe.reshape(2, 1, 2 * feat), w1, w2)


def _nsa_attn_kernel(q_ref, ks_ref, vs_ref, kw_ref, vw_ref, kc_ref, vc_ref, gate_ref, o_ref,
                     vst_scr, vwt_scr, sel_scr, acc_scr):
    qi = pl.program_id(2)
    tq = ATT_TQ
    tk = ATT_TK
    dh = NSA_HEAD_DIM
    r_heads = NSA_Q_PER_KV
    n_kt = SEQ // tk
    scale = dh ** -0.5

    @pl.when(qi == 0)
    def _():
        ones = jnp.ones((ATT_ONES, tk), BF16)
        for kt in range(n_kt):
            rows = slice(kt * tk, (kt + 1) * tk)
            vst_scr[kt, 0:dh, :] = vs_ref[0, rows, :].astype(F32).T.astype(BF16)
            vwt_scr[kt, 0:dh, :] = vw_ref[0, rows, :].astype(F32).T.astype(BF16)
            vst_scr[kt, dh:dh + ATT_ONES, :] = ones
            vwt_scr[kt, dh:dh + ATT_ONES, :] = ones

    q_t = jnp.concatenate([q_ref[r].astype(F32).T for r in range(r_heads)], axis=1).astype(BF16)
    n_cp = SEQ // CMP_STRIDE
    sub = lax.broadcasted_iota(I32, (n_cp, tq), 0)
    t_pos = qi * tq + lax.broadcasted_iota(I32, (n_cp, tq), 1)

    def tile4(a):
        return jnp.concatenate([a] * r_heads, axis=1)

    kc = kc_ref[0, 0, 0].astype(BF16)
    vc = vc_ref[0, 0, 0].astype(BF16)
    ok_c = jnp.where(sub * CMP_STRIDE + CMP_BLOCK - 1 <= t_pos, jnp.where(sub < N_CMP, 1.0, 0.0), 0.0)
    ok_c4 = tile4(ok_c)
    s_c = _dot(kc, q_t) * scale + (ok_c4 - 1.0) * (-NEG_INF)
    e_c = jnp.exp(s_c - jnp.max(s_c, axis=0, keepdims=True))
    p_c = (e_c / jnp.sum(e_c, axis=0, keepdims=True)) * ok_c4
    o_cmp = _dot_tn(vc, p_c.astype(BF16))
    p_sum = p_c[:, 0:tq]
    for r in range(1, r_heads):
        p_sum = p_sum + p_c[:, r * tq:(r + 1) * tq]

    blk_row = lax.broadcasted_iota(I32, (LANES, LANES), 0)
    cmp_col = lax.broadcasted_iota(I32, (LANES, LANES), 1)
    s_start = blk_row * SLC_BLOCK
    c_start = cmp_col * CMP_STRIDE
    ov_t = jnp.maximum(jnp.minimum(c_start + CMP_BLOCK, s_start + SLC_BLOCK)
                       - jnp.maximum(c_start, s_start), 0).astype(F32) / CMP_BLOCK
    ov_t = jnp.where(blk_row < N_SLC, ov_t, 0.0).astype(BF16)
    imp = _dot_split_rhs(ov_t, p_sum)[0:N_SLC]
    j_blk = lax.broadcasted_iota(I32, (N_SLC, tq), 0)
    dist = (qi * tq + lax.broadcasted_iota(I32, (N_SLC, tq), 1)) // SLC_BLOCK - j_blk
    imp = jnp.where(j_blk == 0, SLC_FORCE, jnp.where(dist < 0, imp, jnp.where(dist < SLC_LOCAL, SLC_FORCE, imp)))
    imp = jnp.where(dist >= 0, imp, -jnp.inf)
    cnt = jnp.zeros((N_SLC, tq), I32)
    for k in range(N_SLC):
        row_k = imp[k:k + 1, :]
        tie = jnp.where(j_blk > k, 1, 0)
        cnt = cnt + jnp.where(row_k > imp, 1, jnp.where(row_k == imp, tie, 0))
    sel = jnp.where(cnt < min(SLC_TOPK, N_SLC), 1.0, 0.0)
    for j in range(N_SLC):
        sel_scr[8 * j:8 * j + 8, :] = jnp.broadcast_to(sel[j:j + 1, :], (8, tq))

    def scores(k, ok):
        return _dot(k, q_t) * scale + tile4((ok - 1.0) * (-NEG_INF))

    def weighted_values(vt_scr, kt0, pr, n_tiles):
        out = None
        for u in range(n_tiles):
            term = _dot(vt_scr[kt0 + u], pr[u * tk:(u + 1) * tk].astype(BF16))
            out = term if out is None else out + term
        return out

    n_ct = ATT_CHUNK_TILES
    ck = n_ct * tk
    sub_c = lax.broadcasted_iota(I32, (ck, tq), 0)
    t_pos_c = qi * tq + lax.broadcasted_iota(I32, (ck, tq), 1)
    blocks_per_chunk = ck // SLC_BLOCK
    acc_scr[...] = jnp.zeros(acc_scr.shape, F32)

    def slc_body(c, m_old):
        start = pl.multiple_of(c * ck, ck)
        k = ks_ref[0, pl.ds(start, ck), :]
        rows8 = sel_scr[pl.ds(pl.multiple_of(c * (8 * blocks_per_chunk), 8 * blocks_per_chunk),
                              8 * blocks_per_chunk), :]
        picked = jnp.concatenate(
            [rows8[8 * u:8 * u + 8] for u in range(blocks_per_chunk) for _ in range(SLC_BLOCK // 8)], axis=0)
        sc = scores(k, jnp.where(start + sub_c <= t_pos_c, picked, 0.0))
        m_new = jnp.maximum(m_old, jnp.max(sc, axis=0, keepdims=True))
        alpha = jnp.exp(m_old - m_new)
        pr = jnp.exp(sc - m_new)
        acc_scr[...] = alpha * acc_scr[...] + weighted_values(vst_scr, c * n_ct, pr, n_ct)
        return m_new

    lax.fori_loop(0, ((qi + 1) * tq + ck - 1) // ck, slc_body, jnp.full((1, r_heads * tq), NEG_INF, F32))
    acc = acc_scr[...]
    o_slc = acc[0:dh] / acc[dh:dh + 1]

    n_wt = (WINDOW + tq) // tk
    kt0 = jnp.maximum(qi * (tq // tk) - WINDOW // tk, 0)
    w_start = pl.multiple_of(kt0 * tk, tk)
    key_w = w_start + lax.broadcasted_iota(I32, (n_wt * tk, tq), 0)
    t_pos_w = qi * tq + lax.broadcasted_iota(I32, (n_wt * tk, tq), 1)
    ok_w = jnp.where(key_w <= t_pos_w, jnp.where(key_w > t_pos_w - WINDOW, 1.0, 0.0), 0.0)
    sc_w = scores(kw_ref[0, pl.ds(w_start, n_wt * tk), :], ok_w)
    pr_w = jnp.exp(sc_w - jnp.max(sc_w, axis=0, keepdims=True))
    acc_w = weighted_values(vwt_scr, kt0, pr_w, n_wt)
    o_win = acc_w[0:dh] / acc_w[dh:dh + 1]

    gate = jax.nn.sigmoid(gate_ref[0])
    for r in range(r_heads):
        cols = slice(r * tq, (r + 1) * tq)
        o = (gate[3 * r:3 * r + 1] * o_cmp[:, cols] + gate[3 * r + 1:3 * r + 2] * o_slc[:, cols]
             + gate[3 * r + 2:3 * r + 3] * o_win[:, cols])
        o_ref[:, r * dh:(r + 1) * dh] = o.T.astype(BF16)


def _nsa_attention(heads, kc_vc, gates):
    tq = ATT_TQ
    nq = SEQ // tq
    r = NSA_Q_PER_KV
    g_heads = NSA_KV_HEADS
    q_spec = pl.BlockSpec((r, tq, NSA_HEAD_DIM), lambda b, g, i: (g, b * nq + i, 0))

    def kv_spec(first_head):
        return pl.BlockSpec((1, SEQ, NSA_HEAD_DIM), lambda b, g, i: (first_head + g, b, 0))

    first = NSA_HEADS
    specs = [q_spec,
             kv_spec(first + 2 * g_heads), kv_spec(first + 3 * g_heads),
             kv_spec(first + 4 * g_heads), kv_spec(first + 5 * g_heads),
             pl.BlockSpec((1, 1, 1, SEQ // CMP_STRIDE, NSA_HEAD_DIM), lambda b, g, i: (0, g, b, 0, 0)),
             pl.BlockSpec((1, 1, 1, SEQ // CMP_STRIDE, NSA_HEAD_DIM), lambda b, g, i: (1, g, b, 0, 0)),
             pl.BlockSpec((1, 3 * r, tq), lambda b, g, i: (g, 0, b * nq + i))]
    vt_shape = (SEQ // ATT_TK, NSA_HEAD_DIM + ATT_ONES, ATT_TK)
    return pl.pallas_call(
        _nsa_attn_kernel, name="nsa_attn", grid=(BATCH, g_heads, nq),
        in_specs=specs,
        out_specs=pl.BlockSpec((tq, r * NSA_HEAD_DIM), lambda b, g, i: (b * nq + i, g)),
        out_shape=jax.ShapeDtypeStruct((TOKENS, NSA_QD), BF16),
        scratch_shapes=[pltpu.VMEM(vt_shape, BF16), pltpu.VMEM(vt_shape, BF16),
                        pltpu.VMEM((8 * N_SLC, tq), F32),
                        pltpu.VMEM((NSA_HEAD_DIM + ATT_ONES, r * tq), F32)],
        compiler_params=_cparams(("arbitrary", "arbitrary", "arbitrary")),
    )(heads, heads, heads, heads, heads, kc_vc, kc_vc, gates)


def _ssd_chunk_kernel(zx_ref, dtc_ref, dtr_ref, cw_ref, cb_ref, dtb_c_ref, alog_c_ref, dtb_r_ref, alog_r_ref,
                      dskip_ref, ng_ref, shift_ref, echan_ref, ehead_ref, o_ref,
                      prev_scr, acum_r_scr, st_scr, fac_scr, acp_scr):
    chunk = pl.program_id(1)
    L = SSD_CHUNK
    W = SSD_GROUP_W
    hpg = SSD_HEADS_PER_GROUP
    n_xt = W // LANES
    x0 = SSD_D_INNER // LANES
    b0 = 2 * SSD_D_INNER // LANES
    c0 = b0 + SSD_GROUPS
    cb0 = SSD_D_INNER // LANES
    cc0 = cb0 + SSD_GROUPS

    @pl.when(chunk == 0)
    def _():
        prev_scr[...] = jnp.zeros(prev_scr.shape, BF16)
        st_scr[...] = jnp.zeros(st_scr.shape, F32)

    dt_c = jax.nn.softplus(dtc_ref[...] + dtb_c_ref[...])
    adt_c = dt_c * (-jnp.exp(alog_c_ref[...]))
    dt_r = jax.nn.softplus(dtr_ref[...] + dtb_r_ref[...])
    adt_r = dt_r * (-jnp.exp(alog_r_ref[...]))
    row = lax.broadcasted_iota(I32, (L, L), 0)
    col = lax.broadcasted_iota(I32, (L, L), 1)
    causal = row >= col
    tri = jnp.where(causal, 1.0, 0.0).astype(BF16)
    tri_t = jnp.where(col >= row, 1.0, 0.0).astype(BF16)
    acum_c = _dot_split_rhs(tri, adt_c)
    acum_r_scr[...] = _dot_split_lhs(adt_r, tri_t)
    a_last = acum_c[L - 1:L, :]
    fac = jnp.concatenate([dt_c, jnp.exp(acum_c), jnp.exp(a_last - acum_c),
                           jnp.broadcast_to(jnp.exp(a_last), (SSD_FAC_PAD, LANES))], axis=0)
    fac_hi = fac.astype(BF16)
    fac_scr[0] = fac_hi
    fac_scr[1] = (fac - fac_hi.astype(F32)).astype(BF16)
    for i, part in enumerate(_split3(acum_c)):
        acp_scr[i] = part
    lane_w = lax.broadcasted_iota(I32, (L, LANES), 1)
    first_half = lane_w < SSD_HEAD_DIM

    def tiles(ref, first, n):
        return jnp.concatenate([ref[first + q] for q in range(n)], axis=1)

    def group_body(g, carry):
        e_chan = echan_ref[g]
        ex = _dot(fac_scr[0], e_chan) + _dot(fac_scr[1], e_chan)
        dt_x, ea_x, sd_x, cd_x = ex[0:L], ex[L:2 * L], ex[2 * L:3 * L], ex[3 * L:3 * L + 1]
        e_head = ehead_ref[g]
        acum_b = _dot(acp_scr[0], e_head) + _dot(acp_scr[1], e_head) + _dot(acp_scr[2], e_head)

        def conv_silu(zx_first, conv_first, n):
            cur = tiles(zx_ref, zx_first, n)
            ext = jnp.concatenate([tiles(prev_scr, conv_first, n), cur], axis=0)
            w = tiles(cw_ref, conv_first, n)
            acc = jnp.broadcast_to(tiles(cb_ref, conv_first, n), (L, n * LANES))
            for k in range(SSD_CONV):
                back = SSD_CONV - 1 - k
                xk = cur.astype(F32) if back == 0 else _dot(shift_ref[back - 1], ext)
                acc = acc + xk * w[k:k + 1, :]
            for q in range(n):
                prev_scr[conv_first + q] = zx_ref[zx_first + q]
            return jax.nn.silu(acc)

        xs = conv_silu(x0 + n_xt * g, n_xt * g, n_xt)
        bm = conv_silu(b0 + g, cb0 + g, 1)
        cm = conv_silu(c0 + g, cc0 + g, 1)

        xdt = xs * dt_x
        cb = jnp.where(causal, _dot_nt(cm.astype(BF16), bm.astype(BF16)), 0.0)
        y_parts = []
        for pair in range(hpg // 2):
            xd = xdt[:, pair * LANES:(pair + 1) * LANES]
            y_pair = None
            for sub in range(2):
                h = 2 * pair + sub
                a_row = acum_r_scr[pl.ds(g * hpg + h, 1), :]
                seg = jnp.minimum(acum_b[:, h * L:(h + 1) * L] - a_row, 0.0)
                m_h = (cb * jnp.exp(seg)).astype(BF16)
                x_h = jnp.where(first_half if sub == 0 else jnp.logical_not(first_half), xd, 0.0)
                term = _dot(m_h, x_h.astype(BF16))
                y_pair = term if y_pair is None else y_pair + term
            y_parts.append(y_pair)
        y_diag = jnp.concatenate(y_parts, axis=1)

        st = st_scr[g]
        y_off = _dot(cm.astype(BF16), st.astype(BF16)) * ea_x
        st_scr[g] = st * cd_x + _dot_tn(bm.astype(BF16), (xdt * sd_x).astype(BF16))

        y = y_diag + y_off + xs * tiles(dskip_ref, n_xt * g, n_xt)
        y = y * jax.nn.silu(tiles(zx_ref, n_xt * g, n_xt).astype(F32))
        y = y * lax.rsqrt(jnp.mean(y * y, axis=-1, keepdims=True) + NORM_EPS)
        y = y * tiles(ng_ref, n_xt * g, n_xt)
        for q in range(n_xt):
            o_ref[n_xt * g + q] = y[:, q * LANES:(q + 1) * LANES].astype(BF16)
        return carry

    lax.fori_loop(0, SSD_GROUPS, group_body, 0)


def _ssd_chunks(zx_tiles, dt_small, dt_small_t, conv_w, conv_b, dt_bias, a_log, d_skip, norm_g):
    L = SSD_CHUNK
    nc = SEQ // L
    n_zx = SSD_MAIN // LANES
    n_conv = SSD_CONV_CH // LANES
    n_inner = SSD_D_INNER // LANES
    hpg = SSD_HEADS_PER_GROUP

    def pad_heads(v):
        return jnp.zeros((LANES,), F32).at[:SSD_HEADS].set(v)

    dtb = pad_heads(dt_bias)
    alog = pad_heads(a_log)
    cw = conv_w.reshape(SSD_CONV, n_conv, LANES).transpose(1, 0, 2)
    cb = conv_b.reshape(n_conv, 1, LANES)
    d_chan = jnp.repeat(d_skip, SSD_HEAD_DIM).reshape(n_inner, 1, LANES)
    ng = norm_g.reshape(n_inner, 1, LANES)
    t_idx = jnp.arange(L, dtype=I32)[None, :, None]
    r_idx = jnp.arange(2 * L, dtype=I32)[None, None, :]
    back = jnp.arange(1, SSD_CONV, dtype=I32)[:, None, None]
    shift = (r_idx == L + t_idx - back).astype(BF16)
    head = jnp.arange(LANES, dtype=I32)[None, :, None]
    grp = jnp.arange(SSD_GROUPS, dtype=I32)[:, None, None]
    e_chan = (head == grp * hpg + jnp.arange(SSD_GROUP_W, dtype=I32)[None, None, :] // SSD_HEAD_DIM).astype(BF16)
    e_head = (head == grp * hpg + jnp.arange(hpg * L, dtype=I32)[None, None, :] // L).astype(BF16)
    row = lambda b, c: b * nc + c
    const3 = lambda b, c: (0, 0, 0)
    const2 = lambda b, c: (0, 0)
    in_specs = [
        pl.BlockSpec((n_zx, L, LANES), lambda b, c: (0, row(b, c), 0)),
        pl.BlockSpec((L, LANES), lambda b, c: (row(b, c), 0)),
        pl.BlockSpec((LANES, L), lambda b, c: (0, row(b, c))),
        pl.BlockSpec((n_conv, SSD_CONV, LANES), const3),
        pl.BlockSpec((n_conv, 1, LANES), const3),
        pl.BlockSpec((1, LANES), const2), pl.BlockSpec((1, LANES), const2),
        pl.BlockSpec((LANES, 1), const2), pl.BlockSpec((LANES, 1), const2),
        pl.BlockSpec((n_inner, 1, LANES), const3),
        pl.BlockSpec((n_inner, 1, LANES), const3),
        pl.BlockSpec((SSD_CONV - 1, L, 2 * L), const3),
        pl.BlockSpec((SSD_GROUPS, LANES, SSD_GROUP_W), const3),
        pl.BlockSpec((SSD_GROUPS, LANES, hpg * L), const3),
    ]
    return pl.pallas_call(
        _ssd_chunk_kernel, name="ssd_chunks", grid=(BATCH, nc),
        in_specs=in_specs,
        out_specs=pl.BlockSpec((n_inner, L, LANES), lambda b, c: (0, row(b, c), 0)),
        out_shape=jax.ShapeDtypeStruct((n_inner, TOKENS, LANES), BF16),
        scratch_shapes=[pltpu.VMEM((n_conv, L, LANES), BF16),
                        pltpu.VMEM((LANES, L), F32),
                        pltpu.VMEM((SSD_GROUPS, SSD_D_STATE, SSD_GROUP_W), F32),
                        pltpu.VMEM((2, 3 * L + SSD_FAC_PAD, LANES), BF16),
                        pltpu.VMEM((3, L, LANES), BF16)],
        compiler_params=_cparams(("arbitrary", "arbitrary")),
    )(zx_tiles, dt_small, dt_small_t, cw, cb, dtb.reshape(1, LANES), alog.reshape(1, LANES),
      dtb.reshape(LANES, 1), alog.reshape(LANES, 1), d_chan, ng, shift, e_chan, e_head)


def _gather_rows(src_hbm, idx_ref, base, dst, sem, n_rows):
    def body(r2, carry):
        for q in range(DMA_QUEUES):
            r = DMA_QUEUES * r2 + q
            tok = idx_ref[base + r]
            pltpu.make_async_copy(src_hbm.at[pl.ds(tok, 1), :], dst.at[pl.ds(r, 1), :], sem).start(priority=q)
        return carry

    lax.fori_loop(0, n_rows // DMA_QUEUES, body, 0, unroll=GATHER_UNROLL // DMA_QUEUES)


def _start_rows(src_hbm, idx_ref, base, dst, sem, lo, hi):
    for r in range(lo, hi):
        tok = idx_ref[base + r]
        pltpu.make_async_copy(src_hbm.at[pl.ds(tok, 1), :], dst.at[pl.ds(r, 1), :], sem).start(
            priority=r % DMA_QUEUES)


def _moe_ffn_kernel(te_ref, pos_ref, nact_ref, pad_ref, x_hbm, g_ref, wg_ref, wu_ref, wd_ref, y_ref,
                    tok_ref, buf, sem, wg_bf, wu_bf, wd_bf):
    i = pl.program_id(0)
    n_act = nact_ref[0]
    tm = MOE_TM
    slot = i % 2

    @pl.when(i == 0)
    def _():
        def clear(q, carry):
            tok_ref[q] = 0
            return carry

        def clear_padding(e, carry):
            lax.fori_loop(pad_ref[e], pad_ref[MOE_EXPERTS + e], clear, 0)
            return carry

        def place(p, carry):
            tok_ref[pos_ref[p]] = p % TOKENS
            return carry

        lax.fori_loop(0, MOE_EXPERTS, clear_padding, 0)
        lax.fori_loop(0, TOKENS * MOE_TOPK, place, 0, unroll=GATHER_UNROLL)
        _gather_rows(x_hbm, tok_ref, 0, buf.at[0], sem.at[0], tm)

    e_cur = te_ref[i]
    e_prev = te_ref[jnp.maximum(i - 1, 0)]

    @pl.when((i == 0) | (e_cur != e_prev))
    def _():
        wg_bf[...] = wg_ref[...].astype(BF16)
        wu_bf[...] = wu_ref[...].astype(BF16)
        wd_bf[...] = wd_ref[...].astype(BF16)

    @pl.when(i >= n_act)
    def _():
        y_ref[...] = jnp.zeros(y_ref.shape, F32)

    def tile_body(prefetch_next):
        def start_next(part):
            if prefetch_next:
                lo, hi = (part * tm) // MOE_ISSUE_PARTS, ((part + 1) * tm) // MOE_ISSUE_PARTS
                _start_rows(x_hbm, tok_ref, (i + 1) * tm, buf.at[1 - slot], sem.at[1 - slot], lo, hi)

        pltpu.make_async_copy(x_hbm.at[pl.ds(0, tm), :], buf.at[slot], sem.at[slot]).wait()
        h = _rms(buf[slot], g_ref[...]).astype(BF16)
        start_next(0)
        gate = _dot(h, wg_bf[...])
        start_next(1)
        act = jax.nn.silu(gate) * _dot(h, wu_bf[...])
        start_next(2)
        y_ref[...] = _dot(act.astype(BF16), wd_bf[...])

    @pl.when(i + 1 < n_act)
    def _():
        tile_body(True)

    @pl.when(i + 1 == n_act)
    def _():
        tile_body(False)


def _moe_ffn(x, g, w_gate, w_up, w_down, layer, tile_expert, pos_kmajor, n_active, pad_rows):
    weight = lambda i, te, *_: (layer, te[i], 0, 0)
    grid_spec = pltpu.PrefetchScalarGridSpec(
        num_scalar_prefetch=4, grid=(MOE_TILES,),
        in_specs=[pl.BlockSpec(memory_space=pl.ANY),
                  pl.BlockSpec((1, D_MODEL), lambda i, *_: (0, 0)),
                  pl.BlockSpec((None, None, D_MODEL, MOE_D_FF), weight),
                  pl.BlockSpec((None, None, D_MODEL, MOE_D_FF), weight),
                  pl.BlockSpec((None, None, MOE_D_FF, D_MODEL), weight)],
        out_specs=pl.BlockSpec((MOE_TM, D_MODEL), lambda i, *_: (i, 0)),
        scratch_shapes=[pltpu.SMEM((MOE_ROWS,), I32),
                        pltpu.VMEM((2, MOE_TM, D_MODEL), F32), pltpu.SemaphoreType.DMA((2,)),
                        pltpu.VMEM((D_MODEL, MOE_D_FF), BF16), pltpu.VMEM((D_MODEL, MOE_D_FF), BF16),
                        pltpu.VMEM((MOE_D_FF, D_MODEL), BF16)])
    return pl.pallas_call(
        _moe_ffn_kernel, name="moe_ffn", grid_spec=grid_spec,
        out_shape=jax.ShapeDtypeStruct((MOE_ROWS, D_MODEL), F32),
        compiler_params=_cparams(("arbitrary",)),
    )(tile_expert, pos_kmajor, n_active, pad_rows, x, g.reshape(1, D_MODEL), w_gate, w_up, w_down)


def _moe_combine_kernel(pos_ref, x_ref, cw_ref, g_ref, y_hbm, o_ref, buf, sem, *, final_norm):
    i = pl.program_id(0)
    n = pl.num_programs(0)
    tm = CMB_TM
    slot = i % 2

    def issue(tile, s):
        for k in range(MOE_TOPK):
            _gather_rows(y_hbm, pos_ref, (k * (TOKENS // tm) + tile) * tm, buf.at[s, k], sem.at[s], tm)

    @pl.when(i == 0)
    def _():
        issue(0, 0)

    @pl.when(i + 1 < n)
    def _():
        issue(i + 1, 1 - slot)

    for k in range(MOE_TOPK):
        pltpu.make_async_copy(y_hbm.at[pl.ds(0, tm), :], buf.at[slot, k], sem.at[slot]).wait()
    cw = cw_ref[...]
    out = x_ref[...] + cw[:, 0:1] * buf[slot, 0] + cw[:, 1:2] * buf[slot, 1]
    if final_norm:
        out = _rms(out, g_ref[...])
    o_ref[...] = out


def _moe_combine(x, cw, y_sorted, pos_kmajor, g_final, final_norm):
    grid_spec = pltpu.PrefetchScalarGridSpec(
        num_scalar_prefetch=1, grid=(TOKENS // CMB_TM,),
        in_specs=[pl.BlockSpec((CMB_TM, D_MODEL), lambda i, pos: (i, 0)),
                  pl.BlockSpec((CMB_TM, LANES), lambda i, pos: (i, 0)),
                  pl.BlockSpec((1, D_MODEL), lambda i, pos: (0, 0)),
                  pl.BlockSpec(memory_space=pl.ANY)],
        out_specs=pl.BlockSpec((CMB_TM, D_MODEL), lambda i, pos: (i, 0)),
        scratch_shapes=[pltpu.VMEM((2, MOE_TOPK, CMB_TM, D_MODEL), F32), pltpu.SemaphoreType.DMA((2,))])
    return pl.pallas_call(
        functools.partial(_moe_combine_kernel, final_norm=final_norm), name="moe_combine", grid_spec=grid_spec,
        out_shape=jax.ShapeDtypeStruct((TOKENS, D_MODEL), F32),
        compiler_params=_cparams(("arbitrary",)),
    )(pos_kmajor, x, cw, g_final.reshape(1, D_MODEL), y_sorted)


def _moe_plan(eid):
    e = eid[:, :MOE_TOPK].reshape(-1)
    onehot = (e[:, None] == jnp.arange(MOE_EXPERTS, dtype=I32)[None, :]).astype(I32)
    csum = jnp.cumsum(onehot, axis=0)
    counts = csum[-1]
    rank = jnp.take_along_axis(csum, e[:, None], axis=1)[:, 0] - 1
    padded = ((counts + MOE_TM - 1) // MOE_TM) * MOE_TM
    g_end = jnp.cumsum(padded)
    g_start = g_end - padded
    pos = g_start[e] + rank
    n_active = (g_end[-1] // MOE_TM).astype(I32)
    tile_start = jnp.arange(MOE_TILES, dtype=I32) * MOE_TM
    te = jnp.sum((g_end[None, :] <= tile_start[:, None]).astype(I32), axis=1)
    last = jnp.max(jnp.where(counts > 0, jnp.arange(MOE_EXPERTS, dtype=I32), 0))
    tile_expert = jnp.minimum(te, last)
    pos_kmajor = pos.reshape(TOKENS, MOE_TOPK).T.reshape(-1)
    pad_rows = jnp.concatenate([g_start + counts, g_end])
    return tile_expert, n_active.reshape(1), pos_kmajor, pad_rows


def _hier_moe_add(x, ln_g, w_group, b_group, w_expert, b_expert, w_gate, w_up, w_down, layer,
                  g_final, final_norm):
    eid, cw = _router(x, ln_g, w_group, b_group, w_expert, b_expert)
    tile_expert, n_active, pos_kmajor, pad_rows = _moe_plan(eid)
    y_sorted = _moe_ffn(x, ln_g, w_gate, w_up, w_down, layer, tile_expert, pos_kmajor, n_active, pad_rows)
    return _moe_combine(x, cw, y_sorted, pos_kmajor, g_final, final_norm)


def _rope_tables():
    pos = jnp.arange(SEQ, dtype=F32)
    inv = 1.0 / (ROPE_THETA ** (jnp.arange(0, NSA_HEAD_DIM, 2, dtype=F32) / NSA_HEAD_DIM))
    ang = pos[:, None] * inv[None, :]
    cos, sin = jnp.cos(ang), jnp.sin(ang)
    return jnp.concatenate([cos, cos], axis=1), jnp.concatenate([-sin, sin], axis=1)


def _nsa_mixer_add(x, ln_g, w_in, cmp_pe, cmp_w1, cmp_w2, w_out):
    w_in_t = w_in.T
    hn, _, g_lin_t = _norm_small(x, ln_g, w_in_t[NSA_MAIN:].T, transposed=True)
    cos_full, sin_signed = _rope_tables()
    heads = _nsa_inproj(hn, w_in_t, cos_full, sin_signed)
    first_c = NSA_HEADS
    kv_c = heads[first_c:first_c + 2 * NSA_KV_HEADS]
    kv_chunks = kv_c.reshape(2, NSA_KV_HEADS, BATCH, SEQ // CMP_STRIDE, CMP_STRIDE * NSA_HEAD_DIM)
    kc_vc = _compress(kv_chunks, cmp_pe, cmp_w1, cmp_w2)
    gates_t = g_lin_t[:NSA_GATES].reshape(NSA_KV_HEADS, 3 * NSA_Q_PER_KV, TOKENS)
    o = _nsa_attention(heads, kc_vc, gates_t)
    return _outproj_resid(o, w_out, x)


def _ssd_mixer_add(x, ln_g, w_in, conv_w, conv_b, dt_bias, a_log, d_skip, norm_g, w_out):
    w_in_t = w_in.T
    hn, dt_small, dt_small_t = _norm_small(x, ln_g, w_in_t[SSD_MAIN:].T, transposed=True)
    zx_tiles = _matmul_tiles(hn, w_in_t, SSD_MAIN, 1024)
    y_tiles = _ssd_chunks(zx_tiles, dt_small, dt_small_t, conv_w, conv_b, dt_bias, a_log, d_skip, norm_g)
    return _outproj_resid(y_tiles, w_out, x)


def kernel(x, ln_mix, ln_ffn, ln_final, nsa_w_in, nsa_cmp_pe, nsa_cmp_w1, nsa_cmp_w2, nsa_w_out,
           ssd_w_in, ssd_conv_w, ssd_conv_b, ssd_dt_bias, ssd_a_log, ssd_d, ssd_norm, ssd_w_out,
           moe_w_group, moe_b_group, moe_w_expert, moe_b_expert, moe_w_gate, moe_w_up, moe_w_down):
    h = x.reshape(TOKENS, D_MODEL)
    for i in range(DEPTH):
        j = i // N_MIXERS
        if i % N_MIXERS == 0:
            h = _nsa_mixer_add(h, ln_mix[i], nsa_w_in[j], nsa_cmp_pe[j], nsa_cmp_w1[j], nsa_cmp_w2[j],
                               nsa_w_out[j])
        else:
            h = _ssd_mixer_add(h, ln_mix[i], ssd_w_in[j], ssd_conv_w[j], ssd_conv_b[j], ssd_dt_bias[j],
                               ssd_a_log[j], ssd_d[j], ssd_norm[j], ssd_w_out[j])
        h = _hier_moe_add(h, ln_ffn[i], moe_w_group[i], moe_b_group[i], moe_w_expert[i], moe_b_expert[i],
                          moe_w_gate, moe_w_up, moe_w_down, i, ln_final, i == DEPTH - 1)
    return h.reshape(BATCH, SEQ, D_MODEL)
```

```python
import functools

import jax
import jax.numpy as jnp
from jax import lax
from jax.experimental import pallas as pl
from jax.experimental.pallas import tpu as pltpu

F32 = jnp.float32
BF16 = jnp.bfloat16
I32 = jnp.int32

D_MODEL = 2048
BATCH = 4
SEQ = 2048
TOKENS = BATCH * SEQ
DEPTH = 2
N_MIXERS = 2
NORM_EPS = 1e-6
NEG_INF = -1e30
ROPE_THETA = 10000.0

NSA_HEADS = 16
NSA_KV_HEADS = 4
NSA_HEAD_DIM = D_MODEL // NSA_HEADS
NSA_Q_PER_KV = NSA_HEADS // NSA_KV_HEADS
CMP_BLOCK = 32
CMP_STRIDE = 16
CMP_HIDDEN = 256
N_CMP = (SEQ - CMP_BLOCK) // CMP_STRIDE + 1
SLC_BLOCK = 64
SLC_TOPK = 16
SLC_LOCAL = 2
SLC_FORCE = 1e4
N_SLC = SEQ // SLC_BLOCK
WINDOW = 512
NSA_QD = NSA_HEADS * NSA_HEAD_DIM
NSA_KVD = NSA_KV_HEADS * NSA_HEAD_DIM
NSA_MAIN = NSA_QD + 6 * NSA_KVD
NSA_GATES = 3 * NSA_HEADS

SSD_D_INNER = 2 * D_MODEL
SSD_HEAD_DIM = 64
SSD_HEADS = SSD_D_INNER // SSD_HEAD_DIM
SSD_GROUPS = 8
SSD_HEADS_PER_GROUP = SSD_HEADS // SSD_GROUPS
SSD_D_STATE = 128
SSD_CONV = 4
SSD_CHUNK = 128
SSD_GROUP_W = SSD_D_INNER // SSD_GROUPS
SSD_BC = SSD_GROUPS * SSD_D_STATE
SSD_CONV_CH = SSD_D_INNER + 2 * SSD_BC
SSD_MAIN = SSD_D_INNER + SSD_CONV_CH
SSD_FAC_PAD = 16

MOE_GROUPS = 4
MOE_EPG = 8
MOE_EXPERTS = MOE_GROUPS * MOE_EPG
MOE_TOPK = 2
MOE_D_FF = 512

LANES = 128
VMEM_LIMIT = 56 * 1024 * 1024

NORM_TM = 256
MM_TM = 512
OUT_TM = 256
ATT_TQ = 256
ATT_TK = 128
ATT_CHUNK_TILES = 4
ATT_ONES = 16
MOE_TM = 256
MOE_TILES = (TOKENS * MOE_TOPK) // MOE_TM + MOE_EXPERTS
MOE_ROWS = MOE_TILES * MOE_TM
CMB_TM = 128
GATHER_UNROLL = 8
MOE_ISSUE_PARTS = 3


def _cparams(sem):
    return pltpu.CompilerParams(dimension_semantics=sem, vmem_limit_bytes=VMEM_LIMIT)


def _split3(x):
    hi = x.astype(BF16)
    r1 = x - hi.astype(F32)
    mid = r1.astype(BF16)
    lo = (r1 - mid.astype(F32)).astype(BF16)
    return hi, mid, lo


def _dot(a, b):
    return jnp.dot(a, b, preferred_element_type=F32)


def _dot_nt(a, b):
    return lax.dot_general(a, b, (((1,), (1,)), ((), ())), preferred_element_type=F32)


def _dot_tn(a, b):
    return lax.dot_general(a, b, (((0,), (0,)), ((), ())), preferred_element_type=F32)


def _dot_split_lhs(x, m_bf16):
    hi, mid, lo = _split3(x)
    return _dot(hi, m_bf16) + _dot(mid, m_bf16) + _dot(lo, m_bf16)


def _dot_split_rhs(m_bf16, x):
    hi, mid, lo = _split3(x)
    return _dot(m_bf16, hi) + _dot(m_bf16, mid) + _dot(m_bf16, lo)


def _dot_x3(a, w):
    a_hi = a.astype(BF16)
    a_lo = (a - a_hi.astype(F32)).astype(BF16)
    w_hi = w.astype(BF16)
    w_lo = (w - w_hi.astype(F32)).astype(BF16)
    return _dot(a_hi, w_hi) + _dot(a_hi, w_lo) + _dot(a_lo, w_hi)


def _dot_x3_nt(a, w):
    a_hi = a.astype(BF16)
    a_lo = (a - a_hi.astype(F32)).astype(BF16)
    w_hi = w.astype(BF16)
    w_lo = (w - w_hi.astype(F32)).astype(BF16)
    return _dot_nt(a_hi, w_hi) + _dot_nt(a_hi, w_lo) + _dot_nt(a_lo, w_hi)


def _rms(x, g):
    y = x * lax.rsqrt(jnp.mean(x * x, axis=-1, keepdims=True) + NORM_EPS)
    return y * g


def _norm_small_kernel(x_ref, g_ref, ws_ref, hn_ref, small_ref):
    y = _rms(x_ref[...], g_ref[...])
    hn_ref[...] = y.astype(BF16)
    small_ref[...] = _dot_x3(y, ws_ref[...])


def _norm_small_t_kernel(x_ref, g_ref, ws_ref, hn_ref, small_ref, small_t_ref):
    y = _rms(x_ref[...], g_ref[...])
    hn_ref[...] = y.astype(BF16)
    small = _dot_x3(y, ws_ref[...])
    small_ref[...] = small
    small_t_ref[...] = small.T


def _norm_small(x, g, w_small, transposed=False):
    n = w_small.shape[1]
    ws = jnp.zeros((D_MODEL, LANES), F32).at[:, :n].set(w_small)
    grid = (TOKENS // NORM_TM,)
    x_spec = pl.BlockSpec((NORM_TM, D_MODEL), lambda i: (i, 0))
    g_spec = pl.BlockSpec((1, D_MODEL), lambda i: (0, 0))
    w_spec = pl.BlockSpec((D_MODEL, LANES), lambda i: (0, 0))
    hn_spec = pl.BlockSpec((NORM_TM, D_MODEL), lambda i: (i, 0))
    sm_spec = pl.BlockSpec((NORM_TM, LANES), lambda i: (i, 0))
    hn_shape = jax.ShapeDtypeStruct((TOKENS, D_MODEL), BF16)
    sm_shape = jax.ShapeDtypeStruct((TOKENS, LANES), F32)
    if not transposed:
        return pl.pallas_call(
            _norm_small_kernel, name="norm_small", grid=grid,
            in_specs=[x_spec, g_spec, w_spec],
            out_specs=[hn_spec, sm_spec],
            out_shape=[hn_shape, sm_shape],
            compiler_params=_cparams(("parallel",)),
        )(x, g.reshape(1, D_MODEL), ws)
    smt_spec = pl.BlockSpec((LANES, NORM_TM), lambda i: (0, i))
    smt_shape = jax.ShapeDtypeStruct((LANES, TOKENS), F32)
    return pl.pallas_call(
        _norm_small_t_kernel, name="norm_small_t", grid=grid,
        in_specs=[x_spec, g_spec, w_spec],
        out_specs=[hn_spec, sm_spec, smt_spec],
        out_shape=[hn_shape, sm_shape, smt_shape],
        compiler_params=_cparams(("parallel",)),
    )(x, g.reshape(1, D_MODEL), ws)


def _router_kernel(x_ref, g_ref, ws_ref, b_ref, eid_ref, cw_ref):
    y = _rms(x_ref[...], g_ref[...])
    logits = _dot_x3(y, ws_ref[...]) + b_ref[...]
    lane = lax.broadcasted_iota(I32, logits.shape, 1)
    big = jnp.int32(LANES)
    neg = -jnp.inf
    gl = jnp.where(lane < MOE_GROUPS, logits, neg)
    gmax = jnp.max(gl, axis=-1, keepdims=True)
    gsum = jnp.sum(jnp.exp(gl - gmax), axis=-1, keepdims=True)
    g_w = 1.0 / gsum
    g_sel = jnp.min(jnp.where(gl == gmax, lane, big), axis=-1, keepdims=True)
    lo = MOE_GROUPS + g_sel * MOE_EPG
    el = jnp.where((lane >= lo) & (lane < lo + MOE_EPG), logits, neg)
    v1 = jnp.max(el, axis=-1, keepdims=True)
    i1 = jnp.min(jnp.where(el == v1, lane, big), axis=-1, keepdims=True)
    el2 = jnp.where(lane == i1, neg, el)
    v2 = jnp.max(el2, axis=-1, keepdims=True)
    i2 = jnp.min(jnp.where(el2 == v2, lane, big), axis=-1, keepdims=True)
    e2 = jnp.exp(v2 - v1)
    den = 1.0 + e2
    w1 = (1.0 / den) * g_w
    w2 = (e2 / den) * g_w
    eid_ref[...] = jnp.where(lane == 0, i1 - MOE_GROUPS, jnp.where(lane == 1, i2 - MOE_GROUPS, 0))
    cw_ref[...] = jnp.where(lane == 0, w1, jnp.where(lane == 1, w2, 0.0))


def _router(x, g, w_group, b_group, w_expert, b_expert):
    n = MOE_GROUPS + MOE_EXPERTS
    ws = jnp.zeros((D_MODEL, LANES), F32).at[:, :n].set(jnp.concatenate([w_group, w_expert], axis=1))
    bs = jnp.zeros((1, LANES), F32).at[0, :n].set(jnp.concatenate([b_group, b_expert]))
    return pl.pallas_call(
        _router_kernel, name="router", grid=(TOKENS // NORM_TM,),
        in_specs=[pl.BlockSpec((NORM_TM, D_MODEL), lambda i: (i, 0)),
                  pl.BlockSpec((1, D_MODEL), lambda i: (0, 0)),
                  pl.BlockSpec((D_MODEL, LANES), lambda i: (0, 0)),
                  pl.BlockSpec((1, LANES), lambda i: (0, 0))],
        out_specs=[pl.BlockSpec((NORM_TM, LANES), lambda i: (i, 0)),
                   pl.BlockSpec((NORM_TM, LANES), lambda i: (i, 0))],
        out_shape=[jax.ShapeDtypeStruct((TOKENS, LANES), I32),
                   jax.ShapeDtypeStruct((TOKENS, LANES), F32)],
        compiler_params=_cparams(("parallel",)),
    )(x, g.reshape(1, D_MODEL), ws, bs)


def _inproj_heads_kernel(a_ref, w_ref, cos_ref, sin_ref, o_ref, wbf_ref):
    j = pl.program_id(0)

    @pl.when(pl.program_id(1) == 0)
    def _():
        wbf_ref[...] = w_ref[...].astype(BF16)

    acc = _dot_nt(a_ref[...], wbf_ref[...])
    q_tiles = NSA_QD // acc.shape[1]
    c = cos_ref[...]
    s = sin_ref[...]

    def head(h):
        return acc[:, h * NSA_HEAD_DIM:(h + 1) * NSA_HEAD_DIM]

    def rotary(xh):
        return (xh * c + pltpu.roll(xh, NSA_HEAD_DIM // 2, 1) * s).astype(BF16)

    for h in range(NSA_KV_HEADS):
        o_ref[h] = rotary(head(h))

    @pl.when(j < q_tiles)
    def _():
        for h in range(NSA_KV_HEADS, 2 * NSA_KV_HEADS):
            o_ref[h] = rotary(head(h))

    @pl.when(j >= q_tiles)
    def _():
        for h in range(NSA_KV_HEADS, 2 * NSA_KV_HEADS):
            o_ref[h] = head(h).astype(BF16)


def _nsa_inproj(hn, w_in_t, cos_full, sin_signed):
    tn = 2 * NSA_KVD
    n_heads_tile = tn // NSA_HEAD_DIM
    s_tiles = SEQ // MM_TM
    return pl.pallas_call(
        _inproj_heads_kernel, name="nsa_inproj", grid=(NSA_MAIN // tn, TOKENS // MM_TM),
        in_specs=[pl.BlockSpec((MM_TM, D_MODEL), lambda j, i: (i, 0)),
                  pl.BlockSpec((tn, D_MODEL), lambda j, i: (j, 0)),
                  pl.BlockSpec((MM_TM, NSA_HEAD_DIM), lambda j, i: (i % s_tiles, 0)),
                  pl.BlockSpec((MM_TM, NSA_HEAD_DIM), lambda j, i: (i % s_tiles, 0))],
        out_specs=pl.BlockSpec((n_heads_tile, MM_TM, NSA_HEAD_DIM), lambda j, i: (j, i, 0)),
        out_shape=jax.ShapeDtypeStruct((NSA_MAIN // NSA_HEAD_DIM, TOKENS, NSA_HEAD_DIM), BF16),
        scratch_shapes=[pltpu.VMEM((tn, D_MODEL), BF16)],
        compiler_params=_cparams(("arbitrary", "arbitrary")),
    )(hn, w_in_t, cos_full, sin_signed)


def _matmul_tiles_kernel(a_ref, w_ref, o_ref, wbf_ref):
    @pl.when(pl.program_id(1) == 0)
    def _():
        wbf_ref[...] = w_ref[...].astype(BF16)

    acc = _dot_nt(a_ref[...], wbf_ref[...])
    for t in range(acc.shape[1] // LANES):
        o_ref[t] = acc[:, t * LANES:(t + 1) * LANES].astype(o_ref.dtype)


def _matmul_tiles(a, w_t, n_cols, tn):
    k = a.shape[1]
    return pl.pallas_call(
        _matmul_tiles_kernel, name="matmul_tiles", grid=(n_cols // tn, TOKENS // MM_TM),
        in_specs=[pl.BlockSpec((MM_TM, k), lambda j, i: (i, 0)),
                  pl.BlockSpec((tn, k), lambda j, i: (j, 0))],
        out_specs=pl.BlockSpec((tn // LANES, MM_TM, LANES), lambda j, i: (j, i, 0)),
        out_shape=jax.ShapeDtypeStruct((n_cols // LANES, TOKENS, LANES), BF16),
        scratch_shapes=[pltpu.VMEM((tn, k), BF16)],
        compiler_params=_cparams(("arbitrary", "arbitrary")),
    )(a, w_t)


def _outproj_kernel(a_ref, w_ref, r_ref, o_ref):
    if len(a_ref.shape) == 3:
        a = jnp.concatenate([a_ref[t] for t in range(a_ref.shape[0])], axis=1)
    else:
        a = a_ref[...]
    o_ref[...] = r_ref[...] + _dot(a, w_ref[...])


def _outproj_resid(a, w, resid):
    tiled = a.ndim == 3
    k, n = w.shape
    a_spec = (pl.BlockSpec((k // LANES, OUT_TM, LANES), lambda i: (0, i, 0)) if tiled
              else pl.BlockSpec((OUT_TM, k), lambda i: (i, 0)))
    return pl.pallas_call(
        _outproj_kernel, name="outproj_resid", grid=(TOKENS // OUT_TM,),
        in_specs=[a_spec,
                  pl.BlockSpec((k, n), lambda i: (0, 0)),
                  pl.BlockSpec((OUT_TM, n), lambda i: (i, 0))],
        out_specs=pl.BlockSpec((OUT_TM, n), lambda i: (i, 0)),
        out_shape=jax.ShapeDtypeStruct((TOKENS, n), F32),
        compiler_params=_cparams(("parallel",)),
    )(a, w.astype(BF16), resid)


def _compress_kernel(x_ref, pe_ref, w1_ref, w2_ref, o_ref):
    half = CMP_STRIDE * NSA_HEAD_DIM
    x = x_ref[0, 0, 0]
    w1 = w1_ref[0].astype(BF16)
    top = _dot(x, w1[:half])
    bot = _dot(x, w1[half:])
    pe = jnp.broadcast_to(pe_ref[0], (8, 2 * half)).astype(BF16)
    pe_bias = _dot(pe, w1)[0:1]
    hid = top + pltpu.roll(bot, bot.shape[0] - 1, 0) + pe_bias
    act = jax.nn.gelu(hid)
    o_ref[0, 0, 0] = _dot(act.astype(BF16), w2_ref[0].astype(BF16))


def _compress(kv_chunks, pe, w1, w2):
    n_chunk = SEQ // CMP_STRIDE
    feat = CMP_STRIDE * NSA_HEAD_DIM
    return pl.pallas_call(
        _compress_kernel, name="nsa_compress", grid=(2, NSA_KV_HEADS, BATCH),
        in_specs=[pl.BlockSpec((1, 1, 1, n_chunk, feat), lambda a, g, b: (a, g, b, 0, 0)),
                  pl.BlockSpec((1, 1, 2 * feat), lambda a, g, b: (a, 0, 0)),
                  pl.BlockSpec((1, 2 * feat, CMP_HIDDEN), lambda a, g, b: (a, 0, 0)),
                  pl.BlockSpec((1, CMP_HIDDEN, NSA_HEAD_DIM), lambda a, g, b: (a, 0, 0))],
        out_specs=pl.BlockSpec((1, 1, 1, n_chunk, NSA_HEAD_DIM), lambda a, g, b: (a, g, b, 0, 0)),
        out_shape=jax.ShapeDtypeStruct((2, NSA_KV_HEADS, BATCH, n_chunk, NSA_HEAD_DIM), F32),
        compiler_params=_cparams(("parallel", "parallel", "parallel")),
    )(kv_chunks, pe.reshape(2, 1, 2 * feat), w1, w2)


def _nsa_attn_kernel(q_ref, ks_ref, vs_ref, kw_ref, vw_ref, kc_ref, vc_ref, gate_ref, o_ref,
                     vst_scr, vwt_scr, sel_scr, acc_scr):
    qi = pl.program_id(2)
    tq = ATT_TQ
    tk = ATT_TK
    dh = NSA_HEAD_DIM
    r_heads = NSA_Q_PER_KV
    n_kt = SEQ // tk
    scale = dh ** -0.5

    @pl.when(qi == 0)
    def _():
        ones = jnp.ones((ATT_ONES, tk), BF16)
        for kt in range(n_kt):
            rows = slice(kt * tk, (kt + 1) * tk)
            vst_scr[kt, 0:dh, :] = vs_ref[0, rows, :].astype(F32).T.astype(BF16)
            vwt_scr[kt, 0:dh, :] = vw_ref[0, rows, :].astype(F32).T.astype(BF16)
            vst_scr[kt, dh:dh + ATT_ONES, :] = ones
            vwt_scr[kt, dh:dh + ATT_ONES, :] = ones

    q_t = jnp.concatenate([q_ref[r].astype(F32).T for r in range(r_heads)], axis=1).astype(BF16)
    n_cp = SEQ // CMP_STRIDE
    sub = lax.broadcasted_iota(I32, (n_cp, tq), 0)
    t_pos = qi * tq + lax.broadcasted_iota(I32, (n_cp, tq), 1)

    def tile4(a):
        return jnp.concatenate([a] * r_heads, axis=1)

    kc = kc_ref[0, 0, 0].astype(BF16)
    vc = vc_ref[0, 0, 0].astype(BF16)
    ok_c = jnp.where(sub * CMP_STRIDE + CMP_BLOCK - 1 <= t_pos, jnp.where(sub < N_CMP, 1.0, 0.0), 0.0)
    ok_c4 = tile4(ok_c)
    s_c = _dot(kc, q_t) * scale + (ok_c4 - 1.0) * (-NEG_INF)
    e_c = jnp.exp(s_c - jnp.max(s_c, axis=0, keepdims=True))
    p_c = (e_c / jnp.sum(e_c, axis=0, keepdims=True)) * ok_c4
    o_cmp = _dot_tn(vc, p_c.astype(BF16))
    p_sum = p_c[:, 0:tq]
    for r in range(1, r_heads):
        p_sum = p_sum + p_c[:, r * tq:(r + 1) * tq]

    blk_row = lax.broadcasted_iota(I32, (LANES, LANES), 0)
    cmp_col = lax.broadcasted_iota(I32, (LANES, LANES), 1)
    s_start = blk_row * SLC_BLOCK
    c_start = cmp_col * CMP_STRIDE
    ov_t = jnp.maximum(jnp.minimum(c_start + CMP_BLOCK, s_start + SLC_BLOCK)
                       - jnp.maximum(c_start, s_start), 0).astype(F32) / CMP_BLOCK
    ov_t = jnp.where(blk_row < N_SLC, ov_t, 0.0).astype(BF16)
    imp = _dot_split_rhs(ov_t, p_sum)[0:N_SLC]
    j_blk = lax.broadcasted_iota(I32, (N_SLC, tq), 0)
    dist = (qi * tq + lax.broadcasted_iota(I32, (N_SLC, tq), 1)) // SLC_BLOCK - j_blk
    imp = jnp.where(j_blk == 0, SLC_FORCE, jnp.where(dist < 0, imp, jnp.where(dist < SLC_LOCAL, SLC_FORCE, imp)))
    imp = jnp.where(dist >= 0, imp, -jnp.inf)
    cnt = jnp.zeros((N_SLC, tq), I32)
    for k in range(N_SLC):
        row_k = imp[k:k + 1, :]
        tie = jnp.where(j_blk > k, 1, 0)
        cnt = cnt + jnp.where(row_k > imp, 1, jnp.where(row_k == imp, tie, 0))
    sel = jnp.where(cnt < min(SLC_TOPK, N_SLC), 1.0, 0.0)
    for j in range(N_SLC):
        sel_scr[8 * j:8 * j + 8, :] = jnp.broadcast_to(sel[j:j + 1, :], (8, tq))

    def scores(k, ok):
        return _dot(k, q_t) * scale + tile4((ok - 1.0) * (-NEG_INF))

    def weighted_values(vt_scr, kt0, pr, n_tiles):
        out = None
        for u in range(n_tiles):
            term = _dot(vt_scr[kt0 + u], pr[u * tk:(u + 1) * tk].astype(BF16))
            out = term if out is None else out + term
        return out

    n_ct = ATT_CHUNK_TILES
    ck = n_ct * tk
    sub_c = lax.broadcasted_iota(I32, (ck, tq), 0)
    t_pos_c = qi * tq + lax.broadcasted_iota(I32, (ck, tq), 1)
    blocks_per_chunk = ck // SLC_BLOCK
    acc_scr[...] = jnp.zeros(acc_scr.shape, F32)

    def slc_body(c, m_old):
        start = pl.multiple_of(c * ck, ck)
        k = ks_ref[0, pl.ds(start, ck), :]
        rows8 = sel_scr[pl.ds(pl.multiple_of(c * (8 * blocks_per_chunk), 8 * blocks_per_chunk),
                              8 * blocks_per_chunk), :]
        picked = jnp.concatenate(
            [rows8[8 * u:8 * u + 8] for u in range(blocks_per_chunk) for _ in range(SLC_BLOCK // 8)], axis=0)
        sc = scores(k, jnp.where(start + sub_c <= t_pos_c, picked, 0.0))
        m_new = jnp.maximum(m_old, jnp.max(sc, axis=0, keepdims=True))
        alpha = jnp.exp(m_old - m_new)
        pr = jnp.exp(sc - m_new)
        acc_scr[...] = alpha * acc_scr[...] + weighted_values(vst_scr, c * n_ct, pr, n_ct)
        return m_new

    lax.fori_loop(0, ((qi + 1) * tq + ck - 1) // ck, slc_body, jnp.full((1, r_heads * tq), NEG_INF, F32))
    acc = acc_scr[...]
    o_slc = acc[0:dh] / acc[dh:dh + 1]

    n_wt = (WINDOW + tq) // tk
    kt0 = jnp.maximum(qi * (tq // tk) - WINDOW // tk, 0)
    w_start = pl.multiple_of(kt0 * tk, tk)
    key_w = w_start + lax.broadcasted_iota(I32, (n_wt * tk, tq), 0)
    t_pos_w = qi * tq + lax.broadcasted_iota(I32, (n_wt * tk, tq), 1)
    ok_w = jnp.where(key_w <= t_pos_w, jnp.where(key_w > t_pos_w - WINDOW, 1.0, 0.0), 0.0)
    sc_w = scores(kw_ref[0, pl.ds(w_start, n_wt * tk), :], ok_w)
    pr_w = jnp.exp(sc_w - jnp.max(sc_w, axis=0, keepdims=True))
    acc_w = weighted_values(vwt_scr, kt0, pr_w, n_wt)
    o_win = acc_w[0:dh] / acc_w[dh:dh + 1]

    gate = jax.nn.sigmoid(gate_ref[0])
    for r in range(r_heads):
        cols = slice(r * tq, (r + 1) * tq)
        o = (gate[3 * r:3 * r + 1] * o_cmp[:, cols] + gate[3 * r + 1:3 * r + 2] * o_slc[:, cols]
             + gate[3 * r + 2:3 * r + 3] * o_win[:, cols])
        o_ref[:, r * dh:(r + 1) * dh] = o.T.astype(BF16)


def _nsa_attention(heads, kc_vc, gates):
    tq = ATT_TQ
    nq = SEQ // tq
    r = NSA_Q_PER_KV
    g_heads = NSA_KV_HEADS
    q_spec = pl.BlockSpec((r, tq, NSA_HEAD_DIM), lambda b, g, i: (g, b * nq + i, 0))

    def kv_spec(first_head):
        return pl.BlockSpec((1, SEQ, NSA_HEAD_DIM), lambda b, g, i: (first_head + g, b, 0))

    first = NSA_HEADS
    specs = [q_spec,
             kv_spec(first + 2 * g_heads), kv_spec(first + 3 * g_heads),
             kv_spec(first + 4 * g_heads), kv_spec(first + 5 * g_heads),
             pl.BlockSpec((1, 1, 1, SEQ // CMP_STRIDE, NSA_HEAD_DIM), lambda b, g, i: (0, g, b, 0, 0)),
             pl.BlockSpec((1, 1, 1, SEQ // CMP_STRIDE, NSA_HEAD_DIM), lambda b, g, i: (1, g, b, 0, 0)),
             pl.BlockSpec((1, 3 * r, tq), lambda b, g, i: (g, 0, b * nq + i))]
    vt_shape = (SEQ // ATT_TK, NSA_HEAD_DIM + ATT_ONES, ATT_TK)
    return pl.pallas_call(
        _nsa_attn_kernel, name="nsa_attn", grid=(BATCH, g_heads, nq),
        in_specs=specs,
        out_specs=pl.BlockSpec((tq, r * NSA_HEAD_DIM), lambda b, g, i: (b * nq + i, g)),
        out_shape=jax.ShapeDtypeStruct((TOKENS, NSA_QD), BF16),
        scratch_shapes=[pltpu.VMEM(vt_shape, BF16), pltpu.VMEM(vt_shape, BF16),
                        pltpu.VMEM((8 * N_SLC, tq), F32),
                        pltpu.VMEM((NSA_HEAD_DIM + ATT_ONES, r * tq), F32)],
        compiler_params=_cparams(("arbitrary", "arbitrary", "arbitrary")),
    )(heads, heads, heads, heads, heads, kc_vc, kc_vc, gates)


def _ssd_chunk_kernel(zx_ref, dtc_ref, dtr_ref, cw_ref, cb_ref, dtb_c_ref, alog_c_ref, dtb_r_ref, alog_r_ref,
                      dskip_ref, ng_ref, shift_ref, echan_ref, o_ref,
                      prev_scr, acum_r_scr, st_scr, fac_scr):
    chunk = pl.program_id(1)
    L = SSD_CHUNK
    W = SSD_GROUP_W
    hpg = SSD_HEADS_PER_GROUP
    n_xt = W // LANES
    x0 = SSD_D_INNER // LANES
    b0 = 2 * SSD_D_INNER // LANES
    c0 = b0 + SSD_GROUPS
    cb0 = SSD_D_INNER // LANES
    cc0 = cb0 + SSD_GROUPS

    @pl.when(chunk == 0)
    def _():
        prev_scr[...] = jnp.zeros(prev_scr.shape, BF16)
        st_scr[...] = jnp.zeros(st_scr.shape, F32)

    dt_c = jax.nn.softplus(dtc_ref[...] + dtb_c_ref[...])
    adt_c = dt_c * (-jnp.exp(alog_c_ref[...]))
    dt_r = jax.nn.softplus(dtr_ref[...] + dtb_r_ref[...])
    adt_r = dt_r * (-jnp.exp(alog_r_ref[...]))
    row = lax.broadcasted_iota(I32, (L, L), 0)
    col = lax.broadcasted_iota(I32, (L, L), 1)
    causal = row >= col
    tri = jnp.where(causal, 1.0, 0.0).astype(BF16)
    tri_t = jnp.where(col >= row, 1.0, 0.0).astype(BF16)
    acum_c = _dot_split_rhs(tri, adt_c)
    acum_r_scr[...] = _dot_split_lhs(adt_r, tri_t)
    a_last = acum_c[L - 1:L, :]
    fac = jnp.concatenate([dt_c, jnp.exp(acum_c), jnp.exp(a_last - acum_c),
                           jnp.broadcast_to(jnp.exp(a_last), (SSD_FAC_PAD, LANES))], axis=0)
    fac_hi = fac.astype(BF16)
    fac_scr[0] = fac_hi
    fac_scr[1] = (fac - fac_hi.astype(F32)).astype(BF16)
    lane_w = lax.broadcasted_iota(I32, (L, LANES), 1)
    first_half = lane_w < SSD_HEAD_DIM

    def tiles(ref, first, n):
        return jnp.concatenate([ref[first + q] for q in range(n)], axis=1)

    def group_body(g, carry):
        e_chan = echan_ref[g]
        ex = _dot(fac_scr[0], e_chan) + _dot(fac_scr[1], e_chan)
        dt_x, ea_x, sd_x, cd_x = ex[0:L], ex[L:2 * L], ex[2 * L:3 * L], ex[3 * L:3 * L + 1]

        def conv_silu(zx_first, conv_first, n):
            cur = tiles(zx_ref, zx_first, n)
            ext = jnp.concatenate([tiles(prev_scr, conv_first, n), cur], axis=0)
            w = tiles(cw_ref, conv_first, n)
            acc = jnp.broadcast_to(tiles(cb_ref, conv_first, n), (L, n * LANES))
            for k in range(SSD_CONV):
                back = SSD_CONV - 1 - k
                xk = cur.astype(F32) if back == 0 else _dot(shift_ref[back - 1], ext)
                acc = acc + xk * w[k:k + 1, :]
            for q in range(n):
                prev_scr[conv_first + q] = zx_ref[zx_first + q]
            return jax.nn.silu(acc)

        xs = conv_silu(x0 + n_xt * g, n_xt * g, n_xt)
        bm = conv_silu(b0 + g, cb0 + g, 1)
        cm = conv_silu(c0 + g, cc0 + g, 1)

        xdt = xs * dt_x
        cb = jnp.where(causal, _dot_nt(cm.astype(BF16), bm.astype(BF16)), 0.0)
        y_parts = []
        for pair in range(hpg // 2):
            xd = xdt[:, pair * LANES:(pair + 1) * LANES]
            m_pair = []
            for sub in range(2):
                a_row = jnp.broadcast_to(acum_r_scr[pl.ds(g * hpg + 2 * pair + sub, 1), :], (L, L))
                seg = jnp.minimum(a_row.T - a_row, 0.0)
                m_pair.append((cb * jnp.exp(seg)).astype(BF16))
            x_pair = jnp.concatenate([jnp.where(first_half, xd, 0.0), jnp.where(first_half, 0.0, xd)], axis=0)
            y_parts.append(_dot(jnp.concatenate(m_pair, axis=1), x_pair.astype(BF16)))
        y_diag = jnp.concatenate(y_parts, axis=1)

        st = st_scr[g]
        y_off = _dot(cm.astype(BF16), st.astype(BF16)) * ea_x
        st_scr[g] = st * cd_x + _dot_tn(bm.astype(BF16), (xdt * sd_x).astype(BF16))

        y = y_diag + y_off + xs * tiles(dskip_ref, n_xt * g, n_xt)
        y = y * jax.nn.silu(tiles(zx_ref, n_xt * g, n_xt).astype(F32))
        y = y * lax.rsqrt(jnp.mean(y * y, axis=-1, keepdims=True) + NORM_EPS)
        y = y * tiles(ng_ref, n_xt * g, n_xt)
        for q in range(n_xt):
            o_ref[n_xt * g + q] = y[:, q * LANES:(q + 1) * LANES].astype(BF16)
        return carry

    lax.fori_loop(0, SSD_GROUPS, group_body, 0)


def _ssd_chunks(zx_tiles, dt_small, dt_small_t, conv_w, conv_b, dt_bias, a_log, d_skip, norm_g):
    L = SSD_CHUNK
    nc = SEQ // L
    n_zx = SSD_MAIN // LANES
    n_conv = SSD_CONV_CH // LANES
    n_inner = SSD_D_INNER // LANES
    hpg = SSD_HEADS_PER_GROUP

    def pad_heads(v):
        return jnp.zeros((LANES,), F32).at[:SSD_HEADS].set(v)

    dtb = pad_heads(dt_bias)
    alog = pad_heads(a_log)
    cw = conv_w.reshape(SSD_CONV, n_conv, LANES).transpose(1, 0, 2)
    cb = conv_b.reshape(n_conv, 1, LANES)
    d_chan = jnp.repeat(d_skip, SSD_HEAD_DIM).reshape(n_inner, 1, LANES)
    ng = norm_g.reshape(n_inner, 1, LANES)
    t_idx = jnp.arange(L, dtype=I32)[None, :, None]
    r_idx = jnp.arange(2 * L, dtype=I32)[None, None, :]
    back = jnp.arange(1, SSD_CONV, dtype=I32)[:, None, None]
    shift = (r_idx == L + t_idx - back).astype(BF16)
    head = jnp.arange(LANES, dtype=I32)[None, :, None]
    grp = jnp.arange(SSD_GROUPS, dtype=I32)[:, None, None]
    e_chan = (head == grp * hpg + jnp.arange(SSD_GROUP_W, dtype=I32)[None, None, :] // SSD_HEAD_DIM).astype(BF16)
    row = lambda b, c: b * nc + c
    const3 = lambda b, c: (0, 0, 0)
    const2 = lambda b, c: (0, 0)
    in_specs = [
        pl.BlockSpec((n_zx, L, LANES), lambda b, c: (0, row(b, c), 0)),
        pl.BlockSpec((L, LANES), lambda b, c: (row(b, c), 0)),
        pl.BlockSpec((LANES, L), lambda b, c: (0, row(b, c))),
        pl.BlockSpec((n_conv, SSD_CONV, LANES), const3),
        pl.BlockSpec((n_conv, 1, LANES), const3),
        pl.BlockSpec((1, LANES), const2), pl.BlockSpec((1, LANES), const2),
        pl.BlockSpec((LANES, 1), const2), pl.BlockSpec((LANES, 1), const2),
        pl.BlockSpec((n_inner, 1, LANES), const3),
        pl.BlockSpec((n_inner, 1, LANES), const3),
        pl.BlockSpec((SSD_CONV - 1, L, 2 * L), const3),
        pl.BlockSpec((SSD_GROUPS, LANES, SSD_GROUP_W), const3),
    ]
    return pl.pallas_call(
        _ssd_chunk_kernel, name="ssd_chunks", grid=(BATCH, nc),
        in_specs=in_specs,
        out_specs=pl.BlockSpec((n_inner, L, LANES), lambda b, c: (0, row(b, c), 0)),
        out_shape=jax.ShapeDtypeStruct((n_inner, TOKENS, LANES), BF16),
        scratch_shapes=[pltpu.VMEM((n_conv, L, LANES), BF16),
                        pltpu.VMEM((LANES, L), F32),
                        pltpu.VMEM((SSD_GROUPS, SSD_D_STATE, SSD_GROUP_W), F32),
                        pltpu.VMEM((2, 3 * L + SSD_FAC_PAD, LANES), BF16)],
        compiler_params=_cparams(("arbitrary", "arbitrary")),
    )(zx_tiles, dt_small, dt_small_t, cw, cb, dtb.reshape(1, LANES), alog.reshape(1, LANES),
      dtb.reshape(LANES, 1), alog.reshape(LANES, 1), d_chan, ng, shift, e_chan)


def _gather_rows(src_hbm, idx_ref, base, dst, sem, n_rows):
    def body(r, carry):
        tok = idx_ref[base + r]
        pltpu.make_async_copy(src_hbm.at[pl.ds(tok, 1), :], dst.at[pl.ds(r, 1), :], sem).start()
        return carry

    lax.fori_loop(0, n_rows, body, 0, unroll=GATHER_UNROLL)


def _start_rows(src_hbm, idx_ref, base, dst, sem, lo, hi):
    for r in range(lo, hi):
        tok = idx_ref[base + r]
        pltpu.make_async_copy(src_hbm.at[pl.ds(tok, 1), :], dst.at[pl.ds(r, 1), :], sem).start()


def _moe_ffn_kernel(te_ref, pos_ref, nact_ref, pad_ref, wp_ref, x_hbm, g_ref, wg_hbm, wu_hbm, wd_hbm, y_ref,
                    tok_ref, buf, sem, wg_f, wu_f, wd_f, wsem, wg_bf, wu_bf, wd_bf, *, layer):
    i = pl.program_id(0)
    n_act = nact_ref[0]
    tm = MOE_TM
    slot = i % 2
    run_start = wp_ref[i] == 1
    w_slot = wp_ref[MOE_TILES + i]
    next_expert = wp_ref[2 * MOE_TILES + i]

    def weight_copies(e, s):
        return (pltpu.make_async_copy(wg_hbm.at[layer, e], wg_f.at[s], wsem.at[s, 0]),
                pltpu.make_async_copy(wu_hbm.at[layer, e], wu_f.at[s], wsem.at[s, 1]),
                pltpu.make_async_copy(wd_hbm.at[layer, e], wd_f.at[s], wsem.at[s, 2]))

    @pl.when(i == 0)
    def _():
        for c in weight_copies(te_ref[0], 0):
            c.start()
        def clear(q, carry):
            tok_ref[q] = 0
            return carry

        def clear_padding(e, carry):
            lax.fori_loop(pad_ref[e], pad_ref[MOE_EXPERTS + e], clear, 0)
            return carry

        lax.fori_loop(0, MOE_EXPERTS, clear_padding, 0)
        for k in range(MOE_TOPK):
            def place(t, carry, k=k):
                tok_ref[pos_ref[k * TOKENS + t]] = t
                return carry

            lax.fori_loop(0, TOKENS, place, 0, unroll=GATHER_UNROLL)
        _gather_rows(x_hbm, tok_ref, 0, buf.at[0], sem.at[0], tm)

    @pl.when(run_start)
    def _():
        for c in weight_copies(te_ref[i], w_slot):
            c.wait()

        @pl.when(next_expert >= 0)
        def _():
            for c in weight_copies(next_expert, 1 - w_slot):
                c.start()

        wg_bf[...] = wg_f[w_slot].astype(BF16)
        wu_bf[...] = wu_f[w_slot].astype(BF16)
        wd_bf[...] = wd_f[w_slot].astype(BF16)

    @pl.when(i >= n_act)
    def _():
        y_ref[...] = jnp.zeros(y_ref.shape, F32)

    def tile_body(prefetch_next):
        def start_next(part):
            if prefetch_next:
                lo, hi = (part * tm) // MOE_ISSUE_PARTS, ((part + 1) * tm) // MOE_ISSUE_PARTS
                _start_rows(x_hbm, tok_ref, (i + 1) * tm, buf.at[1 - slot], sem.at[1 - slot], lo, hi)

        pltpu.make_async_copy(x_hbm.at[pl.ds(0, tm), :], buf.at[slot], sem.at[slot]).wait()
        h = _rms(buf[slot], g_ref[...]).astype(BF16)
        start_next(0)
        gate = _dot(h, wg_bf[...])
        start_next(1)
        act = jax.nn.silu(gate) * _dot(h, wu_bf[...])
        start_next(2)
        y_ref[...] = _dot(act.astype(BF16), wd_bf[...])

    @pl.when(i + 1 < n_act)
    def _():
        tile_body(True)

    @pl.when(i + 1 == n_act)
    def _():
        tile_body(False)


def _moe_ffn(x, g, w_gate, w_up, w_down, layer, tile_expert, pos_kmajor, n_active, pad_rows, weight_plan):
    any_spec = pl.BlockSpec(memory_space=pl.ANY)
    grid_spec = pltpu.PrefetchScalarGridSpec(
        num_scalar_prefetch=5, grid=(MOE_TILES,),
        in_specs=[any_spec, pl.BlockSpec((1, D_MODEL), lambda i, *_: (0, 0)), any_spec, any_spec, any_spec],
        out_specs=pl.BlockSpec((MOE_TM, D_MODEL), lambda i, *_: (i, 0)),
        scratch_shapes=[pltpu.SMEM((MOE_ROWS,), I32),
                        pltpu.VMEM((2, MOE_TM, D_MODEL), F32), pltpu.SemaphoreType.DMA((2,)),
                        pltpu.VMEM((2, D_MODEL, MOE_D_FF), F32), pltpu.VMEM((2, D_MODEL, MOE_D_FF), F32),
                        pltpu.VMEM((2, MOE_D_FF, D_MODEL), F32), pltpu.SemaphoreType.DMA((2, 3)),
                        pltpu.VMEM((D_MODEL, MOE_D_FF), BF16), pltpu.VMEM((D_MODEL, MOE_D_FF), BF16),
                        pltpu.VMEM((MOE_D_FF, D_MODEL), BF16)])
    return pl.pallas_call(
        functools.partial(_moe_ffn_kernel, layer=layer), name="moe_ffn", grid_spec=grid_spec,
        out_shape=jax.ShapeDtypeStruct((MOE_ROWS, D_MODEL), F32),
        compiler_params=_cparams(("arbitrary",)),
    )(tile_expert, pos_kmajor, n_active, pad_rows, weight_plan, x, g.reshape(1, D_MODEL), w_gate, w_up, w_down)


def _moe_combine_kernel(pos_ref, x_ref, cw_ref, g_ref, y_hbm, o_ref, buf, sem, *, final_norm):
    i = pl.program_id(0)
    n = pl.num_programs(0)
    tm = CMB_TM
    slot = i % 2

    def issue(tile, s):
        for k in range(MOE_TOPK):
            _gather_rows(y_hbm, pos_ref, (k * (TOKENS // tm) + tile) * tm, buf.at[s, k], sem.at[s], tm)

    @pl.when(i == 0)
    def _():
        issue(0, 0)

    @pl.when(i + 1 < n)
    def _():
        issue(i + 1, 1 - slot)

    for k in range(MOE_TOPK):
        pltpu.make_async_copy(y_hbm.at[pl.ds(0, tm), :], buf.at[slot, k], sem.at[slot]).wait()
    cw = cw_ref[...]
    out = x_ref[...] + cw[:, 0:1] * buf[slot, 0] + cw[:, 1:2] * buf[slot, 1]
    if final_norm:
        out = _rms(out, g_ref[...])
    o_ref[...] = out


def _moe_combine(x, cw, y_sorted, pos_kmajor, g_final, final_norm):
    grid_spec = pltpu.PrefetchScalarGridSpec(
        num_scalar_prefetch=1, grid=(TOKENS // CMB_TM,),
        in_specs=[pl.BlockSpec((CMB_TM, D_MODEL), lambda i, pos: (i, 0)),
                  pl.BlockSpec((CMB_TM, LANES), lambda i, pos: (i, 0)),
                  pl.BlockSpec((1, D_MODEL), lambda i, pos: (0, 0)),
                  pl.BlockSpec(memory_space=pl.ANY)],
        out_specs=pl.BlockSpec((CMB_TM, D_MODEL), lambda i, pos: (i, 0)),
        scratch_shapes=[pltpu.VMEM((2, MOE_TOPK, CMB_TM, D_MODEL), F32), pltpu.SemaphoreType.DMA((2,))])
    return pl.pallas_call(
        functools.partial(_moe_combine_kernel, final_norm=final_norm), name="moe_combine", grid_spec=grid_spec,
        out_shape=jax.ShapeDtypeStruct((TOKENS, D_MODEL), F32),
        compiler_params=_cparams(("arbitrary",)),
    )(pos_kmajor, x, cw, g_final.reshape(1, D_MODEL), y_sorted)


def _moe_plan(eid):
    e = eid[:, :MOE_TOPK].reshape(-1)
    onehot = (e[:, None] == jnp.arange(MOE_EXPERTS, dtype=I32)[None, :]).astype(I32)
    csum = jnp.cumsum(onehot, axis=0)
    counts = csum[-1]
    rank = jnp.take_along_axis(csum, e[:, None], axis=1)[:, 0] - 1
    padded = ((counts + MOE_TM - 1) // MOE_TM) * MOE_TM
    g_end = jnp.cumsum(padded)
    g_start = g_end - padded
    pos = g_start[e] + rank
    n_active = (g_end[-1] // MOE_TM).astype(I32)
    tile_start = jnp.arange(MOE_TILES, dtype=I32) * MOE_TM
    te = jnp.sum((g_end[None, :] <= tile_start[:, None]).astype(I32), axis=1)
    last = jnp.max(jnp.where(counts > 0, jnp.arange(MOE_EXPERTS, dtype=I32), 0))
    tile_expert = jnp.minimum(te, last)
    pos_kmajor = pos.reshape(TOKENS, MOE_TOPK).T.reshape(-1)
    pad_rows = jnp.concatenate([g_start + counts, g_end])
    experts = jnp.arange(MOE_EXPERTS, dtype=I32)
    run_start = jnp.concatenate([jnp.ones((1,), I32), (tile_expert[1:] != tile_expert[:-1]).astype(I32)])
    w_slot = (jnp.cumsum(run_start) - 1) % 2
    later = (experts[None, :] > experts[:, None]) & (counts > 0)[None, :]
    next_nonempty = jnp.min(jnp.where(later, experts[None, :], MOE_EXPERTS), axis=1)
    next_nonempty = jnp.where(next_nonempty == MOE_EXPERTS, -1, next_nonempty)
    weight_plan = jnp.concatenate([run_start, w_slot, next_nonempty[tile_expert]]).astype(I32)
    return tile_expert, n_active.reshape(1), pos_kmajor, pad_rows, weight_plan


def _hier_moe_add(x, ln_g, w_group, b_group, w_expert, b_expert, w_gate, w_up, w_down, layer,
                  g_final, final_norm):
    eid, cw = _router(x, ln_g, w_group, b_group, w_expert, b_expert)
    tile_expert, n_active, pos_kmajor, pad_rows, weight_plan = _moe_plan(eid)
    y_sorted = _moe_ffn(x, ln_g, w_gate, w_up, w_down, layer, tile_expert, pos_kmajor, n_active, pad_rows,
                        weight_plan)
    return _moe_combine(x, cw, y_sorted, pos_kmajor, g_final, final_norm)


def _rope_tables():
    pos = jnp.arange(SEQ, dtype=F32)
    inv = 1.0 / (ROPE_THETA ** (jnp.arange(0, NSA_HEAD_DIM, 2, dtype=F32) / NSA_HEAD_DIM))
    ang = pos[:, None] * inv[None, :]
    cos, sin = jnp.cos(ang), jnp.sin(ang)
    return jnp.concatenate([cos, cos], axis=1), jnp.concatenate([-sin, sin], axis=1)


def _nsa_mixer_add(x, ln_g, w_in, cmp_pe, cmp_w1, cmp_w2, w_out):
    w_in_t = w_in.T
    hn, _, g_lin_t = _norm_small(x, ln_g, w_in_t[NSA_MAIN:].T, transposed=True)
    cos_full, sin_signed = _rope_tables()
    heads = _nsa_inproj(hn, w_in_t, cos_full, sin_signed)
    first_c = NSA_HEADS
    kv_c = heads[first_c:first_c + 2 * NSA_KV_HEADS]
    kv_chunks = kv_c.reshape(2, NSA_KV_HEADS, BATCH, SEQ // CMP_STRIDE, CMP_STRIDE * NSA_HEAD_DIM)
    kc_vc = _compress(kv_chunks, cmp_pe, cmp_w1, cmp_w2)
    gates_t = g_lin_t[:NSA_GATES].reshape(NSA_KV_HEADS, 3 * NSA_Q_PER_KV, TOKENS)
    o = _nsa_attention(heads, kc_vc, gates_t)
    return _outproj_resid(o, w_out, x)


def _ssd_mixer_add(x, ln_g, w_in, conv_w, conv_b, dt_bias, a_log, d_skip, norm_g, w_out):
    w_in_t = w_in.T
    hn, dt_small, dt_small_t = _norm_small(x, ln_g, w_in_t[SSD_MAIN:].T, transposed=True)
    zx_tiles = _matmul_tiles(hn, w_in_t, SSD_MAIN, 1024)
    y_tiles = _ssd_chunks(zx_tiles, dt_small, dt_small_t, conv_w, conv_b, dt_bias, a_log, d_skip, norm_g)
    return _outproj_resid(y_tiles, w_out, x)


def kernel(x, ln_mix, ln_ffn, ln_final, nsa_w_in, nsa_cmp_pe, nsa_cmp_w1, nsa_cmp_w2, nsa_w_out,
           ssd_w_in, ssd_conv_w, ssd_conv_b, ssd_dt_bias, ssd_a_log, ssd_d, ssd_norm, ssd_w_out,
           moe_w_group, moe_b_group, moe_w_expert, moe_b_expert, moe_w_gate, moe_w_up, moe_w_down):
    h = x.reshape(TOKENS, D_MODEL)
    for i in range(DEPTH):
        j = i // N_MIXERS
        if i % N_MIXERS == 0:
            h = _nsa_mixer_add(h, ln_mix[i], nsa_w_in[j], nsa_cmp_pe[j], nsa_cmp_w1[j], nsa_cmp_w2[j],
                               nsa_w_out[j])
        else:
            h = _ssd_mixer_add(h, ln_mix[i], ssd_w_in[j], ssd_conv_w[j], ssd_conv_b[j], ssd_dt_bias[j],
                               ssd_a_log[j], ssd_d[j], ssd_norm[j], ssd_w_out[j])
        h = _hier_moe_add(h, ln_ffn[i], moe_w_group[i], moe_b_group[i], moe_w_expert[i], moe_b_expert[i],
                          moe_w_gate, moe_w_up, moe_w_down, i, ln_final, i == DEPTH - 1)
    return h.reshape(BATCH, SEQ, D_MODEL)
```

```python
import functools

import jax
import jax.numpy as jnp
from jax import lax
from jax.experimental import pallas as pl
from jax.experimental.pallas import tpu as pltpu

F32 = jnp.float32
BF16 = jnp.bfloat16
I32 = jnp.int32

D_MODEL = 2048
BATCH = 4
SEQ = 2048
TOKENS = BATCH * SEQ
DEPTH = 2
N_MIXERS = 2
NORM_EPS = 1e-6
NEG_INF = -1e30
ROPE_THETA = 10000.0

NSA_HEADS = 16
NSA_KV_HEADS = 4
NSA_HEAD_DIM = D_MODEL // NSA_HEADS
NSA_Q_PER_KV = NSA_HEADS // NSA_KV_HEADS
CMP_BLOCK = 32
CMP_STRIDE = 16
CMP_HIDDEN = 256
N_CMP = (SEQ - CMP_BLOCK) // CMP_STRIDE + 1
SLC_BLOCK = 64
SLC_TOPK = 16
SLC_LOCAL = 2
SLC_FORCE = 1e4
N_SLC = SEQ // SLC_BLOCK
WINDOW = 512
NSA_QD = NSA_HEADS * NSA_HEAD_DIM
NSA_KVD = NSA_KV_HEADS * NSA_HEAD_DIM
NSA_MAIN = NSA_QD + 6 * NSA_KVD
NSA_GATES = 3 * NSA_HEADS

SSD_D_INNER = 2 * D_MODEL
SSD_HEAD_DIM = 64
SSD_HEADS = SSD_D_INNER // SSD_HEAD_DIM
SSD_GROUPS = 8
SSD_HEADS_PER_GROUP = SSD_HEADS // SSD_GROUPS
SSD_D_STATE = 128
SSD_CONV = 4
SSD_CHUNK = 128
SSD_GROUP_W = SSD_D_INNER // SSD_GROUPS
SSD_BC = SSD_GROUPS * SSD_D_STATE
SSD_CONV_CH = SSD_D_INNER + 2 * SSD_BC
SSD_MAIN = SSD_D_INNER + SSD_CONV_CH
SSD_FAC_PAD = 16

MOE_GROUPS = 4
MOE_EPG = 8
MOE_EXPERTS = MOE_GROUPS * MOE_EPG
MOE_TOPK = 2
MOE_D_FF = 512

LANES = 128
VMEM_LIMIT = 56 * 1024 * 1024

NORM_TM = 256
MM_TM = 512
OUT_TM = 256
ATT_TQ = 256
ATT_TK = 256
ATT_CHUNK_TILES = 2
ATT_ONES = 16
MOE_TM = 256
MOE_TILES = (TOKENS * MOE_TOPK) // MOE_TM + MOE_EXPERTS
MOE_ROWS = MOE_TILES * MOE_TM
CMB_TM = 128
GATHER_UNROLL = 8


def _cparams(sem):
    return pltpu.CompilerParams(dimension_semantics=sem, vmem_limit_bytes=VMEM_LIMIT)


def _split3(x):
    hi = x.astype(BF16)
    r1 = x - hi.astype(F32)
    mid = r1.astype(BF16)
    lo = (r1 - mid.astype(F32)).astype(BF16)
    return hi, mid, lo


def _dot(a, b):
    return jnp.dot(a, b, preferred_element_type=F32)


def _dot_nt(a, b):
    return lax.dot_general(a, b, (((1,), (1,)), ((), ())), preferred_element_type=F32)


def _dot_tn(a, b):
    return lax.dot_general(a, b, (((0,), (0,)), ((), ())), preferred_element_type=F32)


def _dot_split_lhs(x, m_bf16):
    hi, mid, lo = _split3(x)
    return _dot(hi, m_bf16) + _dot(mid, m_bf16) + _dot(lo, m_bf16)


def _dot_split_rhs(m_bf16, x):
    hi, mid, lo = _split3(x)
    return _dot(m_bf16, hi) + _dot(m_bf16, mid) + _dot(m_bf16, lo)


def _dot_x3(a, w):
    a_hi = a.astype(BF16)
    a_lo = (a - a_hi.astype(F32)).astype(BF16)
    w_hi = w.astype(BF16)
    w_lo = (w - w_hi.astype(F32)).astype(BF16)
    return _dot(a_hi, w_hi) + _dot(a_hi, w_lo) + _dot(a_lo, w_hi)


def _dot_x3_nt(a, w):
    a_hi = a.astype(BF16)
    a_lo = (a - a_hi.astype(F32)).astype(BF16)
    w_hi = w.astype(BF16)
    w_lo = (w - w_hi.astype(F32)).astype(BF16)
    return _dot_nt(a_hi, w_hi) + _dot_nt(a_hi, w_lo) + _dot_nt(a_lo, w_hi)


def _rms(x, g):
    y = x * lax.rsqrt(jnp.mean(x * x, axis=-1, keepdims=True) + NORM_EPS)
    return y * g


def _norm_small_kernel(x_ref, g_ref, ws_ref, hn_ref, small_ref):
    y = _rms(x_ref[...], g_ref[...])
    hn_ref[...] = y.astype(BF16)
    small_ref[...] = _dot_x3(y, ws_ref[...])


def _norm_small_t_kernel(x_ref, g_ref, ws_ref, hn_ref, small_ref, small_t_ref):
    y = _rms(x_ref[...], g_ref[...])
    hn_ref[...] = y.astype(BF16)
    small = _dot_x3(y, ws_ref[...])
    small_ref[...] = small
    small_t_ref[...] = small.T


def _norm_small(x, g, w_small, transposed=False):
    n = w_small.shape[1]
    ws = jnp.zeros((D_MODEL, LANES), F32).at[:, :n].set(w_small)
    grid = (TOKENS // NORM_TM,)
    x_spec = pl.BlockSpec((NORM_TM, D_MODEL), lambda i: (i, 0))
    g_spec = pl.BlockSpec((1, D_MODEL), lambda i: (0, 0))
    w_spec = pl.BlockSpec((D_MODEL, LANES), lambda i: (0, 0))
    hn_spec = pl.BlockSpec((NORM_TM, D_MODEL), lambda i: (i, 0))
    sm_spec = pl.BlockSpec((NORM_TM, LANES), lambda i: (i, 0))
    hn_shape = jax.ShapeDtypeStruct((TOKENS, D_MODEL), BF16)
    sm_shape = jax.ShapeDtypeStruct((TOKENS, LANES), F32)
    if not transposed:
        return pl.pallas_call(
            _norm_small_kernel, name="norm_small", grid=grid,
            in_specs=[x_spec, g_spec, w_spec],
            out_specs=[hn_spec, sm_spec],
            out_shape=[hn_shape, sm_shape],
            compiler_params=_cparams(("parallel",)),
        )(x, g.reshape(1, D_MODEL), ws)
    smt_spec = pl.BlockSpec((LANES, NORM_TM), lambda i: (0, i))
    smt_shape = jax.ShapeDtypeStruct((LANES, TOKENS), F32)
    return pl.pallas_call(
        _norm_small_t_kernel, name="norm_small_t", grid=grid,
        in_specs=[x_spec, g_spec, w_spec],
        out_specs=[hn_spec, sm_spec, smt_spec],
        out_shape=[hn_shape, sm_shape, smt_shape],
        compiler_params=_cparams(("parallel",)),
    )(x, g.reshape(1, D_MODEL), ws)


def _router_kernel(x_ref, g_ref, ws_ref, b_ref, eid_ref, cw_ref):
    y = _rms(x_ref[...], g_ref[...])
    logits = _dot_x3(y, ws_ref[...]) + b_ref[...]
    lane = lax.broadcasted_iota(I32, logits.shape, 1)
    big = jnp.int32(LANES)
    neg = -jnp.inf
    gl = jnp.where(lane < MOE_GROUPS, logits, neg)
    gmax = jnp.max(gl, axis=-1, keepdims=True)
    gsum = jnp.sum(jnp.exp(gl - gmax), axis=-1, keepdims=True)
    g_w = 1.0 / gsum
    g_sel = jnp.min(jnp.where(gl == gmax, lane, big), axis=-1, keepdims=True)
    lo = MOE_GROUPS + g_sel * MOE_EPG
    el = jnp.where((lane >= lo) & (lane < lo + MOE_EPG), logits, neg)
    v1 = jnp.max(el, axis=-1, keepdims=True)
    i1 = jnp.min(jnp.where(el == v1, lane, big), axis=-1, keepdims=True)
    el2 = jnp.where(lane == i1, neg, el)
    v2 = jnp.max(el2, axis=-1, keepdims=True)
    i2 = jnp.min(jnp.where(el2 == v2, lane, big), axis=-1, keepdims=True)
    e2 = jnp.exp(v2 - v1)
    den = 1.0 + e2
    w1 = (1.0 / den) * g_w
    w2 = (e2 / den) * g_w
    eid_ref[...] = jnp.where(lane == 0, i1 - MOE_GROUPS, jnp.where(lane == 1, i2 - MOE_GROUPS, 0))
    cw_ref[...] = jnp.where(lane == 0, w1, jnp.where(lane == 1, w2, 0.0))


def _router(x, g, w_group, b_group, w_expert, b_expert):
    n = MOE_GROUPS + MOE_EXPERTS
    ws = jnp.zeros((D_MODEL, LANES), F32).at[:, :n].set(jnp.concatenate([w_group, w_expert], axis=1))
    bs = jnp.zeros((1, LANES), F32).at[0, :n].set(jnp.concatenate([b_group, b_expert]))
    return pl.pallas_call(
        _router_kernel, name="router", grid=(TOKENS // NORM_TM,),
        in_specs=[pl.BlockSpec((NORM_TM, D_MODEL), lambda i: (i, 0)),
                  pl.BlockSpec((1, D_MODEL), lambda i: (0, 0)),
                  pl.BlockSpec((D_MODEL, LANES), lambda i: (0, 0)),
                  pl.BlockSpec((1, LANES), lambda i: (0, 0))],
        out_specs=[pl.BlockSpec((NORM_TM, LANES), lambda i: (i, 0)),
                   pl.BlockSpec((NORM_TM, LANES), lambda i: (i, 0))],
        out_shape=[jax.ShapeDtypeStruct((TOKENS, LANES), I32),
                   jax.ShapeDtypeStruct((TOKENS, LANES), F32)],
        compiler_params=_cparams(("parallel",)),
    )(x, g.reshape(1, D_MODEL), ws, bs)


def _inproj_heads_kernel(a_ref, w_ref, cos_ref, sin_ref, o_ref, wbf_ref):
    j = pl.program_id(0)

    @pl.when(pl.program_id(1) == 0)
    def _():
        wbf_ref[...] = w_ref[...].astype(BF16)

    acc = _dot_nt(a_ref[...], wbf_ref[...])
    q_tiles = NSA_QD // acc.shape[1]
    c = cos_ref[...]
    s = sin_ref[...]

    def head(h):
        return acc[:, h * NSA_HEAD_DIM:(h + 1) * NSA_HEAD_DIM]

    def rotary(xh):
        return (xh * c + pltpu.roll(xh, NSA_HEAD_DIM // 2, 1) * s).astype(BF16)

    for h in range(NSA_KV_HEADS):
        o_ref[h] = rotary(head(h))

    @pl.when(j < q_tiles)
    def _():
        for h in range(NSA_KV_HEADS, 2 * NSA_KV_HEADS):
            o_ref[h] = rotary(head(h))

    @pl.when(j >= q_tiles)
    def _():
        for h in range(NSA_KV_HEADS, 2 * NSA_KV_HEADS):
            o_ref[h] = head(h).astype(BF16)


def _nsa_inproj(hn, w_in_t, cos_full, sin_signed):
    tn = 2 * NSA_KVD
    n_heads_tile = tn // NSA_HEAD_DIM
    s_tiles = SEQ // MM_TM
    return pl.pallas_call(
        _inproj_heads_kernel, name="nsa_inproj", grid=(NSA_MAIN // tn, TOKENS // MM_TM),
        in_specs=[pl.BlockSpec((MM_TM, D_MODEL), lambda j, i: (i, 0)),
                  pl.BlockSpec((tn, D_MODEL), lambda j, i: (j, 0)),
                  pl.BlockSpec((MM_TM, NSA_HEAD_DIM), lambda j, i: (i % s_tiles, 0)),
                  pl.BlockSpec((MM_TM, NSA_HEAD_DIM), lambda j, i: (i % s_tiles, 0))],
        out_specs=pl.BlockSpec((n_heads_tile, MM_TM, NSA_HEAD_DIM), lambda j, i: (j, i, 0)),
        out_shape=jax.ShapeDtypeStruct((NSA_MAIN // NSA_HEAD_DIM, TOKENS, NSA_HEAD_DIM), BF16),
        scratch_shapes=[pltpu.VMEM((tn, D_MODEL), BF16)],
        compiler_params=_cparams(("arbitrary", "arbitrary")),
    )(hn, w_in_t, cos_full, sin_signed)


def _matmul_tiles_kernel(a_ref, w_ref, o_ref, wbf_ref):
    @pl.when(pl.program_id(1) == 0)
    def _():
        wbf_ref[...] = w_ref[...].astype(BF16)

    acc = _dot_nt(a_ref[...], wbf_ref[...])
    for t in range(acc.shape[1] // LANES):
        o_ref[t] = acc[:, t * LANES:(t + 1) * LANES].astype(o_ref.dtype)


def _matmul_tiles(a, w_t, n_cols, tn):
    k = a.shape[1]
    return pl.pallas_call(
        _matmul_tiles_kernel, name="matmul_tiles", grid=(n_cols // tn, TOKENS // MM_TM),
        in_specs=[pl.BlockSpec((MM_TM, k), lambda j, i: (i, 0)),
                  pl.BlockSpec((tn, k), lambda j, i: (j, 0))],
        out_specs=pl.BlockSpec((tn // LANES, MM_TM, LANES), lambda j, i: (j, i, 0)),
        out_shape=jax.ShapeDtypeStruct((n_cols // LANES, TOKENS, LANES), BF16),
        scratch_shapes=[pltpu.VMEM((tn, k), BF16)],
        compiler_params=_cparams(("arbitrary", "arbitrary")),
    )(a, w_t)


def _outproj_kernel(a_ref, w_ref, r_ref, o_ref):
    if len(a_ref.shape) == 3:
        a = jnp.concatenate([a_ref[t] for t in range(a_ref.shape[0])], axis=1)
    else:
        a = a_ref[...]
    o_ref[...] = r_ref[...] + _dot(a, w_ref[...])


def _outproj_resid(a, w, resid):
    tiled = a.ndim == 3
    k, n = w.shape
    a_spec = (pl.BlockSpec((k // LANES, OUT_TM, LANES), lambda i: (0, i, 0)) if tiled
              else pl.BlockSpec((OUT_TM, k), lambda i: (i, 0)))
    return pl.pallas_call(
        _outproj_kernel, name="outproj_resid", grid=(TOKENS // OUT_TM,),
        in_specs=[a_spec,
                  pl.BlockSpec((k, n), lambda i: (0, 0)),
                  pl.BlockSpec((OUT_TM, n), lambda i: (i, 0))],
        out_specs=pl.BlockSpec((OUT_TM, n), lambda i: (i, 0)),
        out_shape=jax.ShapeDtypeStruct((TOKENS, n), F32),
        compiler_params=_cparams(("parallel",)),
    )(a, w.astype(BF16), resid)


def _compress_kernel(x_ref, pe_ref, w1_ref, w2_ref, o_ref):
    half = CMP_STRIDE * NSA_HEAD_DIM
    x = x_ref[0, 0, 0]
    w1 = w1_ref[0].astype(BF16)
    top = _dot(x, w1[:half])
    bot = _dot(x, w1[half:])
    pe = jnp.broadcast_to(pe_ref[0], (8, 2 * half)).astype(BF16)
    pe_bias = _dot(pe, w1)[0:1]
    hid = top + pltpu.roll(bot, bot.shape[0] - 1, 0) + pe_bias
    act = jax.nn.gelu(hid)
    o_ref[0, 0, 0] = _dot(act.astype(BF16), w2_ref[0].astype(BF16))


def _compress(kv_chunks, pe, w1, w2):
    n_chunk = SEQ // CMP_STRIDE
    feat = CMP_STRIDE * NSA_HEAD_DIM
    return pl.pallas_call(
        _compress_kernel, name="nsa_compress", grid=(2, NSA_KV_HEADS, BATCH),
        in_specs=[pl.BlockSpec((1, 1, 1, n_chunk, feat), lambda a, g, b: (a, g, b, 0, 0)),
                  pl.BlockSpec((1, 1, 2 * feat), lambda a, g, b: (a, 0, 0)),
                  pl.BlockSpec((1, 2 * feat, CMP_HIDDEN), lambda a, g, b: (a, 0, 0)),
                  pl.BlockSpec((1, CMP_HIDDEN, NSA_HEAD_DIM), lambda a, g, b: (a, 0, 0))],
        out_specs=pl.BlockSpec((1, 1, 1, n_chunk, NSA_HEAD_DIM), lambda a, g, b: (a, g, b, 0, 0)),
        out_shape=jax.ShapeDtypeStruct((2, NSA_KV_HEADS, BATCH, n_chunk, NSA_HEAD_DIM), F32),
        compiler_params=_cparams(("parallel", "parallel", "parallel")),
    )(kv_chunks, pe.reshape(2, 1, 2 * feat), w1, w2)


def _nsa_attn_kernel(q_ref, ks_ref, vs_ref, kw_ref, vw_ref, kc_ref, vc_ref, gate_ref, o_ref,
                     vst_scr, vwt_scr, sel_scr, acc_scr):
    qi = pl.program_id(2)
    tq = ATT_TQ
    tk = ATT_TK
    dh = NSA_HEAD_DIM
    r_heads = NSA_Q_PER_KV
    n_kt = SEQ // tk
    scale = dh ** -0.5

    @pl.when(qi == 0)
    def _():
        ones = jnp.ones((ATT_ONES, tk), BF16)
        for kt in range(n_kt):
            rows = slice(kt * tk, (kt + 1) * tk)
            vst_scr[kt, 0:dh, :] = vs_ref[0, rows, :].astype(F32).T.astype(BF16)
            vwt_scr[kt, 0:dh, :] = vw_ref[0, rows, :].astype(F32).T.astype(BF16)
            vst_scr[kt, dh:dh + ATT_ONES, :] = ones
            vwt_scr[kt, dh:dh + ATT_ONES, :] = ones

    q_t = jnp.concatenate([q_ref[r].astype(F32).T for r in range(r_heads)], axis=1).astype(BF16)
    n_cp = SEQ // CMP_STRIDE
    sub = lax.broadcasted_iota(I32, (n_cp, tq), 0)
    t_pos = qi * tq + lax.broadcasted_iota(I32, (n_cp, tq), 1)

    def tile4(a):
        return jnp.concatenate([a] * r_heads, axis=1)

    kc = kc_ref[0, 0, 0].astype(BF16)
    vc = vc_ref[0, 0, 0].astype(BF16)
    ok_c = jnp.where(sub * CMP_STRIDE + CMP_BLOCK - 1 <= t_pos, jnp.where(sub < N_CMP, 1.0, 0.0), 0.0)
    ok_c4 = tile4(ok_c)
    s_c = _dot(kc, q_t) * scale + (ok_c4 - 1.0) * (-NEG_INF)
    e_c = jnp.exp(s_c - jnp.max(s_c, axis=0, keepdims=True))
    p_c = (e_c / jnp.sum(e_c, axis=0, keepdims=True)) * ok_c4
    o_cmp = _dot_tn(vc, p_c.astype(BF16))
    p_sum = p_c[:, 0:tq]
    for r in range(1, r_heads):
        p_sum = p_sum + p_c[:, r * tq:(r + 1) * tq]

    blk_row = lax.broadcasted_iota(I32, (LANES, LANES), 0)
    cmp_col = lax.broadcasted_iota(I32, (LANES, LANES), 1)
    s_start = blk_row * SLC_BLOCK
    c_start = cmp_col * CMP_STRIDE
    ov_t = jnp.maximum(jnp.minimum(c_start + CMP_BLOCK, s_start + SLC_BLOCK)
                       - jnp.maximum(c_start, s_start), 0).astype(F32) / CMP_BLOCK
    ov_t = jnp.where(blk_row < N_SLC, ov_t, 0.0).astype(BF16)
    imp = _dot_split_rhs(ov_t, p_sum)[0:N_SLC]
    j_blk = lax.broadcasted_iota(I32, (N_SLC, tq), 0)
    dist = (qi * tq + lax.broadcasted_iota(I32, (N_SLC, tq), 1)) // SLC_BLOCK - j_blk
    imp = jnp.where(j_blk == 0, SLC_FORCE, jnp.where(dist < 0, imp, jnp.where(dist < SLC_LOCAL, SLC_FORCE, imp)))
    imp = jnp.where(dist >= 0, imp, -jnp.inf)
    cnt = jnp.zeros((N_SLC, tq), I32)
    for k in range(N_SLC):
        row_k = imp[k:k + 1, :]
        tie = jnp.where(j_blk > k, 1, 0)
        cnt = cnt + jnp.where(row_k > imp, 1, jnp.where(row_k == imp, tie, 0))
    sel = jnp.where(cnt < min(SLC_TOPK, N_SLC), 1.0, 0.0)
    for j in range(N_SLC):
        sel_scr[8 * j:8 * j + 8, :] = jnp.broadcast_to(sel[j:j + 1, :], (8, tq))

    def scores(k, ok):
        return _dot(k, q_t) * scale + tile4((ok - 1.0) * (-NEG_INF))

    def weighted_values(vt_scr, kt0, pr, n_tiles):
        out = None
        for u in range(n_tiles):
            term = _dot(vt_scr[kt0 + u], pr[u * tk:(u + 1) * tk].astype(BF16))
            out = term if out is None else out + term
        return out

    n_ct = ATT_CHUNK_TILES
    ck = n_ct * tk
    sub_c = lax.broadcasted_iota(I32, (ck, tq), 0)
    t_pos_c = qi * tq + lax.broadcasted_iota(I32, (ck, tq), 1)
    blocks_per_chunk = ck // SLC_BLOCK
    acc_scr[...] = jnp.zeros(acc_scr.shape, F32)

    def slc_body(c, m_old):
        start = pl.multiple_of(c * ck, ck)
        k = ks_ref[0, pl.ds(start, ck), :]
        rows8 = sel_scr[pl.ds(pl.multiple_of(c * (8 * blocks_per_chunk), 8 * blocks_per_chunk),
                              8 * blocks_per_chunk), :]
        picked = jnp.concatenate(
            [rows8[8 * u:8 * u + 8] for u in range(blocks_per_chunk) for _ in range(SLC_BLOCK // 8)], axis=0)
        sc = scores(k, jnp.where(start + sub_c <= t_pos_c, picked, 0.0))
        m_new = jnp.maximum(m_old, jnp.max(sc, axis=0, keepdims=True))
        alpha = jnp.exp(m_old - m_new)
        pr = jnp.exp(sc - m_new)
        acc_scr[...] = alpha * acc_scr[...] + weighted_values(vst_scr, c * n_ct, pr, n_ct)
        return m_new

    lax.fori_loop(0, ((qi + 1) * tq + ck - 1) // ck, slc_body, jnp.full((1, r_heads * tq), NEG_INF, F32))
    acc = acc_scr[...]
    o_slc = acc[0:dh] / acc[dh:dh + 1]

    n_wt = (WINDOW + tq) // tk
    kt0 = jnp.maximum(qi * (tq // tk) - WINDOW // tk, 0)
    w_start = pl.multiple_of(kt0 * tk, tk)
    key_w = w_start + lax.broadcasted_iota(I32, (n_wt * tk, tq), 0)
    t_pos_w = qi * tq + lax.broadcasted_iota(I32, (n_wt * tk, tq), 1)
    ok_w = jnp.where(key_w <= t_pos_w, jnp.where(key_w > t_pos_w - WINDOW, 1.0, 0.0), 0.0)
    sc_w = scores(kw_ref[0, pl.ds(w_start, n_wt * tk), :], ok_w)
    pr_w = jnp.exp(sc_w - jnp.max(sc_w, axis=0, keepdims=True))
    acc_w = weighted_values(vwt_scr, kt0, pr_w, n_wt)
    o_win = acc_w[0:dh] / acc_w[dh:dh + 1]

    gate = jax.nn.sigmoid(gate_ref[0])
    for r in range(r_heads):
        cols = slice(r * tq, (r + 1) * tq)
        o = (gate[3 * r:3 * r + 1] * o_cmp[:, cols] + gate[3 * r + 1:3 * r + 2] * o_slc[:, cols]
             + gate[3 * r + 2:3 * r + 3] * o_win[:, cols])
        o_ref[:, r * dh:(r + 1) * dh] = o.T.astype(BF16)


def _nsa_attention(heads, kc_vc, gates):
    tq = ATT_TQ
    nq = SEQ // tq
    r = NSA_Q_PER_KV
    g_heads = NSA_KV_HEADS
    q_spec = pl.BlockSpec((r, tq, NSA_HEAD_DIM), lambda b, g, i: (g, b * nq + i, 0))

    def kv_spec(first_head):
        return pl.BlockSpec((1, SEQ, NSA_HEAD_DIM), lambda b, g, i: (first_head + g, b, 0))

    first = NSA_HEADS
    specs = [q_spec,
             kv_spec(first + 2 * g_heads), kv_spec(first + 3 * g_heads),
             kv_spec(first + 4 * g_heads), kv_spec(first + 5 * g_heads),
             pl.BlockSpec((1, 1, 1, SEQ // CMP_STRIDE, NSA_HEAD_DIM), lambda b, g, i: (0, g, b, 0, 0)),
             pl.BlockSpec((1, 1, 1, SEQ // CMP_STRIDE, NSA_HEAD_DIM), lambda b, g, i: (1, g, b, 0, 0)),
             pl.BlockSpec((1, 3 * r, tq), lambda b, g, i: (g, 0, b * nq + i))]
    vt_shape = (SEQ // ATT_TK, NSA_HEAD_DIM + ATT_ONES, ATT_TK)
    return pl.pallas_call(
        _nsa_attn_kernel, name="nsa_attn", grid=(BATCH, g_heads, nq),
        in_specs=specs,
        out_specs=pl.BlockSpec((tq, r * NSA_HEAD_DIM), lambda b, g, i: (b * nq + i, g)),
        out_shape=jax.ShapeDtypeStruct((TOKENS, NSA_QD), BF16),
        scratch_shapes=[pltpu.VMEM(vt_shape, BF16), pltpu.VMEM(vt_shape, BF16),
                        pltpu.VMEM((8 * N_SLC, tq), F32),
                        pltpu.VMEM((NSA_HEAD_DIM + ATT_ONES, r * tq), F32)],
        compiler_params=_cparams(("arbitrary", "arbitrary", "arbitrary")),
    )(heads, heads, heads, heads, heads, kc_vc, kc_vc, gates)


def _ssd_chunk_kernel(zx_ref, dtc_ref, dtr_ref, cw_ref, cb_ref, dtb_c_ref, alog_c_ref, dtb_r_ref, alog_r_ref,
                      dskip_ref, ng_ref, shift_ref, echan_ref, o_ref,
                      prev_scr, acum_r_scr, st_scr, fac_scr):
    chunk = pl.program_id(1)
    L = SSD_CHUNK
    W = SSD_GROUP_W
    hpg = SSD_HEADS_PER_GROUP
    n_xt = W // LANES
    x0 = SSD_D_INNER // LANES
    b0 = 2 * SSD_D_INNER // LANES
    c0 = b0 + SSD_GROUPS
    cb0 = SSD_D_INNER // LANES
    cc0 = cb0 + SSD_GROUPS

    @pl.when(chunk == 0)
    def _():
        prev_scr[...] = jnp.zeros(prev_scr.shape, BF16)
        st_scr[...] = jnp.zeros(st_scr.shape, F32)

    dt_c = jax.nn.softplus(dtc_ref[...] + dtb_c_ref[...])
    adt_c = dt_c * (-jnp.exp(alog_c_ref[...]))
    dt_r = jax.nn.softplus(dtr_ref[...] + dtb_r_ref[...])
    adt_r = dt_r * (-jnp.exp(alog_r_ref[...]))
    row = lax.broadcasted_iota(I32, (L, L), 0)
    col = lax.broadcasted_iota(I32, (L, L), 1)
    causal = row >= col
    tri = jnp.where(causal, 1.0, 0.0).astype(BF16)
    tri_t = jnp.where(col >= row, 1.0, 0.0).astype(BF16)
    acum_c = _dot_split_rhs(tri, adt_c)
    acum_r_scr[...] = _dot_split_lhs(adt_r, tri_t)
    a_last = acum_c[L - 1:L, :]
    fac = jnp.concatenate([dt_c, jnp.exp(acum_c), jnp.exp(a_last - acum_c),
                           jnp.broadcast_to(jnp.exp(a_last), (SSD_FAC_PAD, LANES))], axis=0)
    fac_hi = fac.astype(BF16)
    fac_scr[0] = fac_hi
    fac_scr[1] = (fac - fac_hi.astype(F32)).astype(BF16)
    lane_w = lax.broadcasted_iota(I32, (L, LANES), 1)
    first_half = lane_w < SSD_HEAD_DIM

    def tiles(ref, first, n):
        return jnp.concatenate([ref[first + q] for q in range(n)], axis=1)

    def group_body(g, carry):
        e_chan = echan_ref[g]
        ex = _dot(fac_scr[0], e_chan) + _dot(fac_scr[1], e_chan)
        dt_x, ea_x, sd_x, cd_x = ex[0:L], ex[L:2 * L], ex[2 * L:3 * L], ex[3 * L:3 * L + 1]

        def conv_silu(zx_first, conv_first, n):
            cur = tiles(zx_ref, zx_first, n)
            ext = jnp.concatenate([tiles(prev_scr, conv_first, n), cur], axis=0)
            w = tiles(cw_ref, conv_first, n)
            acc = jnp.broadcast_to(tiles(cb_ref, conv_first, n), (L, n * LANES))
            for k in range(SSD_CONV):
                back = SSD_CONV - 1 - k
                xk = cur.astype(F32) if back == 0 else _dot(shift_ref[back - 1], ext)
                acc = acc + xk * w[k:k + 1, :]
            for q in range(n):
                prev_scr[conv_first + q] = zx_ref[zx_first + q]
            return jax.nn.silu(acc)

        xs = conv_silu(x0 + n_xt * g, n_xt * g, n_xt)
        bm = conv_silu(b0 + g, cb0 + g, 1)
        cm = conv_silu(c0 + g, cc0 + g, 1)

        xdt = xs * dt_x
        cb = jnp.where(causal, _dot_nt(cm.astype(BF16), bm.astype(BF16)), 0.0)
        y_parts = []
        for pair in range(hpg // 2):
            xd = xdt[:, pair * LANES:(pair + 1) * LANES]
            m_pair = []
            for sub in range(2):
                a_row = jnp.broadcast_to(acum_r_scr[pl.ds(g * hpg + 2 * pair + sub, 1), :], (L, L))
                seg = jnp.minimum(a_row.T - a_row, 0.0)
                m_pair.append((cb * jnp.exp(seg)).astype(BF16))
            x_pair = jnp.concatenate([jnp.where(first_half, xd, 0.0), jnp.where(first_half, 0.0, xd)], axis=0)
            y_parts.append(_dot(jnp.concatenate(m_pair, axis=1), x_pair.astype(BF16)))
        y_diag = jnp.concatenate(y_parts, axis=1)

        st = st_scr[g]
        y_off = _dot(cm.astype(BF16), st.astype(BF16)) * ea_x
        st_scr[g] = st * cd_x + _dot_tn(bm.astype(BF16), (xdt * sd_x).astype(BF16))

        y = y_diag + y_off + xs * tiles(dskip_ref, n_xt * g, n_xt)
        y = y * jax.nn.silu(tiles(zx_ref, n_xt * g, n_xt).astype(F32))
        y = y * lax.rsqrt(jnp.mean(y * y, axis=-1, keepdims=True) + NORM_EPS)
        y = y * tiles(ng_ref, n_xt * g, n_xt)
        for q in range(n_xt):
            o_ref[n_xt * g + q] = y[:, q * LANES:(q + 1) * LANES].astype(BF16)
        return carry

    lax.fori_loop(0, SSD_GROUPS, group_body, 0)


def _ssd_chunks(zx_tiles, dt_small, dt_small_t, conv_w, conv_b, dt_bias, a_log, d_skip, norm_g):
    L = SSD_CHUNK
    nc = SEQ // L
    n_zx = SSD_MAIN // LANES
    n_conv = SSD_CONV_CH // LANES
    n_inner = SSD_D_INNER // LANES
    hpg = SSD_HEADS_PER_GROUP

    def pad_heads(v):
        return jnp.zeros((LANES,), F32).at[:SSD_HEADS].set(v)

    dtb = pad_heads(dt_bias)
    alog = pad_heads(a_log)
    cw = conv_w.reshape(SSD_CONV, n_conv, LANES).transpose(1, 0, 2)
    cb = conv_b.reshape(n_conv, 1, LANES)
    d_chan = jnp.repeat(d_skip, SSD_HEAD_DIM).reshape(n_inner, 1, LANES)
    ng = norm_g.reshape(n_inner, 1, LANES)
    t_idx = jnp.arange(L, dtype=I32)[None, :, None]
    r_idx = jnp.arange(2 * L, dtype=I32)[None, None, :]
    back = jnp.arange(1, SSD_CONV, dtype=I32)[:, None, None]
    shift = (r_idx == L + t_idx - back).astype(BF16)
    head = jnp.arange(LANES, dtype=I32)[None, :, None]
    grp = jnp.arange(SSD_GROUPS, dtype=I32)[:, None, None]
    e_chan = (head == grp * hpg + jnp.arange(SSD_GROUP_W, dtype=I32)[None, None, :] // SSD_HEAD_DIM).astype(BF16)
    row = lambda b, c: b * nc + c
    const3 = lambda b, c: (0, 0, 0)
    const2 = lambda b, c: (0, 0)
    in_specs = [
        pl.BlockSpec((n_zx, L, LANES), lambda b, c: (0, row(b, c), 0)),
        pl.BlockSpec((L, LANES), lambda b, c: (row(b, c), 0)),
        pl.BlockSpec((LANES, L), lambda b, c: (0, row(b, c))),
        pl.BlockSpec((n_conv, SSD_CONV, LANES), const3),
        pl.BlockSpec((n_conv, 1, LANES), const3),
        pl.BlockSpec((1, LANES), const2), pl.BlockSpec((1, LANES), const2),
        pl.BlockSpec((LANES, 1), const2), pl.BlockSpec((LANES, 1), const2),
        pl.BlockSpec((n_inner, 1, LANES), const3),
        pl.BlockSpec((n_inner, 1, LANES), const3),
        pl.BlockSpec((SSD_CONV - 1, L, 2 * L), const3),
        pl.BlockSpec((SSD_GROUPS, LANES, SSD_GROUP_W), const3),
    ]
    return pl.pallas_call(
        _ssd_chunk_kernel, name="ssd_chunks", grid=(BATCH, nc),
        in_specs=in_specs,
        out_specs=pl.BlockSpec((n_inner, L, LANES), lambda b, c: (0, row(b, c), 0)),
        out_shape=jax.ShapeDtypeStruct((n_inner, TOKENS, LANES), BF16),
        scratch_shapes=[pltpu.VMEM((n_conv, L, LANES), BF16),
                        pltpu.VMEM((LANES, L), F32),
                        pltpu.VMEM((SSD_GROUPS, SSD_D_STATE, SSD_GROUP_W), F32),
                        pltpu.VMEM((2, 3 * L + SSD_FAC_PAD, LANES), BF16)],
        compiler_params=_cparams(("arbitrary", "arbitrary")),
    )(zx_tiles, dt_small, dt_small_t, cw, cb, dtb.reshape(1, LANES), alog.reshape(1, LANES),
      dtb.reshape(LANES, 1), alog.reshape(LANES, 1), d_chan, ng, shift, e_chan)


def _gather_rows(src_hbm, idx_ref, base, dst, sem, n_rows):
    def body(b, carry):
        for u in range(GATHER_UNROLL):
            r = b * GATHER_UNROLL + u
            tok = idx_ref[base + r]
            pltpu.make_async_copy(src_hbm.at[pl.ds(tok, 1), :], dst.at[pl.ds(r, 1), :], sem).start()
        return carry

    lax.fori_loop(0, n_rows // GATHER_UNROLL, body, 0)


def _moe_ffn_kernel(te_ref, pos_ref, nact_ref, pad_ref, wp_ref, x_hbm, g_ref, wg_hbm, wu_hbm, wd_hbm, y_ref,
                    tok_ref, buf, sem, wg_f, wu_f, wd_f, wsem, wg_bf, wu_bf, wd_bf, *, layer):
    i = pl.program_id(0)
    n_act = nact_ref[0]
    tm = MOE_TM
    slot = i % 2
    run_start = wp_ref[i] == 1
    w_slot = wp_ref[MOE_TILES + i]
    next_expert = wp_ref[2 * MOE_TILES + i]
    n_rows = pl.multiple_of(wp_ref[3 * MOE_TILES + i], GATHER_UNROLL)

    def gather(tile, s):
        _gather_rows(x_hbm, tok_ref, tile * tm, buf.at[s], sem.at[s], wp_ref[3 * MOE_TILES + tile])

    def weight_copies(e, s):
        return (pltpu.make_async_copy(wg_hbm.at[layer, e], wg_f.at[s], wsem.at[s, 0]),
                pltpu.make_async_copy(wu_hbm.at[layer, e], wu_f.at[s], wsem.at[s, 1]),
                pltpu.make_async_copy(wd_hbm.at[layer, e], wd_f.at[s], wsem.at[s, 2]))

    @pl.when(i == 0)
    def _():
        for c in weight_copies(te_ref[0], 0):
            c.start()
        buf[...] = jnp.zeros(buf.shape, F32)
        def clear(q, carry):
            tok_ref[q] = 0
            return carry

        def clear_padding(e, carry):
            lax.fori_loop(pad_ref[e], pad_ref[MOE_EXPERTS + e], clear, 0)
            return carry

        lax.fori_loop(0, MOE_EXPERTS, clear_padding, 0)
        for k in range(MOE_TOPK):
            def place(t, carry, k=k):
                tok_ref[pos_ref[k * TOKENS + t]] = t
                return carry

            lax.fori_loop(0, TOKENS, place, 0, unroll=GATHER_UNROLL)
        gather(0, 0)

    @pl.when(run_start)
    def _():
        for c in weight_copies(te_ref[i], w_slot):
            c.wait()

        @pl.when(next_expert >= 0)
        def _():
            for c in weight_copies(next_expert, 1 - w_slot):
                c.start()

        wg_bf[...] = wg_f[w_slot].astype(BF16)
        wu_bf[...] = wu_f[w_slot].astype(BF16)
        wd_bf[...] = wd_f[w_slot].astype(BF16)

    @pl.when(i >= n_act)
    def _():
        y_ref[...] = jnp.zeros(y_ref.shape, F32)

    @pl.when(i < n_act)
    def _():
        pltpu.make_async_copy(x_hbm.at[pl.ds(0, n_rows), :], buf.at[slot, pl.ds(0, n_rows), :],
                              sem.at[slot]).wait()

        @pl.when(i + 1 < n_act)
        def _():
            gather(i + 1, 1 - slot)

        h = _rms(buf[slot], g_ref[...]).astype(BF16)
        act = jax.nn.silu(_dot(h, wg_bf[...])) * _dot(h, wu_bf[...])
        y_ref[...] = _dot(act.astype(BF16), wd_bf[...])


def _moe_ffn(x, g, w_gate, w_up, w_down, layer, tile_expert, pos_kmajor, n_active, pad_rows, weight_plan):
    any_spec = pl.BlockSpec(memory_space=pl.ANY)
    grid_spec = pltpu.PrefetchScalarGridSpec(
        num_scalar_prefetch=5, grid=(MOE_TILES,),
        in_specs=[any_spec, pl.BlockSpec((1, D_MODEL), lambda i, *_: (0, 0)), any_spec, any_spec, any_spec],
        out_specs=pl.BlockSpec((MOE_TM, D_MODEL), lambda i, *_: (i, 0)),
        scratch_shapes=[pltpu.SMEM((MOE_ROWS,), I32),
                        pltpu.VMEM((2, MOE_TM, D_MODEL), F32), pltpu.SemaphoreType.DMA((2,)),
                        pltpu.VMEM((2, D_MODEL, MOE_D_FF), F32), pltpu.VMEM((2, D_MODEL, MOE_D_FF), F32),
                        pltpu.VMEM((2, MOE_D_FF, D_MODEL), F32), pltpu.SemaphoreType.DMA((2, 3)),
                        pltpu.VMEM((D_MODEL, MOE_D_FF), BF16), pltpu.VMEM((D_MODEL, MOE_D_FF), BF16),
                        pltpu.VMEM((MOE_D_FF, D_MODEL), BF16)])
    return pl.pallas_call(
        functools.partial(_moe_ffn_kernel, layer=layer), name="moe_ffn", grid_spec=grid_spec,
        out_shape=jax.ShapeDtypeStruct((MOE_ROWS, D_MODEL), F32),
        compiler_params=_cparams(("arbitrary",)),
    )(tile_expert, pos_kmajor, n_active, pad_rows, weight_plan, x, g.reshape(1, D_MODEL), w_gate, w_up, w_down)


def _moe_combine_kernel(pos_ref, x_ref, cw_ref, g_ref, y_hbm, o_ref, buf, sem, *, final_norm):
    i = pl.program_id(0)
    n = pl.num_programs(0)
    tm = CMB_TM
    slot = i % 2

    def issue(tile, s):
        for k in range(MOE_TOPK):
            _gather_rows(y_hbm, pos_ref, (k * (TOKENS // tm) + tile) * tm, buf.at[s, k], sem.at[s], tm)

    @pl.when(i == 0)
    def _():
        issue(0, 0)

    @pl.when(i + 1 < n)
    def _():
        issue(i + 1, 1 - slot)

    for k in range(MOE_TOPK):
        pltpu.make_async_copy(y_hbm.at[pl.ds(0, tm), :], buf.at[slot, k], sem.at[slot]).wait()
    cw = cw_ref[...]
    out = x_ref[...] + cw[:, 0:1] * buf[slot, 0] + cw[:, 1:2] * buf[slot, 1]
    if final_norm:
        out = _rms(out, g_ref[...])
    o_ref[...] = out


def _moe_combine(x, cw, y_sorted, pos_kmajor, g_final, final_norm):
    grid_spec = pltpu.PrefetchScalarGridSpec(
        num_scalar_prefetch=1, grid=(TOKENS // CMB_TM,),
        in_specs=[pl.BlockSpec((CMB_TM, D_MODEL), lambda i, pos: (i, 0)),
                  pl.BlockSpec((CMB_TM, LANES), lambda i, pos: (i, 0)),
                  pl.BlockSpec((1, D_MODEL), lambda i, pos: (0, 0)),
                  pl.BlockSpec(memory_space=pl.ANY)],
        out_specs=pl.BlockSpec((CMB_TM, D_MODEL), lambda i, pos: (i, 0)),
        scratch_shapes=[pltpu.VMEM((2, MOE_TOPK, CMB_TM, D_MODEL), F32), pltpu.SemaphoreType.DMA((2,))])
    return pl.pallas_call(
        functools.partial(_moe_combine_kernel, final_norm=final_norm), name="moe_combine", grid_spec=grid_spec,
        out_shape=jax.ShapeDtypeStruct((TOKENS, D_MODEL), F32),
        compiler_params=_cparams(("arbitrary",)),
    )(pos_kmajor, x, cw, g_final.reshape(1, D_MODEL), y_sorted)


def _moe_plan(eid):
    e = eid[:, :MOE_TOPK].reshape(-1)
    onehot = (e[:, None] == jnp.arange(MOE_EXPERTS, dtype=I32)[None, :]).astype(I32)
    csum = jnp.cumsum(onehot, axis=0)
    counts = csum[-1]
    rank = jnp.take_along_axis(csum, e[:, None], axis=1)[:, 0] - 1
    padded = ((counts + MOE_TM - 1) // MOE_TM) * MOE_TM
    g_end = jnp.cumsum(padded)
    g_start = g_end - padded
    pos = g_start[e] + rank
    n_active = (g_end[-1] // MOE_TM).astype(I32)
    tile_start = jnp.arange(MOE_TILES, dtype=I32) * MOE_TM
    te = jnp.sum((g_end[None, :] <= tile_start[:, None]).astype(I32), axis=1)
    last = jnp.max(jnp.where(counts > 0, jnp.arange(MOE_EXPERTS, dtype=I32), 0))
    tile_expert = jnp.minimum(te, last)
    pos_kmajor = pos.reshape(TOKENS, MOE_TOPK).T.reshape(-1)
    pad_rows = jnp.concatenate([g_start + counts, g_end])
    experts = jnp.arange(MOE_EXPERTS, dtype=I32)
    run_start = jnp.concatenate([jnp.ones((1,), I32), (tile_expert[1:] != tile_expert[:-1]).astype(I32)])
    w_slot = (jnp.cumsum(run_start) - 1) % 2
    later = (experts[None, :] > experts[:, None]) & (counts > 0)[None, :]
    next_nonempty = jnp.min(jnp.where(later, experts[None, :], MOE_EXPERTS), axis=1)
    next_nonempty = jnp.where(next_nonempty == MOE_EXPERTS, -1, next_nonempty)
    real_rows = jnp.clip((g_start + counts)[tile_expert] - tile_start, 0, MOE_TM)
    real_rows = jnp.where(tile_start < g_end[-1], real_rows, 0)
    gather_rows = jnp.minimum(-(-real_rows // GATHER_UNROLL) * GATHER_UNROLL, MOE_TM)
    weight_plan = jnp.concatenate([run_start, w_slot, next_nonempty[tile_expert], gather_rows]).astype(I32)
    return tile_expert, n_active.reshape(1), pos_kmajor, pad_rows, weight_plan


def _hier_moe_add(x, ln_g, w_group, b_group, w_expert, b_expert, w_gate, w_up, w_down, layer,
                  g_final, final_norm):
    eid, cw = _router(x, ln_g, w_group, b_group, w_expert, b_expert)
    tile_expert, n_active, pos_kmajor, pad_rows, weight_plan = _moe_plan(eid)
    y_sorted = _moe_ffn(x, ln_g, w_gate, w_up, w_down, layer, tile_expert, pos_kmajor, n_active, pad_rows,
                        weight_plan)
    return _moe_combine(x, cw, y_sorted, pos_kmajor, g_final, final_norm)


def _rope_tables():
    pos = jnp.arange(SEQ, dtype=F32)
    inv = 1.0 / (ROPE_THETA ** (jnp.arange(0, NSA_HEAD_DIM, 2, dtype=F32) / NSA_HEAD_DIM))
    ang = pos[:, None] * inv[None, :]
    cos, sin = jnp.cos(ang), jnp.sin(ang)
    return jnp.concatenate([cos, cos], axis=1), jnp.concatenate([-sin, sin], axis=1)


def _nsa_mixer_add(x, ln_g, w_in, cmp_pe, cmp_w1, cmp_w2, w_out):
    w_in_t = w_in.T
    hn, _, g_lin_t = _norm_small(x, ln_g, w_in_t[NSA_MAIN:].T, transposed=True)
    cos_full, sin_signed = _rope_tables()
    heads = _nsa_inproj(hn, w_in_t, cos_full, sin_signed)
    first_c = NSA_HEADS
    kv_c = heads[first_c:first_c + 2 * NSA_KV_HEADS]
    kv_chunks = kv_c.reshape(2, NSA_KV_HEADS, BATCH, SEQ // CMP_STRIDE, CMP_STRIDE * NSA_HEAD_DIM)
    kc_vc = _compress(kv_chunks, cmp_pe, cmp_w1, cmp_w2)
    gates_t = g_lin_t[:NSA_GATES].reshape(NSA_KV_HEADS, 3 * NSA_Q_PER_KV, TOKENS)
    o = _nsa_attention(heads, kc_vc, gates_t)
    return _outproj_resid(o, w_out, x)


def _ssd_mixer_add(x, ln_g, w_in, conv_w, conv_b, dt_bias, a_log, d_skip, norm_g, w_out):
    w_in_t = w_in.T
    hn, dt_small, dt_small_t = _norm_small(x, ln_g, w_in_t[SSD_MAIN:].T, transposed=True)
    zx_tiles = _matmul_tiles(hn, w_in_t, SSD_MAIN, 1024)
    y_tiles = _ssd_chunks(zx_tiles, dt_small, dt_small_t, conv_w, conv_b, dt_bias, a_log, d_skip, norm_g)
    return _outproj_resid(y_tiles, w_out, x)


def kernel(x, ln_mix, ln_ffn, ln_final, nsa_w_in, nsa_cmp_pe, nsa_cmp_w1, nsa_cmp_w2, nsa_w_out,
           ssd_w_in, ssd_conv_w, ssd_conv_b, ssd_dt_bias, ssd_a_log, ssd_d, ssd_norm, ssd_w_out,
           moe_w_group, moe_b_group, moe_w_expert, moe_b_expert, moe_w_gate, moe_w_up, moe_w_down):
    h = x.reshape(TOKENS, D_MODEL)
    for i in range(DEPTH):
        j = i // N_MIXERS
        if i % N_MIXERS == 0:
            h = _nsa_mixer_add(h, ln_mix[i], nsa_w_in[j], nsa_cmp_pe[j], nsa_cmp_w1[j], nsa_cmp_w2[j],
                               nsa_w_out[j])
        else:
            h = _ssd_mixer_add(h, ln_mix[i], ssd_w_in[j], ssd_conv_w[j], ssd_conv_b[j], ssd_dt_bias[j],
                               ssd_a_log[j], ssd_d[j], ssd_norm[j], ssd_w_out[j])
        h = _hier_moe_add(h, ln_ffn[i], moe_w_group[i], moe_b_group[i], moe_w_expert[i], moe_b_expert[i],
                          moe_w_gate, moe_w_up, moe_w_down, i, ln_final, i == DEPTH - 1)
    return h.reshape(BATCH, SEQ, D_MODEL)
```

```python
import functools

import jax
import jax.numpy as jnp
from jax import lax
from jax.experimental import pallas as pl
from jax.experimental.pallas import tpu as pltpu

F32 = jnp.float32
BF16 = jnp.bfloat16
I32 = jnp.int32

D_MODEL = 2048
BATCH = 4
SEQ = 2048
TOKENS = BATCH * SEQ
DEPTH = 2
N_MIXERS = 2
NORM_EPS = 1e-6
NEG_INF = -1e30
LOG2_E = 1.4426950408889634
ROPE_THETA = 10000.0

NSA_HEADS = 16
NSA_KV_HEADS = 4
NSA_HEAD_DIM = D_MODEL // NSA_HEADS
NSA_Q_PER_KV = NSA_HEADS // NSA_KV_HEADS
CMP_BLOCK = 32
CMP_STRIDE = 16
CMP_HIDDEN = 256
N_CMP = (SEQ - CMP_BLOCK) // CMP_STRIDE + 1
SLC_BLOCK = 64
SLC_TOPK = 16
SLC_LOCAL = 2
SLC_FORCE = 1e4
N_SLC = SEQ // SLC_BLOCK
WINDOW = 512
NSA_QD = NSA_HEADS * NSA_HEAD_DIM
NSA_KVD = NSA_KV_HEADS * NSA_HEAD_DIM
NSA_MAIN = NSA_QD + 6 * NSA_KVD
NSA_GATES = 3 * NSA_HEADS

SSD_D_INNER = 2 * D_MODEL
SSD_HEAD_DIM = 64
SSD_HEADS = SSD_D_INNER // SSD_HEAD_DIM
SSD_GROUPS = 8
SSD_HEADS_PER_GROUP = SSD_HEADS // SSD_GROUPS
SSD_D_STATE = 128
SSD_CONV = 4
SSD_CHUNK = 128
SSD_GROUP_W = SSD_D_INNER // SSD_GROUPS
SSD_BC = SSD_GROUPS * SSD_D_STATE
SSD_CONV_CH = SSD_D_INNER + 2 * SSD_BC
SSD_MAIN = SSD_D_INNER + SSD_CONV_CH
SSD_FAC_PAD = 16

MOE_GROUPS = 4
MOE_EPG = 8
MOE_EXPERTS = MOE_GROUPS * MOE_EPG
MOE_TOPK = 2
MOE_D_FF = 512

LANES = 128
VMEM_LIMIT = 56 * 1024 * 1024

NORM_TM = 256
MM_TM = 512
OUT_TM = 256
ATT_TQ = 256
ATT_TK = 256
ATT_CHUNK_TILES = 2
ATT_ONES = 16
MOE_TM = 256
MOE_TILES = (TOKENS * MOE_TOPK) // MOE_TM + MOE_EXPERTS
MOE_ROWS = MOE_TILES * MOE_TM
CMB_TM = 128
GATHER_UNROLL = 8


def _cparams(sem):
    return pltpu.CompilerParams(dimension_semantics=sem, vmem_limit_bytes=VMEM_LIMIT)


def _split3(x):
    hi = x.astype(BF16)
    r1 = x - hi.astype(F32)
    mid = r1.astype(BF16)
    lo = (r1 - mid.astype(F32)).astype(BF16)
    return hi, mid, lo


def _dot(a, b):
    return jnp.dot(a, b, preferred_element_type=F32)


def _dot_nt(a, b):
    return lax.dot_general(a, b, (((1,), (1,)), ((), ())), preferred_element_type=F32)


def _dot_tn(a, b):
    return lax.dot_general(a, b, (((0,), (0,)), ((), ())), preferred_element_type=F32)


def _dot_split_lhs(x, m_bf16):
    hi, mid, lo = _split3(x)
    return _dot(hi, m_bf16) + _dot(mid, m_bf16) + _dot(lo, m_bf16)


def _dot_split_rhs(m_bf16, x):
    hi, mid, lo = _split3(x)
    return _dot(m_bf16, hi) + _dot(m_bf16, mid) + _dot(m_bf16, lo)


def _dot_x3(a, w):
    a_hi = a.astype(BF16)
    a_lo = (a - a_hi.astype(F32)).astype(BF16)
    w_hi = w.astype(BF16)
    w_lo = (w - w_hi.astype(F32)).astype(BF16)
    return _dot(a_hi, w_hi) + _dot(a_hi, w_lo) + _dot(a_lo, w_hi)


def _dot_x3_nt(a, w):
    a_hi = a.astype(BF16)
    a_lo = (a - a_hi.astype(F32)).astype(BF16)
    w_hi = w.astype(BF16)
    w_lo = (w - w_hi.astype(F32)).astype(BF16)
    return _dot_nt(a_hi, w_hi) + _dot_nt(a_hi, w_lo) + _dot_nt(a_lo, w_hi)


def _rms(x, g):
    y = x * lax.rsqrt(jnp.mean(x * x, axis=-1, keepdims=True) + NORM_EPS)
    return y * g


def _norm_small_kernel(x_ref, g_ref, ws_ref, hn_ref, small_ref):
    y = _rms(x_ref[...], g_ref[...])
    hn_ref[...] = y.astype(BF16)
    small_ref[...] = _dot_x3(y, ws_ref[...])


def _norm_small_t_kernel(x_ref, g_ref, ws_ref, hn_ref, small_ref, small_t_ref):
    y = _rms(x_ref[...], g_ref[...])
    hn_ref[...] = y.astype(BF16)
    small = _dot_x3(y, ws_ref[...])
    small_ref[...] = small
    small_t_ref[...] = small.T


def _norm_small(x, g, w_small, transposed=False):
    n = w_small.shape[1]
    ws = jnp.zeros((D_MODEL, LANES), F32).at[:, :n].set(w_small)
    grid = (TOKENS // NORM_TM,)
    x_spec = pl.BlockSpec((NORM_TM, D_MODEL), lambda i: (i, 0))
    g_spec = pl.BlockSpec((1, D_MODEL), lambda i: (0, 0))
    w_spec = pl.BlockSpec((D_MODEL, LANES), lambda i: (0, 0))
    hn_spec = pl.BlockSpec((NORM_TM, D_MODEL), lambda i: (i, 0))
    sm_spec = pl.BlockSpec((NORM_TM, LANES), lambda i: (i, 0))
    hn_shape = jax.ShapeDtypeStruct((TOKENS, D_MODEL), BF16)
    sm_shape = jax.ShapeDtypeStruct((TOKENS, LANES), F32)
    if not transposed:
        return pl.pallas_call(
            _norm_small_kernel, name="norm_small", grid=grid,
            in_specs=[x_spec, g_spec, w_spec],
            out_specs=[hn_spec, sm_spec],
            out_shape=[hn_shape, sm_shape],
            compiler_params=_cparams(("parallel",)),
        )(x, g.reshape(1, D_MODEL), ws)
    smt_spec = pl.BlockSpec((LANES, NORM_TM), lambda i: (0, i))
    smt_shape = jax.ShapeDtypeStruct((LANES, TOKENS), F32)
    return pl.pallas_call(
        _norm_small_t_kernel, name="norm_small_t", grid=grid,
        in_specs=[x_spec, g_spec, w_spec],
        out_specs=[hn_spec, sm_spec, smt_spec],
        out_shape=[hn_shape, sm_shape, smt_shape],
        compiler_params=_cparams(("parallel",)),
    )(x, g.reshape(1, D_MODEL), ws)


def _router_kernel(x_ref, g_ref, ws_ref, b_ref, eid_ref, cw_ref):
    y = _rms(x_ref[...], g_ref[...])
    logits = _dot_x3(y, ws_ref[...]) + b_ref[...]
    lane = lax.broadcasted_iota(I32, logits.shape, 1)
    big = jnp.int32(LANES)
    neg = -jnp.inf
    gl = jnp.where(lane < MOE_GROUPS, logits, neg)
    gmax = jnp.max(gl, axis=-1, keepdims=True)
    gsum = jnp.sum(jnp.exp(gl - gmax), axis=-1, keepdims=True)
    g_w = 1.0 / gsum
    g_sel = jnp.min(jnp.where(gl == gmax, lane, big), axis=-1, keepdims=True)
    lo = MOE_GROUPS + g_sel * MOE_EPG
    el = jnp.where((lane >= lo) & (lane < lo + MOE_EPG), logits, neg)
    v1 = jnp.max(el, axis=-1, keepdims=True)
    i1 = jnp.min(jnp.where(el == v1, lane, big), axis=-1, keepdims=True)
    el2 = jnp.where(lane == i1, neg, el)
    v2 = jnp.max(el2, axis=-1, keepdims=True)
    i2 = jnp.min(jnp.where(el2 == v2, lane, big), axis=-1, keepdims=True)
    e2 = jnp.exp(v2 - v1)
    den = 1.0 + e2
    w1 = (1.0 / den) * g_w
    w2 = (e2 / den) * g_w
    eid_ref[...] = jnp.where(lane == 0, i1 - MOE_GROUPS, jnp.where(lane == 1, i2 - MOE_GROUPS, 0))
    cw_ref[...] = jnp.where(lane == 0, w1, jnp.where(lane == 1, w2, 0.0))


def _router(x, g, w_group, b_group, w_expert, b_expert):
    n = MOE_GROUPS + MOE_EXPERTS
    ws = jnp.zeros((D_MODEL, LANES), F32).at[:, :n].set(jnp.concatenate([w_group, w_expert], axis=1))
    bs = jnp.zeros((1, LANES), F32).at[0, :n].set(jnp.concatenate([b_group, b_expert]))
    return pl.pallas_call(
        _router_kernel, name="router", grid=(TOKENS // NORM_TM,),
        in_specs=[pl.BlockSpec((NORM_TM, D_MODEL), lambda i: (i, 0)),
                  pl.BlockSpec((1, D_MODEL), lambda i: (0, 0)),
                  pl.BlockSpec((D_MODEL, LANES), lambda i: (0, 0)),
                  pl.BlockSpec((1, LANES), lambda i: (0, 0))],
        out_specs=[pl.BlockSpec((NORM_TM, LANES), lambda i: (i, 0)),
                   pl.BlockSpec((NORM_TM, LANES), lambda i: (i, 0))],
        out_shape=[jax.ShapeDtypeStruct((TOKENS, LANES), I32),
                   jax.ShapeDtypeStruct((TOKENS, LANES), F32)],
        compiler_params=_cparams(("parallel",)),
    )(x, g.reshape(1, D_MODEL), ws, bs)


def _inproj_heads_kernel(a_ref, w_ref, cos_ref, sin_ref, o_ref, wbf_ref):
    j = pl.program_id(0)

    @pl.when(pl.program_id(1) == 0)
    def _():
        wbf_ref[...] = w_ref[...].astype(BF16)

    acc = _dot_nt(a_ref[...], wbf_ref[...])
    q_tiles = NSA_QD // acc.shape[1]
    c = cos_ref[...]
    s = sin_ref[...]

    def head(h):
        return acc[:, h * NSA_HEAD_DIM:(h + 1) * NSA_HEAD_DIM]

    def rotary(xh):
        return (xh * c + pltpu.roll(xh, NSA_HEAD_DIM // 2, 1) * s).astype(BF16)

    for h in range(NSA_KV_HEADS):
        o_ref[h] = rotary(head(h))

    @pl.when(j < q_tiles)
    def _():
        for h in range(NSA_KV_HEADS, 2 * NSA_KV_HEADS):
            o_ref[h] = rotary(head(h))

    @pl.when(j >= q_tiles)
    def _():
        for h in range(NSA_KV_HEADS, 2 * NSA_KV_HEADS):
            o_ref[h] = head(h).astype(BF16)


def _nsa_inproj(hn, w_in_t, cos_full, sin_signed):
    tn = 2 * NSA_KVD
    n_heads_tile = tn // NSA_HEAD_DIM
    s_tiles = SEQ // MM_TM
    return pl.pallas_call(
        _inproj_heads_kernel, name="nsa_inproj", grid=(NSA_MAIN // tn, TOKENS // MM_TM),
        in_specs=[pl.BlockSpec((MM_TM, D_MODEL), lambda j, i: (i, 0)),
                  pl.BlockSpec((tn, D_MODEL), lambda j, i: (j, 0)),
                  pl.BlockSpec((MM_TM, NSA_HEAD_DIM), lambda j, i: (i % s_tiles, 0)),
                  pl.BlockSpec((MM_TM, NSA_HEAD_DIM), lambda j, i: (i % s_tiles, 0))],
        out_specs=pl.BlockSpec((n_heads_tile, MM_TM, NSA_HEAD_DIM), lambda j, i: (j, i, 0)),
        out_shape=jax.ShapeDtypeStruct((NSA_MAIN // NSA_HEAD_DIM, TOKENS, NSA_HEAD_DIM), BF16),
        scratch_shapes=[pltpu.VMEM((tn, D_MODEL), BF16)],
        compiler_params=_cparams(("arbitrary", "arbitrary")),
    )(hn, w_in_t, cos_full, sin_signed)


def _matmul_tiles_kernel(a_ref, w_ref, o_ref, wbf_ref):
    @pl.when(pl.program_id(1) == 0)
    def _():
        wbf_ref[...] = w_ref[...].astype(BF16)

    acc = _dot_nt(a_ref[...], wbf_ref[...])
    for t in range(acc.shape[1] // LANES):
        o_ref[t] = acc[:, t * LANES:(t + 1) * LANES].astype(o_ref.dtype)


def _matmul_tiles(a, w_t, n_cols, tn):
    k = a.shape[1]
    return pl.pallas_call(
        _matmul_tiles_kernel, name="matmul_tiles", grid=(n_cols // tn, TOKENS // MM_TM),
        in_specs=[pl.BlockSpec((MM_TM, k), lambda j, i: (i, 0)),
                  pl.BlockSpec((tn, k), lambda j, i: (j, 0))],
        out_specs=pl.BlockSpec((tn // LANES, MM_TM, LANES), lambda j, i: (j, i, 0)),
        out_shape=jax.ShapeDtypeStruct((n_cols // LANES, TOKENS, LANES), BF16),
        scratch_shapes=[pltpu.VMEM((tn, k), BF16)],
        compiler_params=_cparams(("arbitrary", "arbitrary")),
    )(a, w_t)


def _outproj_kernel(a_ref, w_ref, r_ref, o_ref):
    if len(a_ref.shape) == 3:
        a = jnp.concatenate([a_ref[t] for t in range(a_ref.shape[0])], axis=1)
    else:
        a = a_ref[...]
    o_ref[...] = r_ref[...] + _dot(a, w_ref[...])


def _outproj_resid(a, w, resid):
    tiled = a.ndim == 3
    k, n = w.shape
    a_spec = (pl.BlockSpec((k // LANES, OUT_TM, LANES), lambda i: (0, i, 0)) if tiled
              else pl.BlockSpec((OUT_TM, k), lambda i: (i, 0)))
    return pl.pallas_call(
        _outproj_kernel, name="outproj_resid", grid=(TOKENS // OUT_TM,),
        in_specs=[a_spec,
                  pl.BlockSpec((k, n), lambda i: (0, 0)),
                  pl.BlockSpec((OUT_TM, n), lambda i: (i, 0))],
        out_specs=pl.BlockSpec((OUT_TM, n), lambda i: (i, 0)),
        out_shape=jax.ShapeDtypeStruct((TOKENS, n), F32),
        compiler_params=_cparams(("parallel",)),
    )(a, w.astype(BF16), resid)


def _compress_kernel(x_ref, pe_ref, w1_ref, w2_ref, o_ref):
    half = CMP_STRIDE * NSA_HEAD_DIM
    x = x_ref[0, 0, 0]
    w1 = w1_ref[0].astype(BF16)
    top = _dot(x, w1[:half])
    bot = _dot(x, w1[half:])
    pe = jnp.broadcast_to(pe_ref[0], (8, 2 * half)).astype(BF16)
    pe_bias = _dot(pe, w1)[0:1]
    hid = top + pltpu.roll(bot, bot.shape[0] - 1, 0) + pe_bias
    act = jax.nn.gelu(hid)
    o_ref[0, 0, 0] = _dot(act.astype(BF16), w2_ref[0].astype(BF16))


def _compress(kv_chunks, pe, w1, w2):
    n_chunk = SEQ // CMP_STRIDE
    feat = CMP_STRIDE * NSA_HEAD_DIM
    return pl.pallas_call(
        _compress_kernel, name="nsa_compress", grid=(2, NSA_KV_HEADS, BATCH),
        in_specs=[pl.BlockSpec((1, 1, 1, n_chunk, feat), lambda a, g, b: (a, g, b, 0, 0)),
                  pl.BlockSpec((1, 1, 2 * feat), lambda a, g, b: (a, 0, 0)),
                  pl.BlockSpec((1, 2 * feat, CMP_HIDDEN), lambda a, g, b: (a, 0, 0)),
                  pl.BlockSpec((1, CMP_HIDDEN, NSA_HEAD_DIM), lambda a, g, b: (a, 0, 0))],
        out_specs=pl.BlockSpec((1, 1, 1, n_chunk, NSA_HEAD_DIM), lambda a, g, b: (a, g, b, 0, 0)),
        out_shape=jax.ShapeDtypeStruct((2, NSA_KV_HEADS, BATCH, n_chunk, NSA_HEAD_DIM), F32),
        compiler_params=_cparams(("parallel", "parallel", "parallel")),
    )(kv_chunks, pe.reshape(2, 1, 2 * feat), w1, w2)


def _nsa_attn_kernel(q_ref, ks_ref, vs_ref, kw_ref, vw_ref, kc_ref, vc_ref, gate_ref, o_ref,
                     vst_scr, vwt_scr, sel_scr, acc_scr, sc_scr):
    qi = pl.program_id(2)
    tq = ATT_TQ
    tk = ATT_TK
    dh = NSA_HEAD_DIM
    r_heads = NSA_Q_PER_KV
    n_kt = SEQ // tk
    scale = dh ** -0.5 * LOG2_E

    @pl.when(qi == 0)
    def _():
        ones = jnp.ones((ATT_ONES, tk), BF16)
        for kt in range(n_kt):
            rows = slice(kt * tk, (kt + 1) * tk)
            vst_scr[kt, 0:dh, :] = vs_ref[0, rows, :].astype(F32).T.astype(BF16)
            vwt_scr[kt, 0:dh, :] = vw_ref[0, rows, :].astype(F32).T.astype(BF16)
            vst_scr[kt, dh:dh + ATT_ONES, :] = ones
            vwt_scr[kt, dh:dh + ATT_ONES, :] = ones

    q_s = jnp.concatenate([q_ref[r].astype(F32).T for r in range(r_heads)], axis=1) * scale
    q_hi = q_s.astype(BF16)
    q_t = jnp.concatenate([q_hi, (q_s - q_hi.astype(F32)).astype(BF16)], axis=0)

    def qk(k):
        return _dot(jnp.concatenate([k, k], axis=1), q_t)
    n_cp = SEQ // CMP_STRIDE
    sub = lax.broadcasted_iota(I32, (n_cp, tq), 0)
    t_pos = qi * tq + lax.broadcasted_iota(I32, (n_cp, tq), 1)

    def tile4(a):
        return jnp.concatenate([a] * r_heads, axis=1)

    kc = kc_ref[0, 0, 0].astype(BF16)
    vc = vc_ref[0, 0, 0].astype(BF16)
    ok_c = jnp.where(sub * CMP_STRIDE + CMP_BLOCK - 1 <= t_pos, jnp.where(sub < N_CMP, 1.0, 0.0), 0.0)
    ok_c4 = tile4(ok_c)
    s_c = qk(kc) + (ok_c4 - 1.0) * (-NEG_INF)
    e_c = jnp.exp2(s_c - jnp.max(s_c, axis=0, keepdims=True))
    p_c = (e_c / jnp.sum(e_c, axis=0, keepdims=True)) * ok_c4
    o_cmp = _dot_tn(vc, p_c.astype(BF16))
    p_sum = p_c[:, 0:tq]
    for r in range(1, r_heads):
        p_sum = p_sum + p_c[:, r * tq:(r + 1) * tq]

    blk_row = lax.broadcasted_iota(I32, (LANES, LANES), 0)
    cmp_col = lax.broadcasted_iota(I32, (LANES, LANES), 1)
    s_start = blk_row * SLC_BLOCK
    c_start = cmp_col * CMP_STRIDE
    ov_t = jnp.maximum(jnp.minimum(c_start + CMP_BLOCK, s_start + SLC_BLOCK)
                       - jnp.maximum(c_start, s_start), 0).astype(F32) / CMP_BLOCK
    ov_t = jnp.where(blk_row < N_SLC, ov_t, 0.0).astype(BF16)
    imp = _dot_split_rhs(ov_t, p_sum)[0:N_SLC]
    j_blk = lax.broadcasted_iota(I32, (N_SLC, tq), 0)
    dist = (qi * tq + lax.broadcasted_iota(I32, (N_SLC, tq), 1)) // SLC_BLOCK - j_blk
    imp = jnp.where(j_blk == 0, SLC_FORCE, jnp.where(dist < 0, imp, jnp.where(dist < SLC_LOCAL, SLC_FORCE, imp)))
    imp = jnp.where(dist >= 0, imp, -jnp.inf)
    cnt = jnp.zeros((N_SLC, tq), I32)
    for k in range(N_SLC):
        row_k = imp[k:k + 1, :]
        tie = jnp.where(j_blk > k, 1, 0)
        cnt = cnt + jnp.where(row_k > imp, 1, jnp.where(row_k == imp, tie, 0))
    sel = jnp.where(cnt < min(SLC_TOPK, N_SLC), 1.0, 0.0)
    for j in range(N_SLC):
        sel_scr[8 * j:8 * j + 8, :] = jnp.broadcast_to(sel[j:j + 1, :], (8, tq))

    def scores(k, ok):
        return qk(k) + tile4((ok - 1.0) * (-NEG_INF))

    def weighted_values(vt_scr, kt0, pr, n_tiles):
        out = None
        for u in range(n_tiles):
            term = _dot(vt_scr[kt0 + u], pr[u * tk:(u + 1) * tk].astype(BF16))
            out = term if out is None else out + term
        return out

    n_ct = ATT_CHUNK_TILES
    ck = n_ct * tk
    sub_c = lax.broadcasted_iota(I32, (ck, tq), 0)
    t_pos_c = qi * tq + lax.broadcasted_iota(I32, (ck, tq), 1)
    blocks_per_chunk = ck // SLC_BLOCK
    acc_scr[...] = jnp.zeros(acc_scr.shape, F32)

    def chunk_scores(c):
        c = jnp.minimum(c, SEQ // ck - 1)
        start = pl.multiple_of(c * ck, ck)
        k = ks_ref[0, pl.ds(start, ck), :]
        rows8 = sel_scr[pl.ds(pl.multiple_of(c * (8 * blocks_per_chunk), 8 * blocks_per_chunk),
                              8 * blocks_per_chunk), :]
        picked = jnp.concatenate(
            [rows8[8 * u:8 * u + 8] for u in range(blocks_per_chunk) for _ in range(SLC_BLOCK // 8)], axis=0)
        return scores(k, jnp.where(start + sub_c <= t_pos_c, picked, 0.0))

    sc_scr[...] = chunk_scores(0)

    def slc_body(c, m_old):
        sc = sc_scr[...]
        sc_next = chunk_scores(c + 1)
        m_new = jnp.maximum(m_old, jnp.max(sc, axis=0, keepdims=True))
        alpha = jnp.exp2(m_old - m_new)
        pr = jnp.exp2(sc - m_new)
        acc_scr[...] = alpha * acc_scr[...] + weighted_values(vst_scr, c * n_ct, pr, n_ct)
        sc_scr[...] = sc_next
        return m_new

    lax.fori_loop(0, ((qi + 1) * tq + ck - 1) // ck, slc_body, jnp.full((1, r_heads * tq), NEG_INF, F32))
    acc = acc_scr[...]
    o_slc = acc[0:dh] / acc[dh:dh + 1]

    n_wt = (WINDOW + tq) // tk
    kt0 = jnp.maximum(qi * (tq // tk) - WINDOW // tk, 0)
    w_start = pl.multiple_of(kt0 * tk, tk)
    key_w = w_start + lax.broadcasted_iota(I32, (n_wt * tk, tq), 0)
    t_pos_w = qi * tq + lax.broadcasted_iota(I32, (n_wt * tk, tq), 1)
    ok_w = jnp.where(key_w <= t_pos_w, jnp.where(key_w > t_pos_w - WINDOW, 1.0, 0.0), 0.0)
    sc_w = scores(kw_ref[0, pl.ds(w_start, n_wt * tk), :], ok_w)
    pr_w = jnp.exp2(sc_w - jnp.max(sc_w, axis=0, keepdims=True))
    acc_w = weighted_values(vwt_scr, kt0, pr_w, n_wt)
    o_win = acc_w[0:dh] / acc_w[dh:dh + 1]

    gate = jax.nn.sigmoid(gate_ref[0])
    for r in range(r_heads):
        cols = slice(r * tq, (r + 1) * tq)
        o = (gate[3 * r:3 * r + 1] * o_cmp[:, cols] + gate[3 * r + 1:3 * r + 2] * o_slc[:, cols]
             + gate[3 * r + 2:3 * r + 3] * o_win[:, cols])
        o_ref[:, r * dh:(r + 1) * dh] = o.T.astype(BF16)


def _nsa_attention(heads, kc_vc, gates):
    tq = ATT_TQ
    nq = SEQ // tq
    r = NSA_Q_PER_KV
    g_heads = NSA_KV_HEADS
    q_spec = pl.BlockSpec((r, tq, NSA_HEAD_DIM), lambda b, g, i: (g, b * nq + i, 0))

    def kv_spec(first_head):
        return pl.BlockSpec((1, SEQ, NSA_HEAD_DIM), lambda b, g, i: (first_head + g, b, 0))

    first = NSA_HEADS
    specs = [q_spec,
             kv_spec(first + 2 * g_heads), kv_spec(first + 3 * g_heads),
             kv_spec(first + 4 * g_heads), kv_spec(first + 5 * g_heads),
             pl.BlockSpec((1, 1, 1, SEQ // CMP_STRIDE, NSA_HEAD_DIM), lambda b, g, i: (0, g, b, 0, 0)),
             pl.BlockSpec((1, 1, 1, SEQ // CMP_STRIDE, NSA_HEAD_DIM), lambda b, g, i: (1, g, b, 0, 0)),
             pl.BlockSpec((1, 3 * r, tq), lambda b, g, i: (g, 0, b * nq + i))]
    vt_shape = (SEQ // ATT_TK, NSA_HEAD_DIM + ATT_ONES, ATT_TK)
    return pl.pallas_call(
        _nsa_attn_kernel, name="nsa_attn", grid=(BATCH, g_heads, nq),
        in_specs=specs,
        out_specs=pl.BlockSpec((tq, r * NSA_HEAD_DIM), lambda b, g, i: (b * nq + i, g)),
        out_shape=jax.ShapeDtypeStruct((TOKENS, NSA_QD), BF16),
        scratch_shapes=[pltpu.VMEM(vt_shape, BF16), pltpu.VMEM(vt_shape, BF16),
                        pltpu.VMEM((8 * N_SLC, tq), F32),
                        pltpu.VMEM((NSA_HEAD_DIM + ATT_ONES, r * tq), F32),
                        pltpu.VMEM((ATT_CHUNK_TILES * ATT_TK, r * tq), F32)],
        compiler_params=_cparams(("arbitrary", "arbitrary", "arbitrary")),
    )(heads, heads, heads, heads, heads, kc_vc, kc_vc, gates)


def _ssd_chunk_kernel(zx_ref, dtc_ref, dtr_ref, cw_ref, cb_ref, dtb_c_ref, alog_c_ref, dtb_r_ref, alog_r_ref,
                      dskip_ref, ng_ref, shift_ref, echan_ref, o_ref,
                      prev_scr, acum_r_scr, st_scr, fac_scr):
    chunk = pl.program_id(1)
    L = SSD_CHUNK
    W = SSD_GROUP_W
    hpg = SSD_HEADS_PER_GROUP
    n_xt = W // LANES
    x0 = SSD_D_INNER // LANES
    b0 = 2 * SSD_D_INNER // LANES
    c0 = b0 + SSD_GROUPS
    cb0 = SSD_D_INNER // LANES
    cc0 = cb0 + SSD_GROUPS

    @pl.when(chunk == 0)
    def _():
        prev_scr[...] = jnp.zeros(prev_scr.shape, BF16)
        st_scr[...] = jnp.zeros(st_scr.shape, F32)

    dt_c = jax.nn.softplus(dtc_ref[...] + dtb_c_ref[...])
    adt_c = dt_c * (-jnp.exp(alog_c_ref[...]))
    dt_r = jax.nn.softplus(dtr_ref[...] + dtb_r_ref[...])
    adt_r = dt_r * (-jnp.exp(alog_r_ref[...]))
    row = lax.broadcasted_iota(I32, (L, L), 0)
    col = lax.broadcasted_iota(I32, (L, L), 1)
    causal = row >= col
    tri = jnp.where(causal, 1.0, 0.0).astype(BF16)
    tri_t = jnp.where(col >= row, 1.0, 0.0).astype(BF16)
    acum_c = _dot_split_rhs(tri, adt_c)
    acum_r_scr[...] = _dot_split_lhs(adt_r, tri_t)
    a_last = acum_c[L - 1:L, :]
    fac = jnp.concatenate([dt_c, jnp.exp(acum_c), jnp.exp(a_last - acum_c),
                           jnp.broadcast_to(jnp.exp(a_last), (SSD_FAC_PAD, LANES))], axis=0)
    fac_hi = fac.astype(BF16)
    fac_scr[0] = fac_hi
    fac_scr[1] = (fac - fac_hi.astype(F32)).astype(BF16)
    lane_w = lax.broadcasted_iota(I32, (L, LANES), 1)
    first_half = lane_w < SSD_HEAD_DIM

    def tiles(ref, first, n):
        return jnp.concatenate([ref[first + q] for q in range(n)], axis=1)

    def group_body(g, carry):
        e_chan = echan_ref[g]
        ex = _dot(fac_scr[0], e_chan) + _dot(fac_scr[1], e_chan)
        dt_x, ea_x, sd_x, cd_x = ex[0:L], ex[L:2 * L], ex[2 * L:3 * L], ex[3 * L:3 * L + 1]

        def conv_silu(zx_first, conv_first, n):
            cur = tiles(zx_ref, zx_first, n)
            ext = jnp.concatenate([tiles(prev_scr, conv_first, n), cur], axis=0)
            w = tiles(cw_ref, conv_first, n)
            acc = jnp.broadcast_to(tiles(cb_ref, conv_first, n), (L, n * LANES))
            for k in range(SSD_CONV):
                back = SSD_CONV - 1 - k
                xk = cur.astype(F32) if back == 0 else _dot(shift_ref[back - 1], ext)
                acc = acc + xk * w[k:k + 1, :]
            for q in range(n):
                prev_scr[conv_first + q] = zx_ref[zx_first + q]
            return jax.nn.silu(acc)

        xs = conv_silu(x0 + n_xt * g, n_xt * g, n_xt)
        bm = conv_silu(b0 + g, cb0 + g, 1)
        cm = conv_silu(c0 + g, cc0 + g, 1)

        xdt = xs * dt_x
        cb = jnp.where(causal, _dot_nt(cm.astype(BF16), bm.astype(BF16)), 0.0)
        y_parts = []
        for pair in range(hpg // 2):
            xd = xdt[:, pair * LANES:(pair + 1) * LANES]
            m_pair = []
            for sub in range(2):
                a_row = jnp.broadcast_to(acum_r_scr[pl.ds(g * hpg + 2 * pair + sub, 1), :], (L, L))
                seg = jnp.minimum(a_row.T - a_row, 0.0)
                m_pair.append((cb * jnp.exp(seg)).astype(BF16))
            x_pair = jnp.concatenate([jnp.where(first_half, xd, 0.0), jnp.where(first_half, 0.0, xd)], axis=0)
            y_parts.append(_dot(jnp.concatenate(m_pair, axis=1), x_pair.astype(BF16)))
        y_diag = jnp.concatenate(y_parts, axis=1)

        st = st_scr[g]
        y_off = _dot(cm.astype(BF16), st.astype(BF16)) * ea_x
        st_scr[g] = st * cd_x + _dot_tn(bm.astype(BF16), (xdt * sd_x).astype(BF16))

        y = y_diag + y_off + xs * tiles(dskip_ref, n_xt * g, n_xt)
        y = y * jax.nn.silu(tiles(zx_ref, n_xt * g, n_xt).astype(F32))
        y = y * lax.rsqrt(jnp.mean(y * y, axis=-1, keepdims=True) + NORM_EPS)
        y = y * tiles(ng_ref, n_xt * g, n_xt)
        for q in range(n_xt):
            o_ref[n_xt * g + q] = y[:, q * LANES:(q + 1) * LANES].astype(BF16)
        return carry

    lax.fori_loop(0, SSD_GROUPS, group_body, 0)


def _ssd_chunks(zx_tiles, dt_small, dt_small_t, conv_w, conv_b, dt_bias, a_log, d_skip, norm_g):
    L = SSD_CHUNK
    nc = SEQ // L
    n_zx = SSD_MAIN // LANES
    n_conv = SSD_CONV_CH // LANES
    n_inner = SSD_D_INNER // LANES
    hpg = SSD_HEADS_PER_GROUP

    def pad_heads(v):
        return jnp.zeros((LANES,), F32).at[:SSD_HEADS].set(v)

    dtb = pad_heads(dt_bias)
    alog = pad_heads(a_log)
    cw = conv_w.reshape(SSD_CONV, n_conv, LANES).transpose(1, 0, 2)
    cb = conv_b.reshape(n_conv, 1, LANES)
    d_chan = jnp.repeat(d_skip, SSD_HEAD_DIM).reshape(n_inner, 1, LANES)
    ng = norm_g.reshape(n_inner, 1, LANES)
    t_idx = jnp.arange(L, dtype=I32)[None, :, None]
    r_idx = jnp.arange(2 * L, dtype=I32)[None, None, :]
    back = jnp.arange(1, SSD_CONV, dtype=I32)[:, None, None]
    shift = (r_idx == L + t_idx - back).astype(BF16)
    head = jnp.arange(LANES, dtype=I32)[None, :, None]
    grp = jnp.arange(SSD_GROUPS, dtype=I32)[:, None, None]
    e_chan = (head == grp * hpg + jnp.arange(SSD_GROUP_W, dtype=I32)[None, None, :] // SSD_HEAD_DIM).astype(BF16)
    row = lambda b, c: b * nc + c
    const3 = lambda b, c: (0, 0, 0)
    const2 = lambda b, c: (0, 0)
    in_specs = [
        pl.BlockSpec((n_zx, L, LANES), lambda b, c: (0, row(b, c), 0)),
        pl.BlockSpec((L, LANES), lambda b, c: (row(b, c), 0)),
        pl.BlockSpec((LANES, L), lambda b, c: (0, row(b, c))),
        pl.BlockSpec((n_conv, SSD_CONV, LANES), const3),
        pl.BlockSpec((n_conv, 1, LANES), const3),
        pl.BlockSpec((1, LANES), const2), pl.BlockSpec((1, LANES), const2),
        pl.BlockSpec((LANES, 1), const2), pl.BlockSpec((LANES, 1), const2),
        pl.BlockSpec((n_inner, 1, LANES), const3),
        pl.BlockSpec((n_inner, 1, LANES), const3),
        pl.BlockSpec((SSD_CONV - 1, L, 2 * L), const3),
        pl.BlockSpec((SSD_GROUPS, LANES, SSD_GROUP_W), const3),
    ]
    return pl.pallas_call(
        _ssd_chunk_kernel, name="ssd_chunks", grid=(BATCH, nc),
        in_specs=in_specs,
        out_specs=pl.BlockSpec((n_inner, L, LANES), lambda b, c: (0, row(b, c), 0)),
        out_shape=jax.ShapeDtypeStruct((n_inner, TOKENS, LANES), BF16),
        scratch_shapes=[pltpu.VMEM((n_conv, L, LANES), BF16),
                        pltpu.VMEM((LANES, L), F32),
                        pltpu.VMEM((SSD_GROUPS, SSD_D_STATE, SSD_GROUP_W), F32),
                        pltpu.VMEM((2, 3 * L + SSD_FAC_PAD, LANES), BF16)],
        compiler_params=_cparams(("arbitrary", "arbitrary")),
    )(zx_tiles, dt_small, dt_small_t, cw, cb, dtb.reshape(1, LANES), alog.reshape(1, LANES),
      dtb.reshape(LANES, 1), alog.reshape(LANES, 1), d_chan, ng, shift, e_chan)


def _gather_rows(src_hbm, idx_ref, base, dst, sem, n_rows):
    def body(b, carry):
        for u in range(GATHER_UNROLL):
            r = b * GATHER_UNROLL + u
            tok = idx_ref[base + r]
            pltpu.make_async_copy(src_hbm.at[pl.ds(tok, 1), :], dst.at[pl.ds(r, 1), :], sem).start()
        return carry

    lax.fori_loop(0, n_rows // GATHER_UNROLL, body, 0)


def _moe_ffn_kernel(te_ref, pos_ref, nact_ref, pad_ref, wp_ref, x_hbm, g_ref, wg_hbm, wu_hbm, wd_hbm, y_ref,
                    tok_ref, buf, sem, wg_f, wu_f, wd_f, wsem, wg_bf, wu_bf, wd_bf, *, layer):
    i = pl.program_id(0)
    n_act = nact_ref[0]
    tm = MOE_TM
    slot = i % 2
    run_start = wp_ref[i] == 1
    w_slot = wp_ref[MOE_TILES + i]
    next_expert = wp_ref[2 * MOE_TILES + i]
    n_rows = pl.multiple_of(wp_ref[3 * MOE_TILES + i], GATHER_UNROLL)

    def gather(tile, s):
        _gather_rows(x_hbm, tok_ref, tile * tm, buf.at[s], sem.at[s], wp_ref[3 * MOE_TILES + tile])

    def weight_copies(e, s):
        return (pltpu.make_async_copy(wg_hbm.at[layer, e], wg_f.at[s], wsem.at[s, 0]),
                pltpu.make_async_copy(wu_hbm.at[layer, e], wu_f.at[s], wsem.at[s, 1]),
                pltpu.make_async_copy(wd_hbm.at[layer, e], wd_f.at[s], wsem.at[s, 2]))

    @pl.when(i == 0)
    def _():
        for c in weight_copies(te_ref[0], 0):
            c.start()
        buf[...] = jnp.zeros(buf.shape, F32)
        def clear(q, carry):
            tok_ref[q] = 0
            return carry

        def clear_padding(e, carry):
            lax.fori_loop(pad_ref[e], pad_ref[MOE_EXPERTS + e], clear, 0)
            return carry

        lax.fori_loop(0, MOE_EXPERTS, clear_padding, 0)
        for k in range(MOE_TOPK):
            def place(t, carry, k=k):
                tok_ref[pos_ref[k * TOKENS + t]] = t
                return carry

            lax.fori_loop(0, TOKENS, place, 0, unroll=GATHER_UNROLL)
        gather(0, 0)

    @pl.when(run_start)
    def _():
        for c in weight_copies(te_ref[i], w_slot):
            c.wait()

        @pl.when(next_expert >= 0)
        def _():
            for c in weight_copies(next_expert, 1 - w_slot):
                c.start()

        wg_bf[...] = wg_f[w_slot].astype(BF16)
        wu_bf[...] = wu_f[w_slot].astype(BF16)
        wd_bf[...] = wd_f[w_slot].astype(BF16)

    @pl.when(i >= n_act)
    def _():
        y_ref[...] = jnp.zeros(y_ref.shape, F32)

    @pl.when(i < n_act)
    def _():
        pltpu.make_async_copy(x_hbm.at[pl.ds(0, n_rows), :], buf.at[slot, pl.ds(0, n_rows), :],
                              sem.at[slot]).wait()

        @pl.when(i + 1 < n_act)
        def _():
            gather(i + 1, 1 - slot)

        h = _rms(buf[slot], g_ref[...]).astype(BF16)
        act = jax.nn.silu(_dot(h, wg_bf[...])) * _dot(h, wu_bf[...])
        y_ref[...] = _dot(act.astype(BF16), wd_bf[...])


def _moe_ffn(x, g, w_gate, w_up, w_down, layer, tile_expert, pos_kmajor, n_active, pad_rows, weight_plan):
    any_spec = pl.BlockSpec(memory_space=pl.ANY)
    grid_spec = pltpu.PrefetchScalarGridSpec(
        num_scalar_prefetch=5, grid=(MOE_TILES,),
        in_specs=[any_spec, pl.BlockSpec((1, D_MODEL), lambda i, *_: (0, 0)), any_spec, any_spec, any_spec],
        out_specs=pl.BlockSpec((MOE_TM, D_MODEL), lambda i, *_: (i, 0)),
        scratch_shapes=[pltpu.SMEM((MOE_ROWS,), I32),
                        pltpu.VMEM((2, MOE_TM, D_MODEL), F32), pltpu.SemaphoreType.DMA((2,)),
                        pltpu.VMEM((2, D_MODEL, MOE_D_FF), F32), pltpu.VMEM((2, D_MODEL, MOE_D_FF), F32),
                        pltpu.VMEM((2, MOE_D_FF, D_MODEL), F32), pltpu.SemaphoreType.DMA((2, 3)),
                        pltpu.VMEM((D_MODEL, MOE_D_FF), BF16), pltpu.VMEM((D_MODEL, MOE_D_FF), BF16),
                        pltpu.VMEM((MOE_D_FF, D_MODEL), BF16)])
    return pl.pallas_call(
        functools.partial(_moe_ffn_kernel, layer=layer), name="moe_ffn", grid_spec=grid_spec,
        out_shape=jax.ShapeDtypeStruct((MOE_ROWS, D_MODEL), F32),
        compiler_params=_cparams(("arbitrary",)),
    )(tile_expert, pos_kmajor, n_active, pad_rows, weight_plan, x, g.reshape(1, D_MODEL), w_gate, w_up, w_down)


def _moe_combine_kernel(pos_ref, x_ref, cw_ref, g_ref, y_hbm, o_ref, buf, sem, *, final_norm):
    i = pl.program_id(0)
    n = pl.num_programs(0)
    tm = CMB_TM
    slot = i % 2

    def issue(tile, s):
        for k in range(MOE_TOPK):
            _gather_rows(y_hbm, pos_ref, (k * (TOKENS // tm) + tile) * tm, buf.at[s, k], sem.at[s], tm)

    @pl.when(i == 0)
    def _():
        issue(0, 0)

    @pl.when(i + 1 < n)
    def _():
        issue(i + 1, 1 - slot)

    for k in range(MOE_TOPK):
        pltpu.make_async_copy(y_hbm.at[pl.ds(0, tm), :], buf.at[slot, k], sem.at[slot]).wait()
    cw = cw_ref[...]
    out = x_ref[...] + cw[:, 0:1] * buf[slot, 0] + cw[:, 1:2] * buf[slot, 1]
    if final_norm:
        out = _rms(out, g_ref[...])
    o_ref[...] = out


def _moe_combine(x, cw, y_sorted, pos_kmajor, g_final, final_norm):
    grid_spec = pltpu.PrefetchScalarGridSpec(
        num_scalar_prefetch=1, grid=(TOKENS // CMB_TM,),
        in_specs=[pl.BlockSpec((CMB_TM, D_MODEL), lambda i, pos: (i, 0)),
                  pl.BlockSpec((CMB_TM, LANES), lambda i, pos: (i, 0)),
                  pl.BlockSpec((1, D_MODEL), lambda i, pos: (0, 0)),
                  pl.BlockSpec(memory_space=pl.ANY)],
        out_specs=pl.BlockSpec((CMB_TM, D_MODEL), lambda i, pos: (i, 0)),
        scratch_shapes=[pltpu.VMEM((2, MOE_TOPK, CMB_TM, D_MODEL), F32), pltpu.SemaphoreType.DMA((2,))])
    return pl.pallas_call(
        functools.partial(_moe_combine_kernel, final_norm=final_norm), name="moe_combine", grid_spec=grid_spec,
        out_shape=jax.ShapeDtypeStruct((TOKENS, D_MODEL), F32),
        compiler_params=_cparams(("arbitrary",)),
    )(pos_kmajor, x, cw, g_final.reshape(1, D_MODEL), y_sorted)


def _moe_plan(eid):
    e = eid[:, :MOE_TOPK].reshape(-1)
    onehot = (e[:, None] == jnp.arange(MOE_EXPERTS, dtype=I32)[None, :]).astype(I32)
    csum = jnp.cumsum(onehot, axis=0)
    counts = csum[-1]
    rank = jnp.take_along_axis(csum, e[:, None], axis=1)[:, 0] - 1
    padded = ((counts + MOE_TM - 1) // MOE_TM) * MOE_TM
    g_end = jnp.cumsum(padded)
    g_start = g_end - padded
    pos = g_start[e] + rank
    n_active = (g_end[-1] // MOE_TM).astype(I32)
    tile_start = jnp.arange(MOE_TILES, dtype=I32) * MOE_TM
    te = jnp.sum((g_end[None, :] <= tile_start[:, None]).astype(I32), axis=1)
    last = jnp.max(jnp.where(counts > 0, jnp.arange(MOE_EXPERTS, dtype=I32), 0))
    tile_expert = jnp.minimum(te, last)
    pos_kmajor = pos.reshape(TOKENS, MOE_TOPK).T.reshape(-1)
    pad_rows = jnp.concatenate([g_start + counts, g_end])
    experts = jnp.arange(MOE_EXPERTS, dtype=I32)
    run_start = jnp.concatenate([jnp.ones((1,), I32), (tile_expert[1:] != tile_expert[:-1]).astype(I32)])
    w_slot = (jnp.cumsum(run_start) - 1) % 2
    later = (experts[None, :] > experts[:, None]) & (counts > 0)[None, :]
    next_nonempty = jnp.min(jnp.where(later, experts[None, :], MOE_EXPERTS), axis=1)
    next_nonempty = jnp.where(next_nonempty == MOE_EXPERTS, -1, next_nonempty)
    real_rows = jnp.clip((g_start + counts)[tile_expert] - tile_start, 0, MOE_TM)
    real_rows = jnp.where(tile_start < g_end[-1], real_rows, 0)
    gather_rows = jnp.minimum(-(-real_rows // GATHER_UNROLL) * GATHER_UNROLL, MOE_TM)
    weight_plan = jnp.concatenate([run_start, w_slot, next_nonempty[tile_expert], gather_rows]).astype(I32)
    return tile_expert, n_active.reshape(1), pos_kmajor, pad_rows, weight_plan


def _hier_moe_add(x, ln_g, w_group, b_group, w_expert, b_expert, w_gate, w_up, w_down, layer,
                  g_final, final_norm):
    eid, cw = _router(x, ln_g, w_group, b_group, w_expert, b_expert)
    tile_expert, n_active, pos_kmajor, pad_rows, weight_plan = _moe_plan(eid)
    y_sorted = _moe_ffn(x, ln_g, w_gate, w_up, w_down, layer, tile_expert, pos_kmajor, n_active, pad_rows,
                        weight_plan)
    return _moe_combine(x, cw, y_sorted, pos_kmajor, g_final, final_norm)


def _rope_tables():
    pos = jnp.arange(SEQ, dtype=F32)
    inv = 1.0 / (ROPE_THETA ** (jnp.arange(0, NSA_HEAD_DIM, 2, dtype=F32) / NSA_HEAD_DIM))
    ang = pos[:, None] * inv[None, :]
    cos, sin = jnp.cos(ang), jnp.sin(ang)
    return jnp.concatenate([cos, cos], axis=1), jnp.concatenate([-sin, sin], axis=1)


def _nsa_mixer_add(x, ln_g, w_in, cmp_pe, cmp_w1, cmp_w2, w_out):
    w_in_t = w_in.T
    hn, _, g_lin_t = _norm_small(x, ln_g, w_in_t[NSA_MAIN:].T, transposed=True)
    cos_full, sin_signed = _rope_tables()
    heads = _nsa_inproj(hn, w_in_t, cos_full, sin_signed)
    first_c = NSA_HEADS
    kv_c = heads[first_c:first_c + 2 * NSA_KV_HEADS]
    kv_chunks = kv_c.reshape(2, NSA_KV_HEADS, BATCH, SEQ // CMP_STRIDE, CMP_STRIDE * NSA_HEAD_DIM)
    kc_vc = _compress(kv_chunks, cmp_pe, cmp_w1, cmp_w2)
    gates_t = g_lin_t[:NSA_GATES].reshape(NSA_KV_HEADS, 3 * NSA_Q_PER_KV, TOKENS)
    o = _nsa_attention(heads, kc_vc, gates_t)
    return _outproj_resid(o, w_out, x)


def _ssd_mixer_add(x, ln_g, w_in, conv_w, conv_b, dt_bias, a_log, d_skip, norm_g, w_out):
    w_in_t = w_in.T
    hn, dt_small, dt_small_t = _norm_small(x, ln_g, w_in_t[SSD_MAIN:].T, transposed=True)
    zx_tiles = _matmul_tiles(hn, w_in_t, SSD_MAIN, 1024)
    y_tiles = _ssd_chunks(zx_tiles, dt_small, dt_small_t, conv_w, conv_b, dt_bias, a_log, d_skip, norm_g)
    return _outproj_resid(y_tiles, w_out, x)


def kernel(x, ln_mix, ln_ffn, ln_final, nsa_w_in, nsa_cmp_pe, nsa_cmp_w1, nsa_cmp_w2, nsa_w_out,
           ssd_w_in, ssd_conv_w, ssd_conv_b, ssd_dt_bias, ssd_a_log, ssd_d, ssd_norm, ssd_w_out,
           moe_w_group, moe_b_group, moe_w_expert, moe_b_expert, moe_w_gate, moe_w_up, moe_w_down):
    h = x.reshape(TOKENS, D_MODEL)
    for i in range(DEPTH):
        j = i // N_MIXERS
        if i % N_MIXERS == 0:
            h = _nsa_mixer_add(h, ln_mix[i], nsa_w_in[j], nsa_cmp_pe[j], nsa_cmp_w1[j], nsa_cmp_w2[j],
                               nsa_w_out[j])
        else:
            h = _ssd_mixer_add(h, ln_mix[i], ssd_w_in[j], ssd_conv_w[j], ssd_conv_b[j], ssd_dt_bias[j],
                               ssd_a_log[j], ssd_d[j], ssd_norm[j], ssd_w_out[j])
        h = _hier_moe_add(h, ln_ffn[i], moe_w_group[i], moe_b_group[i], moe_w_expert[i], moe_b_expert[i],
                          moe_w_gate, moe_w_up, moe_w_down, i, ln_final, i == DEPTH - 1)
    return h.reshape(BATCH, SEQ, D_MODEL)
```

```python
import functools

import jax
import jax.numpy as jnp
from jax import lax
from jax.experimental import pallas as pl
from jax.experimental.pallas import tpu as pltpu

F32 = jnp.float32
BF16 = jnp.bfloat16
I32 = jnp.int32

D_MODEL = 2048
BATCH = 4
SEQ = 2048
TOKENS = BATCH * SEQ
DEPTH = 2
N_MIXERS = 2
NORM_EPS = 1e-6
NEG_INF = -1e30
LOG2_E = 1.4426950408889634
ROPE_THETA = 10000.0

NSA_HEADS = 16
NSA_KV_HEADS = 4
NSA_HEAD_DIM = D_MODEL // NSA_HEADS
NSA_Q_PER_KV = NSA_HEADS // NSA_KV_HEADS
CMP_BLOCK = 32
CMP_STRIDE = 16
CMP_HIDDEN = 256
N_CMP = (SEQ - CMP_BLOCK) // CMP_STRIDE + 1
SLC_BLOCK = 64
SLC_TOPK = 16
SLC_LOCAL = 2
SLC_FORCE = 1e4
N_SLC = SEQ // SLC_BLOCK
WINDOW = 512
NSA_QD = NSA_HEADS * NSA_HEAD_DIM
NSA_KVD = NSA_KV_HEADS * NSA_HEAD_DIM
NSA_MAIN = NSA_QD + 6 * NSA_KVD
NSA_GATES = 3 * NSA_HEADS

SSD_D_INNER = 2 * D_MODEL
SSD_HEAD_DIM = 64
SSD_HEADS = SSD_D_INNER // SSD_HEAD_DIM
SSD_GROUPS = 8
SSD_HEADS_PER_GROUP = SSD_HEADS // SSD_GROUPS
SSD_D_STATE = 128
SSD_CONV = 4
SSD_CHUNK = 128
SSD_GROUP_W = SSD_D_INNER // SSD_GROUPS
SSD_BC = SSD_GROUPS * SSD_D_STATE
SSD_CONV_CH = SSD_D_INNER + 2 * SSD_BC
SSD_MAIN = SSD_D_INNER + SSD_CONV_CH
SSD_FAC_PAD = 16

MOE_GROUPS = 4
MOE_EPG = 8
MOE_EXPERTS = MOE_GROUPS * MOE_EPG
MOE_TOPK = 2
MOE_D_FF = 512

LANES = 128
VMEM_LIMIT = 56 * 1024 * 1024

NORM_TM = 256
MM_TM = 512
OUT_TM = 256
ATT_TQ = 256
ATT_TK = 256
ATT_CHUNK_TILES = 2
ATT_ONES = 16
MOE_TM = 256
MOE_TILES = (TOKENS * MOE_TOPK) // MOE_TM + MOE_EXPERTS
MOE_ROWS = MOE_TILES * MOE_TM
CMB_TM = 128
GATHER_UNROLL = 8


def _cparams(sem):
    return pltpu.CompilerParams(dimension_semantics=sem, vmem_limit_bytes=VMEM_LIMIT)


def _split3(x):
    hi = x.astype(BF16)
    r1 = x - hi.astype(F32)
    mid = r1.astype(BF16)
    lo = (r1 - mid.astype(F32)).astype(BF16)
    return hi, mid, lo


def _dot(a, b):
    return jnp.dot(a, b, preferred_element_type=F32)


def _dot_nt(a, b):
    return lax.dot_general(a, b, (((1,), (1,)), ((), ())), preferred_element_type=F32)


def _dot_tn(a, b):
    return lax.dot_general(a, b, (((0,), (0,)), ((), ())), preferred_element_type=F32)


def _dot_split_lhs(x, m_bf16):
    hi, mid, lo = _split3(x)
    return _dot(hi, m_bf16) + _dot(mid, m_bf16) + _dot(lo, m_bf16)


def _dot_split_rhs(m_bf16, x):
    hi, mid, lo = _split3(x)
    return _dot(m_bf16, hi) + _dot(m_bf16, mid) + _dot(m_bf16, lo)


def _dot_x3(a, w):
    a_hi = a.astype(BF16)
    a_lo = (a - a_hi.astype(F32)).astype(BF16)
    w_hi = w.astype(BF16)
    w_lo = (w - w_hi.astype(F32)).astype(BF16)
    return _dot(a_hi, w_hi) + _dot(a_hi, w_lo) + _dot(a_lo, w_hi)


def _dot_x3_nt(a, w):
    a_hi = a.astype(BF16)
    a_lo = (a - a_hi.astype(F32)).astype(BF16)
    w_hi = w.astype(BF16)
    w_lo = (w - w_hi.astype(F32)).astype(BF16)
    return _dot_nt(a_hi, w_hi) + _dot_nt(a_hi, w_lo) + _dot_nt(a_lo, w_hi)


def _rms(x, g):
    y = x * lax.rsqrt(jnp.mean(x * x, axis=-1, keepdims=True) + NORM_EPS)
    return y * g


def _norm_small_kernel(x_ref, g_ref, ws_ref, hn_ref, small_ref):
    y = _rms(x_ref[...], g_ref[...])
    hn_ref[...] = y.astype(BF16)
    small_ref[...] = _dot_x3(y, ws_ref[...])


def _norm_small_t_kernel(x_ref, g_ref, ws_ref, hn_ref, small_ref, small_t_ref):
    y = _rms(x_ref[...], g_ref[...])
    hn_ref[...] = y.astype(BF16)
    small = _dot_x3(y, ws_ref[...])
    small_ref[...] = small
    small_t_ref[...] = small.T


def _norm_small(x, g, w_small, transposed=False):
    n = w_small.shape[1]
    ws = jnp.zeros((D_MODEL, LANES), F32).at[:, :n].set(w_small)
    grid = (TOKENS // NORM_TM,)
    x_spec = pl.BlockSpec((NORM_TM, D_MODEL), lambda i: (i, 0))
    g_spec = pl.BlockSpec((1, D_MODEL), lambda i: (0, 0))
    w_spec = pl.BlockSpec((D_MODEL, LANES), lambda i: (0, 0))
    hn_spec = pl.BlockSpec((NORM_TM, D_MODEL), lambda i: (i, 0))
    sm_spec = pl.BlockSpec((NORM_TM, LANES), lambda i: (i, 0))
    hn_shape = jax.ShapeDtypeStruct((TOKENS, D_MODEL), BF16)
    sm_shape = jax.ShapeDtypeStruct((TOKENS, LANES), F32)
    if not transposed:
        return pl.pallas_call(
            _norm_small_kernel, name="norm_small", grid=grid,
            in_specs=[x_spec, g_spec, w_spec],
            out_specs=[hn_spec, sm_spec],
            out_shape=[hn_shape, sm_shape],
            compiler_params=_cparams(("parallel",)),
        )(x, g.reshape(1, D_MODEL), ws)
    smt_spec = pl.BlockSpec((LANES, NORM_TM), lambda i: (0, i))
    smt_shape = jax.ShapeDtypeStruct((LANES, TOKENS), F32)
    return pl.pallas_call(
        _norm_small_t_kernel, name="norm_small_t", grid=grid,
        in_specs=[x_spec, g_spec, w_spec],
        out_specs=[hn_spec, sm_spec, smt_spec],
        out_shape=[hn_shape, sm_shape, smt_shape],
        compiler_params=_cparams(("parallel",)),
    )(x, g.reshape(1, D_MODEL), ws)


def _route(x, g, ws, bias):
    y = _rms(x, g)
    logits = _dot_x3(y, ws) + bias
    lane = lax.broadcasted_iota(I32, logits.shape, 1)
    big = jnp.int32(LANES)
    neg = -jnp.inf
    gl = jnp.where(lane < MOE_GROUPS, logits, neg)
    gmax = jnp.max(gl, axis=-1, keepdims=True)
    gsum = jnp.sum(jnp.exp(gl - gmax), axis=-1, keepdims=True)
    g_w = 1.0 / gsum
    g_sel = jnp.min(jnp.where(gl == gmax, lane, big), axis=-1, keepdims=True)
    lo = MOE_GROUPS + g_sel * MOE_EPG
    el = jnp.where((lane >= lo) & (lane < lo + MOE_EPG), logits, neg)
    v1 = jnp.max(el, axis=-1, keepdims=True)
    i1 = jnp.min(jnp.where(el == v1, lane, big), axis=-1, keepdims=True)
    el2 = jnp.where(lane == i1, neg, el)
    v2 = jnp.max(el2, axis=-1, keepdims=True)
    i2 = jnp.min(jnp.where(el2 == v2, lane, big), axis=-1, keepdims=True)
    e2 = jnp.exp(v2 - v1)
    den = 1.0 + e2
    w1 = (1.0 / den) * g_w
    w2 = (e2 / den) * g_w
    eid = jnp.where(lane == 0, i1 - MOE_GROUPS, jnp.where(lane == 1, i2 - MOE_GROUPS, 0))
    cw = jnp.where(lane == 0, w1, jnp.where(lane == 1, w2, 0.0))
    return eid, cw


def _inproj_heads_kernel(a_ref, w_ref, cos_ref, sin_ref, o_ref, wbf_ref):
    j = pl.program_id(0)

    @pl.when(pl.program_id(1) == 0)
    def _():
        wbf_ref[...] = w_ref[...].astype(BF16)

    acc = _dot_nt(a_ref[...], wbf_ref[...])
    q_tiles = NSA_QD // acc.shape[1]
    c = cos_ref[...]
    s = sin_ref[...]

    def head(h):
        return acc[:, h * NSA_HEAD_DIM:(h + 1) * NSA_HEAD_DIM]

    def rotary(xh):
        return (xh * c + pltpu.roll(xh, NSA_HEAD_DIM // 2, 1) * s).astype(BF16)

    for h in range(NSA_KV_HEADS):
        o_ref[h] = rotary(head(h))

    @pl.when(j < q_tiles)
    def _():
        for h in range(NSA_KV_HEADS, 2 * NSA_KV_HEADS):
            o_ref[h] = rotary(head(h))

    @pl.when(j >= q_tiles)
    def _():
        for h in range(NSA_KV_HEADS, 2 * NSA_KV_HEADS):
            o_ref[h] = head(h).astype(BF16)


def _nsa_inproj(hn, w_in_t, cos_full, sin_signed):
    tn = 2 * NSA_KVD
    n_heads_tile = tn // NSA_HEAD_DIM
    s_tiles = SEQ // MM_TM
    return pl.pallas_call(
        _inproj_heads_kernel, name="nsa_inproj", grid=(NSA_MAIN // tn, TOKENS // MM_TM),
        in_specs=[pl.BlockSpec((MM_TM, D_MODEL), lambda j, i: (i, 0)),
                  pl.BlockSpec((tn, D_MODEL), lambda j, i: (j, 0)),
                  pl.BlockSpec((MM_TM, NSA_HEAD_DIM), lambda j, i: (i % s_tiles, 0)),
                  pl.BlockSpec((MM_TM, NSA_HEAD_DIM), lambda j, i: (i % s_tiles, 0))],
        out_specs=pl.BlockSpec((n_heads_tile, MM_TM, NSA_HEAD_DIM), lambda j, i: (j, i, 0)),
        out_shape=jax.ShapeDtypeStruct((NSA_MAIN // NSA_HEAD_DIM, TOKENS, NSA_HEAD_DIM), BF16),
        scratch_shapes=[pltpu.VMEM((tn, D_MODEL), BF16)],
        compiler_params=_cparams(("arbitrary", "arbitrary")),
    )(hn, w_in_t, cos_full, sin_signed)


def _matmul_tiles_kernel(a_ref, w_ref, o_ref, wbf_ref):
    @pl.when(pl.program_id(1) == 0)
    def _():
        wbf_ref[...] = w_ref[...].astype(BF16)

    acc = _dot_nt(a_ref[...], wbf_ref[...])
    for t in range(acc.shape[1] // LANES):
        o_ref[t] = acc[:, t * LANES:(t + 1) * LANES].astype(o_ref.dtype)


def _matmul_tiles(a, w_t, n_cols, tn):
    k = a.shape[1]
    return pl.pallas_call(
        _matmul_tiles_kernel, name="matmul_tiles", grid=(n_cols // tn, TOKENS // MM_TM),
        in_specs=[pl.BlockSpec((MM_TM, k), lambda j, i: (i, 0)),
                  pl.BlockSpec((tn, k), lambda j, i: (j, 0))],
        out_specs=pl.BlockSpec((tn // LANES, MM_TM, LANES), lambda j, i: (j, i, 0)),
        out_shape=jax.ShapeDtypeStruct((n_cols // LANES, TOKENS, LANES), BF16),
        scratch_shapes=[pltpu.VMEM((tn, k), BF16)],
        compiler_params=_cparams(("arbitrary", "arbitrary")),
    )(a, w_t)


def _outproj_route_kernel(a_ref, w_ref, r_ref, g_ref, ws_ref, b_ref, o_ref, eid_ref, cw_ref):
    if len(a_ref.shape) == 3:
        a = jnp.concatenate([a_ref[t] for t in range(a_ref.shape[0])], axis=1)
    else:
        a = a_ref[...]
    x_new = r_ref[...] + _dot(a, w_ref[...])
    o_ref[...] = x_new
    eid_ref[...], cw_ref[...] = _route(x_new, g_ref[...], ws_ref[...], b_ref[...])


def _outproj_route(a, w, resid, ln_ffn_g, w_group, b_group, w_expert, b_expert):
    tiled = a.ndim == 3
    k, n = w.shape
    n_r = MOE_GROUPS + MOE_EXPERTS
    ws = jnp.zeros((D_MODEL, LANES), F32).at[:, :n_r].set(jnp.concatenate([w_group, w_expert], axis=1))
    bs = jnp.zeros((1, LANES), F32).at[0, :n_r].set(jnp.concatenate([b_group, b_expert]))
    a_spec = (pl.BlockSpec((k // LANES, OUT_TM, LANES), lambda i: (0, i, 0)) if tiled
              else pl.BlockSpec((OUT_TM, k), lambda i: (i, 0)))
    small = pl.BlockSpec((OUT_TM, LANES), lambda i: (i, 0))
    return pl.pallas_call(
        _outproj_route_kernel, name="outproj_route", grid=(TOKENS // OUT_TM,),
        in_specs=[a_spec,
                  pl.BlockSpec((k, n), lambda i: (0, 0)),
                  pl.BlockSpec((OUT_TM, n), lambda i: (i, 0)),
                  pl.BlockSpec((1, D_MODEL), lambda i: (0, 0)),
                  pl.BlockSpec((D_MODEL, LANES), lambda i: (0, 0)),
                  pl.BlockSpec((1, LANES), lambda i: (0, 0))],
        out_specs=[pl.BlockSpec((OUT_TM, n), lambda i: (i, 0)), small, small],
        out_shape=[jax.ShapeDtypeStruct((TOKENS, n), F32),
                   jax.ShapeDtypeStruct((TOKENS, LANES), I32),
                   jax.ShapeDtypeStruct((TOKENS, LANES), F32)],
        compiler_params=_cparams(("parallel",)),
    )(a, w.astype(BF16), resid, ln_ffn_g.reshape(1, D_MODEL), ws, bs)


def _compress_kernel(x_ref, pe_ref, w1_ref, w2_ref, o_ref):
    half = CMP_STRIDE * NSA_HEAD_DIM
    x = x_ref[0, 0, 0]
    w1 = w1_ref[0].astype(BF16)
    top = _dot(x, w1[:half])
    bot = _dot(x, w1[half:])
    pe = jnp.broadcast_to(pe_ref[0], (8, 2 * half)).astype(BF16)
    pe_bias = _dot(pe, w1)[0:1]
    hid = top + pltpu.roll(bot, bot.shape[0] - 1, 0) + pe_bias
    act = jax.nn.gelu(hid)
    o_ref[0, 0, 0] = _dot(act.astype(BF16), w2_ref[0].astype(BF16))


def _compress(kv_chunks, pe, w1, w2):
    n_chunk = SEQ // CMP_STRIDE
    feat = CMP_STRIDE * NSA_HEAD_DIM
    return pl.pallas_call(
        _compress_kernel, name="nsa_compress", grid=(2, NSA_KV_HEADS, BATCH),
        in_specs=[pl.BlockSpec((1, 1, 1, n_chunk, feat), lambda a, g, b: (a, g, b, 0, 0)),
                  pl.BlockSpec((1, 1, 2 * feat), lambda a, g, b: (a, 0, 0)),
                  pl.BlockSpec((1, 2 * feat, CMP_HIDDEN), lambda a, g, b: (a, 0, 0)),
                  pl.BlockSpec((1, CMP_HIDDEN, NSA_HEAD_DIM), lambda a, g, b: (a, 0, 0))],
        out_specs=pl.BlockSpec((1, 1, 1, n_chunk, NSA_HEAD_DIM), lambda a, g, b: (a, g, b, 0, 0)),
        out_shape=jax.ShapeDtypeStruct((2, NSA_KV_HEADS, BATCH, n_chunk, NSA_HEAD_DIM), F32),
        compiler_params=_cparams(("parallel", "parallel", "parallel")),
    )(kv_chunks, pe.reshape(2, 1, 2 * feat), w1, w2)


def _nsa_attn_kernel(q_ref, ks_ref, vs_ref, kw_ref, vw_ref, kc_ref, vc_ref, gate_ref, o_ref,
                     vst_scr, vwt_scr, sel_scr, acc_scr, sc_scr):
    qi = pl.program_id(2)
    tq = ATT_TQ
    tk = ATT_TK
    dh = NSA_HEAD_DIM
    r_heads = NSA_Q_PER_KV
    n_kt = SEQ // tk
    scale = dh ** -0.5 * LOG2_E

    @pl.when(qi == 0)
    def _():
        ones = jnp.ones((ATT_ONES, tk), BF16)
        for kt in range(n_kt):
            rows = slice(kt * tk, (kt + 1) * tk)
            vst_scr[kt, 0:dh, :] = vs_ref[0, rows, :].astype(F32).T.astype(BF16)
            vwt_scr[kt, 0:dh, :] = vw_ref[0, rows, :].astype(F32).T.astype(BF16)
            vst_scr[kt, dh:dh + ATT_ONES, :] = ones
            vwt_scr[kt, dh:dh + ATT_ONES, :] = ones

    q_s = jnp.concatenate([q_ref[r].astype(F32).T for r in range(r_heads)], axis=1) * scale
    q_hi = q_s.astype(BF16)
    q_t = jnp.concatenate([q_hi, (q_s - q_hi.astype(F32)).astype(BF16)], axis=0)

    def qk(k):
        return _dot(jnp.concatenate([k, k], axis=1), q_t)
    n_cp = SEQ // CMP_STRIDE
    sub = lax.broadcasted_iota(I32, (n_cp, tq), 0)
    t_pos = qi * tq + lax.broadcasted_iota(I32, (n_cp, tq), 1)

    def tile4(a):
        return jnp.concatenate([a] * r_heads, axis=1)

    kc = kc_ref[0, 0, 0].astype(BF16)
    vc = vc_ref[0, 0, 0].astype(BF16)
    ok_c = jnp.where(sub * CMP_STRIDE + CMP_BLOCK - 1 <= t_pos, jnp.where(sub < N_CMP, 1.0, 0.0), 0.0)
    ok_c4 = tile4(ok_c)
    s_c = qk(kc) + (ok_c4 - 1.0) * (-NEG_INF)
    e_c = jnp.exp2(s_c - jnp.max(s_c, axis=0, keepdims=True))
    p_c = (e_c / jnp.sum(e_c, axis=0, keepdims=True)) * ok_c4
    o_cmp = _dot_tn(vc, p_c.astype(BF16))
    p_sum = p_c[:, 0:tq]
    for r in range(1, r_heads):
        p_sum = p_sum + p_c[:, r * tq:(r + 1) * tq]

    blk_row = lax.broadcasted_iota(I32, (LANES, LANES), 0)
    cmp_col = lax.broadcasted_iota(I32, (LANES, LANES), 1)
    s_start = blk_row * SLC_BLOCK
    c_start = cmp_col * CMP_STRIDE
    ov_t = jnp.maximum(jnp.minimum(c_start + CMP_BLOCK, s_start + SLC_BLOCK)
                       - jnp.maximum(c_start, s_start), 0).astype(F32) / CMP_BLOCK
    ov_t = jnp.where(blk_row < N_SLC, ov_t, 0.0).astype(BF16)
    imp = _dot_split_rhs(ov_t, p_sum)[0:N_SLC]
    j_blk = lax.broadcasted_iota(I32, (N_SLC, tq), 0)
    dist = (qi * tq + lax.broadcasted_iota(I32, (N_SLC, tq), 1)) // SLC_BLOCK - j_blk
    imp = jnp.where(j_blk == 0, SLC_FORCE, jnp.where(dist < 0, imp, jnp.where(dist < SLC_LOCAL, SLC_FORCE, imp)))
    imp = jnp.where(dist >= 0, imp, -jnp.inf)
    cnt = jnp.zeros((N_SLC, tq), I32)
    for k in range(N_SLC):
        row_k = imp[k:k + 1, :]
        tie = jnp.where(j_blk > k, 1, 0)
        cnt = cnt + jnp.where(row_k > imp, 1, jnp.where(row_k == imp, tie, 0))
    sel = jnp.where(cnt < min(SLC_TOPK, N_SLC), 1.0, 0.0)
    for j in range(N_SLC):
        sel_scr[8 * j:8 * j + 8, :] = jnp.broadcast_to(sel[j:j + 1, :], (8, tq))

    def scores(k, ok):
        return qk(k) + tile4((ok - 1.0) * (-NEG_INF))

    def weighted_values(vt_scr, kt0, pr, n_tiles):
        out = None
        for u in range(n_tiles):
            term = _dot(vt_scr[kt0 + u], pr[u * tk:(u + 1) * tk].astype(BF16))
            out = term if out is None else out + term
        return out

    n_ct = ATT_CHUNK_TILES
    ck = n_ct * tk
    sub_c = lax.broadcasted_iota(I32, (ck, tq), 0)
    t_pos_c = qi * tq + lax.broadcasted_iota(I32, (ck, tq), 1)
    blocks_per_chunk = ck // SLC_BLOCK
    acc_scr[...] = jnp.zeros(acc_scr.shape, F32)

    def chunk_scores(c):
        c = jnp.minimum(c, SEQ // ck - 1)
        start = pl.multiple_of(c * ck, ck)
        k = ks_ref[0, pl.ds(start, ck), :]
        rows8 = sel_scr[pl.ds(pl.multiple_of(c * (8 * blocks_per_chunk), 8 * blocks_per_chunk),
                              8 * blocks_per_chunk), :]
        picked = jnp.concatenate(
            [rows8[8 * u:8 * u + 8] for u in range(blocks_per_chunk) for _ in range(SLC_BLOCK // 8)], axis=0)
        return scores(k, jnp.where(start + sub_c <= t_pos_c, picked, 0.0))

    sc_scr[...] = chunk_scores(0)

    def slc_body(c, m_old):
        sc = sc_scr[...]
        sc_next = chunk_scores(c + 1)
        m_new = jnp.maximum(m_old, jnp.max(sc, axis=0, keepdims=True))
        alpha = jnp.exp2(m_old - m_new)
        pr = jnp.exp2(sc - m_new)
        acc_scr[...] = alpha * acc_scr[...] + weighted_values(vst_scr, c * n_ct, pr, n_ct)
        sc_scr[...] = sc_next
        return m_new

    lax.fori_loop(0, ((qi + 1) * tq + ck - 1) // ck, slc_body, jnp.full((1, r_heads * tq), NEG_INF, F32))
    acc = acc_scr[...]
    o_slc = acc[0:dh] / acc[dh:dh + 1]

    n_wt = (WINDOW + tq) // tk
    kt0 = jnp.maximum(qi * (tq // tk) - WINDOW // tk, 0)
    w_start = pl.multiple_of(kt0 * tk, tk)
    key_w = w_start + lax.broadcasted_iota(I32, (n_wt * tk, tq), 0)
    t_pos_w = qi * tq + lax.broadcasted_iota(I32, (n_wt * tk, tq), 1)
    ok_w = jnp.where(key_w <= t_pos_w, jnp.where(key_w > t_pos_w - WINDOW, 1.0, 0.0), 0.0)
    sc_w = scores(kw_ref[0, pl.ds(w_start, n_wt * tk), :], ok_w)
    pr_w = jnp.exp2(sc_w - jnp.max(sc_w, axis=0, keepdims=True))
    acc_w = weighted_values(vwt_scr, kt0, pr_w, n_wt)
    o_win = acc_w[0:dh] / acc_w[dh:dh + 1]

    gate = jax.nn.sigmoid(gate_ref[0])
    for r in range(r_heads):
        cols = slice(r * tq, (r + 1) * tq)
        o = (gate[3 * r:3 * r + 1] * o_cmp[:, cols] + gate[3 * r + 1:3 * r + 2] * o_slc[:, cols]
             + gate[3 * r + 2:3 * r + 3] * o_win[:, cols])
        o_ref[:, r * dh:(r + 1) * dh] = o.T.astype(BF16)


def _nsa_attention(heads, kc_vc, gates):
    tq = ATT_TQ
    nq = SEQ // tq
    r = NSA_Q_PER_KV
    g_heads = NSA_KV_HEADS
    q_spec = pl.BlockSpec((r, tq, NSA_HEAD_DIM), lambda b, g, i: (g, b * nq + i, 0))

    def kv_spec(first_head):
        return pl.BlockSpec((1, SEQ, NSA_HEAD_DIM), lambda b, g, i: (first_head + g, b, 0))

    first = NSA_HEADS
    specs = [q_spec,
             kv_spec(first + 2 * g_heads), kv_spec(first + 3 * g_heads),
             kv_spec(first + 4 * g_heads), kv_spec(first + 5 * g_heads),
             pl.BlockSpec((1, 1, 1, SEQ // CMP_STRIDE, NSA_HEAD_DIM), lambda b, g, i: (0, g, b, 0, 0)),
             pl.BlockSpec((1, 1, 1, SEQ // CMP_STRIDE, NSA_HEAD_DIM), lambda b, g, i: (1, g, b, 0, 0)),
             pl.BlockSpec((1, 3 * r, tq), lambda b, g, i: (g, 0, b * nq + i))]
    vt_shape = (SEQ // ATT_TK, NSA_HEAD_DIM + ATT_ONES, ATT_TK)
    return pl.pallas_call(
        _nsa_attn_kernel, name="nsa_attn", grid=(BATCH, g_heads, nq),
        in_specs=specs,
        out_specs=pl.BlockSpec((tq, r * NSA_HEAD_DIM), lambda b, g, i: (b * nq + i, g)),
        out_shape=jax.ShapeDtypeStruct((TOKENS, NSA_QD), BF16),
        scratch_shapes=[pltpu.VMEM(vt_shape, BF16), pltpu.VMEM(vt_shape, BF16),
                        pltpu.VMEM((8 * N_SLC, tq), F32),
                        pltpu.VMEM((NSA_HEAD_DIM + ATT_ONES, r * tq), F32),
                        pltpu.VMEM((ATT_CHUNK_TILES * ATT_TK, r * tq), F32)],
        compiler_params=_cparams(("arbitrary", "arbitrary", "arbitrary")),
    )(heads, heads, heads, heads, heads, kc_vc, kc_vc, gates)


def _ssd_chunk_kernel(zx_ref, dtc_ref, dtr_ref, cw_ref, cb_ref, dtb_c_ref, alog_c_ref, dtb_r_ref, alog_r_ref,
                      dskip_ref, ng_ref, shift_ref, echan_ref, o_ref,
                      prev_scr, acum_r_scr, st_scr, fac_scr):
    chunk = pl.program_id(1)
    L = SSD_CHUNK
    W = SSD_GROUP_W
    hpg = SSD_HEADS_PER_GROUP
    n_xt = W // LANES
    x0 = SSD_D_INNER // LANES
    b0 = 2 * SSD_D_INNER // LANES
    c0 = b0 + SSD_GROUPS
    cb0 = SSD_D_INNER // LANES
    cc0 = cb0 + SSD_GROUPS

    @pl.when(chunk == 0)
    def _():
        prev_scr[...] = jnp.zeros(prev_scr.shape, BF16)
        st_scr[...] = jnp.zeros(st_scr.shape, F32)

    dt_c = jax.nn.softplus(dtc_ref[...] + dtb_c_ref[...])
    adt_c = dt_c * (-jnp.exp(alog_c_ref[...]))
    dt_r = jax.nn.softplus(dtr_ref[...] + dtb_r_ref[...])
    adt_r = dt_r * (-jnp.exp(alog_r_ref[...]))
    row = lax.broadcasted_iota(I32, (L, L), 0)
    col = lax.broadcasted_iota(I32, (L, L), 1)
    causal = row >= col
    tri = jnp.where(causal, 1.0, 0.0).astype(BF16)
    tri_t = jnp.where(col >= row, 1.0, 0.0).astype(BF16)
    acum_c = _dot_split_rhs(tri, adt_c)
    acum_r_scr[...] = _dot_split_lhs(adt_r, tri_t)
    a_last = acum_c[L - 1:L, :]
    fac = jnp.concatenate([dt_c, jnp.exp(acum_c), jnp.exp(a_last - acum_c),
                           jnp.broadcast_to(jnp.exp(a_last), (SSD_FAC_PAD, LANES))], axis=0)
    fac_hi = fac.astype(BF16)
    fac_scr[0] = fac_hi
    fac_scr[1] = (fac - fac_hi.astype(F32)).astype(BF16)
    lane_w = lax.broadcasted_iota(I32, (L, LANES), 1)
    first_half = lane_w < SSD_HEAD_DIM

    def tiles(ref, first, n):
        return jnp.concatenate([ref[first + q] for q in range(n)], axis=1)

    def group_body(g, carry):
        e_chan = echan_ref[g]
        ex = _dot(fac_scr[0, 0:3 * L], e_chan)
        dt_x, ea_x, sd_x = ex[0:L], ex[L:2 * L], ex[2 * L:3 * L]
        cd_x = (_dot(fac_scr[0, 3 * L:3 * L + SSD_FAC_PAD], e_chan)
                + _dot(fac_scr[1, 3 * L:3 * L + SSD_FAC_PAD], e_chan))[0:1]

        def conv_silu(zx_first, conv_first, n):
            cur = tiles(zx_ref, zx_first, n)
            ext = jnp.concatenate([tiles(prev_scr, conv_first, n), cur], axis=0)
            w = tiles(cw_ref, conv_first, n)
            acc = jnp.broadcast_to(tiles(cb_ref, conv_first, n), (L, n * LANES))
            for k in range(SSD_CONV):
                back = SSD_CONV - 1 - k
                xk = cur.astype(F32) if back == 0 else _dot(shift_ref[back - 1], ext)
                acc = acc + xk * w[k:k + 1, :]
            for q in range(n):
                prev_scr[conv_first + q] = zx_ref[zx_first + q]
            return jax.nn.silu(acc)

        xs = conv_silu(x0 + n_xt * g, n_xt * g, n_xt)
        bm = conv_silu(b0 + g, cb0 + g, 1)
        cm = conv_silu(c0 + g, cc0 + g, 1)

        xdt = xs * dt_x
        cb = jnp.where(causal, _dot_nt(cm.astype(BF16), bm.astype(BF16)), 0.0)
        y_parts = []
        for pair in range(hpg // 2):
            xd = xdt[:, pair * LANES:(pair + 1) * LANES]
            m_pair = []
            for sub in range(2):
                a_row = jnp.broadcast_to(acum_r_scr[pl.ds(g * hpg + 2 * pair + sub, 1), :], (L, L))
                seg = jnp.minimum(a_row.T - a_row, 0.0)
                m_pair.append((cb * jnp.exp(seg)).astype(BF16))
            x_pair = jnp.concatenate([jnp.where(first_half, xd, 0.0), jnp.where(first_half, 0.0, xd)], axis=0)
            y_parts.append(_dot(jnp.concatenate(m_pair, axis=1), x_pair.astype(BF16)))
        y_diag = jnp.concatenate(y_parts, axis=1)

        st = st_scr[g]
        y_off = _dot(cm.astype(BF16), st.astype(BF16)) * ea_x
        st_scr[g] = st * cd_x + _dot_tn(bm.astype(BF16), (xdt * sd_x).astype(BF16))

        y = y_diag + y_off + xs * tiles(dskip_ref, n_xt * g, n_xt)
        y = y * jax.nn.silu(tiles(zx_ref, n_xt * g, n_xt).astype(F32))
        y = y * lax.rsqrt(jnp.mean(y * y, axis=-1, keepdims=True) + NORM_EPS)
        y = y * tiles(ng_ref, n_xt * g, n_xt)
        for q in range(n_xt):
            o_ref[n_xt * g + q] = y[:, q * LANES:(q + 1) * LANES].astype(BF16)
        return carry

    lax.fori_loop(0, SSD_GROUPS, group_body, 0)


def _ssd_chunks(zx_tiles, dt_small, dt_small_t, conv_w, conv_b, dt_bias, a_log, d_skip, norm_g):
    L = SSD_CHUNK
    nc = SEQ // L
    n_zx = SSD_MAIN // LANES
    n_conv = SSD_CONV_CH // LANES
    n_inner = SSD_D_INNER // LANES
    hpg = SSD_HEADS_PER_GROUP

    def pad_heads(v):
        return jnp.zeros((LANES,), F32).at[:SSD_HEADS].set(v)

    dtb = pad_heads(dt_bias)
    alog = pad_heads(a_log)
    cw = conv_w.reshape(SSD_CONV, n_conv, LANES).transpose(1, 0, 2)
    cb = conv_b.reshape(n_conv, 1, LANES)
    d_chan = jnp.repeat(d_skip, SSD_HEAD_DIM).reshape(n_inner, 1, LANES)
    ng = norm_g.reshape(n_inner, 1, LANES)
    t_idx = jnp.arange(L, dtype=I32)[None, :, None]
    r_idx = jnp.arange(2 * L, dtype=I32)[None, None, :]
    back = jnp.arange(1, SSD_CONV, dtype=I32)[:, None, None]
    shift = (r_idx == L + t_idx - back).astype(BF16)
    head = jnp.arange(LANES, dtype=I32)[None, :, None]
    grp = jnp.arange(SSD_GROUPS, dtype=I32)[:, None, None]
    e_chan = (head == grp * hpg + jnp.arange(SSD_GROUP_W, dtype=I32)[None, None, :] // SSD_HEAD_DIM).astype(BF16)
    row = lambda b, c: b * nc + c
    const3 = lambda b, c: (0, 0, 0)
    const2 = lambda b, c: (0, 0)
    in_specs = [
        pl.BlockSpec((n_zx, L, LANES), lambda b, c: (0, row(b, c), 0)),
        pl.BlockSpec((L, LANES), lambda b, c: (row(b, c), 0)),
        pl.BlockSpec((LANES, L), lambda b, c: (0, row(b, c))),
        pl.BlockSpec((n_conv, SSD_CONV, LANES), const3),
        pl.BlockSpec((n_conv, 1, LANES), const3),
        pl.BlockSpec((1, LANES), const2), pl.BlockSpec((1, LANES), const2),
        pl.BlockSpec((LANES, 1), const2), pl.BlockSpec((LANES, 1), const2),
        pl.BlockSpec((n_inner, 1, LANES), const3),
        pl.BlockSpec((n_inner, 1, LANES), const3),
        pl.BlockSpec((SSD_CONV - 1, L, 2 * L), const3),
        pl.BlockSpec((SSD_GROUPS, LANES, SSD_GROUP_W), const3),
    ]
    return pl.pallas_call(
        _ssd_chunk_kernel, name="ssd_chunks", grid=(BATCH, nc),
        in_specs=in_specs,
        out_specs=pl.BlockSpec((n_inner, L, LANES), lambda b, c: (0, row(b, c), 0)),
        out_shape=jax.ShapeDtypeStruct((n_inner, TOKENS, LANES), BF16),
        scratch_shapes=[pltpu.VMEM((n_conv, L, LANES), BF16),
                        pltpu.VMEM((LANES, L), F32),
                        pltpu.VMEM((SSD_GROUPS, SSD_D_STATE, SSD_GROUP_W), F32),
                        pltpu.VMEM((2, 3 * L + SSD_FAC_PAD, LANES), BF16)],
        compiler_params=_cparams(("arbitrary", "arbitrary")),
    )(zx_tiles, dt_small, dt_small_t, cw, cb, dtb.reshape(1, LANES), alog.reshape(1, LANES),
      dtb.reshape(LANES, 1), alog.reshape(LANES, 1), d_chan, ng, shift, e_chan)


def _gather_rows(src_hbm, idx_ref, base, dst, sem, n_rows):
    def body(b, carry):
        for u in range(GATHER_UNROLL):
            r = b * GATHER_UNROLL + u
            tok = idx_ref[base + r]
            pltpu.make_async_copy(src_hbm.at[pl.ds(tok, 1), :], dst.at[pl.ds(r, 1), :], sem).start()
        return carry

    lax.fori_loop(0, n_rows // GATHER_UNROLL, body, 0)


def _moe_ffn_kernel(te_ref, pos_ref, nact_ref, pad_ref, wp_ref, x_hbm, g_ref, wg_hbm, wu_hbm, wd_hbm, y_ref,
                    tok_ref, buf, sem, wg_f, wu_f, wd_f, wsem, wg_bf, wu_bf, wd_bf, *, layer):
    i = pl.program_id(0)
    n_act = nact_ref[0]
    tm = MOE_TM
    slot = i % 2
    run_start = wp_ref[i] == 1
    w_slot = wp_ref[MOE_TILES + i]
    next_expert = wp_ref[2 * MOE_TILES + i]
    n_rows = pl.multiple_of(wp_ref[3 * MOE_TILES + i], GATHER_UNROLL)

    def gather(tile, s):
        _gather_rows(x_hbm, tok_ref, tile * tm, buf.at[s], sem.at[s], wp_ref[3 * MOE_TILES + tile])

    def weight_copies(e, s):
        return (pltpu.make_async_copy(wg_hbm.at[layer, e], wg_f.at[s], wsem.at[s, 0]),
                pltpu.make_async_copy(wu_hbm.at[layer, e], wu_f.at[s], wsem.at[s, 1]),
                pltpu.make_async_copy(wd_hbm.at[layer, e], wd_f.at[s], wsem.at[s, 2]))

    @pl.when(i == 0)
    def _():
        for c in weight_copies(te_ref[0], 0):
            c.start()
        buf[...] = jnp.zeros(buf.shape, F32)
        def clear(q, carry):
            tok_ref[q] = 0
            return carry

        def clear_padding(e, carry):
            lax.fori_loop(pad_ref[e], pad_ref[MOE_EXPERTS + e], clear, 0)
            return carry

        lax.fori_loop(0, MOE_EXPERTS, clear_padding, 0)
        for k in range(MOE_TOPK):
            def place(t, carry, k=k):
                tok_ref[pos_ref[k * TOKENS + t]] = t
                return carry

            lax.fori_loop(0, TOKENS, place, 0, unroll=GATHER_UNROLL)
        gather(0, 0)

    @pl.when(run_start)
    def _():
        for c in weight_copies(te_ref[i], w_slot):
            c.wait()

        @pl.when(next_expert >= 0)
        def _():
            for c in weight_copies(next_expert, 1 - w_slot):
                c.start()

        wg_bf[...] = wg_f[w_slot].astype(BF16)
        wu_bf[...] = wu_f[w_slot].astype(BF16)
        wd_bf[...] = wd_f[w_slot].astype(BF16)

    @pl.when(i >= n_act)
    def _():
        y_ref[...] = jnp.zeros(y_ref.shape, F32)

    @pl.when(i < n_act)
    def _():
        pltpu.make_async_copy(x_hbm.at[pl.ds(0, n_rows), :], buf.at[slot, pl.ds(0, n_rows), :],
                              sem.at[slot]).wait()

        @pl.when(i + 1 < n_act)
        def _():
            gather(i + 1, 1 - slot)

        h = _rms(buf[slot], g_ref[...]).astype(BF16)
        act = jax.nn.silu(_dot(h, wg_bf[...])) * _dot(h, wu_bf[...])
        y_ref[...] = _dot(act.astype(BF16), wd_bf[...])


def _moe_ffn(x, g, w_gate, w_up, w_down, layer, tile_expert, pos_kmajor, n_active, pad_rows, weight_plan):
    any_spec = pl.BlockSpec(memory_space=pl.ANY)
    grid_spec = pltpu.PrefetchScalarGridSpec(
        num_scalar_prefetch=5, grid=(MOE_TILES,),
        in_specs=[any_spec, pl.BlockSpec((1, D_MODEL), lambda i, *_: (0, 0)), any_spec, any_spec, any_spec],
        out_specs=pl.BlockSpec((MOE_TM, D_MODEL), lambda i, *_: (i, 0)),
        scratch_shapes=[pltpu.SMEM((MOE_ROWS,), I32),
                        pltpu.VMEM((2, MOE_TM, D_MODEL), F32), pltpu.SemaphoreType.DMA((2,)),
                        pltpu.VMEM((2, D_MODEL, MOE_D_FF), F32), pltpu.VMEM((2, D_MODEL, MOE_D_FF), F32),
                        pltpu.VMEM((2, MOE_D_FF, D_MODEL), F32), pltpu.SemaphoreType.DMA((2, 3)),
                        pltpu.VMEM((D_MODEL, MOE_D_FF), BF16), pltpu.VMEM((D_MODEL, MOE_D_FF), BF16),
                        pltpu.VMEM((MOE_D_FF, D_MODEL), BF16)])
    return pl.pallas_call(
        functools.partial(_moe_ffn_kernel, layer=layer), name="moe_ffn", grid_spec=grid_spec,
        out_shape=jax.ShapeDtypeStruct((MOE_ROWS, D_MODEL), F32),
        compiler_params=_cparams(("arbitrary",)),
    )(tile_expert, pos_kmajor, n_active, pad_rows, weight_plan, x, g.reshape(1, D_MODEL), w_gate, w_up, w_down)


def _moe_combine_kernel(pos_ref, x_ref, cw_ref, g_ref, ws_ref, y_hbm, *refs, final_norm):
    if final_norm:
        o_ref, buf, sem = refs
    else:
        o_ref, hn_ref, small_ref, small_t_ref, buf, sem = refs
    i = pl.program_id(0)
    n = pl.num_programs(0)
    tm = CMB_TM
    slot = i % 2

    def issue(tile, s):
        for k in range(MOE_TOPK):
            _gather_rows(y_hbm, pos_ref, (k * (TOKENS // tm) + tile) * tm, buf.at[s, k], sem.at[s], tm)

    @pl.when(i == 0)
    def _():
        issue(0, 0)

    @pl.when(i + 1 < n)
    def _():
        issue(i + 1, 1 - slot)

    for k in range(MOE_TOPK):
        pltpu.make_async_copy(y_hbm.at[pl.ds(0, tm), :], buf.at[slot, k], sem.at[slot]).wait()
    cw = cw_ref[...]
    out = x_ref[...] + cw[:, 0:1] * buf[slot, 0] + cw[:, 1:2] * buf[slot, 1]
    y = _rms(out, g_ref[...])
    if final_norm:
        o_ref[...] = y
    else:
        o_ref[...] = out
        hn_ref[...] = y.astype(BF16)
        small = _dot_x3(y, ws_ref[...])
        small_ref[...] = small
        small_t_ref[...] = small.T


def _moe_combine(x, cw, y_sorted, pos_kmajor, g_norm, w_small, final_norm):
    ws = jnp.zeros((D_MODEL, LANES), F32)
    if not final_norm:
        ws = ws.at[:, :w_small.shape[1]].set(w_small)
    row = pl.BlockSpec((CMB_TM, D_MODEL), lambda i, pos: (i, 0))
    small = pl.BlockSpec((CMB_TM, LANES), lambda i, pos: (i, 0))
    x_shape = jax.ShapeDtypeStruct((TOKENS, D_MODEL), F32)
    if final_norm:
        out_specs, out_shape = row, x_shape
    else:
        out_specs = [row, row, small, pl.BlockSpec((LANES, CMB_TM), lambda i, pos: (0, i))]
        out_shape = [x_shape, jax.ShapeDtypeStruct((TOKENS, D_MODEL), BF16),
                     jax.ShapeDtypeStruct((TOKENS, LANES), F32), jax.ShapeDtypeStruct((LANES, TOKENS), F32)]
    grid_spec = pltpu.PrefetchScalarGridSpec(
        num_scalar_prefetch=1, grid=(TOKENS // CMB_TM,),
        in_specs=[row, small,
                  pl.BlockSpec((1, D_MODEL), lambda i, pos: (0, 0)),
                  pl.BlockSpec((D_MODEL, LANES), lambda i, pos: (0, 0)),
                  pl.BlockSpec(memory_space=pl.ANY)],
        out_specs=out_specs,
        scratch_shapes=[pltpu.VMEM((2, MOE_TOPK, CMB_TM, D_MODEL), F32), pltpu.SemaphoreType.DMA((2,))])
    return pl.pallas_call(
        functools.partial(_moe_combine_kernel, final_norm=final_norm), name="moe_combine", grid_spec=grid_spec,
        out_shape=out_shape,
        compiler_params=_cparams(("arbitrary",)),
    )(pos_kmajor, x, cw, g_norm.reshape(1, D_MODEL), ws, y_sorted)


def _moe_plan(eid):
    e = eid[:, :MOE_TOPK].reshape(-1)
    onehot = (e[:, None] == jnp.arange(MOE_EXPERTS, dtype=I32)[None, :]).astype(I32)
    csum = jnp.cumsum(onehot, axis=0)
    counts = csum[-1]
    rank = jnp.take_along_axis(csum, e[:, None], axis=1)[:, 0] - 1
    padded = ((counts + MOE_TM - 1) // MOE_TM) * MOE_TM
    g_end = jnp.cumsum(padded)
    g_start = g_end - padded
    pos = g_start[e] + rank
    n_active = (g_end[-1] // MOE_TM).astype(I32)
    tile_start = jnp.arange(MOE_TILES, dtype=I32) * MOE_TM
    te = jnp.sum((g_end[None, :] <= tile_start[:, None]).astype(I32), axis=1)
    last = jnp.max(jnp.where(counts > 0, jnp.arange(MOE_EXPERTS, dtype=I32), 0))
    tile_expert = jnp.minimum(te, last)
    pos_kmajor = pos.reshape(TOKENS, MOE_TOPK).T.reshape(-1)
    pad_rows = jnp.concatenate([g_start + counts, g_end])
    experts = jnp.arange(MOE_EXPERTS, dtype=I32)
    run_start = jnp.concatenate([jnp.ones((1,), I32), (tile_expert[1:] != tile_expert[:-1]).astype(I32)])
    w_slot = (jnp.cumsum(run_start) - 1) % 2
    later = (experts[None, :] > experts[:, None]) & (counts > 0)[None, :]
    next_nonempty = jnp.min(jnp.where(later, experts[None, :], MOE_EXPERTS), axis=1)
    next_nonempty = jnp.where(next_nonempty == MOE_EXPERTS, -1, next_nonempty)
    real_rows = jnp.clip((g_start + counts)[tile_expert] - tile_start, 0, MOE_TM)
    real_rows = jnp.where(tile_start < g_end[-1], real_rows, 0)
    gather_rows = jnp.minimum(-(-real_rows // GATHER_UNROLL) * GATHER_UNROLL, MOE_TM)
    weight_plan = jnp.concatenate([run_start, w_slot, next_nonempty[tile_expert], gather_rows]).astype(I32)
    return tile_expert, n_active.reshape(1), pos_kmajor, pad_rows, weight_plan


def _hier_moe_add(x, eid, cw, ln_g, w_gate, w_up, w_down, layer, g_norm, w_small, final_norm):
    tile_expert, n_active, pos_kmajor, pad_rows, weight_plan = _moe_plan(eid)
    y_sorted = _moe_ffn(x, ln_g, w_gate, w_up, w_down, layer, tile_expert, pos_kmajor, n_active, pad_rows,
                        weight_plan)
    return _moe_combine(x, cw, y_sorted, pos_kmajor, g_norm, w_small, final_norm)


def _rope_tables():
    pos = jnp.arange(SEQ, dtype=F32)
    inv = 1.0 / (ROPE_THETA ** (jnp.arange(0, NSA_HEAD_DIM, 2, dtype=F32) / NSA_HEAD_DIM))
    ang = pos[:, None] * inv[None, :]
    cos, sin = jnp.cos(ang), jnp.sin(ang)
    return jnp.concatenate([cos, cos], axis=1), jnp.concatenate([-sin, sin], axis=1)


def _nsa_mixer(hn, g_lin_t, w_in, cmp_pe, cmp_w1, cmp_w2):
    w_in_t = w_in.T
    cos_full, sin_signed = _rope_tables()
    heads = _nsa_inproj(hn, w_in_t, cos_full, sin_signed)
    first_c = NSA_HEADS
    kv_c = heads[first_c:first_c + 2 * NSA_KV_HEADS]
    kv_chunks = kv_c.reshape(2, NSA_KV_HEADS, BATCH, SEQ // CMP_STRIDE, CMP_STRIDE * NSA_HEAD_DIM)
    kc_vc = _compress(kv_chunks, cmp_pe, cmp_w1, cmp_w2)
    gates_t = g_lin_t[:NSA_GATES].reshape(NSA_KV_HEADS, 3 * NSA_Q_PER_KV, TOKENS)
    return _nsa_attention(heads, kc_vc, gates_t)


def _ssd_mixer(hn, dt_small, dt_small_t, w_in, conv_w, conv_b, dt_bias, a_log, d_skip, norm_g):
    zx_tiles = _matmul_tiles(hn, w_in.T, SSD_MAIN, 1024)
    return _ssd_chunks(zx_tiles, dt_small, dt_small_t, conv_w, conv_b, dt_bias, a_log, d_skip, norm_g)


def kernel(x, ln_mix, ln_ffn, ln_final, nsa_w_in, nsa_cmp_pe, nsa_cmp_w1, nsa_cmp_w2, nsa_w_out,
           ssd_w_in, ssd_conv_w, ssd_conv_b, ssd_dt_bias, ssd_a_log, ssd_d, ssd_norm, ssd_w_out,
           moe_w_group, moe_b_group, moe_w_expert, moe_b_expert, moe_w_gate, moe_w_up, moe_w_down):
    def small_weight(i):
        if i % N_MIXERS == 0:
            return nsa_w_in[i // N_MIXERS].T[NSA_MAIN:].T
        return ssd_w_in[i // N_MIXERS].T[SSD_MAIN:].T

    h = x.reshape(TOKENS, D_MODEL)
    hn, small, small_t = _norm_small(h, ln_mix[0], small_weight(0), transposed=True)
    for i in range(DEPTH):
        j = i // N_MIXERS
        last = i == DEPTH - 1
        if i % N_MIXERS == 0:
            mix = _nsa_mixer(hn, small_t, nsa_w_in[j], nsa_cmp_pe[j], nsa_cmp_w1[j], nsa_cmp_w2[j])
            w_out = nsa_w_out[j]
        else:
            mix = _ssd_mixer(hn, small, small_t, ssd_w_in[j], ssd_conv_w[j], ssd_conv_b[j], ssd_dt_bias[j],
                             ssd_a_log[j], ssd_d[j], ssd_norm[j])
            w_out = ssd_w_out[j]
        h, eid, cw = _outproj_route(mix, w_out, h, ln_ffn[i], moe_w_group[i], moe_b_group[i],
                                    moe_w_expert[i], moe_b_expert[i])
        out = _hier_moe_add(h, eid, cw, ln_ffn[i], moe_w_gate, moe_w_up, moe_w_down, i,
                            ln_final if last else ln_mix[i + 1], None if last else small_weight(i + 1), last)
        if last:
            h = out
        else:
            h, hn, small, small_t = out
    return h.reshape(BATCH, SEQ, D_MODEL)
```

```python
import functools

import jax
import jax.numpy as jnp
from jax import lax
from jax.experimental import pallas as pl
from jax.experimental.pallas import tpu as pltpu

F32 = jnp.float32
BF16 = jnp.bfloat16
I32 = jnp.int32

D_MODEL = 2048
BATCH = 4
SEQ = 2048
TOKENS = BATCH * SEQ
DEPTH = 2
N_MIXERS = 2
NORM_EPS = 1e-6
NEG_INF = -1e30
LOG2_E = 1.4426950408889634
ROPE_THETA = 10000.0

NSA_HEADS = 16
NSA_KV_HEADS = 4
NSA_HEAD_DIM = D_MODEL // NSA_HEADS
NSA_Q_PER_KV = NSA_HEADS // NSA_KV_HEADS
CMP_BLOCK = 32
CMP_STRIDE = 16
CMP_HIDDEN = 256
N_CMP = (SEQ - CMP_BLOCK) // CMP_STRIDE + 1
SLC_BLOCK = 64
SLC_TOPK = 16
SLC_LOCAL = 2
SLC_FORCE = 1e4
N_SLC = SEQ // SLC_BLOCK
WINDOW = 512
NSA_QD = NSA_HEADS * NSA_HEAD_DIM
NSA_KVD = NSA_KV_HEADS * NSA_HEAD_DIM
NSA_MAIN = NSA_QD + 6 * NSA_KVD
NSA_GATES = 3 * NSA_HEADS

SSD_D_INNER = 2 * D_MODEL
SSD_HEAD_DIM = 64
SSD_HEADS = SSD_D_INNER // SSD_HEAD_DIM
SSD_GROUPS = 8
SSD_HEADS_PER_GROUP = SSD_HEADS // SSD_GROUPS
SSD_D_STATE = 128
SSD_CONV = 4
SSD_CHUNK = 128
SSD_GROUP_W = SSD_D_INNER // SSD_GROUPS
SSD_BC = SSD_GROUPS * SSD_D_STATE
SSD_CONV_CH = SSD_D_INNER + 2 * SSD_BC
SSD_MAIN = SSD_D_INNER + SSD_CONV_CH
SSD_FAC_PAD = 16

MOE_GROUPS = 4
MOE_EPG = 8
MOE_EXPERTS = MOE_GROUPS * MOE_EPG
MOE_TOPK = 2
MOE_D_FF = 512

LANES = 128
VMEM_LIMIT = 56 * 1024 * 1024

NORM_TM = 256
MM_TM = 1024
OUT_TM = 256
ATT_TQ = 256
ATT_TK = 256
ATT_CHUNK_TILES = 2
ATT_ONES = 16
MOE_TM = 256
MOE_TILES = (TOKENS * MOE_TOPK) // MOE_TM + MOE_EXPERTS
MOE_ROWS = MOE_TILES * MOE_TM
CMB_TM = 128
GATHER_UNROLL = 8


def _cparams(sem):
    return pltpu.CompilerParams(dimension_semantics=sem, vmem_limit_bytes=VMEM_LIMIT)


def _split3(x):
    hi = x.astype(BF16)
    r1 = x - hi.astype(F32)
    mid = r1.astype(BF16)
    lo = (r1 - mid.astype(F32)).astype(BF16)
    return hi, mid, lo


def _dot(a, b):
    return jnp.dot(a, b, preferred_element_type=F32)


def _dot_nt(a, b):
    return lax.dot_general(a, b, (((1,), (1,)), ((), ())), preferred_element_type=F32)


def _dot_tn(a, b):
    return lax.dot_general(a, b, (((0,), (0,)), ((), ())), preferred_element_type=F32)


def _dot_split_lhs(x, m_bf16):
    hi, mid, lo = _split3(x)
    return _dot(hi, m_bf16) + _dot(mid, m_bf16) + _dot(lo, m_bf16)


def _dot_split_rhs(m_bf16, x):
    hi, mid, lo = _split3(x)
    return _dot(m_bf16, hi) + _dot(m_bf16, mid) + _dot(m_bf16, lo)


def _dot_x3(a, w):
    a_hi = a.astype(BF16)
    a_lo = (a - a_hi.astype(F32)).astype(BF16)
    w_hi = w.astype(BF16)
    w_lo = (w - w_hi.astype(F32)).astype(BF16)
    return _dot(a_hi, w_hi) + _dot(a_hi, w_lo) + _dot(a_lo, w_hi)


def _dot_x3_nt(a, w):
    a_hi = a.astype(BF16)
    a_lo = (a - a_hi.astype(F32)).astype(BF16)
    w_hi = w.astype(BF16)
    w_lo = (w - w_hi.astype(F32)).astype(BF16)
    return _dot_nt(a_hi, w_hi) + _dot_nt(a_hi, w_lo) + _dot_nt(a_lo, w_hi)


def _rms(x, g):
    y = x * lax.rsqrt(jnp.mean(x * x, axis=-1, keepdims=True) + NORM_EPS)
    return y * g


def _norm_small_kernel(x_ref, g_ref, ws_ref, hn_ref, small_ref):
    y = _rms(x_ref[...], g_ref[...])
    hn_ref[...] = y.astype(BF16)
    small_ref[...] = _dot_x3(y, ws_ref[...])


def _norm_small_t_kernel(x_ref, g_ref, ws_ref, hn_ref, small_ref, small_t_ref):
    y = _rms(x_ref[...], g_ref[...])
    hn_ref[...] = y.astype(BF16)
    small = _dot_x3(y, ws_ref[...])
    small_ref[...] = small
    small_t_ref[...] = small.T


def _norm_small(x, g, w_small, transposed=False):
    n = w_small.shape[1]
    ws = jnp.zeros((D_MODEL, LANES), F32).at[:, :n].set(w_small)
    grid = (TOKENS // NORM_TM,)
    x_spec = pl.BlockSpec((NORM_TM, D_MODEL), lambda i: (i, 0))
    g_spec = pl.BlockSpec((1, D_MODEL), lambda i: (0, 0))
    w_spec = pl.BlockSpec((D_MODEL, LANES), lambda i: (0, 0))
    hn_spec = pl.BlockSpec((NORM_TM, D_MODEL), lambda i: (i, 0))
    sm_spec = pl.BlockSpec((NORM_TM, LANES), lambda i: (i, 0))
    hn_shape = jax.ShapeDtypeStruct((TOKENS, D_MODEL), BF16)
    sm_shape = jax.ShapeDtypeStruct((TOKENS, LANES), F32)
    if not transposed:
        return pl.pallas_call(
            _norm_small_kernel, name="norm_small", grid=grid,
            in_specs=[x_spec, g_spec, w_spec],
            out_specs=[hn_spec, sm_spec],
            out_shape=[hn_shape, sm_shape],
            compiler_params=_cparams(("parallel",)),
        )(x, g.reshape(1, D_MODEL), ws)
    smt_spec = pl.BlockSpec((LANES, NORM_TM), lambda i: (0, i))
    smt_shape = jax.ShapeDtypeStruct((LANES, TOKENS), F32)
    return pl.pallas_call(
        _norm_small_t_kernel, name="norm_small_t", grid=grid,
        in_specs=[x_spec, g_spec, w_spec],
        out_specs=[hn_spec, sm_spec, smt_spec],
        out_shape=[hn_shape, sm_shape, smt_shape],
        compiler_params=_cparams(("parallel",)),
    )(x, g.reshape(1, D_MODEL), ws)


def _route(x, g, ws, bias):
    y = _rms(x, g)
    logits = _dot_x3(y, ws) + bias
    lane = lax.broadcasted_iota(I32, logits.shape, 1)
    big = jnp.int32(LANES)
    neg = -jnp.inf
    gl = jnp.where(lane < MOE_GROUPS, logits, neg)
    gmax = jnp.max(gl, axis=-1, keepdims=True)
    gsum = jnp.sum(jnp.exp(gl - gmax), axis=-1, keepdims=True)
    g_w = 1.0 / gsum
    g_sel = jnp.min(jnp.where(gl == gmax, lane, big), axis=-1, keepdims=True)
    lo = MOE_GROUPS + g_sel * MOE_EPG
    el = jnp.where((lane >= lo) & (lane < lo + MOE_EPG), logits, neg)
    v1 = jnp.max(el, axis=-1, keepdims=True)
    i1 = jnp.min(jnp.where(el == v1, lane, big), axis=-1, keepdims=True)
    el2 = jnp.where(lane == i1, neg, el)
    v2 = jnp.max(el2, axis=-1, keepdims=True)
    i2 = jnp.min(jnp.where(el2 == v2, lane, big), axis=-1, keepdims=True)
    e2 = jnp.exp(v2 - v1)
    den = 1.0 + e2
    w1 = (1.0 / den) * g_w
    w2 = (e2 / den) * g_w
    eid = jnp.where(lane == 0, i1 - MOE_GROUPS, jnp.where(lane == 1, i2 - MOE_GROUPS, 0))
    cw = jnp.where(lane == 0, w1, jnp.where(lane == 1, w2, 0.0))
    return eid, cw


def _inproj_heads_kernel(a_ref, w_ref, cos_ref, sin_ref, o_ref, wbf_ref):
    j = pl.program_id(0)

    @pl.when(pl.program_id(1) == 0)
    def _():
        wbf_ref[...] = w_ref[...].astype(BF16)

    acc = _dot_nt(a_ref[...], wbf_ref[...])
    q_tiles = NSA_QD // acc.shape[1]
    c = cos_ref[...]
    s = sin_ref[...]

    def head(h):
        return acc[:, h * NSA_HEAD_DIM:(h + 1) * NSA_HEAD_DIM]

    def rotary(xh):
        return (xh * c + pltpu.roll(xh, NSA_HEAD_DIM // 2, 1) * s).astype(BF16)

    for h in range(NSA_KV_HEADS):
        o_ref[h] = rotary(head(h))

    @pl.when(j < q_tiles)
    def _():
        for h in range(NSA_KV_HEADS, 2 * NSA_KV_HEADS):
            o_ref[h] = rotary(head(h))

    @pl.when(j >= q_tiles)
    def _():
        for h in range(NSA_KV_HEADS, 2 * NSA_KV_HEADS):
            o_ref[h] = head(h).astype(BF16)


def _nsa_inproj(hn, w_in_t, cos_full, sin_signed):
    tn = 2 * NSA_KVD
    n_heads_tile = tn // NSA_HEAD_DIM
    s_tiles = SEQ // MM_TM
    return pl.pallas_call(
        _inproj_heads_kernel, name="nsa_inproj", grid=(NSA_MAIN // tn, TOKENS // MM_TM),
        in_specs=[pl.BlockSpec((MM_TM, D_MODEL), lambda j, i: (i, 0)),
                  pl.BlockSpec((tn, D_MODEL), lambda j, i: (j, 0)),
                  pl.BlockSpec((MM_TM, NSA_HEAD_DIM), lambda j, i: (i % s_tiles, 0)),
                  pl.BlockSpec((MM_TM, NSA_HEAD_DIM), lambda j, i: (i % s_tiles, 0))],
        out_specs=pl.BlockSpec((n_heads_tile, MM_TM, NSA_HEAD_DIM), lambda j, i: (j, i, 0)),
        out_shape=jax.ShapeDtypeStruct((NSA_MAIN // NSA_HEAD_DIM, TOKENS, NSA_HEAD_DIM), BF16),
        scratch_shapes=[pltpu.VMEM((tn, D_MODEL), BF16)],
        compiler_params=_cparams(("arbitrary", "arbitrary")),
    )(hn, w_in_t, cos_full, sin_signed)


def _matmul_tiles_kernel(a_ref, w_ref, o_ref, wbf_ref):
    @pl.when(pl.program_id(1) == 0)
    def _():
        wbf_ref[...] = w_ref[...].astype(BF16)

    acc = _dot_nt(a_ref[...], wbf_ref[...])
    for t in range(acc.shape[1] // LANES):
        o_ref[t] = acc[:, t * LANES:(t + 1) * LANES].astype(o_ref.dtype)


def _matmul_tiles(a, w_t, n_cols, tn):
    k = a.shape[1]
    return pl.pallas_call(
        _matmul_tiles_kernel, name="matmul_tiles", grid=(n_cols // tn, TOKENS // MM_TM),
        in_specs=[pl.BlockSpec((MM_TM, k), lambda j, i: (i, 0)),
                  pl.BlockSpec((tn, k), lambda j, i: (j, 0))],
        out_specs=pl.BlockSpec((tn // LANES, MM_TM, LANES), lambda j, i: (j, i, 0)),
        out_shape=jax.ShapeDtypeStruct((n_cols // LANES, TOKENS, LANES), BF16),
        scratch_shapes=[pltpu.VMEM((tn, k), BF16)],
        compiler_params=_cparams(("arbitrary", "arbitrary")),
    )(a, w_t)


def _outproj_route_kernel(a_ref, w_ref, r_ref, g_ref, ws_ref, b_ref, o_ref, eid_ref, cw_ref):
    if len(a_ref.shape) == 3:
        a = jnp.concatenate([a_ref[t] for t in range(a_ref.shape[0])], axis=1)
    else:
        a = a_ref[...]
    x_new = r_ref[...] + _dot(a, w_ref[...])
    o_ref[...] = x_new
    eid_ref[...], cw_ref[...] = _route(x_new, g_ref[...], ws_ref[...], b_ref[...])


def _outproj_route(a, w, resid, ln_ffn_g, w_group, b_group, w_expert, b_expert):
    tiled = a.ndim == 3
    k, n = w.shape
    n_r = MOE_GROUPS + MOE_EXPERTS
    ws = jnp.zeros((D_MODEL, LANES), F32).at[:, :n_r].set(jnp.concatenate([w_group, w_expert], axis=1))
    bs = jnp.zeros((1, LANES), F32).at[0, :n_r].set(jnp.concatenate([b_group, b_expert]))
    a_spec = (pl.BlockSpec((k // LANES, OUT_TM, LANES), lambda i: (0, i, 0)) if tiled
              else pl.BlockSpec((OUT_TM, k), lambda i: (i, 0)))
    small = pl.BlockSpec((OUT_TM, LANES), lambda i: (i, 0))
    return pl.pallas_call(
        _outproj_route_kernel, name="outproj_route", grid=(TOKENS // OUT_TM,),
        in_specs=[a_spec,
                  pl.BlockSpec((k, n), lambda i: (0, 0)),
                  pl.BlockSpec((OUT_TM, n), lambda i: (i, 0)),
                  pl.BlockSpec((1, D_MODEL), lambda i: (0, 0)),
                  pl.BlockSpec((D_MODEL, LANES), lambda i: (0, 0)),
                  pl.BlockSpec((1, LANES), lambda i: (0, 0))],
        out_specs=[pl.BlockSpec((OUT_TM, n), lambda i: (i, 0)), small, small],
        out_shape=[jax.ShapeDtypeStruct((TOKENS, n), F32),
                   jax.ShapeDtypeStruct((TOKENS, LANES), I32),
                   jax.ShapeDtypeStruct((TOKENS, LANES), F32)],
        compiler_params=_cparams(("parallel",)),
    )(a, w.astype(BF16), resid, ln_ffn_g.reshape(1, D_MODEL), ws, bs)


def _compress_kernel(x_ref, pe_ref, w1_ref, w2_ref, o_ref):
    half = CMP_STRIDE * NSA_HEAD_DIM
    x = x_ref[0, 0, 0]
    w1 = w1_ref[0].astype(BF16)
    top = _dot(x, w1[:half])
    bot = _dot(x, w1[half:])
    pe = jnp.broadcast_to(pe_ref[0], (8, 2 * half)).astype(BF16)
    pe_bias = _dot(pe, w1)[0:1]
    hid = top + pltpu.roll(bot, bot.shape[0] - 1, 0) + pe_bias
    act = jax.nn.gelu(hid)
    o_ref[0, 0, 0] = _dot(act.astype(BF16), w2_ref[0].astype(BF16))


def _compress(kv_chunks, pe, w1, w2):
    n_chunk = SEQ // CMP_STRIDE
    feat = CMP_STRIDE * NSA_HEAD_DIM
    return pl.pallas_call(
        _compress_kernel, name="nsa_compress", grid=(2, NSA_KV_HEADS, BATCH),
        in_specs=[pl.BlockSpec((1, 1, 1, n_chunk, feat), lambda a, g, b: (a, g, b, 0, 0)),
                  pl.BlockSpec((1, 1, 2 * feat), lambda a, g, b: (a, 0, 0)),
                  pl.BlockSpec((1, 2 * feat, CMP_HIDDEN), lambda a, g, b: (a, 0, 0)),
                  pl.BlockSpec((1, CMP_HIDDEN, NSA_HEAD_DIM), lambda a, g, b: (a, 0, 0))],
        out_specs=pl.BlockSpec((1, 1, 1, n_chunk, NSA_HEAD_DIM), lambda a, g, b: (a, g, b, 0, 0)),
        out_shape=jax.ShapeDtypeStruct((2, NSA_KV_HEADS, BATCH, n_chunk, NSA_HEAD_DIM), F32),
        compiler_params=_cparams(("parallel", "parallel", "parallel")),
    )(kv_chunks, pe.reshape(2, 1, 2 * feat), w1, w2)


def _nsa_attn_kernel(q_ref, ks_ref, vs_ref, kw_ref, vw_ref, kc_ref, vc_ref, gate_ref, o_ref,
                     vst_scr, vwt_scr, sel_scr, acc_scr, sc_scr):
    qi = pl.program_id(2)
    tq = ATT_TQ
    tk = ATT_TK
    dh = NSA_HEAD_DIM
    r_heads = NSA_Q_PER_KV
    n_kt = SEQ // tk
    scale = dh ** -0.5 * LOG2_E

    @pl.when(qi == 0)
    def _():
        ones = jnp.ones((ATT_ONES, tk), BF16)
        for kt in range(n_kt):
            rows = slice(kt * tk, (kt + 1) * tk)
            vst_scr[kt, 0:dh, :] = vs_ref[0, rows, :].astype(F32).T.astype(BF16)
            vwt_scr[kt, 0:dh, :] = vw_ref[0, rows, :].astype(F32).T.astype(BF16)
            vst_scr[kt, dh:dh + ATT_ONES, :] = ones
            vwt_scr[kt, dh:dh + ATT_ONES, :] = ones

    q_s = jnp.concatenate([q_ref[r].astype(F32).T for r in range(r_heads)], axis=1) * scale
    q_hi = q_s.astype(BF16)
    q_t = jnp.concatenate([q_hi, (q_s - q_hi.astype(F32)).astype(BF16)], axis=0)

    def qk(k):
        return _dot(jnp.concatenate([k, k], axis=1), q_t)
    n_cp = SEQ // CMP_STRIDE
    sub = lax.broadcasted_iota(I32, (n_cp, tq), 0)
    t_pos = qi * tq + lax.broadcasted_iota(I32, (n_cp, tq), 1)

    def tile4(a):
        return jnp.concatenate([a] * r_heads, axis=1)

    kc = kc_ref[0, 0, 0].astype(BF16)
    vc = vc_ref[0, 0, 0].astype(BF16)
    ok_c = jnp.where(sub * CMP_STRIDE + CMP_BLOCK - 1 <= t_pos, jnp.where(sub < N_CMP, 1.0, 0.0), 0.0)
    ok_c4 = tile4(ok_c)
    s_c = qk(kc) + (ok_c4 - 1.0) * (-NEG_INF)
    e_c = jnp.exp2(s_c - jnp.max(s_c, axis=0, keepdims=True))
    p_c = (e_c / jnp.sum(e_c, axis=0, keepdims=True)) * ok_c4
    o_cmp = _dot_tn(vc, p_c.astype(BF16))
    p_sum = p_c[:, 0:tq]
    for r in range(1, r_heads):
        p_sum = p_sum + p_c[:, r * tq:(r + 1) * tq]

    blk_row = lax.broadcasted_iota(I32, (LANES, LANES), 0)
    cmp_col = lax.broadcasted_iota(I32, (LANES, LANES), 1)
    s_start = blk_row * SLC_BLOCK
    c_start = cmp_col * CMP_STRIDE
    ov_t = jnp.maximum(jnp.minimum(c_start + CMP_BLOCK, s_start + SLC_BLOCK)
                       - jnp.maximum(c_start, s_start), 0).astype(F32) / CMP_BLOCK
    ov_t = jnp.where(blk_row < N_SLC, ov_t, 0.0).astype(BF16)
    imp = _dot_split_rhs(ov_t, p_sum)[0:N_SLC]
    j_blk = lax.broadcasted_iota(I32, (N_SLC, tq), 0)
    dist = (qi * tq + lax.broadcasted_iota(I32, (N_SLC, tq), 1)) // SLC_BLOCK - j_blk
    imp = jnp.where(j_blk == 0, SLC_FORCE, jnp.where(dist < 0, imp, jnp.where(dist < SLC_LOCAL, SLC_FORCE, imp)))
    imp = jnp.where(dist >= 0, imp, -jnp.inf)
    cnt = jnp.zeros((N_SLC, tq), I32)
    for k in range(N_SLC):
        row_k = imp[k:k + 1, :]
        tie = jnp.where(j_blk > k, 1, 0)
        cnt = cnt + jnp.where(row_k > imp, 1, jnp.where(row_k == imp, tie, 0))
    sel = jnp.where(cnt < min(SLC_TOPK, N_SLC), 1.0, 0.0)
    for j in range(N_SLC):
        sel_scr[8 * j:8 * j + 8, :] = jnp.broadcast_to(sel[j:j + 1, :], (8, tq))

    def scores(k, ok):
        return qk(k) + tile4((ok - 1.0) * (-NEG_INF))

    def weighted_values(vt_scr, kt0, pr, n_tiles):
        out = None
        for u in range(n_tiles):
            term = _dot(vt_scr[kt0 + u], pr[u * tk:(u + 1) * tk].astype(BF16))
            out = term if out is None else out + term
        return out

    n_ct = ATT_CHUNK_TILES
    ck = n_ct * tk
    sub_c = lax.broadcasted_iota(I32, (ck, tq), 0)
    t_pos_c = qi * tq + lax.broadcasted_iota(I32, (ck, tq), 1)
    blocks_per_chunk = ck // SLC_BLOCK
    acc_scr[...] = jnp.zeros(acc_scr.shape, F32)

    def chunk_scores(c):
        c = jnp.minimum(c, SEQ // ck - 1)
        start = pl.multiple_of(c * ck, ck)
        k = ks_ref[0, pl.ds(start, ck), :]
        rows8 = sel_scr[pl.ds(pl.multiple_of(c * (8 * blocks_per_chunk), 8 * blocks_per_chunk),
                              8 * blocks_per_chunk), :]
        picked = jnp.concatenate(
            [rows8[8 * u:8 * u + 8] for u in range(blocks_per_chunk) for _ in range(SLC_BLOCK // 8)], axis=0)
        return scores(k, jnp.where(start + sub_c <= t_pos_c, picked, 0.0))

    sc_scr[...] = chunk_scores(0)

    def slc_body(c, m_old):
        sc = sc_scr[...]
        sc_next = chunk_scores(c + 1)
        m_new = jnp.maximum(m_old, jnp.max(sc, axis=0, keepdims=True))
        alpha = jnp.exp2(m_old - m_new)
        pr = jnp.exp2(sc - m_new)
        acc_scr[...] = alpha * acc_scr[...] + weighted_values(vst_scr, c * n_ct, pr, n_ct)
        sc_scr[...] = sc_next
        return m_new

    lax.fori_loop(0, ((qi + 1) * tq + ck - 1) // ck, slc_body, jnp.full((1, r_heads * tq), NEG_INF, F32))
    acc = acc_scr[...]
    o_slc = acc[0:dh] / acc[dh:dh + 1]

    n_wt = (WINDOW + tq) // tk
    kt0 = jnp.maximum(qi * (tq // tk) - WINDOW // tk, 0)
    w_start = pl.multiple_of(kt0 * tk, tk)
    key_w = w_start + lax.broadcasted_iota(I32, (n_wt * tk, tq), 0)
    t_pos_w = qi * tq + lax.broadcasted_iota(I32, (n_wt * tk, tq), 1)
    ok_w = jnp.where(key_w <= t_pos_w, jnp.where(key_w > t_pos_w - WINDOW, 1.0, 0.0), 0.0)
    sc_w = scores(kw_ref[0, pl.ds(w_start, n_wt * tk), :], ok_w)
    pr_w = jnp.exp2(sc_w - jnp.max(sc_w, axis=0, keepdims=True))
    acc_w = weighted_values(vwt_scr, kt0, pr_w, n_wt)
    o_win = acc_w[0:dh] / acc_w[dh:dh + 1]

    gate = jax.nn.sigmoid(gate_ref[0])
    for r in range(r_heads):
        cols = slice(r * tq, (r + 1) * tq)
        o = (gate[3 * r:3 * r + 1] * o_cmp[:, cols] + gate[3 * r + 1:3 * r + 2] * o_slc[:, cols]
             + gate[3 * r + 2:3 * r + 3] * o_win[:, cols])
        o_ref[:, r * dh:(r + 1) * dh] = o.T.astype(BF16)


def _nsa_attention(heads, kc_vc, gates):
    tq = ATT_TQ
    nq = SEQ // tq
    r = NSA_Q_PER_KV
    g_heads = NSA_KV_HEADS
    q_spec = pl.BlockSpec((r, tq, NSA_HEAD_DIM), lambda b, g, i: (g, b * nq + i, 0))

    def kv_spec(first_head):
        return pl.BlockSpec((1, SEQ, NSA_HEAD_DIM), lambda b, g, i: (first_head + g, b, 0))

    first = NSA_HEADS
    specs = [q_spec,
             kv_spec(first + 2 * g_heads), kv_spec(first + 3 * g_heads),
             kv_spec(first + 4 * g_heads), kv_spec(first + 5 * g_heads),
             pl.BlockSpec((1, 1, 1, SEQ // CMP_STRIDE, NSA_HEAD_DIM), lambda b, g, i: (0, g, b, 0, 0)),
             pl.BlockSpec((1, 1, 1, SEQ // CMP_STRIDE, NSA_HEAD_DIM), lambda b, g, i: (1, g, b, 0, 0)),
             pl.BlockSpec((1, 3 * r, tq), lambda b, g, i: (g, 0, b * nq + i))]
    vt_shape = (SEQ // ATT_TK, NSA_HEAD_DIM + ATT_ONES, ATT_TK)
    return pl.pallas_call(
        _nsa_attn_kernel, name="nsa_attn", grid=(BATCH, g_heads, nq),
        in_specs=specs,
        out_specs=pl.BlockSpec((tq, r * NSA_HEAD_DIM), lambda b, g, i: (b * nq + i, g)),
        out_shape=jax.ShapeDtypeStruct((TOKENS, NSA_QD), BF16),
        scratch_shapes=[pltpu.VMEM(vt_shape, BF16), pltpu.VMEM(vt_shape, BF16),
                        pltpu.VMEM((8 * N_SLC, tq), F32),
                        pltpu.VMEM((NSA_HEAD_DIM + ATT_ONES, r * tq), F32),
                        pltpu.VMEM((ATT_CHUNK_TILES * ATT_TK, r * tq), F32)],
        compiler_params=_cparams(("arbitrary", "arbitrary", "arbitrary")),
    )(heads, heads, heads, heads, heads, kc_vc, kc_vc, gates)


def _ssd_chunk_kernel(zx_ref, dtc_ref, dtr_ref, cw_ref, cb_ref, dtb_c_ref, alog_c_ref, dtb_r_ref, alog_r_ref,
                      dskip_ref, ng_ref, echan_ref, o_ref,
                      prev_scr, acum_r_scr, st_scr, fac_scr):
    chunk = pl.program_id(1)
    L = SSD_CHUNK
    W = SSD_GROUP_W
    hpg = SSD_HEADS_PER_GROUP
    n_xt = W // LANES
    x0 = SSD_D_INNER // LANES
    b0 = 2 * SSD_D_INNER // LANES
    c0 = b0 + SSD_GROUPS
    cb0 = SSD_D_INNER // LANES
    cc0 = cb0 + SSD_GROUPS

    @pl.when(chunk == 0)
    def _():
        prev_scr[...] = jnp.zeros(prev_scr.shape, F32)
        st_scr[...] = jnp.zeros(st_scr.shape, F32)

    dt_c = jax.nn.softplus(dtc_ref[...] + dtb_c_ref[...])
    adt_c = dt_c * (-jnp.exp(alog_c_ref[...]))
    dt_r = jax.nn.softplus(dtr_ref[...] + dtb_r_ref[...])
    adt_r = dt_r * (-jnp.exp(alog_r_ref[...]))
    row = lax.broadcasted_iota(I32, (L, L), 0)
    col = lax.broadcasted_iota(I32, (L, L), 1)
    causal = row >= col
    tri = jnp.where(causal, 1.0, 0.0).astype(BF16)
    tri_t = jnp.where(col >= row, 1.0, 0.0).astype(BF16)
    acum_c = _dot_split_rhs(tri, adt_c)
    acum_r_scr[...] = _dot_split_lhs(adt_r, tri_t)
    a_last = acum_c[L - 1:L, :]
    fac = jnp.concatenate([dt_c, jnp.exp(acum_c), jnp.exp(a_last - acum_c),
                           jnp.broadcast_to(jnp.exp(a_last), (SSD_FAC_PAD, LANES))], axis=0)
    fac_hi = fac.astype(BF16)
    fac_scr[0] = fac_hi
    fac_scr[1] = (fac - fac_hi.astype(F32)).astype(BF16)
    lane_w = lax.broadcasted_iota(I32, (L, LANES), 1)
    first_half = lane_w < SSD_HEAD_DIM

    def tiles(ref, first, n):
        return jnp.concatenate([ref[first + q] for q in range(n)], axis=1)

    def group_body(g, carry):
        e_chan = echan_ref[g]
        ex = _dot(fac_scr[0, 0:3 * L], e_chan)
        dt_x, ea_x, sd_x = ex[0:L], ex[L:2 * L], ex[2 * L:3 * L]
        cd_x = (_dot(fac_scr[0, 3 * L:3 * L + SSD_FAC_PAD], e_chan)
                + _dot(fac_scr[1, 3 * L:3 * L + SSD_FAC_PAD], e_chan))[0:1]

        def conv_silu(zx_first, conv_first, n):
            cur = tiles(zx_ref, zx_first, n)
            cur_f = cur.astype(F32)
            tail = tiles(prev_scr, conv_first, n)
            row8 = lax.broadcasted_iota(I32, (8, n * LANES), 0)
            w = tiles(cw_ref, conv_first, n)
            acc = jnp.broadcast_to(tiles(cb_ref, conv_first, n), (L, n * LANES))
            for k in range(SSD_CONV):
                back = SSD_CONV - 1 - k
                if back == 0:
                    xk = cur_f
                else:
                    rolled = pltpu.roll(cur_f, back, 0)
                    head = jnp.where(row8 < back, pltpu.roll(tail, back, 0), rolled[0:8])
                    xk = jnp.concatenate([head, rolled[8:]], axis=0)
                acc = acc + xk * w[k:k + 1, :]
            for q in range(n):
                prev_scr[conv_first + q] = cur_f[L - 8:L, q * LANES:(q + 1) * LANES]
            return jax.nn.silu(acc)

        xs = conv_silu(x0 + n_xt * g, n_xt * g, n_xt)
        bm = conv_silu(b0 + g, cb0 + g, 1)
        cm = conv_silu(c0 + g, cc0 + g, 1)

        xdt = xs * dt_x
        cb = jnp.where(causal, _dot_nt(cm.astype(BF16), bm.astype(BF16)), 0.0)
        y_parts = []
        for pair in range(hpg // 2):
            xd = xdt[:, pair * LANES:(pair + 1) * LANES]
            m_pair = []
            for sub in range(2):
                a_row = jnp.broadcast_to(acum_r_scr[pl.ds(g * hpg + 2 * pair + sub, 1), :], (L, L))
                seg = jnp.minimum(a_row.T - a_row, 0.0)
                m_pair.append((cb * jnp.exp(seg)).astype(BF16))
            x_pair = jnp.concatenate([jnp.where(first_half, xd, 0.0), jnp.where(first_half, 0.0, xd)], axis=0)
            y_parts.append(_dot(jnp.concatenate(m_pair, axis=1), x_pair.astype(BF16)))
        y_diag = jnp.concatenate(y_parts, axis=1)

        st = st_scr[g]
        y_off = _dot(cm.astype(BF16), st.astype(BF16)) * ea_x
        st_scr[g] = st * cd_x + _dot_tn(bm.astype(BF16), (xdt * sd_x).astype(BF16))

        y = y_diag + y_off + xs * tiles(dskip_ref, n_xt * g, n_xt)
        y = y * jax.nn.silu(tiles(zx_ref, n_xt * g, n_xt).astype(F32))
        y = y * lax.rsqrt(jnp.mean(y * y, axis=-1, keepdims=True) + NORM_EPS)
        y = y * tiles(ng_ref, n_xt * g, n_xt)
        for q in range(n_xt):
            o_ref[n_xt * g + q] = y[:, q * LANES:(q + 1) * LANES].astype(BF16)
        return carry

    lax.fori_loop(0, SSD_GROUPS, group_body, 0)


def _ssd_chunks(zx_tiles, dt_small, dt_small_t, conv_w, conv_b, dt_bias, a_log, d_skip, norm_g):
    L = SSD_CHUNK
    nc = SEQ // L
    n_zx = SSD_MAIN // LANES
    n_conv = SSD_CONV_CH // LANES
    n_inner = SSD_D_INNER // LANES
    hpg = SSD_HEADS_PER_GROUP

    def pad_heads(v):
        return jnp.zeros((LANES,), F32).at[:SSD_HEADS].set(v)

    dtb = pad_heads(dt_bias)
    alog = pad_heads(a_log)
    cw = conv_w.reshape(SSD_CONV, n_conv, LANES).transpose(1, 0, 2)
    cb = conv_b.reshape(n_conv, 1, LANES)
    d_chan = jnp.repeat(d_skip, SSD_HEAD_DIM).reshape(n_inner, 1, LANES)
    ng = norm_g.reshape(n_inner, 1, LANES)
    head =jnp.arange(LANES, dtype=I32)[None, :, None]
    grp = jnp.arange(SSD_GROUPS, dtype=I32)[:, None, None]
    e_chan = (head == grp * hpg + jnp.arange(SSD_GROUP_W, dtype=I32)[None, None, :] // SSD_HEAD_DIM).astype(BF16)
    row = lambda b, c: b * nc + c
    const3 = lambda b, c: (0, 0, 0)
    const2 = lambda b, c: (0, 0)
    in_specs = [
        pl.BlockSpec((n_zx, L, LANES), lambda b, c: (0, row(b, c), 0)),
        pl.BlockSpec((L, LANES), lambda b, c: (row(b, c), 0)),
        pl.BlockSpec((LANES, L), lambda b, c: (0, row(b, c))),
        pl.BlockSpec((n_conv, SSD_CONV, LANES), const3),
        pl.BlockSpec((n_conv, 1, LANES), const3),
        pl.BlockSpec((1, LANES), const2), pl.BlockSpec((1, LANES), const2),
        pl.BlockSpec((LANES, 1), const2), pl.BlockSpec((LANES, 1), const2),
        pl.BlockSpec((n_inner, 1, LANES), const3),
        pl.BlockSpec((n_inner, 1, LANES), const3),
        pl.BlockSpec((SSD_GROUPS, LANES, SSD_GROUP_W), const3),
    ]
    return pl.pallas_call(
        _ssd_chunk_kernel, name="ssd_chunks", grid=(BATCH, nc),
        in_specs=in_specs,
        out_specs=pl.BlockSpec((n_inner, L, LANES), lambda b, c: (0, row(b, c), 0)),
        out_shape=jax.ShapeDtypeStruct((n_inner, TOKENS, LANES), BF16),
        scratch_shapes=[pltpu.VMEM((n_conv, 8, LANES), F32),
                        pltpu.VMEM((LANES, L), F32),
                        pltpu.VMEM((SSD_GROUPS, SSD_D_STATE, SSD_GROUP_W), F32),
                        pltpu.VMEM((2, 3 * L + SSD_FAC_PAD, LANES), BF16)],
        compiler_params=_cparams(("arbitrary", "arbitrary")),
    )(zx_tiles, dt_small, dt_small_t, cw, cb, dtb.reshape(1, LANES), alog.reshape(1, LANES),
      dtb.reshape(LANES, 1), alog.reshape(LANES, 1), d_chan, ng, e_chan)


def _gather_rows(src_hbm, idx_ref, base, dst, sem, n_rows):
    def body(b, carry):
        for u in range(GATHER_UNROLL):
            r = b * GATHER_UNROLL + u
            tok = idx_ref[base + r]
            pltpu.make_async_copy(src_hbm.at[pl.ds(tok, 1), :], dst.at[pl.ds(r, 1), :], sem).start()
        return carry

    lax.fori_loop(0, n_rows // GATHER_UNROLL, body, 0)


def _moe_ffn_kernel(te_ref, pos_ref, nact_ref, pad_ref, wp_ref, x_hbm, g_ref, wg_hbm, wu_hbm, wd_hbm, y_ref,
                    tok_ref, buf, sem, wg_f, wu_f, wd_f, wsem, wg_bf, wu_bf, wd_bf, *, layer):
    i = pl.program_id(0)
    n_act = nact_ref[0]
    tm = MOE_TM
    slot = i % 2
    run_start = wp_ref[i] == 1
    w_slot = wp_ref[MOE_TILES + i]
    next_expert = wp_ref[2 * MOE_TILES + i]
    n_rows = pl.multiple_of(wp_ref[3 * MOE_TILES + i], GATHER_UNROLL)

    def gather(tile, s):
        _gather_rows(x_hbm, tok_ref, tile * tm, buf.at[s], sem.at[s], wp_ref[3 * MOE_TILES + tile])

    def weight_copies(e, s):
        return (pltpu.make_async_copy(wg_hbm.at[layer, e], wg_f.at[s], wsem.at[s, 0]),
                pltpu.make_async_copy(wu_hbm.at[layer, e], wu_f.at[s], wsem.at[s, 1]),
                pltpu.make_async_copy(wd_hbm.at[layer, e], wd_f.at[s], wsem.at[s, 2]))

    @pl.when(i == 0)
    def _():
        for c in weight_copies(te_ref[0], 0):
            c.start()
        buf[...] = jnp.zeros(buf.shape, F32)
        def clear(q, carry):
            tok_ref[q] = 0
            return carry

        def clear_padding(e, carry):
            lax.fori_loop(pad_ref[e], pad_ref[MOE_EXPERTS + e], clear, 0)
            return carry

        lax.fori_loop(0, MOE_EXPERTS, clear_padding, 0)
        for k in range(MOE_TOPK):
            def place(t, carry, k=k):
                tok_ref[pos_ref[k * TOKENS + t]] = t
                return carry

            lax.fori_loop(0, TOKENS, place, 0, unroll=GATHER_UNROLL)
        gather(0, 0)

    @pl.when(run_start)
    def _():
        for c in weight_copies(te_ref[i], w_slot):
            c.wait()

        @pl.when(next_expert >= 0)
        def _():
            for c in weight_copies(next_expert, 1 - w_slot):
                c.start()

        wg_bf[...] = wg_f[w_slot].astype(BF16)
        wu_bf[...] = wu_f[w_slot].astype(BF16)
        wd_bf[...] = wd_f[w_slot].astype(BF16)

    @pl.when(i >= n_act)
    def _():
        y_ref[...] = jnp.zeros(y_ref.shape, F32)

    @pl.when(i < n_act)
    def _():
        pltpu.make_async_copy(x_hbm.at[pl.ds(0, n_rows), :], buf.at[slot, pl.ds(0, n_rows), :],
                              sem.at[slot]).wait()

        @pl.when(i + 1 < n_act)
        def _():
            gather(i + 1, 1 - slot)

        h = _rms(buf[slot], g_ref[...]).astype(BF16)
        act = jax.nn.silu(_dot(h, wg_bf[...])) * _dot(h, wu_bf[...])
        y_ref[...] = _dot(act.astype(BF16), wd_bf[...])


def _moe_ffn(x, g, w_gate, w_up, w_down, layer, tile_expert, pos_kmajor, n_active, pad_rows, weight_plan):
    any_spec = pl.BlockSpec(memory_space=pl.ANY)
    grid_spec = pltpu.PrefetchScalarGridSpec(
        num_scalar_prefetch=5, grid=(MOE_TILES,),
        in_specs=[any_spec, pl.BlockSpec((1, D_MODEL), lambda i, *_: (0, 0)), any_spec, any_spec, any_spec],
        out_specs=pl.BlockSpec((MOE_TM, D_MODEL), lambda i, *_: (i, 0)),
        scratch_shapes=[pltpu.SMEM((MOE_ROWS,), I32),
                        pltpu.VMEM((2, MOE_TM, D_MODEL), F32), pltpu.SemaphoreType.DMA((2,)),
                        pltpu.VMEM((2, D_MODEL, MOE_D_FF), F32), pltpu.VMEM((2, D_MODEL, MOE_D_FF), F32),
                        pltpu.VMEM((2, MOE_D_FF, D_MODEL), F32), pltpu.SemaphoreType.DMA((2, 3)),
                        pltpu.VMEM((D_MODEL, MOE_D_FF), BF16), pltpu.VMEM((D_MODEL, MOE_D_FF), BF16),
                        pltpu.VMEM((MOE_D_FF, D_MODEL), BF16)])
    return pl.pallas_call(
        functools.partial(_moe_ffn_kernel, layer=layer), name="moe_ffn", grid_spec=grid_spec,
        out_shape=jax.ShapeDtypeStruct((MOE_ROWS, D_MODEL), F32),
        compiler_params=_cparams(("arbitrary",)),
    )(tile_expert, pos_kmajor, n_active, pad_rows, weight_plan, x, g.reshape(1, D_MODEL), w_gate, w_up, w_down)


def _moe_combine_kernel(pos_ref, x_ref, cw_ref, g_ref, ws_ref, y_hbm, *refs, final_norm):
    if final_norm:
        o_ref, buf, sem = refs
    else:
        o_ref, hn_ref, small_ref, small_t_ref, buf, sem = refs
    i = pl.program_id(0)
    n = pl.num_programs(0)
    tm = CMB_TM
    slot = i % 2

    def issue(tile, s):
        for k in range(MOE_TOPK):
            _gather_rows(y_hbm, pos_ref, (k * (TOKENS // tm) + tile) * tm, buf.at[s, k], sem.at[s], tm)

    @pl.when(i == 0)
    def _():
        issue(0, 0)

    @pl.when(i + 1 < n)
    def _():
        issue(i + 1, 1 - slot)

    for k in range(MOE_TOPK):
        pltpu.make_async_copy(y_hbm.at[pl.ds(0, tm), :], buf.at[slot, k], sem.at[slot]).wait()
    cw = cw_ref[...]
    out = x_ref[...] + cw[:, 0:1] * buf[slot, 0] + cw[:, 1:2] * buf[slot, 1]
    y = _rms(out, g_ref[...])
    if final_norm:
        o_ref[...] = y
    else:
        o_ref[...] = out
        hn_ref[...] = y.astype(BF16)
        small = _dot_x3(y, ws_ref[...])
        small_ref[...] = small
        small_t_ref[...] = small.T


def _moe_combine(x, cw, y_sorted, pos_kmajor, g_norm, w_small, final_norm):
    ws = jnp.zeros((D_MODEL, LANES), F32)
    if not final_norm:
        ws = ws.at[:, :w_small.shape[1]].set(w_small)
    row = pl.BlockSpec((CMB_TM, D_MODEL), lambda i, pos: (i, 0))
    small = pl.BlockSpec((CMB_TM, LANES), lambda i, pos: (i, 0))
    x_shape = jax.ShapeDtypeStruct((TOKENS, D_MODEL), F32)
    if final_norm:
        out_specs, out_shape = row, x_shape
    else:
        out_specs = [row, row, small, pl.BlockSpec((LANES, CMB_TM), lambda i, pos: (0, i))]
        out_shape = [x_shape, jax.ShapeDtypeStruct((TOKENS, D_MODEL), BF16),
                     jax.ShapeDtypeStruct((TOKENS, LANES), F32), jax.ShapeDtypeStruct((LANES, TOKENS), F32)]
    grid_spec = pltpu.PrefetchScalarGridSpec(
        num_scalar_prefetch=1, grid=(TOKENS // CMB_TM,),
        in_specs=[row, small,
                  pl.BlockSpec((1, D_MODEL), lambda i, pos: (0, 0)),
                  pl.BlockSpec((D_MODEL, LANES), lambda i, pos: (0, 0)),
                  pl.BlockSpec(memory_space=pl.ANY)],
        out_specs=out_specs,
        scratch_shapes=[pltpu.VMEM((2, MOE_TOPK, CMB_TM, D_MODEL), F32), pltpu.SemaphoreType.DMA((2,))])
    return pl.pallas_call(
        functools.partial(_moe_combine_kernel, final_norm=final_norm), name="moe_combine", grid_spec=grid_spec,
        out_shape=out_shape,
        compiler_params=_cparams(("arbitrary",)),
    )(pos_kmajor, x, cw, g_norm.reshape(1, D_MODEL), ws, y_sorted)


def _moe_plan(eid):
    e = eid[:, :MOE_TOPK].reshape(-1)
    onehot = (e[:, None] == jnp.arange(MOE_EXPERTS, dtype=I32)[None, :]).astype(I32)
    csum = jnp.cumsum(onehot, axis=0)
    counts = csum[-1]
    padded = ((counts + MOE_TM - 1) // MOE_TM) * MOE_TM
    g_end = jnp.cumsum(padded)
    g_start = g_end - padded
    pos = jnp.sum(onehot * (g_start[None, :] + csum - 1), axis=1)
    n_active = (g_end[-1] // MOE_TM).astype(I32)
    tile_start = jnp.arange(MOE_TILES, dtype=I32) * MOE_TM
    te = jnp.sum((g_end[None, :] <= tile_start[:, None]).astype(I32), axis=1)
    last = jnp.max(jnp.where(counts > 0, jnp.arange(MOE_EXPERTS, dtype=I32), 0))
    tile_expert = jnp.minimum(te, last)
    pos_kmajor = pos.reshape(TOKENS, MOE_TOPK).T.reshape(-1)
    pad_rows = jnp.concatenate([g_start + counts, g_end])
    experts = jnp.arange(MOE_EXPERTS, dtype=I32)
    run_start = jnp.concatenate([jnp.ones((1,), I32), (tile_expert[1:] != tile_expert[:-1]).astype(I32)])
    w_slot = (jnp.cumsum(run_start) - 1) % 2
    later = (experts[None, :] > experts[:, None]) & (counts > 0)[None, :]
    next_nonempty = jnp.min(jnp.where(later, experts[None, :], MOE_EXPERTS), axis=1)
    next_nonempty = jnp.where(next_nonempty == MOE_EXPERTS, -1, next_nonempty)
    real_rows = jnp.clip((g_start + counts)[tile_expert] - tile_start, 0, MOE_TM)
    real_rows = jnp.where(tile_start < g_end[-1], real_rows, 0)
    gather_rows = jnp.minimum(-(-real_rows // GATHER_UNROLL) * GATHER_UNROLL, MOE_TM)
    weight_plan = jnp.concatenate([run_start, w_slot, next_nonempty[tile_expert], gather_rows]).astype(I32)
    return tile_expert, n_active.reshape(1), pos_kmajor, pad_rows, weight_plan


def _hier_moe_add(x, eid, cw, ln_g, w_gate, w_up, w_down, layer, g_norm, w_small, final_norm):
    tile_expert, n_active, pos_kmajor, pad_rows, weight_plan = _moe_plan(eid)
    y_sorted = _moe_ffn(x, ln_g, w_gate, w_up, w_down, layer, tile_expert, pos_kmajor, n_active, pad_rows,
                        weight_plan)
    return _moe_combine(x, cw, y_sorted, pos_kmajor, g_norm, w_small, final_norm)


def _rope_tables():
    pos = jnp.arange(SEQ, dtype=F32)
    inv = 1.0 / (ROPE_THETA ** (jnp.arange(0, NSA_HEAD_DIM, 2, dtype=F32) / NSA_HEAD_DIM))
    ang = pos[:, None] * inv[None, :]
    cos, sin = jnp.cos(ang), jnp.sin(ang)
    return jnp.concatenate([cos, cos], axis=1), jnp.concatenate([-sin, sin], axis=1)


def _nsa_mixer(hn, g_lin_t, w_in, cmp_pe, cmp_w1, cmp_w2):
    w_in_t = w_in.T
    cos_full, sin_signed = _rope_tables()
    heads = _nsa_inproj(hn, w_in_t, cos_full, sin_signed)
    first_c = NSA_HEADS
    kv_c = heads[first_c:first_c + 2 * NSA_KV_HEADS]
    kv_chunks = kv_c.reshape(2, NSA_KV_HEADS, BATCH, SEQ // CMP_STRIDE, CMP_STRIDE * NSA_HEAD_DIM)
    kc_vc = _compress(kv_chunks, cmp_pe, cmp_w1, cmp_w2)
    gates_t = g_lin_t[:NSA_GATES].reshape(NSA_KV_HEADS, 3 * NSA_Q_PER_KV, TOKENS)
    return _nsa_attention(heads, kc_vc, gates_t)


def _ssd_mixer(hn, dt_small, dt_small_t, w_in, conv_w, conv_b, dt_bias, a_log, d_skip, norm_g):
    zx_tiles = _matmul_tiles(hn, w_in.T, SSD_MAIN, 1024)
    return _ssd_chunks(zx_tiles, dt_small, dt_small_t, conv_w, conv_b, dt_bias, a_log, d_skip, norm_g)


def kernel(x, ln_mix, ln_ffn, ln_final, nsa_w_in, nsa_cmp_pe, nsa_cmp_w1, nsa_cmp_w2, nsa_w_out,
           ssd_w_in, ssd_conv_w, ssd_conv_b, ssd_dt_bias, ssd_a_log, ssd_d, ssd_norm, ssd_w_out,
           moe_w_group, moe_b_group, moe_w_expert, moe_b_expert, moe_w_gate, moe_w_up, moe_w_down):
    def small_weight(i):
        if i % N_MIXERS == 0:
            return nsa_w_in[i // N_MIXERS].T[NSA_MAIN:].T
        return ssd_w_in[i // N_MIXERS].T[SSD_MAIN:].T

    h = x.reshape(TOKENS, D_MODEL)
    hn, small, small_t = _norm_small(h, ln_mix[0], small_weight(0), transposed=True)
    for i in range(DEPTH):
        j = i // N_MIXERS
        last = i == DEPTH - 1
        if i % N_MIXERS == 0:
            mix = _nsa_mixer(hn, small_t, nsa_w_in[j], nsa_cmp_pe[j], nsa_cmp_w1[j], nsa_cmp_w2[j])
            w_out = nsa_w_out[j]
        else:
            mix = _ssd_mixer(hn, small, small_t, ssd_w_in[j], ssd_conv_w[j], ssd_conv_b[j], ssd_dt_bias[j],
                             ssd_a_log[j], ssd_d[j], ssd_norm[j])
            w_out = ssd_w_out[j]
        h, eid, cw = _outproj_route(mix, w_out, h, ln_ffn[i], moe_w_group[i], moe_b_group[i],
                                    moe_w_expert[i], moe_b_expert[i])
        out = _hier_moe_add(h, eid, cw, ln_ffn[i], moe_w_gate, moe_w_up, moe_w_down, i,
                            ln_final if last else ln_mix[i + 1], None if last else small_weight(i + 1), last)
        if last:
            h = out
        else:
            h, hn, small, small_t = out
    return h.reshape(BATCH, SEQ, D_MODEL)
```

```python
import functools

import jax
import jax.numpy as jnp
from jax import lax
from jax.experimental import pallas as pl
from jax.experimental.pallas import tpu as pltpu

F32 = jnp.float32
BF16 = jnp.bfloat16
I32 = jnp.int32

D_MODEL = 2048
BATCH = 4
SEQ = 2048
TOKENS = BATCH * SEQ
DEPTH = 2
N_MIXERS = 2
NORM_EPS = 1e-6
NEG_INF = -1e30
LOG2_E = 1.4426950408889634
ROPE_THETA = 10000.0

NSA_HEADS = 16
NSA_KV_HEADS = 4
NSA_HEAD_DIM = D_MODEL // NSA_HEADS
NSA_Q_PER_KV = NSA_HEADS // NSA_KV_HEADS
CMP_BLOCK = 32
CMP_STRIDE = 16
CMP_HIDDEN = 256
N_CMP = (SEQ - CMP_BLOCK) // CMP_STRIDE + 1
SLC_BLOCK = 64
SLC_TOPK = 16
SLC_LOCAL = 2
SLC_FORCE = 1e4
N_SLC = SEQ // SLC_BLOCK
WINDOW = 512
NSA_QD = NSA_HEADS * NSA_HEAD_DIM
NSA_KVD = NSA_KV_HEADS * NSA_HEAD_DIM
NSA_MAIN = NSA_QD + 6 * NSA_KVD
NSA_GATES = 3 * NSA_HEADS

SSD_D_INNER = 2 * D_MODEL
SSD_HEAD_DIM = 64
SSD_HEADS = SSD_D_INNER // SSD_HEAD_DIM
SSD_GROUPS = 8
SSD_HEADS_PER_GROUP = SSD_HEADS // SSD_GROUPS
SSD_D_STATE = 128
SSD_CONV = 4
SSD_CHUNK = 128
SSD_GROUP_W = SSD_D_INNER // SSD_GROUPS
SSD_BC = SSD_GROUPS * SSD_D_STATE
SSD_CONV_CH = SSD_D_INNER + 2 * SSD_BC
SSD_MAIN = SSD_D_INNER + SSD_CONV_CH
SSD_FAC_PAD = 16

MOE_GROUPS = 4
MOE_EPG = 8
MOE_EXPERTS = MOE_GROUPS * MOE_EPG
MOE_TOPK = 2
MOE_D_FF = 512

LANES = 128
VMEM_LIMIT = 56 * 1024 * 1024

NORM_TM = 256
MM_TM = 1024
OUT_TM = 256
ATT_TQ = 256
ATT_TK = 256
ATT_CHUNK_TILES = 2
ATT_ONES = 16
MOE_TM = 256
MOE_TILES = (TOKENS * MOE_TOPK) // MOE_TM + MOE_EXPERTS
MOE_ROWS = MOE_TILES * MOE_TM
CMB_TM = 128
GATHER_UNROLL = 8


def _cparams(sem):
    return pltpu.CompilerParams(dimension_semantics=sem, vmem_limit_bytes=VMEM_LIMIT)


def _split3(x):
    hi = x.astype(BF16)
    r1 = x - hi.astype(F32)
    mid = r1.astype(BF16)
    lo = (r1 - mid.astype(F32)).astype(BF16)
    return hi, mid, lo


def _dot(a, b):
    return jnp.dot(a, b, preferred_element_type=F32)


def _dot_nt(a, b):
    return lax.dot_general(a, b, (((1,), (1,)), ((), ())), preferred_element_type=F32)


def _dot_tn(a, b):
    return lax.dot_general(a, b, (((0,), (0,)), ((), ())), preferred_element_type=F32)


def _dot_split_lhs(x, m_bf16):
    hi, mid, lo = _split3(x)
    return _dot(hi, m_bf16) + _dot(mid, m_bf16) + _dot(lo, m_bf16)


def _dot_split_rhs(m_bf16, x):
    hi, mid, lo = _split3(x)
    return _dot(m_bf16, hi) + _dot(m_bf16, mid) + _dot(m_bf16, lo)


def _dot_x3(a, w):
    a_hi = a.astype(BF16)
    a_lo = (a - a_hi.astype(F32)).astype(BF16)
    w_hi = w.astype(BF16)
    w_lo = (w - w_hi.astype(F32)).astype(BF16)
    n = w.shape[1]
    both = _dot(a_hi, jnp.concatenate([w_hi, w_lo], axis=1))
    return both[:, :n] + both[:, n:] + _dot(a_lo, w_hi)


def _rms(x, g):
    y = x * lax.rsqrt(jnp.mean(x * x, axis=-1, keepdims=True) + NORM_EPS)
    return y * g


def _norm_small_kernel(x_ref, g_ref, ws_ref, hn_ref, small_ref, small_t_ref):
    y = _rms(x_ref[...], g_ref[...])
    hn_ref[...] = y.astype(BF16)
    small = _dot_x3(y, ws_ref[...])
    small_ref[...] = small
    small_t_ref[...] = small.T


def _norm_small(x, g, w_small):
    n = w_small.shape[1]
    ws = jnp.zeros((D_MODEL, LANES), F32).at[:, :n].set(w_small)
    return pl.pallas_call(
        _norm_small_kernel, name="norm_small", grid=(TOKENS // NORM_TM,),
        in_specs=[pl.BlockSpec((NORM_TM, D_MODEL), lambda i: (i, 0)),
                  pl.BlockSpec((1, D_MODEL), lambda i: (0, 0)),
                  pl.BlockSpec((D_MODEL, LANES), lambda i: (0, 0))],
        out_specs=[pl.BlockSpec((NORM_TM, D_MODEL), lambda i: (i, 0)),
                   pl.BlockSpec((NORM_TM, LANES), lambda i: (i, 0)),
                   pl.BlockSpec((LANES, NORM_TM), lambda i: (0, i))],
        out_shape=[jax.ShapeDtypeStruct((TOKENS, D_MODEL), BF16),
                   jax.ShapeDtypeStruct((TOKENS, LANES), F32),
                   jax.ShapeDtypeStruct((LANES, TOKENS), F32)],
        compiler_params=_cparams(("parallel",)),
    )(x, g.reshape(1, D_MODEL), ws)


def _route(x, g, ws, bias):
    y = _rms(x, g)
    logits = _dot_x3(y, ws) + bias
    lane = lax.broadcasted_iota(I32, logits.shape, 1)
    big = jnp.int32(LANES)
    neg = -jnp.inf
    gl = jnp.where(lane < MOE_GROUPS, logits, neg)
    gmax = jnp.max(gl, axis=-1, keepdims=True)
    gsum = jnp.sum(jnp.exp(gl - gmax), axis=-1, keepdims=True)
    g_w = 1.0 / gsum
    g_sel = jnp.min(jnp.where(gl == gmax, lane, big), axis=-1, keepdims=True)
    lo = MOE_GROUPS + g_sel * MOE_EPG
    el = jnp.where((lane >= lo) & (lane < lo + MOE_EPG), logits, neg)
    v1 = jnp.max(el, axis=-1, keepdims=True)
    i1 = jnp.min(jnp.where(el == v1, lane, big), axis=-1, keepdims=True)
    el2 = jnp.where(lane == i1, neg, el)
    v2 = jnp.max(el2, axis=-1, keepdims=True)
    i2 = jnp.min(jnp.where(el2 == v2, lane, big), axis=-1, keepdims=True)
    e2 = jnp.exp(v2 - v1)
    den = 1.0 + e2
    w1 = (1.0 / den) * g_w
    w2 = (e2 / den) * g_w
    eid = jnp.where(lane == 0, i1 - MOE_GROUPS, jnp.where(lane == 1, i2 - MOE_GROUPS, 0))
    cw = jnp.where(lane == 0, w1, jnp.where(lane == 1, w2, 0.0))
    return eid, cw


def _inproj_heads_kernel(a_ref, w_ref, cos_ref, sin_ref, o_ref, wbf_ref):
    j = pl.program_id(0)

    @pl.when(pl.program_id(1) == 0)
    def _():
        wbf_ref[...] = w_ref[...].astype(BF16)

    acc = _dot_nt(a_ref[...], wbf_ref[...])
    q_tiles = NSA_QD // acc.shape[1]
    c = cos_ref[...]
    s = sin_ref[...]

    def head(h):
        return acc[:, h * NSA_HEAD_DIM:(h + 1) * NSA_HEAD_DIM]

    def rotary(xh):
        return (xh * c + pltpu.roll(xh, NSA_HEAD_DIM // 2, 1) * s).astype(BF16)

    for h in range(NSA_KV_HEADS):
        o_ref[h] = rotary(head(h))

    @pl.when(j < q_tiles)
    def _():
        for h in range(NSA_KV_HEADS, 2 * NSA_KV_HEADS):
            o_ref[h] = rotary(head(h))

    @pl.when(j >= q_tiles)
    def _():
        for h in range(NSA_KV_HEADS, 2 * NSA_KV_HEADS):
            o_ref[h] = head(h).astype(BF16)


def _nsa_inproj(hn, w_in_t, cos_full, sin_signed):
    tn = 2 * NSA_KVD
    n_heads_tile = tn // NSA_HEAD_DIM
    s_tiles = SEQ // MM_TM
    return pl.pallas_call(
        _inproj_heads_kernel, name="nsa_inproj", grid=(NSA_MAIN // tn, TOKENS // MM_TM),
        in_specs=[pl.BlockSpec((MM_TM, D_MODEL), lambda j, i: (i, 0)),
                  pl.BlockSpec((tn, D_MODEL), lambda j, i: (j, 0)),
                  pl.BlockSpec((MM_TM, NSA_HEAD_DIM), lambda j, i: (i % s_tiles, 0)),
                  pl.BlockSpec((MM_TM, NSA_HEAD_DIM), lambda j, i: (i % s_tiles, 0))],
        out_specs=pl.BlockSpec((n_heads_tile, MM_TM, NSA_HEAD_DIM), lambda j, i: (j, i, 0)),
        out_shape=jax.ShapeDtypeStruct((NSA_MAIN // NSA_HEAD_DIM, TOKENS, NSA_HEAD_DIM), BF16),
        scratch_shapes=[pltpu.VMEM((tn, D_MODEL), BF16)],
        compiler_params=_cparams(("arbitrary", "arbitrary")),
    )(hn, w_in_t, cos_full, sin_signed)


def _matmul_tiles_kernel(a_ref, w_ref, o_ref, wbf_ref):
    @pl.when(pl.program_id(1) == 0)
    def _():
        wbf_ref[...] = w_ref[...].astype(BF16)

    acc = _dot_nt(a_ref[...], wbf_ref[...])
    for t in range(acc.shape[1] // LANES):
        o_ref[t] = acc[:, t * LANES:(t + 1) * LANES].astype(o_ref.dtype)


def _matmul_tiles(a, w_t, n_cols, tn):
    k = a.shape[1]
    return pl.pallas_call(
        _matmul_tiles_kernel, name="matmul_tiles", grid=(n_cols // tn, TOKENS // MM_TM),
        in_specs=[pl.BlockSpec((MM_TM, k), lambda j, i: (i, 0)),
                  pl.BlockSpec((tn, k), lambda j, i: (j, 0))],
        out_specs=pl.BlockSpec((tn // LANES, MM_TM, LANES), lambda j, i: (j, i, 0)),
        out_shape=jax.ShapeDtypeStruct((n_cols // LANES, TOKENS, LANES), BF16),
        scratch_shapes=[pltpu.VMEM((tn, k), BF16)],
        compiler_params=_cparams(("arbitrary", "arbitrary")),
    )(a, w_t)


def _outproj_route_kernel(a_ref, w_ref, r_ref, g_ref, ws_ref, b_ref, o_ref, eid_ref, cw_ref):
    if len(a_ref.shape) == 3:
        a = jnp.concatenate([a_ref[t] for t in range(a_ref.shape[0])], axis=1)
    else:
        a = a_ref[...]
    x_new = r_ref[...] + _dot(a, w_ref[...])
    o_ref[...] = x_new
    eid_ref[...], cw_ref[...] = _route(x_new, g_ref[...], ws_ref[...], b_ref[...])


def _outproj_route(a, w, resid, ln_ffn_g, w_group, b_group, w_expert, b_expert):
    tiled = a.ndim == 3
    k, n = w.shape
    n_r = MOE_GROUPS + MOE_EXPERTS
    ws = jnp.zeros((D_MODEL, LANES), F32).at[:, :n_r].set(jnp.concatenate([w_group, w_expert], axis=1))
    bs = jnp.zeros((1, LANES), F32).at[0, :n_r].set(jnp.concatenate([b_group, b_expert]))
    a_spec = (pl.BlockSpec((k // LANES, OUT_TM, LANES), lambda i: (0, i, 0)) if tiled
              else pl.BlockSpec((OUT_TM, k), lambda i: (i, 0)))
    small = pl.BlockSpec((OUT_TM, LANES), lambda i: (i, 0))
    return pl.pallas_call(
        _outproj_route_kernel, name="outproj_route", grid=(TOKENS // OUT_TM,),
        in_specs=[a_spec,
                  pl.BlockSpec((k, n), lambda i: (0, 0)),
                  pl.BlockSpec((OUT_TM, n), lambda i: (i, 0)),
                  pl.BlockSpec((1, D_MODEL), lambda i: (0, 0)),
                  pl.BlockSpec((D_MODEL, LANES), lambda i: (0, 0)),
                  pl.BlockSpec((1, LANES), lambda i: (0, 0))],
        out_specs=[pl.BlockSpec((OUT_TM, n), lambda i: (i, 0)), small, small],
        out_shape=[jax.ShapeDtypeStruct((TOKENS, n), F32),
                   jax.ShapeDtypeStruct((TOKENS, LANES), I32),
                   jax.ShapeDtypeStruct((TOKENS, LANES), F32)],
        compiler_params=_cparams(("parallel",)),
    )(a, w.astype(BF16), resid, ln_ffn_g.reshape(1, D_MODEL), ws, bs)


def _compress_kernel(x_ref, pe_ref, w1_ref, w2_ref, o_ref):
    half = CMP_STRIDE * NSA_HEAD_DIM
    x = x_ref[0, 0, 0]
    w1 = w1_ref[0].astype(BF16)
    top = _dot(x, w1[:half])
    bot = _dot(x, w1[half:])
    pe = jnp.broadcast_to(pe_ref[0], (8, 2 * half)).astype(BF16)
    pe_bias = _dot(pe, w1)[0:1]
    hid = top + pltpu.roll(bot, bot.shape[0] - 1, 0) + pe_bias
    act = jax.nn.gelu(hid)
    o_ref[0, 0, 0] = _dot(act.astype(BF16), w2_ref[0].astype(BF16))


def _compress(kv_chunks, pe, w1, w2):
    n_chunk = SEQ // CMP_STRIDE
    feat = CMP_STRIDE * NSA_HEAD_DIM
    return pl.pallas_call(
        _compress_kernel, name="nsa_compress", grid=(2, NSA_KV_HEADS, BATCH),
        in_specs=[pl.BlockSpec((1, 1, 1, n_chunk, feat), lambda a, g, b: (a, g, b, 0, 0)),
                  pl.BlockSpec((1, 1, 2 * feat), lambda a, g, b: (a, 0, 0)),
                  pl.BlockSpec((1, 2 * feat, CMP_HIDDEN), lambda a, g, b: (a, 0, 0)),
                  pl.BlockSpec((1, CMP_HIDDEN, NSA_HEAD_DIM), lambda a, g, b: (a, 0, 0))],
        out_specs=pl.BlockSpec((1, 1, 1, n_chunk, NSA_HEAD_DIM), lambda a, g, b: (a, g, b, 0, 0)),
        out_shape=jax.ShapeDtypeStruct((2, NSA_KV_HEADS, BATCH, n_chunk, NSA_HEAD_DIM), F32),
        compiler_params=_cparams(("parallel", "parallel", "parallel")),
    )(kv_chunks, pe.reshape(2, 1, 2 * feat), w1, w2)


def _nsa_attn_kernel(q_ref, ks_ref, vs_ref, kw_ref, vw_ref, kc_ref, vc_ref, gate_ref, o_ref,
                     vst_scr, vwt_scr, sel_scr, acc_scr, sc_scr):
    qi = pl.program_id(2)
    tq = ATT_TQ
    tk = ATT_TK
    dh = NSA_HEAD_DIM
    r_heads = NSA_Q_PER_KV
    n_kt = SEQ // tk
    scale = dh ** -0.5 * LOG2_E

    @pl.when(qi == 0)
    def _():
        ones = jnp.ones((ATT_ONES, tk), BF16)
        for kt in range(n_kt):
            rows = slice(kt * tk, (kt + 1) * tk)
            vst_scr[kt, 0:dh, :] = vs_ref[0, rows, :].astype(F32).T.astype(BF16)
            vwt_scr[kt, 0:dh, :] = vw_ref[0, rows, :].astype(F32).T.astype(BF16)
            vst_scr[kt, dh:dh + ATT_ONES, :] = ones
            vwt_scr[kt, dh:dh + ATT_ONES, :] = ones

    q_s = jnp.concatenate([q_ref[r].astype(F32).T for r in range(r_heads)], axis=1) * scale
    q_hi = q_s.astype(BF16)
    q_t = jnp.concatenate([q_hi, (q_s - q_hi.astype(F32)).astype(BF16)], axis=0)

    def qk(k):
        return _dot(jnp.concatenate([k, k], axis=1), q_t)
    n_cp = SEQ // CMP_STRIDE
    sub = lax.broadcasted_iota(I32, (n_cp, tq), 0)
    t_pos = qi * tq + lax.broadcasted_iota(I32, (n_cp, tq), 1)

    def tile4(a):
        return jnp.concatenate([a] * r_heads, axis=1)

    kc = kc_ref[0, 0, 0].astype(BF16)
    vc = vc_ref[0, 0, 0].astype(BF16)
    ok_c = jnp.where(sub * CMP_STRIDE + CMP_BLOCK - 1 <= t_pos, jnp.where(sub < N_CMP, 1.0, 0.0), 0.0)
    ok_c4 = tile4(ok_c)
    s_c = qk(kc) + (ok_c4 - 1.0) * (-NEG_INF)
    e_c = jnp.exp2(s_c - jnp.max(s_c, axis=0, keepdims=True))
    p_c = (e_c / jnp.sum(e_c, axis=0, keepdims=True)) * ok_c4
    o_cmp = _dot_tn(vc, p_c.astype(BF16))
    p_sum = p_c[:, 0:tq]
    for r in range(1, r_heads):
        p_sum = p_sum + p_c[:, r * tq:(r + 1) * tq]

    blk_row = lax.broadcasted_iota(I32, (LANES, LANES), 0)
    cmp_col = lax.broadcasted_iota(I32, (LANES, LANES), 1)
    s_start = blk_row * SLC_BLOCK
    c_start = cmp_col * CMP_STRIDE
    ov_t = jnp.maximum(jnp.minimum(c_start + CMP_BLOCK, s_start + SLC_BLOCK)
                       - jnp.maximum(c_start, s_start), 0).astype(F32) / CMP_BLOCK
    ov_t = jnp.where(blk_row < N_SLC, ov_t, 0.0).astype(BF16)
    imp = _dot_split_rhs(ov_t, p_sum)[0:N_SLC]
    j_blk = lax.broadcasted_iota(I32, (N_SLC, tq), 0)
    dist = (qi * tq + lax.broadcasted_iota(I32, (N_SLC, tq), 1)) // SLC_BLOCK - j_blk
    imp = jnp.where(j_blk == 0, SLC_FORCE, jnp.where(dist < 0, imp, jnp.where(dist < SLC_LOCAL, SLC_FORCE, imp)))
    imp = jnp.where(dist >= 0, imp, -jnp.inf)
    cnt = jnp.zeros((N_SLC, tq), I32)
    for k in range(N_SLC):
        row_k = imp[k:k + 1, :]
        tie = jnp.where(j_blk > k, 1, 0)
        cnt = cnt + jnp.where(row_k > imp, 1, jnp.where(row_k == imp, tie, 0))
    sel = jnp.where(cnt < min(SLC_TOPK, N_SLC), 1.0, 0.0)
    for j in range(N_SLC):
        sel_scr[8 * j:8 * j + 8, :] = jnp.broadcast_to(sel[j:j + 1, :], (8, tq))

    def scores(k, ok):
        return qk(k) + tile4((ok - 1.0) * (-NEG_INF))

    def weighted_values(vt_scr, kt0, pr, n_tiles):
        out = None
        for u in range(n_tiles):
            term = _dot(vt_scr[kt0 + u], pr[u * tk:(u + 1) * tk].astype(BF16))
            out = term if out is None else out + term
        return out

    n_ct = ATT_CHUNK_TILES
    ck = n_ct * tk
    sub_c = lax.broadcasted_iota(I32, (ck, tq), 0)
    t_pos_c = qi * tq + lax.broadcasted_iota(I32, (ck, tq), 1)
    blocks_per_chunk = ck // SLC_BLOCK
    acc_scr[...] = jnp.zeros(acc_scr.shape, F32)

    def chunk_scores(c):
        c = jnp.minimum(c, SEQ // ck - 1)
        start = pl.multiple_of(c * ck, ck)
        k = ks_ref[0, pl.ds(start, ck), :]
        rows8 = sel_scr[pl.ds(pl.multiple_of(c * (8 * blocks_per_chunk), 8 * blocks_per_chunk),
                              8 * blocks_per_chunk), :]
        picked = jnp.concatenate(
            [rows8[8 * u:8 * u + 8] for u in range(blocks_per_chunk) for _ in range(SLC_BLOCK // 8)], axis=0)
        return scores(k, jnp.where(start + sub_c <= t_pos_c, picked, 0.0))

    sc_scr[...] = chunk_scores(0)

    def slc_body(c, m_old):
        sc = sc_scr[...]
        sc_next = chunk_scores(c + 1)
        m_new = jnp.maximum(m_old, jnp.max(sc, axis=0, keepdims=True))
        alpha = jnp.exp2(m_old - m_new)
        pr = jnp.exp2(sc - m_new)
        acc_scr[...] = alpha * acc_scr[...] + weighted_values(vst_scr, c * n_ct, pr, n_ct)
        sc_scr[...] = sc_next
        return m_new

    lax.fori_loop(0, ((qi + 1) * tq + ck - 1) // ck, slc_body, jnp.full((1, r_heads * tq), NEG_INF, F32))
    acc = acc_scr[...]
    o_slc = acc[0:dh] / acc[dh:dh + 1]

    n_wt = (WINDOW + tq) // tk
    kt0 = jnp.maximum(qi * (tq // tk) - WINDOW // tk, 0)
    w_start = pl.multiple_of(kt0 * tk, tk)
    key_w = w_start + lax.broadcasted_iota(I32, (n_wt * tk, tq), 0)
    t_pos_w = qi * tq + lax.broadcasted_iota(I32, (n_wt * tk, tq), 1)
    ok_w = jnp.where(key_w <= t_pos_w, jnp.where(key_w > t_pos_w - WINDOW, 1.0, 0.0), 0.0)
    sc_w = scores(kw_ref[0, pl.ds(w_start, n_wt * tk), :], ok_w)
    pr_w = jnp.exp2(sc_w - jnp.max(sc_w, axis=0, keepdims=True))
    acc_w = weighted_values(vwt_scr, kt0, pr_w, n_wt)
    o_win = acc_w[0:dh] / acc_w[dh:dh + 1]

    gate = jax.nn.sigmoid(gate_ref[0])
    for r in range(r_heads):
        cols = slice(r * tq, (r + 1) * tq)
        o = (gate[3 * r:3 * r + 1] * o_cmp[:, cols] + gate[3 * r + 1:3 * r + 2] * o_slc[:, cols]
             + gate[3 * r + 2:3 * r + 3] * o_win[:, cols])
        o_ref[:, r * dh:(r + 1) * dh] = o.T.astype(BF16)


def _nsa_attention(heads, kc_vc, gates):
    tq = ATT_TQ
    nq = SEQ // tq
    r = NSA_Q_PER_KV
    g_heads = NSA_KV_HEADS
    q_spec = pl.BlockSpec((r, tq, NSA_HEAD_DIM), lambda b, g, i: (g, b * nq + i, 0))

    def kv_spec(first_head):
        return pl.BlockSpec((1, SEQ, NSA_HEAD_DIM), lambda b, g, i: (first_head + g, b, 0))

    first = NSA_HEADS
    specs = [q_spec,
             kv_spec(first + 2 * g_heads), kv_spec(first + 3 * g_heads),
             kv_spec(first + 4 * g_heads), kv_spec(first + 5 * g_heads),
             pl.BlockSpec((1, 1, 1, SEQ // CMP_STRIDE, NSA_HEAD_DIM), lambda b, g, i: (0, g, b, 0, 0)),
             pl.BlockSpec((1, 1, 1, SEQ // CMP_STRIDE, NSA_HEAD_DIM), lambda b, g, i: (1, g, b, 0, 0)),
             pl.BlockSpec((1, 3 * r, tq), lambda b, g, i: (g, 0, b * nq + i))]
    vt_shape = (SEQ // ATT_TK, NSA_HEAD_DIM + ATT_ONES, ATT_TK)
    return pl.pallas_call(
        _nsa_attn_kernel, name="nsa_attn", grid=(BATCH, g_heads, nq),
        in_specs=specs,
        out_specs=pl.BlockSpec((tq, r * NSA_HEAD_DIM), lambda b, g, i: (b * nq + i, g)),
        out_shape=jax.ShapeDtypeStruct((TOKENS, NSA_QD), BF16),
        scratch_shapes=[pltpu.VMEM(vt_shape, BF16), pltpu.VMEM(vt_shape, BF16),
                        pltpu.VMEM((8 * N_SLC, tq), F32),
                        pltpu.VMEM((NSA_HEAD_DIM + ATT_ONES, r * tq), F32),
                        pltpu.VMEM((ATT_CHUNK_TILES * ATT_TK, r * tq), F32)],
        compiler_params=_cparams(("arbitrary", "arbitrary", "arbitrary")),
    )(heads, heads, heads, heads, heads, kc_vc, kc_vc, gates)


def _ssd_chunk_kernel(zx_ref, dtc_ref, dtr_ref, cw_ref, cb_ref, dtb_c_ref, alog_c_ref, dtb_r_ref, alog_r_ref,
                      dskip_ref, ng_ref, echan_ref, o_ref,
                      prev_scr, acum_r_scr, st_scr, fac_scr):
    chunk = pl.program_id(1)
    L = SSD_CHUNK
    W = SSD_GROUP_W
    hpg = SSD_HEADS_PER_GROUP
    n_xt = W // LANES
    x0 = SSD_D_INNER // LANES
    b0 = 2 * SSD_D_INNER // LANES
    c0 = b0 + SSD_GROUPS
    cb0 = SSD_D_INNER // LANES
    cc0 = cb0 + SSD_GROUPS

    @pl.when(chunk == 0)
    def _():
        prev_scr[...] = jnp.zeros(prev_scr.shape, F32)
        st_scr[...] = jnp.zeros(st_scr.shape, F32)

    dt_c = jax.nn.softplus(dtc_ref[...] + dtb_c_ref[...])
    adt_c = dt_c * (-jnp.exp(alog_c_ref[...]))
    dt_r = jax.nn.softplus(dtr_ref[...] + dtb_r_ref[...])
    adt_r = dt_r * (-jnp.exp(alog_r_ref[...]))
    row = lax.broadcasted_iota(I32, (L, L), 0)
    col = lax.broadcasted_iota(I32, (L, L), 1)
    causal = row >= col
    tri = jnp.where(causal, 1.0, 0.0).astype(BF16)
    tri_t = jnp.where(col >= row, 1.0, 0.0).astype(BF16)
    acum_c = _dot_split_rhs(tri, adt_c)
    acum_r_scr[...] = _dot_split_lhs(adt_r, tri_t)
    a_last = acum_c[L - 1:L, :]
    fac = jnp.concatenate([dt_c, jnp.exp(acum_c), jnp.exp(a_last - acum_c),
                           jnp.broadcast_to(jnp.exp(a_last), (SSD_FAC_PAD, LANES))], axis=0)
    fac_hi = fac.astype(BF16)
    fac_scr[0] = fac_hi
    fac_scr[1] = (fac - fac_hi.astype(F32)).astype(BF16)
    lane_w = lax.broadcasted_iota(I32, (L, LANES), 1)
    first_half = lane_w < SSD_HEAD_DIM

    def tiles(ref, first, n):
        return jnp.concatenate([ref[first + q] for q in range(n)], axis=1)

    def group_body(g, carry):
        e_chan = echan_ref[g]
        ex = _dot(fac_scr[0, 0:3 * L], e_chan)
        dt_x, ea_x, sd_x = ex[0:L], ex[L:2 * L], ex[2 * L:3 * L]
        cd_x = (_dot(fac_scr[0, 3 * L:3 * L + SSD_FAC_PAD], e_chan)
                + _dot(fac_scr[1, 3 * L:3 * L + SSD_FAC_PAD], e_chan))[0:1]

        def conv_silu(zx_first, conv_first, n):
            cur = tiles(zx_ref, zx_first, n)
            cur_f = cur.astype(F32)
            tail = tiles(prev_scr, conv_first, n)
            row8 = lax.broadcasted_iota(I32, (8, n * LANES), 0)
            w = tiles(cw_ref, conv_first, n)
            acc = jnp.broadcast_to(tiles(cb_ref, conv_first, n), (L, n * LANES))
            for k in range(SSD_CONV):
                back = SSD_CONV - 1 - k
                if back == 0:
                    xk = cur_f
                else:
                    rolled = pltpu.roll(cur_f, back, 0)
                    head = jnp.where(row8 < back, pltpu.roll(tail, back, 0), rolled[0:8])
                    xk = jnp.concatenate([head, rolled[8:]], axis=0)
                acc = acc + xk * w[k:k + 1, :]
            for q in range(n):
                prev_scr[conv_first + q] = cur_f[L - 8:L, q * LANES:(q + 1) * LANES]
            return jax.nn.silu(acc)

        xs = conv_silu(x0 + n_xt * g, n_xt * g, n_xt)
        bm = conv_silu(b0 + g, cb0 + g, 1)
        cm = conv_silu(c0 + g, cc0 + g, 1)

        xdt = xs * dt_x
        cb = jnp.where(causal, _dot_nt(cm.astype(BF16), bm.astype(BF16)), 0.0)
        y_parts = []
        for pair in range(hpg // 2):
            xd = xdt[:, pair * LANES:(pair + 1) * LANES]
            m_pair = []
            for sub in range(2):
                a_row = jnp.broadcast_to(acum_r_scr[pl.ds(g * hpg + 2 * pair + sub, 1), :], (L, L))
                seg = jnp.minimum(a_row.T - a_row, 0.0)
                m_pair.append((cb * jnp.exp(seg)).astype(BF16))
            x_pair = jnp.concatenate([jnp.where(first_half, xd, 0.0), jnp.where(first_half, 0.0, xd)], axis=0)
            y_parts.append(_dot(jnp.concatenate(m_pair, axis=1), x_pair.astype(BF16)))
        y_diag = jnp.concatenate(y_parts, axis=1)

        st = st_scr[g]
        y_off = _dot(cm.astype(BF16), st.astype(BF16)) * ea_x
        st_scr[g] = st * cd_x + _dot_tn(bm.astype(BF16), (xdt * sd_x).astype(BF16))

        y = y_diag + y_off + xs * tiles(dskip_ref, n_xt * g, n_xt)
        y = y * jax.nn.silu(tiles(zx_ref, n_xt * g, n_xt).astype(F32))
        y = y * lax.rsqrt(jnp.mean(y * y, axis=-1, keepdims=True) + NORM_EPS)
        y = y * tiles(ng_ref, n_xt * g, n_xt)
        for q in range(n_xt):
            o_ref[n_xt * g + q] = y[:, q * LANES:(q + 1) * LANES].astype(BF16)
        return carry

    lax.fori_loop(0, SSD_GROUPS, group_body, 0)


def _ssd_chunks(zx_tiles, dt_small, dt_small_t, conv_w, conv_b, dt_bias, a_log, d_skip, norm_g):
    L = SSD_CHUNK
    nc = SEQ // L
    n_zx = SSD_MAIN // LANES
    n_conv = SSD_CONV_CH // LANES
    n_inner = SSD_D_INNER // LANES
    hpg = SSD_HEADS_PER_GROUP

    def pad_heads(v):
        return jnp.zeros((LANES,), F32).at[:SSD_HEADS].set(v)

    dtb = pad_heads(dt_bias)
    alog = pad_heads(a_log)
    cw = conv_w.reshape(SSD_CONV, n_conv, LANES).transpose(1, 0, 2)
    cb = conv_b.reshape(n_conv, 1, LANES)
    d_chan = jnp.repeat(d_skip, SSD_HEAD_DIM).reshape(n_inner, 1, LANES)
    ng = norm_g.reshape(n_inner, 1, LANES)
    head =jnp.arange(LANES, dtype=I32)[None, :, None]
    grp = jnp.arange(SSD_GROUPS, dtype=I32)[:, None, None]
    e_chan = (head == grp * hpg + jnp.arange(SSD_GROUP_W, dtype=I32)[None, None, :] // SSD_HEAD_DIM).astype(BF16)
    row = lambda b, c: b * nc + c
    const3 = lambda b, c: (0, 0, 0)
    const2 = lambda b, c: (0, 0)
    in_specs = [
        pl.BlockSpec((n_zx, L, LANES), lambda b, c: (0, row(b, c), 0)),
        pl.BlockSpec((L, LANES), lambda b, c: (row(b, c), 0)),
        pl.BlockSpec((LANES, L), lambda b, c: (0, row(b, c))),
        pl.BlockSpec((n_conv, SSD_CONV, LANES), const3),
        pl.BlockSpec((n_conv, 1, LANES), const3),
        pl.BlockSpec((1, LANES), const2), pl.BlockSpec((1, LANES), const2),
        pl.BlockSpec((LANES, 1), const2), pl.BlockSpec((LANES, 1), const2),
        pl.BlockSpec((n_inner, 1, LANES), const3),
        pl.BlockSpec((n_inner, 1, LANES), const3),
        pl.BlockSpec((SSD_GROUPS, LANES, SSD_GROUP_W), const3),
    ]
    return pl.pallas_call(
        _ssd_chunk_kernel, name="ssd_chunks", grid=(BATCH, nc),
        in_specs=in_specs,
        out_specs=pl.BlockSpec((n_inner, L, LANES), lambda b, c: (0, row(b, c), 0)),
        out_shape=jax.ShapeDtypeStruct((n_inner, TOKENS, LANES), BF16),
        scratch_shapes=[pltpu.VMEM((n_conv, 8, LANES), F32),
                        pltpu.VMEM((LANES, L), F32),
                        pltpu.VMEM((SSD_GROUPS, SSD_D_STATE, SSD_GROUP_W), F32),
                        pltpu.VMEM((2, 3 * L + SSD_FAC_PAD, LANES), BF16)],
        compiler_params=_cparams(("arbitrary", "arbitrary")),
    )(zx_tiles, dt_small, dt_small_t, cw, cb, dtb.reshape(1, LANES), alog.reshape(1, LANES),
      dtb.reshape(LANES, 1), alog.reshape(LANES, 1), d_chan, ng, e_chan)


def _gather_rows(src_hbm, idx_ref, base, dst, sem, n_rows):
    def body(b, carry):
        for u in range(GATHER_UNROLL):
            r = b * GATHER_UNROLL + u
            tok = idx_ref[base + r]
            pltpu.make_async_copy(src_hbm.at[pl.ds(tok, 1), :], dst.at[pl.ds(r, 1), :], sem).start()
        return carry

    lax.fori_loop(0, n_rows // GATHER_UNROLL, body, 0)


def _moe_ffn_kernel(te_ref, pos_ref, nact_ref, pad_ref, wp_ref, x_hbm, g_ref, wg_hbm, wu_hbm, wd_hbm, y_ref,
                    tok_ref, buf, sem, wg_f, wu_f, wd_f, wsem, wg_bf, wu_bf, wd_bf, *, layer):
    i = pl.program_id(0)
    n_act = nact_ref[0]
    tm = MOE_TM
    slot = i % 2
    run_start = wp_ref[i] == 1
    w_slot = wp_ref[MOE_TILES + i]
    next_expert = wp_ref[2 * MOE_TILES + i]
    n_rows = pl.multiple_of(wp_ref[3 * MOE_TILES + i], GATHER_UNROLL)

    def gather(tile, s):
        _gather_rows(x_hbm, tok_ref, tile * tm, buf.at[s], sem.at[s], wp_ref[3 * MOE_TILES + tile])

    def weight_copies(e, s):
        return (pltpu.make_async_copy(wg_hbm.at[layer, e], wg_f.at[s], wsem.at[s, 0]),
                pltpu.make_async_copy(wu_hbm.at[layer, e], wu_f.at[s], wsem.at[s, 1]),
                pltpu.make_async_copy(wd_hbm.at[layer, e], wd_f.at[s], wsem.at[s, 2]))

    @pl.when(i == 0)
    def _():
        for c in weight_copies(te_ref[0], 0):
            c.start()
        buf[...] = jnp.zeros(buf.shape, F32)
        def clear(q, carry):
            tok_ref[q] = 0
            return carry

        def clear_padding(e, carry):
            lax.fori_loop(pad_ref[e], pad_ref[MOE_EXPERTS + e], clear, 0)
            return carry

        lax.fori_loop(0, MOE_EXPERTS, clear_padding, 0)
        for k in range(MOE_TOPK):
            def place(t, carry, k=k):
                tok_ref[pos_ref[k * TOKENS + t]] = t
                return carry

            lax.fori_loop(0, TOKENS, place, 0, unroll=GATHER_UNROLL)
        gather(0, 0)

    @pl.when(run_start)
    def _():
        for c in weight_copies(te_ref[i], w_slot):
            c.wait()

        @pl.when(next_expert >= 0)
        def _():
            for c in weight_copies(next_expert, 1 - w_slot):
                c.start()

        wg_bf[...] = wg_f[w_slot].astype(BF16)
        wu_bf[...] = wu_f[w_slot].astype(BF16)
        wd_bf[...] = wd_f[w_slot].astype(BF16)

    @pl.when(i >= n_act)
    def _():
        y_ref[...] = jnp.zeros(y_ref.shape, F32)

    @pl.when(i < n_act)
    def _():
        pltpu.make_async_copy(x_hbm.at[pl.ds(0, n_rows), :], buf.at[slot, pl.ds(0, n_rows), :],
                              sem.at[slot]).wait()

        @pl.when(i + 1 < n_act)
        def _():
            gather(i + 1, 1 - slot)

        h = _rms(buf[slot], g_ref[...]).astype(BF16)
        act = jax.nn.silu(_dot(h, wg_bf[...])) * _dot(h, wu_bf[...])
        y_ref[...] = _dot(act.astype(BF16), wd_bf[...])


def _moe_ffn(x, g, w_gate, w_up, w_down, layer, tile_expert, pos_kmajor, n_active, pad_rows, weight_plan):
    any_spec = pl.BlockSpec(memory_space=pl.ANY)
    grid_spec = pltpu.PrefetchScalarGridSpec(
        num_scalar_prefetch=5, grid=(MOE_TILES,),
        in_specs=[any_spec, pl.BlockSpec((1, D_MODEL), lambda i, *_: (0, 0)), any_spec, any_spec, any_spec],
        out_specs=pl.BlockSpec((MOE_TM, D_MODEL), lambda i, *_: (i, 0)),
        scratch_shapes=[pltpu.SMEM((MOE_ROWS,), I32),
                        pltpu.VMEM((2, MOE_TM, D_MODEL), F32), pltpu.SemaphoreType.DMA((2,)),
                        pltpu.VMEM((2, D_MODEL, MOE_D_FF), F32), pltpu.VMEM((2, D_MODEL, MOE_D_FF), F32),
                        pltpu.VMEM((2, MOE_D_FF, D_MODEL), F32), pltpu.SemaphoreType.DMA((2, 3)),
                        pltpu.VMEM((D_MODEL, MOE_D_FF), BF16), pltpu.VMEM((D_MODEL, MOE_D_FF), BF16),
                        pltpu.VMEM((MOE_D_FF, D_MODEL), BF16)])
    return pl.pallas_call(
        functools.partial(_moe_ffn_kernel, layer=layer), name="moe_ffn", grid_spec=grid_spec,
        out_shape=jax.ShapeDtypeStruct((MOE_ROWS, D_MODEL), F32),
        compiler_params=_cparams(("arbitrary",)),
    )(tile_expert, pos_kmajor, n_active, pad_rows, weight_plan, x, g.reshape(1, D_MODEL), w_gate, w_up, w_down)


def _moe_combine_kernel(pos_ref, x_ref, cw_ref, g_ref, ws_ref, y_hbm, *refs, final_norm):
    if final_norm:
        o_ref, buf, sem = refs
    else:
        o_ref, hn_ref, small_ref, small_t_ref, buf, sem = refs
    i = pl.program_id(0)
    n = pl.num_programs(0)
    tm = CMB_TM
    slot = i % 2

    def issue(tile, s):
        for k in range(MOE_TOPK):
            _gather_rows(y_hbm, pos_ref, (k * (TOKENS // tm) + tile) * tm, buf.at[s, k], sem.at[s], tm)

    @pl.when(i == 0)
    def _():
        issue(0, 0)

    @pl.when(i + 1 < n)
    def _():
        issue(i + 1, 1 - slot)

    for k in range(MOE_TOPK):
        pltpu.make_async_copy(y_hbm.at[pl.ds(0, tm), :], buf.at[slot, k], sem.at[slot]).wait()
    cw = cw_ref[...]
    out = x_ref[...] + cw[:, 0:1] * buf[slot, 0] + cw[:, 1:2] * buf[slot, 1]
    y = _rms(out, g_ref[...])
    if final_norm:
        o_ref[...] = y
    else:
        o_ref[...] = out
        hn_ref[...] = y.astype(BF16)
        small = _dot_x3(y, ws_ref[...])
        small_ref[...] = small
        small_t_ref[...] = small.T


def _moe_combine(x, cw, y_sorted, pos_kmajor, g_norm, w_small, final_norm):
    ws = jnp.zeros((D_MODEL, LANES), F32)
    if not final_norm:
        ws = ws.at[:, :w_small.shape[1]].set(w_small)
    row = pl.BlockSpec((CMB_TM, D_MODEL), lambda i, pos: (i, 0))
    small = pl.BlockSpec((CMB_TM, LANES), lambda i, pos: (i, 0))
    x_shape = jax.ShapeDtypeStruct((TOKENS, D_MODEL), F32)
    if final_norm:
        out_specs, out_shape = row, x_shape
    else:
        out_specs = [row, row, small, pl.BlockSpec((LANES, CMB_TM), lambda i, pos: (0, i))]
        out_shape = [x_shape, jax.ShapeDtypeStruct((TOKENS, D_MODEL), BF16),
                     jax.ShapeDtypeStruct((TOKENS, LANES), F32), jax.ShapeDtypeStruct((LANES, TOKENS), F32)]
    grid_spec = pltpu.PrefetchScalarGridSpec(
        num_scalar_prefetch=1, grid=(TOKENS // CMB_TM,),
        in_specs=[row, small,
                  pl.BlockSpec((1, D_MODEL), lambda i, pos: (0, 0)),
                  pl.BlockSpec((D_MODEL, LANES), lambda i, pos: (0, 0)),
                  pl.BlockSpec(memory_space=pl.ANY)],
        out_specs=out_specs,
        scratch_shapes=[pltpu.VMEM((2, MOE_TOPK, CMB_TM, D_MODEL), F32), pltpu.SemaphoreType.DMA((2,))])
    return pl.pallas_call(
        functools.partial(_moe_combine_kernel, final_norm=final_norm), name="moe_combine", grid_spec=grid_spec,
        out_shape=out_shape,
        compiler_params=_cparams(("arbitrary",)),
    )(pos_kmajor, x, cw, g_norm.reshape(1, D_MODEL), ws, y_sorted)


def _moe_plan(eid):
    e = eid[:, :MOE_TOPK].reshape(-1)
    onehot = (e[:, None] == jnp.arange(MOE_EXPERTS, dtype=I32)[None, :]).astype(I32)
    csum = jnp.cumsum(onehot, axis=0)
    counts = csum[-1]
    padded = ((counts + MOE_TM - 1) // MOE_TM) * MOE_TM
    g_end = jnp.cumsum(padded)
    g_start = g_end - padded
    pos = jnp.sum(onehot * (g_start[None, :] + csum - 1), axis=1)
    n_active = (g_end[-1] // MOE_TM).astype(I32)
    tile_start = jnp.arange(MOE_TILES, dtype=I32) * MOE_TM
    te = jnp.sum((g_end[None, :] <= tile_start[:, None]).astype(I32), axis=1)
    last = jnp.max(jnp.where(counts > 0, jnp.arange(MOE_EXPERTS, dtype=I32), 0))
    tile_expert = jnp.minimum(te, last)
    pos_kmajor = pos.reshape(TOKENS, MOE_TOPK).T.reshape(-1)
    pad_rows = jnp.concatenate([g_start + counts, g_end])
    experts = jnp.arange(MOE_EXPERTS, dtype=I32)
    run_start = jnp.concatenate([jnp.ones((1,), I32), (tile_expert[1:] != tile_expert[:-1]).astype(I32)])
    w_slot = (jnp.cumsum(run_start) - 1) % 2
    later = (experts[None, :] > experts[:, None]) & (counts > 0)[None, :]
    next_nonempty = jnp.min(jnp.where(later, experts[None, :], MOE_EXPERTS), axis=1)
    next_nonempty = jnp.where(next_nonempty == MOE_EXPERTS, -1, next_nonempty)
    real_rows = jnp.clip((g_start + counts)[tile_expert] - tile_start, 0, MOE_TM)
    real_rows = jnp.where(tile_start < g_end[-1], real_rows, 0)
    gather_rows = jnp.minimum(-(-real_rows // GATHER_UNROLL) * GATHER_UNROLL, MOE_TM)
    weight_plan = jnp.concatenate([run_start, w_slot, next_nonempty[tile_expert], gather_rows]).astype(I32)
    return tile_expert, n_active.reshape(1), pos_kmajor, pad_rows, weight_plan


def _hier_moe_add(x, eid, cw, ln_g, w_gate, w_up, w_down, layer, g_norm, w_small, final_norm):
    tile_expert, n_active, pos_kmajor, pad_rows, weight_plan = _moe_plan(eid)
    y_sorted = _moe_ffn(x, ln_g, w_gate, w_up, w_down, layer, tile_expert, pos_kmajor, n_active, pad_rows,
                        weight_plan)
    return _moe_combine(x, cw, y_sorted, pos_kmajor, g_norm, w_small, final_norm)


def _rope_tables():
    pos = jnp.arange(SEQ, dtype=F32)
    inv = 1.0 / (ROPE_THETA ** (jnp.arange(0, NSA_HEAD_DIM, 2, dtype=F32) / NSA_HEAD_DIM))
    ang = pos[:, None] * inv[None, :]
    cos, sin = jnp.cos(ang), jnp.sin(ang)
    return jnp.concatenate([cos, cos], axis=1), jnp.concatenate([-sin, sin], axis=1)


def _nsa_mixer(hn, g_lin_t, w_in, cmp_pe, cmp_w1, cmp_w2):
    w_in_t = w_in.T
    cos_full, sin_signed = _rope_tables()
    heads = _nsa_inproj(hn, w_in_t, cos_full, sin_signed)
    first_c = NSA_HEADS
    kv_c = heads[first_c:first_c + 2 * NSA_KV_HEADS]
    kv_chunks = kv_c.reshape(2, NSA_KV_HEADS, BATCH, SEQ // CMP_STRIDE, CMP_STRIDE * NSA_HEAD_DIM)
    kc_vc = _compress(kv_chunks, cmp_pe, cmp_w1, cmp_w2)
    gates_t = g_lin_t[:NSA_GATES].reshape(NSA_KV_HEADS, 3 * NSA_Q_PER_KV, TOKENS)
    return _nsa_attention(heads, kc_vc, gates_t)


def _ssd_mixer(hn, dt_small, dt_small_t, w_in, conv_w, conv_b, dt_bias, a_log, d_skip, norm_g):
    zx_tiles = _matmul_tiles(hn, w_in.T, SSD_MAIN, 1024)
    return _ssd_chunks(zx_tiles, dt_small, dt_small_t, conv_w, conv_b, dt_bias, a_log, d_skip, norm_g)


def kernel(x, ln_mix, ln_ffn, ln_final, nsa_w_in, nsa_cmp_pe, nsa_cmp_w1, nsa_cmp_w2, nsa_w_out,
           ssd_w_in, ssd_conv_w, ssd_conv_b, ssd_dt_bias, ssd_a_log, ssd_d, ssd_norm, ssd_w_out,
           moe_w_group, moe_b_group, moe_w_expert, moe_b_expert, moe_w_gate, moe_w_up, moe_w_down):
    def small_weight(i):
        if i % N_MIXERS == 0:
            return nsa_w_in[i // N_MIXERS].T[NSA_MAIN:].T
        return ssd_w_in[i // N_MIXERS].T[SSD_MAIN:].T

    h = x.reshape(TOKENS, D_MODEL)
    hn, small, small_t = _norm_small(h, ln_mix[0], small_weight(0))
    for i in range(DEPTH):
        j = i // N_MIXERS
        last = i == DEPTH - 1
        if i % N_MIXERS == 0:
            mix = _nsa_mixer(hn, small_t, nsa_w_in[j], nsa_cmp_pe[j], nsa_cmp_w1[j], nsa_cmp_w2[j])
            w_out = nsa_w_out[j]
        else:
            mix = _ssd_mixer(hn, small, small_t, ssd_w_in[j], ssd_conv_w[j], ssd_conv_b[j], ssd_dt_bias[j],
                             ssd_a_log[j], ssd_d[j], ssd_norm[j])
            w_out = ssd_w_out[j]
        h, eid, cw = _outproj_route(mix, w_out, h, ln_ffn[i], moe_w_group[i], moe_b_group[i],
                                    moe_w_expert[i], moe_b_expert[i])
        out = _hier_moe_add(h, eid, cw, ln_ffn[i], moe_w_gate, moe_w_up, moe_w_down, i,
                            ln_final if last else ln_mix[i + 1], None if last else small_weight(i + 1), last)
        if last:
            h = out
        else:
            h, hn, small, small_t = out
    return h.reshape(BATCH, SEQ, D_MODEL)
```

```python
import functools

import jax
import jax.numpy as jnp
from jax import lax
from jax.experimental import pallas as pl
from jax.experimental.pallas import tpu as pltpu

F32 = jnp.float32
BF16 = jnp.bfloat16
I32 = jnp.int32

D_MODEL = 2048
BATCH = 4
SEQ = 2048
TOKENS = BATCH * SEQ
DEPTH = 2
N_MIXERS = 2
NORM_EPS = 1e-6
NEG_INF = -1e30
LOG2_E = 1.4426950408889634
ROPE_THETA = 10000.0

NSA_HEADS = 16
NSA_KV_HEADS = 4
NSA_HEAD_DIM = D_MODEL // NSA_HEADS
NSA_Q_PER_KV = NSA_HEADS // NSA_KV_HEADS
CMP_BLOCK = 32
CMP_STRIDE = 16
CMP_HIDDEN = 256
N_CMP = (SEQ - CMP_BLOCK) // CMP_STRIDE + 1
SLC_BLOCK = 64
SLC_TOPK = 16
SLC_LOCAL = 2
SLC_FORCE = 1e4
N_SLC = SEQ // SLC_BLOCK
WINDOW = 512
NSA_QD = NSA_HEADS * NSA_HEAD_DIM
NSA_KVD = NSA_KV_HEADS * NSA_HEAD_DIM
NSA_MAIN = NSA_QD + 6 * NSA_KVD
NSA_GATES = 3 * NSA_HEADS

SSD_D_INNER = 2 * D_MODEL
SSD_HEAD_DIM = 64
SSD_HEADS = SSD_D_INNER // SSD_HEAD_DIM
SSD_GROUPS = 8
SSD_HEADS_PER_GROUP = SSD_HEADS // SSD_GROUPS
SSD_D_STATE = 128
SSD_CONV = 4
SSD_CHUNK = 128
SSD_GROUP_W = SSD_D_INNER // SSD_GROUPS
SSD_BC = SSD_GROUPS * SSD_D_STATE
SSD_CONV_CH = SSD_D_INNER + 2 * SSD_BC
SSD_MAIN = SSD_D_INNER + SSD_CONV_CH
SSD_FAC_PAD = 16

MOE_GROUPS = 4
MOE_EPG = 8
MOE_EXPERTS = MOE_GROUPS * MOE_EPG
MOE_TOPK = 2
MOE_D_FF = 512

LANES = 128
VMEM_LIMIT = 56 * 1024 * 1024

NORM_TM = 512
MM_TM = 1024
OUT_TM = 256
ATT_TQ = 256
ATT_TK = 256
ATT_CHUNK_TILES = 2
ATT_ONES = 16
MOE_TM = 256
MOE_TILES = (TOKENS * MOE_TOPK) // MOE_TM + MOE_EXPERTS
MOE_ROWS = MOE_TILES * MOE_TM
CMB_TM = 256
GATHER_UNROLL = 8


def _cparams(sem):
    return pltpu.CompilerParams(dimension_semantics=sem, vmem_limit_bytes=VMEM_LIMIT)


def _split3(x):
    hi = x.astype(BF16)
    r1 = x - hi.astype(F32)
    mid = r1.astype(BF16)
    lo = (r1 - mid.astype(F32)).astype(BF16)
    return hi, mid, lo


def _dot(a, b):
    return jnp.dot(a, b, preferred_element_type=F32)


def _dot_nt(a, b):
    return lax.dot_general(a, b, (((1,), (1,)), ((), ())), preferred_element_type=F32)


def _dot_tn(a, b):
    return lax.dot_general(a, b, (((0,), (0,)), ((), ())), preferred_element_type=F32)


def _dot_split_lhs(x, m_bf16):
    hi, mid, lo = _split3(x)
    return _dot(hi, m_bf16) + _dot(mid, m_bf16) + _dot(lo, m_bf16)


def _dot_split_rhs(m_bf16, x):
    hi, mid, lo = _split3(x)
    return _dot(m_bf16, hi) + _dot(m_bf16, mid) + _dot(m_bf16, lo)


def _dot_x3(a, w):
    a_hi = a.astype(BF16)
    a_lo = (a - a_hi.astype(F32)).astype(BF16)
    w_hi = w.astype(BF16)
    w_lo = (w - w_hi.astype(F32)).astype(BF16)
    n = w.shape[1]
    both = _dot(a_hi, jnp.concatenate([w_hi, w_lo], axis=1))
    return both[:, :n] + both[:, n:] + _dot(a_lo, w_hi)


def _rms(x, g):
    y = x * lax.rsqrt(jnp.mean(x * x, axis=-1, keepdims=True) + NORM_EPS)
    return y * g


def _norm_small_kernel(x_ref, g_ref, ws_ref, hn_ref, small_ref, small_t_ref):
    y = _rms(x_ref[...], g_ref[...])
    hn_ref[...] = y.astype(BF16)
    small = _dot_x3(y, ws_ref[...])
    small_ref[...] = small
    small_t_ref[...] = small.T


def _norm_small(x, g, w_small):
    n = w_small.shape[1]
    ws = jnp.zeros((D_MODEL, LANES), F32).at[:, :n].set(w_small)
    return pl.pallas_call(
        _norm_small_kernel, name="norm_small", grid=(TOKENS // NORM_TM,),
        in_specs=[pl.BlockSpec((NORM_TM, D_MODEL), lambda i: (i, 0)),
                  pl.BlockSpec((1, D_MODEL), lambda i: (0, 0)),
                  pl.BlockSpec((D_MODEL, LANES), lambda i: (0, 0))],
        out_specs=[pl.BlockSpec((NORM_TM, D_MODEL), lambda i: (i, 0)),
                   pl.BlockSpec((NORM_TM, LANES), lambda i: (i, 0)),
                   pl.BlockSpec((LANES, NORM_TM), lambda i: (0, i))],
        out_shape=[jax.ShapeDtypeStruct((TOKENS, D_MODEL), BF16),
                   jax.ShapeDtypeStruct((TOKENS, LANES), F32),
                   jax.ShapeDtypeStruct((LANES, TOKENS), F32)],
        compiler_params=_cparams(("parallel",)),
    )(x, g.reshape(1, D_MODEL), ws)


def _route(x, g, ws, bias):
    y = _rms(x, g)
    logits = _dot_x3(y, ws) + bias
    lane = lax.broadcasted_iota(I32, logits.shape, 1)
    big = jnp.int32(LANES)
    neg = -jnp.inf
    gl = jnp.where(lane < MOE_GROUPS, logits, neg)
    gmax = jnp.max(gl, axis=-1, keepdims=True)
    gsum = jnp.sum(jnp.exp(gl - gmax), axis=-1, keepdims=True)
    g_w = 1.0 / gsum
    g_sel = jnp.min(jnp.where(gl == gmax, lane, big), axis=-1, keepdims=True)
    lo = MOE_GROUPS + g_sel * MOE_EPG
    el = jnp.where((lane >= lo) & (lane < lo + MOE_EPG), logits, neg)
    v1 = jnp.max(el, axis=-1, keepdims=True)
    i1 = jnp.min(jnp.where(el == v1, lane, big), axis=-1, keepdims=True)
    el2 = jnp.where(lane == i1, neg, el)
    v2 = jnp.max(el2, axis=-1, keepdims=True)
    i2 = jnp.min(jnp.where(el2 == v2, lane, big), axis=-1, keepdims=True)
    e2 = jnp.exp(v2 - v1)
    den = 1.0 + e2
    w1 = (1.0 / den) * g_w
    w2 = (e2 / den) * g_w
    eid = jnp.where(lane == 0, i1 - MOE_GROUPS, jnp.where(lane == 1, i2 - MOE_GROUPS, 0))
    cw = jnp.where(lane == 0, w1, jnp.where(lane == 1, w2, 0.0))
    return eid, cw


def _inproj_heads_kernel(a_ref, w_ref, cos_ref, sin_ref, o_ref, wbf_ref):
    j = pl.program_id(0)

    @pl.when(pl.program_id(1) == 0)
    def _():
        wbf_ref[...] = w_ref[...].astype(BF16)

    acc = _dot_nt(a_ref[...], wbf_ref[...])
    q_tiles = NSA_QD // acc.shape[1]
    c = cos_ref[...]
    s = sin_ref[...]

    def head(h):
        return acc[:, h * NSA_HEAD_DIM:(h + 1) * NSA_HEAD_DIM]

    def rotary(xh):
        return (xh * c + pltpu.roll(xh, NSA_HEAD_DIM // 2, 1) * s).astype(BF16)

    for h in range(NSA_KV_HEADS):
        o_ref[h] = rotary(head(h))

    @pl.when(j < q_tiles)
    def _():
        for h in range(NSA_KV_HEADS, 2 * NSA_KV_HEADS):
            o_ref[h] = rotary(head(h))

    @pl.when(j >= q_tiles)
    def _():
        for h in range(NSA_KV_HEADS, 2 * NSA_KV_HEADS):
            o_ref[h] = head(h).astype(BF16)


def _nsa_inproj(hn, w_in_t, cos_full, sin_signed):
    tn = 2 * NSA_KVD
    n_heads_tile = tn // NSA_HEAD_DIM
    s_tiles = SEQ // MM_TM
    return pl.pallas_call(
        _inproj_heads_kernel, name="nsa_inproj", grid=(NSA_MAIN // tn, TOKENS // MM_TM),
        in_specs=[pl.BlockSpec((MM_TM, D_MODEL), lambda j, i: (i, 0)),
                  pl.BlockSpec((tn, D_MODEL), lambda j, i: (j, 0)),
                  pl.BlockSpec((MM_TM, NSA_HEAD_DIM), lambda j, i: (i % s_tiles, 0)),
                  pl.BlockSpec((MM_TM, NSA_HEAD_DIM), lambda j, i: (i % s_tiles, 0))],
        out_specs=pl.BlockSpec((n_heads_tile, MM_TM, NSA_HEAD_DIM), lambda j, i: (j, i, 0)),
        out_shape=jax.ShapeDtypeStruct((NSA_MAIN // NSA_HEAD_DIM, TOKENS, NSA_HEAD_DIM), BF16),
        scratch_shapes=[pltpu.VMEM((tn, D_MODEL), BF16)],
        compiler_params=_cparams(("arbitrary", "arbitrary")),
    )(hn, w_in_t, cos_full, sin_signed)


def _matmul_tiles_kernel(a_ref, w_ref, o_ref, wbf_ref):
    @pl.when(pl.program_id(1) == 0)
    def _():
        wbf_ref[...] = w_ref[...].astype(BF16)

    acc = _dot_nt(a_ref[...], wbf_ref[...])
    for t in range(acc.shape[1] // LANES):
        o_ref[t] = acc[:, t * LANES:(t + 1) * LANES].astype(o_ref.dtype)


def _matmul_tiles(a, w_t, n_cols, tn):
    k = a.shape[1]
    return pl.pallas_call(
        _matmul_tiles_kernel, name="matmul_tiles", grid=(n_cols // tn, TOKENS // MM_TM),
        in_specs=[pl.BlockSpec((MM_TM, k), lambda j, i: (i, 0)),
                  pl.BlockSpec((tn, k), lambda j, i: (j, 0))],
        out_specs=pl.BlockSpec((tn // LANES, MM_TM, LANES), lambda j, i: (j, i, 0)),
        out_shape=jax.ShapeDtypeStruct((n_cols // LANES, TOKENS, LANES), BF16),
        scratch_shapes=[pltpu.VMEM((tn, k), BF16)],
        compiler_params=_cparams(("arbitrary", "arbitrary")),
    )(a, w_t)


def _outproj_route_kernel(a_ref, w_ref, r_ref, g_ref, ws_ref, b_ref, o_ref, eid_ref, cw_ref):
    if len(a_ref.shape) == 3:
        a = jnp.concatenate([a_ref[t] for t in range(a_ref.shape[0])], axis=1)
    else:
        a = a_ref[...]
    x_new = r_ref[...] + _dot(a, w_ref[...])
    o_ref[...] = x_new
    eid_ref[...], cw_ref[...] = _route(x_new, g_ref[...], ws_ref[...], b_ref[...])


def _outproj_route(a, w, resid, ln_ffn_g, w_group, b_group, w_expert, b_expert):
    tiled = a.ndim == 3
    k, n = w.shape
    n_r = MOE_GROUPS + MOE_EXPERTS
    ws = jnp.zeros((D_MODEL, LANES), F32).at[:, :n_r].set(jnp.concatenate([w_group, w_expert], axis=1))
    bs = jnp.zeros((1, LANES), F32).at[0, :n_r].set(jnp.concatenate([b_group, b_expert]))
    a_spec = (pl.BlockSpec((k // LANES, OUT_TM, LANES), lambda i: (0, i, 0)) if tiled
              else pl.BlockSpec((OUT_TM, k), lambda i: (i, 0)))
    small = pl.BlockSpec((OUT_TM, LANES), lambda i: (i, 0))
    return pl.pallas_call(
        _outproj_route_kernel, name="outproj_route", grid=(TOKENS // OUT_TM,),
        in_specs=[a_spec,
                  pl.BlockSpec((k, n), lambda i: (0, 0)),
                  pl.BlockSpec((OUT_TM, n), lambda i: (i, 0)),
                  pl.BlockSpec((1, D_MODEL), lambda i: (0, 0)),
                  pl.BlockSpec((D_MODEL, LANES), lambda i: (0, 0)),
                  pl.BlockSpec((1, LANES), lambda i: (0, 0))],
        out_specs=[pl.BlockSpec((OUT_TM, n), lambda i: (i, 0)), small, small],
        out_shape=[jax.ShapeDtypeStruct((TOKENS, n), F32),
                   jax.ShapeDtypeStruct((TOKENS, LANES), I32),
                   jax.ShapeDtypeStruct((TOKENS, LANES), F32)],
        compiler_params=_cparams(("parallel",)),
    )(a, w.astype(BF16), resid, ln_ffn_g.reshape(1, D_MODEL), ws, bs)


def _compress_kernel(x_ref, pe_ref, w1_ref, w2_ref, o_ref):
    half = CMP_STRIDE * NSA_HEAD_DIM
    x = x_ref[0, 0, 0]
    w1 = w1_ref[0].astype(BF16)
    top = _dot(x, w1[:half])
    bot = _dot(x, w1[half:])
    pe = jnp.broadcast_to(pe_ref[0], (8, 2 * half)).astype(BF16)
    pe_bias = _dot(pe, w1)[0:1]
    hid = top + pltpu.roll(bot, bot.shape[0] - 1, 0) + pe_bias
    act = jax.nn.gelu(hid)
    o_ref[0, 0, 0] = _dot(act.astype(BF16), w2_ref[0].astype(BF16))


def _compress(kv_chunks, pe, w1, w2):
    n_chunk = SEQ // CMP_STRIDE
    feat = CMP_STRIDE * NSA_HEAD_DIM
    return pl.pallas_call(
        _compress_kernel, name="nsa_compress", grid=(2, NSA_KV_HEADS, BATCH),
        in_specs=[pl.BlockSpec((1, 1, 1, n_chunk, feat), lambda a, g, b: (a, g, b, 0, 0)),
                  pl.BlockSpec((1, 1, 2 * feat), lambda a, g, b: (a, 0, 0)),
                  pl.BlockSpec((1, 2 * feat, CMP_HIDDEN), lambda a, g, b: (a, 0, 0)),
                  pl.BlockSpec((1, CMP_HIDDEN, NSA_HEAD_DIM), lambda a, g, b: (a, 0, 0))],
        out_specs=pl.BlockSpec((1, 1, 1, n_chunk, NSA_HEAD_DIM), lambda a, g, b: (a, g, b, 0, 0)),
        out_shape=jax.ShapeDtypeStruct((2, NSA_KV_HEADS, BATCH, n_chunk, NSA_HEAD_DIM), F32),
        compiler_params=_cparams(("parallel", "parallel", "parallel")),
    )(kv_chunks, pe.reshape(2, 1, 2 * feat), w1, w2)


def _nsa_attn_kernel(q_ref, ks_ref, vs_ref, kw_ref, vw_ref, kc_ref, vc_ref, gate_ref, o_ref,
                     vst_scr, vwt_scr, sel_scr, acc_scr, sc_scr):
    qi = pl.program_id(2)
    tq = ATT_TQ
    tk = ATT_TK
    dh = NSA_HEAD_DIM
    r_heads = NSA_Q_PER_KV
    n_kt = SEQ // tk
    scale = dh ** -0.5 * LOG2_E

    @pl.when(qi == 0)
    def _():
        ones = jnp.ones((ATT_ONES, tk), BF16)
        for kt in range(n_kt):
            rows = slice(kt * tk, (kt + 1) * tk)
            vst_scr[kt, 0:dh, :] = vs_ref[0, rows, :].astype(F32).T.astype(BF16)
            vwt_scr[kt, 0:dh, :] = vw_ref[0, rows, :].astype(F32).T.astype(BF16)
            vst_scr[kt, dh:dh + ATT_ONES, :] = ones
            vwt_scr[kt, dh:dh + ATT_ONES, :] = ones

    q_s = jnp.concatenate([q_ref[r].astype(F32).T for r in range(r_heads)], axis=1) * scale
    q_hi = q_s.astype(BF16)
    q_t = jnp.concatenate([q_hi, (q_s - q_hi.astype(F32)).astype(BF16)], axis=0)

    def qk(k):
        return _dot(jnp.concatenate([k, k], axis=1), q_t)
    n_cp = SEQ // CMP_STRIDE
    sub = lax.broadcasted_iota(I32, (n_cp, tq), 0)
    t_pos = qi * tq + lax.broadcasted_iota(I32, (n_cp, tq), 1)

    def tile4(a):
        return jnp.concatenate([a] * r_heads, axis=1)

    kc = kc_ref[0, 0, 0].astype(BF16)
    vc = vc_ref[0, 0, 0].astype(BF16)
    ok_c = jnp.where(sub * CMP_STRIDE + CMP_BLOCK - 1 <= t_pos, jnp.where(sub < N_CMP, 1.0, 0.0), 0.0)
    ok_c4 = tile4(ok_c)
    s_c = qk(kc) + (ok_c4 - 1.0) * (-NEG_INF)
    e_c = jnp.exp2(s_c - jnp.max(s_c, axis=0, keepdims=True))
    p_c = (e_c / jnp.sum(e_c, axis=0, keepdims=True)) * ok_c4
    o_cmp = _dot_tn(vc, p_c.astype(BF16))
    p_sum = p_c[:, 0:tq]
    for r in range(1, r_heads):
        p_sum = p_sum + p_c[:, r * tq:(r + 1) * tq]

    blk_row = lax.broadcasted_iota(I32, (LANES, LANES), 0)
    cmp_col = lax.broadcasted_iota(I32, (LANES, LANES), 1)
    s_start = blk_row * SLC_BLOCK
    c_start = cmp_col * CMP_STRIDE
    ov_t = jnp.maximum(jnp.minimum(c_start + CMP_BLOCK, s_start + SLC_BLOCK)
                       - jnp.maximum(c_start, s_start), 0).astype(F32) / CMP_BLOCK
    ov_t = jnp.where(blk_row < N_SLC, ov_t, 0.0).astype(BF16)
    imp = _dot_split_rhs(ov_t, p_sum)[0:N_SLC]
    j_blk = lax.broadcasted_iota(I32, (N_SLC, tq), 0)
    dist = (qi * tq + lax.broadcasted_iota(I32, (N_SLC, tq), 1)) // SLC_BLOCK - j_blk
    imp = jnp.where(j_blk == 0, SLC_FORCE, jnp.where(dist < 0, imp, jnp.where(dist < SLC_LOCAL, SLC_FORCE, imp)))
    imp = jnp.where(dist >= 0, imp, -jnp.inf)
    cnt = jnp.zeros((N_SLC, tq), I32)
    for k in range(N_SLC):
        row_k = imp[k:k + 1, :]
        tie = jnp.where(j_blk > k, 1, 0)
        cnt = cnt + jnp.where(row_k > imp, 1, jnp.where(row_k == imp, tie, 0))
    sel = jnp.where(cnt < min(SLC_TOPK, N_SLC), 1.0, 0.0)
    for j in range(N_SLC):
        sel_scr[8 * j:8 * j + 8, :] = jnp.broadcast_to(sel[j:j + 1, :], (8, tq))

    def scores(k, ok):
        return qk(k) + tile4((ok - 1.0) * (-NEG_INF))

    def weighted_values(vt_scr, kt0, pr, n_tiles):
        out = None
        for u in range(n_tiles):
            term = _dot(vt_scr[kt0 + u], pr[u * tk:(u + 1) * tk].astype(BF16))
            out = term if out is None else out + term
        return out

    n_ct = ATT_CHUNK_TILES
    ck = n_ct * tk
    sub_c = lax.broadcasted_iota(I32, (ck, tq), 0)
    t_pos_c = qi * tq + lax.broadcasted_iota(I32, (ck, tq), 1)
    blocks_per_chunk = ck // SLC_BLOCK
    acc_scr[...] = jnp.zeros(acc_scr.shape, F32)

    def chunk_scores(c):
        c = jnp.minimum(c, SEQ // ck - 1)
        start = pl.multiple_of(c * ck, ck)
        k = ks_ref[0, pl.ds(start, ck), :]
        rows8 = sel_scr[pl.ds(pl.multiple_of(c * (8 * blocks_per_chunk), 8 * blocks_per_chunk),
                              8 * blocks_per_chunk), :]
        picked = jnp.concatenate(
            [rows8[8 * u:8 * u + 8] for u in range(blocks_per_chunk) for _ in range(SLC_BLOCK // 8)], axis=0)
        return scores(k, jnp.where(start + sub_c <= t_pos_c, picked, 0.0))

    sc_scr[...] = chunk_scores(0)

    def slc_body(c, m_old):
        sc = sc_scr[...]
        sc_next = chunk_scores(c + 1)
        m_new = jnp.maximum(m_old, jnp.max(sc, axis=0, keepdims=True))
        alpha = jnp.exp2(m_old - m_new)
        pr = jnp.exp2(sc - m_new)
        acc_scr[...] = alpha * acc_scr[...] + weighted_values(vst_scr, c * n_ct, pr, n_ct)
        sc_scr[...] = sc_next
        return m_new

    lax.fori_loop(0, ((qi + 1) * tq + ck - 1) // ck, slc_body, jnp.full((1, r_heads * tq), NEG_INF, F32))
    acc = acc_scr[...]
    o_slc = acc[0:dh] / acc[dh:dh + 1]

    n_wt = (WINDOW + tq) // tk
    kt0 = jnp.maximum(qi * (tq // tk) - WINDOW // tk, 0)
    w_start = pl.multiple_of(kt0 * tk, tk)
    key_w = w_start + lax.broadcasted_iota(I32, (n_wt * tk, tq), 0)
    t_pos_w = qi * tq + lax.broadcasted_iota(I32, (n_wt * tk, tq), 1)
    ok_w = jnp.where(key_w <= t_pos_w, jnp.where(key_w > t_pos_w - WINDOW, 1.0, 0.0), 0.0)
    sc_w = scores(kw_ref[0, pl.ds(w_start, n_wt * tk), :], ok_w)
    pr_w = jnp.exp2(sc_w - jnp.max(sc_w, axis=0, keepdims=True))
    acc_w = weighted_values(vwt_scr, kt0, pr_w, n_wt)
    o_win = acc_w[0:dh] / acc_w[dh:dh + 1]

    gate = jax.nn.sigmoid(gate_ref[0])
    for r in range(r_heads):
        cols = slice(r * tq, (r + 1) * tq)
        o = (gate[3 * r:3 * r + 1] * o_cmp[:, cols] + gate[3 * r + 1:3 * r + 2] * o_slc[:, cols]
             + gate[3 * r + 2:3 * r + 3] * o_win[:, cols])
        o_ref[:, r * dh:(r + 1) * dh] = o.T.astype(BF16)


def _nsa_attention(heads, kc_vc, gates):
    tq = ATT_TQ
    nq = SEQ // tq
    r = NSA_Q_PER_KV
    g_heads = NSA_KV_HEADS
    q_spec = pl.BlockSpec((r, tq, NSA_HEAD_DIM), lambda b, g, i: (g, b * nq + i, 0))

    def kv_spec(first_head):
        return pl.BlockSpec((1, SEQ, NSA_HEAD_DIM), lambda b, g, i: (first_head + g, b, 0))

    first = NSA_HEADS
    specs = [q_spec,
             kv_spec(first + 2 * g_heads), kv_spec(first + 3 * g_heads),
             kv_spec(first + 4 * g_heads), kv_spec(first + 5 * g_heads),
             pl.BlockSpec((1, 1, 1, SEQ // CMP_STRIDE, NSA_HEAD_DIM), lambda b, g, i: (0, g, b, 0, 0)),
             pl.BlockSpec((1, 1, 1, SEQ // CMP_STRIDE, NSA_HEAD_DIM), lambda b, g, i: (1, g, b, 0, 0)),
             pl.BlockSpec((1, 3 * r, tq), lambda b, g, i: (g, 0, b * nq + i))]
    vt_shape = (SEQ // ATT_TK, NSA_HEAD_DIM + ATT_ONES, ATT_TK)
    return pl.pallas_call(
        _nsa_attn_kernel, name="nsa_attn", grid=(BATCH, g_heads, nq),
        in_specs=specs,
        out_specs=pl.BlockSpec((tq, r * NSA_HEAD_DIM), lambda b, g, i: (b * nq + i, g)),
        out_shape=jax.ShapeDtypeStruct((TOKENS, NSA_QD), BF16),
        scratch_shapes=[pltpu.VMEM(vt_shape, BF16), pltpu.VMEM(vt_shape, BF16),
                        pltpu.VMEM((8 * N_SLC, tq), F32),
                        pltpu.VMEM((NSA_HEAD_DIM + ATT_ONES, r * tq), F32),
                        pltpu.VMEM((ATT_CHUNK_TILES * ATT_TK, r * tq), F32)],
        compiler_params=_cparams(("arbitrary", "arbitrary", "arbitrary")),
    )(heads, heads, heads, heads, heads, kc_vc, kc_vc, gates)


def _ssd_chunk_kernel(zx_ref, dtc_ref, dtr_ref, cw_ref, cb_ref, dtb_c_ref, alog_c_ref, dtb_r_ref, alog_r_ref,
                      dskip_ref, ng_ref, echan_ref, o_ref,
                      prev_scr, acum_r_scr, st_scr, fac_scr):
    chunk = pl.program_id(1)
    L = SSD_CHUNK
    W = SSD_GROUP_W
    hpg = SSD_HEADS_PER_GROUP
    n_xt = W // LANES
    x0 = SSD_D_INNER // LANES
    b0 = 2 * SSD_D_INNER // LANES
    c0 = b0 + SSD_GROUPS
    cb0 = SSD_D_INNER // LANES
    cc0 = cb0 + SSD_GROUPS

    @pl.when(chunk == 0)
    def _():
        prev_scr[...] = jnp.zeros(prev_scr.shape, F32)
        st_scr[...] = jnp.zeros(st_scr.shape, F32)

    dt_c = jax.nn.softplus(dtc_ref[...] + dtb_c_ref[...])
    adt_c = dt_c * (-jnp.exp(alog_c_ref[...]))
    dt_r = jax.nn.softplus(dtr_ref[...] + dtb_r_ref[...])
    adt_r = dt_r * (-jnp.exp(alog_r_ref[...]))
    row = lax.broadcasted_iota(I32, (L, L), 0)
    col = lax.broadcasted_iota(I32, (L, L), 1)
    causal = row >= col
    tri = jnp.where(causal, 1.0, 0.0).astype(BF16)
    tri_t = jnp.where(col >= row, 1.0, 0.0).astype(BF16)
    acum_c = _dot_split_rhs(tri, adt_c)
    acum_r_scr[...] = _dot_split_lhs(adt_r, tri_t)
    a_last = acum_c[L - 1:L, :]
    fac = jnp.concatenate([dt_c, jnp.exp(acum_c), jnp.exp(a_last - acum_c),
                           jnp.broadcast_to(jnp.exp(a_last), (SSD_FAC_PAD, LANES))], axis=0)
    fac_hi = fac.astype(BF16)
    fac_scr[0] = fac_hi
    fac_scr[1] = (fac - fac_hi.astype(F32)).astype(BF16)
    lane_w = lax.broadcasted_iota(I32, (L, LANES), 1)
    first_half = lane_w < SSD_HEAD_DIM

    def tiles(ref, first, n):
        return jnp.concatenate([ref[first + q] for q in range(n)], axis=1)

    def group_body(g, carry):
        e_chan = echan_ref[g]
        ex = _dot(fac_scr[0, 0:3 * L], e_chan)
        dt_x, ea_x, sd_x = ex[0:L], ex[L:2 * L], ex[2 * L:3 * L]
        cd_x = (_dot(fac_scr[0, 3 * L:3 * L + SSD_FAC_PAD], e_chan)
                + _dot(fac_scr[1, 3 * L:3 * L + SSD_FAC_PAD], e_chan))[0:1]

        def conv_silu(zx_first, conv_first, n):
            cur = tiles(zx_ref, zx_first, n)
            cur_f = cur.astype(F32)
            tail = tiles(prev_scr, conv_first, n)
            row8 = lax.broadcasted_iota(I32, (8, n * LANES), 0)
            w = tiles(cw_ref, conv_first, n)
            acc = jnp.broadcast_to(tiles(cb_ref, conv_first, n), (L, n * LANES))
            for k in range(SSD_CONV):
                back = SSD_CONV - 1 - k
                if back == 0:
                    xk = cur_f
                else:
                    rolled = pltpu.roll(cur_f, back, 0)
                    head = jnp.where(row8 < back, pltpu.roll(tail, back, 0), rolled[0:8])
                    xk = jnp.concatenate([head, rolled[8:]], axis=0)
                acc = acc + xk * w[k:k + 1, :]
            for q in range(n):
                prev_scr[conv_first + q] = cur_f[L - 8:L, q * LANES:(q + 1) * LANES]
            return jax.nn.silu(acc)

        xs = conv_silu(x0 + n_xt * g, n_xt * g, n_xt)
        bm = conv_silu(b0 + g, cb0 + g, 1)
        cm = conv_silu(c0 + g, cc0 + g, 1)

        xdt = xs * dt_x
        cb = jnp.where(causal, _dot_nt(cm.astype(BF16), bm.astype(BF16)), 0.0)
        y_parts = []
        for pair in range(hpg // 2):
            xd = xdt[:, pair * LANES:(pair + 1) * LANES]
            m_pair = []
            for sub in range(2):
                a_row = jnp.broadcast_to(acum_r_scr[pl.ds(g * hpg + 2 * pair + sub, 1), :], (L, L))
                seg = jnp.minimum(a_row.T - a_row, 0.0)
                m_pair.append((cb * jnp.exp(seg)).astype(BF16))
            x_pair = jnp.concatenate([jnp.where(first_half, xd, 0.0), jnp.where(first_half, 0.0, xd)], axis=0)
            y_parts.append(_dot(jnp.concatenate(m_pair, axis=1), x_pair.astype(BF16)))
        y_diag = jnp.concatenate(y_parts, axis=1)

        st = st_scr[g]
        y_off = _dot(cm.astype(BF16), st.astype(BF16)) * ea_x
        st_scr[g] = st * cd_x + _dot_tn(bm.astype(BF16), (xdt * sd_x).astype(BF16))

        y = y_diag + y_off + xs * tiles(dskip_ref, n_xt * g, n_xt)
        y = y * jax.nn.silu(tiles(zx_ref, n_xt * g, n_xt).astype(F32))
        y = y * lax.rsqrt(jnp.mean(y * y, axis=-1, keepdims=True) + NORM_EPS)
        y = y * tiles(ng_ref, n_xt * g, n_xt)
        for q in range(n_xt):
            o_ref[n_xt * g + q] = y[:, q * LANES:(q + 1) * LANES].astype(BF16)
        return carry

    lax.fori_loop(0, SSD_GROUPS, group_body, 0)


def _ssd_chunks(zx_tiles, dt_small, dt_small_t, conv_w, conv_b, dt_bias, a_log, d_skip, norm_g):
    L = SSD_CHUNK
    nc = SEQ // L
    n_zx = SSD_MAIN // LANES
    n_conv = SSD_CONV_CH // LANES
    n_inner = SSD_D_INNER // LANES
    hpg = SSD_HEADS_PER_GROUP

    def pad_heads(v):
        return jnp.zeros((LANES,), F32).at[:SSD_HEADS].set(v)

    dtb = pad_heads(dt_bias)
    alog = pad_heads(a_log)
    cw = conv_w.reshape(SSD_CONV, n_conv, LANES).transpose(1, 0, 2)
    cb = conv_b.reshape(n_conv, 1, LANES)
    d_chan = jnp.repeat(d_skip, SSD_HEAD_DIM).reshape(n_inner, 1, LANES)
    ng = norm_g.reshape(n_inner, 1, LANES)
    head =jnp.arange(LANES, dtype=I32)[None, :, None]
    grp = jnp.arange(SSD_GROUPS, dtype=I32)[:, None, None]
    e_chan = (head == grp * hpg + jnp.arange(SSD_GROUP_W, dtype=I32)[None, None, :] // SSD_HEAD_DIM).astype(BF16)
    row = lambda b, c: b * nc + c
    const3 = lambda b, c: (0, 0, 0)
    const2 = lambda b, c: (0, 0)
    in_specs = [
        pl.BlockSpec((n_zx, L, LANES), lambda b, c: (0, row(b, c), 0)),
        pl.BlockSpec((L, LANES), lambda b, c: (row(b, c), 0)),
        pl.BlockSpec((LANES, L), lambda b, c: (0, row(b, c))),
        pl.BlockSpec((n_conv, SSD_CONV, LANES), const3),
        pl.BlockSpec((n_conv, 1, LANES), const3),
        pl.BlockSpec((1, LANES), const2), pl.BlockSpec((1, LANES), const2),
        pl.BlockSpec((LANES, 1), const2), pl.BlockSpec((LANES, 1), const2),
        pl.BlockSpec((n_inner, 1, LANES), const3),
        pl.BlockSpec((n_inner, 1, LANES), const3),
        pl.BlockSpec((SSD_GROUPS, LANES, SSD_GROUP_W), const3),
    ]
    return pl.pallas_call(
        _ssd_chunk_kernel, name="ssd_chunks", grid=(BATCH, nc),
        in_specs=in_specs,
        out_specs=pl.BlockSpec((n_inner, L, LANES), lambda b, c: (0, row(b, c), 0)),
        out_shape=jax.ShapeDtypeStruct((n_inner, TOKENS, LANES), BF16),
        scratch_shapes=[pltpu.VMEM((n_conv, 8, LANES), F32),
                        pltpu.VMEM((LANES, L), F32),
                        pltpu.VMEM((SSD_GROUPS, SSD_D_STATE, SSD_GROUP_W), F32),
                        pltpu.VMEM((2, 3 * L + SSD_FAC_PAD, LANES), BF16)],
        compiler_params=_cparams(("arbitrary", "arbitrary")),
    )(zx_tiles, dt_small, dt_small_t, cw, cb, dtb.reshape(1, LANES), alog.reshape(1, LANES),
      dtb.reshape(LANES, 1), alog.reshape(LANES, 1), d_chan, ng, e_chan)


def _gather_rows(src_hbm, idx_ref, base, dst, sem, n_rows):
    def body(b, carry):
        for u in range(GATHER_UNROLL):
            r = b * GATHER_UNROLL + u
            tok = idx_ref[base + r]
            pltpu.make_async_copy(src_hbm.at[pl.ds(tok, 1), :], dst.at[pl.ds(r, 1), :], sem).start()
        return carry

    lax.fori_loop(0, n_rows // GATHER_UNROLL, body, 0)


def _moe_ffn_kernel(te_ref, pos_ref, nact_ref, pad_ref, wp_ref, x_hbm, g_ref, wg_hbm, wu_hbm, wd_hbm, y_ref,
                    tok_ref, buf, sem, wg_f, wu_f, wd_f, wsem, wg_bf, wu_bf, wd_bf, *, layer):
    i = pl.program_id(0)
    n_act = nact_ref[0]
    tm = MOE_TM
    slot = i % 2
    run_start = wp_ref[i] == 1
    w_slot = wp_ref[MOE_TILES + i]
    next_expert = wp_ref[2 * MOE_TILES + i]
    n_rows = pl.multiple_of(wp_ref[3 * MOE_TILES + i], GATHER_UNROLL)

    def gather(tile, s):
        _gather_rows(x_hbm, tok_ref, tile * tm, buf.at[s], sem.at[s], wp_ref[3 * MOE_TILES + tile])

    def weight_copies(e, s):
        return (pltpu.make_async_copy(wg_hbm.at[layer, e], wg_f.at[s], wsem.at[s, 0]),
                pltpu.make_async_copy(wu_hbm.at[layer, e], wu_f.at[s], wsem.at[s, 1]),
                pltpu.make_async_copy(wd_hbm.at[layer, e], wd_f.at[s], wsem.at[s, 2]))

    @pl.when(i == 0)
    def _():
        for c in weight_copies(te_ref[0], 0):
            c.start()
        buf[...] = jnp.zeros(buf.shape, F32)
        def clear(q, carry):
            tok_ref[q] = 0
            return carry

        def clear_padding(e, carry):
            lax.fori_loop(pad_ref[e], pad_ref[MOE_EXPERTS + e], clear, 0)
            return carry

        lax.fori_loop(0, MOE_EXPERTS, clear_padding, 0)
        for k in range(MOE_TOPK):
            def place(t, carry, k=k):
                tok_ref[pos_ref[k * TOKENS + t]] = t
                return carry

            lax.fori_loop(0, TOKENS, place, 0, unroll=GATHER_UNROLL)
        gather(0, 0)

    @pl.when(run_start)
    def _():
        for c in weight_copies(te_ref[i], w_slot):
            c.wait()

        @pl.when(next_expert >= 0)
        def _():
            for c in weight_copies(next_expert, 1 - w_slot):
                c.start()

        wg_bf[...] = wg_f[w_slot].astype(BF16)
        wu_bf[...] = wu_f[w_slot].astype(BF16)
        wd_bf[...] = wd_f[w_slot].astype(BF16)

    @pl.when(i >= n_act)
    def _():
        y_ref[...] = jnp.zeros(y_ref.shape, F32)

    @pl.when(i < n_act)
    def _():
        pltpu.make_async_copy(x_hbm.at[pl.ds(0, n_rows), :], buf.at[slot, pl.ds(0, n_rows), :],
                              sem.at[slot]).wait()

        @pl.when(i + 1 < n_act)
        def _():
            gather(i + 1, 1 - slot)

        h = _rms(buf[slot], g_ref[...]).astype(BF16)
        act = jax.nn.silu(_dot(h, wg_bf[...])) * _dot(h, wu_bf[...])
        y_ref[...] = _dot(act.astype(BF16), wd_bf[...])


def _moe_ffn(x, g, w_gate, w_up, w_down, layer, tile_expert, pos_kmajor, n_active, pad_rows, weight_plan):
    any_spec = pl.BlockSpec(memory_space=pl.ANY)
    grid_spec = pltpu.PrefetchScalarGridSpec(
        num_scalar_prefetch=5, grid=(MOE_TILES,),
        in_specs=[any_spec, pl.BlockSpec((1, D_MODEL), lambda i, *_: (0, 0)), any_spec, any_spec, any_spec],
        out_specs=pl.BlockSpec((MOE_TM, D_MODEL), lambda i, *_: (i, 0)),
        scratch_shapes=[pltpu.SMEM((MOE_ROWS,), I32),
                        pltpu.VMEM((2, MOE_TM, D_MODEL), F32), pltpu.SemaphoreType.DMA((2,)),
                        pltpu.VMEM((2, D_MODEL, MOE_D_FF), F32), pltpu.VMEM((2, D_MODEL, MOE_D_FF), F32),
                        pltpu.VMEM((2, MOE_D_FF, D_MODEL), F32), pltpu.SemaphoreType.DMA((2, 3)),
                        pltpu.VMEM((D_MODEL, MOE_D_FF), BF16), pltpu.VMEM((D_MODEL, MOE_D_FF), BF16),
                        pltpu.VMEM((MOE_D_FF, D_MODEL), BF16)])
    return pl.pallas_call(
        functools.partial(_moe_ffn_kernel, layer=layer), name="moe_ffn", grid_spec=grid_spec,
        out_shape=jax.ShapeDtypeStruct((MOE_ROWS, D_MODEL), F32),
        compiler_params=_cparams(("arbitrary",)),
    )(tile_expert, pos_kmajor, n_active, pad_rows, weight_plan, x, g.reshape(1, D_MODEL), w_gate, w_up, w_down)


def _moe_combine_kernel(pos_ref, x_ref, cw_ref, g_ref, ws_ref, y_hbm, *refs, final_norm):
    if final_norm:
        o_ref, buf, sem = refs
    else:
        o_ref, hn_ref, small_ref, small_t_ref, buf, sem = refs
    i = pl.program_id(0)
    n = pl.num_programs(0)
    tm = CMB_TM
    slot = i % 2

    def issue(tile, s):
        for k in range(MOE_TOPK):
            _gather_rows(y_hbm, pos_ref, (k * (TOKENS // tm) + tile) * tm, buf.at[s, k], sem.at[s], tm)

    @pl.when(i == 0)
    def _():
        issue(0, 0)

    @pl.when(i + 1 < n)
    def _():
        issue(i + 1, 1 - slot)

    for k in range(MOE_TOPK):
        pltpu.make_async_copy(y_hbm.at[pl.ds(0, tm), :], buf.at[slot, k], sem.at[slot]).wait()
    cw = cw_ref[...]
    out = x_ref[...] + cw[:, 0:1] * buf[slot, 0] + cw[:, 1:2] * buf[slot, 1]
    y = _rms(out, g_ref[...])
    if final_norm:
        o_ref[...] = y
    else:
        o_ref[...] = out
        hn_ref[...] = y.astype(BF16)
        small = _dot_x3(y, ws_ref[...])
        small_ref[...] = small
        small_t_ref[...] = small.T


def _moe_combine(x, cw, y_sorted, pos_kmajor, g_norm, w_small, final_norm):
    ws = jnp.zeros((D_MODEL, LANES), F32)
    if not final_norm:
        ws = ws.at[:, :w_small.shape[1]].set(w_small)
    row = pl.BlockSpec((CMB_TM, D_MODEL), lambda i, pos: (i, 0))
    small = pl.BlockSpec((CMB_TM, LANES), lambda i, pos: (i, 0))
    x_shape = jax.ShapeDtypeStruct((TOKENS, D_MODEL), F32)
    if final_norm:
        out_specs, out_shape = row, x_shape
    else:
        out_specs = [row, row, small, pl.BlockSpec((LANES, CMB_TM), lambda i, pos: (0, i))]
        out_shape = [x_shape, jax.ShapeDtypeStruct((TOKENS, D_MODEL), BF16),
                     jax.ShapeDtypeStruct((TOKENS, LANES), F32), jax.ShapeDtypeStruct((LANES, TOKENS), F32)]
    grid_spec = pltpu.PrefetchScalarGridSpec(
        num_scalar_prefetch=1, grid=(TOKENS // CMB_TM,),
        in_specs=[row, small,
                  pl.BlockSpec((1, D_MODEL), lambda i, pos: (0, 0)),
                  pl.BlockSpec((D_MODEL, LANES), lambda i, pos: (0, 0)),
                  pl.BlockSpec(memory_space=pl.ANY)],
        out_specs=out_specs,
        scratch_shapes=[pltpu.VMEM((2, MOE_TOPK, CMB_TM, D_MODEL), F32), pltpu.SemaphoreType.DMA((2,))])
    return pl.pallas_call(
        functools.partial(_moe_combine_kernel, final_norm=final_norm), name="moe_combine", grid_spec=grid_spec,
        out_shape=out_shape,
        compiler_params=_cparams(("arbitrary",)),
    )(pos_kmajor, x, cw, g_norm.reshape(1, D_MODEL), ws, y_sorted)


def _moe_plan(eid):
    e = eid[:, :MOE_TOPK].reshape(-1)
    onehot = (e[:, None] == jnp.arange(MOE_EXPERTS, dtype=I32)[None, :]).astype(I32)
    csum = jnp.cumsum(onehot, axis=0)
    counts = csum[-1]
    padded = ((counts + MOE_TM - 1) // MOE_TM) * MOE_TM
    g_end = jnp.cumsum(padded)
    g_start = g_end - padded
    pos = jnp.sum(onehot * (g_start[None, :] + csum - 1), axis=1)
    n_active = (g_end[-1] // MOE_TM).astype(I32)
    tile_start = jnp.arange(MOE_TILES, dtype=I32) * MOE_TM
    te = jnp.sum((g_end[None, :] <= tile_start[:, None]).astype(I32), axis=1)
    last = jnp.max(jnp.where(counts > 0, jnp.arange(MOE_EXPERTS, dtype=I32), 0))
    tile_expert = jnp.minimum(te, last)
    pos_kmajor = pos.reshape(TOKENS, MOE_TOPK).T.reshape(-1)
    pad_rows = jnp.concatenate([g_start + counts, g_end])
    experts = jnp.arange(MOE_EXPERTS, dtype=I32)
    run_start = jnp.concatenate([jnp.ones((1,), I32), (tile_expert[1:] != tile_expert[:-1]).astype(I32)])
    w_slot = (jnp.cumsum(run_start) - 1) % 2
    later = (experts[None, :] > experts[:, None]) & (counts > 0)[None, :]
    next_nonempty = jnp.min(jnp.where(later, experts[None, :], MOE_EXPERTS), axis=1)
    next_nonempty = jnp.where(next_nonempty == MOE_EXPERTS, -1, next_nonempty)
    real_rows = jnp.clip((g_start + counts)[tile_expert] - tile_start, 0, MOE_TM)
    real_rows = jnp.where(tile_start < g_end[-1], real_rows, 0)
    gather_rows = jnp.minimum(-(-real_rows // GATHER_UNROLL) * GATHER_UNROLL, MOE_TM)
    weight_plan = jnp.concatenate([run_start, w_slot, next_nonempty[tile_expert], gather_rows]).astype(I32)
    return tile_expert, n_active.reshape(1), pos_kmajor, pad_rows, weight_plan


def _hier_moe_add(x, eid, cw, ln_g, w_gate, w_up, w_down, layer, g_norm, w_small, final_norm):
    tile_expert, n_active, pos_kmajor, pad_rows, weight_plan = _moe_plan(eid)
    y_sorted = _moe_ffn(x, ln_g, w_gate, w_up, w_down, layer, tile_expert, pos_kmajor, n_active, pad_rows,
                        weight_plan)
    return _moe_combine(x, cw, y_sorted, pos_kmajor, g_norm, w_small, final_norm)


def _rope_tables():
    pos = jnp.arange(SEQ, dtype=F32)
    inv = 1.0 / (ROPE_THETA ** (jnp.arange(0, NSA_HEAD_DIM, 2, dtype=F32) / NSA_HEAD_DIM))
    ang = pos[:, None] * inv[None, :]
    cos, sin = jnp.cos(ang), jnp.sin(ang)
    return jnp.concatenate([cos, cos], axis=1), jnp.concatenate([-sin, sin], axis=1)


def _nsa_mixer(hn, g_lin_t, w_in, cmp_pe, cmp_w1, cmp_w2):
    w_in_t = w_in.T
    cos_full, sin_signed = _rope_tables()
    heads = _nsa_inproj(hn, w_in_t, cos_full, sin_signed)
    first_c = NSA_HEADS
    kv_c = heads[first_c:first_c + 2 * NSA_KV_HEADS]
    kv_chunks = kv_c.reshape(2, NSA_KV_HEADS, BATCH, SEQ // CMP_STRIDE, CMP_STRIDE * NSA_HEAD_DIM)
    kc_vc = _compress(kv_chunks, cmp_pe, cmp_w1, cmp_w2)
    gates_t = g_lin_t[:NSA_GATES].reshape(NSA_KV_HEADS, 3 * NSA_Q_PER_KV, TOKENS)
    return _nsa_attention(heads, kc_vc, gates_t)


def _ssd_mixer(hn, dt_small, dt_small_t, w_in, conv_w, conv_b, dt_bias, a_log, d_skip, norm_g):
    zx_tiles = _matmul_tiles(hn, w_in.T, SSD_MAIN, 1024)
    return _ssd_chunks(zx_tiles, dt_small, dt_small_t, conv_w, conv_b, dt_bias, a_log, d_skip, norm_g)


def kernel(x, ln_mix, ln_ffn, ln_final, nsa_w_in, nsa_cmp_pe, nsa_cmp_w1, nsa_cmp_w2, nsa_w_out,
           ssd_w_in, ssd_conv_w, ssd_conv_b, ssd_dt_bias, ssd_a_log, ssd_d, ssd_norm, ssd_w_out,
           moe_w_group, moe_b_group, moe_w_expert, moe_b_expert, moe_w_gate, moe_w_up, moe_w_down):
    def small_weight(i):
        if i % N_MIXERS == 0:
            return nsa_w_in[i // N_MIXERS].T[NSA_MAIN:].T
        return ssd_w_in[i // N_MIXERS].T[SSD_MAIN:].T

    h = x.reshape(TOKENS, D_MODEL)
    hn, small, small_t = _norm_small(h, ln_mix[0], small_weight(0))
    for i in range(DEPTH):
        j = i // N_MIXERS
        last = i == DEPTH - 1
        if i % N_MIXERS == 0:
            mix = _nsa_mixer(hn, small_t, nsa_w_in[j], nsa_cmp_pe[j], nsa_cmp_w1[j], nsa_cmp_w2[j])
            w_out = nsa_w_out[j]
        else:
            mix = _ssd_mixer(hn, small, small_t, ssd_w_in[j], ssd_conv_w[j], ssd_conv_b[j], ssd_dt_bias[j],
                             ssd_a_log[j], ssd_d[j], ssd_norm[j])
            w_out = ssd_w_out[j]
        h, eid, cw = _outproj_route(mix, w_out, h, ln_ffn[i], moe_w_group[i], moe_b_group[i],
                                    moe_w_expert[i], moe_b_expert[i])
        out = _hier_moe_add(h, eid, cw, ln_ffn[i], moe_w_gate, moe_w_up, moe_w_down, i,
                            ln_final if last else ln_mix[i + 1], None if last else small_weight(i + 1), last)
        if last:
            h = out
        else:
            h, hn, small, small_t = out
    return h.reshape(BATCH, SEQ, D_MODEL)
```

```python
import functools

import jax
import jax.numpy as jnp
from jax import lax
from jax.experimental import pallas as pl
from jax.experimental.pallas import tpu as pltpu

F32 = jnp.float32
BF16 = jnp.bfloat16
I32 = jnp.int32

D_MODEL = 2048
BATCH = 4
SEQ = 2048
TOKENS = BATCH * SEQ
DEPTH = 2
N_MIXERS = 2
NORM_EPS = 1e-6
NEG_INF = -1e30
LOG2_E = 1.4426950408889634
ROPE_THETA = 10000.0

NSA_HEADS = 16
NSA_KV_HEADS = 4
NSA_HEAD_DIM = D_MODEL // NSA_HEADS
NSA_Q_PER_KV = NSA_HEADS // NSA_KV_HEADS
CMP_BLOCK = 32
CMP_STRIDE = 16
CMP_HIDDEN = 256
N_CMP = (SEQ - CMP_BLOCK) // CMP_STRIDE + 1
SLC_BLOCK = 64
SLC_TOPK = 16
SLC_LOCAL = 2
SLC_FORCE = 1e4
N_SLC = SEQ // SLC_BLOCK
WINDOW = 512
NSA_QD = NSA_HEADS * NSA_HEAD_DIM
NSA_KVD = NSA_KV_HEADS * NSA_HEAD_DIM
NSA_MAIN = NSA_QD + 6 * NSA_KVD
NSA_GATES = 3 * NSA_HEADS

SSD_D_INNER = 2 * D_MODEL
SSD_HEAD_DIM = 64
SSD_HEADS = SSD_D_INNER // SSD_HEAD_DIM
SSD_GROUPS = 8
SSD_HEADS_PER_GROUP = SSD_HEADS // SSD_GROUPS
SSD_D_STATE = 128
SSD_CONV = 4
SSD_CHUNK = 128
SSD_GROUP_W = SSD_D_INNER // SSD_GROUPS
SSD_BC = SSD_GROUPS * SSD_D_STATE
SSD_CONV_CH = SSD_D_INNER + 2 * SSD_BC
SSD_MAIN = SSD_D_INNER + SSD_CONV_CH
SSD_FAC_PAD = 16

MOE_GROUPS = 4
MOE_EPG = 8
MOE_EXPERTS = MOE_GROUPS * MOE_EPG
MOE_TOPK = 2
MOE_D_FF = 512

LANES = 128
VMEM_LIMIT = 56 * 1024 * 1024

NORM_TM = 512
MM_TM = 1024
OUT_TM = 256
ATT_TQ = 256
ATT_TK = 256
ATT_CHUNK_TILES = 2
ATT_ONES = 16
MOE_TM = 256
MOE_TILES = (TOKENS * MOE_TOPK) // MOE_TM + MOE_EXPERTS
MOE_ROWS = MOE_TILES * MOE_TM
CMB_TM = 256
GATHER_UNROLL = 8


def _cparams(sem):
    return pltpu.CompilerParams(dimension_semantics=sem, vmem_limit_bytes=VMEM_LIMIT)


def _split3(x):
    hi = x.astype(BF16)
    r1 = x - hi.astype(F32)
    mid = r1.astype(BF16)
    lo = (r1 - mid.astype(F32)).astype(BF16)
    return hi, mid, lo


def _dot(a, b):
    return jnp.dot(a, b, preferred_element_type=F32)


def _dot_nt(a, b):
    return lax.dot_general(a, b, (((1,), (1,)), ((), ())), preferred_element_type=F32)


def _dot_tn(a, b):
    return lax.dot_general(a, b, (((0,), (0,)), ((), ())), preferred_element_type=F32)


def _dot_split_lhs(x, m_bf16):
    hi, mid, lo = _split3(x)
    return _dot(hi, m_bf16) + _dot(mid, m_bf16) + _dot(lo, m_bf16)


def _dot_split_rhs(m_bf16, x):
    hi, mid, lo = _split3(x)
    return _dot(m_bf16, hi) + _dot(m_bf16, mid) + _dot(m_bf16, lo)


def _dot_x3(a, w):
    a_hi = a.astype(BF16)
    a_lo = (a - a_hi.astype(F32)).astype(BF16)
    w_hi = w.astype(BF16)
    w_lo = (w - w_hi.astype(F32)).astype(BF16)
    n = w.shape[1]
    both = _dot(a_hi, jnp.concatenate([w_hi, w_lo], axis=1))
    return both[:, :n] + both[:, n:] + _dot(a_lo, w_hi)


def _rms(x, g):
    y = x * lax.rsqrt(jnp.mean(x * x, axis=-1, keepdims=True) + NORM_EPS)
    return y * g


def _norm_small_kernel(x_ref, g_ref, ws_ref, hn_ref, small_ref, small_t_ref):
    y = _rms(x_ref[...], g_ref[...])
    hn_ref[...] = y.astype(BF16)
    small = _dot_x3(y, ws_ref[...])
    small_ref[...] = small
    small_t_ref[...] = small.T


def _norm_small(x, g, w_small):
    n = w_small.shape[1]
    ws = jnp.zeros((D_MODEL, LANES), F32).at[:, :n].set(w_small)
    return pl.pallas_call(
        _norm_small_kernel, name="norm_small", grid=(TOKENS // NORM_TM,),
        in_specs=[pl.BlockSpec((NORM_TM, D_MODEL), lambda i: (i, 0)),
                  pl.BlockSpec((1, D_MODEL), lambda i: (0, 0)),
                  pl.BlockSpec((D_MODEL, LANES), lambda i: (0, 0))],
        out_specs=[pl.BlockSpec((NORM_TM, D_MODEL), lambda i: (i, 0)),
                   pl.BlockSpec((NORM_TM, LANES), lambda i: (i, 0)),
                   pl.BlockSpec((LANES, NORM_TM), lambda i: (0, i))],
        out_shape=[jax.ShapeDtypeStruct((TOKENS, D_MODEL), BF16),
                   jax.ShapeDtypeStruct((TOKENS, LANES), F32),
                   jax.ShapeDtypeStruct((LANES, TOKENS), F32)],
        compiler_params=_cparams(("parallel",)),
    )(x, g.reshape(1, D_MODEL), ws)


def _route(x, g, ws, bias):
    y = _rms(x, g)
    logits = _dot_x3(y, ws) + bias
    lane = lax.broadcasted_iota(I32, logits.shape, 1)
    big = jnp.int32(LANES)
    neg = -jnp.inf
    gl = jnp.where(lane < MOE_GROUPS, logits, neg)
    gmax = jnp.max(gl, axis=-1, keepdims=True)
    gsum = jnp.sum(jnp.exp(gl - gmax), axis=-1, keepdims=True)
    g_w = 1.0 / gsum
    g_sel = jnp.min(jnp.where(gl == gmax, lane, big), axis=-1, keepdims=True)
    lo = MOE_GROUPS + g_sel * MOE_EPG
    el = jnp.where((lane >= lo) & (lane < lo + MOE_EPG), logits, neg)
    v1 = jnp.max(el, axis=-1, keepdims=True)
    i1 = jnp.min(jnp.where(el == v1, lane, big), axis=-1, keepdims=True)
    el2 = jnp.where(lane == i1, neg, el)
    v2 = jnp.max(el2, axis=-1, keepdims=True)
    i2 = jnp.min(jnp.where(el2 == v2, lane, big), axis=-1, keepdims=True)
    e2 = jnp.exp(v2 - v1)
    den = 1.0 + e2
    w1 = (1.0 / den) * g_w
    w2 = (e2 / den) * g_w
    eid = jnp.where(lane == 0, i1 - MOE_GROUPS, jnp.where(lane == 1, i2 - MOE_GROUPS, 0))
    cw = jnp.where(lane == 0, w1, jnp.where(lane == 1, w2, 0.0))
    return eid, cw


def _inproj_heads_kernel(a_ref, w_ref, cos_ref, sin_ref, o_ref, wbf_ref):
    j = pl.program_id(0)

    @pl.when(pl.program_id(1) == 0)
    def _():
        wbf_ref[...] = w_ref[...].astype(BF16)

    acc = _dot_nt(a_ref[...], wbf_ref[...])
    q_tiles = NSA_QD // acc.shape[1]
    c = cos_ref[...]
    s = sin_ref[...]

    def head(h):
        return acc[:, h * NSA_HEAD_DIM:(h + 1) * NSA_HEAD_DIM]

    def rotary(xh):
        return (xh * c + pltpu.roll(xh, NSA_HEAD_DIM // 2, 1) * s).astype(BF16)

    for h in range(NSA_KV_HEADS):
        o_ref[h] = rotary(head(h))

    @pl.when(j < q_tiles)
    def _():
        for h in range(NSA_KV_HEADS, 2 * NSA_KV_HEADS):
            o_ref[h] = rotary(head(h))

    @pl.when(j >= q_tiles)
    def _():
        for h in range(NSA_KV_HEADS, 2 * NSA_KV_HEADS):
            o_ref[h] = head(h).astype(BF16)


def _nsa_inproj(hn, w_in_t, cos_full, sin_signed):
    tn = 2 * NSA_KVD
    n_heads_tile = tn // NSA_HEAD_DIM
    s_tiles = SEQ // MM_TM
    return pl.pallas_call(
        _inproj_heads_kernel, name="nsa_inproj", grid=(NSA_MAIN // tn, TOKENS // MM_TM),
        in_specs=[pl.BlockSpec((MM_TM, D_MODEL), lambda j, i: (i, 0)),
                  pl.BlockSpec((tn, D_MODEL), lambda j, i: (j, 0)),
                  pl.BlockSpec((MM_TM, NSA_HEAD_DIM), lambda j, i: (i % s_tiles, 0)),
                  pl.BlockSpec((MM_TM, NSA_HEAD_DIM), lambda j, i: (i % s_tiles, 0))],
        out_specs=pl.BlockSpec((n_heads_tile, MM_TM, NSA_HEAD_DIM), lambda j, i: (j, i, 0)),
        out_shape=jax.ShapeDtypeStruct((NSA_MAIN // NSA_HEAD_DIM, TOKENS, NSA_HEAD_DIM), BF16),
        scratch_shapes=[pltpu.VMEM((tn, D_MODEL), BF16)],
        compiler_params=_cparams(("arbitrary", "arbitrary")),
    )(hn, w_in_t, cos_full, sin_signed)


def _matmul_tiles_kernel(a_ref, w_ref, o_ref, wbf_ref):
    @pl.when(pl.program_id(1) == 0)
    def _():
        wbf_ref[...] = w_ref[...].astype(BF16)

    acc = _dot_nt(a_ref[...], wbf_ref[...])
    for t in range(acc.shape[1] // LANES):
        o_ref[t] = acc[:, t * LANES:(t + 1) * LANES].astype(o_ref.dtype)


def _matmul_tiles(a, w_t, n_cols, tn):
    k = a.shape[1]
    return pl.pallas_call(
        _matmul_tiles_kernel, name="matmul_tiles", grid=(n_cols // tn, TOKENS // MM_TM),
        in_specs=[pl.BlockSpec((MM_TM, k), lambda j, i: (i, 0)),
                  pl.BlockSpec((tn, k), lambda j, i: (j, 0))],
        out_specs=pl.BlockSpec((tn // LANES, MM_TM, LANES), lambda j, i: (j, i, 0)),
        out_shape=jax.ShapeDtypeStruct((n_cols // LANES, TOKENS, LANES), BF16),
        scratch_shapes=[pltpu.VMEM((tn, k), BF16)],
        compiler_params=_cparams(("arbitrary", "arbitrary")),
    )(a, w_t)


def _outproj_route_kernel(a_ref, w_ref, r_ref, g_ref, ws_ref, b_ref, o_ref, eid_ref, cw_ref):
    if len(a_ref.shape) == 3:
        a = jnp.concatenate([a_ref[t] for t in range(a_ref.shape[0])], axis=1)
    else:
        a = a_ref[...]
    x_new = r_ref[...] + _dot(a, w_ref[...])
    o_ref[...] = x_new
    eid_ref[...], cw_ref[...] = _route(x_new, g_ref[...], ws_ref[...], b_ref[...])


def _outproj_route(a, w, resid, ln_ffn_g, w_group, b_group, w_expert, b_expert):
    tiled = a.ndim == 3
    k, n = w.shape
    n_r = MOE_GROUPS + MOE_EXPERTS
    ws = jnp.zeros((D_MODEL, LANES), F32).at[:, :n_r].set(jnp.concatenate([w_group, w_expert], axis=1))
    bs = jnp.zeros((1, LANES), F32).at[0, :n_r].set(jnp.concatenate([b_group, b_expert]))
    a_spec = (pl.BlockSpec((k // LANES, OUT_TM, LANES), lambda i: (0, i, 0)) if tiled
              else pl.BlockSpec((OUT_TM, k), lambda i: (i, 0)))
    small = pl.BlockSpec((OUT_TM, LANES), lambda i: (i, 0))
    return pl.pallas_call(
        _outproj_route_kernel, name="outproj_route", grid=(TOKENS // OUT_TM,),
        in_specs=[a_spec,
                  pl.BlockSpec((k, n), lambda i: (0, 0)),
                  pl.BlockSpec((OUT_TM, n), lambda i: (i, 0)),
                  pl.BlockSpec((1, D_MODEL), lambda i: (0, 0)),
                  pl.BlockSpec((D_MODEL, LANES), lambda i: (0, 0)),
                  pl.BlockSpec((1, LANES), lambda i: (0, 0))],
        out_specs=[pl.BlockSpec((OUT_TM, n), lambda i: (i, 0)), small, small],
        out_shape=[jax.ShapeDtypeStruct((TOKENS, n), F32),
                   jax.ShapeDtypeStruct((TOKENS, LANES), I32),
                   jax.ShapeDtypeStruct((TOKENS, LANES), F32)],
        compiler_params=_cparams(("parallel",)),
    )(a, w.astype(BF16), resid, ln_ffn_g.reshape(1, D_MODEL), ws, bs)


def _compress_kernel(x_ref, pe_ref, w1_ref, w2_ref, o_ref, tok_scr):
    half = CMP_STRIDE * NSA_HEAD_DIM
    n_chunk = SEQ // CMP_STRIDE
    tok_scr[...] = x_ref[0].astype(F32)
    x = jnp.concatenate([tok_scr[pl.ds(p, n_chunk, stride=CMP_STRIDE), :] for p in range(CMP_STRIDE)],
                        axis=1).astype(BF16)
    w1 = w1_ref[0].astype(BF16)
    top = _dot(x, w1[:half])
    bot = _dot(x, w1[half:])
    pe = jnp.broadcast_to(pe_ref[0], (8, 2 * half)).astype(BF16)
    pe_bias = _dot(pe, w1)[0:1]
    hid = top + pltpu.roll(bot, bot.shape[0] - 1, 0) + pe_bias
    act = jax.nn.gelu(hid)
    o_ref[0, 0, 0] = _dot(act.astype(BF16), w2_ref[0].astype(BF16))


def _compress(heads, first_head, pe, w1, w2):
    n_chunk = SEQ // CMP_STRIDE
    feat = CMP_STRIDE * NSA_HEAD_DIM
    return pl.pallas_call(
        _compress_kernel, name="nsa_compress", grid=(2, NSA_KV_HEADS, BATCH),
        in_specs=[pl.BlockSpec((1, SEQ, NSA_HEAD_DIM), lambda a, g, b: (first_head + a * NSA_KV_HEADS + g, b, 0)),
                  pl.BlockSpec((1, 1, 2 * feat), lambda a, g, b: (a, 0, 0)),
                  pl.BlockSpec((1, 2 * feat, CMP_HIDDEN), lambda a, g, b: (a, 0, 0)),
                  pl.BlockSpec((1, CMP_HIDDEN, NSA_HEAD_DIM), lambda a, g, b: (a, 0, 0))],
        out_specs=pl.BlockSpec((1, 1, 1, n_chunk, NSA_HEAD_DIM), lambda a, g, b: (a, g, b, 0, 0)),
        out_shape=jax.ShapeDtypeStruct((2, NSA_KV_HEADS, BATCH, n_chunk, NSA_HEAD_DIM), F32),
        scratch_shapes=[pltpu.VMEM((SEQ, NSA_HEAD_DIM), F32)],
        compiler_params=_cparams(("parallel", "parallel", "parallel")),
    )(heads, pe.reshape(2, 1, 2 * feat), w1, w2)


def _nsa_attn_kernel(q_ref, ks_ref, vs_ref, kw_ref, vw_ref, kc_ref, vc_ref, gate_ref, o_ref,
                     vst_scr, vwt_scr, sel_scr, acc_scr, sc_scr):
    qi = pl.program_id(2)
    tq = ATT_TQ
    tk = ATT_TK
    dh = NSA_HEAD_DIM
    r_heads = NSA_Q_PER_KV
    n_kt = SEQ // tk
    scale = dh ** -0.5 * LOG2_E

    @pl.when(qi == 0)
    def _():
        ones = jnp.ones((ATT_ONES, tk), BF16)
        for kt in range(n_kt):
            rows = slice(kt * tk, (kt + 1) * tk)
            vst_scr[kt, 0:dh, :] = vs_ref[0, rows, :].astype(F32).T.astype(BF16)
            vwt_scr[kt, 0:dh, :] = vw_ref[0, rows, :].astype(F32).T.astype(BF16)
            vst_scr[kt, dh:dh + ATT_ONES, :] = ones
            vwt_scr[kt, dh:dh + ATT_ONES, :] = ones

    q_s = jnp.concatenate([q_ref[r].astype(F32).T for r in range(r_heads)], axis=1) * scale
    q_hi = q_s.astype(BF16)
    q_t = jnp.concatenate([q_hi, (q_s - q_hi.astype(F32)).astype(BF16)], axis=0)

    def qk(k):
        return _dot(jnp.concatenate([k, k], axis=1), q_t)
    n_cp = SEQ // CMP_STRIDE
    sub = lax.broadcasted_iota(I32, (n_cp, tq), 0)
    t_pos = qi * tq + lax.broadcasted_iota(I32, (n_cp, tq), 1)

    def tile4(a):
        return jnp.concatenate([a] * r_heads, axis=1)

    kc = kc_ref[0, 0, 0].astype(BF16)
    vc = vc_ref[0, 0, 0].astype(BF16)
    ok_c = jnp.where(sub * CMP_STRIDE + CMP_BLOCK - 1 <= t_pos, jnp.where(sub < N_CMP, 1.0, 0.0), 0.0)
    ok_c4 = tile4(ok_c)
    s_c = qk(kc) + (ok_c4 - 1.0) * (-NEG_INF)
    e_c = jnp.exp2(s_c - jnp.max(s_c, axis=0, keepdims=True))
    p_c = (e_c / jnp.sum(e_c, axis=0, keepdims=True)) * ok_c4
    o_cmp = _dot_tn(vc, p_c.astype(BF16))
    p_sum = p_c[:, 0:tq]
    for r in range(1, r_heads):
        p_sum = p_sum + p_c[:, r * tq:(r + 1) * tq]

    blk_row = lax.broadcasted_iota(I32, (LANES, LANES), 0)
    cmp_col = lax.broadcasted_iota(I32, (LANES, LANES), 1)
    s_start = blk_row * SLC_BLOCK
    c_start = cmp_col * CMP_STRIDE
    ov_t = jnp.maximum(jnp.minimum(c_start + CMP_BLOCK, s_start + SLC_BLOCK)
                       - jnp.maximum(c_start, s_start), 0).astype(F32) / CMP_BLOCK
    ov_t = jnp.where(blk_row < N_SLC, ov_t, 0.0).astype(BF16)
    imp = _dot_split_rhs(ov_t, p_sum)[0:N_SLC]
    j_blk = lax.broadcasted_iota(I32, (N_SLC, tq), 0)
    dist = (qi * tq + lax.broadcasted_iota(I32, (N_SLC, tq), 1)) // SLC_BLOCK - j_blk
    imp = jnp.where(j_blk == 0, SLC_FORCE, jnp.where(dist < 0, imp, jnp.where(dist < SLC_LOCAL, SLC_FORCE, imp)))
    imp = jnp.where(dist >= 0, imp, -jnp.inf)
    cnt = jnp.zeros((N_SLC, tq), I32)
    for k in range(N_SLC):
        row_k = imp[k:k + 1, :]
        tie = jnp.where(j_blk > k, 1, 0)
        cnt = cnt + jnp.where(row_k > imp, 1, jnp.where(row_k == imp, tie, 0))
    sel = jnp.where(cnt < min(SLC_TOPK, N_SLC), 1.0, 0.0)
    for j in range(N_SLC):
        sel_scr[8 * j:8 * j + 8, :] = jnp.broadcast_to(sel[j:j + 1, :], (8, tq))

    def scores(k, ok):
        return qk(k) + tile4((ok - 1.0) * (-NEG_INF))

    def weighted_values(vt_scr, kt0, pr, n_tiles):
        out = None
        for u in range(n_tiles):
            term = _dot(vt_scr[kt0 + u], pr[u * tk:(u + 1) * tk].astype(BF16))
            out = term if out is None else out + term
        return out

    n_ct = ATT_CHUNK_TILES
    ck = n_ct * tk
    sub_c = lax.broadcasted_iota(I32, (ck, tq), 0)
    t_pos_c = qi * tq + lax.broadcasted_iota(I32, (ck, tq), 1)
    blocks_per_chunk = ck // SLC_BLOCK
    acc_scr[...] = jnp.zeros(acc_scr.shape, F32)

    def chunk_scores(c):
        c = jnp.minimum(c, SEQ // ck - 1)
        start = pl.multiple_of(c * ck, ck)
        k = ks_ref[0, pl.ds(start, ck), :]
        rows8 = sel_scr[pl.ds(pl.multiple_of(c * (8 * blocks_per_chunk), 8 * blocks_per_chunk),
                              8 * blocks_per_chunk), :]
        picked = jnp.concatenate(
            [rows8[8 * u:8 * u + 8] for u in range(blocks_per_chunk) for _ in range(SLC_BLOCK // 8)], axis=0)
        return scores(k, jnp.where(start + sub_c <= t_pos_c, picked, 0.0))

    sc_scr[...] = chunk_scores(0)

    def slc_body(c, m_old):
        sc = sc_scr[...]
        sc_next = chunk_scores(c + 1)
        m_new = jnp.maximum(m_old, jnp.max(sc, axis=0, keepdims=True))
        alpha = jnp.exp2(m_old - m_new)
        pr = jnp.exp2(sc - m_new)
        acc_scr[...] = alpha * acc_scr[...] + weighted_values(vst_scr, c * n_ct, pr, n_ct)
        sc_scr[...] = sc_next
        return m_new

    lax.fori_loop(0, ((qi + 1) * tq + ck - 1) // ck, slc_body, jnp.full((1, r_heads * tq), NEG_INF, F32))
    acc = acc_scr[...]
    o_slc = acc[0:dh] / acc[dh:dh + 1]

    n_wt = (WINDOW + tq) // tk
    kt0 = jnp.maximum(qi * (tq // tk) - WINDOW // tk, 0)
    w_start = pl.multiple_of(kt0 * tk, tk)
    key_w = w_start + lax.broadcasted_iota(I32, (n_wt * tk, tq), 0)
    t_pos_w = qi * tq + lax.broadcasted_iota(I32, (n_wt * tk, tq), 1)
    ok_w = jnp.where(key_w <= t_pos_w, jnp.where(key_w > t_pos_w - WINDOW, 1.0, 0.0), 0.0)
    sc_w = scores(kw_ref[0, pl.ds(w_start, n_wt * tk), :], ok_w)
    pr_w = jnp.exp2(sc_w - jnp.max(sc_w, axis=0, keepdims=True))
    acc_w = weighted_values(vwt_scr, kt0, pr_w, n_wt)
    o_win = acc_w[0:dh] / acc_w[dh:dh + 1]

    gate = jax.nn.sigmoid(gate_ref[0])
    for r in range(r_heads):
        cols = slice(r * tq, (r + 1) * tq)
        o = (gate[3 * r:3 * r + 1] * o_cmp[:, cols] + gate[3 * r + 1:3 * r + 2] * o_slc[:, cols]
             + gate[3 * r + 2:3 * r + 3] * o_win[:, cols])
        o_ref[:, r * dh:(r + 1) * dh] = o.T.astype(BF16)


def _nsa_attention(heads, kc_vc, gates):
    tq = ATT_TQ
    nq = SEQ // tq
    r = NSA_Q_PER_KV
    g_heads = NSA_KV_HEADS
    q_spec = pl.BlockSpec((r, tq, NSA_HEAD_DIM), lambda b, g, i: (g, b * nq + i, 0))

    def kv_spec(first_head):
        return pl.BlockSpec((1, SEQ, NSA_HEAD_DIM), lambda b, g, i: (first_head + g, b, 0))

    first = NSA_HEADS
    specs = [q_spec,
             kv_spec(first + 2 * g_heads), kv_spec(first + 3 * g_heads),
             kv_spec(first + 4 * g_heads), kv_spec(first + 5 * g_heads),
             pl.BlockSpec((1, 1, 1, SEQ // CMP_STRIDE, NSA_HEAD_DIM), lambda b, g, i: (0, g, b, 0, 0)),
             pl.BlockSpec((1, 1, 1, SEQ // CMP_STRIDE, NSA_HEAD_DIM), lambda b, g, i: (1, g, b, 0, 0)),
             pl.BlockSpec((1, 3 * r, tq), lambda b, g, i: (g, 0, b * nq + i))]
    vt_shape = (SEQ // ATT_TK, NSA_HEAD_DIM + ATT_ONES, ATT_TK)
    return pl.pallas_call(
        _nsa_attn_kernel, name="nsa_attn", grid=(BATCH, g_heads, nq),
        in_specs=specs,
        out_specs=pl.BlockSpec((tq, r * NSA_HEAD_DIM), lambda b, g, i: (b * nq + i, g)),
        out_shape=jax.ShapeDtypeStruct((TOKENS, NSA_QD), BF16),
        scratch_shapes=[pltpu.VMEM(vt_shape, BF16), pltpu.VMEM(vt_shape, BF16),
                        pltpu.VMEM((8 * N_SLC, tq), F32),
                        pltpu.VMEM((NSA_HEAD_DIM + ATT_ONES, r * tq), F32),
                        pltpu.VMEM((ATT_CHUNK_TILES * ATT_TK, r * tq), F32)],
        compiler_params=_cparams(("arbitrary", "arbitrary", "arbitrary")),
    )(heads, heads, heads, heads, heads, kc_vc, kc_vc, gates)


def _ssd_chunk_kernel(zx_ref, dtc_ref, dtr_ref, cw_ref, cb_ref, dtb_c_ref, alog_c_ref, dtb_r_ref, alog_r_ref,
                      dskip_ref, ng_ref, echan_ref, o_ref,
                      prev_scr, acum_r_scr, st_scr, fac_scr):
    chunk = pl.program_id(1)
    L = SSD_CHUNK
    W = SSD_GROUP_W
    hpg = SSD_HEADS_PER_GROUP
    n_xt = W // LANES
    x0 = SSD_D_INNER // LANES
    b0 = 2 * SSD_D_INNER // LANES
    c0 = b0 + SSD_GROUPS
    cb0 = SSD_D_INNER // LANES
    cc0 = cb0 + SSD_GROUPS

    @pl.when(chunk == 0)
    def _():
        prev_scr[...] = jnp.zeros(prev_scr.shape, F32)
        st_scr[...] = jnp.zeros(st_scr.shape, F32)

    dt_c = jax.nn.softplus(dtc_ref[...] + dtb_c_ref[...])
    adt_c = dt_c * (-jnp.exp(alog_c_ref[...]))
    dt_r = jax.nn.softplus(dtr_ref[...] + dtb_r_ref[...])
    adt_r = dt_r * (-jnp.exp(alog_r_ref[...]))
    row = lax.broadcasted_iota(I32, (L, L), 0)
    col = lax.broadcasted_iota(I32, (L, L), 1)
    causal = row >= col
    tri = jnp.where(causal, 1.0, 0.0).astype(BF16)
    tri_t = jnp.where(col >= row, 1.0, 0.0).astype(BF16)
    acum_c = _dot_split_rhs(tri, adt_c)
    acum_r_scr[...] = _dot_split_lhs(adt_r, tri_t)
    a_last = acum_c[L - 1:L, :]
    fac = jnp.concatenate([dt_c, jnp.exp(acum_c), jnp.exp(a_last - acum_c),
                           jnp.broadcast_to(jnp.exp(a_last), (SSD_FAC_PAD, LANES))], axis=0)
    fac_hi = fac.astype(BF16)
    fac_scr[0] = fac_hi
    fac_scr[1] = (fac - fac_hi.astype(F32)).astype(BF16)
    lane_w = lax.broadcasted_iota(I32, (L, LANES), 1)
    first_half = lane_w < SSD_HEAD_DIM

    def tiles(ref, first, n):
        return jnp.concatenate([ref[first + q] for q in range(n)], axis=1)

    def group_body(g, carry):
        e_chan = echan_ref[g]
        ex = _dot(fac_scr[0, 0:3 * L], e_chan)
        dt_x, ea_x, sd_x = ex[0:L], ex[L:2 * L], ex[2 * L:3 * L]
        cd_x = (_dot(fac_scr[0, 3 * L:3 * L + SSD_FAC_PAD], e_chan)
                + _dot(fac_scr[1, 3 * L:3 * L + SSD_FAC_PAD], e_chan))[0:1]

        def conv_silu(zx_first, conv_first, n):
            cur = tiles(zx_ref, zx_first, n)
            cur_f = cur.astype(F32)
            tail = tiles(prev_scr, conv_first, n)
            row8 = lax.broadcasted_iota(I32, (8, n * LANES), 0)
            w = tiles(cw_ref, conv_first, n)
            acc = jnp.broadcast_to(tiles(cb_ref, conv_first, n), (L, n * LANES))
            for k in range(SSD_CONV):
                back = SSD_CONV - 1 - k
                if back == 0:
                    xk = cur_f
                else:
                    rolled = pltpu.roll(cur_f, back, 0)
                    head = jnp.where(row8 < back, pltpu.roll(tail, back, 0), rolled[0:8])
                    xk = jnp.concatenate([head, rolled[8:]], axis=0)
                acc = acc + xk * w[k:k + 1, :]
            for q in range(n):
                prev_scr[conv_first + q] = cur_f[L - 8:L, q * LANES:(q + 1) * LANES]
            return jax.nn.silu(acc)

        xs = conv_silu(x0 + n_xt * g, n_xt * g, n_xt)
        bm = conv_silu(b0 + g, cb0 + g, 1)
        cm = conv_silu(c0 + g, cc0 + g, 1)

        xdt = xs * dt_x
        cb = jnp.where(causal, _dot_nt(cm.astype(BF16), bm.astype(BF16)), 0.0)
        y_parts = []
        for pair in range(hpg // 2):
            xd = xdt[:, pair * LANES:(pair + 1) * LANES]
            m_pair = []
            for sub in range(2):
                a_row = jnp.broadcast_to(acum_r_scr[pl.ds(g * hpg + 2 * pair + sub, 1), :], (L, L))
                seg = jnp.minimum(a_row.T - a_row, 0.0)
                m_pair.append((cb * jnp.exp(seg)).astype(BF16))
            x_pair = jnp.concatenate([jnp.where(first_half, xd, 0.0), jnp.where(first_half, 0.0, xd)], axis=0)
            y_parts.append(_dot(jnp.concatenate(m_pair, axis=1), x_pair.astype(BF16)))
        y_diag = jnp.concatenate(y_parts, axis=1)

        st = st_scr[g]
        y_off = _dot(cm.astype(BF16), st.astype(BF16)) * ea_x
        st_scr[g] = st * cd_x + _dot_tn(bm.astype(BF16), (xdt * sd_x).astype(BF16))

        y = y_diag + y_off + xs * tiles(dskip_ref, n_xt * g, n_xt)
        y = y * jax.nn.silu(tiles(zx_ref, n_xt * g, n_xt).astype(F32))
        y = y * lax.rsqrt(jnp.mean(y * y, axis=-1, keepdims=True) + NORM_EPS)
        y = y * tiles(ng_ref, n_xt * g, n_xt)
        for q in range(n_xt):
            o_ref[n_xt * g + q] = y[:, q * LANES:(q + 1) * LANES].astype(BF16)
        return carry

    lax.fori_loop(0, SSD_GROUPS, group_body, 0)


def _ssd_chunks(zx_tiles, dt_small, dt_small_t, conv_w, conv_b, dt_bias, a_log, d_skip, norm_g):
    L = SSD_CHUNK
    nc = SEQ // L
    n_zx = SSD_MAIN // LANES
    n_conv = SSD_CONV_CH // LANES
    n_inner = SSD_D_INNER // LANES
    hpg = SSD_HEADS_PER_GROUP

    def pad_heads(v):
        return jnp.zeros((LANES,), F32).at[:SSD_HEADS].set(v)

    dtb = pad_heads(dt_bias)
    alog = pad_heads(a_log)
    cw = conv_w.reshape(SSD_CONV, n_conv, LANES).transpose(1, 0, 2)
    cb = conv_b.reshape(n_conv, 1, LANES)
    d_chan = jnp.repeat(d_skip, SSD_HEAD_DIM).reshape(n_inner, 1, LANES)
    ng = norm_g.reshape(n_inner, 1, LANES)
    head =jnp.arange(LANES, dtype=I32)[None, :, None]
    grp = jnp.arange(SSD_GROUPS, dtype=I32)[:, None, None]
    e_chan = (head == grp * hpg + jnp.arange(SSD_GROUP_W, dtype=I32)[None, None, :] // SSD_HEAD_DIM).astype(BF16)
    row = lambda b, c: b * nc + c
    const3 = lambda b, c: (0, 0, 0)
    const2 = lambda b, c: (0, 0)
    in_specs = [
        pl.BlockSpec((n_zx, L, LANES), lambda b, c: (0, row(b, c), 0)),
        pl.BlockSpec((L, LANES), lambda b, c: (row(b, c), 0)),
        pl.BlockSpec((LANES, L), lambda b, c: (0, row(b, c))),
        pl.BlockSpec((n_conv, SSD_CONV, LANES), const3),
        pl.BlockSpec((n_conv, 1, LANES), const3),
        pl.BlockSpec((1, LANES), const2), pl.BlockSpec((1, LANES), const2),
        pl.BlockSpec((LANES, 1), const2), pl.BlockSpec((LANES, 1), const2),
        pl.BlockSpec((n_inner, 1, LANES), const3),
        pl.BlockSpec((n_inner, 1, LANES), const3),
        pl.BlockSpec((SSD_GROUPS, LANES, SSD_GROUP_W), const3),
    ]
    return pl.pallas_call(
        _ssd_chunk_kernel, name="ssd_chunks", grid=(BATCH, nc),
        in_specs=in_specs,
        out_specs=pl.BlockSpec((n_inner, L, LANES), lambda b, c: (0, row(b, c), 0)),
        out_shape=jax.ShapeDtypeStruct((n_inner, TOKENS, LANES), BF16),
        scratch_shapes=[pltpu.VMEM((n_conv, 8, LANES), F32),
                        pltpu.VMEM((LANES, L), F32),
                        pltpu.VMEM((SSD_GROUPS, SSD_D_STATE, SSD_GROUP_W), F32),
                        pltpu.VMEM((2, 3 * L + SSD_FAC_PAD, LANES), BF16)],
        compiler_params=_cparams(("arbitrary", "arbitrary")),
    )(zx_tiles, dt_small, dt_small_t, cw, cb, dtb.reshape(1, LANES), alog.reshape(1, LANES),
      dtb.reshape(LANES, 1), alog.reshape(LANES, 1), d_chan, ng, e_chan)


def _gather_rows(src_hbm, idx_ref, base, dst, sem, n_rows):
    def body(b, carry):
        for u in range(GATHER_UNROLL):
            r = b * GATHER_UNROLL + u
            tok = idx_ref[base + r]
            pltpu.make_async_copy(src_hbm.at[pl.ds(tok, 1), :], dst.at[pl.ds(r, 1), :], sem).start()
        return carry

    lax.fori_loop(0, n_rows // GATHER_UNROLL, body, 0)


def _moe_ffn_kernel(te_ref, pos_ref, nact_ref, pad_ref, wp_ref, x_hbm, g_ref, wg_hbm, wu_hbm, wd_hbm, y_ref,
                    tok_ref, buf, sem, wg_f, wu_f, wd_f, wsem, wg_bf, wu_bf, wd_bf, *, layer):
    i = pl.program_id(0)
    n_act = nact_ref[0]
    tm = MOE_TM
    slot = i % 2
    run_start = wp_ref[i] == 1
    w_slot = wp_ref[MOE_TILES + i]
    next_expert = wp_ref[2 * MOE_TILES + i]
    n_rows = pl.multiple_of(wp_ref[3 * MOE_TILES + i], GATHER_UNROLL)

    def gather(tile, s):
        _gather_rows(x_hbm, tok_ref, tile * tm, buf.at[s], sem.at[s], wp_ref[3 * MOE_TILES + tile])

    def weight_copies(e, s):
        return (pltpu.make_async_copy(wg_hbm.at[layer, e], wg_f.at[s], wsem.at[s, 0]),
                pltpu.make_async_copy(wu_hbm.at[layer, e], wu_f.at[s], wsem.at[s, 1]),
                pltpu.make_async_copy(wd_hbm.at[layer, e], wd_f.at[s], wsem.at[s, 2]))

    @pl.when(i == 0)
    def _():
        for c in weight_copies(te_ref[0], 0):
            c.start()
        buf[...] = jnp.zeros(buf.shape, F32)
        def clear(q, carry):
            tok_ref[q] = 0
            return carry

        def clear_padding(e, carry):
            lax.fori_loop(pad_ref[e], pad_ref[MOE_EXPERTS + e], clear, 0)
            return carry

        lax.fori_loop(0, MOE_EXPERTS, clear_padding, 0)
        for k in range(MOE_TOPK):
            def place(t, carry, k=k):
                tok_ref[pos_ref[k * TOKENS + t]] = t
                return carry

            lax.fori_loop(0, TOKENS, place, 0, unroll=GATHER_UNROLL)
        gather(0, 0)

    @pl.when(run_start)
    def _():
        for c in weight_copies(te_ref[i], w_slot):
            c.wait()

        @pl.when(next_expert >= 0)
        def _():
            for c in weight_copies(next_expert, 1 - w_slot):
                c.start()

        wg_bf[...] = wg_f[w_slot].astype(BF16)
        wu_bf[...] = wu_f[w_slot].astype(BF16)
        wd_bf[...] = wd_f[w_slot].astype(BF16)

    @pl.when(i >= n_act)
    def _():
        y_ref[...] = jnp.zeros(y_ref.shape, F32)

    @pl.when(i < n_act)
    def _():
        pltpu.make_async_copy(x_hbm.at[pl.ds(0, n_rows), :], buf.at[slot, pl.ds(0, n_rows), :],
                              sem.at[slot]).wait()

        @pl.when(i + 1 < n_act)
        def _():
            gather(i + 1, 1 - slot)

        h = _rms(buf[slot], g_ref[...]).astype(BF16)
        act = jax.nn.silu(_dot(h, wg_bf[...])) * _dot(h, wu_bf[...])
        y_ref[...] = _dot(act.astype(BF16), wd_bf[...])


def _moe_ffn(x, g, w_gate, w_up, w_down, layer, tile_expert, pos_kmajor, n_active, pad_rows, weight_plan):
    any_spec = pl.BlockSpec(memory_space=pl.ANY)
    grid_spec = pltpu.PrefetchScalarGridSpec(
        num_scalar_prefetch=5, grid=(MOE_TILES,),
        in_specs=[any_spec, pl.BlockSpec((1, D_MODEL), lambda i, *_: (0, 0)), any_spec, any_spec, any_spec],
        out_specs=pl.BlockSpec((MOE_TM, D_MODEL), lambda i, *_: (i, 0)),
        scratch_shapes=[pltpu.SMEM((MOE_ROWS,), I32),
                        pltpu.VMEM((2, MOE_TM, D_MODEL), F32), pltpu.SemaphoreType.DMA((2,)),
                        pltpu.VMEM((2, D_MODEL, MOE_D_FF), F32), pltpu.VMEM((2, D_MODEL, MOE_D_FF), F32),
                        pltpu.VMEM((2, MOE_D_FF, D_MODEL), F32), pltpu.SemaphoreType.DMA((2, 3)),
                        pltpu.VMEM((D_MODEL, MOE_D_FF), BF16), pltpu.VMEM((D_MODEL, MOE_D_FF), BF16),
                        pltpu.VMEM((MOE_D_FF, D_MODEL), BF16)])
    return pl.pallas_call(
        functools.partial(_moe_ffn_kernel, layer=layer), name="moe_ffn", grid_spec=grid_spec,
        out_shape=jax.ShapeDtypeStruct((MOE_ROWS, D_MODEL), F32),
        compiler_params=_cparams(("arbitrary",)),
    )(tile_expert, pos_kmajor, n_active, pad_rows, weight_plan, x, g.reshape(1, D_MODEL), w_gate, w_up, w_down)


def _moe_combine_kernel(pos_ref, x_ref, cw_ref, g_ref, ws_ref, y_hbm, *refs, final_norm):
    if final_norm:
        o_ref, buf, sem = refs
    else:
        o_ref, hn_ref, small_ref, small_t_ref, buf, sem = refs
    i = pl.program_id(0)
    n = pl.num_programs(0)
    tm = CMB_TM
    slot = i % 2

    def issue(tile, s):
        for k in range(MOE_TOPK):
            _gather_rows(y_hbm, pos_ref, (k * (TOKENS // tm) + tile) * tm, buf.at[s, k], sem.at[s], tm)

    @pl.when(i == 0)
    def _():
        issue(0, 0)

    @pl.when(i + 1 < n)
    def _():
        issue(i + 1, 1 - slot)

    for k in range(MOE_TOPK):
        pltpu.make_async_copy(y_hbm.at[pl.ds(0, tm), :], buf.at[slot, k], sem.at[slot]).wait()
    cw = cw_ref[...]
    out = x_ref[...] + cw[:, 0:1] * buf[slot, 0] + cw[:, 1:2] * buf[slot, 1]
    y = _rms(out, g_ref[...])
    if final_norm:
        o_ref[...] = y
    else:
        o_ref[...] = out
        hn_ref[...] = y.astype(BF16)
        small = _dot_x3(y, ws_ref[...])
        small_ref[...] = small
        small_t_ref[...] = small.T


def _moe_combine(x, cw, y_sorted, pos_kmajor, g_norm, w_small, final_norm):
    ws = jnp.zeros((D_MODEL, LANES), F32)
    if not final_norm:
        ws = ws.at[:, :w_small.shape[1]].set(w_small)
    row = pl.BlockSpec((CMB_TM, D_MODEL), lambda i, pos: (i, 0))
    small = pl.BlockSpec((CMB_TM, LANES), lambda i, pos: (i, 0))
    x_shape = jax.ShapeDtypeStruct((TOKENS, D_MODEL), F32)
    if final_norm:
        out_specs, out_shape = row, x_shape
    else:
        out_specs = [row, row, small, pl.BlockSpec((LANES, CMB_TM), lambda i, pos: (0, i))]
        out_shape = [x_shape, jax.ShapeDtypeStruct((TOKENS, D_MODEL), BF16),
                     jax.ShapeDtypeStruct((TOKENS, LANES), F32), jax.ShapeDtypeStruct((LANES, TOKENS), F32)]
    grid_spec = pltpu.PrefetchScalarGridSpec(
        num_scalar_prefetch=1, grid=(TOKENS // CMB_TM,),
        in_specs=[row, small,
                  pl.BlockSpec((1, D_MODEL), lambda i, pos: (0, 0)),
                  pl.BlockSpec((D_MODEL, LANES), lambda i, pos: (0, 0)),
                  pl.BlockSpec(memory_space=pl.ANY)],
        out_specs=out_specs,
        scratch_shapes=[pltpu.VMEM((2, MOE_TOPK, CMB_TM, D_MODEL), F32), pltpu.SemaphoreType.DMA((2,))])
    return pl.pallas_call(
        functools.partial(_moe_combine_kernel, final_norm=final_norm), name="moe_combine", grid_spec=grid_spec,
        out_shape=out_shape,
        compiler_params=_cparams(("arbitrary",)),
    )(pos_kmajor, x, cw, g_norm.reshape(1, D_MODEL), ws, y_sorted)


def _moe_plan(eid):
    e = eid[:, :MOE_TOPK].reshape(-1)
    onehot = (e[:, None] == jnp.arange(MOE_EXPERTS, dtype=I32)[None, :]).astype(I32)
    csum = jnp.cumsum(onehot, axis=0)
    counts = csum[-1]
    padded = ((counts + MOE_TM - 1) // MOE_TM) * MOE_TM
    g_end = jnp.cumsum(padded)
    g_start = g_end - padded
    pos = jnp.sum(onehot * (g_start[None, :] + csum - 1), axis=1)
    n_active = (g_end[-1] // MOE_TM).astype(I32)
    tile_start = jnp.arange(MOE_TILES, dtype=I32) * MOE_TM
    te = jnp.sum((g_end[None, :] <= tile_start[:, None]).astype(I32), axis=1)
    last = jnp.max(jnp.where(counts > 0, jnp.arange(MOE_EXPERTS, dtype=I32), 0))
    tile_expert = jnp.minimum(te, last)
    pos_kmajor = pos.reshape(TOKENS, MOE_TOPK).T.reshape(-1)
    pad_rows = jnp.concatenate([g_start + counts, g_end])
    experts = jnp.arange(MOE_EXPERTS, dtype=I32)
    run_start = jnp.concatenate([jnp.ones((1,), I32), (tile_expert[1:] != tile_expert[:-1]).astype(I32)])
    w_slot = (jnp.cumsum(run_start) - 1) % 2
    later = (experts[None, :] > experts[:, None]) & (counts > 0)[None, :]
    next_nonempty = jnp.min(jnp.where(later, experts[None, :], MOE_EXPERTS), axis=1)
    next_nonempty = jnp.where(next_nonempty == MOE_EXPERTS, -1, next_nonempty)
    real_rows = jnp.clip((g_start + counts)[tile_expert] - tile_start, 0, MOE_TM)
    real_rows = jnp.where(tile_start < g_end[-1], real_rows, 0)
    gather_rows = jnp.minimum(-(-real_rows // GATHER_UNROLL) * GATHER_UNROLL, MOE_TM)
    weight_plan = jnp.concatenate([run_start, w_slot, next_nonempty[tile_expert], gather_rows]).astype(I32)
    return tile_expert, n_active.reshape(1), pos_kmajor, pad_rows, weight_plan


def _hier_moe_add(x, eid, cw, ln_g, w_gate, w_up, w_down, layer, g_norm, w_small, final_norm):
    tile_expert, n_active, pos_kmajor, pad_rows, weight_plan = _moe_plan(eid)
    y_sorted = _moe_ffn(x, ln_g, w_gate, w_up, w_down, layer, tile_expert, pos_kmajor, n_active, pad_rows,
                        weight_plan)
    return _moe_combine(x, cw, y_sorted, pos_kmajor, g_norm, w_small, final_norm)


def _rope_tables():
    pos = jnp.arange(SEQ, dtype=F32)
    inv = 1.0 / (ROPE_THETA ** (jnp.arange(0, NSA_HEAD_DIM, 2, dtype=F32) / NSA_HEAD_DIM))
    ang = pos[:, None] * inv[None, :]
    cos, sin = jnp.cos(ang), jnp.sin(ang)
    return jnp.concatenate([cos, cos], axis=1), jnp.concatenate([-sin, sin], axis=1)


def _nsa_mixer(hn, g_lin_t, w_in, cmp_pe, cmp_w1, cmp_w2):
    w_in_t = w_in.T
    cos_full, sin_signed = _rope_tables()
    heads = _nsa_inproj(hn, w_in_t, cos_full, sin_signed)
    first_c = NSA_HEADS
    kc_vc = _compress(heads, first_c, cmp_pe, cmp_w1, cmp_w2)
    gates_t = g_lin_t[:NSA_GATES].reshape(NSA_KV_HEADS, 3 * NSA_Q_PER_KV, TOKENS)
    return _nsa_attention(heads, kc_vc, gates_t)


def _ssd_mixer(hn, dt_small, dt_small_t, w_in, conv_w, conv_b, dt_bias, a_log, d_skip, norm_g):
    zx_tiles = _matmul_tiles(hn, w_in.T, SSD_MAIN, 1024)
    return _ssd_chunks(zx_tiles, dt_small, dt_small_t, conv_w, conv_b, dt_bias, a_log, d_skip, norm_g)


def kernel(x, ln_mix, ln_ffn, ln_final, nsa_w_in, nsa_cmp_pe, nsa_cmp_w1, nsa_cmp_w2, nsa_w_out,
           ssd_w_in, ssd_conv_w, ssd_conv_b, ssd_dt_bias, ssd_a_log, ssd_d, ssd_norm, ssd_w_out,
           moe_w_group, moe_b_group, moe_w_expert, moe_b_expert, moe_w_gate, moe_w_up, moe_w_down):
    def small_weight(i):
        if i % N_MIXERS == 0:
            return nsa_w_in[i // N_MIXERS].T[NSA_MAIN:].T
        return ssd_w_in[i // N_MIXERS].T[SSD_MAIN:].T

    h = x.reshape(TOKENS, D_MODEL)
    hn, small, small_t = _norm_small(h, ln_mix[0], small_weight(0))
    for i in range(DEPTH):
        j = i // N_MIXERS
        last = i == DEPTH - 1
        if i % N_MIXERS == 0:
            mix = _nsa_mixer(hn, small_t, nsa_w_in[j], nsa_cmp_pe[j], nsa_cmp_w1[j], nsa_cmp_w2[j])
            w_out = nsa_w_out[j]
        else:
            mix = _ssd_mixer(hn, small, small_t, ssd_w_in[j], ssd_conv_w[j], ssd_conv_b[j], ssd_dt_bias[j],
                             ssd_a_log[j], ssd_d[j], ssd_norm[j])
            w_out = ssd_w_out[j]
        h, eid, cw = _outproj_route(mix, w_out, h, ln_ffn[i], moe_w_group[i], moe_b_group[i],
                                    moe_w_expert[i], moe_b_expert[i])
        out = _hier_moe_add(h, eid, cw, ln_ffn[i], moe_w_gate, moe_w_up, moe_w_down, i,
                            ln_final if last else ln_mix[i + 1], None if last else small_weight(i + 1), last)
        if last:
            h = out
        else:
            h, hn, small, small_t = out
    return h.reshape(BATCH, SEQ, D_MODEL)
```

```python
import functools

import jax
import jax.numpy as jnp
from jax import lax
from jax.experimental import pallas as pl
from jax.experimental.pallas import tpu as pltpu

F32 = jnp.float32
BF16 = jnp.bfloat16
I32 = jnp.int32

D_MODEL = 2048
BATCH = 4
SEQ = 2048
TOKENS = BATCH * SEQ
DEPTH = 2
N_MIXERS = 2
NORM_EPS = 1e-6
NEG_INF = -1e30
LOG2_E = 1.4426950408889634
ROPE_THETA = 10000.0

NSA_HEADS = 16
NSA_KV_HEADS = 4
NSA_HEAD_DIM = D_MODEL // NSA_HEADS
NSA_Q_PER_KV = NSA_HEADS // NSA_KV_HEADS
CMP_BLOCK = 32
CMP_STRIDE = 16
CMP_HIDDEN = 256
N_CMP = (SEQ - CMP_BLOCK) // CMP_STRIDE + 1
SLC_BLOCK = 64
SLC_TOPK = 16
SLC_LOCAL = 2
SLC_FORCE = 1e4
N_SLC = SEQ // SLC_BLOCK
WINDOW = 512
NSA_QD = NSA_HEADS * NSA_HEAD_DIM
NSA_KVD = NSA_KV_HEADS * NSA_HEAD_DIM
NSA_MAIN = NSA_QD + 6 * NSA_KVD
NSA_GATES = 3 * NSA_HEADS

SSD_D_INNER = 2 * D_MODEL
SSD_HEAD_DIM = 64
SSD_HEADS = SSD_D_INNER // SSD_HEAD_DIM
SSD_GROUPS = 8
SSD_HEADS_PER_GROUP = SSD_HEADS // SSD_GROUPS
SSD_D_STATE = 128
SSD_CONV = 4
SSD_CHUNK = 128
SSD_GROUP_W = SSD_D_INNER // SSD_GROUPS
SSD_BC = SSD_GROUPS * SSD_D_STATE
SSD_CONV_CH = SSD_D_INNER + 2 * SSD_BC
SSD_MAIN = SSD_D_INNER + SSD_CONV_CH
SSD_FAC_PAD = 16

MOE_GROUPS = 4
MOE_EPG = 8
MOE_EXPERTS = MOE_GROUPS * MOE_EPG
MOE_TOPK = 2
MOE_D_FF = 512

LANES = 128
VMEM_LIMIT = 56 * 1024 * 1024

NORM_TM = 512
MM_TM = 1024
OUT_TM = 256
ATT_TQ = 256
ATT_TK = 256
ATT_CHUNK_TILES = 1
ATT_ONES = 16
MOE_TM = 256
MOE_TILES = (TOKENS * MOE_TOPK) // MOE_TM + MOE_EXPERTS
MOE_ROWS = MOE_TILES * MOE_TM
CMB_TM = 256
GATHER_UNROLL = 8


def _cparams(sem):
    return pltpu.CompilerParams(dimension_semantics=sem, vmem_limit_bytes=VMEM_LIMIT)


def _split3(x):
    hi = x.astype(BF16)
    r1 = x - hi.astype(F32)
    mid = r1.astype(BF16)
    lo = (r1 - mid.astype(F32)).astype(BF16)
    return hi, mid, lo


def _dot(a, b):
    return jnp.dot(a, b, preferred_element_type=F32)


def _dot_nt(a, b):
    return lax.dot_general(a, b, (((1,), (1,)), ((), ())), preferred_element_type=F32)


def _dot_tn(a, b):
    return lax.dot_general(a, b, (((0,), (0,)), ((), ())), preferred_element_type=F32)


def _dot_split_lhs(x, m_bf16):
    hi, mid, lo = _split3(x)
    return _dot(hi, m_bf16) + _dot(mid, m_bf16) + _dot(lo, m_bf16)


def _dot_split_rhs(m_bf16, x):
    hi, mid, lo = _split3(x)
    return _dot(m_bf16, hi) + _dot(m_bf16, mid) + _dot(m_bf16, lo)


def _dot_x3(a, w):
    a_hi = a.astype(BF16)
    a_lo = (a - a_hi.astype(F32)).astype(BF16)
    w_hi = w.astype(BF16)
    w_lo = (w - w_hi.astype(F32)).astype(BF16)
    n = w.shape[1]
    both = _dot(a_hi, jnp.concatenate([w_hi, w_lo], axis=1))
    return both[:, :n] + both[:, n:] + _dot(a_lo, w_hi)


def _rms(x, g):
    y = x * lax.rsqrt(jnp.mean(x * x, axis=-1, keepdims=True) + NORM_EPS)
    return y * g


def _norm_small_kernel(x_ref, g_ref, ws_ref, hn_ref, small_ref, small_t_ref):
    y = _rms(x_ref[...], g_ref[...])
    hn_ref[...] = y.astype(BF16)
    small = _dot_x3(y, ws_ref[...])
    small_ref[...] = small
    small_t_ref[...] = small.T


def _norm_small(x, g, w_small):
    n = w_small.shape[1]
    ws = jnp.zeros((D_MODEL, LANES), F32).at[:, :n].set(w_small)
    return pl.pallas_call(
        _norm_small_kernel, name="norm_small", grid=(TOKENS // NORM_TM,),
        in_specs=[pl.BlockSpec((NORM_TM, D_MODEL), lambda i: (i, 0)),
                  pl.BlockSpec((1, D_MODEL), lambda i: (0, 0)),
                  pl.BlockSpec((D_MODEL, LANES), lambda i: (0, 0))],
        out_specs=[pl.BlockSpec((NORM_TM, D_MODEL), lambda i: (i, 0)),
                   pl.BlockSpec((NORM_TM, LANES), lambda i: (i, 0)),
                   pl.BlockSpec((LANES, NORM_TM), lambda i: (0, i))],
        out_shape=[jax.ShapeDtypeStruct((TOKENS, D_MODEL), BF16),
                   jax.ShapeDtypeStruct((TOKENS, LANES), F32),
                   jax.ShapeDtypeStruct((LANES, TOKENS), F32)],
        compiler_params=_cparams(("parallel",)),
    )(x, g.reshape(1, D_MODEL), ws)


def _route(x, g, ws, bias):
    y = _rms(x, g)
    logits = _dot_x3(y, ws) + bias
    lane = lax.broadcasted_iota(I32, logits.shape, 1)
    big = jnp.int32(LANES)
    neg = -jnp.inf
    gl = jnp.where(lane < MOE_GROUPS, logits, neg)
    gmax = jnp.max(gl, axis=-1, keepdims=True)
    gsum = jnp.sum(jnp.exp(gl - gmax), axis=-1, keepdims=True)
    g_w = 1.0 / gsum
    g_sel = jnp.min(jnp.where(gl == gmax, lane, big), axis=-1, keepdims=True)
    lo = MOE_GROUPS + g_sel * MOE_EPG
    el = jnp.where((lane >= lo) & (lane < lo + MOE_EPG), logits, neg)
    v1 = jnp.max(el, axis=-1, keepdims=True)
    i1 = jnp.min(jnp.where(el == v1, lane, big), axis=-1, keepdims=True)
    el2 = jnp.where(lane == i1, neg, el)
    v2 = jnp.max(el2, axis=-1, keepdims=True)
    i2 = jnp.min(jnp.where(el2 == v2, lane, big), axis=-1, keepdims=True)
    e2 = jnp.exp(v2 - v1)
    den = 1.0 + e2
    w1 = (1.0 / den) * g_w
    w2 = (e2 / den) * g_w
    eid = jnp.where(lane == 0, i1 - MOE_GROUPS, jnp.where(lane == 1, i2 - MOE_GROUPS, 0))
    cw = jnp.where(lane == 0, w1, jnp.where(lane == 1, w2, 0.0))
    return eid, cw


def _inproj_heads_kernel(a_ref, w_ref, cos_ref, sin_ref, o_ref, wbf_ref):
    j = pl.program_id(0)

    @pl.when(pl.program_id(1) == 0)
    def _():
        wbf_ref[...] = w_ref[...].astype(BF16)

    acc = _dot_nt(a_ref[...], wbf_ref[...])
    q_tiles = NSA_QD // acc.shape[1]
    c = cos_ref[...]
    s = sin_ref[...]

    def head(h):
        return acc[:, h * NSA_HEAD_DIM:(h + 1) * NSA_HEAD_DIM]

    def rotary(xh):
        return (xh * c + pltpu.roll(xh, NSA_HEAD_DIM // 2, 1) * s).astype(BF16)

    for h in range(NSA_KV_HEADS):
        o_ref[h] = rotary(head(h))

    @pl.when(j < q_tiles)
    def _():
        for h in range(NSA_KV_HEADS, 2 * NSA_KV_HEADS):
            o_ref[h] = rotary(head(h))

    @pl.when(j >= q_tiles)
    def _():
        for h in range(NSA_KV_HEADS, 2 * NSA_KV_HEADS):
            o_ref[h] = head(h).astype(BF16)


def _nsa_inproj(hn, w_in_t, cos_full, sin_signed):
    tn = 2 * NSA_KVD
    n_heads_tile = tn // NSA_HEAD_DIM
    s_tiles = SEQ // MM_TM
    return pl.pallas_call(
        _inproj_heads_kernel, name="nsa_inproj", grid=(NSA_MAIN // tn, TOKENS // MM_TM),
        in_specs=[pl.BlockSpec((MM_TM, D_MODEL), lambda j, i: (i, 0)),
                  pl.BlockSpec((tn, D_MODEL), lambda j, i: (j, 0)),
                  pl.BlockSpec((MM_TM, NSA_HEAD_DIM), lambda j, i: (i % s_tiles, 0)),
                  pl.BlockSpec((MM_TM, NSA_HEAD_DIM), lambda j, i: (i % s_tiles, 0))],
        out_specs=pl.BlockSpec((n_heads_tile, MM_TM, NSA_HEAD_DIM), lambda j, i: (j, i, 0)),
        out_shape=jax.ShapeDtypeStruct((NSA_MAIN // NSA_HEAD_DIM, TOKENS, NSA_HEAD_DIM), BF16),
        scratch_shapes=[pltpu.VMEM((tn, D_MODEL), BF16)],
        compiler_params=_cparams(("arbitrary", "arbitrary")),
    )(hn, w_in_t, cos_full, sin_signed)


def _matmul_tiles_kernel(a_ref, w_ref, o_ref, wbf_ref):
    @pl.when(pl.program_id(1) == 0)
    def _():
        wbf_ref[...] = w_ref[...].astype(BF16)

    acc = _dot_nt(a_ref[...], wbf_ref[...])
    for t in range(acc.shape[1] // LANES):
        o_ref[t] = acc[:, t * LANES:(t + 1) * LANES].astype(o_ref.dtype)


def _matmul_tiles(a, w_t, n_cols, tn):
    k = a.shape[1]
    return pl.pallas_call(
        _matmul_tiles_kernel, name="matmul_tiles", grid=(n_cols // tn, TOKENS // MM_TM),
        in_specs=[pl.BlockSpec((MM_TM, k), lambda j, i: (i, 0)),
                  pl.BlockSpec((tn, k), lambda j, i: (j, 0))],
        out_specs=pl.BlockSpec((tn // LANES, MM_TM, LANES), lambda j, i: (j, i, 0)),
        out_shape=jax.ShapeDtypeStruct((n_cols // LANES, TOKENS, LANES), BF16),
        scratch_shapes=[pltpu.VMEM((tn, k), BF16)],
        compiler_params=_cparams(("arbitrary", "arbitrary")),
    )(a, w_t)


def _outproj_route_kernel(a_ref, w_ref, r_ref, g_ref, ws_ref, b_ref, o_ref, eid_ref, cw_ref):
    if len(a_ref.shape) == 3:
        a = jnp.concatenate([a_ref[t] for t in range(a_ref.shape[0])], axis=1)
    else:
        a = a_ref[...]
    x_new = r_ref[...] + _dot(a, w_ref[...])
    o_ref[...] = x_new
    eid_ref[...], cw_ref[...] = _route(x_new, g_ref[...], ws_ref[...], b_ref[...])


def _outproj_route(a, w, resid, ln_ffn_g, w_group, b_group, w_expert, b_expert):
    tiled = a.ndim == 3
    k, n = w.shape
    n_r = MOE_GROUPS + MOE_EXPERTS
    ws = jnp.zeros((D_MODEL, LANES), F32).at[:, :n_r].set(jnp.concatenate([w_group, w_expert], axis=1))
    bs = jnp.zeros((1, LANES), F32).at[0, :n_r].set(jnp.concatenate([b_group, b_expert]))
    a_spec = (pl.BlockSpec((k // LANES, OUT_TM, LANES), lambda i: (0, i, 0)) if tiled
              else pl.BlockSpec((OUT_TM, k), lambda i: (i, 0)))
    small = pl.BlockSpec((OUT_TM, LANES), lambda i: (i, 0))
    return pl.pallas_call(
        _outproj_route_kernel, name="outproj_route", grid=(TOKENS // OUT_TM,),
        in_specs=[a_spec,
                  pl.BlockSpec((k, n), lambda i: (0, 0)),
                  pl.BlockSpec((OUT_TM, n), lambda i: (i, 0)),
                  pl.BlockSpec((1, D_MODEL), lambda i: (0, 0)),
                  pl.BlockSpec((D_MODEL, LANES), lambda i: (0, 0)),
                  pl.BlockSpec((1, LANES), lambda i: (0, 0))],
        out_specs=[pl.BlockSpec((OUT_TM, n), lambda i: (i, 0)), small, small],
        out_shape=[jax.ShapeDtypeStruct((TOKENS, n), F32),
                   jax.ShapeDtypeStruct((TOKENS, LANES), I32),
                   jax.ShapeDtypeStruct((TOKENS, LANES), F32)],
        compiler_params=_cparams(("parallel",)),
    )(a, w.astype(BF16), resid, ln_ffn_g.reshape(1, D_MODEL), ws, bs)


def _compress_kernel(x_ref, pe_ref, w1_ref, w2_ref, o_ref, tok_scr):
    half = CMP_STRIDE * NSA_HEAD_DIM
    n_chunk = SEQ // CMP_STRIDE
    tok_scr[...] = x_ref[0].astype(F32)
    x = jnp.concatenate([tok_scr[pl.ds(p, n_chunk, stride=CMP_STRIDE), :] for p in range(CMP_STRIDE)],
                        axis=1).astype(BF16)
    w1 = w1_ref[0].astype(BF16)
    top = _dot(x, w1[:half])
    bot = _dot(x, w1[half:])
    pe = jnp.broadcast_to(pe_ref[0], (8, 2 * half)).astype(BF16)
    pe_bias = _dot(pe, w1)[0:1]
    hid = top + pltpu.roll(bot, bot.shape[0] - 1, 0) + pe_bias
    act = jax.nn.gelu(hid)
    o_ref[0, 0, 0] = _dot(act.astype(BF16), w2_ref[0].astype(BF16))


def _compress(heads, first_head, pe, w1, w2):
    n_chunk = SEQ // CMP_STRIDE
    feat = CMP_STRIDE * NSA_HEAD_DIM
    return pl.pallas_call(
        _compress_kernel, name="nsa_compress", grid=(2, NSA_KV_HEADS, BATCH),
        in_specs=[pl.BlockSpec((1, SEQ, NSA_HEAD_DIM), lambda a, g, b: (first_head + a * NSA_KV_HEADS + g, b, 0)),
                  pl.BlockSpec((1, 1, 2 * feat), lambda a, g, b: (a, 0, 0)),
                  pl.BlockSpec((1, 2 * feat, CMP_HIDDEN), lambda a, g, b: (a, 0, 0)),
                  pl.BlockSpec((1, CMP_HIDDEN, NSA_HEAD_DIM), lambda a, g, b: (a, 0, 0))],
        out_specs=pl.BlockSpec((1, 1, 1, n_chunk, NSA_HEAD_DIM), lambda a, g, b: (a, g, b, 0, 0)),
        out_shape=jax.ShapeDtypeStruct((2, NSA_KV_HEADS, BATCH, n_chunk, NSA_HEAD_DIM), F32),
        scratch_shapes=[pltpu.VMEM((SEQ, NSA_HEAD_DIM), F32)],
        compiler_params=_cparams(("parallel", "parallel", "parallel")),
    )(heads, pe.reshape(2, 1, 2 * feat), w1, w2)


def _nsa_attn_kernel(q_ref, ks_ref, vs_ref, kw_ref, vw_ref, kc_ref, vc_ref, gate_ref, o_ref,
                     vst_scr, vwt_scr, sel_scr, acc_scr, sc_scr):
    qi = pl.program_id(2)
    tq = ATT_TQ
    tk = ATT_TK
    dh = NSA_HEAD_DIM
    r_heads = NSA_Q_PER_KV
    n_kt = SEQ // tk
    scale = dh ** -0.5 * LOG2_E

    @pl.when(qi == 0)
    def _():
        ones = jnp.ones((ATT_ONES, tk), BF16)
        for kt in range(n_kt):
            rows = slice(kt * tk, (kt + 1) * tk)
            vst_scr[kt, 0:dh, :] = vs_ref[0, rows, :].astype(F32).T.astype(BF16)
            vwt_scr[kt, 0:dh, :] = vw_ref[0, rows, :].astype(F32).T.astype(BF16)
            vst_scr[kt, dh:dh + ATT_ONES, :] = ones
            vwt_scr[kt, dh:dh + ATT_ONES, :] = ones

    q_s = jnp.concatenate([q_ref[r].astype(F32).T for r in range(r_heads)], axis=1) * scale
    q_hi = q_s.astype(BF16)
    q_t = jnp.concatenate([q_hi, (q_s - q_hi.astype(F32)).astype(BF16)], axis=0)

    def qk(k):
        return _dot(jnp.concatenate([k, k], axis=1), q_t)
    n_cp = SEQ // CMP_STRIDE
    sub = lax.broadcasted_iota(I32, (n_cp, tq), 0)
    t_pos = qi * tq + lax.broadcasted_iota(I32, (n_cp, tq), 1)

    def tile4(a):
        return jnp.concatenate([a] * r_heads, axis=1)

    kc = kc_ref[0, 0, 0].astype(BF16)
    vc = vc_ref[0, 0, 0].astype(BF16)
    ok_c = jnp.where(sub * CMP_STRIDE + CMP_BLOCK - 1 <= t_pos, jnp.where(sub < N_CMP, 1.0, 0.0), 0.0)
    ok_c4 = tile4(ok_c)
    s_c = qk(kc) + (ok_c4 - 1.0) * (-NEG_INF)
    e_c = jnp.exp2(s_c - jnp.max(s_c, axis=0, keepdims=True))
    p_c = (e_c / jnp.sum(e_c, axis=0, keepdims=True)) * ok_c4
    o_cmp = _dot_tn(vc, p_c.astype(BF16))
    p_sum = p_c[:, 0:tq]
    for r in range(1, r_heads):
        p_sum = p_sum + p_c[:, r * tq:(r + 1) * tq]

    blk_row = lax.broadcasted_iota(I32, (LANES, LANES), 0)
    cmp_col = lax.broadcasted_iota(I32, (LANES, LANES), 1)
    s_start = blk_row * SLC_BLOCK
    c_start = cmp_col * CMP_STRIDE
    ov_t = jnp.maximum(jnp.minimum(c_start + CMP_BLOCK, s_start + SLC_BLOCK)
                       - jnp.maximum(c_start, s_start), 0).astype(F32) / CMP_BLOCK
    ov_t = jnp.where(blk_row < N_SLC, ov_t, 0.0).astype(BF16)
    imp = _dot_split_rhs(ov_t, p_sum)[0:N_SLC]
    j_blk = lax.broadcasted_iota(I32, (N_SLC, tq), 0)
    dist = (qi * tq + lax.broadcasted_iota(I32, (N_SLC, tq), 1)) // SLC_BLOCK - j_blk
    imp = jnp.where(j_blk == 0, SLC_FORCE, jnp.where(dist < 0, imp, jnp.where(dist < SLC_LOCAL, SLC_FORCE, imp)))
    imp = jnp.where(dist >= 0, imp, -jnp.inf)
    cnt = jnp.zeros((N_SLC, tq), I32)
    for k in range(N_SLC):
        row_k = imp[k:k + 1, :]
        tie = jnp.where(j_blk > k, 1, 0)
        cnt = cnt + jnp.where(row_k > imp, 1, jnp.where(row_k == imp, tie, 0))
    sel = jnp.where(cnt < min(SLC_TOPK, N_SLC), 1.0, 0.0)
    for j in range(N_SLC):
        sel_scr[8 * j:8 * j + 8, :] = jnp.broadcast_to(sel[j:j + 1, :], (8, tq))

    def scores(k, ok):
        return qk(k) + tile4((ok - 1.0) * (-NEG_INF))

    def weighted_values(vt_scr, kt0, pr, n_tiles):
        out = None
        for u in range(n_tiles):
            term = _dot(vt_scr[kt0 + u], pr[u * tk:(u + 1) * tk].astype(BF16))
            out = term if out is None else out + term
        return out

    n_ct = ATT_CHUNK_TILES
    ck = n_ct * tk
    sub_c = lax.broadcasted_iota(I32, (ck, tq), 0)
    t_pos_c = qi * tq + lax.broadcasted_iota(I32, (ck, tq), 1)
    blocks_per_chunk = ck // SLC_BLOCK
    acc_scr[...] = jnp.zeros(acc_scr.shape, F32)

    def chunk_scores(c):
        c = jnp.minimum(c, SEQ // ck - 1)
        start = pl.multiple_of(c * ck, ck)
        k = ks_ref[0, pl.ds(start, ck), :]
        rows8 = sel_scr[pl.ds(pl.multiple_of(c * (8 * blocks_per_chunk), 8 * blocks_per_chunk),
                              8 * blocks_per_chunk), :]
        picked = jnp.concatenate(
            [rows8[8 * u:8 * u + 8] for u in range(blocks_per_chunk) for _ in range(SLC_BLOCK // 8)], axis=0)
        return scores(k, jnp.where(start + sub_c <= t_pos_c, picked, 0.0))

    sc_scr[...] = chunk_scores(0)

    def slc_body(c, m_old):
        sc = sc_scr[...]
        sc_next = chunk_scores(c + 1)
        m_new = jnp.maximum(m_old, jnp.max(sc, axis=0, keepdims=True))
        alpha = jnp.exp2(m_old - m_new)
        pr = jnp.exp2(sc - m_new)
        acc_scr[...] = alpha * acc_scr[...] + weighted_values(vst_scr, c * n_ct, pr, n_ct)
        sc_scr[...] = sc_next
        return m_new

    lax.fori_loop(0, ((qi + 1) * tq + ck - 1) // ck, slc_body, jnp.full((1, r_heads * tq), NEG_INF, F32))
    acc = acc_scr[...]
    o_slc = acc[0:dh] / acc[dh:dh + 1]

    n_wt = (WINDOW + tq) // tk
    kt0 = jnp.maximum(qi * (tq // tk) - WINDOW // tk, 0)
    w_start = pl.multiple_of(kt0 * tk, tk)
    key_w = w_start + lax.broadcasted_iota(I32, (n_wt * tk, tq), 0)
    t_pos_w = qi * tq + lax.broadcasted_iota(I32, (n_wt * tk, tq), 1)
    ok_w = jnp.where(key_w <= t_pos_w, jnp.where(key_w > t_pos_w - WINDOW, 1.0, 0.0), 0.0)
    sc_w = scores(kw_ref[0, pl.ds(w_start, n_wt * tk), :], ok_w)
    pr_w = jnp.exp2(sc_w - jnp.max(sc_w, axis=0, keepdims=True))
    acc_w = weighted_values(vwt_scr, kt0, pr_w, n_wt)
    o_win = acc_w[0:dh] / acc_w[dh:dh + 1]

    gate = jax.nn.sigmoid(gate_ref[0])
    for r in range(r_heads):
        cols = slice(r * tq, (r + 1) * tq)
        o = (gate[3 * r:3 * r + 1] * o_cmp[:, cols] + gate[3 * r + 1:3 * r + 2] * o_slc[:, cols]
             + gate[3 * r + 2:3 * r + 3] * o_win[:, cols])
        o_ref[:, r * dh:(r + 1) * dh] = o.T.astype(BF16)


def _nsa_attention(heads, kc_vc, gates):
    tq = ATT_TQ
    nq = SEQ // tq
    r = NSA_Q_PER_KV
    g_heads = NSA_KV_HEADS
    q_spec = pl.BlockSpec((r, tq, NSA_HEAD_DIM), lambda b, g, i: (g, b * nq + i, 0))

    def kv_spec(first_head):
        return pl.BlockSpec((1, SEQ, NSA_HEAD_DIM), lambda b, g, i: (first_head + g, b, 0))

    first = NSA_HEADS
    specs = [q_spec,
             kv_spec(first + 2 * g_heads), kv_spec(first + 3 * g_heads),
             kv_spec(first + 4 * g_heads), kv_spec(first + 5 * g_heads),
             pl.BlockSpec((1, 1, 1, SEQ // CMP_STRIDE, NSA_HEAD_DIM), lambda b, g, i: (0, g, b, 0, 0)),
             pl.BlockSpec((1, 1, 1, SEQ // CMP_STRIDE, NSA_HEAD_DIM), lambda b, g, i: (1, g, b, 0, 0)),
             pl.BlockSpec((1, 3 * r, tq), lambda b, g, i: (g, 0, b * nq + i))]
    vt_shape = (SEQ // ATT_TK, NSA_HEAD_DIM + ATT_ONES, ATT_TK)
    return pl.pallas_call(
        _nsa_attn_kernel, name="nsa_attn", grid=(BATCH, g_heads, nq),
        in_specs=specs,
        out_specs=pl.BlockSpec((tq, r * NSA_HEAD_DIM), lambda b, g, i: (b * nq + i, g)),
        out_shape=jax.ShapeDtypeStruct((TOKENS, NSA_QD), BF16),
        scratch_shapes=[pltpu.VMEM(vt_shape, BF16), pltpu.VMEM(vt_shape, BF16),
                        pltpu.VMEM((8 * N_SLC, tq), F32),
                        pltpu.VMEM((NSA_HEAD_DIM + ATT_ONES, r * tq), F32),
                        pltpu.VMEM((ATT_CHUNK_TILES * ATT_TK, r * tq), F32)],
        compiler_params=_cparams(("arbitrary", "arbitrary", "arbitrary")),
    )(heads, heads, heads, heads, heads, kc_vc, kc_vc, gates)


def _ssd_chunk_kernel(zx_ref, dtc_ref, dtr_ref, cw_ref, cb_ref, dtb_c_ref, alog_c_ref, dtb_r_ref, alog_r_ref,
                      dskip_ref, ng_ref, echan_ref, o_ref,
                      prev_scr, acum_r_scr, st_scr, fac_scr):
    chunk = pl.program_id(1)
    L = SSD_CHUNK
    W = SSD_GROUP_W
    hpg = SSD_HEADS_PER_GROUP
    n_xt = W // LANES
    x0 = SSD_D_INNER // LANES
    b0 = 2 * SSD_D_INNER // LANES
    c0 = b0 + SSD_GROUPS
    cb0 = SSD_D_INNER // LANES
    cc0 = cb0 + SSD_GROUPS

    @pl.when(chunk == 0)
    def _():
        prev_scr[...] = jnp.zeros(prev_scr.shape, F32)
        st_scr[...] = jnp.zeros(st_scr.shape, F32)

    dt_c = jax.nn.softplus(dtc_ref[...] + dtb_c_ref[...])
    adt_c = dt_c * (-jnp.exp(alog_c_ref[...]))
    dt_r = jax.nn.softplus(dtr_ref[...] + dtb_r_ref[...])
    adt_r = dt_r * (-jnp.exp(alog_r_ref[...]))
    row = lax.broadcasted_iota(I32, (L, L), 0)
    col = lax.broadcasted_iota(I32, (L, L), 1)
    causal = row >= col
    tri = jnp.where(causal, 1.0, 0.0).astype(BF16)
    tri_t = jnp.where(col >= row, 1.0, 0.0).astype(BF16)
    acum_c = _dot_split_rhs(tri, adt_c)
    acum_r_scr[...] = _dot_split_lhs(adt_r, tri_t)
    a_last = acum_c[L - 1:L, :]
    fac = jnp.concatenate([dt_c, jnp.exp(acum_c), jnp.exp(a_last - acum_c),
                           jnp.broadcast_to(jnp.exp(a_last), (SSD_FAC_PAD, LANES))], axis=0)
    fac_hi = fac.astype(BF16)
    fac_scr[0] = fac_hi
    fac_scr[1] = (fac - fac_hi.astype(F32)).astype(BF16)
    lane_w = lax.broadcasted_iota(I32, (L, LANES), 1)
    first_half = lane_w < SSD_HEAD_DIM

    def tiles(ref, first, n):
        return jnp.concatenate([ref[first + q] for q in range(n)], axis=1)

    def group_body(g, carry):
        e_chan = echan_ref[g]
        ex = _dot(fac_scr[0, 0:3 * L], e_chan)
        dt_x, ea_x, sd_x = ex[0:L], ex[L:2 * L], ex[2 * L:3 * L]
        cd_x = (_dot(fac_scr[0, 3 * L:3 * L + SSD_FAC_PAD], e_chan)
                + _dot(fac_scr[1, 3 * L:3 * L + SSD_FAC_PAD], e_chan))[0:1]

        def conv_silu(zx_first, conv_first, n):
            cur = tiles(zx_ref, zx_first, n)
            cur_f = cur.astype(F32)
            tail = tiles(prev_scr, conv_first, n)
            row8 = lax.broadcasted_iota(I32, (8, n * LANES), 0)
            w = tiles(cw_ref, conv_first, n)
            acc = jnp.broadcast_to(tiles(cb_ref, conv_first, n), (L, n * LANES))
            for k in range(SSD_CONV):
                back = SSD_CONV - 1 - k
                if back == 0:
                    xk = cur_f
                else:
                    rolled = pltpu.roll(cur_f, back, 0)
                    head = jnp.where(row8 < back, pltpu.roll(tail, back, 0), rolled[0:8])
                    xk = jnp.concatenate([head, rolled[8:]], axis=0)
                acc = acc + xk * w[k:k + 1, :]
            for q in range(n):
                prev_scr[conv_first + q] = cur_f[L - 8:L, q * LANES:(q + 1) * LANES]
            return jax.nn.silu(acc)

        xs = conv_silu(x0 + n_xt * g, n_xt * g, n_xt)
        bm = conv_silu(b0 + g, cb0 + g, 1)
        cm = conv_silu(c0 + g, cc0 + g, 1)

        xdt = xs * dt_x
        cb = jnp.where(causal, _dot_nt(cm.astype(BF16), bm.astype(BF16)), 0.0)
        y_parts = []
        for pair in range(hpg // 2):
            xd = xdt[:, pair * LANES:(pair + 1) * LANES]
            m_pair = []
            for sub in range(2):
                a_row = jnp.broadcast_to(acum_r_scr[pl.ds(g * hpg + 2 * pair + sub, 1), :], (L, L))
                seg = jnp.minimum(a_row.T - a_row, 0.0)
                m_pair.append((cb * jnp.exp(seg)).astype(BF16))
            x_pair = jnp.concatenate([jnp.where(first_half, xd, 0.0), jnp.where(first_half, 0.0, xd)], axis=0)
            y_parts.append(_dot(jnp.concatenate(m_pair, axis=1), x_pair.astype(BF16)))
        y_diag = jnp.concatenate(y_parts, axis=1)

        st = st_scr[g]
        y_off = _dot(cm.astype(BF16), st.astype(BF16)) * ea_x
        st_scr[g] = st * cd_x + _dot_tn(bm.astype(BF16), (xdt * sd_x).astype(BF16))

        y = y_diag + y_off + xs * tiles(dskip_ref, n_xt * g, n_xt)
        y = y * jax.nn.silu(tiles(zx_ref, n_xt * g, n_xt).astype(F32))
        y = y * lax.rsqrt(jnp.mean(y * y, axis=-1, keepdims=True) + NORM_EPS)
        y = y * tiles(ng_ref, n_xt * g, n_xt)
        for q in range(n_xt):
            o_ref[n_xt * g + q] = y[:, q * LANES:(q + 1) * LANES].astype(BF16)
        return carry

    lax.fori_loop(0, SSD_GROUPS, group_body, 0)


def _ssd_chunks(zx_tiles, dt_small, dt_small_t, conv_w, conv_b, dt_bias, a_log, d_skip, norm_g):
    L = SSD_CHUNK
    nc = SEQ // L
    n_zx = SSD_MAIN // LANES
    n_conv = SSD_CONV_CH // LANES
    n_inner = SSD_D_INNER // LANES
    hpg = SSD_HEADS_PER_GROUP

    def pad_heads(v):
        return jnp.zeros((LANES,), F32).at[:SSD_HEADS].set(v)

    dtb = pad_heads(dt_bias)
    alog = pad_heads(a_log)
    cw = conv_w.reshape(SSD_CONV, n_conv, LANES).transpose(1, 0, 2)
    cb = conv_b.reshape(n_conv, 1, LANES)
    d_chan = jnp.repeat(d_skip, SSD_HEAD_DIM).reshape(n_inner, 1, LANES)
    ng = norm_g.reshape(n_inner, 1, LANES)
    head =jnp.arange(LANES, dtype=I32)[None, :, None]
    grp = jnp.arange(SSD_GROUPS, dtype=I32)[:, None, None]
    e_chan = (head == grp * hpg + jnp.arange(SSD_GROUP_W, dtype=I32)[None, None, :] // SSD_HEAD_DIM).astype(BF16)
    row = lambda b, c: b * nc + c
    const3 = lambda b, c: (0, 0, 0)
    const2 = lambda b, c: (0, 0)
    in_specs = [
        pl.BlockSpec((n_zx, L, LANES), lambda b, c: (0, row(b, c), 0)),
        pl.BlockSpec((L, LANES), lambda b, c: (row(b, c), 0)),
        pl.BlockSpec((LANES, L), lambda b, c: (0, row(b, c))),
        pl.BlockSpec((n_conv, SSD_CONV, LANES), const3),
        pl.BlockSpec((n_conv, 1, LANES), const3),
        pl.BlockSpec((1, LANES), const2), pl.BlockSpec((1, LANES), const2),
        pl.BlockSpec((LANES, 1), const2), pl.BlockSpec((LANES, 1), const2),
        pl.BlockSpec((n_inner, 1, LANES), const3),
        pl.BlockSpec((n_inner, 1, LANES), const3),
        pl.BlockSpec((SSD_GROUPS, LANES, SSD_GROUP_W), const3),
    ]
    return pl.pallas_call(
        _ssd_chunk_kernel, name="ssd_chunks", grid=(BATCH, nc),
        in_specs=in_specs,
        out_specs=pl.BlockSpec((n_inner, L, LANES), lambda b, c: (0, row(b, c), 0)),
        out_shape=jax.ShapeDtypeStruct((n_inner, TOKENS, LANES), BF16),
        scratch_shapes=[pltpu.VMEM((n_conv, 8, LANES), F32),
                        pltpu.VMEM((LANES, L), F32),
                        pltpu.VMEM((SSD_GROUPS, SSD_D_STATE, SSD_GROUP_W), F32),
                        pltpu.VMEM((2, 3 * L + SSD_FAC_PAD, LANES), BF16)],
        compiler_params=_cparams(("arbitrary", "arbitrary")),
    )(zx_tiles, dt_small, dt_small_t, cw, cb, dtb.reshape(1, LANES), alog.reshape(1, LANES),
      dtb.reshape(LANES, 1), alog.reshape(LANES, 1), d_chan, ng, e_chan)


def _gather_rows(src_hbm, idx_ref, base, dst, sem, n_rows):
    def body(b, carry):
        for u in range(GATHER_UNROLL):
            r = b * GATHER_UNROLL + u
            tok = idx_ref[base + r]
            pltpu.make_async_copy(src_hbm.at[pl.ds(tok, 1), :], dst.at[pl.ds(r, 1), :], sem).start()
        return carry

    lax.fori_loop(0, n_rows // GATHER_UNROLL, body, 0)


def _moe_ffn_kernel(te_ref, pos_ref, nact_ref, pad_ref, wp_ref, x_hbm, g_ref, wg_hbm, wu_hbm, wd_hbm, y_ref,
                    tok_ref, buf, sem, wg_f, wu_f, wd_f, wsem, wg_bf, wu_bf, wd_bf, *, layer):
    i = pl.program_id(0)
    n_act = nact_ref[0]
    tm = MOE_TM
    slot = i % 2
    run_start = wp_ref[i] == 1
    w_slot = wp_ref[MOE_TILES + i]
    next_expert = wp_ref[2 * MOE_TILES + i]
    n_rows = pl.multiple_of(wp_ref[3 * MOE_TILES + i], GATHER_UNROLL)

    def gather(tile, s):
        _gather_rows(x_hbm, tok_ref, tile * tm, buf.at[s], sem.at[s], wp_ref[3 * MOE_TILES + tile])

    def weight_copies(e, s):
        return (pltpu.make_async_copy(wg_hbm.at[layer, e], wg_f.at[s], wsem.at[s, 0]),
                pltpu.make_async_copy(wu_hbm.at[layer, e], wu_f.at[s], wsem.at[s, 1]),
                pltpu.make_async_copy(wd_hbm.at[layer, e], wd_f.at[s], wsem.at[s, 2]))

    @pl.when(i == 0)
    def _():
        for c in weight_copies(te_ref[0], 0):
            c.start()
        buf[...] = jnp.zeros(buf.shape, F32)
        def clear(q, carry):
            tok_ref[q] = 0
            return carry

        def clear_padding(e, carry):
            lax.fori_loop(pad_ref[e], pad_ref[MOE_EXPERTS + e], clear, 0)
            return carry

        lax.fori_loop(0, MOE_EXPERTS, clear_padding, 0)
        for k in range(MOE_TOPK):
            def place(t, carry, k=k):
                tok_ref[pos_ref[k * TOKENS + t]] = t
                return carry

            lax.fori_loop(0, TOKENS, place, 0, unroll=GATHER_UNROLL)
        gather(0, 0)

    @pl.when(run_start)
    def _():
        for c in weight_copies(te_ref[i], w_slot):
            c.wait()

        @pl.when(next_expert >= 0)
        def _():
            for c in weight_copies(next_expert, 1 - w_slot):
                c.start()

        wg_bf[...] = wg_f[w_slot].astype(BF16)
        wu_bf[...] = wu_f[w_slot].astype(BF16)
        wd_bf[...] = wd_f[w_slot].astype(BF16)

    @pl.when(i >= n_act)
    def _():
        y_ref[...] = jnp.zeros(y_ref.shape, F32)

    @pl.when(i < n_act)
    def _():
        pltpu.make_async_copy(x_hbm.at[pl.ds(0, n_rows), :], buf.at[slot, pl.ds(0, n_rows), :],
                              sem.at[slot]).wait()

        @pl.when(i + 1 < n_act)
        def _():
            gather(i + 1, 1 - slot)

        h = _rms(buf[slot], g_ref[...]).astype(BF16)
        act = jax.nn.silu(_dot(h, wg_bf[...])) * _dot(h, wu_bf[...])
        y_ref[...] = _dot(act.astype(BF16), wd_bf[...])


def _moe_ffn(x, g, w_gate, w_up, w_down, layer, tile_expert, pos_kmajor, n_active, pad_rows, weight_plan):
    any_spec = pl.BlockSpec(memory_space=pl.ANY)
    grid_spec = pltpu.PrefetchScalarGridSpec(
        num_scalar_prefetch=5, grid=(MOE_TILES,),
        in_specs=[any_spec, pl.BlockSpec((1, D_MODEL), lambda i, *_: (0, 0)), any_spec, any_spec, any_spec],
        out_specs=pl.BlockSpec((MOE_TM, D_MODEL), lambda i, *_: (i, 0)),
        scratch_shapes=[pltpu.SMEM((MOE_ROWS,), I32),
                        pltpu.VMEM((2, MOE_TM, D_MODEL), F32), pltpu.SemaphoreType.DMA((2,)),
                        pltpu.VMEM((2, D_MODEL, MOE_D_FF), F32), pltpu.VMEM((2, D_MODEL, MOE_D_FF), F32),
                        pltpu.VMEM((2, MOE_D_FF, D_MODEL), F32), pltpu.SemaphoreType.DMA((2, 3)),
                        pltpu.VMEM((D_MODEL, MOE_D_FF), BF16), pltpu.VMEM((D_MODEL, MOE_D_FF), BF16),
                        pltpu.VMEM((MOE_D_FF, D_MODEL), BF16)])
    return pl.pallas_call(
        functools.partial(_moe_ffn_kernel, layer=layer), name="moe_ffn", grid_spec=grid_spec,
        out_shape=jax.ShapeDtypeStruct((MOE_ROWS, D_MODEL), F32),
        compiler_params=_cparams(("arbitrary",)),
    )(tile_expert, pos_kmajor, n_active, pad_rows, weight_plan, x, g.reshape(1, D_MODEL), w_gate, w_up, w_down)


def _moe_combine_kernel(pos_ref, x_ref, cw_ref, g_ref, ws_ref, y_hbm, *refs, final_norm):
    if final_norm:
        o_ref, buf, sem = refs
    else:
        o_ref, hn_ref, small_ref, small_t_ref, buf, sem = refs
    i = pl.program_id(0)
    n = pl.num_programs(0)
    tm = CMB_TM
    slot = i % 2

    def issue(tile, s):
        for k in range(MOE_TOPK):
            _gather_rows(y_hbm, pos_ref, (k * (TOKENS // tm) + tile) * tm, buf.at[s, k], sem.at[s], tm)

    @pl.when(i == 0)
    def _():
        issue(0, 0)

    @pl.when(i + 1 < n)
    def _():
        issue(i + 1, 1 - slot)

    for k in range(MOE_TOPK):
        pltpu.make_async_copy(y_hbm.at[pl.ds(0, tm), :], buf.at[slot, k], sem.at[slot]).wait()
    cw = cw_ref[...]
    out = x_ref[...] + cw[:, 0:1] * buf[slot, 0] + cw[:, 1:2] * buf[slot, 1]
    y = _rms(out, g_ref[...])
    if final_norm:
        o_ref[...] = y
    else:
        o_ref[...] = out
        hn_ref[...] = y.astype(BF16)
        small = _dot_x3(y, ws_ref[...])
        small_ref[...] = small
        small_t_ref[...] = small.T


def _moe_combine(x, cw, y_sorted, pos_kmajor, g_norm, w_small, final_norm):
    ws = jnp.zeros((D_MODEL, LANES), F32)
    if not final_norm:
        ws = ws.at[:, :w_small.shape[1]].set(w_small)
    row = pl.BlockSpec((CMB_TM, D_MODEL), lambda i, pos: (i, 0))
    small = pl.BlockSpec((CMB_TM, LANES), lambda i, pos: (i, 0))
    x_shape = jax.ShapeDtypeStruct((TOKENS, D_MODEL), F32)
    if final_norm:
        out_specs, out_shape = row, x_shape
    else:
        out_specs = [row, row, small, pl.BlockSpec((LANES, CMB_TM), lambda i, pos: (0, i))]
        out_shape = [x_shape, jax.ShapeDtypeStruct((TOKENS, D_MODEL), BF16),
                     jax.ShapeDtypeStruct((TOKENS, LANES), F32), jax.ShapeDtypeStruct((LANES, TOKENS), F32)]
    grid_spec = pltpu.PrefetchScalarGridSpec(
        num_scalar_prefetch=1, grid=(TOKENS // CMB_TM,),
        in_specs=[row, small,
                  pl.BlockSpec((1, D_MODEL), lambda i, pos: (0, 0)),
                  pl.BlockSpec((D_MODEL, LANES), lambda i, pos: (0, 0)),
                  pl.BlockSpec(memory_space=pl.ANY)],
        out_specs=out_specs,
        scratch_shapes=[pltpu.VMEM((2, MOE_TOPK, CMB_TM, D_MODEL), F32), pltpu.SemaphoreType.DMA((2,))])
    return pl.pallas_call(
        functools.partial(_moe_combine_kernel, final_norm=final_norm), name="moe_combine", grid_spec=grid_spec,
        out_shape=out_shape,
        compiler_params=_cparams(("arbitrary",)),
    )(pos_kmajor, x, cw, g_norm.reshape(1, D_MODEL), ws, y_sorted)


def _moe_plan(eid):
    e = eid[:, :MOE_TOPK].reshape(-1)
    onehot = (e[:, None] == jnp.arange(MOE_EXPERTS, dtype=I32)[None, :]).astype(I32)
    csum = jnp.cumsum(onehot, axis=0)
    counts = csum[-1]
    padded = ((counts + MOE_TM - 1) // MOE_TM) * MOE_TM
    g_end = jnp.cumsum(padded)
    g_start = g_end - padded
    pos = jnp.sum(onehot * (g_start[None, :] + csum - 1), axis=1)
    n_active = (g_end[-1] // MOE_TM).astype(I32)
    tile_start = jnp.arange(MOE_TILES, dtype=I32) * MOE_TM
    te = jnp.sum((g_end[None, :] <= tile_start[:, None]).astype(I32), axis=1)
    last = jnp.max(jnp.where(counts > 0, jnp.arange(MOE_EXPERTS, dtype=I32), 0))
    tile_expert = jnp.minimum(te, last)
    pos_kmajor = pos.reshape(TOKENS, MOE_TOPK).T.reshape(-1)
    pad_rows = jnp.concatenate([g_start + counts, g_end])
    experts = jnp.arange(MOE_EXPERTS, dtype=I32)
    run_start = jnp.concatenate([jnp.ones((1,), I32), (tile_expert[1:] != tile_expert[:-1]).astype(I32)])
    w_slot = (jnp.cumsum(run_start) - 1) % 2
    later = (experts[None, :] > experts[:, None]) & (counts > 0)[None, :]
    next_nonempty = jnp.min(jnp.where(later, experts[None, :], MOE_EXPERTS), axis=1)
    next_nonempty = jnp.where(next_nonempty == MOE_EXPERTS, -1, next_nonempty)
    real_rows = jnp.clip((g_start + counts)[tile_expert] - tile_start, 0, MOE_TM)
    real_rows = jnp.where(tile_start < g_end[-1], real_rows, 0)
    gather_rows = jnp.minimum(-(-real_rows // GATHER_UNROLL) * GATHER_UNROLL, MOE_TM)
    weight_plan = jnp.concatenate([run_start, w_slot, next_nonempty[tile_expert], gather_rows]).astype(I32)
    return tile_expert, n_active.reshape(1), pos_kmajor, pad_rows, weight_plan


def _hier_moe_add(x, eid, cw, ln_g, w_gate, w_up, w_down, layer, g_norm, w_small, final_norm):
    tile_expert, n_active, pos_kmajor, pad_rows, weight_plan = _moe_plan(eid)
    y_sorted = _moe_ffn(x, ln_g, w_gate, w_up, w_down, layer, tile_expert, pos_kmajor, n_active, pad_rows,
                        weight_plan)
    return _moe_combine(x, cw, y_sorted, pos_kmajor, g_norm, w_small, final_norm)


def _rope_tables():
    pos = jnp.arange(SEQ, dtype=F32)
    inv = 1.0 / (ROPE_THETA ** (jnp.arange(0, NSA_HEAD_DIM, 2, dtype=F32) / NSA_HEAD_DIM))
    ang = pos[:, None] * inv[None, :]
    cos, sin = jnp.cos(ang), jnp.sin(ang)
    return jnp.concatenate([cos, cos], axis=1), jnp.concatenate([-sin, sin], axis=1)


def _nsa_mixer(hn, g_lin_t, w_in, cmp_pe, cmp_w1, cmp_w2):
    w_in_t = w_in.T
    cos_full, sin_signed = _rope_tables()
    heads = _nsa_inproj(hn, w_in_t, cos_full, sin_signed)
    first_c = NSA_HEADS
    kc_vc = _compress(heads, first_c, cmp_pe, cmp_w1, cmp_w2)
    gates_t = g_lin_t[:NSA_GATES].reshape(NSA_KV_HEADS, 3 * NSA_Q_PER_KV, TOKENS)
    return _nsa_attention(heads, kc_vc, gates_t)


def _ssd_mixer(hn, dt_small, dt_small_t, w_in, conv_w, conv_b, dt_bias, a_log, d_skip, norm_g):
    zx_tiles = _matmul_tiles(hn, w_in.T, SSD_MAIN, 1024)
    return _ssd_chunks(zx_tiles, dt_small, dt_small_t, conv_w, conv_b, dt_bias, a_log, d_skip, norm_g)


def kernel(x, ln_mix, ln_ffn, ln_final, nsa_w_in, nsa_cmp_pe, nsa_cmp_w1, nsa_cmp_w2, nsa_w_out,
           ssd_w_in, ssd_conv_w, ssd_conv_b, ssd_dt_bias, ssd_a_log, ssd_d, ssd_norm, ssd_w_out,
           moe_w_group, moe_b_group, moe_w_expert, moe_b_expert, moe_w_gate, moe_w_up, moe_w_down):
    def small_weight(i):
        if i % N_MIXERS == 0:
            return nsa_w_in[i // N_MIXERS].T[NSA_MAIN:].T
        return ssd_w_in[i // N_MIXERS].T[SSD_MAIN:].T

    h = x.reshape(TOKENS, D_MODEL)
    hn, small, small_t = _norm_small(h, ln_mix[0], small_weight(0))
    for i in range(DEPTH):
        j = i // N_MIXERS
        last = i == DEPTH - 1
        if i % N_MIXERS == 0:
            mix = _nsa_mixer(hn, small_t, nsa_w_in[j], nsa_cmp_pe[j], nsa_cmp_w1[j], nsa_cmp_w2[j])
            w_out = nsa_w_out[j]
        else:
            mix = _ssd_mixer(hn, small, small_t, ssd_w_in[j], ssd_conv_w[j], ssd_conv_b[j], ssd_dt_bias[j],
                             ssd_a_log[j], ssd_d[j], ssd_norm[j])
            w_out = ssd_w_out[j]
        h, eid, cw = _outproj_route(mix, w_out, h, ln_ffn[i], moe_w_group[i], moe_b_group[i],
                                    moe_w_expert[i], moe_b_expert[i])
        out = _hier_moe_add(h, eid, cw, ln_ffn[i], moe_w_gate, moe_w_up, moe_w_down, i,
                            ln_final if last else ln_mix[i + 1], None if last else small_weight(i + 1), last)
        if last:
            h = out
        else:
            h, hn, small, small_t = out
    return h.reshape(BATCH, SEQ, D_MODEL)
```

```python
import functools

import jax
import jax.numpy as jnp
from jax import lax
from jax.experimental import pallas as pl
from jax.experimental.pallas import tpu as pltpu

F32 = jnp.float32
BF16 = jnp.bfloat16
I32 = jnp.int32

D_MODEL = 2048
BATCH = 4
SEQ = 2048
TOKENS = BATCH * SEQ
DEPTH = 2
N_MIXERS = 2
NORM_EPS = 1e-6
NEG_INF = -1e30
LOG2_E = 1.4426950408889634
ROPE_THETA = 10000.0

NSA_HEADS = 16
NSA_KV_HEADS = 4
NSA_HEAD_DIM = D_MODEL // NSA_HEADS
NSA_Q_PER_KV = NSA_HEADS // NSA_KV_HEADS
CMP_BLOCK = 32
CMP_STRIDE = 16
CMP_HIDDEN = 256
N_CMP = (SEQ - CMP_BLOCK) // CMP_STRIDE + 1
SLC_BLOCK = 64
SLC_TOPK = 16
SLC_LOCAL = 2
SLC_FORCE = 1e4
N_SLC = SEQ // SLC_BLOCK
WINDOW = 512
NSA_QD = NSA_HEADS * NSA_HEAD_DIM
NSA_KVD = NSA_KV_HEADS * NSA_HEAD_DIM
NSA_MAIN = NSA_QD + 6 * NSA_KVD
NSA_GATES = 3 * NSA_HEADS

SSD_D_INNER = 2 * D_MODEL
SSD_HEAD_DIM = 64
SSD_HEADS = SSD_D_INNER // SSD_HEAD_DIM
SSD_GROUPS = 8
SSD_HEADS_PER_GROUP = SSD_HEADS // SSD_GROUPS
SSD_D_STATE = 128
SSD_CONV = 4
SSD_CHUNK = 128
SSD_GROUP_W = SSD_D_INNER // SSD_GROUPS
SSD_BC = SSD_GROUPS * SSD_D_STATE
SSD_CONV_CH = SSD_D_INNER + 2 * SSD_BC
SSD_MAIN = SSD_D_INNER + SSD_CONV_CH
SSD_FAC_PAD = 16

MOE_GROUPS = 4
MOE_EPG = 8
MOE_EXPERTS = MOE_GROUPS * MOE_EPG
MOE_TOPK = 2
MOE_D_FF = 512

LANES = 128
VMEM_LIMIT = 56 * 1024 * 1024

NORM_TM = 512
MM_TM = 1024
OUT_TM = 512
ATT_TQ = 256
ATT_TK = 256
ATT_CHUNK_TILES = 2
ATT_ONES = 16
MOE_TM = 256
MOE_TILES = (TOKENS * MOE_TOPK) // MOE_TM + MOE_EXPERTS
MOE_ROWS = MOE_TILES * MOE_TM
CMB_TM = 256
GATHER_UNROLL = 8


def _cparams(sem):
    return pltpu.CompilerParams(dimension_semantics=sem, vmem_limit_bytes=VMEM_LIMIT)


def _split3(x):
    hi = x.astype(BF16)
    r1 = x - hi.astype(F32)
    mid = r1.astype(BF16)
    lo = (r1 - mid.astype(F32)).astype(BF16)
    return hi, mid, lo


def _dot(a, b):
    return jnp.dot(a, b, preferred_element_type=F32)


def _dot_nt(a, b):
    return lax.dot_general(a, b, (((1,), (1,)), ((), ())), preferred_element_type=F32)


def _dot_tn(a, b):
    return lax.dot_general(a, b, (((0,), (0,)), ((), ())), preferred_element_type=F32)


def _dot_split_lhs(x, m_bf16):
    hi, mid, lo = _split3(x)
    return _dot(hi, m_bf16) + _dot(mid, m_bf16) + _dot(lo, m_bf16)


def _dot_split_rhs(m_bf16, x):
    hi, mid, lo = _split3(x)
    return _dot(m_bf16, hi) + _dot(m_bf16, mid) + _dot(m_bf16, lo)


def _dot_x3(a, w):
    a_hi = a.astype(BF16)
    a_lo = (a - a_hi.astype(F32)).astype(BF16)
    w_hi = w.astype(BF16)
    w_lo = (w - w_hi.astype(F32)).astype(BF16)
    n = w.shape[1]
    both = _dot(a_hi, jnp.concatenate([w_hi, w_lo], axis=1))
    return both[:, :n] + both[:, n:] + _dot(a_lo, w_hi)


def _rms(x, g):
    y = x * lax.rsqrt(jnp.mean(x * x, axis=-1, keepdims=True) + NORM_EPS)
    return y * g


def _norm_small_kernel(x_ref, g_ref, ws_ref, hn_ref, small_ref, small_t_ref):
    y = _rms(x_ref[...], g_ref[...])
    hn_ref[...] = y.astype(BF16)
    small = _dot_x3(y, ws_ref[...])
    small_ref[...] = small
    small_t_ref[...] = small.T


def _norm_small(x, g, w_small):
    n = w_small.shape[1]
    ws = jnp.zeros((D_MODEL, LANES), F32).at[:, :n].set(w_small)
    return pl.pallas_call(
        _norm_small_kernel, name="norm_small", grid=(TOKENS // NORM_TM,),
        in_specs=[pl.BlockSpec((NORM_TM, D_MODEL), lambda i: (i, 0)),
                  pl.BlockSpec((1, D_MODEL), lambda i: (0, 0)),
                  pl.BlockSpec((D_MODEL, LANES), lambda i: (0, 0))],
        out_specs=[pl.BlockSpec((NORM_TM, D_MODEL), lambda i: (i, 0)),
                   pl.BlockSpec((NORM_TM, LANES), lambda i: (i, 0)),
                   pl.BlockSpec((LANES, NORM_TM), lambda i: (0, i))],
        out_shape=[jax.ShapeDtypeStruct((TOKENS, D_MODEL), BF16),
                   jax.ShapeDtypeStruct((TOKENS, LANES), F32),
                   jax.ShapeDtypeStruct((LANES, TOKENS), F32)],
        compiler_params=_cparams(("parallel",)),
    )(x, g.reshape(1, D_MODEL), ws)


def _route(x, g, ws, bias):
    y = _rms(x, g)
    logits = _dot_x3(y, ws) + bias
    lane = lax.broadcasted_iota(I32, logits.shape, 1)
    big = jnp.int32(LANES)
    neg = -jnp.inf
    gl = jnp.where(lane < MOE_GROUPS, logits, neg)
    gmax = jnp.max(gl, axis=-1, keepdims=True)
    gsum = jnp.sum(jnp.exp(gl - gmax), axis=-1, keepdims=True)
    g_w = 1.0 / gsum
    g_sel = jnp.min(jnp.where(gl == gmax, lane, big), axis=-1, keepdims=True)
    lo = MOE_GROUPS + g_sel * MOE_EPG
    el = jnp.where((lane >= lo) & (lane < lo + MOE_EPG), logits, neg)
    v1 = jnp.max(el, axis=-1, keepdims=True)
    i1 = jnp.min(jnp.where(el == v1, lane, big), axis=-1, keepdims=True)
    el2 = jnp.where(lane == i1, neg, el)
    v2 = jnp.max(el2, axis=-1, keepdims=True)
    i2 = jnp.min(jnp.where(el2 == v2, lane, big), axis=-1, keepdims=True)
    e2 = jnp.exp(v2 - v1)
    den = 1.0 + e2
    w1 = (1.0 / den) * g_w
    w2 = (e2 / den) * g_w
    eid = jnp.where(lane == 0, i1 - MOE_GROUPS, jnp.where(lane == 1, i2 - MOE_GROUPS, 0))
    cw = jnp.where(lane == 0, w1, jnp.where(lane == 1, w2, 0.0))
    return eid, cw


def _inproj_heads_kernel(a_ref, w_ref, cos_ref, sin_ref, o_ref, wbf_ref):
    j = pl.program_id(0)

    @pl.when(pl.program_id(1) == 0)
    def _():
        wbf_ref[...] = w_ref[...].astype(BF16)

    acc = _dot_nt(a_ref[...], wbf_ref[...])
    q_tiles = NSA_QD // acc.shape[1]
    c = cos_ref[...]
    s = sin_ref[...]

    def head(h):
        return acc[:, h * NSA_HEAD_DIM:(h + 1) * NSA_HEAD_DIM]

    def rotary(xh):
        return (xh * c + pltpu.roll(xh, NSA_HEAD_DIM // 2, 1) * s).astype(BF16)

    for h in range(NSA_KV_HEADS):
        o_ref[h] = rotary(head(h))

    @pl.when(j < q_tiles)
    def _():
        for h in range(NSA_KV_HEADS, 2 * NSA_KV_HEADS):
            o_ref[h] = rotary(head(h))

    @pl.when(j >= q_tiles)
    def _():
        for h in range(NSA_KV_HEADS, 2 * NSA_KV_HEADS):
            o_ref[h] = head(h).astype(BF16)


def _nsa_inproj(hn, w_in_t, cos_full, sin_signed):
    tn = 2 * NSA_KVD
    n_heads_tile = tn // NSA_HEAD_DIM
    s_tiles = SEQ // MM_TM
    return pl.pallas_call(
        _inproj_heads_kernel, name="nsa_inproj", grid=(NSA_MAIN // tn, TOKENS // MM_TM),
        in_specs=[pl.BlockSpec((MM_TM, D_MODEL), lambda j, i: (i, 0)),
                  pl.BlockSpec((tn, D_MODEL), lambda j, i: (j, 0)),
                  pl.BlockSpec((MM_TM, NSA_HEAD_DIM), lambda j, i: (i % s_tiles, 0)),
                  pl.BlockSpec((MM_TM, NSA_HEAD_DIM), lambda j, i: (i % s_tiles, 0))],
        out_specs=pl.BlockSpec((n_heads_tile, MM_TM, NSA_HEAD_DIM), lambda j, i: (j, i, 0)),
        out_shape=jax.ShapeDtypeStruct((NSA_MAIN // NSA_HEAD_DIM, TOKENS, NSA_HEAD_DIM), BF16),
        scratch_shapes=[pltpu.VMEM((tn, D_MODEL), BF16)],
        compiler_params=_cparams(("arbitrary", "arbitrary")),
    )(hn, w_in_t, cos_full, sin_signed)


def _matmul_tiles_kernel(a_ref, w_ref, o_ref, wbf_ref):
    @pl.when(pl.program_id(1) == 0)
    def _():
        wbf_ref[...] = w_ref[...].astype(BF16)

    acc = _dot_nt(a_ref[...], wbf_ref[...])
    for t in range(acc.shape[1] // LANES):
        o_ref[t] = acc[:, t * LANES:(t + 1) * LANES].astype(o_ref.dtype)


def _matmul_tiles(a, w_t, n_cols, tn):
    k = a.shape[1]
    return pl.pallas_call(
        _matmul_tiles_kernel, name="matmul_tiles", grid=(n_cols // tn, TOKENS // MM_TM),
        in_specs=[pl.BlockSpec((MM_TM, k), lambda j, i: (i, 0)),
                  pl.BlockSpec((tn, k), lambda j, i: (j, 0))],
        out_specs=pl.BlockSpec((tn // LANES, MM_TM, LANES), lambda j, i: (j, i, 0)),
        out_shape=jax.ShapeDtypeStruct((n_cols // LANES, TOKENS, LANES), BF16),
        scratch_shapes=[pltpu.VMEM((tn, k), BF16)],
        compiler_params=_cparams(("arbitrary", "arbitrary")),
    )(a, w_t)


def _outproj_route_kernel(a_ref, w_ref, r_ref, g_ref, ws_ref, b_ref, o_ref, eid_ref, cw_ref):
    if len(a_ref.shape) == 3:
        a = jnp.concatenate([a_ref[t] for t in range(a_ref.shape[0])], axis=1)
    else:
        a = a_ref[...]
    x_new = r_ref[...] + _dot(a, w_ref[...])
    o_ref[...] = x_new
    eid_ref[...], cw_ref[...] = _route(x_new, g_ref[...], ws_ref[...], b_ref[...])


def _outproj_route(a, w, resid, ln_ffn_g, w_group, b_group, w_expert, b_expert):
    tiled = a.ndim == 3
    k, n = w.shape
    n_r = MOE_GROUPS + MOE_EXPERTS
    ws = jnp.zeros((D_MODEL, LANES), F32).at[:, :n_r].set(jnp.concatenate([w_group, w_expert], axis=1))
    bs = jnp.zeros((1, LANES), F32).at[0, :n_r].set(jnp.concatenate([b_group, b_expert]))
    a_spec = (pl.BlockSpec((k // LANES, OUT_TM, LANES), lambda i: (0, i, 0)) if tiled
              else pl.BlockSpec((OUT_TM, k), lambda i: (i, 0)))
    small = pl.BlockSpec((OUT_TM, LANES), lambda i: (i, 0))
    return pl.pallas_call(
        _outproj_route_kernel, name="outproj_route", grid=(TOKENS // OUT_TM,),
        in_specs=[a_spec,
                  pl.BlockSpec((k, n), lambda i: (0, 0), pipeline_mode=pl.Buffered(1)),
                  pl.BlockSpec((OUT_TM, n), lambda i: (i, 0)),
                  pl.BlockSpec((1, D_MODEL), lambda i: (0, 0)),
                  pl.BlockSpec((D_MODEL, LANES), lambda i: (0, 0)),
                  pl.BlockSpec((1, LANES), lambda i: (0, 0))],
        out_specs=[pl.BlockSpec((OUT_TM, n), lambda i: (i, 0)), small, small],
        out_shape=[jax.ShapeDtypeStruct((TOKENS, n), F32),
                   jax.ShapeDtypeStruct((TOKENS, LANES), I32),
                   jax.ShapeDtypeStruct((TOKENS, LANES), F32)],
        compiler_params=_cparams(("parallel",)),
    )(a, w.astype(BF16), resid, ln_ffn_g.reshape(1, D_MODEL), ws, bs)


def _compress_kernel(x_ref, pe_ref, w1_ref, w2_ref, o_ref, tok_scr):
    half = CMP_STRIDE * NSA_HEAD_DIM
    n_chunk = SEQ // CMP_STRIDE
    tok_scr[...] = x_ref[0].astype(F32)
    x = jnp.concatenate([tok_scr[pl.ds(p, n_chunk, stride=CMP_STRIDE), :] for p in range(CMP_STRIDE)],
                        axis=1).astype(BF16)
    w1 = w1_ref[0].astype(BF16)
    top = _dot(x, w1[:half])
    bot = _dot(x, w1[half:])
    pe = jnp.broadcast_to(pe_ref[0], (8, 2 * half)).astype(BF16)
    pe_bias = _dot(pe, w1)[0:1]
    hid = top + pltpu.roll(bot, bot.shape[0] - 1, 0) + pe_bias
    act = jax.nn.gelu(hid)
    o_ref[0, 0, 0] = _dot(act.astype(BF16), w2_ref[0].astype(BF16))


def _compress(heads, first_head, pe, w1, w2):
    n_chunk = SEQ // CMP_STRIDE
    feat = CMP_STRIDE * NSA_HEAD_DIM
    return pl.pallas_call(
        _compress_kernel, name="nsa_compress", grid=(2, NSA_KV_HEADS, BATCH),
        in_specs=[pl.BlockSpec((1, SEQ, NSA_HEAD_DIM), lambda a, g, b: (first_head + a * NSA_KV_HEADS + g, b, 0)),
                  pl.BlockSpec((1, 1, 2 * feat), lambda a, g, b: (a, 0, 0)),
                  pl.BlockSpec((1, 2 * feat, CMP_HIDDEN), lambda a, g, b: (a, 0, 0)),
                  pl.BlockSpec((1, CMP_HIDDEN, NSA_HEAD_DIM), lambda a, g, b: (a, 0, 0))],
        out_specs=pl.BlockSpec((1, 1, 1, n_chunk, NSA_HEAD_DIM), lambda a, g, b: (a, g, b, 0, 0)),
        out_shape=jax.ShapeDtypeStruct((2, NSA_KV_HEADS, BATCH, n_chunk, NSA_HEAD_DIM), F32),
        scratch_shapes=[pltpu.VMEM((SEQ, NSA_HEAD_DIM), F32)],
        compiler_params=_cparams(("parallel", "parallel", "parallel")),
    )(heads, pe.reshape(2, 1, 2 * feat), w1, w2)


def _nsa_attn_kernel(q_ref, ks_ref, vs_ref, kw_ref, vw_ref, kc_ref, vc_ref, gate_ref, o_ref,
                     vst_scr, vwt_scr, sel_scr, acc_scr, sc_scr):
    qi = pl.program_id(2)
    tq = ATT_TQ
    tk = ATT_TK
    dh = NSA_HEAD_DIM
    r_heads = NSA_Q_PER_KV
    n_kt = SEQ // tk
    scale = dh ** -0.5 * LOG2_E

    @pl.when(qi == 0)
    def _():
        ones = jnp.ones((ATT_ONES, tk), BF16)
        for kt in range(n_kt):
            rows = slice(kt * tk, (kt + 1) * tk)
            vst_scr[kt, 0:dh, :] = vs_ref[0, rows, :].astype(F32).T.astype(BF16)
            vwt_scr[kt, 0:dh, :] = vw_ref[0, rows, :].astype(F32).T.astype(BF16)
            vst_scr[kt, dh:dh + ATT_ONES, :] = ones
            vwt_scr[kt, dh:dh + ATT_ONES, :] = ones

    q_s = jnp.concatenate([q_ref[r].astype(F32).T for r in range(r_heads)], axis=1) * scale
    q_hi = q_s.astype(BF16)
    q_t = jnp.concatenate([q_hi, (q_s - q_hi.astype(F32)).astype(BF16)], axis=0)

    def qk(k):
        return _dot(jnp.concatenate([k, k], axis=1), q_t)
    n_cp = SEQ // CMP_STRIDE
    sub = lax.broadcasted_iota(I32, (n_cp, tq), 0)
    t_pos = qi * tq + lax.broadcasted_iota(I32, (n_cp, tq), 1)

    def tile4(a):
        return jnp.concatenate([a] * r_heads, axis=1)

    kc = kc_ref[0, 0, 0].astype(BF16)
    vc = vc_ref[0, 0, 0].astype(BF16)
    ok_c = jnp.where(sub * CMP_STRIDE + CMP_BLOCK - 1 <= t_pos, jnp.where(sub < N_CMP, 1.0, 0.0), 0.0)
    ok_c4 = tile4(ok_c)
    s_c = qk(kc) + (ok_c4 - 1.0) * (-NEG_INF)
    e_c = jnp.exp2(s_c - jnp.max(s_c, axis=0, keepdims=True))
    p_c = (e_c / jnp.sum(e_c, axis=0, keepdims=True)) * ok_c4
    o_cmp = _dot_tn(vc, p_c.astype(BF16))
    p_sum = p_c[:, 0:tq]
    for r in range(1, r_heads):
        p_sum = p_sum + p_c[:, r * tq:(r + 1) * tq]

    blk_row = lax.broadcasted_iota(I32, (LANES, LANES), 0)
    cmp_col = lax.broadcasted_iota(I32, (LANES, LANES), 1)
    s_start = blk_row * SLC_BLOCK
    c_start = cmp_col * CMP_STRIDE
    ov_t = jnp.maximum(jnp.minimum(c_start + CMP_BLOCK, s_start + SLC_BLOCK)
                       - jnp.maximum(c_start, s_start), 0).astype(F32) / CMP_BLOCK
    ov_t = jnp.where(blk_row < N_SLC, ov_t, 0.0).astype(BF16)
    imp = _dot_split_rhs(ov_t, p_sum)[0:N_SLC]
    j_blk = lax.broadcasted_iota(I32, (N_SLC, tq), 0)
    dist = (qi * tq + lax.broadcasted_iota(I32, (N_SLC, tq), 1)) // SLC_BLOCK - j_blk
    imp = jnp.where(j_blk == 0, SLC_FORCE, jnp.where(dist < 0, imp, jnp.where(dist < SLC_LOCAL, SLC_FORCE, imp)))
    imp = jnp.where(dist >= 0, imp, -jnp.inf)
    cnt = jnp.zeros((N_SLC, tq), I32)
    for k in range(N_SLC):
        row_k = imp[k:k + 1, :]
        tie = jnp.where(j_blk > k, 1, 0)
        cnt = cnt + jnp.where(row_k > imp, 1, jnp.where(row_k == imp, tie, 0))
    sel = jnp.where(cnt < min(SLC_TOPK, N_SLC), 1.0, 0.0)
    for j in range(N_SLC):
        sel_scr[8 * j:8 * j + 8, :] = jnp.broadcast_to(sel[j:j + 1, :], (8, tq))

    def scores(k, ok):
        return qk(k) + tile4((ok - 1.0) * (-NEG_INF))

    def weighted_values(vt_scr, kt0, pr, n_tiles):
        out = None
        for u in range(n_tiles):
            term = _dot(vt_scr[kt0 + u], pr[u * tk:(u + 1) * tk].astype(BF16))
            out = term if out is None else out + term
        return out

    n_ct = ATT_CHUNK_TILES
    ck = n_ct * tk
    sub_c = lax.broadcasted_iota(I32, (ck, tq), 0)
    t_pos_c = qi * tq + lax.broadcasted_iota(I32, (ck, tq), 1)
    blocks_per_chunk = ck // SLC_BLOCK
    acc_scr[...] = jnp.zeros(acc_scr.shape, F32)

    def chunk_scores(c):
        c = jnp.minimum(c, SEQ // ck - 1)
        start = pl.multiple_of(c * ck, ck)
        k = ks_ref[0, pl.ds(start, ck), :]
        rows8 = sel_scr[pl.ds(pl.multiple_of(c * (8 * blocks_per_chunk), 8 * blocks_per_chunk),
                              8 * blocks_per_chunk), :]
        picked = jnp.concatenate(
            [rows8[8 * u:8 * u + 8] for u in range(blocks_per_chunk) for _ in range(SLC_BLOCK // 8)], axis=0)
        return scores(k, jnp.where(start + sub_c <= t_pos_c, picked, 0.0))

    sc_scr[...] = chunk_scores(0)

    def slc_body(c, m_old):
        sc = sc_scr[...]
        sc_next = chunk_scores(c + 1)
        m_new = jnp.maximum(m_old, jnp.max(sc, axis=0, keepdims=True))
        alpha = jnp.exp2(m_old - m_new)
        pr = jnp.exp2(sc - m_new)
        acc_scr[...] = alpha * acc_scr[...] + weighted_values(vst_scr, c * n_ct, pr, n_ct)
        sc_scr[...] = sc_next
        return m_new

    lax.fori_loop(0, ((qi + 1) * tq + ck - 1) // ck, slc_body, jnp.full((1, r_heads * tq), NEG_INF, F32))
    acc = acc_scr[...]
    o_slc = acc[0:dh] / acc[dh:dh + 1]

    n_wt = (WINDOW + tq) // tk
    kt0 = jnp.maximum(qi * (tq // tk) - WINDOW // tk, 0)
    w_start = pl.multiple_of(kt0 * tk, tk)
    key_w = w_start + lax.broadcasted_iota(I32, (n_wt * tk, tq), 0)
    t_pos_w = qi * tq + lax.broadcasted_iota(I32, (n_wt * tk, tq), 1)
    ok_w = jnp.where(key_w <= t_pos_w, jnp.where(key_w > t_pos_w - WINDOW, 1.0, 0.0), 0.0)
    sc_w = scores(kw_ref[0, pl.ds(w_start, n_wt * tk), :], ok_w)
    pr_w = jnp.exp2(sc_w - jnp.max(sc_w, axis=0, keepdims=True))
    acc_w = weighted_values(vwt_scr, kt0, pr_w, n_wt)
    o_win = acc_w[0:dh] / acc_w[dh:dh + 1]

    gate = jax.nn.sigmoid(gate_ref[0])
    for r in range(r_heads):
        cols = slice(r * tq, (r + 1) * tq)
        o = (gate[3 * r:3 * r + 1] * o_cmp[:, cols] + gate[3 * r + 1:3 * r + 2] * o_slc[:, cols]
             + gate[3 * r + 2:3 * r + 3] * o_win[:, cols])
        o_ref[:, r * dh:(r + 1) * dh] = o.T.astype(BF16)


def _nsa_attention(heads, kc_vc, gates):
    tq = ATT_TQ
    nq = SEQ // tq
    r = NSA_Q_PER_KV
    g_heads = NSA_KV_HEADS
    q_spec = pl.BlockSpec((r, tq, NSA_HEAD_DIM), lambda b, g, i: (g, b * nq + i, 0))

    def kv_spec(first_head):
        return pl.BlockSpec((1, SEQ, NSA_HEAD_DIM), lambda b, g, i: (first_head + g, b, 0))

    first = NSA_HEADS
    specs = [q_spec,
             kv_spec(first + 2 * g_heads), kv_spec(first + 3 * g_heads),
             kv_spec(first + 4 * g_heads), kv_spec(first + 5 * g_heads),
             pl.BlockSpec((1, 1, 1, SEQ // CMP_STRIDE, NSA_HEAD_DIM), lambda b, g, i: (0, g, b, 0, 0)),
             pl.BlockSpec((1, 1, 1, SEQ // CMP_STRIDE, NSA_HEAD_DIM), lambda b, g, i: (1, g, b, 0, 0)),
             pl.BlockSpec((1, 3 * r, tq), lambda b, g, i: (g, 0, b * nq + i))]
    vt_shape = (SEQ // ATT_TK, NSA_HEAD_DIM + ATT_ONES, ATT_TK)
    return pl.pallas_call(
        _nsa_attn_kernel, name="nsa_attn", grid=(BATCH, g_heads, nq),
        in_specs=specs,
        out_specs=pl.BlockSpec((tq, r * NSA_HEAD_DIM), lambda b, g, i: (b * nq + i, g)),
        out_shape=jax.ShapeDtypeStruct((TOKENS, NSA_QD), BF16),
        scratch_shapes=[pltpu.VMEM(vt_shape, BF16), pltpu.VMEM(vt_shape, BF16),
                        pltpu.VMEM((8 * N_SLC, tq), F32),
                        pltpu.VMEM((NSA_HEAD_DIM + ATT_ONES, r * tq), F32),
                        pltpu.VMEM((ATT_CHUNK_TILES * ATT_TK, r * tq), F32)],
        compiler_params=_cparams(("arbitrary", "arbitrary", "arbitrary")),
    )(heads, heads, heads, heads, heads, kc_vc, kc_vc, gates)


def _ssd_chunk_kernel(zx_ref, dtc_ref, dtr_ref, cw_ref, cb_ref, dtb_c_ref, alog_c_ref, dtb_r_ref, alog_r_ref,
                      dskip_ref, ng_ref, echan_ref, o_ref,
                      prev_scr, acum_r_scr, st_scr, fac_scr):
    chunk = pl.program_id(1)
    L = SSD_CHUNK
    W = SSD_GROUP_W
    hpg = SSD_HEADS_PER_GROUP
    n_xt = W // LANES
    x0 = SSD_D_INNER // LANES
    b0 = 2 * SSD_D_INNER // LANES
    c0 = b0 + SSD_GROUPS
    cb0 = SSD_D_INNER // LANES
    cc0 = cb0 + SSD_GROUPS

    @pl.when(chunk == 0)
    def _():
        prev_scr[...] = jnp.zeros(prev_scr.shape, F32)
        st_scr[...] = jnp.zeros(st_scr.shape, F32)

    dt_c = jax.nn.softplus(dtc_ref[...] + dtb_c_ref[...])
    adt_c = dt_c * (-jnp.exp(alog_c_ref[...]))
    dt_r = jax.nn.softplus(dtr_ref[...] + dtb_r_ref[...])
    adt_r = dt_r * (-jnp.exp(alog_r_ref[...]))
    row = lax.broadcasted_iota(I32, (L, L), 0)
    col = lax.broadcasted_iota(I32, (L, L), 1)
    causal = row >= col
    tri = jnp.where(causal, 1.0, 0.0).astype(BF16)
    tri_t = jnp.where(col >= row, 1.0, 0.0).astype(BF16)
    acum_c = _dot_split_rhs(tri, adt_c)
    acum_r_scr[...] = _dot_split_lhs(adt_r, tri_t)
    a_last = acum_c[L - 1:L, :]
    fac = jnp.concatenate([dt_c, jnp.exp(acum_c), jnp.exp(a_last - acum_c),
                           jnp.broadcast_to(jnp.exp(a_last), (SSD_FAC_PAD, LANES))], axis=0)
    fac_hi = fac.astype(BF16)
    fac_scr[0] = fac_hi
    fac_scr[1] = (fac - fac_hi.astype(F32)).astype(BF16)
    lane_w = lax.broadcasted_iota(I32, (L, LANES), 1)
    first_half = lane_w < SSD_HEAD_DIM

    def tiles(ref, first, n):
        return jnp.concatenate([ref[first + q] for q in range(n)], axis=1)

    def group_body(g, carry):
        e_chan = echan_ref[g]
        ex = _dot(fac_scr[0, 0:3 * L], e_chan)
        dt_x, ea_x, sd_x = ex[0:L], ex[L:2 * L], ex[2 * L:3 * L]
        cd_x = (_dot(fac_scr[0, 3 * L:3 * L + SSD_FAC_PAD], e_chan)
                + _dot(fac_scr[1, 3 * L:3 * L + SSD_FAC_PAD], e_chan))[0:1]

        def conv_silu(zx_first, conv_first, n):
            cur = tiles(zx_ref, zx_first, n)
            cur_f = cur.astype(F32)
            tail = tiles(prev_scr, conv_first, n)
            row8 = lax.broadcasted_iota(I32, (8, n * LANES), 0)
            w = tiles(cw_ref, conv_first, n)
            acc = jnp.broadcast_to(tiles(cb_ref, conv_first, n), (L, n * LANES))
            for k in range(SSD_CONV):
                back = SSD_CONV - 1 - k
                if back == 0:
                    xk = cur_f
                else:
                    rolled = pltpu.roll(cur_f, back, 0)
                    head = jnp.where(row8 < back, pltpu.roll(tail, back, 0), rolled[0:8])
                    xk = jnp.concatenate([head, rolled[8:]], axis=0)
                acc = acc + xk * w[k:k + 1, :]
            for q in range(n):
                prev_scr[conv_first + q] = cur_f[L - 8:L, q * LANES:(q + 1) * LANES]
            return jax.nn.silu(acc)

        xs = conv_silu(x0 + n_xt * g, n_xt * g, n_xt)
        bm = conv_silu(b0 + g, cb0 + g, 1)
        cm = conv_silu(c0 + g, cc0 + g, 1)

        xdt = xs * dt_x
        cb = jnp.where(causal, _dot_nt(cm.astype(BF16), bm.astype(BF16)), 0.0)
        y_parts = []
        for pair in range(hpg // 2):
            xd = xdt[:, pair * LANES:(pair + 1) * LANES]
            m_pair = []
            for sub in range(2):
                a_row = jnp.broadcast_to(acum_r_scr[pl.ds(g * hpg + 2 * pair + sub, 1), :], (L, L))
                seg = jnp.minimum(a_row.T - a_row, 0.0)
                m_pair.append((cb * jnp.exp(seg)).astype(BF16))
            x_pair = jnp.concatenate([jnp.where(first_half, xd, 0.0), jnp.where(first_half, 0.0, xd)], axis=0)
            y_parts.append(_dot(jnp.concatenate(m_pair, axis=1), x_pair.astype(BF16)))
        y_diag = jnp.concatenate(y_parts, axis=1)

        st = st_scr[g]
        y_off = _dot(cm.astype(BF16), st.astype(BF16)) * ea_x
        st_scr[g] = st * cd_x + _dot_tn(bm.astype(BF16), (xdt * sd_x).astype(BF16))

        y = y_diag + y_off + xs * tiles(dskip_ref, n_xt * g, n_xt)
        y = y * jax.nn.silu(tiles(zx_ref, n_xt * g, n_xt).astype(F32))
        y = y * lax.rsqrt(jnp.mean(y * y, axis=-1, keepdims=True) + NORM_EPS)
        y = y * tiles(ng_ref, n_xt * g, n_xt)
        for q in range(n_xt):
            o_ref[n_xt * g + q] = y[:, q * LANES:(q + 1) * LANES].astype(BF16)
        return carry

    lax.fori_loop(0, SSD_GROUPS, group_body, 0)


def _ssd_chunks(zx_tiles, dt_small, dt_small_t, conv_w, conv_b, dt_bias, a_log, d_skip, norm_g):
    L = SSD_CHUNK
    nc = SEQ // L
    n_zx = SSD_MAIN // LANES
    n_conv = SSD_CONV_CH // LANES
    n_inner = SSD_D_INNER // LANES
    hpg = SSD_HEADS_PER_GROUP

    def pad_heads(v):
        return jnp.zeros((LANES,), F32).at[:SSD_HEADS].set(v)

    dtb = pad_heads(dt_bias)
    alog = pad_heads(a_log)
    cw = conv_w.reshape(SSD_CONV, n_conv, LANES).transpose(1, 0, 2)
    cb = conv_b.reshape(n_conv, 1, LANES)
    d_chan = jnp.repeat(d_skip, SSD_HEAD_DIM).reshape(n_inner, 1, LANES)
    ng = norm_g.reshape(n_inner, 1, LANES)
    head =jnp.arange(LANES, dtype=I32)[None, :, None]
    grp = jnp.arange(SSD_GROUPS, dtype=I32)[:, None, None]
    e_chan = (head == grp * hpg + jnp.arange(SSD_GROUP_W, dtype=I32)[None, None, :] // SSD_HEAD_DIM).astype(BF16)
    row = lambda b, c: b * nc + c
    const3 = lambda b, c: (0, 0, 0)
    const2 = lambda b, c: (0, 0)
    in_specs = [
        pl.BlockSpec((n_zx, L, LANES), lambda b, c: (0, row(b, c), 0)),
        pl.BlockSpec((L, LANES), lambda b, c: (row(b, c), 0)),
        pl.BlockSpec((LANES, L), lambda b, c: (0, row(b, c))),
        pl.BlockSpec((n_conv, SSD_CONV, LANES), const3),
        pl.BlockSpec((n_conv, 1, LANES), const3),
        pl.BlockSpec((1, LANES), const2), pl.BlockSpec((1, LANES), const2),
        pl.BlockSpec((LANES, 1), const2), pl.BlockSpec((LANES, 1), const2),
        pl.BlockSpec((n_inner, 1, LANES), const3),
        pl.BlockSpec((n_inner, 1, LANES), const3),
        pl.BlockSpec((SSD_GROUPS, LANES, SSD_GROUP_W), const3),
    ]
    return pl.pallas_call(
        _ssd_chunk_kernel, name="ssd_chunks", grid=(BATCH, nc),
        in_specs=in_specs,
        out_specs=pl.BlockSpec((n_inner, L, LANES), lambda b, c: (0, row(b, c), 0)),
        out_shape=jax.ShapeDtypeStruct((n_inner, TOKENS, LANES), BF16),
        scratch_shapes=[pltpu.VMEM((n_conv, 8, LANES), F32),
                        pltpu.VMEM((LANES, L), F32),
                        pltpu.VMEM((SSD_GROUPS, SSD_D_STATE, SSD_GROUP_W), F32),
                        pltpu.VMEM((2, 3 * L + SSD_FAC_PAD, LANES), BF16)],
        compiler_params=_cparams(("arbitrary", "arbitrary")),
    )(zx_tiles, dt_small, dt_small_t, cw, cb, dtb.reshape(1, LANES), alog.reshape(1, LANES),
      dtb.reshape(LANES, 1), alog.reshape(LANES, 1), d_chan, ng, e_chan)


def _gather_rows(src_hbm, idx_ref, base, dst, sem, n_rows):
    def body(b, carry):
        for u in range(GATHER_UNROLL):
            r = b * GATHER_UNROLL + u
            tok = idx_ref[base + r]
            pltpu.make_async_copy(src_hbm.at[pl.ds(tok, 1), :], dst.at[pl.ds(r, 1), :], sem).start()
        return carry

    lax.fori_loop(0, n_rows // GATHER_UNROLL, body, 0)


def _moe_ffn_kernel(te_ref, pos_ref, nact_ref, pad_ref, wp_ref, x_hbm, g_ref, wg_hbm, wu_hbm, wd_hbm, y_ref,
                    tok_ref, buf, sem, wg_f, wu_f, wd_f, wsem, wg_bf, wu_bf, wd_bf, *, layer):
    i = pl.program_id(0)
    n_act = nact_ref[0]
    tm = MOE_TM
    slot = i % 2
    run_start = wp_ref[i] == 1
    w_slot = wp_ref[MOE_TILES + i]
    next_expert = wp_ref[2 * MOE_TILES + i]
    n_rows = pl.multiple_of(wp_ref[3 * MOE_TILES + i], GATHER_UNROLL)

    def gather(tile, s):
        _gather_rows(x_hbm, tok_ref, tile * tm, buf.at[s], sem.at[s], wp_ref[3 * MOE_TILES + tile])

    def weight_copies(e, s):
        return (pltpu.make_async_copy(wg_hbm.at[layer, e], wg_f.at[s], wsem.at[s, 0]),
                pltpu.make_async_copy(wu_hbm.at[layer, e], wu_f.at[s], wsem.at[s, 1]),
                pltpu.make_async_copy(wd_hbm.at[layer, e], wd_f.at[s], wsem.at[s, 2]))

    @pl.when(i == 0)
    def _():
        for c in weight_copies(te_ref[0], 0):
            c.start()
        buf[...] = jnp.zeros(buf.shape, F32)
        def clear(q, carry):
            tok_ref[q] = 0
            return carry

        def clear_padding(e, carry):
            lax.fori_loop(pad_ref[e], pad_ref[MOE_EXPERTS + e], clear, 0)
            return carry

        lax.fori_loop(0, MOE_EXPERTS, clear_padding, 0)
        for k in range(MOE_TOPK):
            def place(t, carry, k=k):
                tok_ref[pos_ref[k * TOKENS + t]] = t
                return carry

            lax.fori_loop(0, TOKENS, place, 0, unroll=GATHER_UNROLL)
        gather(0, 0)

    @pl.when(run_start)
    def _():
        for c in weight_copies(te_ref[i], w_slot):
            c.wait()

        @pl.when(next_expert >= 0)
        def _():
            for c in weight_copies(next_expert, 1 - w_slot):
                c.start()

        wg_bf[...] = wg_f[w_slot].astype(BF16)
        wu_bf[...] = wu_f[w_slot].astype(BF16)
        wd_bf[...] = wd_f[w_slot].astype(BF16)

    @pl.when(i >= n_act)
    def _():
        y_ref[...] = jnp.zeros(y_ref.shape, F32)

    @pl.when(i < n_act)
    def _():
        pltpu.make_async_copy(x_hbm.at[pl.ds(0, n_rows), :], buf.at[slot, pl.ds(0, n_rows), :],
                              sem.at[slot]).wait()

        @pl.when(i + 1 < n_act)
        def _():
            gather(i + 1, 1 - slot)

        h = _rms(buf[slot], g_ref[...]).astype(BF16)
        act = jax.nn.silu(_dot(h, wg_bf[...])) * _dot(h, wu_bf[...])
        y_ref[...] = _dot(act.astype(BF16), wd_bf[...])


def _moe_ffn(x, g, w_gate, w_up, w_down, layer, tile_expert, pos_kmajor, n_active, pad_rows, weight_plan):
    any_spec = pl.BlockSpec(memory_space=pl.ANY)
    grid_spec = pltpu.PrefetchScalarGridSpec(
        num_scalar_prefetch=5, grid=(MOE_TILES,),
        in_specs=[any_spec, pl.BlockSpec((1, D_MODEL), lambda i, *_: (0, 0)), any_spec, any_spec, any_spec],
        out_specs=pl.BlockSpec((MOE_TM, D_MODEL), lambda i, *_: (i, 0)),
        scratch_shapes=[pltpu.SMEM((MOE_ROWS,), I32),
                        pltpu.VMEM((2, MOE_TM, D_MODEL), F32), pltpu.SemaphoreType.DMA((2,)),
                        pltpu.VMEM((2, D_MODEL, MOE_D_FF), F32), pltpu.VMEM((2, D_MODEL, MOE_D_FF), F32),
                        pltpu.VMEM((2, MOE_D_FF, D_MODEL), F32), pltpu.SemaphoreType.DMA((2, 3)),
                        pltpu.VMEM((D_MODEL, MOE_D_FF), BF16), pltpu.VMEM((D_MODEL, MOE_D_FF), BF16),
                        pltpu.VMEM((MOE_D_FF, D_MODEL), BF16)])
    return pl.pallas_call(
        functools.partial(_moe_ffn_kernel, layer=layer), name="moe_ffn", grid_spec=grid_spec,
        out_shape=jax.ShapeDtypeStruct((MOE_ROWS, D_MODEL), F32),
        compiler_params=_cparams(("arbitrary",)),
    )(tile_expert, pos_kmajor, n_active, pad_rows, weight_plan, x, g.reshape(1, D_MODEL), w_gate, w_up, w_down)


def _moe_combine_kernel(pos_ref, x_ref, cw_ref, g_ref, ws_ref, y_hbm, *refs, final_norm):
    if final_norm:
        o_ref, buf, sem = refs
    else:
        o_ref, hn_ref, small_ref, small_t_ref, buf, sem = refs
    i = pl.program_id(0)
    n = pl.num_programs(0)
    tm = CMB_TM
    slot = i % 2

    def issue(tile, s):
        for k in range(MOE_TOPK):
            _gather_rows(y_hbm, pos_ref, (k * (TOKENS // tm) + tile) * tm, buf.at[s, k], sem.at[s], tm)

    @pl.when(i == 0)
    def _():
        issue(0, 0)

    @pl.when(i + 1 < n)
    def _():
        issue(i + 1, 1 - slot)

    for k in range(MOE_TOPK):
        pltpu.make_async_copy(y_hbm.at[pl.ds(0, tm), :], buf.at[slot, k], sem.at[slot]).wait()
    cw = cw_ref[...]
    out = x_ref[...] + cw[:, 0:1] * buf[slot, 0] + cw[:, 1:2] * buf[slot, 1]
    y = _rms(out, g_ref[...])
    if final_norm:
        o_ref[...] = y
    else:
        o_ref[...] = out
        hn_ref[...] = y.astype(BF16)
        small = _dot_x3(y, ws_ref[...])
        small_ref[...] = small
        small_t_ref[...] = small.T


def _moe_combine(x, cw, y_sorted, pos_kmajor, g_norm, w_small, final_norm):
    ws = jnp.zeros((D_MODEL, LANES), F32)
    if not final_norm:
        ws = ws.at[:, :w_small.shape[1]].set(w_small)
    row = pl.BlockSpec((CMB_TM, D_MODEL), lambda i, pos: (i, 0))
    small = pl.BlockSpec((CMB_TM, LANES), lambda i, pos: (i, 0))
    x_shape = jax.ShapeDtypeStruct((TOKENS, D_MODEL), F32)
    if final_norm:
        out_specs, out_shape = row, x_shape
    else:
        out_specs = [row, row, small, pl.BlockSpec((LANES, CMB_TM), lambda i, pos: (0, i))]
        out_shape = [x_shape, jax.ShapeDtypeStruct((TOKENS, D_MODEL), BF16),
                     jax.ShapeDtypeStruct((TOKENS, LANES), F32), jax.ShapeDtypeStruct((LANES, TOKENS), F32)]
    grid_spec = pltpu.PrefetchScalarGridSpec(
        num_scalar_prefetch=1, grid=(TOKENS // CMB_TM,),
        in_specs=[row, small,
                  pl.BlockSpec((1, D_MODEL), lambda i, pos: (0, 0)),
                  pl.BlockSpec((D_MODEL, LANES), lambda i, pos: (0, 0)),
                  pl.BlockSpec(memory_space=pl.ANY)],
        out_specs=out_specs,
        scratch_shapes=[pltpu.VMEM((2, MOE_TOPK, CMB_TM, D_MODEL), F32), pltpu.SemaphoreType.DMA((2,))])
    return pl.pallas_call(
        functools.partial(_moe_combine_kernel, final_norm=final_norm), name="moe_combine", grid_spec=grid_spec,
        out_shape=out_shape,
        compiler_params=_cparams(("arbitrary",)),
    )(pos_kmajor, x, cw, g_norm.reshape(1, D_MODEL), ws, y_sorted)


def _moe_plan(eid):
    e = eid[:, :MOE_TOPK].reshape(-1)
    onehot = (e[:, None] == jnp.arange(MOE_EXPERTS, dtype=I32)[None, :]).astype(I32)
    csum = jnp.cumsum(onehot, axis=0)
    counts = csum[-1]
    padded = ((counts + MOE_TM - 1) // MOE_TM) * MOE_TM
    g_end = jnp.cumsum(padded)
    g_start = g_end - padded
    pos = jnp.sum(onehot * (g_start[None, :] + csum - 1), axis=1)
    n_active = (g_end[-1] // MOE_TM).astype(I32)
    tile_start = jnp.arange(MOE_TILES, dtype=I32) * MOE_TM
    te = jnp.sum((g_end[None, :] <= tile_start[:, None]).astype(I32), axis=1)
    last = jnp.max(jnp.where(counts > 0, jnp.arange(MOE_EXPERTS, dtype=I32), 0))
    tile_expert = jnp.minimum(te, last)
    pos_kmajor = pos.reshape(TOKENS, MOE_TOPK).T.reshape(-1)
    pad_rows = jnp.concatenate([g_start + counts, g_end])
    experts = jnp.arange(MOE_EXPERTS, dtype=I32)
    run_start = jnp.concatenate([jnp.ones((1,), I32), (tile_expert[1:] != tile_expert[:-1]).astype(I32)])
    w_slot = (jnp.cumsum(run_start) - 1) % 2
    later = (experts[None, :] > experts[:, None]) & (counts > 0)[None, :]
    next_nonempty = jnp.min(jnp.where(later, experts[None, :], MOE_EXPERTS), axis=1)
    next_nonempty = jnp.where(next_nonempty == MOE_EXPERTS, -1, next_nonempty)
    real_rows = jnp.clip((g_start + counts)[tile_expert] - tile_start, 0, MOE_TM)
    real_rows = jnp.where(tile_start < g_end[-1], real_rows, 0)
    gather_rows = jnp.minimum(-(-real_rows // GATHER_UNROLL) * GATHER_UNROLL, MOE_TM)
    weight_plan = jnp.concatenate([run_start, w_slot, next_nonempty[tile_expert], gather_rows]).astype(I32)
    return tile_expert, n_active.reshape(1), pos_kmajor, pad_rows, weight_plan


def _hier_moe_add(x, eid, cw, ln_g, w_gate, w_up, w_down, layer, g_norm, w_small, final_norm):
    tile_expert, n_active, pos_kmajor, pad_rows, weight_plan = _moe_plan(eid)
    y_sorted = _moe_ffn(x, ln_g, w_gate, w_up, w_down, layer, tile_expert, pos_kmajor, n_active, pad_rows,
                        weight_plan)
    return _moe_combine(x, cw, y_sorted, pos_kmajor, g_norm, w_small, final_norm)


def _rope_tables():
    pos = jnp.arange(SEQ, dtype=F32)
    inv = 1.0 / (ROPE_THETA ** (jnp.arange(0, NSA_HEAD_DIM, 2, dtype=F32) / NSA_HEAD_DIM))
    ang = pos[:, None] * inv[None, :]
    cos, sin = jnp.cos(ang), jnp.sin(ang)
    return jnp.concatenate([cos, cos], axis=1), jnp.concatenate([-sin, sin], axis=1)


def _nsa_mixer(hn, g_lin_t, w_in, cmp_pe, cmp_w1, cmp_w2):
    w_in_t = w_in.T
    cos_full, sin_signed = _rope_tables()
    heads = _nsa_inproj(hn, w_in_t, cos_full, sin_signed)
    first_c = NSA_HEADS
    kc_vc = _compress(heads, first_c, cmp_pe, cmp_w1, cmp_w2)
    gates_t = g_lin_t[:NSA_GATES].reshape(NSA_KV_HEADS, 3 * NSA_Q_PER_KV, TOKENS)
    return _nsa_attention(heads, kc_vc, gates_t)


def _ssd_mixer(hn, dt_small, dt_small_t, w_in, conv_w, conv_b, dt_bias, a_log, d_skip, norm_g):
    zx_tiles = _matmul_tiles(hn, w_in.T, SSD_MAIN, 1024)
    return _ssd_chunks(zx_tiles, dt_small, dt_small_t, conv_w, conv_b, dt_bias, a_log, d_skip, norm_g)


def kernel(x, ln_mix, ln_ffn, ln_final, nsa_w_in, nsa_cmp_pe, nsa_cmp_w1, nsa_cmp_w2, nsa_w_out,
           ssd_w_in, ssd_conv_w, ssd_conv_b, ssd_dt_bias, ssd_a_log, ssd_d, ssd_norm, ssd_w_out,
           moe_w_group, moe_b_group, moe_w_expert, moe_b_expert, moe_w_gate, moe_w_up, moe_w_down):
    def small_weight(i):
        if i % N_MIXERS == 0:
            return nsa_w_in[i // N_MIXERS].T[NSA_MAIN:].T
        return ssd_w_in[i // N_MIXERS].T[SSD_MAIN:].T

    h = x.reshape(TOKENS, D_MODEL)
    hn, small, small_t = _norm_small(h, ln_mix[0], small_weight(0))
    for i in range(DEPTH):
        j = i // N_MIXERS
        last = i == DEPTH - 1
        if i % N_MIXERS == 0:
            mix = _nsa_mixer(hn, small_t, nsa_w_in[j], nsa_cmp_pe[j], nsa_cmp_w1[j], nsa_cmp_w2[j])
            w_out = nsa_w_out[j]
        else:
            mix = _ssd_mixer(hn, small, small_t, ssd_w_in[j], ssd_conv_w[j], ssd_conv_b[j], ssd_dt_bias[j],
                             ssd_a_log[j], ssd_d[j], ssd_norm[j])
            w_out = ssd_w_out[j]
        h, eid, cw = _outproj_route(mix, w_out, h, ln_ffn[i], moe_w_group[i], moe_b_group[i],
                                    moe_w_expert[i], moe_b_expert[i])
        out = _hier_moe_add(h, eid, cw, ln_ffn[i], moe_w_gate, moe_w_up, moe_w_down, i,
                            ln_final if last else ln_mix[i + 1], None if last else small_weight(i + 1), last)
        if last:
            h = out
        else:
            h, hn, small, small_t = out
    return h.reshape(BATCH, SEQ, D_MODEL)
```

```python
import functools

import jax
import jax.numpy as jnp
from jax import lax
from jax.experimental import pallas as pl
from jax.experimental.pallas import tpu as pltpu

F32 = jnp.float32
BF16 = jnp.bfloat16
I32 = jnp.int32

D_MODEL = 2048
BATCH = 4
SEQ = 2048
TOKENS = BATCH * SEQ
DEPTH = 2
N_MIXERS = 2
NORM_EPS = 1e-6
NEG_INF = -1e30
LOG2_E = 1.4426950408889634
ROPE_THETA = 10000.0

NSA_HEADS = 16
NSA_KV_HEADS = 4
NSA_HEAD_DIM = D_MODEL // NSA_HEADS
NSA_Q_PER_KV = NSA_HEADS // NSA_KV_HEADS
CMP_BLOCK = 32
CMP_STRIDE = 16
CMP_HIDDEN = 256
N_CMP = (SEQ - CMP_BLOCK) // CMP_STRIDE + 1
SLC_BLOCK = 64
SLC_TOPK = 16
SLC_LOCAL = 2
SLC_FORCE = 1e4
N_SLC = SEQ // SLC_BLOCK
WINDOW = 512
NSA_QD = NSA_HEADS * NSA_HEAD_DIM
NSA_KVD = NSA_KV_HEADS * NSA_HEAD_DIM
NSA_MAIN = NSA_QD + 6 * NSA_KVD
NSA_GATES = 3 * NSA_HEADS

SSD_D_INNER = 2 * D_MODEL
SSD_HEAD_DIM = 64
SSD_HEADS = SSD_D_INNER // SSD_HEAD_DIM
SSD_GROUPS = 8
SSD_HEADS_PER_GROUP = SSD_HEADS // SSD_GROUPS
SSD_D_STATE = 128
SSD_CONV = 4
SSD_CHUNK = 128
SSD_GROUP_W = SSD_D_INNER // SSD_GROUPS
SSD_BC = SSD_GROUPS * SSD_D_STATE
SSD_CONV_CH = SSD_D_INNER + 2 * SSD_BC
SSD_MAIN = SSD_D_INNER + SSD_CONV_CH
SSD_FAC_PAD = 16

MOE_GROUPS = 4
MOE_EPG = 8
MOE_EXPERTS = MOE_GROUPS * MOE_EPG
MOE_TOPK = 2
MOE_D_FF = 512

LANES = 128
VMEM_LIMIT = 56 * 1024 * 1024

NORM_TM = 512
MM_TM = 1024
OUT_TM = 512
ATT_TQ = 256
ATT_TK = 256
ATT_CHUNK_TILES = 2
ATT_ONES = 16
MOE_TM = 256
MOE_TILES = (TOKENS * MOE_TOPK) // MOE_TM + MOE_EXPERTS
MOE_ROWS = MOE_TILES * MOE_TM
CMB_TM = 512
GATHER_UNROLL = 8


def _cparams(sem):
    return pltpu.CompilerParams(dimension_semantics=sem, vmem_limit_bytes=VMEM_LIMIT)


def _split3(x):
    hi = x.astype(BF16)
    r1 = x - hi.astype(F32)
    mid = r1.astype(BF16)
    lo = (r1 - mid.astype(F32)).astype(BF16)
    return hi, mid, lo


def _dot(a, b):
    return jnp.dot(a, b, preferred_element_type=F32)


def _dot_nt(a, b):
    return lax.dot_general(a, b, (((1,), (1,)), ((), ())), preferred_element_type=F32)


def _dot_tn(a, b):
    return lax.dot_general(a, b, (((0,), (0,)), ((), ())), preferred_element_type=F32)


def _dot_split_lhs(x, m_bf16):
    hi, mid, lo = _split3(x)
    return _dot(hi, m_bf16) + _dot(mid, m_bf16) + _dot(lo, m_bf16)


def _dot_split_rhs(m_bf16, x):
    hi, mid, lo = _split3(x)
    return _dot(m_bf16, hi) + _dot(m_bf16, mid) + _dot(m_bf16, lo)


def _dot_x3(a, w):
    a_hi = a.astype(BF16)
    a_lo = (a - a_hi.astype(F32)).astype(BF16)
    w_hi = w.astype(BF16)
    w_lo = (w - w_hi.astype(F32)).astype(BF16)
    n = w.shape[1]
    both = _dot(a_hi, jnp.concatenate([w_hi, w_lo], axis=1))
    return both[:, :n] + both[:, n:] + _dot(a_lo, w_hi)


def _rms(x, g):
    y = x * lax.rsqrt(jnp.mean(x * x, axis=-1, keepdims=True) + NORM_EPS)
    return y * g


def _norm_small_kernel(x_ref, g_ref, ws_ref, hn_ref, small_ref, small_t_ref):
    y = _rms(x_ref[...], g_ref[...])
    hn_ref[...] = y.astype(BF16)
    small = _dot_x3(y, ws_ref[...])
    small_ref[...] = small
    small_t_ref[...] = small.T


def _norm_small(x, g, w_small):
    n = w_small.shape[1]
    ws = jnp.zeros((D_MODEL, LANES), F32).at[:, :n].set(w_small)
    return pl.pallas_call(
        _norm_small_kernel, name="norm_small", grid=(TOKENS // NORM_TM,),
        in_specs=[pl.BlockSpec((NORM_TM, D_MODEL), lambda i: (i, 0)),
                  pl.BlockSpec((1, D_MODEL), lambda i: (0, 0)),
                  pl.BlockSpec((D_MODEL, LANES), lambda i: (0, 0))],
        out_specs=[pl.BlockSpec((NORM_TM, D_MODEL), lambda i: (i, 0)),
                   pl.BlockSpec((NORM_TM, LANES), lambda i: (i, 0)),
                   pl.BlockSpec((LANES, NORM_TM), lambda i: (0, i))],
        out_shape=[jax.ShapeDtypeStruct((TOKENS, D_MODEL), BF16),
                   jax.ShapeDtypeStruct((TOKENS, LANES), F32),
                   jax.ShapeDtypeStruct((LANES, TOKENS), F32)],
        compiler_params=_cparams(("parallel",)),
    )(x, g.reshape(1, D_MODEL), ws)


def _route(x, g, ws, bias):
    y = _rms(x, g)
    logits = _dot_x3(y, ws) + bias
    lane = lax.broadcasted_iota(I32, logits.shape, 1)
    big = jnp.int32(LANES)
    neg = -jnp.inf
    gl = jnp.where(lane < MOE_GROUPS, logits, neg)
    gmax = jnp.max(gl, axis=-1, keepdims=True)
    gsum = jnp.sum(jnp.exp(gl - gmax), axis=-1, keepdims=True)
    g_w = 1.0 / gsum
    g_sel = jnp.min(jnp.where(gl == gmax, lane, big), axis=-1, keepdims=True)
    lo = MOE_GROUPS + g_sel * MOE_EPG
    el = jnp.where((lane >= lo) & (lane < lo + MOE_EPG), logits, neg)
    v1 = jnp.max(el, axis=-1, keepdims=True)
    i1 = jnp.min(jnp.where(el == v1, lane, big), axis=-1, keepdims=True)
    el2 = jnp.where(lane == i1, neg, el)
    v2 = jnp.max(el2, axis=-1, keepdims=True)
    i2 = jnp.min(jnp.where(el2 == v2, lane, big), axis=-1, keepdims=True)
    e2 = jnp.exp(v2 - v1)
    den = 1.0 + e2
    w1 = (1.0 / den) * g_w
    w2 = (e2 / den) * g_w
    eid = jnp.where(lane == 0, i1 - MOE_GROUPS, jnp.where(lane == 1, i2 - MOE_GROUPS, 0))
    cw = jnp.where(lane == 0, w1, jnp.where(lane == 1, w2, 0.0))
    return eid, cw


def _inproj_heads_kernel(a_ref, w_ref, cos_ref, sin_ref, o_ref, wbf_ref):
    j = pl.program_id(0)

    @pl.when(pl.program_id(1) == 0)
    def _():
        wbf_ref[...] = w_ref[...].astype(BF16)

    acc = _dot_nt(a_ref[...], wbf_ref[...])
    q_tiles = NSA_QD // acc.shape[1]
    c = cos_ref[...]
    s = sin_ref[...]

    def head(h):
        return acc[:, h * NSA_HEAD_DIM:(h + 1) * NSA_HEAD_DIM]

    def rotary(xh):
        return (xh * c + pltpu.roll(xh, NSA_HEAD_DIM // 2, 1) * s).astype(BF16)

    for h in range(NSA_KV_HEADS):
        o_ref[h] = rotary(head(h))

    @pl.when(j < q_tiles)
    def _():
        for h in range(NSA_KV_HEADS, 2 * NSA_KV_HEADS):
            o_ref[h] = rotary(head(h))

    @pl.when(j >= q_tiles)
    def _():
        for h in range(NSA_KV_HEADS, 2 * NSA_KV_HEADS):
            o_ref[h] = head(h).astype(BF16)


def _nsa_inproj(hn, w_in_t, cos_full, sin_signed):
    tn = 2 * NSA_KVD
    n_heads_tile = tn // NSA_HEAD_DIM
    s_tiles = SEQ // MM_TM
    return pl.pallas_call(
        _inproj_heads_kernel, name="nsa_inproj", grid=(NSA_MAIN // tn, TOKENS // MM_TM),
        in_specs=[pl.BlockSpec((MM_TM, D_MODEL), lambda j, i: (i, 0)),
                  pl.BlockSpec((tn, D_MODEL), lambda j, i: (j, 0)),
                  pl.BlockSpec((MM_TM, NSA_HEAD_DIM), lambda j, i: (i % s_tiles, 0)),
                  pl.BlockSpec((MM_TM, NSA_HEAD_DIM), lambda j, i: (i % s_tiles, 0))],
        out_specs=pl.BlockSpec((n_heads_tile, MM_TM, NSA_HEAD_DIM), lambda j, i: (j, i, 0)),
        out_shape=jax.ShapeDtypeStruct((NSA_MAIN // NSA_HEAD_DIM, TOKENS, NSA_HEAD_DIM), BF16),
        scratch_shapes=[pltpu.VMEM((tn, D_MODEL), BF16)],
        compiler_params=_cparams(("arbitrary", "arbitrary")),
    )(hn, w_in_t, cos_full, sin_signed)


def _matmul_tiles_kernel(a_ref, w_ref, o_ref, wbf_ref):
    @pl.when(pl.program_id(1) == 0)
    def _():
        wbf_ref[...] = w_ref[...].astype(BF16)

    acc = _dot_nt(a_ref[...], wbf_ref[...])
    for t in range(acc.shape[1] // LANES):
        o_ref[t] = acc[:, t * LANES:(t + 1) * LANES].astype(o_ref.dtype)


def _matmul_tiles(a, w_t, n_cols, tn):
    k = a.shape[1]
    return pl.pallas_call(
        _matmul_tiles_kernel, name="matmul_tiles", grid=(n_cols // tn, TOKENS // MM_TM),
        in_specs=[pl.BlockSpec((MM_TM, k), lambda j, i: (i, 0)),
                  pl.BlockSpec((tn, k), lambda j, i: (j, 0))],
        out_specs=pl.BlockSpec((tn // LANES, MM_TM, LANES), lambda j, i: (j, i, 0)),
        out_shape=jax.ShapeDtypeStruct((n_cols // LANES, TOKENS, LANES), BF16),
        scratch_shapes=[pltpu.VMEM((tn, k), BF16)],
        compiler_params=_cparams(("arbitrary", "arbitrary")),
    )(a, w_t)


def _outproj_route_kernel(a_ref, w_ref, r_ref, g_ref, ws_ref, b_ref, o_ref, eid_ref, cw_ref):
    if len(a_ref.shape) == 3:
        a = jnp.concatenate([a_ref[t] for t in range(a_ref.shape[0])], axis=1)
    else:
        a = a_ref[...]
    x_new = r_ref[...] + _dot(a, w_ref[...])
    o_ref[...] = x_new
    eid_ref[...], cw_ref[...] = _route(x_new, g_ref[...], ws_ref[...], b_ref[...])


def _outproj_route(a, w, resid, ln_ffn_g, w_group, b_group, w_expert, b_expert):
    tiled = a.ndim == 3
    k, n = w.shape
    n_r = MOE_GROUPS + MOE_EXPERTS
    ws = jnp.zeros((D_MODEL, LANES), F32).at[:, :n_r].set(jnp.concatenate([w_group, w_expert], axis=1))
    bs = jnp.zeros((1, LANES), F32).at[0, :n_r].set(jnp.concatenate([b_group, b_expert]))
    a_spec = (pl.BlockSpec((k // LANES, OUT_TM, LANES), lambda i: (0, i, 0)) if tiled
              else pl.BlockSpec((OUT_TM, k), lambda i: (i, 0)))
    small = pl.BlockSpec((OUT_TM, LANES), lambda i: (i, 0))
    return pl.pallas_call(
        _outproj_route_kernel, name="outproj_route", grid=(TOKENS // OUT_TM,),
        in_specs=[a_spec,
                  pl.BlockSpec((k, n), lambda i: (0, 0), pipeline_mode=pl.Buffered(1)),
                  pl.BlockSpec((OUT_TM, n), lambda i: (i, 0)),
                  pl.BlockSpec((1, D_MODEL), lambda i: (0, 0)),
                  pl.BlockSpec((D_MODEL, LANES), lambda i: (0, 0)),
                  pl.BlockSpec((1, LANES), lambda i: (0, 0))],
        out_specs=[pl.BlockSpec((OUT_TM, n), lambda i: (i, 0)), small, small],
        out_shape=[jax.ShapeDtypeStruct((TOKENS, n), F32),
                   jax.ShapeDtypeStruct((TOKENS, LANES), I32),
                   jax.ShapeDtypeStruct((TOKENS, LANES), F32)],
        compiler_params=_cparams(("parallel",)),
    )(a, w.astype(BF16), resid, ln_ffn_g.reshape(1, D_MODEL), ws, bs)


def _compress_kernel(x_ref, pe_ref, w1_ref, w2_ref, o_ref, tok_scr):
    half = CMP_STRIDE * NSA_HEAD_DIM
    n_chunk = SEQ // CMP_STRIDE
    tok_scr[...] = x_ref[0].astype(F32)
    x = jnp.concatenate([tok_scr[pl.ds(p, n_chunk, stride=CMP_STRIDE), :] for p in range(CMP_STRIDE)],
                        axis=1).astype(BF16)
    w1 = w1_ref[0].astype(BF16)
    top = _dot(x, w1[:half])
    bot = _dot(x, w1[half:])
    pe = jnp.broadcast_to(pe_ref[0], (8, 2 * half)).astype(BF16)
    pe_bias = _dot(pe, w1)[0:1]
    hid = top + pltpu.roll(bot, bot.shape[0] - 1, 0) + pe_bias
    act = jax.nn.gelu(hid)
    o_ref[0, 0, 0] = _dot(act.astype(BF16), w2_ref[0].astype(BF16))


def _compress(heads, first_head, pe, w1, w2):
    n_chunk = SEQ // CMP_STRIDE
    feat = CMP_STRIDE * NSA_HEAD_DIM
    return pl.pallas_call(
        _compress_kernel, name="nsa_compress", grid=(2, NSA_KV_HEADS, BATCH),
        in_specs=[pl.BlockSpec((1, SEQ, NSA_HEAD_DIM), lambda a, g, b: (first_head + a * NSA_KV_HEADS + g, b, 0)),
                  pl.BlockSpec((1, 1, 2 * feat), lambda a, g, b: (a, 0, 0)),
                  pl.BlockSpec((1, 2 * feat, CMP_HIDDEN), lambda a, g, b: (a, 0, 0)),
                  pl.BlockSpec((1, CMP_HIDDEN, NSA_HEAD_DIM), lambda a, g, b: (a, 0, 0))],
        out_specs=pl.BlockSpec((1, 1, 1, n_chunk, NSA_HEAD_DIM), lambda a, g, b: (a, g, b, 0, 0)),
        out_shape=jax.ShapeDtypeStruct((2, NSA_KV_HEADS, BATCH, n_chunk, NSA_HEAD_DIM), F32),
        scratch_shapes=[pltpu.VMEM((SEQ, NSA_HEAD_DIM), F32)],
        compiler_params=_cparams(("parallel", "parallel", "parallel")),
    )(heads, pe.reshape(2, 1, 2 * feat), w1, w2)


def _nsa_attn_kernel(q_ref, ks_ref, vs_ref, kw_ref, vw_ref, kc_ref, vc_ref, gate_ref, o_ref,
                     vst_scr, vwt_scr, sel_scr, acc_scr, sc_scr):
    qi = pl.program_id(2)
    tq = ATT_TQ
    tk = ATT_TK
    dh = NSA_HEAD_DIM
    r_heads = NSA_Q_PER_KV
    n_kt = SEQ // tk
    scale = dh ** -0.5 * LOG2_E

    @pl.when(qi == 0)
    def _():
        ones = jnp.ones((ATT_ONES, tk), BF16)
        for kt in range(n_kt):
            rows = slice(kt * tk, (kt + 1) * tk)
            vst_scr[kt, 0:dh, :] = vs_ref[0, rows, :].astype(F32).T.astype(BF16)
            vwt_scr[kt, 0:dh, :] = vw_ref[0, rows, :].astype(F32).T.astype(BF16)
            vst_scr[kt, dh:dh + ATT_ONES, :] = ones
            vwt_scr[kt, dh:dh + ATT_ONES, :] = ones

    q_s = jnp.concatenate([q_ref[r].astype(F32).T for r in range(r_heads)], axis=1) * scale
    q_hi = q_s.astype(BF16)
    q_t = jnp.concatenate([q_hi, (q_s - q_hi.astype(F32)).astype(BF16)], axis=0)

    def qk(k):
        return _dot(jnp.concatenate([k, k], axis=1), q_t)
    n_cp = SEQ // CMP_STRIDE
    sub = lax.broadcasted_iota(I32, (n_cp, tq), 0)
    t_pos = qi * tq + lax.broadcasted_iota(I32, (n_cp, tq), 1)

    def tile4(a):
        return jnp.concatenate([a] * r_heads, axis=1)

    kc = kc_ref[0, 0, 0].astype(BF16)
    vc = vc_ref[0, 0, 0].astype(BF16)
    ok_c = jnp.where(sub * CMP_STRIDE + CMP_BLOCK - 1 <= t_pos, jnp.where(sub < N_CMP, 1.0, 0.0), 0.0)
    ok_c4 = tile4(ok_c)
    s_c = qk(kc) + (ok_c4 - 1.0) * (-NEG_INF)
    e_c = jnp.exp2(s_c - jnp.max(s_c, axis=0, keepdims=True))
    p_c = (e_c / jnp.sum(e_c, axis=0, keepdims=True)) * ok_c4
    o_cmp = _dot_tn(vc, p_c.astype(BF16))
    p_sum = p_c[:, 0:tq]
    for r in range(1, r_heads):
        p_sum = p_sum + p_c[:, r * tq:(r + 1) * tq]

    blk_row = lax.broadcasted_iota(I32, (LANES, LANES), 0)
    cmp_col = lax.broadcasted_iota(I32, (LANES, LANES), 1)
    s_start = blk_row * SLC_BLOCK
    c_start = cmp_col * CMP_STRIDE
    ov_t = jnp.maximum(jnp.minimum(c_start + CMP_BLOCK, s_start + SLC_BLOCK)
                       - jnp.maximum(c_start, s_start), 0).astype(F32) / CMP_BLOCK
    ov_t = jnp.where(blk_row < N_SLC, ov_t, 0.0).astype(BF16)
    imp = _dot_split_rhs(ov_t, p_sum)[0:N_SLC]
    j_blk = lax.broadcasted_iota(I32, (N_SLC, tq), 0)
    dist = (qi * tq + lax.broadcasted_iota(I32, (N_SLC, tq), 1)) // SLC_BLOCK - j_blk
    imp = jnp.where(j_blk == 0, SLC_FORCE, jnp.where(dist < 0, imp, jnp.where(dist < SLC_LOCAL, SLC_FORCE, imp)))
    imp = jnp.where(dist >= 0, imp, -jnp.inf)
    cnt = jnp.zeros((N_SLC, tq), I32)
    for k in range(N_SLC):
        row_k = imp[k:k + 1, :]
        tie = jnp.where(j_blk > k, 1, 0)
        cnt = cnt + jnp.where(row_k > imp, 1, jnp.where(row_k == imp, tie, 0))
    sel = jnp.where(cnt < min(SLC_TOPK, N_SLC), 1.0, 0.0)
    for j in range(N_SLC):
        sel_scr[8 * j:8 * j + 8, :] = jnp.broadcast_to(sel[j:j + 1, :], (8, tq))

    def scores(k, ok):
        return qk(k) + tile4((ok - 1.0) * (-NEG_INF))

    def weighted_values(vt_scr, kt0, pr, n_tiles):
        out = None
        for u in range(n_tiles):
            term = _dot(vt_scr[kt0 + u], pr[u * tk:(u + 1) * tk].astype(BF16))
            out = term if out is None else out + term
        return out

    n_ct = ATT_CHUNK_TILES
    ck = n_ct * tk
    sub_c = lax.broadcasted_iota(I32, (ck, tq), 0)
    t_pos_c = qi * tq + lax.broadcasted_iota(I32, (ck, tq), 1)
    blocks_per_chunk = ck // SLC_BLOCK
    acc_scr[...] = jnp.zeros(acc_scr.shape, F32)

    def chunk_scores(c):
        c = jnp.minimum(c, SEQ // ck - 1)
        start = pl.multiple_of(c * ck, ck)
        k = ks_ref[0, pl.ds(start, ck), :]
        rows8 = sel_scr[pl.ds(pl.multiple_of(c * (8 * blocks_per_chunk), 8 * blocks_per_chunk),
                              8 * blocks_per_chunk), :]
        picked = jnp.concatenate(
            [rows8[8 * u:8 * u + 8] for u in range(blocks_per_chunk) for _ in range(SLC_BLOCK // 8)], axis=0)
        return scores(k, jnp.where(start + sub_c <= t_pos_c, picked, 0.0))

    sc_scr[...] = chunk_scores(0)

    def slc_body(c, m_old):
        sc = sc_scr[...]
        sc_next = chunk_scores(c + 1)
        m_new = jnp.maximum(m_old, jnp.max(sc, axis=0, keepdims=True))
        alpha = jnp.exp2(m_old - m_new)
        pr = jnp.exp2(sc - m_new)
        acc_scr[...] = alpha * acc_scr[...] + weighted_values(vst_scr, c * n_ct, pr, n_ct)
        sc_scr[...] = sc_next
        return m_new

    lax.fori_loop(0, ((qi + 1) * tq + ck - 1) // ck, slc_body, jnp.full((1, r_heads * tq), NEG_INF, F32))
    acc = acc_scr[...]
    o_slc = acc[0:dh] / acc[dh:dh + 1]

    n_wt = (WINDOW + tq) // tk
    kt0 = jnp.maximum(qi * (tq // tk) - WINDOW // tk, 0)
    w_start = pl.multiple_of(kt0 * tk, tk)
    key_w = w_start + lax.broadcasted_iota(I32, (n_wt * tk, tq), 0)
    t_pos_w = qi * tq + lax.broadcasted_iota(I32, (n_wt * tk, tq), 1)
    ok_w = jnp.where(key_w <= t_pos_w, jnp.where(key_w > t_pos_w - WINDOW, 1.0, 0.0), 0.0)
    sc_w = scores(kw_ref[0, pl.ds(w_start, n_wt * tk), :], ok_w)
    pr_w = jnp.exp2(sc_w - jnp.max(sc_w, axis=0, keepdims=True))
    acc_w = weighted_values(vwt_scr, kt0, pr_w, n_wt)
    o_win = acc_w[0:dh] / acc_w[dh:dh + 1]

    gate = jax.nn.sigmoid(gate_ref[0])
    for r in range(r_heads):
        cols = slice(r * tq, (r + 1) * tq)
        o = (gate[3 * r:3 * r + 1] * o_cmp[:, cols] + gate[3 * r + 1:3 * r + 2] * o_slc[:, cols]
             + gate[3 * r + 2:3 * r + 3] * o_win[:, cols])
        o_ref[:, r * dh:(r + 1) * dh] = o.T.astype(BF16)


def _nsa_attention(heads, kc_vc, gates):
    tq = ATT_TQ
    nq = SEQ // tq
    r = NSA_Q_PER_KV
    g_heads = NSA_KV_HEADS
    q_spec = pl.BlockSpec((r, tq, NSA_HEAD_DIM), lambda b, g, i: (g, b * nq + i, 0))

    def kv_spec(first_head):
        return pl.BlockSpec((1, SEQ, NSA_HEAD_DIM), lambda b, g, i: (first_head + g, b, 0))

    first = NSA_HEADS
    specs = [q_spec,
             kv_spec(first + 2 * g_heads), kv_spec(first + 3 * g_heads),
             kv_spec(first + 4 * g_heads), kv_spec(first + 5 * g_heads),
             pl.BlockSpec((1, 1, 1, SEQ // CMP_STRIDE, NSA_HEAD_DIM), lambda b, g, i: (0, g, b, 0, 0)),
             pl.BlockSpec((1, 1, 1, SEQ // CMP_STRIDE, NSA_HEAD_DIM), lambda b, g, i: (1, g, b, 0, 0)),
             pl.BlockSpec((1, 3 * r, tq), lambda b, g, i: (g, 0, b * nq + i))]
    vt_shape = (SEQ // ATT_TK, NSA_HEAD_DIM + ATT_ONES, ATT_TK)
    return pl.pallas_call(
        _nsa_attn_kernel, name="nsa_attn", grid=(BATCH, g_heads, nq),
        in_specs=specs,
        out_specs=pl.BlockSpec((tq, r * NSA_HEAD_DIM), lambda b, g, i: (b * nq + i, g)),
        out_shape=jax.ShapeDtypeStruct((TOKENS, NSA_QD), BF16),
        scratch_shapes=[pltpu.VMEM(vt_shape, BF16), pltpu.VMEM(vt_shape, BF16),
                        pltpu.VMEM((8 * N_SLC, tq), F32),
                        pltpu.VMEM((NSA_HEAD_DIM + ATT_ONES, r * tq), F32),
                        pltpu.VMEM((ATT_CHUNK_TILES * ATT_TK, r * tq), F32)],
        compiler_params=_cparams(("arbitrary", "arbitrary", "arbitrary")),
    )(heads, heads, heads, heads, heads, kc_vc, kc_vc, gates)


def _ssd_chunk_kernel(zx_ref, dtc_ref, dtr_ref, cw_ref, cb_ref, dtb_c_ref, alog_c_ref, dtb_r_ref, alog_r_ref,
                      dskip_ref, ng_ref, echan_ref, o_ref,
                      prev_scr, acum_r_scr, st_scr, fac_scr):
    chunk = pl.program_id(1)
    L = SSD_CHUNK
    W = SSD_GROUP_W
    hpg = SSD_HEADS_PER_GROUP
    n_xt = W // LANES
    x0 = SSD_D_INNER // LANES
    b0 = 2 * SSD_D_INNER // LANES
    c0 = b0 + SSD_GROUPS
    cb0 = SSD_D_INNER // LANES
    cc0 = cb0 + SSD_GROUPS

    @pl.when(chunk == 0)
    def _():
        prev_scr[...] = jnp.zeros(prev_scr.shape, F32)
        st_scr[...] = jnp.zeros(st_scr.shape, F32)

    dt_c = jax.nn.softplus(dtc_ref[...] + dtb_c_ref[...])
    adt_c = dt_c * (-jnp.exp(alog_c_ref[...]))
    dt_r = jax.nn.softplus(dtr_ref[...] + dtb_r_ref[...])
    adt_r = dt_r * (-jnp.exp(alog_r_ref[...]))
    row = lax.broadcasted_iota(I32, (L, L), 0)
    col = lax.broadcasted_iota(I32, (L, L), 1)
    causal = row >= col
    tri = jnp.where(causal, 1.0, 0.0).astype(BF16)
    tri_t = jnp.where(col >= row, 1.0, 0.0).astype(BF16)
    acum_c = _dot_split_rhs(tri, adt_c)
    acum_r_scr[...] = _dot_split_lhs(adt_r, tri_t)
    a_last = acum_c[L - 1:L, :]
    fac = jnp.concatenate([dt_c, jnp.exp(acum_c), jnp.exp(a_last - acum_c),
                           jnp.broadcast_to(jnp.exp(a_last), (SSD_FAC_PAD, LANES))], axis=0)
    fac_hi = fac.astype(BF16)
    fac_scr[0] = fac_hi
    fac_scr[1] = (fac - fac_hi.astype(F32)).astype(BF16)
    lane_w = lax.broadcasted_iota(I32, (L, LANES), 1)
    first_half = lane_w < SSD_HEAD_DIM

    def tiles(ref, first, n):
        return jnp.concatenate([ref[first + q] for q in range(n)], axis=1)

    def group_body(g, carry):
        e_chan = echan_ref[g]
        ex = _dot(fac_scr[0, 0:3 * L], e_chan)
        dt_x, ea_x, sd_x = ex[0:L], ex[L:2 * L], ex[2 * L:3 * L]
        cd_x = (_dot(fac_scr[0, 3 * L:3 * L + SSD_FAC_PAD], e_chan)
                + _dot(fac_scr[1, 3 * L:3 * L + SSD_FAC_PAD], e_chan))[0:1]

        def conv_silu(zx_first, conv_first, n):
            cur = tiles(zx_ref, zx_first, n)
            cur_f = cur.astype(F32)
            tail = tiles(prev_scr, conv_first, n)
            row8 = lax.broadcasted_iota(I32, (8, n * LANES), 0)
            w = tiles(cw_ref, conv_first, n)
            acc = jnp.broadcast_to(tiles(cb_ref, conv_first, n), (L, n * LANES))
            for k in range(SSD_CONV):
                back = SSD_CONV - 1 - k
                if back == 0:
                    xk = cur_f
                else:
                    rolled = pltpu.roll(cur_f, back, 0)
                    head = jnp.where(row8 < back, pltpu.roll(tail, back, 0), rolled[0:8])
                    xk = jnp.concatenate([head, rolled[8:]], axis=0)
                acc = acc + xk * w[k:k + 1, :]
            for q in range(n):
                prev_scr[conv_first + q] = cur_f[L - 8:L, q * LANES:(q + 1) * LANES]
            return jax.nn.silu(acc)

        xs = conv_silu(x0 + n_xt * g, n_xt * g, n_xt)
        bm = conv_silu(b0 + g, cb0 + g, 1)
        cm = conv_silu(c0 + g, cc0 + g, 1)

        xdt = xs * dt_x
        cb = jnp.where(causal, _dot_nt(cm.astype(BF16), bm.astype(BF16)), 0.0)
        y_parts = []
        for pair in range(hpg // 2):
            xd = xdt[:, pair * LANES:(pair + 1) * LANES]
            m_pair = []
            for sub in range(2):
                a_row = jnp.broadcast_to(acum_r_scr[pl.ds(g * hpg + 2 * pair + sub, 1), :], (L, L))
                seg = jnp.minimum(a_row.T - a_row, 0.0)
                m_pair.append((cb * jnp.exp(seg)).astype(BF16))
            x_pair = jnp.concatenate([jnp.where(first_half, xd, 0.0), jnp.where(first_half, 0.0, xd)], axis=0)
            y_parts.append(_dot(jnp.concatenate(m_pair, axis=1), x_pair.astype(BF16)))
        y_diag = jnp.concatenate(y_parts, axis=1)

        st = st_scr[g]
        y_off = _dot(cm.astype(BF16), st.astype(BF16)) * ea_x
        st_scr[g] = st * cd_x + _dot_tn(bm.astype(BF16), (xdt * sd_x).astype(BF16))

        y = y_diag + y_off + xs * tiles(dskip_ref, n_xt * g, n_xt)
        y = y * jax.nn.silu(tiles(zx_ref, n_xt * g, n_xt).astype(F32))
        y = y * lax.rsqrt(jnp.mean(y * y, axis=-1, keepdims=True) + NORM_EPS)
        y = y * tiles(ng_ref, n_xt * g, n_xt)
        for q in range(n_xt):
            o_ref[n_xt * g + q] = y[:, q * LANES:(q + 1) * LANES].astype(BF16)
        return carry

    lax.fori_loop(0, SSD_GROUPS, group_body, 0)


def _ssd_chunks(zx_tiles, dt_small, dt_small_t, conv_w, conv_b, dt_bias, a_log, d_skip, norm_g):
    L = SSD_CHUNK
    nc = SEQ // L
    n_zx = SSD_MAIN // LANES
    n_conv = SSD_CONV_CH // LANES
    n_inner = SSD_D_INNER // LANES
    hpg = SSD_HEADS_PER_GROUP

    def pad_heads(v):
        return jnp.zeros((LANES,), F32).at[:SSD_HEADS].set(v)

    dtb = pad_heads(dt_bias)
    alog = pad_heads(a_log)
    cw = conv_w.reshape(SSD_CONV, n_conv, LANES).transpose(1, 0, 2)
    cb = conv_b.reshape(n_conv, 1, LANES)
    d_chan = jnp.repeat(d_skip, SSD_HEAD_DIM).reshape(n_inner, 1, LANES)
    ng = norm_g.reshape(n_inner, 1, LANES)
    head =jnp.arange(LANES, dtype=I32)[None, :, None]
    grp = jnp.arange(SSD_GROUPS, dtype=I32)[:, None, None]
    e_chan = (head == grp * hpg + jnp.arange(SSD_GROUP_W, dtype=I32)[None, None, :] // SSD_HEAD_DIM).astype(BF16)
    row = lambda b, c: b * nc + c
    const3 = lambda b, c: (0, 0, 0)
    const2 = lambda b, c: (0, 0)
    in_specs = [
        pl.BlockSpec((n_zx, L, LANES), lambda b, c: (0, row(b, c), 0)),
        pl.BlockSpec((L, LANES), lambda b, c: (row(b, c), 0)),
        pl.BlockSpec((LANES, L), lambda b, c: (0, row(b, c))),
        pl.BlockSpec((n_conv, SSD_CONV, LANES), const3),
        pl.BlockSpec((n_conv, 1, LANES), const3),
        pl.BlockSpec((1, LANES), const2), pl.BlockSpec((1, LANES), const2),
        pl.BlockSpec((LANES, 1), const2), pl.BlockSpec((LANES, 1), const2),
        pl.BlockSpec((n_inner, 1, LANES), const3),
        pl.BlockSpec((n_inner, 1, LANES), const3),
        pl.BlockSpec((SSD_GROUPS, LANES, SSD_GROUP_W), const3),
    ]
    return pl.pallas_call(
        _ssd_chunk_kernel, name="ssd_chunks", grid=(BATCH, nc),
        in_specs=in_specs,
        out_specs=pl.BlockSpec((n_inner, L, LANES), lambda b, c: (0, row(b, c), 0)),
        out_shape=jax.ShapeDtypeStruct((n_inner, TOKENS, LANES), BF16),
        scratch_shapes=[pltpu.VMEM((n_conv, 8, LANES), F32),
                        pltpu.VMEM((LANES, L), F32),
                        pltpu.VMEM((SSD_GROUPS, SSD_D_STATE, SSD_GROUP_W), F32),
                        pltpu.VMEM((2, 3 * L + SSD_FAC_PAD, LANES), BF16)],
        compiler_params=_cparams(("arbitrary", "arbitrary")),
    )(zx_tiles, dt_small, dt_small_t, cw, cb, dtb.reshape(1, LANES), alog.reshape(1, LANES),
      dtb.reshape(LANES, 1), alog.reshape(LANES, 1), d_chan, ng, e_chan)


def _gather_rows(src_hbm, idx_ref, base, dst, sem, n_rows):
    def body(b, carry):
        for u in range(GATHER_UNROLL):
            r = b * GATHER_UNROLL + u
            tok = idx_ref[base + r]
            pltpu.make_async_copy(src_hbm.at[pl.ds(tok, 1), :], dst.at[pl.ds(r, 1), :], sem).start()
        return carry

    lax.fori_loop(0, n_rows // GATHER_UNROLL, body, 0)


def _moe_ffn_kernel(te_ref, pos_ref, nact_ref, pad_ref, wp_ref, x_hbm, g_ref, wg_hbm, wu_hbm, wd_hbm, y_ref,
                    tok_ref, buf, sem, wg_f, wu_f, wd_f, wsem, wg_bf, wu_bf, wd_bf, *, layer):
    i = pl.program_id(0)
    n_act = nact_ref[0]
    tm = MOE_TM
    slot = i % 2
    run_start = wp_ref[i] == 1
    w_slot = wp_ref[MOE_TILES + i]
    next_expert = wp_ref[2 * MOE_TILES + i]
    n_rows = pl.multiple_of(wp_ref[3 * MOE_TILES + i], GATHER_UNROLL)

    def gather(tile, s):
        _gather_rows(x_hbm, tok_ref, tile * tm, buf.at[s], sem.at[s], wp_ref[3 * MOE_TILES + tile])

    def weight_copies(e, s):
        return (pltpu.make_async_copy(wg_hbm.at[layer, e], wg_f.at[s], wsem.at[s, 0]),
                pltpu.make_async_copy(wu_hbm.at[layer, e], wu_f.at[s], wsem.at[s, 1]),
                pltpu.make_async_copy(wd_hbm.at[layer, e], wd_f.at[s], wsem.at[s, 2]))

    @pl.when(i == 0)
    def _():
        for c in weight_copies(te_ref[0], 0):
            c.start()
        buf[...] = jnp.zeros(buf.shape, F32)
        def clear(q, carry):
            tok_ref[q] = 0
            return carry

        def clear_padding(e, carry):
            lax.fori_loop(pad_ref[e], pad_ref[MOE_EXPERTS + e], clear, 0)
            return carry

        lax.fori_loop(0, MOE_EXPERTS, clear_padding, 0)
        for k in range(MOE_TOPK):
            def place(t, carry, k=k):
                tok_ref[pos_ref[k * TOKENS + t]] = t
                return carry

            lax.fori_loop(0, TOKENS, place, 0, unroll=GATHER_UNROLL)
        gather(0, 0)

    @pl.when(run_start)
    def _():
        for c in weight_copies(te_ref[i], w_slot):
            c.wait()

        @pl.when(next_expert >= 0)
        def _():
            for c in weight_copies(next_expert, 1 - w_slot):
                c.start()

        wg_bf[...] = wg_f[w_slot].astype(BF16)
        wu_bf[...] = wu_f[w_slot].astype(BF16)
        wd_bf[...] = wd_f[w_slot].astype(BF16)

    @pl.when(i >= n_act)
    def _():
        y_ref[...] = jnp.zeros(y_ref.shape, F32)

    @pl.when(i < n_act)
    def _():
        pltpu.make_async_copy(x_hbm.at[pl.ds(0, n_rows), :], buf.at[slot, pl.ds(0, n_rows), :],
                              sem.at[slot]).wait()

        @pl.when(i + 1 < n_act)
        def _():
            gather(i + 1, 1 - slot)

        h = _rms(buf[slot], g_ref[...]).astype(BF16)
        act = jax.nn.silu(_dot(h, wg_bf[...])) * _dot(h, wu_bf[...])
        y_ref[...] = _dot(act.astype(BF16), wd_bf[...])


def _moe_ffn(x, g, w_gate, w_up, w_down, layer, tile_expert, pos_kmajor, n_active, pad_rows, weight_plan):
    any_spec = pl.BlockSpec(memory_space=pl.ANY)
    grid_spec = pltpu.PrefetchScalarGridSpec(
        num_scalar_prefetch=5, grid=(MOE_TILES,),
        in_specs=[any_spec, pl.BlockSpec((1, D_MODEL), lambda i, *_: (0, 0)), any_spec, any_spec, any_spec],
        out_specs=pl.BlockSpec((MOE_TM, D_MODEL), lambda i, *_: (i, 0)),
        scratch_shapes=[pltpu.SMEM((MOE_ROWS,), I32),
                        pltpu.VMEM((2, MOE_TM, D_MODEL), F32), pltpu.SemaphoreType.DMA((2,)),
                        pltpu.VMEM((2, D_MODEL, MOE_D_FF), F32), pltpu.VMEM((2, D_MODEL, MOE_D_FF), F32),
                        pltpu.VMEM((2, MOE_D_FF, D_MODEL), F32), pltpu.SemaphoreType.DMA((2, 3)),
                        pltpu.VMEM((D_MODEL, MOE_D_FF), BF16), pltpu.VMEM((D_MODEL, MOE_D_FF), BF16),
                        pltpu.VMEM((MOE_D_FF, D_MODEL), BF16)])
    return pl.pallas_call(
        functools.partial(_moe_ffn_kernel, layer=layer), name="moe_ffn", grid_spec=grid_spec,
        out_shape=jax.ShapeDtypeStruct((MOE_ROWS, D_MODEL), F32),
        compiler_params=_cparams(("arbitrary",)),
    )(tile_expert, pos_kmajor, n_active, pad_rows, weight_plan, x, g.reshape(1, D_MODEL), w_gate, w_up, w_down)


def _moe_combine_kernel(pos_ref, x_ref, cw_ref, g_ref, ws_ref, y_hbm, *refs, final_norm):
    if final_norm:
        o_ref, buf, sem = refs
    else:
        o_ref, hn_ref, small_ref, small_t_ref, buf, sem = refs
    i = pl.program_id(0)
    n = pl.num_programs(0)
    tm = CMB_TM
    slot = i % 2

    def issue(tile, s):
        for k in range(MOE_TOPK):
            _gather_rows(y_hbm, pos_ref, (k * (TOKENS // tm) + tile) * tm, buf.at[s, k], sem.at[s], tm)

    @pl.when(i == 0)
    def _():
        issue(0, 0)

    @pl.when(i + 1 < n)
    def _():
        issue(i + 1, 1 - slot)

    for k in range(MOE_TOPK):
        pltpu.make_async_copy(y_hbm.at[pl.ds(0, tm), :], buf.at[slot, k], sem.at[slot]).wait()
    cw = cw_ref[...]
    out = x_ref[...] + cw[:, 0:1] * buf[slot, 0] + cw[:, 1:2] * buf[slot, 1]
    y = _rms(out, g_ref[...])
    if final_norm:
        o_ref[...] = y
    else:
        o_ref[...] = out
        hn_ref[...] = y.astype(BF16)
        small = _dot_x3(y, ws_ref[...])
        small_ref[...] = small
        small_t_ref[...] = small.T


def _moe_combine(x, cw, y_sorted, pos_kmajor, g_norm, w_small, final_norm):
    ws = jnp.zeros((D_MODEL, LANES), F32)
    if not final_norm:
        ws = ws.at[:, :w_small.shape[1]].set(w_small)
    row = pl.BlockSpec((CMB_TM, D_MODEL), lambda i, pos: (i, 0))
    small = pl.BlockSpec((CMB_TM, LANES), lambda i, pos: (i, 0))
    x_shape = jax.ShapeDtypeStruct((TOKENS, D_MODEL), F32)
    if final_norm:
        out_specs, out_shape = row, x_shape
    else:
        out_specs = [row, row, small, pl.BlockSpec((LANES, CMB_TM), lambda i, pos: (0, i))]
        out_shape = [x_shape, jax.ShapeDtypeStruct((TOKENS, D_MODEL), BF16),
                     jax.ShapeDtypeStruct((TOKENS, LANES), F32), jax.ShapeDtypeStruct((LANES, TOKENS), F32)]
    grid_spec = pltpu.PrefetchScalarGridSpec(
        num_scalar_prefetch=1, grid=(TOKENS // CMB_TM,),
        in_specs=[row, small,
                  pl.BlockSpec((1, D_MODEL), lambda i, pos: (0, 0)),
                  pl.BlockSpec((D_MODEL, LANES), lambda i, pos: (0, 0)),
                  pl.BlockSpec(memory_space=pl.ANY)],
        out_specs=out_specs,
        scratch_shapes=[pltpu.VMEM((2, MOE_TOPK, CMB_TM, D_MODEL), F32), pltpu.SemaphoreType.DMA((2,))])
    return pl.pallas_call(
        functools.partial(_moe_combine_kernel, final_norm=final_norm), name="moe_combine", grid_spec=grid_spec,
        out_shape=out_shape,
        compiler_params=_cparams(("arbitrary",)),
    )(pos_kmajor, x, cw, g_norm.reshape(1, D_MODEL), ws, y_sorted)


def _moe_plan(eid):
    e = eid[:, :MOE_TOPK].reshape(-1)
    onehot = (e[:, None] == jnp.arange(MOE_EXPERTS, dtype=I32)[None, :]).astype(I32)
    csum = jnp.cumsum(onehot, axis=0)
    counts = csum[-1]
    padded = ((counts + MOE_TM - 1) // MOE_TM) * MOE_TM
    g_end = jnp.cumsum(padded)
    g_start = g_end - padded
    pos = jnp.sum(onehot * (g_start[None, :] + csum - 1), axis=1)
    n_active = (g_end[-1] // MOE_TM).astype(I32)
    tile_start = jnp.arange(MOE_TILES, dtype=I32) * MOE_TM
    te = jnp.sum((g_end[None, :] <= tile_start[:, None]).astype(I32), axis=1)
    last = jnp.max(jnp.where(counts > 0, jnp.arange(MOE_EXPERTS, dtype=I32), 0))
    tile_expert = jnp.minimum(te, last)
    pos_kmajor = pos.reshape(TOKENS, MOE_TOPK).T.reshape(-1)
    pad_rows = jnp.concatenate([g_start + counts, g_end])
    experts = jnp.arange(MOE_EXPERTS, dtype=I32)
    run_start = jnp.concatenate([jnp.ones((1,), I32), (tile_expert[1:] != tile_expert[:-1]).astype(I32)])
    w_slot = (jnp.cumsum(run_start) - 1) % 2
    later = (experts[None, :] > experts[:, None]) & (counts > 0)[None, :]
    next_nonempty = jnp.min(jnp.where(later, experts[None, :], MOE_EXPERTS), axis=1)
    next_nonempty = jnp.where(next_nonempty == MOE_EXPERTS, -1, next_nonempty)
    real_rows = jnp.clip((g_start + counts)[tile_expert] - tile_start, 0, MOE_TM)
    real_rows = jnp.where(tile_start < g_end[-1], real_rows, 0)
    gather_rows = jnp.minimum(-(-real_rows // GATHER_UNROLL) * GATHER_UNROLL, MOE_TM)
    weight_plan = jnp.concatenate([run_start, w_slot, next_nonempty[tile_expert], gather_rows]).astype(I32)
    return tile_expert, n_active.reshape(1), pos_kmajor, pad_rows, weight_plan


def _hier_moe_add(x, eid, cw, ln_g, w_gate, w_up, w_down, layer, g_norm, w_small, final_norm):
    tile_expert, n_active, pos_kmajor, pad_rows, weight_plan = _moe_plan(eid)
    y_sorted = _moe_ffn(x, ln_g, w_gate, w_up, w_down, layer, tile_expert, pos_kmajor, n_active, pad_rows,
                        weight_plan)
    return _moe_combine(x, cw, y_sorted, pos_kmajor, g_norm, w_small, final_norm)


def _rope_tables():
    pos = jnp.arange(SEQ, dtype=F32)
    inv = 1.0 / (ROPE_THETA ** (jnp.arange(0, NSA_HEAD_DIM, 2, dtype=F32) / NSA_HEAD_DIM))
    ang = pos[:, None] * inv[None, :]
    cos, sin = jnp.cos(ang), jnp.sin(ang)
    return jnp.concatenate([cos, cos], axis=1), jnp.concatenate([-sin, sin], axis=1)


def _nsa_mixer(hn, g_lin_t, w_in, cmp_pe, cmp_w1, cmp_w2):
    w_in_t = w_in.T
    cos_full, sin_signed = _rope_tables()
    heads = _nsa_inproj(hn, w_in_t, cos_full, sin_signed)
    first_c = NSA_HEADS
    kc_vc = _compress(heads, first_c, cmp_pe, cmp_w1, cmp_w2)
    gates_t = g_lin_t[:NSA_GATES].reshape(NSA_KV_HEADS, 3 * NSA_Q_PER_KV, TOKENS)
    return _nsa_attention(heads, kc_vc, gates_t)


def _ssd_mixer(hn, dt_small, dt_small_t, w_in, conv_w, conv_b, dt_bias, a_log, d_skip, norm_g):
    zx_tiles = _matmul_tiles(hn, w_in.T, SSD_MAIN, 1024)
    return _ssd_chunks(zx_tiles, dt_small, dt_small_t, conv_w, conv_b, dt_bias, a_log, d_skip, norm_g)


def kernel(x, ln_mix, ln_ffn, ln_final, nsa_w_in, nsa_cmp_pe, nsa_cmp_w1, nsa_cmp_w2, nsa_w_out,
           ssd_w_in, ssd_conv_w, ssd_conv_b, ssd_dt_bias, ssd_a_log, ssd_d, ssd_norm, ssd_w_out,
           moe_w_group, moe_b_group, moe_w_expert, moe_b_expert, moe_w_gate, moe_w_up, moe_w_down):
    def small_weight(i):
        if i % N_MIXERS == 0:
            return nsa_w_in[i // N_MIXERS].T[NSA_MAIN:].T
        return ssd_w_in[i // N_MIXERS].T[SSD_MAIN:].T

    h = x.reshape(TOKENS, D_MODEL)
    hn, small, small_t = _norm_small(h, ln_mix[0], small_weight(0))
    for i in range(DEPTH):
        j = i // N_MIXERS
        last = i == DEPTH - 1
        if i % N_MIXERS == 0:
            mix = _nsa_mixer(hn, small_t, nsa_w_in[j], nsa_cmp_pe[j], nsa_cmp_w1[j], nsa_cmp_w2[j])
            w_out = nsa_w_out[j]
        else:
            mix = _ssd_mixer(hn, small, small_t, ssd_w_in[j], ssd_conv_w[j], ssd_conv_b[j], ssd_dt_bias[j],
                             ssd_a_log[j], ssd_d[j], ssd_norm[j])
            w_out = ssd_w_out[j]
        h, eid, cw = _outproj_route(mix, w_out, h, ln_ffn[i], moe_w_group[i], moe_b_group[i],
                                    moe_w_expert[i], moe_b_expert[i])
        out = _hier_moe_add(h, eid, cw, ln_ffn[i], moe_w_gate, moe_w_up, moe_w_down, i,
                            ln_final if last else ln_mix[i + 1], None if last else small_weight(i + 1), last)
        if last:
            h = out
        else:
            h, hn, small, small_t = out
    return h.reshape(BATCH, SEQ, D_MODEL)
```

```python
import functools

import jax
import jax.numpy as jnp
from jax import lax
from jax.experimental import pallas as pl
from jax.experimental.pallas import tpu as pltpu

F32 = jnp.float32
BF16 = jnp.bfloat16
I32 = jnp.int32

D_MODEL = 2048
BATCH = 4
SEQ = 2048
TOKENS = BATCH * SEQ
DEPTH = 2
N_MIXERS = 2
NORM_EPS = 1e-6
NEG_INF = -1e30
LOG2_E = 1.4426950408889634
ROPE_THETA = 10000.0

NSA_HEADS = 16
NSA_KV_HEADS = 4
NSA_HEAD_DIM = D_MODEL // NSA_HEADS
NSA_Q_PER_KV = NSA_HEADS // NSA_KV_HEADS
CMP_BLOCK = 32
CMP_STRIDE = 16
CMP_HIDDEN = 256
N_CMP = (SEQ - CMP_BLOCK) // CMP_STRIDE + 1
SLC_BLOCK = 64
SLC_TOPK = 16
SLC_LOCAL = 2
SLC_FORCE = 1e4
N_SLC = SEQ // SLC_BLOCK
WINDOW = 512
NSA_QD = NSA_HEADS * NSA_HEAD_DIM
NSA_KVD = NSA_KV_HEADS * NSA_HEAD_DIM
NSA_MAIN = NSA_QD + 6 * NSA_KVD
NSA_GATES = 3 * NSA_HEADS

SSD_D_INNER = 2 * D_MODEL
SSD_HEAD_DIM = 64
SSD_HEADS = SSD_D_INNER // SSD_HEAD_DIM
SSD_GROUPS = 8
SSD_HEADS_PER_GROUP = SSD_HEADS // SSD_GROUPS
SSD_D_STATE = 128
SSD_CONV = 4
SSD_CHUNK = 128
SSD_GROUP_W = SSD_D_INNER // SSD_GROUPS
SSD_BC = SSD_GROUPS * SSD_D_STATE
SSD_CONV_CH = SSD_D_INNER + 2 * SSD_BC
SSD_MAIN = SSD_D_INNER + SSD_CONV_CH
SSD_FAC_PAD = 16

MOE_GROUPS = 4
MOE_EPG = 8
MOE_EXPERTS = MOE_GROUPS * MOE_EPG
MOE_TOPK = 2
MOE_D_FF = 512

LANES = 128
VMEM_LIMIT = 56 * 1024 * 1024

NORM_TM = 512
MM_TM = 1024
OUT_TM = 512
ATT_TQ = 256
ATT_TK = 256
ATT_CHUNK_TILES = 2
ATT_ONES = 16
MOE_TM = 256
MOE_TILES = (TOKENS * MOE_TOPK) // MOE_TM + MOE_EXPERTS
MOE_ROWS = MOE_TILES * MOE_TM
CMB_TM = 512
GATHER_UNROLL = 8


def _cparams(sem):
    return pltpu.CompilerParams(dimension_semantics=sem, vmem_limit_bytes=VMEM_LIMIT)


def _split3(x):
    hi = x.astype(BF16)
    r1 = x - hi.astype(F32)
    mid = r1.astype(BF16)
    lo = (r1 - mid.astype(F32)).astype(BF16)
    return hi, mid, lo


def _dot(a, b):
    return jnp.dot(a, b, preferred_element_type=F32)


def _dot_nt(a, b):
    return lax.dot_general(a, b, (((1,), (1,)), ((), ())), preferred_element_type=F32)


def _dot_tn(a, b):
    return lax.dot_general(a, b, (((0,), (0,)), ((), ())), preferred_element_type=F32)


def _dot_split_lhs(x, m_bf16):
    hi, mid, lo = _split3(x)
    return _dot(hi, m_bf16) + _dot(mid, m_bf16) + _dot(lo, m_bf16)


def _dot_split_rhs(m_bf16, x):
    hi, mid, lo = _split3(x)
    return _dot(m_bf16, hi) + _dot(m_bf16, mid) + _dot(m_bf16, lo)


def _dot_x3(a, w):
    a_hi = a.astype(BF16)
    a_lo = (a - a_hi.astype(F32)).astype(BF16)
    w_hi = w.astype(BF16)
    w_lo = (w - w_hi.astype(F32)).astype(BF16)
    n = w.shape[1]
    both = _dot(a_hi, jnp.concatenate([w_hi, w_lo], axis=1))
    return both[:, :n] + both[:, n:] + _dot(a_lo, w_hi)


def _rms(x, g):
    y = x * lax.rsqrt(jnp.mean(x * x, axis=-1, keepdims=True) + NORM_EPS)
    return y * g


def _norm_small_kernel(x_ref, g_ref, ws_ref, hn_ref, small_ref, small_t_ref):
    y = _rms(x_ref[...], g_ref[...])
    hn_ref[...] = y.astype(BF16)
    small = _dot_x3(y, ws_ref[...])
    small_ref[...] = small
    small_t_ref[...] = small.T


def _norm_small(x, g, w_small):
    n = w_small.shape[1]
    ws = jnp.zeros((D_MODEL, LANES), F32).at[:, :n].set(w_small)
    return pl.pallas_call(
        _norm_small_kernel, name="norm_small", grid=(TOKENS // NORM_TM,),
        in_specs=[pl.BlockSpec((NORM_TM, D_MODEL), lambda i: (i, 0)),
                  pl.BlockSpec((1, D_MODEL), lambda i: (0, 0)),
                  pl.BlockSpec((D_MODEL, LANES), lambda i: (0, 0))],
        out_specs=[pl.BlockSpec((NORM_TM, D_MODEL), lambda i: (i, 0)),
                   pl.BlockSpec((NORM_TM, LANES), lambda i: (i, 0)),
                   pl.BlockSpec((LANES, NORM_TM), lambda i: (0, i))],
        out_shape=[jax.ShapeDtypeStruct((TOKENS, D_MODEL), BF16),
                   jax.ShapeDtypeStruct((TOKENS, LANES), F32),
                   jax.ShapeDtypeStruct((LANES, TOKENS), F32)],
        compiler_params=_cparams(("parallel",)),
    )(x, g.reshape(1, D_MODEL), ws)


def _route(x, g, ws, bias):
    y = _rms(x, g)
    logits = _dot_x3(y, ws) + bias
    lane = lax.broadcasted_iota(I32, logits.shape, 1)
    big = jnp.int32(LANES)
    neg = -jnp.inf
    gl = jnp.where(lane < MOE_GROUPS, logits, neg)
    gmax = jnp.max(gl, axis=-1, keepdims=True)
    gsum = jnp.sum(jnp.exp(gl - gmax), axis=-1, keepdims=True)
    g_w = 1.0 / gsum
    g_sel = jnp.min(jnp.where(gl == gmax, lane, big), axis=-1, keepdims=True)
    lo = MOE_GROUPS + g_sel * MOE_EPG
    el = jnp.where((lane >= lo) & (lane < lo + MOE_EPG), logits, neg)
    v1 = jnp.max(el, axis=-1, keepdims=True)
    i1 = jnp.min(jnp.where(el == v1, lane, big), axis=-1, keepdims=True)
    el2 = jnp.where(lane == i1, neg, el)
    v2 = jnp.max(el2, axis=-1, keepdims=True)
    i2 = jnp.min(jnp.where(el2 == v2, lane, big), axis=-1, keepdims=True)
    e2 = jnp.exp(v2 - v1)
    den = 1.0 + e2
    w1 = (1.0 / den) * g_w
    w2 = (e2 / den) * g_w
    eid = jnp.where(lane == 0, i1 - MOE_GROUPS, jnp.where(lane == 1, i2 - MOE_GROUPS, 0))
    cw = jnp.where(lane == 0, w1, jnp.where(lane == 1, w2, 0.0))
    return eid, cw


def _inproj_heads_kernel(a_ref, w_ref, cos_ref, sin_ref, o_ref, wbf_ref):
    j = pl.program_id(0)

    @pl.when(pl.program_id(1) == 0)
    def _():
        wbf_ref[...] = w_ref[...].astype(BF16)

    acc = _dot_nt(a_ref[...], wbf_ref[...])
    q_tiles = NSA_QD // acc.shape[1]
    c = cos_ref[...]
    s = sin_ref[...]

    def head(h):
        return acc[:, h * NSA_HEAD_DIM:(h + 1) * NSA_HEAD_DIM]

    def rotary(xh):
        return (xh * c + pltpu.roll(xh, NSA_HEAD_DIM // 2, 1) * s).astype(BF16)

    for h in range(NSA_KV_HEADS):
        o_ref[h] = rotary(head(h))

    @pl.when(j < q_tiles)
    def _():
        for h in range(NSA_KV_HEADS, 2 * NSA_KV_HEADS):
            o_ref[h] = rotary(head(h))

    @pl.when(j >= q_tiles)
    def _():
        for h in range(NSA_KV_HEADS, 2 * NSA_KV_HEADS):
            o_ref[h] = head(h).astype(BF16)


def _nsa_inproj(hn, w_in_t, cos_full, sin_signed):
    tn = 2 * NSA_KVD
    n_heads_tile = tn // NSA_HEAD_DIM
    s_tiles = SEQ // MM_TM
    return pl.pallas_call(
        _inproj_heads_kernel, name="nsa_inproj", grid=(NSA_MAIN // tn, TOKENS // MM_TM),
        in_specs=[pl.BlockSpec((MM_TM, D_MODEL), lambda j, i: (i, 0)),
                  pl.BlockSpec((tn, D_MODEL), lambda j, i: (j, 0)),
                  pl.BlockSpec((MM_TM, NSA_HEAD_DIM), lambda j, i: (i % s_tiles, 0)),
                  pl.BlockSpec((MM_TM, NSA_HEAD_DIM), lambda j, i: (i % s_tiles, 0))],
        out_specs=pl.BlockSpec((n_heads_tile, MM_TM, NSA_HEAD_DIM), lambda j, i: (j, i, 0)),
        out_shape=jax.ShapeDtypeStruct((NSA_MAIN // NSA_HEAD_DIM, TOKENS, NSA_HEAD_DIM), BF16),
        scratch_shapes=[pltpu.VMEM((tn, D_MODEL), BF16)],
        compiler_params=_cparams(("arbitrary", "arbitrary")),
    )(hn, w_in_t, cos_full, sin_signed)


def _matmul_tiles_kernel(a_ref, w_ref, o_ref, wbf_ref):
    @pl.when(pl.program_id(1) == 0)
    def _():
        wbf_ref[...] = w_ref[...].astype(BF16)

    acc = _dot_nt(a_ref[...], wbf_ref[...])
    for t in range(acc.shape[1] // LANES):
        o_ref[t] = acc[:, t * LANES:(t + 1) * LANES].astype(o_ref.dtype)


def _matmul_tiles(a, w_t, n_cols, tn):
    k = a.shape[1]
    return pl.pallas_call(
        _matmul_tiles_kernel, name="matmul_tiles", grid=(n_cols // tn, TOKENS // MM_TM),
        in_specs=[pl.BlockSpec((MM_TM, k), lambda j, i: (i, 0)),
                  pl.BlockSpec((tn, k), lambda j, i: (j, 0))],
        out_specs=pl.BlockSpec((tn // LANES, MM_TM, LANES), lambda j, i: (j, i, 0)),
        out_shape=jax.ShapeDtypeStruct((n_cols // LANES, TOKENS, LANES), BF16),
        scratch_shapes=[pltpu.VMEM((tn, k), BF16)],
        compiler_params=_cparams(("arbitrary", "arbitrary")),
    )(a, w_t)


def _outproj_route_kernel(a_ref, w_ref, r_ref, g_ref, ws_ref, b_ref, o_ref, eid_ref, cw_ref):
    if len(a_ref.shape) == 3:
        a = jnp.concatenate([a_ref[t] for t in range(a_ref.shape[0])], axis=1)
    else:
        a = a_ref[...]
    x_new = r_ref[...] + _dot(a, w_ref[...])
    o_ref[...] = x_new
    eid_ref[...], cw_ref[...] = _route(x_new, g_ref[...], ws_ref[...], b_ref[...])


def _outproj_route(a, w, resid, ln_ffn_g, w_group, b_group, w_expert, b_expert):
    tiled = a.ndim == 3
    k, n = w.shape
    n_r = MOE_GROUPS + MOE_EXPERTS
    ws = jnp.zeros((D_MODEL, LANES), F32).at[:, :n_r].set(jnp.concatenate([w_group, w_expert], axis=1))
    bs = jnp.zeros((1, LANES), F32).at[0, :n_r].set(jnp.concatenate([b_group, b_expert]))
    a_spec = (pl.BlockSpec((k // LANES, OUT_TM, LANES), lambda i: (0, i, 0)) if tiled
              else pl.BlockSpec((OUT_TM, k), lambda i: (i, 0)))
    small = pl.BlockSpec((OUT_TM, LANES), lambda i: (i, 0))
    return pl.pallas_call(
        _outproj_route_kernel, name="outproj_route", grid=(TOKENS // OUT_TM,),
        in_specs=[a_spec,
                  pl.BlockSpec((k, n), lambda i: (0, 0), pipeline_mode=pl.Buffered(1)),
                  pl.BlockSpec((OUT_TM, n), lambda i: (i, 0)),
                  pl.BlockSpec((1, D_MODEL), lambda i: (0, 0)),
                  pl.BlockSpec((D_MODEL, LANES), lambda i: (0, 0)),
                  pl.BlockSpec((1, LANES), lambda i: (0, 0))],
        out_specs=[pl.BlockSpec((OUT_TM, n), lambda i: (i, 0)), small, small],
        out_shape=[jax.ShapeDtypeStruct((TOKENS, n), F32),
                   jax.ShapeDtypeStruct((TOKENS, LANES), I32),
                   jax.ShapeDtypeStruct((TOKENS, LANES), F32)],
        compiler_params=_cparams(("parallel",)),
    )(a, w.astype(BF16), resid, ln_ffn_g.reshape(1, D_MODEL), ws, bs)


def _compress_kernel(x_ref, pe_ref, w1_ref, w2_ref, o_ref, tok_scr):
    half = CMP_STRIDE * NSA_HEAD_DIM
    n_chunk = SEQ // CMP_STRIDE
    tok_scr[...] = x_ref[0].astype(F32)
    x = jnp.concatenate([tok_scr[pl.ds(p, n_chunk, stride=CMP_STRIDE), :] for p in range(CMP_STRIDE)],
                        axis=1).astype(BF16)
    w1 = w1_ref[0].astype(BF16)
    top = _dot(x, w1[:half])
    bot = _dot(x, w1[half:])
    pe = jnp.broadcast_to(pe_ref[0], (8, 2 * half)).astype(BF16)
    pe_bias = _dot(pe, w1)[0:1]
    hid = top + pltpu.roll(bot, bot.shape[0] - 1, 0) + pe_bias
    act = jax.nn.gelu(hid)
    o_ref[0, 0, 0] = _dot(act.astype(BF16), w2_ref[0].astype(BF16))


def _compress(heads, first_head, pe, w1, w2):
    n_chunk = SEQ // CMP_STRIDE
    feat = CMP_STRIDE * NSA_HEAD_DIM
    return pl.pallas_call(
        _compress_kernel, name="nsa_compress", grid=(2, NSA_KV_HEADS, BATCH),
        in_specs=[pl.BlockSpec((1, SEQ, NSA_HEAD_DIM), lambda a, g, b: (first_head + a * NSA_KV_HEADS + g, b, 0)),
                  pl.BlockSpec((1, 1, 2 * feat), lambda a, g, b: (a, 0, 0)),
                  pl.BlockSpec((1, 2 * feat, CMP_HIDDEN), lambda a, g, b: (a, 0, 0)),
                  pl.BlockSpec((1, CMP_HIDDEN, NSA_HEAD_DIM), lambda a, g, b: (a, 0, 0))],
        out_specs=pl.BlockSpec((1, 1, 1, n_chunk, NSA_HEAD_DIM), lambda a, g, b: (a, g, b, 0, 0)),
        out_shape=jax.ShapeDtypeStruct((2, NSA_KV_HEADS, BATCH, n_chunk, NSA_HEAD_DIM), F32),
        scratch_shapes=[pltpu.VMEM((SEQ, NSA_HEAD_DIM), F32)],
        compiler_params=_cparams(("parallel", "parallel", "parallel")),
    )(heads, pe.reshape(2, 1, 2 * feat), w1, w2)


def _nsa_attn_kernel(q_ref, ks_ref, vs_ref, kw_ref, vw_ref, kc_ref, vc_ref, gate_ref, o_ref,
                     vst_scr, vwt_scr, sel_scr, acc_scr, sc_scr):
    qi = pl.program_id(2)
    tq = ATT_TQ
    tk = ATT_TK
    dh = NSA_HEAD_DIM
    r_heads = NSA_Q_PER_KV
    n_kt = SEQ // tk
    scale = dh ** -0.5 * LOG2_E

    @pl.when(qi == 0)
    def _():
        ones = jnp.ones((ATT_ONES, tk), BF16)
        for kt in range(n_kt):
            rows = slice(kt * tk, (kt + 1) * tk)
            vst_scr[kt, 0:dh, :] = vs_ref[0, rows, :].astype(F32).T.astype(BF16)
            vwt_scr[kt, 0:dh, :] = vw_ref[0, rows, :].astype(F32).T.astype(BF16)
            vst_scr[kt, dh:dh + ATT_ONES, :] = ones
            vwt_scr[kt, dh:dh + ATT_ONES, :] = ones

    q_s = jnp.concatenate([q_ref[r].astype(F32).T for r in range(r_heads)], axis=1) * scale
    q_hi = q_s.astype(BF16)
    q_t = jnp.concatenate([q_hi, (q_s - q_hi.astype(F32)).astype(BF16)], axis=0)

    def qk(k):
        return _dot(jnp.concatenate([k, k], axis=1), q_t)
    n_cp = SEQ // CMP_STRIDE
    sub = lax.broadcasted_iota(I32, (n_cp, tq), 0)
    t_pos = qi * tq + lax.broadcasted_iota(I32, (n_cp, tq), 1)

    def tile4(a):
        return jnp.concatenate([a] * r_heads, axis=1)

    kc = kc_ref[0, 0, 0].astype(BF16)
    vc = vc_ref[0, 0, 0].astype(BF16)
    ok_c = jnp.where(sub * CMP_STRIDE + CMP_BLOCK - 1 <= t_pos, jnp.where(sub < N_CMP, 1.0, 0.0), 0.0)
    ok_c4 = tile4(ok_c)
    s_c = qk(kc) + (ok_c4 - 1.0) * (-NEG_INF)
    e_c = jnp.exp2(s_c - jnp.max(s_c, axis=0, keepdims=True))
    p_c = (e_c / jnp.sum(e_c, axis=0, keepdims=True)) * ok_c4
    o_cmp = _dot_tn(vc, p_c.astype(BF16))
    p_sum = p_c[:, 0:tq]
    for r in range(1, r_heads):
        p_sum = p_sum + p_c[:, r * tq:(r + 1) * tq]

    blk_row = lax.broadcasted_iota(I32, (LANES, LANES), 0)
    cmp_col = lax.broadcasted_iota(I32, (LANES, LANES), 1)
    s_start = blk_row * SLC_BLOCK
    c_start = cmp_col * CMP_STRIDE
    ov_t = jnp.maximum(jnp.minimum(c_start + CMP_BLOCK, s_start + SLC_BLOCK)
                       - jnp.maximum(c_start, s_start), 0).astype(F32) / CMP_BLOCK
    ov_t = jnp.where(blk_row < N_SLC, ov_t, 0.0).astype(BF16)
    imp = _dot_split_rhs(ov_t, p_sum)[0:N_SLC]
    j_blk = lax.broadcasted_iota(I32, (N_SLC, tq), 0)
    dist = (qi * tq + lax.broadcasted_iota(I32, (N_SLC, tq), 1)) // SLC_BLOCK - j_blk
    imp = jnp.where(j_blk == 0, SLC_FORCE, jnp.where(dist < 0, imp, jnp.where(dist < SLC_LOCAL, SLC_FORCE, imp)))
    imp = jnp.where(dist >= 0, imp, -jnp.inf)
    cnt = jnp.zeros((N_SLC, tq), I32)
    for k in range(N_SLC):
        row_k = imp[k:k + 1, :]
        tie = jnp.where(j_blk > k, 1, 0)
        cnt = cnt + jnp.where(row_k > imp, 1, jnp.where(row_k == imp, tie, 0))
    sel = jnp.where(cnt < min(SLC_TOPK, N_SLC), 1.0, 0.0)
    for j in range(N_SLC):
        sel_scr[8 * j:8 * j + 8, :] = jnp.broadcast_to(sel[j:j + 1, :], (8, tq))

    def scores(k, ok):
        return qk(k) + tile4((ok - 1.0) * (-NEG_INF))

    def weighted_values(vt_scr, kt0, pr, n_tiles):
        out = None
        for u in range(n_tiles):
            term = _dot(vt_scr[kt0 + u], pr[u * tk:(u + 1) * tk].astype(BF16))
            out = term if out is None else out + term
        return out

    n_ct = ATT_CHUNK_TILES
    ck = n_ct * tk
    sub_c = lax.broadcasted_iota(I32, (ck, tq), 0)
    t_pos_c = qi * tq + lax.broadcasted_iota(I32, (ck, tq), 1)
    blocks_per_chunk = ck // SLC_BLOCK
    acc_scr[...] = jnp.zeros(acc_scr.shape, F32)

    def chunk_scores(c):
        c = jnp.minimum(c, SEQ // ck - 1)
        start = pl.multiple_of(c * ck, ck)
        k = ks_ref[0, pl.ds(start, ck), :]
        rows8 = sel_scr[pl.ds(pl.multiple_of(c * (8 * blocks_per_chunk), 8 * blocks_per_chunk),
                              8 * blocks_per_chunk), :]
        picked = jnp.concatenate(
            [rows8[8 * u:8 * u + 8] for u in range(blocks_per_chunk) for _ in range(SLC_BLOCK // 8)], axis=0)
        return scores(k, jnp.where(start + sub_c <= t_pos_c, picked, 0.0))

    sc_scr[...] = chunk_scores(0)

    def slc_body(c, m_old):
        sc = sc_scr[...]
        sc_next = chunk_scores(c + 1)
        m_new = jnp.maximum(m_old, jnp.max(sc, axis=0, keepdims=True))
        alpha = jnp.exp2(m_old - m_new)
        pr = jnp.exp2(sc - m_new)
        acc_scr[...] = alpha * acc_scr[...] + weighted_values(vst_scr, c * n_ct, pr, n_ct)
        sc_scr[...] = sc_next
        return m_new

    lax.fori_loop(0, ((qi + 1) * tq + ck - 1) // ck, slc_body, jnp.full((1, r_heads * tq), NEG_INF, F32))
    acc = acc_scr[...]
    o_slc = acc[0:dh] / acc[dh:dh + 1]

    n_wt = (WINDOW + tq) // tk
    kt0 = jnp.maximum(qi * (tq // tk) - WINDOW // tk, 0)
    w_start = pl.multiple_of(kt0 * tk, tk)
    key_w = w_start + lax.broadcasted_iota(I32, (n_wt * tk, tq), 0)
    t_pos_w = qi * tq + lax.broadcasted_iota(I32, (n_wt * tk, tq), 1)
    ok_w = jnp.where(key_w <= t_pos_w, jnp.where(key_w > t_pos_w - WINDOW, 1.0, 0.0), 0.0)
    k_w = kw_ref[0, pl.ds(w_start, n_wt * tk), :]
    k_w2 = jnp.concatenate([k_w, k_w], axis=1)
    bias_w = (ok_w - 1.0) * (-NEG_INF)
    bias_w2 = jnp.concatenate([bias_w] * (r_heads // 2), axis=1)
    o_win_parts = []
    for hp in range(2):
        lanes = slice(hp * (r_heads // 2) * tq, (hp + 1) * (r_heads // 2) * tq)
        sc_w = _dot(k_w2, q_t[:, lanes]) + bias_w2
        pr_w = jnp.exp2(sc_w - jnp.max(sc_w, axis=0, keepdims=True))
        acc_w = weighted_values(vwt_scr, kt0, pr_w, n_wt)
        o_win_parts.append(acc_w[0:dh] / acc_w[dh:dh + 1])
    o_win = jnp.concatenate(o_win_parts, axis=1)

    gate = jax.nn.sigmoid(gate_ref[0])
    for r in range(r_heads):
        cols = slice(r * tq, (r + 1) * tq)
        o = (gate[3 * r:3 * r + 1] * o_cmp[:, cols] + gate[3 * r + 1:3 * r + 2] * o_slc[:, cols]
             + gate[3 * r + 2:3 * r + 3] * o_win[:, cols])
        o_ref[:, r * dh:(r + 1) * dh] = o.T.astype(BF16)


def _nsa_attention(heads, kc_vc, gates):
    tq = ATT_TQ
    nq = SEQ // tq
    r = NSA_Q_PER_KV
    g_heads = NSA_KV_HEADS
    q_spec = pl.BlockSpec((r, tq, NSA_HEAD_DIM), lambda b, g, i: (g, b * nq + i, 0))

    def kv_spec(first_head):
        return pl.BlockSpec((1, SEQ, NSA_HEAD_DIM), lambda b, g, i: (first_head + g, b, 0))

    first = NSA_HEADS
    specs = [q_spec,
             kv_spec(first + 2 * g_heads), kv_spec(first + 3 * g_heads),
             kv_spec(first + 4 * g_heads), kv_spec(first + 5 * g_heads),
             pl.BlockSpec((1, 1, 1, SEQ // CMP_STRIDE, NSA_HEAD_DIM), lambda b, g, i: (0, g, b, 0, 0)),
             pl.BlockSpec((1, 1, 1, SEQ // CMP_STRIDE, NSA_HEAD_DIM), lambda b, g, i: (1, g, b, 0, 0)),
             pl.BlockSpec((1, 3 * r, tq), lambda b, g, i: (g, 0, b * nq + i))]
    vt_shape = (SEQ // ATT_TK, NSA_HEAD_DIM + ATT_ONES, ATT_TK)
    return pl.pallas_call(
        _nsa_attn_kernel, name="nsa_attn", grid=(BATCH, g_heads, nq),
        in_specs=specs,
        out_specs=pl.BlockSpec((tq, r * NSA_HEAD_DIM), lambda b, g, i: (b * nq + i, g)),
        out_shape=jax.ShapeDtypeStruct((TOKENS, NSA_QD), BF16),
        scratch_shapes=[pltpu.VMEM(vt_shape, BF16), pltpu.VMEM(vt_shape, BF16),
                        pltpu.VMEM((8 * N_SLC, tq), F32),
                        pltpu.VMEM((NSA_HEAD_DIM + ATT_ONES, r * tq), F32),
                        pltpu.VMEM((ATT_CHUNK_TILES * ATT_TK, r * tq), F32)],
        compiler_params=_cparams(("arbitrary", "arbitrary", "arbitrary")),
    )(heads, heads, heads, heads, heads, kc_vc, kc_vc, gates)


def _ssd_chunk_kernel(zx_ref, dtc_ref, dtr_ref, cw_ref, cb_ref, dtb_c_ref, alog_c_ref, dtb_r_ref, alog_r_ref,
                      dskip_ref, ng_ref, echan_ref, o_ref,
                      prev_scr, acum_r_scr, st_scr, fac_scr):
    chunk = pl.program_id(1)
    L = SSD_CHUNK
    W = SSD_GROUP_W
    hpg = SSD_HEADS_PER_GROUP
    n_xt = W // LANES
    x0 = SSD_D_INNER // LANES
    b0 = 2 * SSD_D_INNER // LANES
    c0 = b0 + SSD_GROUPS
    cb0 = SSD_D_INNER // LANES
    cc0 = cb0 + SSD_GROUPS

    @pl.when(chunk == 0)
    def _():
        prev_scr[...] = jnp.zeros(prev_scr.shape, F32)
        st_scr[...] = jnp.zeros(st_scr.shape, F32)

    dt_c = jax.nn.softplus(dtc_ref[...] + dtb_c_ref[...])
    adt_c = dt_c * (-jnp.exp(alog_c_ref[...]))
    dt_r = jax.nn.softplus(dtr_ref[...] + dtb_r_ref[...])
    adt_r = dt_r * (-jnp.exp(alog_r_ref[...]))
    row = lax.broadcasted_iota(I32, (L, L), 0)
    col = lax.broadcasted_iota(I32, (L, L), 1)
    causal = row >= col
    tri = jnp.where(causal, 1.0, 0.0).astype(BF16)
    tri_t = jnp.where(col >= row, 1.0, 0.0).astype(BF16)
    acum_c = _dot_split_rhs(tri, adt_c)
    acum_r_scr[...] = _dot_split_lhs(adt_r, tri_t)
    a_last = acum_c[L - 1:L, :]
    fac = jnp.concatenate([dt_c, jnp.exp(acum_c), jnp.exp(a_last - acum_c),
                           jnp.broadcast_to(jnp.exp(a_last), (SSD_FAC_PAD, LANES))], axis=0)
    fac_hi = fac.astype(BF16)
    fac_scr[0] = fac_hi
    fac_scr[1] = (fac - fac_hi.astype(F32)).astype(BF16)
    lane_w = lax.broadcasted_iota(I32, (L, LANES), 1)
    first_half = lane_w < SSD_HEAD_DIM

    def tiles(ref, first, n):
        return jnp.concatenate([ref[first + q] for q in range(n)], axis=1)

    def group_body(g, carry):
        e_chan = echan_ref[g]
        ex = _dot(fac_scr[0, 0:3 * L], e_chan)
        dt_x, ea_x, sd_x = ex[0:L], ex[L:2 * L], ex[2 * L:3 * L]
        cd_x = (_dot(fac_scr[0, 3 * L:3 * L + SSD_FAC_PAD], e_chan)
                + _dot(fac_scr[1, 3 * L:3 * L + SSD_FAC_PAD], e_chan))[0:1]

        def conv_silu(zx_first, conv_first, n):
            cur = tiles(zx_ref, zx_first, n)
            cur_f = cur.astype(F32)
            tail = tiles(prev_scr, conv_first, n)
            row8 = lax.broadcasted_iota(I32, (8, n * LANES), 0)
            w = tiles(cw_ref, conv_first, n)
            acc = jnp.broadcast_to(tiles(cb_ref, conv_first, n), (L, n * LANES))
            for k in range(SSD_CONV):
                back = SSD_CONV - 1 - k
                if back == 0:
                    xk = cur_f
                else:
                    rolled = pltpu.roll(cur_f, back, 0)
                    head = jnp.where(row8 < back, pltpu.roll(tail, back, 0), rolled[0:8])
                    xk = jnp.concatenate([head, rolled[8:]], axis=0)
                acc = acc + xk * w[k:k + 1, :]
            for q in range(n):
                prev_scr[conv_first + q] = cur_f[L - 8:L, q * LANES:(q + 1) * LANES]
            return jax.nn.silu(acc)

        xs = conv_silu(x0 + n_xt * g, n_xt * g, n_xt)
        bm = conv_silu(b0 + g, cb0 + g, 1)
        cm = conv_silu(c0 + g, cc0 + g, 1)

        xdt = xs * dt_x
        cb = jnp.where(causal, _dot_nt(cm.astype(BF16), bm.astype(BF16)), 0.0)
        y_parts = []
        for pair in range(hpg // 2):
            xd = xdt[:, pair * LANES:(pair + 1) * LANES]
            m_pair = []
            for sub in range(2):
                a_row = jnp.broadcast_to(acum_r_scr[pl.ds(g * hpg + 2 * pair + sub, 1), :], (L, L))
                seg = jnp.minimum(a_row.T - a_row, 0.0)
                m_pair.append((cb * jnp.exp(seg)).astype(BF16))
            x_pair = jnp.concatenate([jnp.where(first_half, xd, 0.0), jnp.where(first_half, 0.0, xd)], axis=0)
            y_parts.append(_dot(jnp.concatenate(m_pair, axis=1), x_pair.astype(BF16)))
        y_diag = jnp.concatenate(y_parts, axis=1)

        st = st_scr[g]
        y_off = _dot(cm.astype(BF16), st.astype(BF16)) * ea_x
        st_scr[g] = st * cd_x + _dot_tn(bm.astype(BF16), (xdt * sd_x).astype(BF16))

        y = y_diag + y_off + xs * tiles(dskip_ref, n_xt * g, n_xt)
        y = y * jax.nn.silu(tiles(zx_ref, n_xt * g, n_xt).astype(F32))
        y = y * lax.rsqrt(jnp.mean(y * y, axis=-1, keepdims=True) + NORM_EPS)
        y = y * tiles(ng_ref, n_xt * g, n_xt)
        for q in range(n_xt):
            o_ref[n_xt * g + q] = y[:, q * LANES:(q + 1) * LANES].astype(BF16)
        return carry

    lax.fori_loop(0, SSD_GROUPS, group_body, 0)


def _ssd_chunks(zx_tiles, dt_small, dt_small_t, conv_w, conv_b, dt_bias, a_log, d_skip, norm_g):
    L = SSD_CHUNK
    nc = SEQ // L
    n_zx = SSD_MAIN // LANES
    n_conv = SSD_CONV_CH // LANES
    n_inner = SSD_D_INNER // LANES
    hpg = SSD_HEADS_PER_GROUP

    def pad_heads(v):
        return jnp.zeros((LANES,), F32).at[:SSD_HEADS].set(v)

    dtb = pad_heads(dt_bias)
    alog = pad_heads(a_log)
    cw = conv_w.reshape(SSD_CONV, n_conv, LANES).transpose(1, 0, 2)
    cb = conv_b.reshape(n_conv, 1, LANES)
    d_chan = jnp.repeat(d_skip, SSD_HEAD_DIM).reshape(n_inner, 1, LANES)
    ng = norm_g.reshape(n_inner, 1, LANES)
    head =jnp.arange(LANES, dtype=I32)[None, :, None]
    grp = jnp.arange(SSD_GROUPS, dtype=I32)[:, None, None]
    e_chan = (head == grp * hpg + jnp.arange(SSD_GROUP_W, dtype=I32)[None, None, :] // SSD_HEAD_DIM).astype(BF16)
    row = lambda b, c: b * nc + c
    const3 = lambda b, c: (0, 0, 0)
    const2 = lambda b, c: (0, 0)
    in_specs = [
        pl.BlockSpec((n_zx, L, LANES), lambda b, c: (0, row(b, c), 0)),
        pl.BlockSpec((L, LANES), lambda b, c: (row(b, c), 0)),
        pl.BlockSpec((LANES, L), lambda b, c: (0, row(b, c))),
        pl.BlockSpec((n_conv, SSD_CONV, LANES), const3),
        pl.BlockSpec((n_conv, 1, LANES), const3),
        pl.BlockSpec((1, LANES), const2), pl.BlockSpec((1, LANES), const2),
        pl.BlockSpec((LANES, 1), const2), pl.BlockSpec((LANES, 1), const2),
        pl.BlockSpec((n_inner, 1, LANES), const3),
        pl.BlockSpec((n_inner, 1, LANES), const3),
        pl.BlockSpec((SSD_GROUPS, LANES, SSD_GROUP_W), const3),
    ]
    return pl.pallas_call(
        _ssd_chunk_kernel, name="ssd_chunks", grid=(BATCH, nc),
        in_specs=in_specs,
        out_specs=pl.BlockSpec((n_inner, L, LANES), lambda b, c: (0, row(b, c), 0)),
        out_shape=jax.ShapeDtypeStruct((n_inner, TOKENS, LANES), BF16),
        scratch_shapes=[pltpu.VMEM((n_conv, 8, LANES), F32),
                        pltpu.VMEM((LANES, L), F32),
                        pltpu.VMEM((SSD_GROUPS, SSD_D_STATE, SSD_GROUP_W), F32),
                        pltpu.VMEM((2, 3 * L + SSD_FAC_PAD, LANES), BF16)],
        compiler_params=_cparams(("arbitrary", "arbitrary")),
    )(zx_tiles, dt_small, dt_small_t, cw, cb, dtb.reshape(1, LANES), alog.reshape(1, LANES),
      dtb.reshape(LANES, 1), alog.reshape(LANES, 1), d_chan, ng, e_chan)


def _gather_rows(src_hbm, idx_ref, base, dst, sem, n_rows):
    def body(b, carry):
        for u in range(GATHER_UNROLL):
            r = b * GATHER_UNROLL + u
            tok = idx_ref[base + r]
            pltpu.make_async_copy(src_hbm.at[pl.ds(tok, 1), :], dst.at[pl.ds(r, 1), :], sem).start()
        return carry

    lax.fori_loop(0, n_rows // GATHER_UNROLL, body, 0)


def _moe_ffn_kernel(te_ref, pos_ref, nact_ref, pad_ref, wp_ref, x_hbm, g_ref, wg_hbm, wu_hbm, wd_hbm, y_ref,
                    tok_ref, buf, sem, wg_f, wu_f, wd_f, wsem, wg_bf, wu_bf, wd_bf, *, layer):
    i = pl.program_id(0)
    n_act = nact_ref[0]
    tm = MOE_TM
    slot = i % 2
    run_start = wp_ref[i] == 1
    w_slot = wp_ref[MOE_TILES + i]
    next_expert = wp_ref[2 * MOE_TILES + i]
    n_rows = pl.multiple_of(wp_ref[3 * MOE_TILES + i], GATHER_UNROLL)

    def gather(tile, s):
        _gather_rows(x_hbm, tok_ref, tile * tm, buf.at[s], sem.at[s], wp_ref[3 * MOE_TILES + tile])

    def weight_copies(e, s):
        return (pltpu.make_async_copy(wg_hbm.at[layer, e], wg_f.at[s], wsem.at[s, 0]),
                pltpu.make_async_copy(wu_hbm.at[layer, e], wu_f.at[s], wsem.at[s, 1]),
                pltpu.make_async_copy(wd_hbm.at[layer, e], wd_f.at[s], wsem.at[s, 2]))

    @pl.when(i == 0)
    def _():
        for c in weight_copies(te_ref[0], 0):
            c.start()
        buf[...] = jnp.zeros(buf.shape, F32)
        def clear(q, carry):
            tok_ref[q] = 0
            return carry

        def clear_padding(e, carry):
            lax.fori_loop(pad_ref[e], pad_ref[MOE_EXPERTS + e], clear, 0)
            return carry

        lax.fori_loop(0, MOE_EXPERTS, clear_padding, 0)
        for k in range(MOE_TOPK):
            def place(t, carry, k=k):
                tok_ref[pos_ref[k * TOKENS + t]] = t
                return carry

            lax.fori_loop(0, TOKENS, place, 0, unroll=GATHER_UNROLL)
        gather(0, 0)

    @pl.when(run_start)
    def _():
        for c in weight_copies(te_ref[i], w_slot):
            c.wait()

        @pl.when(next_expert >= 0)
        def _():
            for c in weight_copies(next_expert, 1 - w_slot):
                c.start()

        wg_bf[...] = wg_f[w_slot].astype(BF16)
        wu_bf[...] = wu_f[w_slot].astype(BF16)
        wd_bf[...] = wd_f[w_slot].astype(BF16)

    @pl.when(i >= n_act)
    def _():
        y_ref[...] = jnp.zeros(y_ref.shape, F32)

    @pl.when(i < n_act)
    def _():
        pltpu.make_async_copy(x_hbm.at[pl.ds(0, n_rows), :], buf.at[slot, pl.ds(0, n_rows), :],
                              sem.at[slot]).wait()

        @pl.when(i + 1 < n_act)
        def _():
            gather(i + 1, 1 - slot)

        h = _rms(buf[slot], g_ref[...]).astype(BF16)
        act = jax.nn.silu(_dot(h, wg_bf[...])) * _dot(h, wu_bf[...])
        y_ref[...] = _dot(act.astype(BF16), wd_bf[...])


def _moe_ffn(x, g, w_gate, w_up, w_down, layer, tile_expert, pos_kmajor, n_active, pad_rows, weight_plan):
    any_spec = pl.BlockSpec(memory_space=pl.ANY)
    grid_spec = pltpu.PrefetchScalarGridSpec(
        num_scalar_prefetch=5, grid=(MOE_TILES,),
        in_specs=[any_spec, pl.BlockSpec((1, D_MODEL), lambda i, *_: (0, 0)), any_spec, any_spec, any_spec],
        out_specs=pl.BlockSpec((MOE_TM, D_MODEL), lambda i, *_: (i, 0)),
        scratch_shapes=[pltpu.SMEM((MOE_ROWS,), I32),
                        pltpu.VMEM((2, MOE_TM, D_MODEL), F32), pltpu.SemaphoreType.DMA((2,)),
                        pltpu.VMEM((2, D_MODEL, MOE_D_FF), F32), pltpu.VMEM((2, D_MODEL, MOE_D_FF), F32),
                        pltpu.VMEM((2, MOE_D_FF, D_MODEL), F32), pltpu.SemaphoreType.DMA((2, 3)),
                        pltpu.VMEM((D_MODEL, MOE_D_FF), BF16), pltpu.VMEM((D_MODEL, MOE_D_FF), BF16),
                        pltpu.VMEM((MOE_D_FF, D_MODEL), BF16)])
    return pl.pallas_call(
        functools.partial(_moe_ffn_kernel, layer=layer), name="moe_ffn", grid_spec=grid_spec,
        out_shape=jax.ShapeDtypeStruct((MOE_ROWS, D_MODEL), F32),
        compiler_params=_cparams(("arbitrary",)),
    )(tile_expert, pos_kmajor, n_active, pad_rows, weight_plan, x, g.reshape(1, D_MODEL), w_gate, w_up, w_down)


def _moe_combine_kernel(pos_ref, x_ref, cw_ref, g_ref, ws_ref, y_hbm, *refs, final_norm):
    if final_norm:
        o_ref, buf, sem = refs
    else:
        o_ref, hn_ref, small_ref, small_t_ref, buf, sem = refs
    i = pl.program_id(0)
    n = pl.num_programs(0)
    tm = CMB_TM
    slot = i % 2

    def issue(tile, s):
        for k in range(MOE_TOPK):
            _gather_rows(y_hbm, pos_ref, (k * (TOKENS // tm) + tile) * tm, buf.at[s, k], sem.at[s], tm)

    @pl.when(i == 0)
    def _():
        issue(0, 0)

    @pl.when(i + 1 < n)
    def _():
        issue(i + 1, 1 - slot)

    for k in range(MOE_TOPK):
        pltpu.make_async_copy(y_hbm.at[pl.ds(0, tm), :], buf.at[slot, k], sem.at[slot]).wait()
    cw = cw_ref[...]
    out = x_ref[...] + cw[:, 0:1] * buf[slot, 0] + cw[:, 1:2] * buf[slot, 1]
    y = _rms(out, g_ref[...])
    if final_norm:
        o_ref[...] = y
    else:
        o_ref[...] = out
        hn_ref[...] = y.astype(BF16)
        small = _dot_x3(y, ws_ref[...])
        small_ref[...] = small
        small_t_ref[...] = small.T


def _moe_combine(x, cw, y_sorted, pos_kmajor, g_norm, w_small, final_norm):
    ws = jnp.zeros((D_MODEL, LANES), F32)
    if not final_norm:
        ws = ws.at[:, :w_small.shape[1]].set(w_small)
    row = pl.BlockSpec((CMB_TM, D_MODEL), lambda i, pos: (i, 0))
    small = pl.BlockSpec((CMB_TM, LANES), lambda i, pos: (i, 0))
    x_shape = jax.ShapeDtypeStruct((TOKENS, D_MODEL), F32)
    if final_norm:
        out_specs, out_shape = row, x_shape
    else:
        out_specs = [row, row, small, pl.BlockSpec((LANES, CMB_TM), lambda i, pos: (0, i))]
        out_shape = [x_shape, jax.ShapeDtypeStruct((TOKENS, D_MODEL), BF16),
                     jax.ShapeDtypeStruct((TOKENS, LANES), F32), jax.ShapeDtypeStruct((LANES, TOKENS), F32)]
    grid_spec = pltpu.PrefetchScalarGridSpec(
        num_scalar_prefetch=1, grid=(TOKENS // CMB_TM,),
        in_specs=[row, small,
                  pl.BlockSpec((1, D_MODEL), lambda i, pos: (0, 0)),
                  pl.BlockSpec((D_MODEL, LANES), lambda i, pos: (0, 0)),
                  pl.BlockSpec(memory_space=pl.ANY)],
        out_specs=out_specs,
        scratch_shapes=[pltpu.VMEM((2, MOE_TOPK, CMB_TM, D_MODEL), F32), pltpu.SemaphoreType.DMA((2,))])
    return pl.pallas_call(
        functools.partial(_moe_combine_kernel, final_norm=final_norm), name="moe_combine", grid_spec=grid_spec,
        out_shape=out_shape,
        compiler_params=_cparams(("arbitrary",)),
    )(pos_kmajor, x, cw, g_norm.reshape(1, D_MODEL), ws, y_sorted)


def _moe_plan(eid):
    e = eid[:, :MOE_TOPK].reshape(-1)
    onehot = (e[:, None] == jnp.arange(MOE_EXPERTS, dtype=I32)[None, :]).astype(I32)
    csum = jnp.cumsum(onehot, axis=0)
    counts = csum[-1]
    padded = ((counts + MOE_TM - 1) // MOE_TM) * MOE_TM
    g_end = jnp.cumsum(padded)
    g_start = g_end - padded
    pos = jnp.sum(onehot * (g_start[None, :] + csum - 1), axis=1)
    n_active = (g_end[-1] // MOE_TM).astype(I32)
    tile_start = jnp.arange(MOE_TILES, dtype=I32) * MOE_TM
    te = jnp.sum((g_end[None, :] <= tile_start[:, None]).astype(I32), axis=1)
    last = jnp.max(jnp.where(counts > 0, jnp.arange(MOE_EXPERTS, dtype=I32), 0))
    tile_expert = jnp.minimum(te, last)
    pos_kmajor = pos.reshape(TOKENS, MOE_TOPK).T.reshape(-1)
    pad_rows = jnp.concatenate([g_start + counts, g_end])
    experts = jnp.arange(MOE_EXPERTS, dtype=I32)
    run_start = jnp.concatenate([jnp.ones((1,), I32), (tile_expert[1:] != tile_expert[:-1]).astype(I32)])
    w_slot = (jnp.cumsum(run_start) - 1) % 2
    later = (experts[None, :] > experts[:, None]) & (counts > 0)[None, :]
    next_nonempty = jnp.min(jnp.where(later, experts[None, :], MOE_EXPERTS), axis=1)
    next_nonempty = jnp.where(next_nonempty == MOE_EXPERTS, -1, next_nonempty)
    real_rows = jnp.clip((g_start + counts)[tile_expert] - tile_start, 0, MOE_TM)
    real_rows = jnp.where(tile_start < g_end[-1], real_rows, 0)
    gather_rows = jnp.minimum(-(-real_rows // GATHER_UNROLL) * GATHER_UNROLL, MOE_TM)
    weight_plan = jnp.concatenate([run_start, w_slot, next_nonempty[tile_expert], gather_rows]).astype(I32)
    return tile_expert, n_active.reshape(1), pos_kmajor, pad_rows, weight_plan


def _hier_moe_add(x, eid, cw, ln_g, w_gate, w_up, w_down, layer, g_norm, w_small, final_norm):
    tile_expert, n_active, pos_kmajor, pad_rows, weight_plan = _moe_plan(eid)
    y_sorted = _moe_ffn(x, ln_g, w_gate, w_up, w_down, layer, tile_expert, pos_kmajor, n_active, pad_rows,
                        weight_plan)
    return _moe_combine(x, cw, y_sorted, pos_kmajor, g_norm, w_small, final_norm)


def _rope_tables():
    pos = jnp.arange(SEQ, dtype=F32)
    inv = 1.0 / (ROPE_THETA ** (jnp.arange(0, NSA_HEAD_DIM, 2, dtype=F32) / NSA_HEAD_DIM))
    ang = pos[:, None] * inv[None, :]
    cos, sin = jnp.cos(ang), jnp.sin(ang)
    return jnp.concatenate([cos, cos], axis=1), jnp.concatenate([-sin, sin], axis=1)


def _nsa_mixer(hn, g_lin_t, w_in, cmp_pe, cmp_w1, cmp_w2):
    w_in_t = w_in.T
    cos_full, sin_signed = _rope_tables()
    heads = _nsa_inproj(hn, w_in_t, cos_full, sin_signed)
    first_c = NSA_HEADS
    kc_vc = _compress(heads, first_c, cmp_pe, cmp_w1, cmp_w2)
    gates_t = g_lin_t[:NSA_GATES].reshape(NSA_KV_HEADS, 3 * NSA_Q_PER_KV, TOKENS)
    return _nsa_attention(heads, kc_vc, gates_t)


def _ssd_mixer(hn, dt_small, dt_small_t, w_in, conv_w, conv_b, dt_bias, a_log, d_skip, norm_g):
    zx_tiles = _matmul_tiles(hn, w_in.T, SSD_MAIN, 1024)
    return _ssd_chunks(zx_tiles, dt_small, dt_small_t, conv_w, conv_b, dt_bias, a_log, d_skip, norm_g)


def kernel(x, ln_mix, ln_ffn, ln_final, nsa_w_in, nsa_cmp_pe, nsa_cmp_w1, nsa_cmp_w2, nsa_w_out,
           ssd_w_in, ssd_conv_w, ssd_conv_b, ssd_dt_bias, ssd_a_log, ssd_d, ssd_norm, ssd_w_out,
           moe_w_group, moe_b_group, moe_w_expert, moe_b_expert, moe_w_gate, moe_w_up, moe_w_down):
    def small_weight(i):
        if i % N_MIXERS == 0:
            return nsa_w_in[i // N_MIXERS].T[NSA_MAIN:].T
        return ssd_w_in[i // N_MIXERS].T[SSD_MAIN:].T

    h = x.reshape(TOKENS, D_MODEL)
    hn, small, small_t = _norm_small(h, ln_mix[0], small_weight(0))
    for i in range(DEPTH):
        j = i // N_MIXERS
        last = i == DEPTH - 1
        if i % N_MIXERS == 0:
            mix = _nsa_mixer(hn, small_t, nsa_w_in[j], nsa_cmp_pe[j], nsa_cmp_w1[j], nsa_cmp_w2[j])
            w_out = nsa_w_out[j]
        else:
            mix = _ssd_mixer(hn, small, small_t, ssd_w_in[j], ssd_conv_w[j], ssd_conv_b[j], ssd_dt_bias[j],
                             ssd_a_log[j], ssd_d[j], ssd_norm[j])
            w_out = ssd_w_out[j]
        h, eid, cw = _outproj_route(mix, w_out, h, ln_ffn[i], moe_w_group[i], moe_b_group[i],
                                    moe_w_expert[i], moe_b_expert[i])
        out = _hier_moe_add(h, eid, cw, ln_ffn[i], moe_w_gate, moe_w_up, moe_w_down, i,
                            ln_final if last else ln_mix[i + 1], None if last else small_weight(i + 1), last)
        if last:
            h = out
        else:
            h, hn, small, small_t = out
    return h.reshape(BATCH, SEQ, D_MODEL)
```
